```python
import math
import jax, jax.numpy as jnp
from jax import lax
import numpy as np

D_MODEL = 2048
BATCH = 8
SEQ = 8192
DEPTH = 1

N_META = 16
CONV_K = 4
GDN_HEADS = 8
GDN_DK = 128
GDN_DV = 128
GDN_CHUNK = 64
GLA_HEADS = 4
GLA_DK = 128
GLA_DV = 256
GLA_CHUNK = 16
GLA_GATE_RANK = 16
GLA_GATE_NORMALIZER = 16.0
GDN_QK = GDN_HEADS * GDN_DK
GDN_V = GDN_HEADS * GDN_DV
GLA_QK = GLA_HEADS * GLA_DK
GLA_V = GLA_HEADS * GLA_DV
MIX_WIDTH = GDN_V + GLA_V
D_FF = -(-8 * D_MODEL // (3 * 256)) * 256
IN_SPLITS = (2 * GDN_QK + GDN_V, GDN_V, GDN_HEADS, GDN_HEADS, GLA_QK, GLA_QK, GLA_V, GLA_V, GLA_GATE_RANK)
D_IN = sum(IN_SPLITS)
IN_OFFSETS = tuple(int(i) for i in np.cumsum(IN_SPLITS)[:-1])
NORM_EPS = 1e-6

kernel_name = "hybrid_gdn_gla_meta_block"


def rms_norm(x, w):
    xf = x.astype(jnp.float32)
    y = xf * lax.rsqrt(jnp.mean(xf * xf, axis=-1, keepdims=True) + NORM_EPS)
    return (y * w.astype(jnp.float32)).astype(x.dtype)


def l2_normalize(x):
    xf = x.astype(jnp.float32)
    return (xf * lax.rsqrt(jnp.sum(xf * xf, axis=-1, keepdims=True) + NORM_EPS)).astype(x.dtype)


def causal_short_conv(x, w):
    L = x.shape[1]
    xp = jnp.pad(x, ((0, 0), (CONV_K - 1, 0), (0, 0)))
    y = xp[:, 0:L] * w[0]
    for i in range(1, CONV_K):
        y = y + xp[:, i:i + L] * w[i]
    return jax.nn.silu(y)


def to_chunks(t, chunk, pad):
    t = jnp.pad(t, ((0, 0), (pad, 0), (0, 0), (0, 0)))
    b, lp, h, d = t.shape
    return t.reshape(b, lp // chunk, chunk, h, d).transpose(0, 3, 1, 2, 4)


def from_chunks(o, pad):
    b, h, n, c, d = o.shape
    return o.transpose(0, 2, 3, 1, 4).reshape(b, n * c, h, d)[:, pad:]


def gated_delta_rule(q, k, v, beta, g):
    out_dtype = v.dtype
    f32 = jnp.float32
    C = GDN_CHUNK
    pad = (-N_META) % C
    q, k, v = (to_chunks(t.astype(f32), C, pad) for t in (q, k, v))
    beta, g = (to_chunks(t.astype(f32)[..., None], C, pad)[..., 0] for t in (beta, g))
    gc = jnp.cumsum(g, axis=-1)
    causal = jnp.tril(jnp.ones((C, C), bool))
    strict = jnp.tril(jnp.ones((C, C), bool), -1)
    decay = jnp.exp(jnp.where(causal, gc[..., :, None] - gc[..., None, :], -jnp.inf))
    kb = k * beta[..., None]
    a_low = jnp.where(strict, jnp.einsum('bhncd,bhnsd->bhncs', kb, k) * decay, 0.0)
    t_mat = a_low + jnp.eye(C, dtype=f32)
    u = lax.linalg.triangular_solve(t_mat, v * beta[..., None], left_side=True, lower=True, unit_diagonal=True)
    w = lax.linalg.triangular_solve(t_mat, kb * jnp.exp(gc)[..., None], left_side=True, lower=True, unit_diagonal=True)
    qk = jnp.einsum('bhncd,bhnsd->bhncs', q, k) * decay
    q_dec = q * jnp.exp(gc)[..., None]
    k_dec = k * jnp.exp(gc[..., -1:] - gc)[..., None]
    g_last = jnp.exp(gc[..., -1])

    def step(S, inp):
        qd, kd, u_c, w_c, qk_c, gl = inp
        v_new = u_c - jnp.einsum('bhcd,bhde->bhce', w_c, S)
        o = jnp.einsum('bhcd,bhde->bhce', qd, S) + jnp.einsum('bhcs,bhse->bhce', qk_c, v_new)
        S = S * gl[..., None, None] + jnp.einsum('bhcd,bhce->bhde', kd, v_new)
        return S, o

    xs = tuple(jnp.moveaxis(t, 2, 0) for t in (q_dec, k_dec, u, w, qk, g_last))
    b, h = q.shape[0], q.shape[1]
    S0 = jnp.zeros((b, h, GDN_DK, GDN_DV), f32)
    _, o = lax.scan(step, S0, xs)
    return from_chunks(jnp.moveaxis(o, 0, 2), pad).astype(out_dtype)


def gla_chunked(q, k, v, log_a):
    out_dtype = v.dtype
    f32 = jnp.float32
    C = GLA_CHUNK
    pad = (-N_META) % C
    q, k, v, log_a = (to_chunks(t.astype(f32), C, pad) for t in (q, k, v, log_a))
    bcum = jnp.cumsum(log_a, axis=-2)
    causal = jnp.tril(jnp.ones((C, C), bool))

    def step(S, inp):
        q_c, k_c, v_c, b_c = inp
        diff = jnp.where(causal[..., None], b_c[..., :, None, :] - b_c[..., None, :, :], -jnp.inf)
        scores = jnp.einsum('bhid,bhjd,bhijd->bhij', q_c, k_c, jnp.exp(diff))
        o = jnp.einsum('bhid,bhde->bhie', q_c * jnp.exp(b_c), S) + jnp.einsum('bhij,bhje->bhie', scores, v_c)
        b_last = b_c[..., -1, :]
        S = S * jnp.exp(b_last)[..., None] + jnp.einsum(
            'bhjd,bhje->bhde', k_c * jnp.exp(b_last[..., None, :] - b_c), v_c)
        return S, o

    xs = tuple(jnp.moveaxis(t, 2, 0) for t in (q, k, v, bcum))
    b, h = q.shape[0], q.shape[1]
    S0 = jnp.zeros((b, h, GLA_DK, GLA_DV), f32)
    _, o = lax.scan(step, S0, xs)
    return from_chunks(jnp.moveaxis(o, 0, 2), pad).astype(out_dtype)


def _fwd_setup_inputs(seed: int = 0) -> dict:
    key = jax.random.key(seed)
    ks = jax.random.split(key, 20)
    f32 = jnp.float32

    def nrm(k, shape, scale):
        return jax.random.normal(k, shape, f32) * scale

    def gain(k, shape):
        return 1.0 + 0.01 * jax.random.normal(k, shape, f32)

    dt = jnp.exp(jax.random.uniform(ks[7], (DEPTH, GDN_HEADS), f32, math.log(1e-3), math.log(1e-1)))
    return {
        "x": nrm(ks[0], (BATCH, SEQ, D_MODEL), 1.0),
        "meta_tokens": nrm(ks[1], (N_META, D_MODEL), 1.0),
        "attn_norm_w": gain(ks[2], (DEPTH, D_MODEL)),
        "w_in": nrm(ks[3], (DEPTH, D_MODEL, D_IN), D_MODEL ** -0.5),
        "gdn_conv_w": nrm(ks[4], (DEPTH, CONV_K, 2 * GDN_QK + GDN_V), CONV_K ** -0.5),
        "gdn_a_log": jnp.log(jax.random.uniform(ks[5], (DEPTH, GDN_HEADS), f32, 1.0, 16.0)),
        "gdn_dt_bias": dt + jnp.log(-jnp.expm1(-dt)),
        "gdn_norm_w": gain(ks[6], (DEPTH, GDN_DV)),
        "gla_gate_w2": nrm(ks[8], (DEPTH, GLA_GATE_RANK, GLA_QK), GLA_GATE_RANK ** -0.5),
        "gla_gate_b": nrm(ks[9], (DEPTH, GLA_QK), 0.01),
        "gla_norm_w": gain(ks[10], (DEPTH, GLA_DV)),
        "w_out": nrm(ks[11], (DEPTH, MIX_WIDTH, D_MODEL), MIX_WIDTH ** -0.5),
        "ffn_norm_w": gain(ks[12], (DEPTH, D_MODEL)),
        "w_gate": nrm(ks[13], (DEPTH, D_MODEL, D_FF), D_MODEL ** -0.5),
        "w_up": nrm(ks[14], (DEPTH, D_MODEL, D_FF), D_MODEL ** -0.5),
        "w_down": nrm(ks[15], (DEPTH, D_FF, D_MODEL), D_FF ** -0.5),
        "final_norm_w": gain(ks[16], (D_MODEL,)),
    }


def _fwd_reference(x, meta_tokens, attn_norm_w, w_in, gdn_conv_w, gdn_a_log, gdn_dt_bias, gdn_norm_w,
              gla_gate_w2, gla_gate_b, gla_norm_w, w_out, ffn_norm_w, w_gate, w_up, w_down, final_norm_w):
    f32 = jnp.float32
    bsz = x.shape[0]
    meta = jnp.broadcast_to(meta_tokens.astype(x.dtype)[None], (bsz, N_META, D_MODEL))
    h = jnp.concatenate([meta, x], axis=1)
    L = h.shape[1]
    for layer in range(DEPTH):
        n = rms_norm(h, attn_norm_w[layer])
        proj = n @ w_in[layer]
        (gdn_qkv, gdn_z, gdn_a, gdn_b, gla_q, gla_k, gla_v, gla_r, gla_lr) = jnp.split(proj, IN_OFFSETS, axis=-1)

        qkv = causal_short_conv(gdn_qkv, gdn_conv_w[layer])
        q, k, v = jnp.split(qkv, (GDN_QK, 2 * GDN_QK), axis=-1)
        q = l2_normalize(q.reshape(bsz, L, GDN_HEADS, GDN_DK)) * (GDN_DK ** -0.5)
        k = l2_normalize(k.reshape(bsz, L, GDN_HEADS, GDN_DK))
        v = v.reshape(bsz, L, GDN_HEADS, GDN_DV)
        beta = jax.nn.sigmoid(gdn_b.astype(f32))
        g = -jnp.exp(gdn_a_log[layer].astype(f32)) * jax.nn.softplus(
            gdn_a.astype(f32) + gdn_dt_bias[layer].astype(f32))
        o_gdn = gated_delta_rule(q, k, v, beta, g)
        o_gdn = rms_norm(o_gdn, gdn_norm_w[layer]) * jax.nn.silu(gdn_z.reshape(bsz, L, GDN_HEADS, GDN_DV))

        gq = gla_q.reshape(bsz, L, GLA_HEADS, GLA_DK) * (GLA_DK ** -0.5)
        gk = gla_k.reshape(bsz, L, GLA_HEADS, GLA_DK)
        gv = gla_v.reshape(bsz, L, GLA_HEADS, GLA_DV)
        log_a = jax.nn.log_sigmoid((gla_lr @ gla_gate_w2[layer] + gla_gate_b[layer]).astype(f32)) / GLA_GATE_NORMALIZER
        o_gla = gla_chunked(gq, gk, gv, log_a.reshape(bsz, L, GLA_HEADS, GLA_DK))
        o_gla = rms_norm(o_gla, gla_norm_w[layer]) * jax.nn.silu(gla_r.reshape(bsz, L, GLA_HEADS, GLA_DV))

        mixed = jnp.concatenate([o_gdn.reshape(bsz, L, GDN_V), o_gla.reshape(bsz, L, GLA_V)], axis=-1)
        h = h + mixed @ w_out[layer]

        n = rms_norm(h, ffn_norm_w[layer])
        h = h + (jax.nn.silu(n @ w_gate[layer]) * (n @ w_up[layer])) @ w_down[layer]
    return rms_norm(h[:, N_META:], final_norm_w)


import jax as _jax
import jax.numpy as _jnp

TWIN_FORMAT = 'train_step'
FWD_PARAMS = ['x', 'meta_tokens', 'attn_norm_w', 'w_in', 'gdn_conv_w', 'gdn_a_log', 'gdn_dt_bias', 'gdn_norm_w', 'gla_gate_w2', 'gla_gate_b', 'gla_norm_w', 'w_out', 'ffn_norm_w', 'w_gate', 'w_up', 'w_down', 'final_norm_w']
TWIN_WEIGHTS = ['meta_tokens', 'attn_norm_w', 'w_in', 'gdn_conv_w', 'gdn_a_log', 'gdn_dt_bias', 'gdn_norm_w', 'gla_gate_w2', 'gla_gate_b', 'gla_norm_w', 'w_out', 'ffn_norm_w', 'w_gate', 'w_up', 'w_down', 'final_norm_w']
TWIN_DIFF_INPUT = 'x'
TWIN_INPUTS = ['x', 'meta_tokens', 'attn_norm_w', 'w_in', 'gdn_conv_w', 'gdn_a_log', 'gdn_dt_bias', 'gdn_norm_w', 'gla_gate_w2', 'gla_gate_b', 'gla_norm_w', 'w_out', 'ffn_norm_w', 'w_gate', 'w_up', 'w_down', 'final_norm_w', 'loss_target', 'm_meta_tokens', 'm_attn_norm_w', 'm_w_in', 'm_gdn_conv_w', 'm_gdn_a_log', 'm_gdn_dt_bias', 'm_gdn_norm_w', 'm_gla_gate_w2', 'm_gla_gate_b', 'm_gla_norm_w', 'm_w_out', 'm_ffn_norm_w', 'm_w_gate', 'm_w_up', 'm_w_down', 'm_final_norm_w', 'v_meta_tokens', 'v_attn_norm_w', 'v_w_in', 'v_gdn_conv_w', 'v_gdn_a_log', 'v_gdn_dt_bias', 'v_gdn_norm_w', 'v_gla_gate_w2', 'v_gla_gate_b', 'v_gla_norm_w', 'v_w_out', 'v_ffn_norm_w', 'v_w_gate', 'v_w_up', 'v_w_down', 'v_final_norm_w']
TWIN_OUTPUTS = ['loss', 'grad_x', 'grad_meta_tokens', 'grad_attn_norm_w', 'grad_w_in', 'grad_gdn_conv_w', 'grad_gdn_a_log', 'grad_gdn_dt_bias', 'grad_gdn_norm_w', 'grad_gla_gate_w2', 'grad_gla_gate_b', 'grad_gla_norm_w', 'grad_w_out', 'grad_ffn_norm_w', 'grad_w_gate', 'grad_w_up', 'grad_w_down', 'grad_final_norm_w', 'delta_meta_tokens', 'delta_attn_norm_w', 'delta_w_in', 'delta_gdn_conv_w', 'delta_gdn_a_log', 'delta_gdn_dt_bias', 'delta_gdn_norm_w', 'delta_gla_gate_w2', 'delta_gla_gate_b', 'delta_gla_norm_w', 'delta_w_out', 'delta_ffn_norm_w', 'delta_w_gate', 'delta_w_up', 'delta_w_down', 'delta_final_norm_w', 'new_m_meta_tokens', 'new_m_attn_norm_w', 'new_m_w_in', 'new_m_gdn_conv_w', 'new_m_gdn_a_log', 'new_m_gdn_dt_bias', 'new_m_gdn_norm_w', 'new_m_gla_gate_w2', 'new_m_gla_gate_b', 'new_m_gla_norm_w', 'new_m_w_out', 'new_m_ffn_norm_w', 'new_m_w_gate', 'new_m_w_up', 'new_m_w_down', 'new_m_final_norm_w', 'new_v_meta_tokens', 'new_v_attn_norm_w', 'new_v_w_in', 'new_v_gdn_conv_w', 'new_v_gdn_a_log', 'new_v_gdn_dt_bias', 'new_v_gdn_norm_w', 'new_v_gla_gate_w2', 'new_v_gla_gate_b', 'new_v_gla_norm_w', 'new_v_w_out', 'new_v_ffn_norm_w', 'new_v_w_gate', 'new_v_w_up', 'new_v_w_down', 'new_v_final_norm_w']
TWIN_LEAF_KINDS = {'loss': 'loss', 'grad_x': 'grad_x', 'grad_meta_tokens': 'grad_w', 'grad_attn_norm_w': 'grad_w', 'grad_w_in': 'grad_w', 'grad_gdn_conv_w': 'grad_w', 'grad_gdn_a_log': 'grad_w', 'grad_gdn_dt_bias': 'grad_w', 'grad_gdn_norm_w': 'grad_w', 'grad_gla_gate_w2': 'grad_w', 'grad_gla_gate_b': 'grad_w', 'grad_gla_norm_w': 'grad_w', 'grad_w_out': 'grad_w', 'grad_ffn_norm_w': 'grad_w', 'grad_w_gate': 'grad_w', 'grad_w_up': 'grad_w', 'grad_w_down': 'grad_w', 'grad_final_norm_w': 'grad_w', 'delta_meta_tokens': 'delta_w', 'delta_attn_norm_w': 'delta_w', 'delta_w_in': 'delta_w', 'delta_gdn_conv_w': 'delta_w', 'delta_gdn_a_log': 'delta_w', 'delta_gdn_dt_bias': 'delta_w', 'delta_gdn_norm_w': 'delta_w', 'delta_gla_gate_w2': 'delta_w', 'delta_gla_gate_b': 'delta_w', 'delta_gla_norm_w': 'delta_w', 'delta_w_out': 'delta_w', 'delta_ffn_norm_w': 'delta_w', 'delta_w_gate': 'delta_w', 'delta_w_up': 'delta_w', 'delta_w_down': 'delta_w', 'delta_final_norm_w': 'delta_w', 'new_m_meta_tokens': 'new_m', 'new_m_attn_norm_w': 'new_m', 'new_m_w_in': 'new_m', 'new_m_gdn_conv_w': 'new_m', 'new_m_gdn_a_log': 'new_m', 'new_m_gdn_dt_bias': 'new_m', 'new_m_gdn_norm_w': 'new_m', 'new_m_gla_gate_w2': 'new_m', 'new_m_gla_gate_b': 'new_m', 'new_m_gla_norm_w': 'new_m', 'new_m_w_out': 'new_m', 'new_m_ffn_norm_w': 'new_m', 'new_m_w_gate': 'new_m', 'new_m_w_up': 'new_m', 'new_m_w_down': 'new_m', 'new_m_final_norm_w': 'new_m', 'new_v_meta_tokens': 'new_v', 'new_v_attn_norm_w': 'new_v', 'new_v_w_in': 'new_v', 'new_v_gdn_conv_w': 'new_v', 'new_v_gdn_a_log': 'new_v', 'new_v_gdn_dt_bias': 'new_v', 'new_v_gdn_norm_w': 'new_v', 'new_v_gla_gate_w2': 'new_v', 'new_v_gla_gate_b': 'new_v', 'new_v_gla_norm_w': 'new_v', 'new_v_w_out': 'new_v', 'new_v_ffn_norm_w': 'new_v', 'new_v_w_gate': 'new_v', 'new_v_w_up': 'new_v', 'new_v_w_down': 'new_v', 'new_v_final_norm_w': 'new_v'}


def _forward(args):
    return _fwd_reference(*[args[k] for k in FWD_PARAMS])


def _output_shape():
    def fwd():
        inp = _fwd_setup_inputs(0)
        return _fwd_reference(*[inp[k] for k in FWD_PARAMS])
    out = _jax.eval_shape(fwd)
    return out.shape, out.dtype

N_MICROBATCH = 1
ADAM_LR = 0.001
ADAM_B1 = 0.9
ADAM_B2 = 0.999
ADAM_EPS = 1e-08
ADAM_WD = 0.01
ADAM_STEP = 10
PER_EXAMPLE_BATCH_AXIS = {'x': 0, 'loss_target': 0}
SHARED_INPUTS = []
_WEIGHT_DTYPES = {'meta_tokens': _jnp.float32, 'attn_norm_w': _jnp.float32, 'w_in': _jnp.float32, 'gdn_conv_w': _jnp.float32, 'gdn_a_log': _jnp.float32, 'gdn_dt_bias': _jnp.float32, 'gdn_norm_w': _jnp.float32, 'gla_gate_w2': _jnp.float32, 'gla_gate_b': _jnp.float32, 'gla_norm_w': _jnp.float32, 'w_out': _jnp.float32, 'ffn_norm_w': _jnp.float32, 'w_gate': _jnp.float32, 'w_up': _jnp.float32, 'w_down': _jnp.float32, 'final_norm_w': _jnp.float32}
MOMENT_SCALE = {'meta_tokens': 4.300333e-03, 'attn_norm_w': 1.326938e-01, 'w_in': 7.076064e-02, 'gdn_conv_w': 5.392363e-02, 'gdn_a_log': 4.100746e-01, 'gdn_dt_bias': 4.006318e-01, 'gdn_norm_w': 1.997979e-01, 'gla_gate_w2': 1.207218e-02, 'gla_gate_b': 5.058845e-02, 'gla_norm_w': 1.462475e-01, 'w_out': 7.054160e-02, 'ffn_norm_w': 8.245636e-02, 'w_gate': 3.564386e-02, 'w_up': 3.450447e-02, 'w_down': 5.725913e-02, 'final_norm_w': 3.192518e+01}


def _to_microbatches(a, axis):
    t = _jnp.moveaxis(a, axis, 0)
    t = t.reshape((N_MICROBATCH, t.shape[0] // N_MICROBATCH) + t.shape[1:])
    return _jnp.moveaxis(t, 1, axis + 1)


def setup_inputs(seed: int = 0) -> dict:
    inp = _fwd_setup_inputs(seed)
    key = _jax.random.fold_in(_jax.random.key(seed), 7919)
    shape, _ = _output_shape()
    out = dict(inp)
    out["loss_target"] = _jax.random.normal(_jax.random.fold_in(key, 0), shape, _jnp.float32)
    for i, name in enumerate(TWIN_WEIGHTS):
        w = inp[name].astype(_jnp.float32)
        if MOMENT_SCALE is None:
            s = _jnp.sqrt(_jnp.mean(_jnp.square(w)) + 1e-30)
        else:
            s = MOMENT_SCALE[name]
        km, kv = _jax.random.split(_jax.random.fold_in(key, i + 1))
        out[name] = w
        out["m_" + name] = s * _jax.random.normal(km, w.shape, _jnp.float32)
        out["v_" + name] = (s * s) * _jax.random.uniform(kv, w.shape, _jnp.float32, 0.5, 1.5)
    if N_MICROBATCH > 1:
        for name, axis in PER_EXAMPLE_BATCH_AXIS.items():
            out[name] = _to_microbatches(out[name], axis)
    return {'x': out['x'], 'meta_tokens': out['meta_tokens'], 'attn_norm_w': out['attn_norm_w'], 'w_in': out['w_in'], 'gdn_conv_w': out['gdn_conv_w'], 'gdn_a_log': out['gdn_a_log'], 'gdn_dt_bias': out['gdn_dt_bias'], 'gdn_norm_w': out['gdn_norm_w'], 'gla_gate_w2': out['gla_gate_w2'], 'gla_gate_b': out['gla_gate_b'], 'gla_norm_w': out['gla_norm_w'], 'w_out': out['w_out'], 'ffn_norm_w': out['ffn_norm_w'], 'w_gate': out['w_gate'], 'w_up': out['w_up'], 'w_down': out['w_down'], 'final_norm_w': out['final_norm_w'], 'loss_target': out['loss_target'], 'm_meta_tokens': out['m_meta_tokens'], 'm_attn_norm_w': out['m_attn_norm_w'], 'm_w_in': out['m_w_in'], 'm_gdn_conv_w': out['m_gdn_conv_w'], 'm_gdn_a_log': out['m_gdn_a_log'], 'm_gdn_dt_bias': out['m_gdn_dt_bias'], 'm_gdn_norm_w': out['m_gdn_norm_w'], 'm_gla_gate_w2': out['m_gla_gate_w2'], 'm_gla_gate_b': out['m_gla_gate_b'], 'm_gla_norm_w': out['m_gla_norm_w'], 'm_w_out': out['m_w_out'], 'm_ffn_norm_w': out['m_ffn_norm_w'], 'm_w_gate': out['m_w_gate'], 'm_w_up': out['m_w_up'], 'm_w_down': out['m_w_down'], 'm_final_norm_w': out['m_final_norm_w'], 'v_meta_tokens': out['v_meta_tokens'], 'v_attn_norm_w': out['v_attn_norm_w'], 'v_w_in': out['v_w_in'], 'v_gdn_conv_w': out['v_gdn_conv_w'], 'v_gdn_a_log': out['v_gdn_a_log'], 'v_gdn_dt_bias': out['v_gdn_dt_bias'], 'v_gdn_norm_w': out['v_gdn_norm_w'], 'v_gla_gate_w2': out['v_gla_gate_w2'], 'v_gla_gate_b': out['v_gla_gate_b'], 'v_gla_norm_w': out['v_gla_norm_w'], 'v_w_out': out['v_w_out'], 'v_ffn_norm_w': out['v_ffn_norm_w'], 'v_w_gate': out['v_w_gate'], 'v_w_up': out['v_w_up'], 'v_w_down': out['v_w_down'], 'v_final_norm_w': out['v_final_norm_w']}


def _loss(weights, diff, rest, loss_target):
    with _jax.named_scope("forward"):
        args = {**rest, TWIN_DIFF_INPUT: diff, **{k: w.astype(_WEIGHT_DTYPES[k]) for k, w in weights.items()}}
        y = _forward(args)
    with _jax.named_scope("loss_head"):
        err = _jnp.square(y.astype(_jnp.float32) - loss_target)
        return 0.5 * _jnp.sum(_jnp.mean(err, axis=-1)) if err.ndim else 0.5 * err


def _adamw(w, g, m, v):
    m = ADAM_B1 * m + (1.0 - ADAM_B1) * g
    v = ADAM_B2 * v + (1.0 - ADAM_B2) * _jnp.square(g)
    m_hat = m / (1.0 - ADAM_B1 ** ADAM_STEP)
    v_hat = v / (1.0 - ADAM_B2 ** ADAM_STEP)
    delta = -ADAM_LR * (m_hat / (_jnp.sqrt(v_hat) + ADAM_EPS) + ADAM_WD * w)
    return delta, m, v


def reference(x, meta_tokens, attn_norm_w, w_in, gdn_conv_w, gdn_a_log, gdn_dt_bias, gdn_norm_w, gla_gate_w2, gla_gate_b, gla_norm_w, w_out, ffn_norm_w, w_gate, w_up, w_down, final_norm_w, loss_target, m_meta_tokens, m_attn_norm_w, m_w_in, m_gdn_conv_w, m_gdn_a_log, m_gdn_dt_bias, m_gdn_norm_w, m_gla_gate_w2, m_gla_gate_b, m_gla_norm_w, m_w_out, m_ffn_norm_w, m_w_gate, m_w_up, m_w_down, m_final_norm_w, v_meta_tokens, v_attn_norm_w, v_w_in, v_gdn_conv_w, v_gdn_a_log, v_gdn_dt_bias, v_gdn_norm_w, v_gla_gate_w2, v_gla_gate_b, v_gla_norm_w, v_w_out, v_ffn_norm_w, v_w_gate, v_w_up, v_w_down, v_final_norm_w):
    given = dict(x=x, meta_tokens=meta_tokens, attn_norm_w=attn_norm_w, w_in=w_in, gdn_conv_w=gdn_conv_w, gdn_a_log=gdn_a_log, gdn_dt_bias=gdn_dt_bias, gdn_norm_w=gdn_norm_w, gla_gate_w2=gla_gate_w2, gla_gate_b=gla_gate_b, gla_norm_w=gla_norm_w, w_out=w_out, ffn_norm_w=ffn_norm_w, w_gate=w_gate, w_up=w_up, w_down=w_down, final_norm_w=final_norm_w, loss_target=loss_target, m_meta_tokens=m_meta_tokens, m_attn_norm_w=m_attn_norm_w, m_w_in=m_w_in, m_gdn_conv_w=m_gdn_conv_w, m_gdn_a_log=m_gdn_a_log, m_gdn_dt_bias=m_gdn_dt_bias, m_gdn_norm_w=m_gdn_norm_w, m_gla_gate_w2=m_gla_gate_w2, m_gla_gate_b=m_gla_gate_b, m_gla_norm_w=m_gla_norm_w, m_w_out=m_w_out, m_ffn_norm_w=m_ffn_norm_w, m_w_gate=m_w_gate, m_w_up=m_w_up, m_w_down=m_w_down, m_final_norm_w=m_final_norm_w, v_meta_tokens=v_meta_tokens, v_attn_norm_w=v_attn_norm_w, v_w_in=v_w_in, v_gdn_conv_w=v_gdn_conv_w, v_gdn_a_log=v_gdn_a_log, v_gdn_dt_bias=v_gdn_dt_bias, v_gdn_norm_w=v_gdn_norm_w, v_gla_gate_w2=v_gla_gate_w2, v_gla_gate_b=v_gla_gate_b, v_gla_norm_w=v_gla_norm_w, v_w_out=v_w_out, v_ffn_norm_w=v_ffn_norm_w, v_w_gate=v_w_gate, v_w_up=v_w_up, v_w_down=v_w_down, v_final_norm_w=v_final_norm_w)
    weights = {n: given[n] for n in TWIN_WEIGHTS}
    shared = {n: given[n] for n in SHARED_INPUTS}
    per_example = {n: given[n] for n in ['x']}
    grad_fn = _jax.value_and_grad(_loss, argnums=(0, 1))

    def one_microbatch(ex, loss_target):
        ex = dict(ex)
        diff = ex.pop(TWIN_DIFF_INPUT)
        return grad_fn(weights, diff, {**shared, **ex}, loss_target)

    if N_MICROBATCH == 1:
        loss, (grad_w, grad_x) = one_microbatch(per_example, given["loss_target"])
    else:
        def body(carry, xs):
            loss_sum, grad_sum = carry
            l_k, (gw_k, gx_k) = one_microbatch(xs[0], xs[1])
            with _jax.named_scope("update"):
                return (loss_sum + l_k, _jax.tree.map(_jnp.add, grad_sum, gw_k)), gx_k

        init = (_jnp.zeros((), _jnp.float32), _jax.tree.map(_jnp.zeros_like, weights))
        (loss, grad_w), grad_x = _jax.lax.scan(body, init, (per_example, given["loss_target"]))
    with _jax.named_scope("update"):
        delta_w, new_m, new_v = {}, {}, {}
        for n in TWIN_WEIGHTS:
            delta_w[n], new_m[n], new_v[n] = _adamw(weights[n], grad_w[n], given["m_" + n], given["v_" + n])
    return (loss, grad_x, *[grad_w[n] for n in TWIN_WEIGHTS], *[delta_w[n] for n in TWIN_WEIGHTS],
            *[new_m[n] for n in TWIN_WEIGHTS], *[new_v[n] for n in TWIN_WEIGHTS])
```

```python
import jax
import jax.numpy as jnp
from jax import lax
from jax.experimental import pallas as pl
from jax.experimental.pallas import tpu as pltpu

f32 = jnp.float32
bf16 = jnp.bfloat16
HI = lax.Precision.HIGHEST
MESH = pl.DeviceIdType.MESH

N_META = 16
CONV_K = 4
GDN_H, GDN_DK, GDN_DV, GDN_C = 8, 128, 128, 64
GLA_H, GLA_DK, GLA_DV, GLA_C = 4, 128, 256, 16
GATE_RANK = 16
GATE_NORMALIZER = 16.0
EPS = 1e-6
GDN_QK = GDN_H * GDN_DK
GDN_V = GDN_H * GDN_DV
GLA_QK = GLA_H * GLA_DK
GLA_V = GLA_H * GLA_DV
PAD = (-N_META) % GDN_C
OFF = PAD + N_META
ROWS = 64

R_QKV, R_Z, R_AB, R_G, R_GR, R_LR, R_END = 0, 3072, 4096, 4112, 6160, 7184, 7200
C_ZR, C_G, C_QKV, C_END = 0, 2048, 4096, 7168
ZR_W = GDN_V + GLA_V
G_W = 2 * GLA_QK + GLA_V
QKV_W = 2 * GDN_QK + GDN_V
Q0, K0, V0 = 0, GDN_QK, 2 * GDN_QK
SM_W = 128
SM_LR = 2 * GDN_H

ADAM_LR, ADAM_B1, ADAM_B2, ADAM_EPS, ADAM_WD, ADAM_STEP = 0.001, 0.9, 0.999, 1e-08, 0.01, 10

VMEM_LIMIT_V7X = 56 * 1024 * 1024
N_DEV = 8
N_CHIP = 4


def _params(*sem):
    return pltpu.CompilerParams(dimension_semantics=sem, vmem_limit_bytes=VMEM_LIMIT_V7X)


def _tile(n, cap, mult=16):
    best = None
    for d in range(mult, min(n, cap) + 1, mult):
        if n % d == 0:
            best = d
    assert best is not None, (n, cap, mult)
    return best


NN = ((1,), (0,))
NT = ((1,), (1,))
TN = ((0,), (0,))


def _dot(a, b, dims, prec=None):
    return lax.dot_general(a, b, (dims, ((), ())), precision=prec, preferred_element_type=f32)


def _mmb(a, b, dims):
    return _dot(a.astype(bf16), b.astype(bf16), dims)


def _sigmoid(x):
    return jax.nn.sigmoid(x)


def _silu(x):
    return x * _sigmoid(x)


def _dsilu(x):
    s = _sigmoid(x)
    return s * (1.0 + x * (1.0 - s))


def _log1p_exp_neg_abs(x):
    t = jnp.exp(-jnp.abs(x))
    u = 1.0 + t
    d = u - 1.0
    return jnp.where(d == 0.0, t, jnp.log(u) * (t / jnp.where(d == 0.0, 1.0, d)))


def _softplus(x):
    return jnp.maximum(x, 0.0) + _log1p_exp_neg_abs(x)


def _log_sigmoid(x):
    return jnp.minimum(x, 0.0) - _log1p_exp_neg_abs(x)


def _rms(x):
    r = lax.rsqrt(jnp.mean(x * x, axis=-1, keepdims=True) + EPS)
    return x * r, r


def _rms_bwd(dy, xh, r, w):
    t = dy * w
    return r * (t - xh * jnp.mean(t * xh, axis=-1, keepdims=True))


def _l2n(x):
    return x * lax.rsqrt(jnp.sum(x * x, axis=-1, keepdims=True) + EPS)


def _gdn_chunk(S, qr, kr, v, beta, g):
    C, dk = qr.shape
    row = lax.broadcasted_iota(jnp.int32, (C, C), 0)
    col = lax.broadcasted_iota(jnp.int32, (C, C), 1)
    causal = row >= col
    strict = row > col
    eye = (row == col).astype(f32)
    q = _l2n(qr) * (dk ** -0.5)
    k = _l2n(kr)
    mc = _dot(causal.astype(f32), jnp.broadcast_to(g, (C, C)), NN, HI)
    gc = mc[:, 0:1]
    dmat = mc - mc.T
    decay = jnp.where(causal, jnp.exp(jnp.where(causal, dmat, 0.0)), 0.0)
    kb = k * beta
    a = jnp.where(strict, _mmb(kb, k, NT) * decay, 0.0)
    p = eye - a
    pw = a
    n = 2
    while n < C:
        pw = _dot(pw, pw, NN, HI)
        p = _dot(p, eye + pw, NN, HI)
        n *= 2
    egc = jnp.exp(gc)
    u = _mmb(p, v * beta, NN)
    w = _mmb(p, kb * egc, NN)
    qk = jnp.where(causal, _mmb(q, k, NT) * decay, 0.0)
    v_new = u - _mmb(w, S, NN)
    o = _mmb(q * egc, S, NN) + _mmb(qk, v_new, NN)
    gl = gc[C - 1:C, :]
    kd = k * jnp.exp(gl - gc)
    S2 = S * jnp.exp(gl) + _mmb(kd, v_new, TN)
    return S2, o


def _gla_chunk(St, qr, k, v, la):
    C, dk = qr.shape
    q = qr * (dk ** -0.5)
    row = lax.broadcasted_iota(jnp.int32, (C, C), 0)
    col = lax.broadcasted_iota(jnp.int32, (C, C), 1)
    b = _dot((row >= col).astype(f32), la, NN, HI)
    o = _mmb(q * jnp.exp(b), St, NT)
    ri = lax.broadcasted_iota(jnp.int32, (C, dk), 0)
    for j in range(C):
        m = ri >= j
        e = jnp.where(m, jnp.exp(jnp.where(m, b - b[j:j + 1, :], 0.0)), 0.0)
        s = jnp.sum(q * k[j:j + 1, :] * e, axis=-1, keepdims=True)
        o = o + s * v[j:j + 1, :]
    bl = b[C - 1:C, :]
    St2 = St * jnp.exp(bl) + _mmb(v, k * jnp.exp(bl - b), TN)
    return St2, o


def _mm(a, b, mode, name, *, tm_cap=1408, tn_cap=1024, tk_cap=2048, out_dtype=f32, acc_in=None):
    if mode == "nn":
        (M, K), (K2, N) = a.shape, b.shape
    elif mode == "nt":
        (M, K), (N, K2) = a.shape, b.shape
    else:
        (K, M), (K2, N) = a.shape, b.shape
    assert K == K2, (name, a.shape, b.shape)
    tm = _tile(M, tm_cap)
    tn = _tile(N, tn_cap, 128)
    tk = _tile(K, tk_cap, 128 if K % 128 == 0 else 16)
    nk = K // tk
    dims = {"nn": NN, "nt": NT, "tn": TN}[mode]
    use_scratch = nk > 1 and out_dtype != f32

    def body(*refs):
        if acc_in is not None:
            a_ref, b_ref, c_ref, o_ref, *scr = refs
        else:
            a_ref, b_ref, o_ref, *scr = refs
            c_ref = None
        p = _mmb(a_ref[...], b_ref[...], dims)
        if nk == 1:
            if c_ref is not None:
                p = p + c_ref[...]
            o_ref[...] = p.astype(out_dtype)
            return
        k = pl.program_id(2)
        acc = scr[0] if use_scratch else o_ref

        @pl.when(k == 0)
        def _():
            acc[...] = p if c_ref is None else p + c_ref[...]

        @pl.when(k > 0)
        def _():
            acc[...] += p

        if use_scratch:
            @pl.when(k == nk - 1)
            def _():
                o_ref[...] = acc[...].astype(out_dtype)

    if mode == "tn":
        a_spec = pl.BlockSpec((tk, tm), lambda i, j, k: (k, i))
    else:
        a_spec = pl.BlockSpec((tm, tk), lambda i, j, k: (i, k))
    if mode == "nt":
        b_spec = pl.BlockSpec((tn, tk), lambda i, j, k: (j, k))
    else:
        b_spec = pl.BlockSpec((tk, tn), lambda i, j, k: (k, j))
    o_spec = pl.BlockSpec((tm, tn), lambda i, j, k: (i, j))
    in_specs = [a_spec, b_spec]
    args = [a, b]
    if acc_in is not None:
        in_specs.append(o_spec)
        args.append(acc_in)
    return pl.pallas_call(
        body, name=name, grid=(M // tm, N // tn, nk), in_specs=in_specs, out_specs=o_spec,
        out_shape=jax.ShapeDtypeStruct((M, N), out_dtype),
        scratch_shapes=[pltpu.VMEM((tm, tn), f32)] if use_scratch else [],
        compiler_params=_params("parallel", "parallel", "arbitrary"),
    )(*args)


def _embed_norm(x3, m64, w):
    _, S, D = x3.shape
    Lp = OFF + S

    def body(x_ref, m_ref, w_ref, h_ref, n_ref):
        i = pl.program_id(0)
        h = jnp.where(i == 0, m_ref[...], x_ref[...])
        h_ref[...] = h
        xh, _ = _rms(h)
        n_ref[...] = (xh * w_ref[...]).astype(bf16)

    row = pl.BlockSpec((ROWS, D), lambda i: (i, 0))
    return pl.pallas_call(
        body, name="embed_norm", grid=(Lp // ROWS,),
        in_specs=[pl.BlockSpec((None, ROWS, D), lambda i: (0, jnp.maximum(i - 1, 0), 0)),
                  pl.BlockSpec((ROWS, D), lambda i: (0, 0)),
                  pl.BlockSpec((1, D), lambda i: (0, 0))],
        out_specs=[row, row],
        out_shape=[jax.ShapeDtypeStruct((Lp, D), f32), jax.ShapeDtypeStruct((Lp, D), bf16)],
        compiler_params=_params("parallel"),
    )(x3, m64, w)


def _add_norm(h, d, w):
    Lp, D = h.shape
    tr = _tile(Lp, 256)

    def body(h_ref, d_ref, w_ref, o_ref, n_ref):
        h1 = h_ref[...] + d_ref[...]
        o_ref[...] = h1
        xh, _ = _rms(h1)
        n_ref[...] = (xh * w_ref[...]).astype(bf16)

    row = pl.BlockSpec((tr, D), lambda i: (i, 0))
    return pl.pallas_call(
        body, name="add_norm", grid=(Lp // tr,),
        in_specs=[row, row, pl.BlockSpec((1, D), lambda i: (0, 0))], out_specs=[row, row],
        out_shape=[jax.ShapeDtypeStruct((Lp, D), f32), jax.ShapeDtypeStruct((Lp, D), bf16)],
        compiler_params=_params("parallel"),
    )(h, d, w)


def _norm_bwd(dn, h, dh, w):
    Lp, D = h.shape
    tr = _tile(Lp, 256)

    def body(dn_ref, h_ref, dh_ref, w_ref, o_ref, ob_ref, gw_ref):
        i = pl.program_id(0)
        xh, r = _rms(h_ref[...])
        dn_ = dn_ref[...]
        o = dh_ref[...] + _rms_bwd(dn_, xh, r, w_ref[...])
        o_ref[...] = o
        ob_ref[...] = o.astype(bf16)
        gw = jnp.sum(dn_ * xh, axis=0, keepdims=True)

        @pl.when(i == 0)
        def _():
            gw_ref[...] = gw

        @pl.when(i > 0)
        def _():
            gw_ref[...] += gw

    row = pl.BlockSpec((tr, D), lambda i: (i, 0))
    vec = pl.BlockSpec((1, D), lambda i: (0, 0))
    return pl.pallas_call(
        body, name="norm_bwd", grid=(Lp // tr,), in_specs=[row, row, row, vec], out_specs=[row, row, vec],
        out_shape=[jax.ShapeDtypeStruct((Lp, D), f32), jax.ShapeDtypeStruct((Lp, D), bf16),
                   jax.ShapeDtypeStruct((1, D), f32)],
        compiler_params=_params("arbitrary"),
    )(dn, h, dh, w)


def _embed_norm_bwd(dn, h, dh, w, S):
    Lp, D = h.shape

    def body(dn_ref, h_ref, dh_ref, w_ref, gx_ref, gm_ref, gw_ref):
        i = pl.program_id(0)
        xh, r = _rms(h_ref[...])
        dn_ = dn_ref[...]
        d0 = dh_ref[...] + _rms_bwd(dn_, xh, r, w_ref[...])
        gx_ref[...] = d0
        gw = jnp.sum(dn_ * xh, axis=0, keepdims=True)

        @pl.when(i == 0)
        def _():
            gm_ref[...] = d0[PAD:OFF, :]
            gw_ref[...] = gw

        @pl.when(i > 0)
        def _():
            gw_ref[...] += gw

    row = pl.BlockSpec((ROWS, D), lambda i: (i, 0))
    vec = pl.BlockSpec((1, D), lambda i: (0, 0))
    return pl.pallas_call(
        body, name="embed_norm_bwd", grid=(Lp // ROWS,), in_specs=[row, row, row, vec],
        out_specs=[pl.BlockSpec((None, ROWS, D), lambda i: (0, jnp.maximum(i - 1, 0), 0)),
                   pl.BlockSpec((N_META, D), lambda i: (0, 0)), vec],
        out_shape=[jax.ShapeDtypeStruct((1, S, D), f32), jax.ShapeDtypeStruct((N_META, D), f32),
                   jax.ShapeDtypeStruct((1, D), f32)],
        compiler_params=_params("arbitrary"),
    )(dn, h, dh, w)


def _final(h1, ffn, tgt3, w):
    Lp, D = h1.shape

    def body(h_ref, f_ref, t_ref, w_ref, d_ref, db_ref, l_ref, gw_ref):
        i = pl.program_id(0)
        h2 = h_ref[...] + f_ref[...]
        xh, r = _rms(h2)
        w_ = w_ref[...]
        e = xh * w_ - t_ref[...]
        valid = (i > 0).astype(f32)
        loss = 0.5 * jnp.sum(jnp.mean(e * e, axis=-1, keepdims=True), axis=0, keepdims=True) * valid
        dy = e * (valid / D)
        d = _rms_bwd(dy, xh, r, w_)
        d_ref[...] = d
        db_ref[...] = d.astype(bf16)
        gw = jnp.sum(dy * xh, axis=0, keepdims=True)

        @pl.when(i == 0)
        def _():
            l_ref[...] = jnp.zeros_like(l_ref)
            gw_ref[...] = jnp.zeros_like(gw_ref)

        l_ref[...] += jnp.broadcast_to(loss, l_ref.shape)
        gw_ref[...] += gw

    row = pl.BlockSpec((ROWS, D), lambda i: (i, 0))
    vec = pl.BlockSpec((1, D), lambda i: (0, 0))
    return pl.pallas_call(
        body, name="final_loss", grid=(Lp // ROWS,),
        in_specs=[row, row, pl.BlockSpec((None, ROWS, D), lambda i: (0, jnp.maximum(i - 1, 0), 0)), vec],
        out_specs=[row, row, pl.BlockSpec((8, 128), lambda i: (0, 0)), vec],
        out_shape=[jax.ShapeDtypeStruct((Lp, D), f32), jax.ShapeDtypeStruct((Lp, D), bf16),
                   jax.ShapeDtypeStruct((8, 128), f32), jax.ShapeDtypeStruct((1, D), f32)],
        compiler_params=_params("arbitrary"),
    )(h1, ffn, tgt3, w)


def _swiglu(gu):
    Lp, F2 = gu.shape
    F = F2 // 2
    tr = _tile(Lp, 256)
    tc = _tile(F, 512, 128)
    nc = F // tc

    def body(g_ref, u_ref, o_ref):
        o_ref[...] = (_silu(g_ref[...]) * u_ref[...]).astype(bf16)

    return pl.pallas_call(
        body, name="swiglu", grid=(Lp // tr, nc),
        in_specs=[pl.BlockSpec((tr, tc), lambda i, j: (i, j)), pl.BlockSpec((tr, tc), lambda i, j: (i, j + nc))],
        out_specs=pl.BlockSpec((tr, tc), lambda i, j: (i, j)),
        out_shape=jax.ShapeDtypeStruct((Lp, F), bf16),
        compiler_params=_params("parallel", "parallel"),
    )(gu, gu)


def _swiglu_bwd(gu, dact):
    Lp, F2 = gu.shape
    F = F2 // 2
    tr = _tile(Lp, 256)
    tc = _tile(F, 512, 128)
    nc = F // tc

    def body(g_ref, u_ref, d_ref, dg_ref, du_ref):
        g = g_ref[...]
        d = d_ref[...]
        dg_ref[...] = (d * u_ref[...] * _dsilu(g)).astype(bf16)
        du_ref[...] = (d * _silu(g)).astype(bf16)

    lo = pl.BlockSpec((tr, tc), lambda i, j: (i, j))
    hi = pl.BlockSpec((tr, tc), lambda i, j: (i, j + nc))
    dg, du = pl.pallas_call(
        body, name="swiglu_bwd", grid=(Lp // tr, nc), in_specs=[lo, hi, lo], out_specs=[lo, lo],
        out_shape=[jax.ShapeDtypeStruct((Lp, F), bf16)] * 2,
        compiler_params=_params("parallel", "parallel"),
    )(gu, gu, dact)
    return dg, du


def _gates(psm, w2p, gate_b, alog, dtb):
    Lp = psm.shape[0]
    tr = _tile(Lp, 256)

    def body(p_ref, w_ref, b_ref, a_ref, t_ref, gb_ref, la_ref):
        i = pl.program_id(0)
        psm_ = p_ref[...]
        lane = lax.broadcasted_iota(jnp.int32, psm_.shape, 1)
        rowi = lax.broadcasted_iota(jnp.int32, (tr, 1), 0) + i * tr
        g = -jnp.exp(a_ref[...]) * _softplus(psm_ + t_ref[...])
        beta = _sigmoid(psm_)
        gb = jnp.where(lane < GDN_H, g, jnp.where(lane < 2 * GDN_H, beta, 0.0))
        gb_ref[...] = gb * (rowi >= PAD).astype(f32)
        logit = _mmb(psm_, w_ref[...], NN) + b_ref[...]
        la_ref[...] = _log_sigmoid(logit) * (1.0 / GATE_NORMALIZER)

    row = pl.BlockSpec((tr, SM_W), lambda i: (i, 0))
    return pl.pallas_call(
        body, name="gates", grid=(Lp // tr,),
        in_specs=[row, pl.BlockSpec((SM_W, GLA_QK), lambda i: (0, 0)), pl.BlockSpec((1, GLA_QK), lambda i: (0, 0)),
                  pl.BlockSpec((1, SM_W), lambda i: (0, 0)), pl.BlockSpec((1, SM_W), lambda i: (0, 0))],
        out_specs=[row, pl.BlockSpec((tr, GLA_QK), lambda i: (i, 0))],
        out_shape=[jax.ShapeDtypeStruct((Lp, SM_W), f32), jax.ShapeDtypeStruct((Lp, GLA_QK), f32)],
        compiler_params=_params("parallel"),
    )(psm, w2p, gate_b, alog, dtb)


def _gates_bwd(psm, w2p, gate_b, alog, dtb, dgb, dla):
    Lp = psm.shape[0]
    tr = _tile(Lp, 256)

    def body(p_ref, w_ref, b_ref, a_ref, t_ref, dgb_ref, dla_ref, dp_ref, gw_ref, gb_ref, ga_ref, gt_ref):
        i = pl.program_id(0)
        psm_ = p_ref[...]
        lane = lax.broadcasted_iota(jnp.int32, psm_.shape, 1)
        rowi = lax.broadcasted_iota(jnp.int32, (tr, 1), 0) + i * tr
        d = dgb_ref[...] * (rowi >= PAD).astype(f32)
        ea = jnp.exp(a_ref[...])
        z = psm_ + t_ref[...]
        is_g = lane < GDN_H
        dz = jnp.where(is_g, -ea * _sigmoid(z) * d, 0.0)
        dalog = jnp.where(is_g, -ea * _softplus(z) * d, 0.0)
        beta = _sigmoid(psm_)
        dbeta = jnp.where(jnp.logical_and(lane >= GDN_H, lane < 2 * GDN_H), beta * (1.0 - beta) * d, 0.0)
        logit = _mmb(psm_, w_ref[...], NN) + b_ref[...]
        dlogit = dla_ref[...] * (_sigmoid(-logit) * (1.0 / GATE_NORMALIZER))
        dlr = _mmb(dlogit, w_ref[...], NT)
        dp_ref[...] = (dz + dbeta + dlr).astype(bf16)
        gw = _mmb(psm_, dlogit, TN)
        gb = jnp.sum(dlogit, axis=0, keepdims=True)
        ga = jnp.sum(dalog, axis=0, keepdims=True)
        gt = jnp.sum(dz, axis=0, keepdims=True)

        @pl.when(i == 0)
        def _():
            gw_ref[...] = gw
            gb_ref[...] = gb
            ga_ref[...] = ga
            gt_ref[...] = gt

        @pl.when(i > 0)
        def _():
            gw_ref[...] += gw
            gb_ref[...] += gb
            ga_ref[...] += ga
            gt_ref[...] += gt

    row = pl.BlockSpec((tr, SM_W), lambda i: (i, 0))
    wsp = pl.BlockSpec((SM_W, GLA_QK), lambda i: (0, 0))
    bsp = pl.BlockSpec((1, GLA_QK), lambda i: (0, 0))
    vsp = pl.BlockSpec((1, SM_W), lambda i: (0, 0))
    return pl.pallas_call(
        body, name="gates_bwd", grid=(Lp // tr,),
        in_specs=[row, wsp, bsp, vsp, vsp, row, pl.BlockSpec((tr, GLA_QK), lambda i: (i, 0))],
        out_specs=[row, wsp, bsp, vsp, vsp],
        out_shape=[jax.ShapeDtypeStruct((Lp, SM_W), bf16), jax.ShapeDtypeStruct((SM_W, GLA_QK), f32),
                   jax.ShapeDtypeStruct((1, GLA_QK), f32), jax.ShapeDtypeStruct((1, SM_W), f32),
                   jax.ShapeDtypeStruct((1, SM_W), f32)],
        compiler_params=_params("arbitrary"),
    )(psm, w2p, gate_b, alog, dtb, dgb, dla)


def _conv_pre(x_ext, w, n):
    rows = x_ext.shape[0]
    y = x_ext * w[CONV_K - 1:CONV_K, :]
    for s in range(1, CONV_K):
        y = y + pltpu.roll(x_ext, s, 0) * w[CONV_K - 1 - s:CONV_K - s, :]
    return y[rows - n:, :]


def _conv(proj, cw):
    Lp = proj.shape[0]
    W = cw.shape[1]
    tr = _tile(Lp, 256, 64)
    tc = _tile(W, 512, 128)
    c0 = C_QKV // tc

    def body(h_ref, x_ref, w_ref, o_ref):
        i = pl.program_id(1)
        halo = jnp.where(i == 0, 0.0, h_ref[...])
        x_ext = jnp.concatenate([halo, x_ref[...]], axis=0)
        o_ref[...] = _silu(_conv_pre(x_ext, w_ref[...], tr))

    return pl.pallas_call(
        body, name="conv", grid=(W // tc, Lp // tr),
        in_specs=[pl.BlockSpec((8, tc), lambda j, i: (jnp.maximum(i * (tr // 8) - 1, 0), j + c0)),
                  pl.BlockSpec((tr, tc), lambda j, i: (i, j + c0)),
                  pl.BlockSpec((CONV_K, tc), lambda j, i: (0, j))],
        out_specs=pl.BlockSpec((tr, tc), lambda j, i: (i, j)),
        out_shape=jax.ShapeDtypeStruct((Lp, W), f32),
        compiler_params=_params("parallel", "parallel"),
    )(proj, proj, cw)


def _conv_bwd(proj, cw, dy, dproj):
    Lp = proj.shape[0]
    W = cw.shape[1]
    tr = _tile(Lp, 256, 64)
    tc = _tile(W, 512, 128)
    c0 = C_QKV // tc
    nr = Lp // tr
    last8 = Lp // 8 - 1

    def body(xp_ref, x_ref, xn_ref, w_ref, d_ref, dn_ref, dproj_ref, o_ref, gw_ref):
        del dproj_ref
        i = pl.program_id(1)
        w = w_ref[...]
        xp = jnp.where(i == 0, 0.0, xp_ref[...])
        x_ext = jnp.concatenate([xp, x_ref[...], xn_ref[...]], axis=0)
        n = tr + 8
        pre = _conv_pre(x_ext, w, n)
        dn = jnp.where(i == nr - 1, 0.0, dn_ref[...])
        dpre = jnp.concatenate([d_ref[...], dn], axis=0) * _dsilu(pre)
        dx = dpre * w[CONV_K - 1:CONV_K, :]
        for s in range(1, CONV_K):
            dx = dx + pltpu.roll(dpre, n - s, 0) * w[CONV_K - 1 - s:CONV_K - s, :]
        o_ref[...] = dx[:tr, :].astype(bf16)
        dp = dpre[:tr, :]
        rows = []
        for k in range(CONV_K):
            xs = x_ext if k == CONV_K - 1 else pltpu.roll(x_ext, CONV_K - 1 - k, 0)
            rows.append(jnp.sum(dp * xs[8:8 + tr, :], axis=0, keepdims=True))
        gw = jnp.concatenate(rows, axis=0)

        @pl.when(i == 0)
        def _():
            gw_ref[...] = gw

        @pl.when(i > 0)
        def _():
            gw_ref[...] += gw

    cur = pl.BlockSpec((tr, tc), lambda j, i: (i, j))
    nxt = pl.BlockSpec((8, tc), lambda j, i: (jnp.minimum((i + 1) * (tr // 8), last8), j))
    pcur = pl.BlockSpec((tr, tc), lambda j, i: (i, j + c0))
    pprev = pl.BlockSpec((8, tc), lambda j, i: (jnp.maximum(i * (tr // 8) - 1, 0), j + c0))
    pnext = pl.BlockSpec((8, tc), lambda j, i: (jnp.minimum((i + 1) * (tr // 8), last8), j + c0))
    wsp = pl.BlockSpec((CONV_K, tc), lambda j, i: (0, j))
    return pl.pallas_call(
        body, name="conv_bwd", grid=(W // tc, nr),
        in_specs=[pprev, pcur, pnext, wsp, cur, nxt, pl.BlockSpec(memory_space=pl.ANY)],
        out_specs=[pcur, wsp],
        out_shape=[jax.ShapeDtypeStruct(dproj.shape, dproj.dtype), jax.ShapeDtypeStruct((CONV_K, W), f32)],
        input_output_aliases={6: 0},
        compiler_params=_params("parallel", "arbitrary"),
    )(proj, proj, proj, cw, dy, dy, dproj)


def _gdn_fwd(qkvc, gb):
    Lp = qkvc.shape[0]
    N = Lp // GDN_C

    def body(x_ref, gb_ref, o_ref, sall_ref, s_scr):
        @pl.when(pl.program_id(0) == 0)
        def _():
            s_scr[...] = jnp.zeros_like(s_scr)

        gbv = gb_ref[...]
        for h in range(GDN_H):
            S = s_scr[h]
            sall_ref[0, h] = S
            q = x_ref[:, Q0 + h * GDN_DK:Q0 + (h + 1) * GDN_DK]
            k = x_ref[:, K0 + h * GDN_DK:K0 + (h + 1) * GDN_DK]
            v = x_ref[:, V0 + h * GDN_DV:V0 + (h + 1) * GDN_DV]
            S2, o = _gdn_chunk(S, q, k, v, gbv[:, GDN_H + h:GDN_H + h + 1], gbv[:, h:h + 1])
            s_scr[h] = S2
            o_ref[:, h * GDN_DV:(h + 1) * GDN_DV] = o

    return pl.pallas_call(
        body, name="gdn_fwd", grid=(N,),
        in_specs=[pl.BlockSpec((GDN_C, QKV_W), lambda n: (n, 0)), pl.BlockSpec((GDN_C, SM_W), lambda n: (n, 0))],
        out_specs=[pl.BlockSpec((GDN_C, GDN_V), lambda n: (n, 0)),
                   pl.BlockSpec((1, GDN_H, GDN_DK, GDN_DV), lambda n: (n, 0, 0, 0))],
        out_shape=[jax.ShapeDtypeStruct((Lp, GDN_V), f32), jax.ShapeDtypeStruct((N, GDN_H, GDN_DK, GDN_DV), f32)],
        scratch_shapes=[pltpu.VMEM((GDN_H, GDN_DK, GDN_DV), f32)],
        compiler_params=_params("arbitrary"),
    )(qkvc, gb)


def _gdn_bwd(qkvc, gb, sall, do):
    Lp = qkvc.shape[0]
    N = Lp // GDN_C

    def body(x_ref, gb_ref, sall_ref, do_ref, dx_ref, dgb_ref, ds_scr):
        @pl.when(pl.program_id(0) == 0)
        def _():
            ds_scr[...] = jnp.zeros_like(ds_scr)

        gbv = gb_ref[...]
        lane = lax.broadcasted_iota(jnp.int32, (GDN_C, SM_W), 1)
        acc = jnp.zeros((GDN_C, SM_W), f32)
        for h in range(GDN_H):
            sq = slice(Q0 + h * GDN_DK, Q0 + (h + 1) * GDN_DK)
            sk = slice(K0 + h * GDN_DK, K0 + (h + 1) * GDN_DK)
            sv = slice(V0 + h * GDN_DV, V0 + (h + 1) * GDN_DV)
            _, vjp = jax.vjp(_gdn_chunk, sall_ref[0, h], x_ref[:, sq], x_ref[:, sk], x_ref[:, sv],
                             gbv[:, GDN_H + h:GDN_H + h + 1], gbv[:, h:h + 1])
            dS, dq, dk, dv, dbeta, dg = vjp((ds_scr[h], do_ref[:, h * GDN_DV:(h + 1) * GDN_DV]))
            ds_scr[h] = dS
            dx_ref[:, sq] = dq
            dx_ref[:, sk] = dk
            dx_ref[:, sv] = dv
            acc = acc + jnp.where(lane == h, dg, 0.0) + jnp.where(lane == GDN_H + h, dbeta, 0.0)
        dgb_ref[...] = acc

    rev = lambda n: (N - 1 - n, 0)
    return pl.pallas_call(
        body, name="gdn_bwd", grid=(N,),
        in_specs=[pl.BlockSpec((GDN_C, QKV_W), rev), pl.BlockSpec((GDN_C, SM_W), rev),
                  pl.BlockSpec((1, GDN_H, GDN_DK, GDN_DV), lambda n: (N - 1 - n, 0, 0, 0)),
                  pl.BlockSpec((GDN_C, GDN_V), rev)],
        out_specs=[pl.BlockSpec((GDN_C, QKV_W), rev), pl.BlockSpec((GDN_C, SM_W), rev)],
        out_shape=[jax.ShapeDtypeStruct((Lp, QKV_W), f32), jax.ShapeDtypeStruct((Lp, SM_W), f32)],
        scratch_shapes=[pltpu.VMEM((GDN_H, GDN_DK, GDN_DV), f32)],
        compiler_params=_params("arbitrary"),
    )(qkvc, gb, sall, do)


GLA_SUB = ROWS // GLA_C


def _gla_slices(h):
    sq = slice(h * GLA_DK, (h + 1) * GLA_DK)
    sk = slice(GLA_QK + h * GLA_DK, GLA_QK + (h + 1) * GLA_DK)
    sv = slice(2 * GLA_QK + h * GLA_DV, 2 * GLA_QK + (h + 1) * GLA_DV)
    return sq, sk, sv


def _gla_fwd(proj, la):
    Lp = proj.shape[0]
    NB = Lp // ROWS

    def body(x_ref, la_ref, o_ref, sall_ref, s_scr):
        @pl.when(pl.program_id(0) == 0)
        def _():
            s_scr[...] = jnp.zeros_like(s_scr)

        def sub(c, carry):
            r = pl.ds(pl.multiple_of(c * GLA_C, GLA_C), GLA_C)
            for h in range(GLA_H):
                sq, sk, sv = _gla_slices(h)
                St = s_scr[h]
                sall_ref[c, h] = St
                St2, o = _gla_chunk(St, x_ref[r, sq], x_ref[r, sk], x_ref[r, sv], la_ref[r, sq])
                s_scr[h] = St2
                o_ref[r, h * GLA_DV:(h + 1) * GLA_DV] = o
            return carry

        lax.fori_loop(0, GLA_SUB, sub, 0)

    return pl.pallas_call(
        body, name="gla_fwd", grid=(NB,),
        in_specs=[pl.BlockSpec((ROWS, G_W), lambda n: (n, C_G // G_W)),
                  pl.BlockSpec((ROWS, GLA_QK), lambda n: (n, 0))],
        out_specs=[pl.BlockSpec((ROWS, GLA_V), lambda n: (n, 0)),
                   pl.BlockSpec((GLA_SUB, GLA_H, GLA_DV, GLA_DK), lambda n: (n, 0, 0, 0))],
        out_shape=[jax.ShapeDtypeStruct((Lp, GLA_V), f32),
                   jax.ShapeDtypeStruct((Lp // GLA_C, GLA_H, GLA_DV, GLA_DK), f32)],
        scratch_shapes=[pltpu.VMEM((GLA_H, GLA_DV, GLA_DK), f32)],
        compiler_params=_params("arbitrary"),
    )(proj, la)


def _gla_bwd(proj, la, sall, do, dproj):
    Lp = proj.shape[0]
    NB = Lp // ROWS

    def body(x_ref, la_ref, sall_ref, do_ref, dproj_ref, dx_ref, dla_ref, ds_scr):
        del dproj_ref

        @pl.when(pl.program_id(0) == 0)
        def _():
            ds_scr[...] = jnp.zeros_like(ds_scr)

        def sub(ci, carry):
            c = GLA_SUB - 1 - ci
            r = pl.ds(pl.multiple_of(c * GLA_C, GLA_C), GLA_C)
            for h in range(GLA_H):
                sq, sk, sv = _gla_slices(h)
                so = slice(h * GLA_DV, (h + 1) * GLA_DV)
                _, vjp = jax.vjp(_gla_chunk, sall_ref[c, h], x_ref[r, sq], x_ref[r, sk], x_ref[r, sv], la_ref[r, sq])
                dS, dq, dk, dv, dl = vjp((ds_scr[h], do_ref[r, so]))
                ds_scr[h] = dS
                dx_ref[r, sq] = dq.astype(bf16)
                dx_ref[r, sk] = dk.astype(bf16)
                dx_ref[r, sv] = dv.astype(bf16)
                dla_ref[r, sq] = dl
            return carry

        lax.fori_loop(0, GLA_SUB, sub, 0)

    x_spec = pl.BlockSpec((ROWS, G_W), lambda n: (NB - 1 - n, C_G // G_W))
    rev = lambda n: (NB - 1 - n, 0)
    return pl.pallas_call(
        body, name="gla_bwd", grid=(NB,),
        in_specs=[x_spec, pl.BlockSpec((ROWS, GLA_QK), rev),
                  pl.BlockSpec((GLA_SUB, GLA_H, GLA_DV, GLA_DK), lambda n: (NB - 1 - n, 0, 0, 0)),
                  pl.BlockSpec((ROWS, GLA_V), rev), pl.BlockSpec(memory_space=pl.ANY)],
        out_specs=[x_spec, pl.BlockSpec((ROWS, GLA_QK), rev)],
        out_shape=[jax.ShapeDtypeStruct(dproj.shape, dproj.dtype), jax.ShapeDtypeStruct((Lp, GLA_QK), f32)],
        input_output_aliases={4: 0},
        scratch_shapes=[pltpu.VMEM((GLA_H, GLA_DV, GLA_DK), f32)],
        compiler_params=_params("arbitrary"),
    )(proj, la, sall, do, dproj)


def _gated_norm_fn(og, ol, zr, wg, wl):
    outs = []
    for h in range(GDN_H):
        s = slice(h * GDN_DV, (h + 1) * GDN_DV)
        outs.append(_rms(og[:, s])[0] * wg * _silu(zr[:, s]))
    for h in range(GLA_H):
        s = slice(h * GLA_DV, (h + 1) * GLA_DV)
        sr = slice(GDN_V + h * GLA_DV, GDN_V + (h + 1) * GLA_DV)
        outs.append(_rms(ol[:, s])[0] * wl * _silu(zr[:, sr]))
    return jnp.concatenate(outs, axis=-1)


def _gated_norm(og, ol, proj, wg, wl):
    Lp = og.shape[0]
    tr = _tile(Lp, 256)

    def body(og_ref, ol_ref, zr_ref, wg_ref, wl_ref, o_ref):
        o_ref[...] = _gated_norm_fn(og_ref[...], ol_ref[...], zr_ref[...], wg_ref[...], wl_ref[...]).astype(bf16)

    return pl.pallas_call(
        body, name="gated_norm", grid=(Lp // tr,),
        in_specs=[pl.BlockSpec((tr, GDN_V), lambda i: (i, 0)), pl.BlockSpec((tr, GLA_V), lambda i: (i, 0)),
                  pl.BlockSpec((tr, ZR_W), lambda i: (i, C_ZR // ZR_W)),
                  pl.BlockSpec((1, GDN_DV), lambda i: (0, 0)), pl.BlockSpec((1, GLA_DV), lambda i: (0, 0))],
        out_specs=pl.BlockSpec((tr, ZR_W), lambda i: (i, 0)),
        out_shape=jax.ShapeDtypeStruct((Lp, ZR_W), bf16),
        compiler_params=_params("parallel"),
    )(og, ol, proj, wg, wl)


def _gated_norm_bwd(og, ol, proj, wg, wl, dmix):
    Lp = og.shape[0]
    tr = _tile(Lp, 128)

    def body(og_ref, ol_ref, zr_ref, wg_ref, wl_ref, d_ref, dog_ref, dol_ref, dzr_ref, gwg_ref, gwl_ref):
        i = pl.program_id(0)
        _, vjp = jax.vjp(_gated_norm_fn, og_ref[...], ol_ref[...], zr_ref[...], wg_ref[...], wl_ref[...])
        dog, dol, dzr, gwg, gwl = vjp(d_ref[...])
        dog_ref[...] = dog
        dol_ref[...] = dol
        dzr_ref[...] = dzr.astype(bf16)

        @pl.when(i == 0)
        def _():
            gwg_ref[...] = gwg
            gwl_ref[...] = gwl

        @pl.when(i > 0)
        def _():
            gwg_ref[...] += gwg
            gwl_ref[...] += gwl

    og_spec = pl.BlockSpec((tr, GDN_V), lambda i: (i, 0))
    ol_spec = pl.BlockSpec((tr, GLA_V), lambda i: (i, 0))
    zr_spec = pl.BlockSpec((tr, ZR_W), lambda i: (i, C_ZR // ZR_W))
    vg = pl.BlockSpec((1, GDN_DV), lambda i: (0, 0))
    vl = pl.BlockSpec((1, GLA_DV), lambda i: (0, 0))
    return pl.pallas_call(
        body, name="gated_norm_bwd", grid=(Lp // tr,),
        in_specs=[og_spec, ol_spec, zr_spec, vg, vl, pl.BlockSpec((tr, ZR_W), lambda i: (i, 0))],
        out_specs=[og_spec, ol_spec, zr_spec, vg, vl],
        out_shape=[jax.ShapeDtypeStruct((Lp, GDN_V), f32), jax.ShapeDtypeStruct((Lp, GLA_V), f32),
                   jax.ShapeDtypeStruct((Lp, C_END), bf16),
                   jax.ShapeDtypeStruct((1, GDN_DV), f32), jax.ShapeDtypeStruct((1, GLA_DV), f32)],
        compiler_params=_params("arbitrary"),
    )(og, ol, proj, wg, wl, dmix)


def _adamw(g, w, m, v, name):
    R, C = g.shape
    tr = _tile(R, 256, 8) if R % 8 == 0 and R > 256 else R
    c1 = 1.0 - ADAM_B1 ** ADAM_STEP
    c2 = 1.0 - ADAM_B2 ** ADAM_STEP

    def body(g_ref, w_ref, m_ref, v_ref, d_ref, mo_ref, vo_ref):
        g_ = g_ref[...]
        m2 = ADAM_B1 * m_ref[...] + (1.0 - ADAM_B1) * g_
        v2 = ADAM_B2 * v_ref[...] + (1.0 - ADAM_B2) * (g_ * g_)
        mo_ref[...] = m2
        vo_ref[...] = v2
        d_ref[...] = -ADAM_LR * ((m2 / c1) / (jnp.sqrt(v2 / c2) + ADAM_EPS) + ADAM_WD * w_ref[...])

    blk = pl.BlockSpec((tr, C), lambda i: (i, 0))
    return pl.pallas_call(
        body, name=name, grid=(R // tr,), in_specs=[blk] * 4, out_specs=[blk] * 3,
        out_shape=[jax.ShapeDtypeStruct((R, C), f32)] * 3,
        compiler_params=_params("parallel"),
    )(g, w, m, v)


def _sum_slots(r, name):
    n, R, C = r.shape
    tr = _tile(R, 128, 8) if R % 8 == 0 and R > 128 else R

    def body(r_ref, o_ref):
        acc = r_ref[0]
        for s in range(1, n):
            acc = acc + r_ref[s]
        o_ref[...] = acc

    return pl.pallas_call(
        body, name=name, grid=(R // tr,),
        in_specs=[pl.BlockSpec((n, tr, C), lambda i: (0, i, 0))],
        out_specs=pl.BlockSpec((tr, C), lambda i: (i, 0)),
        out_shape=jax.ShapeDtypeStruct((R, C), f32),
        compiler_params=_params("parallel"),
    )(r)


_ANY = pl.BlockSpec(memory_space=pl.ANY)


def _allgather_chips(arrs):
    n = len(arrs)

    def body(*refs):
        ins, outs = refs[:n], refs[n:2 * n]
        send, recv, lsem = refs[2 * n:]
        x, y, c = lax.axis_index("x"), lax.axis_index("y"), lax.axis_index("c")
        chips = [(1 - x, y), (x, 1 - y), (1 - x, 1 - y)]
        me = 2 * x + y
        local = [pltpu.make_async_copy(ins[a], outs[a].at[me], lsem.at[a]) for a in range(n)]
        for cp in local:
            cp.start()
        for a in range(n):
            for j, (px, py) in enumerate(chips):
                pltpu.make_async_remote_copy(
                    src_ref=ins[a], dst_ref=outs[a].at[me], send_sem=send.at[a, j], recv_sem=recv.at[a, j],
                    device_id=(px, py, c), device_id_type=MESH).start()
        for a in range(n):
            for j, (px, py) in enumerate(chips):
                pltpu.make_async_remote_copy(
                    src_ref=ins[a], dst_ref=outs[a].at[2 * px + py], send_sem=send.at[a, j], recv_sem=recv.at[a, j],
                    device_id=(px, py, c), device_id_type=MESH).wait()
        for cp in local:
            cp.wait()

    return pl.pallas_call(
        body, name="allgather_weights", in_specs=[_ANY] * n, out_specs=[_ANY] * n,
        out_shape=[jax.ShapeDtypeStruct((N_CHIP,) + a.shape, a.dtype) for a in arrs],
        scratch_shapes=[pltpu.SemaphoreType.DMA((n, 3)), pltpu.SemaphoreType.DMA((n, 3)), pltpu.SemaphoreType.DMA((n,))],
    )(*arrs)


def _exchange_slots(slotted, shared):
    ns, nb = len(slotted), len(shared)
    n = ns + nb

    def body(*refs):
        ins, outs = refs[:n], refs[n:2 * n]
        send, recv, lsem = refs[2 * n:]
        x, y, c = lax.axis_index("x"), lax.axis_index("y"), lax.axis_index("c")
        me = 4 * x + 2 * y + c

        def peer(o):
            tx = 1 - x if o & 4 else x
            ty = 1 - y if o & 2 else y
            tc = 1 - c if o & 1 else c
            return (tx, ty, tc), 4 * tx + 2 * ty + tc

        def src(a, t):
            return ins[a].at[t] if a < ns else ins[a]

        local = [pltpu.make_async_copy(src(a, me), outs[a].at[me], lsem.at[a]) for a in range(n)]
        for cp in local:
            cp.start()
        for a in range(n):
            for o in range(1, N_DEV):
                dev, t = peer(o)
                pltpu.make_async_remote_copy(
                    src_ref=src(a, t), dst_ref=outs[a].at[me], send_sem=send.at[a, o - 1], recv_sem=recv.at[a, o - 1],
                    device_id=dev, device_id_type=MESH).start()
        for a in range(n):
            for o in range(1, N_DEV):
                dev, t = peer(o)
                pltpu.make_async_remote_copy(
                    src_ref=src(a, t), dst_ref=outs[a].at[t], send_sem=send.at[a, o - 1], recv_sem=recv.at[a, o - 1],
                    device_id=dev, device_id_type=MESH).wait()
        for cp in local:
            cp.wait()

    shapes = [jax.ShapeDtypeStruct(a.shape, a.dtype) for a in slotted]
    shapes += [jax.ShapeDtypeStruct((N_DEV,) + b.shape, b.dtype) for b in shared]
    return pl.pallas_call(
        body, name="exchange_grads", in_specs=[_ANY] * n, out_specs=[_ANY] * n, out_shape=shapes,
        scratch_shapes=[pltpu.SemaphoreType.DMA((n, N_DEV - 1)), pltpu.SemaphoreType.DMA((n, N_DEV - 1)),
                        pltpu.SemaphoreType.DMA((n,))],
    )(*slotted, *shared)


def _exchange_halves(halves):
    n = len(halves)

    def body(*refs):
        ins, outs = refs[:n], refs[n:2 * n]
        send, recv, lsem = refs[2 * n:]
        x, y, c = lax.axis_index("x"), lax.axis_index("y"), lax.axis_index("c")
        local = [pltpu.make_async_copy(ins[a], outs[a].at[c], lsem.at[a]) for a in range(n)]
        for cp in local:
            cp.start()
        for a in range(n):
            pltpu.make_async_remote_copy(
                src_ref=ins[a], dst_ref=outs[a].at[c], send_sem=send.at[a], recv_sem=recv.at[a],
                device_id=(x, y, 1 - c), device_id_type=MESH).start()
        for a in range(n):
            pltpu.make_async_remote_copy(
                src_ref=ins[a], dst_ref=outs[a].at[1 - c], send_sem=send.at[a], recv_sem=recv.at[a],
                device_id=(x, y, 1 - c), device_id_type=MESH).wait()
        for cp in local:
            cp.wait()

    return pl.pallas_call(
        body, name="exchange_halves", in_specs=[_ANY] * n, out_specs=[_ANY] * n,
        out_shape=[jax.ShapeDtypeStruct((2,) + a.shape, a.dtype) for a in halves],
        scratch_shapes=[pltpu.SemaphoreType.DMA((n,)), pltpu.SemaphoreType.DMA((n,)), pltpu.SemaphoreType.DMA((n,))],
    )(*halves)


def _local_step(x, loss_target, meta, attn_norm_w, w_in, conv_w, a_log, dt_bias, gdn_norm_w, w2, gate_b,
                gla_norm_w, w_out, ffn_norm_w, w_gate, w_up, w_down, final_norm_w):
    _, S, D = x.shape
    F = w_gate.shape[1]
    wp = jnp.concatenate([w_in[:, R_Z:R_AB], w_in[:, R_GR:R_LR], w_in[:, R_G:R_GR], w_in[:, R_QKV:R_Z]], axis=1)
    wsm = jnp.concatenate([w_in[:, R_AB:R_G], w_in[:, R_LR:R_END],
                           jnp.zeros((D, SM_W - SM_LR - GATE_RANK), w_in.dtype)], axis=1)
    wgu = jnp.concatenate([w_gate, w_up], axis=1)
    w2p = jnp.pad(w2, ((SM_LR, SM_W - SM_LR - GATE_RANK), (0, 0)))
    alog_p = jnp.pad(a_log, ((0, 0), (0, SM_W - GDN_H)))
    dtb_p = jnp.pad(dt_bias, ((0, 0), (0, SM_W - GDN_H)))
    m64 = jnp.concatenate([jnp.zeros((PAD, D), f32), meta], axis=0)

    h0, n1 = _embed_norm(x, m64, attn_norm_w)
    proj = _mm(n1, wp, "nn", "proj")
    psm = _mm(n1, wsm, "nn", "proj_small")
    gb, la = _gates(psm, w2p, gate_b, alog_p, dtb_p)
    qkvc = _conv(proj, conv_w)
    og, sall = _gdn_fwd(qkvc, gb)
    ol, stall = _gla_fwd(proj, la)
    mixed = _gated_norm(og, ol, proj, gdn_norm_w, gla_norm_w)
    attn = _mm(mixed, w_out, "nn", "out_proj")
    h1, n2 = _add_norm(h0, attn, ffn_norm_w)
    gu = _mm(n2, wgu, "nn", "ffn_gate_up")
    act = _swiglu(gu)
    ffn = _mm(act, w_down, "nn", "ffn_down", tk_cap=1408)
    dh2, dh2b, lossp, g_final = _final(h1, ffn, loss_target, final_norm_w)

    dact = _mm(dh2b, w_down, "nt", "d_act")
    g_down = _mm(act, dh2b, "tn", "g_w_down", tn_cap=2048, tk_cap=1408)
    dg, du = _swiglu_bwd(gu, dact)
    g_gate = _mm(n2, dg, "tn", "g_w_gate", tm_cap=2048, tk_cap=1408)
    g_up = _mm(n2, du, "tn", "g_w_up", tm_cap=2048, tk_cap=1408)
    dn2 = _mm(dg, w_gate, "nt", "d_n2_gate", tk_cap=1408)
    dn2 = _mm(du, w_up, "nt", "d_n2_up", tk_cap=1408, acc_in=dn2)
    dh1, dh1b, g_ffn_norm = _norm_bwd(dn2, h1, dh2, ffn_norm_w)
    dmix = _mm(dh1b, w_out, "nt", "d_mixed")
    g_out = _mm(mixed, dh1b, "tn", "g_w_out", tn_cap=2048, tk_cap=1408)
    dog, dol, dproj, g_gdn_norm, g_gla_norm = _gated_norm_bwd(og, ol, proj, gdn_norm_w, gla_norm_w, dmix)
    dproj, dla = _gla_bwd(proj, la, stall, dol, dproj)
    dqkvc, dgb = _gdn_bwd(qkvc, gb, sall, dog)
    dproj, g_conv = _conv_bwd(proj, conv_w, dqkvc, dproj)
    dpsm, g_w2p, g_gate_b, g_alog, g_dtb = _gates_bwd(psm, w2p, gate_b, alog_p, dtb_p, dgb, dla)
    dn1 = _mm(dproj, wp, "nt", "d_n1", tk_cap=1024)
    dn1 = _mm(dpsm, wsm, "nt", "d_n1_small", acc_in=dn1)
    g_wp = _mm(n1, dproj, "tn", "g_w_in", tm_cap=2048, tk_cap=1408)
    g_wsm = _mm(n1, dpsm, "tn", "g_w_in_small", tm_cap=2048, tk_cap=1408)
    grad_x, g_meta, g_attn_norm = _embed_norm_bwd(dn1, h0, dh1, attn_norm_w, S)

    g_w_in = jnp.concatenate([g_wp[:, C_QKV:C_END], g_wp[:, C_ZR:C_ZR + GDN_V], g_wsm[:, :SM_LR],
                              g_wp[:, C_G:C_G + G_W], g_wp[:, C_ZR + GDN_V:C_ZR + ZR_W],
                              g_wsm[:, SM_LR:SM_LR + GATE_RANK]], axis=1)
    grads = dict(
        meta_tokens=g_meta, attn_norm_w=g_attn_norm, w_in=g_w_in, gdn_conv_w=g_conv, gdn_a_log=g_alog[:, :GDN_H],
        gdn_dt_bias=g_dtb[:, :GDN_H], gdn_norm_w=g_gdn_norm, gla_gate_w2=g_w2p[SM_LR:SM_LR + GATE_RANK],
        gla_gate_b=g_gate_b, gla_norm_w=g_gla_norm, w_out=g_out, ffn_norm_w=g_ffn_norm, w_gate=g_gate, w_up=g_up,
        w_down=g_down, final_norm_w=g_final)
    return lossp[0, 0], grad_x, grads


_WEIGHTS = ("meta_tokens", "attn_norm_w", "w_in", "gdn_conv_w", "gdn_a_log", "gdn_dt_bias", "gdn_norm_w",
            "gla_gate_w2", "gla_gate_b", "gla_norm_w", "w_out", "ffn_norm_w", "w_gate", "w_up", "w_down",
            "final_norm_w")
_BIG_COLS = ("w_in", "w_gate", "w_up")
_BIG_ROWS = ("w_out", "w_down")
_SMALL_SHARDED = ("meta_tokens", "gdn_conv_w", "gla_gate_w2")


def _cat_cols(g):
    return jnp.concatenate([g[i] for i in range(N_CHIP)], axis=-1)


def kernel(x, meta_tokens, attn_norm_w, w_in, gdn_conv_w, gdn_a_log, gdn_dt_bias, gdn_norm_w, gla_gate_w2, gla_gate_b, gla_norm_w, w_out, ffn_norm_w, w_gate, w_up, w_down, final_norm_w, loss_target, m_meta_tokens, m_attn_norm_w, m_w_in, m_gdn_conv_w, m_gdn_a_log, m_gdn_dt_bias, m_gdn_norm_w, m_gla_gate_w2, m_gla_gate_b, m_gla_norm_w, m_w_out, m_ffn_norm_w, m_w_gate, m_w_up, m_w_down, m_final_norm_w, v_meta_tokens, v_attn_norm_w, v_w_in, v_gdn_conv_w, v_gdn_a_log, v_gdn_dt_bias, v_gdn_norm_w, v_gla_gate_w2, v_gla_gate_b, v_gla_norm_w, v_w_out, v_ffn_norm_w, v_w_gate, v_w_up, v_w_down, v_final_norm_w):
    w = dict(meta_tokens=meta_tokens, attn_norm_w=attn_norm_w, w_in=w_in, gdn_conv_w=gdn_conv_w, gdn_a_log=gdn_a_log,
             gdn_dt_bias=gdn_dt_bias, gdn_norm_w=gdn_norm_w, gla_gate_w2=gla_gate_w2, gla_gate_b=gla_gate_b,
             gla_norm_w=gla_norm_w, w_out=w_out, ffn_norm_w=ffn_norm_w, w_gate=w_gate, w_up=w_up, w_down=w_down,
             final_norm_w=final_norm_w)
    m = dict(meta_tokens=m_meta_tokens, attn_norm_w=m_attn_norm_w, w_in=m_w_in, gdn_conv_w=m_gdn_conv_w,
             gdn_a_log=m_gdn_a_log, gdn_dt_bias=m_gdn_dt_bias, gdn_norm_w=m_gdn_norm_w, gla_gate_w2=m_gla_gate_w2,
             gla_gate_b=m_gla_gate_b, gla_norm_w=m_gla_norm_w, w_out=m_w_out, ffn_norm_w=m_ffn_norm_w,
             w_gate=m_w_gate, w_up=m_w_up, w_down=m_w_down, final_norm_w=m_final_norm_w)
    v = dict(meta_tokens=v_meta_tokens, attn_norm_w=v_attn_norm_w, w_in=v_w_in, gdn_conv_w=v_gdn_conv_w,
             gdn_a_log=v_gdn_a_log, gdn_dt_bias=v_gdn_dt_bias, gdn_norm_w=v_gdn_norm_w, gla_gate_w2=v_gla_gate_w2,
             gla_gate_b=v_gla_gate_b, gla_norm_w=v_gla_norm_w, w_out=v_w_out, ffn_norm_w=v_ffn_norm_w,
             w_gate=v_w_gate, w_up=v_w_up, w_down=v_w_down, final_norm_w=v_final_norm_w)
    D = x.shape[2]
    chip = 2 * lax.axis_index("x") + lax.axis_index("y")

    def two_d(name, a):
        return a.reshape(1, -1) if a.ndim == 1 else a.reshape(-1, a.shape[-1])

    w2d = {k: two_d(k, a) for k, a in w.items()}

    names = _BIG_COLS + _BIG_ROWS + _SMALL_SHARDED
    gathered = _allgather_chips([w2d[k].astype(bf16) if k in _BIG_COLS + _BIG_ROWS else w2d[k] for k in names])
    full = dict(zip(names, gathered))
    for k in _BIG_COLS + _SMALL_SHARDED:
        full[k] = _cat_cols(full[k])
    for k in _BIG_ROWS:
        full[k] = full[k].reshape(-1, full[k].shape[-1])

    lossp, grad_x, g = _local_step(
        x, loss_target, full["meta_tokens"], w2d["attn_norm_w"], full["w_in"], full["gdn_conv_w"], w2d["gdn_a_log"],
        w2d["gdn_dt_bias"], w2d["gdn_norm_w"], full["gla_gate_w2"], w2d["gla_gate_b"], w2d["gla_norm_w"],
        full["w_out"], w2d["ffn_norm_w"], full["w_gate"], full["w_up"], full["w_down"], w2d["final_norm_w"])
    loss = lax.psum(lossp, ("x", "y", "c"))

    def slabs(k):
        a = g[k]
        if k in _BIG_COLS:
            R, C = a.shape
            return a.reshape(2, R // 2, N_CHIP, C // N_CHIP).transpose(2, 0, 1, 3).reshape(N_DEV, R // 2, C // N_CHIP)
        return a.reshape(N_DEV, a.shape[0] // N_DEV, a.shape[1])

    big = _BIG_COLS + _BIG_ROWS
    small = tuple(k for k in _WEIGHTS if k not in big)
    sizes = [g[k].size for k in small]
    total = sum(sizes)
    rows = -(-total // 1024)
    rows += (-rows) % 8
    packed = jnp.concatenate([g[k].reshape(-1) for k in small] + [jnp.zeros((rows * 1024 - total,), f32)])
    recv = _exchange_slots([slabs(k) for k in big], [packed.reshape(rows, 1024)])
    halves = _exchange_halves([_sum_slots(r, "sum_" + k) for k, r in zip(big, recv[:len(big)])])
    red = {k: h.reshape(w2d[k].shape) for k, h in zip(big, halves)}
    psum_small = _sum_slots(recv[-1], "sum_small").reshape(-1)
    off = 0
    for k, n in zip(small, sizes):
        a = psum_small[off:off + n].reshape(g[k].shape)
        off += n
        if k in _SMALL_SHARDED:
            c = w2d[k].shape[1]
            a = lax.dynamic_slice_in_dim(a, chip * c, c, axis=1)
        red[k] = a

    grads, deltas, new_m, new_v = [], [], [], []
    for k in _WEIGHTS:
        d, m2, v2 = _adamw(red[k], w2d[k], two_d(k, m[k]), two_d(k, v[k]), "adamw_" + k)
        shape = w[k].shape
        grads.append(red[k].reshape(shape))
        deltas.append(d.reshape(shape))
        new_m.append(m2.reshape(shape))
        new_v.append(v2.reshape(shape))
    return (loss, grad_x, *grads, *deltas, *new_m, *new_v)
```

```python
import jax
import jax.numpy as jnp
from jax import lax
from jax.experimental import pallas as pl
from jax.experimental.pallas import tpu as pltpu

f32 = jnp.float32
bf16 = jnp.bfloat16
HI = lax.Precision.HIGHEST
HIGH = lax.Precision.HIGH
MESH = pl.DeviceIdType.MESH

N_META = 16
CONV_K = 4
GDN_H, GDN_DK, GDN_DV, GDN_C = 8, 128, 128, 64
GLA_H, GLA_DK, GLA_DV, GLA_C = 4, 128, 256, 16
GATE_RANK = 16
GATE_NORMALIZER = 16.0
EPS = 1e-6
GDN_QK = GDN_H * GDN_DK
GDN_V = GDN_H * GDN_DV
GLA_QK = GLA_H * GLA_DK
GLA_V = GLA_H * GLA_DV
PAD = (-N_META) % GDN_C
OFF = PAD + N_META
ROWS = 64

R_QKV, R_Z, R_AB, R_G, R_GR, R_LR, R_END = 0, 3072, 4096, 4112, 6160, 7184, 7200
C_ZR, C_G, C_QKV, C_END = 0, 2048, 4096, 7168
ZR_W = GDN_V + GLA_V
G_W = 2 * GLA_QK + GLA_V
QKV_W = 2 * GDN_QK + GDN_V
Q0, K0, V0 = 0, GDN_QK, 2 * GDN_QK
SM_W = 128
SM_LR = 2 * GDN_H

ADAM_LR, ADAM_B1, ADAM_B2, ADAM_EPS, ADAM_WD, ADAM_STEP = 0.001, 0.9, 0.999, 1e-08, 0.01, 10

VMEM_LIMIT_V7X = 56 * 1024 * 1024
N_DEV = 8
N_CHIP = 4


def _params(*sem):
    return pltpu.CompilerParams(dimension_semantics=sem, vmem_limit_bytes=VMEM_LIMIT_V7X)


def _tile(n, cap, mult=16):
    best = None
    for d in range(mult, min(n, cap) + 1, mult):
        if n % d == 0:
            best = d
    assert best is not None, (n, cap, mult)
    return best


NN = ((1,), (0,))
NT = ((1,), (1,))
TN = ((0,), (0,))


def _dot(a, b, dims, prec=None):
    return lax.dot_general(a, b, (dims, ((), ())), precision=prec, preferred_element_type=f32)


def _mmb(a, b, dims):
    return _dot(a.astype(bf16), b.astype(bf16), dims)


def _sigmoid(x):
    return jax.nn.sigmoid(x)


def _silu(x):
    return x * _sigmoid(x)


def _dsilu(x):
    s = _sigmoid(x)
    return s * (1.0 + x * (1.0 - s))


def _log1p_exp_neg_abs(x):
    t = jnp.exp(-jnp.abs(x))
    u = 1.0 + t
    d = u - 1.0
    return jnp.where(d == 0.0, t, jnp.log(u) * (t / jnp.where(d == 0.0, 1.0, d)))


def _softplus(x):
    return jnp.maximum(x, 0.0) + _log1p_exp_neg_abs(x)


def _log_sigmoid(x):
    return jnp.minimum(x, 0.0) - _log1p_exp_neg_abs(x)


def _rms(x):
    r = lax.rsqrt(jnp.mean(x * x, axis=-1, keepdims=True) + EPS)
    return x * r, r


def _rms_bwd(dy, xh, r, w):
    t = dy * w
    return r * (t - xh * jnp.mean(t * xh, axis=-1, keepdims=True))


def _l2n(x):
    return x * lax.rsqrt(jnp.sum(x * x, axis=-1, keepdims=True) + EPS)


INV_LEAF = 8


def _same_block(C, b):
    sh = b.bit_length() - 1
    row = lax.broadcasted_iota(jnp.int32, (C, C), 0)
    col = lax.broadcasted_iota(jnp.int32, (C, C), 1)
    return lax.shift_right_logical(row, sh) == lax.shift_right_logical(col, sh)


def _tri_inv_impl(As):
    C = As[0].shape[0]
    R = range(len(As))
    row = lax.broadcasted_iota(jnp.int32, (C, C), 0)
    col = lax.broadcasted_iota(jnp.int32, (C, C), 1)
    eye = (row == col).astype(f32)
    b = INV_LEAF
    inner = _same_block(C, b)
    leaf = [jnp.where(inner, As[h], 0.0) for h in R]
    d = [eye - leaf[h] for h in R]
    pw = leaf
    n = 2
    while n < b:
        pw = [_dot(pw[h], pw[h], NN, HIGH) for h in R]
        d = [_dot(d[h], eye + pw[h], NN, HIGH) for h in R]
        n *= 2
    while b < C:
        outer = _same_block(C, 2 * b)
        level = jnp.logical_and(outer, jnp.logical_not(inner))
        ed = [_dot(jnp.where(level, As[h], 0.0), d[h], NN, HIGH) for h in R]
        d = [d[h] - _dot(d[h], ed[h], NN, HIGH) for h in R]
        inner = outer
        b *= 2
    return d


@jax.custom_vjp
def _tri_inv(As):
    return _tri_inv_impl(As)


def _tri_inv_fwd(As):
    d = _tri_inv_impl(As)
    return d, d


def _tri_inv_bwd(d, g):
    R = range(len(d))
    t = [_dot(d[h], g[h], TN, HIGH) for h in R]
    return ([-_dot(t[h], d[h], NT, HIGH) for h in R],)


_tri_inv.defvjp(_tri_inv_fwd, _tri_inv_bwd)


def _gdn_chunk(Ss, qrs, krs, vs, betas, gs):
    C, dk = qrs[0].shape
    R = range(len(Ss))
    row = lax.broadcasted_iota(jnp.int32, (C, C), 0)
    col = lax.broadcasted_iota(jnp.int32, (C, C), 1)
    causal = row >= col
    strict = row > col
    cf = causal.astype(f32)
    q = [_l2n(qrs[h]) * (dk ** -0.5) for h in R]
    k = [_l2n(krs[h]) for h in R]
    mc = [_dot(cf, jnp.broadcast_to(gs[h], (C, C)), NN, HI) for h in R]
    gc = [mc[h][:, 0:1] for h in R]
    decay = [jnp.where(causal, jnp.exp(jnp.where(causal, mc[h] - mc[h].T, 0.0)), 0.0) for h in R]
    kb = [k[h] * betas[h] for h in R]
    a = [jnp.where(strict, _mmb(kb[h], k[h], NT) * decay[h], 0.0) for h in R]
    p = _tri_inv(a)
    egc = [jnp.exp(gc[h]) for h in R]
    u = [_mmb(p[h], vs[h] * betas[h], NN) for h in R]
    w = [_mmb(p[h], kb[h] * egc[h], NN) for h in R]
    qk = [jnp.where(causal, _mmb(q[h], k[h], NT) * decay[h], 0.0) for h in R]
    v_new = [u[h] - _mmb(w[h], Ss[h], NN) for h in R]
    o = [_mmb(q[h] * egc[h], Ss[h], NN) + _mmb(qk[h], v_new[h], NN) for h in R]
    gl = [gc[h][C - 1:C, :] for h in R]
    kd = [k[h] * jnp.exp(gl[h] - gc[h]) for h in R]
    S2 = [Ss[h] * jnp.exp(gl[h]) + _mmb(kd[h], v_new[h], TN) for h in R]
    return S2, o


def _gla_chunk(Sts, qrs, ks, vs, las):
    C, dk = qrs[0].shape
    R = range(len(Sts))
    row = lax.broadcasted_iota(jnp.int32, (C, C), 0)
    col = lax.broadcasted_iota(jnp.int32, (C, C), 1)
    cf = (row >= col).astype(f32)
    q = [qrs[h] * (dk ** -0.5) for h in R]
    b = [_dot(cf, las[h], NN, HI) for h in R]
    o = [_mmb(q[h] * jnp.exp(b[h]), Sts[h], NT) for h in R]
    ri = lax.broadcasted_iota(jnp.int32, (C, dk), 0)
    for j in range(C):
        m = ri >= j
        e = [jnp.where(m, jnp.exp(jnp.where(m, b[h] - b[h][j:j + 1, :], 0.0)), 0.0) for h in R]
        s = [jnp.sum(q[h] * ks[h][j:j + 1, :] * e[h], axis=-1, keepdims=True) for h in R]
        o = [o[h] + s[h] * vs[h][j:j + 1, :] for h in R]
    bl = [b[h][C - 1:C, :] for h in R]
    St2 = [Sts[h] * jnp.exp(bl[h]) + _mmb(vs[h], ks[h] * jnp.exp(bl[h] - b[h]), TN) for h in R]
    return St2, o


def _mm(a, b, mode, name, *, tm_cap=1408, tn_cap=1024, tk_cap=2048, out_dtype=f32, acc_in=None):
    if mode == "nn":
        (M, K), (K2, N) = a.shape, b.shape
    elif mode == "nt":
        (M, K), (N, K2) = a.shape, b.shape
    else:
        (K, M), (K2, N) = a.shape, b.shape
    assert K == K2, (name, a.shape, b.shape)
    tm = _tile(M, tm_cap)
    tn = _tile(N, tn_cap, 128)
    tk = _tile(K, tk_cap, 128 if K % 128 == 0 else 16)
    nk = K // tk
    dims = {"nn": NN, "nt": NT, "tn": TN}[mode]
    use_scratch = nk > 1 and out_dtype != f32

    def body(*refs):
        if acc_in is not None:
            a_ref, b_ref, c_ref, o_ref, *scr = refs
        else:
            a_ref, b_ref, o_ref, *scr = refs
            c_ref = None
        p = _mmb(a_ref[...], b_ref[...], dims)
        if nk == 1:
            if c_ref is not None:
                p = p + c_ref[...]
            o_ref[...] = p.astype(out_dtype)
            return
        k = pl.program_id(2)
        acc = scr[0] if use_scratch else o_ref

        @pl.when(k == 0)
        def _():
            acc[...] = p if c_ref is None else p + c_ref[...]

        @pl.when(k > 0)
        def _():
            acc[...] += p

        if use_scratch:
            @pl.when(k == nk - 1)
            def _():
                o_ref[...] = acc[...].astype(out_dtype)

    if mode == "tn":
        a_spec = pl.BlockSpec((tk, tm), lambda i, j, k: (k, i))
    else:
        a_spec = pl.BlockSpec((tm, tk), lambda i, j, k: (i, k))
    if mode == "nt":
        b_spec = pl.BlockSpec((tn, tk), lambda i, j, k: (j, k))
    else:
        b_spec = pl.BlockSpec((tk, tn), lambda i, j, k: (k, j))
    o_spec = pl.BlockSpec((tm, tn), lambda i, j, k: (i, j))
    in_specs = [a_spec, b_spec]
    args = [a, b]
    if acc_in is not None:
        in_specs.append(o_spec)
        args.append(acc_in)
    return pl.pallas_call(
        body, name=name, grid=(M // tm, N // tn, nk), in_specs=in_specs, out_specs=o_spec,
        out_shape=jax.ShapeDtypeStruct((M, N), out_dtype),
        scratch_shapes=[pltpu.VMEM((tm, tn), f32)] if use_scratch else [],
        compiler_params=_params("parallel", "parallel", "arbitrary"),
    )(*args)


def _embed_norm(x3, m64, w):
    _, S, D = x3.shape
    Lp = OFF + S

    def body(x_ref, m_ref, w_ref, h_ref, n_ref):
        i = pl.program_id(0)
        h = jnp.where(i == 0, m_ref[...], x_ref[...])
        h_ref[...] = h
        xh, _ = _rms(h)
        n_ref[...] = (xh * w_ref[...]).astype(bf16)

    row = pl.BlockSpec((ROWS, D), lambda i: (i, 0))
    return pl.pallas_call(
        body, name="embed_norm", grid=(Lp // ROWS,),
        in_specs=[pl.BlockSpec((None, ROWS, D), lambda i: (0, jnp.maximum(i - 1, 0), 0)),
                  pl.BlockSpec((ROWS, D), lambda i: (0, 0)),
                  pl.BlockSpec((1, D), lambda i: (0, 0))],
        out_specs=[row, row],
        out_shape=[jax.ShapeDtypeStruct((Lp, D), f32), jax.ShapeDtypeStruct((Lp, D), bf16)],
        compiler_params=_params("parallel"),
    )(x3, m64, w)


def _add_norm(h, d, w):
    Lp, D = h.shape
    tr = _tile(Lp, 256)

    def body(h_ref, d_ref, w_ref, o_ref, n_ref):
        h1 = h_ref[...] + d_ref[...]
        o_ref[...] = h1
        xh, _ = _rms(h1)
        n_ref[...] = (xh * w_ref[...]).astype(bf16)

    row = pl.BlockSpec((tr, D), lambda i: (i, 0))
    return pl.pallas_call(
        body, name="add_norm", grid=(Lp // tr,),
        in_specs=[row, row, pl.BlockSpec((1, D), lambda i: (0, 0))], out_specs=[row, row],
        out_shape=[jax.ShapeDtypeStruct((Lp, D), f32), jax.ShapeDtypeStruct((Lp, D), bf16)],
        compiler_params=_params("parallel"),
    )(h, d, w)


def _norm_bwd(dn, h, dh, w):
    Lp, D = h.shape
    tr = _tile(Lp, 256)

    def body(dn_ref, h_ref, dh_ref, w_ref, o_ref, ob_ref, gw_ref):
        i = pl.program_id(0)
        xh, r = _rms(h_ref[...])
        dn_ = dn_ref[...]
        o = dh_ref[...] + _rms_bwd(dn_, xh, r, w_ref[...])
        o_ref[...] = o
        ob_ref[...] = o.astype(bf16)
        gw = jnp.sum(dn_ * xh, axis=0, keepdims=True)

        @pl.when(i == 0)
        def _():
            gw_ref[...] = gw

        @pl.when(i > 0)
        def _():
            gw_ref[...] += gw

    row = pl.BlockSpec((tr, D), lambda i: (i, 0))
    vec = pl.BlockSpec((1, D), lambda i: (0, 0))
    return pl.pallas_call(
        body, name="norm_bwd", grid=(Lp // tr,), in_specs=[row, row, row, vec], out_specs=[row, row, vec],
        out_shape=[jax.ShapeDtypeStruct((Lp, D), f32), jax.ShapeDtypeStruct((Lp, D), bf16),
                   jax.ShapeDtypeStruct((1, D), f32)],
        compiler_params=_params("arbitrary"),
    )(dn, h, dh, w)


def _embed_norm_bwd(dn, h, dh, w, S):
    Lp, D = h.shape

    def body(dn_ref, h_ref, dh_ref, w_ref, gx_ref, gm_ref, gw_ref):
        i = pl.program_id(0)
        xh, r = _rms(h_ref[...])
        dn_ = dn_ref[...]
        d0 = dh_ref[...] + _rms_bwd(dn_, xh, r, w_ref[...])
        gx_ref[...] = d0
        gw = jnp.sum(dn_ * xh, axis=0, keepdims=True)

        @pl.when(i == 0)
        def _():
            gm_ref[...] = d0[PAD:OFF, :]
            gw_ref[...] = gw

        @pl.when(i > 0)
        def _():
            gw_ref[...] += gw

    row = pl.BlockSpec((ROWS, D), lambda i: (i, 0))
    vec = pl.BlockSpec((1, D), lambda i: (0, 0))
    return pl.pallas_call(
        body, name="embed_norm_bwd", grid=(Lp // ROWS,), in_specs=[row, row, row, vec],
        out_specs=[pl.BlockSpec((None, ROWS, D), lambda i: (0, jnp.maximum(i - 1, 0), 0)),
                   pl.BlockSpec((N_META, D), lambda i: (0, 0)), vec],
        out_shape=[jax.ShapeDtypeStruct((1, S, D), f32), jax.ShapeDtypeStruct((N_META, D), f32),
                   jax.ShapeDtypeStruct((1, D), f32)],
        compiler_params=_params("arbitrary"),
    )(dn, h, dh, w)


def _final(h1, ffn, tgt3, w):
    Lp, D = h1.shape

    def body(h_ref, f_ref, t_ref, w_ref, d_ref, db_ref, l_ref, gw_ref):
        i = pl.program_id(0)
        h2 = h_ref[...] + f_ref[...]
        xh, r = _rms(h2)
        w_ = w_ref[...]
        e = xh * w_ - t_ref[...]
        valid = (i > 0).astype(f32)
        loss = 0.5 * jnp.sum(jnp.mean(e * e, axis=-1, keepdims=True), axis=0, keepdims=True) * valid
        dy = e * (valid / D)
        d = _rms_bwd(dy, xh, r, w_)
        d_ref[...] = d
        db_ref[...] = d.astype(bf16)
        gw = jnp.sum(dy * xh, axis=0, keepdims=True)

        @pl.when(i == 0)
        def _():
            l_ref[...] = jnp.zeros_like(l_ref)
            gw_ref[...] = jnp.zeros_like(gw_ref)

        l_ref[...] += jnp.broadcast_to(loss, l_ref.shape)
        gw_ref[...] += gw

    row = pl.BlockSpec((ROWS, D), lambda i: (i, 0))
    vec = pl.BlockSpec((1, D), lambda i: (0, 0))
    return pl.pallas_call(
        body, name="final_loss", grid=(Lp // ROWS,),
        in_specs=[row, row, pl.BlockSpec((None, ROWS, D), lambda i: (0, jnp.maximum(i - 1, 0), 0)), vec],
        out_specs=[row, row, pl.BlockSpec((8, 128), lambda i: (0, 0)), vec],
        out_shape=[jax.ShapeDtypeStruct((Lp, D), f32), jax.ShapeDtypeStruct((Lp, D), bf16),
                   jax.ShapeDtypeStruct((8, 128), f32), jax.ShapeDtypeStruct((1, D), f32)],
        compiler_params=_params("arbitrary"),
    )(h1, ffn, tgt3, w)


def _swiglu(gu):
    Lp, F2 = gu.shape
    F = F2 // 2
    tr = _tile(Lp, 256)
    tc = _tile(F, 512, 128)
    nc = F // tc

    def body(g_ref, u_ref, o_ref):
        o_ref[...] = (_silu(g_ref[...]) * u_ref[...]).astype(bf16)

    return pl.pallas_call(
        body, name="swiglu", grid=(Lp // tr, nc),
        in_specs=[pl.BlockSpec((tr, tc), lambda i, j: (i, j)), pl.BlockSpec((tr, tc), lambda i, j: (i, j + nc))],
        out_specs=pl.BlockSpec((tr, tc), lambda i, j: (i, j)),
        out_shape=jax.ShapeDtypeStruct((Lp, F), bf16),
        compiler_params=_params("parallel", "parallel"),
    )(gu, gu)


def _swiglu_bwd(gu, dact):
    Lp, F2 = gu.shape
    F = F2 // 2
    tr = _tile(Lp, 256)
    tc = _tile(F, 512, 128)
    nc = F // tc

    def body(g_ref, u_ref, d_ref, dg_ref, du_ref):
        g = g_ref[...]
        d = d_ref[...]
        dg_ref[...] = (d * u_ref[...] * _dsilu(g)).astype(bf16)
        du_ref[...] = (d * _silu(g)).astype(bf16)

    lo = pl.BlockSpec((tr, tc), lambda i, j: (i, j))
    hi = pl.BlockSpec((tr, tc), lambda i, j: (i, j + nc))
    dg, du = pl.pallas_call(
        body, name="swiglu_bwd", grid=(Lp // tr, nc), in_specs=[lo, hi, lo], out_specs=[lo, lo],
        out_shape=[jax.ShapeDtypeStruct((Lp, F), bf16)] * 2,
        compiler_params=_params("parallel", "parallel"),
    )(gu, gu, dact)
    return dg, du


def _gates(psm, w2p, gate_b, alog, dtb):
    Lp = psm.shape[0]
    tr = _tile(Lp, 256)

    def body(p_ref, w_ref, b_ref, a_ref, t_ref, gb_ref, la_ref):
        i = pl.program_id(0)
        psm_ = p_ref[...]
        lane = lax.broadcasted_iota(jnp.int32, psm_.shape, 1)
        rowi = lax.broadcasted_iota(jnp.int32, (tr, 1), 0) + i * tr
        g = -jnp.exp(a_ref[...]) * _softplus(psm_ + t_ref[...])
        beta = _sigmoid(psm_)
        gb = jnp.where(lane < GDN_H, g, jnp.where(lane < 2 * GDN_H, beta, 0.0))
        gb_ref[...] = gb * (rowi >= PAD).astype(f32)
        logit = _mmb(psm_, w_ref[...], NN) + b_ref[...]
        la_ref[...] = _log_sigmoid(logit) * (1.0 / GATE_NORMALIZER)

    row = pl.BlockSpec((tr, SM_W), lambda i: (i, 0))
    return pl.pallas_call(
        body, name="gates", grid=(Lp // tr,),
        in_specs=[row, pl.BlockSpec((SM_W, GLA_QK), lambda i: (0, 0)), pl.BlockSpec((1, GLA_QK), lambda i: (0, 0)),
                  pl.BlockSpec((1, SM_W), lambda i: (0, 0)), pl.BlockSpec((1, SM_W), lambda i: (0, 0))],
        out_specs=[row, pl.BlockSpec((tr, GLA_QK), lambda i: (i, 0))],
        out_shape=[jax.ShapeDtypeStruct((Lp, SM_W), f32), jax.ShapeDtypeStruct((Lp, GLA_QK), f32)],
        compiler_params=_params("parallel"),
    )(psm, w2p, gate_b, alog, dtb)


def _gates_bwd(psm, w2p, gate_b, alog, dtb, dgb, dla):
    Lp = psm.shape[0]
    tr = _tile(Lp, 256)

    def body(p_ref, w_ref, b_ref, a_ref, t_ref, dgb_ref, dla_ref, dp_ref, gw_ref, gb_ref, ga_ref, gt_ref):
        i = pl.program_id(0)
        psm_ = p_ref[...]
        lane = lax.broadcasted_iota(jnp.int32, psm_.shape, 1)
        rowi = lax.broadcasted_iota(jnp.int32, (tr, 1), 0) + i * tr
        d = dgb_ref[...] * (rowi >= PAD).astype(f32)
        ea = jnp.exp(a_ref[...])
        z = psm_ + t_ref[...]
        is_g = lane < GDN_H
        dz = jnp.where(is_g, -ea * _sigmoid(z) * d, 0.0)
        dalog = jnp.where(is_g, -ea * _softplus(z) * d, 0.0)
        beta = _sigmoid(psm_)
        dbeta = jnp.where(jnp.logical_and(lane >= GDN_H, lane < 2 * GDN_H), beta * (1.0 - beta) * d, 0.0)
        logit = _mmb(psm_, w_ref[...], NN) + b_ref[...]
        dlogit = dla_ref[...] * (_sigmoid(-logit) * (1.0 / GATE_NORMALIZER))
        dlr = _mmb(dlogit, w_ref[...], NT)
        dp_ref[...] = (dz + dbeta + dlr).astype(bf16)
        gw = _mmb(psm_, dlogit, TN)
        gb = jnp.sum(dlogit, axis=0, keepdims=True)
        ga = jnp.sum(dalog, axis=0, keepdims=True)
        gt = jnp.sum(dz, axis=0, keepdims=True)

        @pl.when(i == 0)
        def _():
            gw_ref[...] = gw
            gb_ref[...] = gb
            ga_ref[...] = ga
            gt_ref[...] = gt

        @pl.when(i > 0)
        def _():
            gw_ref[...] += gw
            gb_ref[...] += gb
            ga_ref[...] += ga
            gt_ref[...] += gt

    row = pl.BlockSpec((tr, SM_W), lambda i: (i, 0))
    wsp = pl.BlockSpec((SM_W, GLA_QK), lambda i: (0, 0))
    bsp = pl.BlockSpec((1, GLA_QK), lambda i: (0, 0))
    vsp = pl.BlockSpec((1, SM_W), lambda i: (0, 0))
    return pl.pallas_call(
        body, name="gates_bwd", grid=(Lp // tr,),
        in_specs=[row, wsp, bsp, vsp, vsp, row, pl.BlockSpec((tr, GLA_QK), lambda i: (i, 0))],
        out_specs=[row, wsp, bsp, vsp, vsp],
        out_shape=[jax.ShapeDtypeStruct((Lp, SM_W), bf16), jax.ShapeDtypeStruct((SM_W, GLA_QK), f32),
                   jax.ShapeDtypeStruct((1, GLA_QK), f32), jax.ShapeDtypeStruct((1, SM_W), f32),
                   jax.ShapeDtypeStruct((1, SM_W), f32)],
        compiler_params=_params("arbitrary"),
    )(psm, w2p, gate_b, alog, dtb, dgb, dla)


def _conv_pre(x_ext, w, n):
    rows = x_ext.shape[0]
    y = x_ext * w[CONV_K - 1:CONV_K, :]
    for s in range(1, CONV_K):
        y = y + pltpu.roll(x_ext, s, 0) * w[CONV_K - 1 - s:CONV_K - s, :]
    return y[rows - n:, :]


def _conv(proj, cw):
    Lp = proj.shape[0]
    W = cw.shape[1]
    tr = _tile(Lp, 256, 64)
    tc = _tile(W, 512, 128)
    c0 = C_QKV // tc

    def body(h_ref, x_ref, w_ref, o_ref):
        i = pl.program_id(1)
        halo = jnp.where(i == 0, 0.0, h_ref[...])
        x_ext = jnp.concatenate([halo, x_ref[...]], axis=0)
        o_ref[...] = _silu(_conv_pre(x_ext, w_ref[...], tr))

    return pl.pallas_call(
        body, name="conv", grid=(W // tc, Lp // tr),
        in_specs=[pl.BlockSpec((8, tc), lambda j, i: (jnp.maximum(i * (tr // 8) - 1, 0), j + c0)),
                  pl.BlockSpec((tr, tc), lambda j, i: (i, j + c0)),
                  pl.BlockSpec((CONV_K, tc), lambda j, i: (0, j))],
        out_specs=pl.BlockSpec((tr, tc), lambda j, i: (i, j)),
        out_shape=jax.ShapeDtypeStruct((Lp, W), f32),
        compiler_params=_params("parallel", "parallel"),
    )(proj, proj, cw)


def _conv_bwd(proj, cw, dy, dproj):
    Lp = proj.shape[0]
    W = cw.shape[1]
    tr = _tile(Lp, 256, 64)
    tc = _tile(W, 512, 128)
    c0 = C_QKV // tc
    nr = Lp // tr
    last8 = Lp // 8 - 1

    def body(xp_ref, x_ref, xn_ref, w_ref, d_ref, dn_ref, dproj_ref, o_ref, gw_ref):
        del dproj_ref
        i = pl.program_id(1)
        w = w_ref[...]
        xp = jnp.where(i == 0, 0.0, xp_ref[...])
        x_ext = jnp.concatenate([xp, x_ref[...], xn_ref[...]], axis=0)
        n = tr + 8
        pre = _conv_pre(x_ext, w, n)
        dn = jnp.where(i == nr - 1, 0.0, dn_ref[...])
        dpre = jnp.concatenate([d_ref[...], dn], axis=0) * _dsilu(pre)
        dx = dpre * w[CONV_K - 1:CONV_K, :]
        for s in range(1, CONV_K):
            dx = dx + pltpu.roll(dpre, n - s, 0) * w[CONV_K - 1 - s:CONV_K - s, :]
        o_ref[...] = dx[:tr, :].astype(bf16)
        dp = dpre[:tr, :]
        rows = []
        for k in range(CONV_K):
            xs = x_ext if k == CONV_K - 1 else pltpu.roll(x_ext, CONV_K - 1 - k, 0)
            rows.append(jnp.sum(dp * xs[8:8 + tr, :], axis=0, keepdims=True))
        gw = jnp.concatenate(rows, axis=0)

        @pl.when(i == 0)
        def _():
            gw_ref[...] = gw

        @pl.when(i > 0)
        def _():
            gw_ref[...] += gw

    cur = pl.BlockSpec((tr, tc), lambda j, i: (i, j))
    nxt = pl.BlockSpec((8, tc), lambda j, i: (jnp.minimum((i + 1) * (tr // 8), last8), j))
    pcur = pl.BlockSpec((tr, tc), lambda j, i: (i, j + c0))
    pprev = pl.BlockSpec((8, tc), lambda j, i: (jnp.maximum(i * (tr // 8) - 1, 0), j + c0))
    pnext = pl.BlockSpec((8, tc), lambda j, i: (jnp.minimum((i + 1) * (tr // 8), last8), j + c0))
    wsp = pl.BlockSpec((CONV_K, tc), lambda j, i: (0, j))
    return pl.pallas_call(
        body, name="conv_bwd", grid=(W // tc, nr),
        in_specs=[pprev, pcur, pnext, wsp, cur, nxt, pl.BlockSpec(memory_space=pl.ANY)],
        out_specs=[pcur, wsp],
        out_shape=[jax.ShapeDtypeStruct(dproj.shape, dproj.dtype), jax.ShapeDtypeStruct((CONV_K, W), f32)],
        input_output_aliases={6: 0},
        compiler_params=_params("parallel", "arbitrary"),
    )(proj, proj, proj, cw, dy, dy, dproj)


def _gdn_heads(x_ref, gbv):
    R = range(GDN_H)
    return ([x_ref[:, Q0 + h * GDN_DK:Q0 + (h + 1) * GDN_DK] for h in R],
            [x_ref[:, K0 + h * GDN_DK:K0 + (h + 1) * GDN_DK] for h in R],
            [x_ref[:, V0 + h * GDN_DV:V0 + (h + 1) * GDN_DV] for h in R],
            [gbv[:, GDN_H + h:GDN_H + h + 1] for h in R],
            [gbv[:, h:h + 1] for h in R])


def _gdn_fwd(qkvc, gb):
    Lp = qkvc.shape[0]
    N = Lp // GDN_C

    def body(x_ref, gb_ref, o_ref, sall_ref, s_scr):
        @pl.when(pl.program_id(0) == 0)
        def _():
            s_scr[...] = jnp.zeros_like(s_scr)

        R = range(GDN_H)
        Ss = [s_scr[h] for h in R]
        for h in R:
            sall_ref[0, h] = Ss[h]
        S2, o = _gdn_chunk(Ss, *_gdn_heads(x_ref, gb_ref[...]))
        for h in R:
            s_scr[h] = S2[h]
            o_ref[:, h * GDN_DV:(h + 1) * GDN_DV] = o[h]

    return pl.pallas_call(
        body, name="gdn_fwd", grid=(N,),
        in_specs=[pl.BlockSpec((GDN_C, QKV_W), lambda n: (n, 0)), pl.BlockSpec((GDN_C, SM_W), lambda n: (n, 0))],
        out_specs=[pl.BlockSpec((GDN_C, GDN_V), lambda n: (n, 0)),
                   pl.BlockSpec((1, GDN_H, GDN_DK, GDN_DV), lambda n: (n, 0, 0, 0))],
        out_shape=[jax.ShapeDtypeStruct((Lp, GDN_V), f32), jax.ShapeDtypeStruct((N, GDN_H, GDN_DK, GDN_DV), f32)],
        scratch_shapes=[pltpu.VMEM((GDN_H, GDN_DK, GDN_DV), f32)],
        compiler_params=_params("arbitrary"),
    )(qkvc, gb)


def _gdn_bwd(qkvc, gb, sall, do):
    Lp = qkvc.shape[0]
    N = Lp // GDN_C

    def body(x_ref, gb_ref, sall_ref, do_ref, dx_ref, dgb_ref, ds_scr):
        @pl.when(pl.program_id(0) == 0)
        def _():
            ds_scr[...] = jnp.zeros_like(ds_scr)

        R = range(GDN_H)
        lane = lax.broadcasted_iota(jnp.int32, (GDN_C, SM_W), 1)
        _, vjp = jax.vjp(_gdn_chunk, [sall_ref[0, h] for h in R], *_gdn_heads(x_ref, gb_ref[...]))
        dS, dq, dk, dv, dbeta, dg = vjp(([ds_scr[h] for h in R],
                                         [do_ref[:, h * GDN_DV:(h + 1) * GDN_DV] for h in R]))
        acc = jnp.zeros((GDN_C, SM_W), f32)
        for h in R:
            ds_scr[h] = dS[h]
            dx_ref[:, Q0 + h * GDN_DK:Q0 + (h + 1) * GDN_DK] = dq[h]
            dx_ref[:, K0 + h * GDN_DK:K0 + (h + 1) * GDN_DK] = dk[h]
            dx_ref[:, V0 + h * GDN_DV:V0 + (h + 1) * GDN_DV] = dv[h]
            acc = acc + jnp.where(lane == h, dg[h], 0.0) + jnp.where(lane == GDN_H + h, dbeta[h], 0.0)
        dgb_ref[...] = acc

    rev = lambda n: (N - 1 - n, 0)
    return pl.pallas_call(
        body, name="gdn_bwd", grid=(N,),
        in_specs=[pl.BlockSpec((GDN_C, QKV_W), rev), pl.BlockSpec((GDN_C, SM_W), rev),
                  pl.BlockSpec((1, GDN_H, GDN_DK, GDN_DV), lambda n: (N - 1 - n, 0, 0, 0)),
                  pl.BlockSpec((GDN_C, GDN_V), rev)],
        out_specs=[pl.BlockSpec((GDN_C, QKV_W), rev), pl.BlockSpec((GDN_C, SM_W), rev)],
        out_shape=[jax.ShapeDtypeStruct((Lp, QKV_W), f32), jax.ShapeDtypeStruct((Lp, SM_W), f32)],
        scratch_shapes=[pltpu.VMEM((GDN_H, GDN_DK, GDN_DV), f32)],
        compiler_params=_params("arbitrary"),
    )(qkvc, gb, sall, do)


GLA_SUB = ROWS // GLA_C


def _gla_slices(h):
    sq = slice(h * GLA_DK, (h + 1) * GLA_DK)
    sk = slice(GLA_QK + h * GLA_DK, GLA_QK + (h + 1) * GLA_DK)
    sv = slice(2 * GLA_QK + h * GLA_DV, 2 * GLA_QK + (h + 1) * GLA_DV)
    return sq, sk, sv


def _gla_heads(x_ref, la_ref, r):
    sl = [_gla_slices(h) for h in range(GLA_H)]
    return ([x_ref[r, s[0]] for s in sl], [x_ref[r, s[1]] for s in sl], [x_ref[r, s[2]] for s in sl],
            [la_ref[r, s[0]] for s in sl])


def _gla_fwd(proj, la):
    Lp = proj.shape[0]
    NB = Lp // ROWS

    def body(x_ref, la_ref, o_ref, sall_ref, s_scr):
        @pl.when(pl.program_id(0) == 0)
        def _():
            s_scr[...] = jnp.zeros_like(s_scr)

        def sub(c, carry):
            r = pl.ds(pl.multiple_of(c * GLA_C, GLA_C), GLA_C)
            R = range(GLA_H)
            Sts = [s_scr[h] for h in R]
            for h in R:
                sall_ref[c, h] = Sts[h]
            St2, o = _gla_chunk(Sts, *_gla_heads(x_ref, la_ref, r))
            for h in R:
                s_scr[h] = St2[h]
                o_ref[r, h * GLA_DV:(h + 1) * GLA_DV] = o[h]
            return carry

        lax.fori_loop(0, GLA_SUB, sub, 0)

    return pl.pallas_call(
        body, name="gla_fwd", grid=(NB,),
        in_specs=[pl.BlockSpec((ROWS, G_W), lambda n: (n, C_G // G_W)),
                  pl.BlockSpec((ROWS, GLA_QK), lambda n: (n, 0))],
        out_specs=[pl.BlockSpec((ROWS, GLA_V), lambda n: (n, 0)),
                   pl.BlockSpec((GLA_SUB, GLA_H, GLA_DV, GLA_DK), lambda n: (n, 0, 0, 0))],
        out_shape=[jax.ShapeDtypeStruct((Lp, GLA_V), f32),
                   jax.ShapeDtypeStruct((Lp // GLA_C, GLA_H, GLA_DV, GLA_DK), f32)],
        scratch_shapes=[pltpu.VMEM((GLA_H, GLA_DV, GLA_DK), f32)],
        compiler_params=_params("arbitrary"),
    )(proj, la)


def _gla_bwd(proj, la, sall, do, dproj):
    Lp = proj.shape[0]
    NB = Lp // ROWS

    def body(x_ref, la_ref, sall_ref, do_ref, dproj_ref, dx_ref, dla_ref, ds_scr):
        del dproj_ref

        @pl.when(pl.program_id(0) == 0)
        def _():
            ds_scr[...] = jnp.zeros_like(ds_scr)

        def sub(ci, carry):
            c = GLA_SUB - 1 - ci
            r = pl.ds(pl.multiple_of(c * GLA_C, GLA_C), GLA_C)
            R = range(GLA_H)
            _, vjp = jax.vjp(_gla_chunk, [sall_ref[c, h] for h in R], *_gla_heads(x_ref, la_ref, r))
            dS, dq, dk, dv, dl = vjp(([ds_scr[h] for h in R], [do_ref[r, h * GLA_DV:(h + 1) * GLA_DV] for h in R]))
            for h in R:
                sq, sk, sv = _gla_slices(h)
                ds_scr[h] = dS[h]
                dx_ref[r, sq] = dq[h].astype(bf16)
                dx_ref[r, sk] = dk[h].astype(bf16)
                dx_ref[r, sv] = dv[h].astype(bf16)
                dla_ref[r, sq] = dl[h]
            return carry

        lax.fori_loop(0, GLA_SUB, sub, 0)

    x_spec = pl.BlockSpec((ROWS, G_W), lambda n: (NB - 1 - n, C_G // G_W))
    rev = lambda n: (NB - 1 - n, 0)
    return pl.pallas_call(
        body, name="gla_bwd", grid=(NB,),
        in_specs=[x_spec, pl.BlockSpec((ROWS, GLA_QK), rev),
                  pl.BlockSpec((GLA_SUB, GLA_H, GLA_DV, GLA_DK), lambda n: (NB - 1 - n, 0, 0, 0)),
                  pl.BlockSpec((ROWS, GLA_V), rev), pl.BlockSpec(memory_space=pl.ANY)],
        out_specs=[x_spec, pl.BlockSpec((ROWS, GLA_QK), rev)],
        out_shape=[jax.ShapeDtypeStruct(dproj.shape, dproj.dtype), jax.ShapeDtypeStruct((Lp, GLA_QK), f32)],
        input_output_aliases={4: 0},
        scratch_shapes=[pltpu.VMEM((GLA_H, GLA_DV, GLA_DK), f32)],
        compiler_params=_params("arbitrary"),
    )(proj, la, sall, do, dproj)


def _gated_norm_fn(og, ol, zr, wg, wl):
    outs = []
    for h in range(GDN_H):
        s = slice(h * GDN_DV, (h + 1) * GDN_DV)
        outs.append(_rms(og[:, s])[0] * wg * _silu(zr[:, s]))
    for h in range(GLA_H):
        s = slice(h * GLA_DV, (h + 1) * GLA_DV)
        sr = slice(GDN_V + h * GLA_DV, GDN_V + (h + 1) * GLA_DV)
        outs.append(_rms(ol[:, s])[0] * wl * _silu(zr[:, sr]))
    return jnp.concatenate(outs, axis=-1)


def _gated_norm(og, ol, proj, wg, wl):
    Lp = og.shape[0]
    tr = _tile(Lp, 256)

    def body(og_ref, ol_ref, zr_ref, wg_ref, wl_ref, o_ref):
        o_ref[...] = _gated_norm_fn(og_ref[...], ol_ref[...], zr_ref[...], wg_ref[...], wl_ref[...]).astype(bf16)

    return pl.pallas_call(
        body, name="gated_norm", grid=(Lp // tr,),
        in_specs=[pl.BlockSpec((tr, GDN_V), lambda i: (i, 0)), pl.BlockSpec((tr, GLA_V), lambda i: (i, 0)),
                  pl.BlockSpec((tr, ZR_W), lambda i: (i, C_ZR // ZR_W)),
                  pl.BlockSpec((1, GDN_DV), lambda i: (0, 0)), pl.BlockSpec((1, GLA_DV), lambda i: (0, 0))],
        out_specs=pl.BlockSpec((tr, ZR_W), lambda i: (i, 0)),
        out_shape=jax.ShapeDtypeStruct((Lp, ZR_W), bf16),
        compiler_params=_params("parallel"),
    )(og, ol, proj, wg, wl)


def _gated_norm_bwd(og, ol, proj, wg, wl, dmix):
    Lp = og.shape[0]
    tr = _tile(Lp, 128)

    def body(og_ref, ol_ref, zr_ref, wg_ref, wl_ref, d_ref, dog_ref, dol_ref, dzr_ref, gwg_ref, gwl_ref):
        i = pl.program_id(0)
        _, vjp = jax.vjp(_gated_norm_fn, og_ref[...], ol_ref[...], zr_ref[...], wg_ref[...], wl_ref[...])
        dog, dol, dzr, gwg, gwl = vjp(d_ref[...])
        dog_ref[...] = dog
        dol_ref[...] = dol
        dzr_ref[...] = dzr.astype(bf16)

        @pl.when(i == 0)
        def _():
            gwg_ref[...] = gwg
            gwl_ref[...] = gwl

        @pl.when(i > 0)
        def _():
            gwg_ref[...] += gwg
            gwl_ref[...] += gwl

    og_spec = pl.BlockSpec((tr, GDN_V), lambda i: (i, 0))
    ol_spec = pl.BlockSpec((tr, GLA_V), lambda i: (i, 0))
    zr_spec = pl.BlockSpec((tr, ZR_W), lambda i: (i, C_ZR // ZR_W))
    vg = pl.BlockSpec((1, GDN_DV), lambda i: (0, 0))
    vl = pl.BlockSpec((1, GLA_DV), lambda i: (0, 0))
    return pl.pallas_call(
        body, name="gated_norm_bwd", grid=(Lp // tr,),
        in_specs=[og_spec, ol_spec, zr_spec, vg, vl, pl.BlockSpec((tr, ZR_W), lambda i: (i, 0))],
        out_specs=[og_spec, ol_spec, zr_spec, vg, vl],
        out_shape=[jax.ShapeDtypeStruct((Lp, GDN_V), f32), jax.ShapeDtypeStruct((Lp, GLA_V), f32),
                   jax.ShapeDtypeStruct((Lp, C_END), bf16),
                   jax.ShapeDtypeStruct((1, GDN_DV), f32), jax.ShapeDtypeStruct((1, GLA_DV), f32)],
        compiler_params=_params("arbitrary"),
    )(og, ol, proj, wg, wl, dmix)


def _adamw(g, w, m, v, name):
    R, C = g.shape
    tr = _tile(R, 256, 8) if R % 8 == 0 and R > 256 else R
    c1 = 1.0 - ADAM_B1 ** ADAM_STEP
    c2 = 1.0 - ADAM_B2 ** ADAM_STEP

    def body(g_ref, w_ref, m_ref, v_ref, d_ref, mo_ref, vo_ref):
        g_ = g_ref[...]
        m2 = ADAM_B1 * m_ref[...] + (1.0 - ADAM_B1) * g_
        v2 = ADAM_B2 * v_ref[...] + (1.0 - ADAM_B2) * (g_ * g_)
        mo_ref[...] = m2
        vo_ref[...] = v2
        d_ref[...] = -ADAM_LR * ((m2 / c1) / (jnp.sqrt(v2 / c2) + ADAM_EPS) + ADAM_WD * w_ref[...])

    blk = pl.BlockSpec((tr, C), lambda i: (i, 0))
    return pl.pallas_call(
        body, name=name, grid=(R // tr,), in_specs=[blk] * 4, out_specs=[blk] * 3,
        out_shape=[jax.ShapeDtypeStruct((R, C), f32)] * 3,
        compiler_params=_params("parallel"),
    )(g, w, m, v)


def _sum_slots(r, name):
    n, R, C = r.shape
    tr = _tile(R, 128, 8) if R % 8 == 0 and R > 128 else R

    def body(r_ref, o_ref):
        acc = r_ref[0]
        for s in range(1, n):
            acc = acc + r_ref[s]
        o_ref[...] = acc

    return pl.pallas_call(
        body, name=name, grid=(R // tr,),
        in_specs=[pl.BlockSpec((n, tr, C), lambda i: (0, i, 0))],
        out_specs=pl.BlockSpec((tr, C), lambda i: (i, 0)),
        out_shape=jax.ShapeDtypeStruct((R, C), f32),
        compiler_params=_params("parallel"),
    )(r)


_ANY = pl.BlockSpec(memory_space=pl.ANY)


def _allgather_chips(arrs):
    n = len(arrs)

    def body(*refs):
        ins, outs = refs[:n], refs[n:2 * n]
        send, recv, lsem = refs[2 * n:]
        x, y, c = lax.axis_index("x"), lax.axis_index("y"), lax.axis_index("c")
        chips = [(1 - x, y), (x, 1 - y), (1 - x, 1 - y)]
        me = 2 * x + y
        local = [pltpu.make_async_copy(ins[a], outs[a].at[me], lsem.at[a]) for a in range(n)]
        for cp in local:
            cp.start()
        for a in range(n):
            for j, (px, py) in enumerate(chips):
                pltpu.make_async_remote_copy(
                    src_ref=ins[a], dst_ref=outs[a].at[me], send_sem=send.at[a, j], recv_sem=recv.at[a, j],
                    device_id=(px, py, c), device_id_type=MESH).start()
        for a in range(n):
            for j, (px, py) in enumerate(chips):
                pltpu.make_async_remote_copy(
                    src_ref=ins[a], dst_ref=outs[a].at[2 * px + py], send_sem=send.at[a, j], recv_sem=recv.at[a, j],
                    device_id=(px, py, c), device_id_type=MESH).wait()
        for cp in local:
            cp.wait()

    return pl.pallas_call(
        body, name="allgather_weights", in_specs=[_ANY] * n, out_specs=[_ANY] * n,
        out_shape=[jax.ShapeDtypeStruct((N_CHIP,) + a.shape, a.dtype) for a in arrs],
        scratch_shapes=[pltpu.SemaphoreType.DMA((n, 3)), pltpu.SemaphoreType.DMA((n, 3)), pltpu.SemaphoreType.DMA((n,))],
    )(*arrs)


def _exchange_slots(slotted, shared):
    ns, nb = len(slotted), len(shared)
    n = ns + nb

    def body(*refs):
        ins, outs = refs[:n], refs[n:2 * n]
        send, recv, lsem = refs[2 * n:]
        x, y, c = lax.axis_index("x"), lax.axis_index("y"), lax.axis_index("c")
        me = 4 * x + 2 * y + c

        def peer(o):
            tx = 1 - x if o & 4 else x
            ty = 1 - y if o & 2 else y
            tc = 1 - c if o & 1 else c
            return (tx, ty, tc), 4 * tx + 2 * ty + tc

        def src(a, t):
            return ins[a].at[t] if a < ns else ins[a]

        local = [pltpu.make_async_copy(src(a, me), outs[a].at[me], lsem.at[a]) for a in range(n)]
        for cp in local:
            cp.start()
        for a in range(n):
            for o in range(1, N_DEV):
                dev, t = peer(o)
                pltpu.make_async_remote_copy(
                    src_ref=src(a, t), dst_ref=outs[a].at[me], send_sem=send.at[a, o - 1], recv_sem=recv.at[a, o - 1],
                    device_id=dev, device_id_type=MESH).start()
        for a in range(n):
            for o in range(1, N_DEV):
                dev, t = peer(o)
                pltpu.make_async_remote_copy(
                    src_ref=src(a, t), dst_ref=outs[a].at[t], send_sem=send.at[a, o - 1], recv_sem=recv.at[a, o - 1],
                    device_id=dev, device_id_type=MESH).wait()
        for cp in local:
            cp.wait()

    shapes = [jax.ShapeDtypeStruct(a.shape, a.dtype) for a in slotted]
    shapes += [jax.ShapeDtypeStruct((N_DEV,) + b.shape, b.dtype) for b in shared]
    return pl.pallas_call(
        body, name="exchange_grads", in_specs=[_ANY] * n, out_specs=[_ANY] * n, out_shape=shapes,
        scratch_shapes=[pltpu.SemaphoreType.DMA((n, N_DEV - 1)), pltpu.SemaphoreType.DMA((n, N_DEV - 1)),
                        pltpu.SemaphoreType.DMA((n,))],
    )(*slotted, *shared)


def _exchange_halves(halves):
    n = len(halves)

    def body(*refs):
        ins, outs = refs[:n], refs[n:2 * n]
        send, recv, lsem = refs[2 * n:]
        x, y, c = lax.axis_index("x"), lax.axis_index("y"), lax.axis_index("c")
        local = [pltpu.make_async_copy(ins[a], outs[a].at[c], lsem.at[a]) for a in range(n)]
        for cp in local:
            cp.start()
        for a in range(n):
            pltpu.make_async_remote_copy(
                src_ref=ins[a], dst_ref=outs[a].at[c], send_sem=send.at[a], recv_sem=recv.at[a],
                device_id=(x, y, 1 - c), device_id_type=MESH).start()
        for a in range(n):
            pltpu.make_async_remote_copy(
                src_ref=ins[a], dst_ref=outs[a].at[1 - c], send_sem=send.at[a], recv_sem=recv.at[a],
                device_id=(x, y, 1 - c), device_id_type=MESH).wait()
        for cp in local:
            cp.wait()

    return pl.pallas_call(
        body, name="exchange_halves", in_specs=[_ANY] * n, out_specs=[_ANY] * n,
        out_shape=[jax.ShapeDtypeStruct((2,) + a.shape, a.dtype) for a in halves],
        scratch_shapes=[pltpu.SemaphoreType.DMA((n,)), pltpu.SemaphoreType.DMA((n,)), pltpu.SemaphoreType.DMA((n,))],
    )(*halves)


def _local_step(x, loss_target, meta, attn_norm_w, w_in, conv_w, a_log, dt_bias, gdn_norm_w, w2, gate_b,
                gla_norm_w, w_out, ffn_norm_w, w_gate, w_up, w_down, final_norm_w):
    _, S, D = x.shape
    F = w_gate.shape[1]
    wp = jnp.concatenate([w_in[:, R_Z:R_AB], w_in[:, R_GR:R_LR], w_in[:, R_G:R_GR], w_in[:, R_QKV:R_Z]], axis=1)
    wsm = jnp.concatenate([w_in[:, R_AB:R_G], w_in[:, R_LR:R_END],
                           jnp.zeros((D, SM_W - SM_LR - GATE_RANK), w_in.dtype)], axis=1)
    wgu = jnp.concatenate([w_gate, w_up], axis=1)
    w2p = jnp.pad(w2, ((SM_LR, SM_W - SM_LR - GATE_RANK), (0, 0)))
    alog_p = jnp.pad(a_log, ((0, 0), (0, SM_W - GDN_H)))
    dtb_p = jnp.pad(dt_bias, ((0, 0), (0, SM_W - GDN_H)))
    m64 = jnp.concatenate([jnp.zeros((PAD, D), f32), meta], axis=0)

    h0, n1 = _embed_norm(x, m64, attn_norm_w)
    proj = _mm(n1, wp, "nn", "proj")
    psm = _mm(n1, wsm, "nn", "proj_small")
    gb, la = _gates(psm, w2p, gate_b, alog_p, dtb_p)
    qkvc = _conv(proj, conv_w)
    og, sall = _gdn_fwd(qkvc, gb)
    ol, stall = _gla_fwd(proj, la)
    mixed = _gated_norm(og, ol, proj, gdn_norm_w, gla_norm_w)
    attn = _mm(mixed, w_out, "nn", "out_proj")
    h1, n2 = _add_norm(h0, attn, ffn_norm_w)
    gu = _mm(n2, wgu, "nn", "ffn_gate_up")
    act = _swiglu(gu)
    ffn = _mm(act, w_down, "nn", "ffn_down", tk_cap=1408)
    dh2, dh2b, lossp, g_final = _final(h1, ffn, loss_target, final_norm_w)

    dact = _mm(dh2b, w_down, "nt", "d_act")
    g_down = _mm(act, dh2b, "tn", "g_w_down", tn_cap=2048, tk_cap=1408)
    dg, du = _swiglu_bwd(gu, dact)
    g_gate = _mm(n2, dg, "tn", "g_w_gate", tm_cap=2048, tk_cap=1408)
    g_up = _mm(n2, du, "tn", "g_w_up", tm_cap=2048, tk_cap=1408)
    dn2 = _mm(dg, w_gate, "nt", "d_n2_gate", tk_cap=1408)
    dn2 = _mm(du, w_up, "nt", "d_n2_up", tk_cap=1408, acc_in=dn2)
    dh1, dh1b, g_ffn_norm = _norm_bwd(dn2, h1, dh2, ffn_norm_w)
    dmix = _mm(dh1b, w_out, "nt", "d_mixed")
    g_out = _mm(mixed, dh1b, "tn", "g_w_out", tn_cap=2048, tk_cap=1408)
    dog, dol, dproj, g_gdn_norm, g_gla_norm = _gated_norm_bwd(og, ol, proj, gdn_norm_w, gla_norm_w, dmix)
    dproj, dla = _gla_bwd(proj, la, stall, dol, dproj)
    dqkvc, dgb = _gdn_bwd(qkvc, gb, sall, dog)
    dproj, g_conv = _conv_bwd(proj, conv_w, dqkvc, dproj)
    dpsm, g_w2p, g_gate_b, g_alog, g_dtb = _gates_bwd(psm, w2p, gate_b, alog_p, dtb_p, dgb, dla)
    dn1 = _mm(dproj, wp, "nt", "d_n1", tk_cap=1024)
    dn1 = _mm(dpsm, wsm, "nt", "d_n1_small", acc_in=dn1)
    g_wp = _mm(n1, dproj, "tn", "g_w_in", tm_cap=2048, tk_cap=1408)
    g_wsm = _mm(n1, dpsm, "tn", "g_w_in_small", tm_cap=2048, tk_cap=1408)
    grad_x, g_meta, g_attn_norm = _embed_norm_bwd(dn1, h0, dh1, attn_norm_w, S)

    g_w_in = jnp.concatenate([g_wp[:, C_QKV:C_END], g_wp[:, C_ZR:C_ZR + GDN_V], g_wsm[:, :SM_LR],
                              g_wp[:, C_G:C_G + G_W], g_wp[:, C_ZR + GDN_V:C_ZR + ZR_W],
                              g_wsm[:, SM_LR:SM_LR + GATE_RANK]], axis=1)
    grads = dict(
        meta_tokens=g_meta, attn_norm_w=g_attn_norm, w_in=g_w_in, gdn_conv_w=g_conv, gdn_a_log=g_alog[:, :GDN_H],
        gdn_dt_bias=g_dtb[:, :GDN_H], gdn_norm_w=g_gdn_norm, gla_gate_w2=g_w2p[SM_LR:SM_LR + GATE_RANK],
        gla_gate_b=g_gate_b, gla_norm_w=g_gla_norm, w_out=g_out, ffn_norm_w=g_ffn_norm, w_gate=g_gate, w_up=g_up,
        w_down=g_down, final_norm_w=g_final)
    return lossp[0, 0], grad_x, grads


_WEIGHTS = ("meta_tokens", "attn_norm_w", "w_in", "gdn_conv_w", "gdn_a_log", "gdn_dt_bias", "gdn_norm_w",
            "gla_gate_w2", "gla_gate_b", "gla_norm_w", "w_out", "ffn_norm_w", "w_gate", "w_up", "w_down",
            "final_norm_w")
_BIG_COLS = ("w_in", "w_gate", "w_up")
_BIG_ROWS = ("w_out", "w_down")
_SMALL_SHARDED = ("meta_tokens", "gdn_conv_w", "gla_gate_w2")


def _cat_cols(g):
    return jnp.concatenate([g[i] for i in range(N_CHIP)], axis=-1)


def kernel(x, meta_tokens, attn_norm_w, w_in, gdn_conv_w, gdn_a_log, gdn_dt_bias, gdn_norm_w, gla_gate_w2, gla_gate_b, gla_norm_w, w_out, ffn_norm_w, w_gate, w_up, w_down, final_norm_w, loss_target, m_meta_tokens, m_attn_norm_w, m_w_in, m_gdn_conv_w, m_gdn_a_log, m_gdn_dt_bias, m_gdn_norm_w, m_gla_gate_w2, m_gla_gate_b, m_gla_norm_w, m_w_out, m_ffn_norm_w, m_w_gate, m_w_up, m_w_down, m_final_norm_w, v_meta_tokens, v_attn_norm_w, v_w_in, v_gdn_conv_w, v_gdn_a_log, v_gdn_dt_bias, v_gdn_norm_w, v_gla_gate_w2, v_gla_gate_b, v_gla_norm_w, v_w_out, v_ffn_norm_w, v_w_gate, v_w_up, v_w_down, v_final_norm_w):
    w = dict(meta_tokens=meta_tokens, attn_norm_w=attn_norm_w, w_in=w_in, gdn_conv_w=gdn_conv_w, gdn_a_log=gdn_a_log,
             gdn_dt_bias=gdn_dt_bias, gdn_norm_w=gdn_norm_w, gla_gate_w2=gla_gate_w2, gla_gate_b=gla_gate_b,
             gla_norm_w=gla_norm_w, w_out=w_out, ffn_norm_w=ffn_norm_w, w_gate=w_gate, w_up=w_up, w_down=w_down,
             final_norm_w=final_norm_w)
    m = dict(meta_tokens=m_meta_tokens, attn_norm_w=m_attn_norm_w, w_in=m_w_in, gdn_conv_w=m_gdn_conv_w,
             gdn_a_log=m_gdn_a_log, gdn_dt_bias=m_gdn_dt_bias, gdn_norm_w=m_gdn_norm_w, gla_gate_w2=m_gla_gate_w2,
             gla_gate_b=m_gla_gate_b, gla_norm_w=m_gla_norm_w, w_out=m_w_out, ffn_norm_w=m_ffn_norm_w,
             w_gate=m_w_gate, w_up=m_w_up, w_down=m_w_down, final_norm_w=m_final_norm_w)
    v = dict(meta_tokens=v_meta_tokens, attn_norm_w=v_attn_norm_w, w_in=v_w_in, gdn_conv_w=v_gdn_conv_w,
             gdn_a_log=v_gdn_a_log, gdn_dt_bias=v_gdn_dt_bias, gdn_norm_w=v_gdn_norm_w, gla_gate_w2=v_gla_gate_w2,
             gla_gate_b=v_gla_gate_b, gla_norm_w=v_gla_norm_w, w_out=v_w_out, ffn_norm_w=v_ffn_norm_w,
             w_gate=v_w_gate, w_up=v_w_up, w_down=v_w_down, final_norm_w=v_final_norm_w)
    D = x.shape[2]
    chip = 2 * lax.axis_index("x") + lax.axis_index("y")

    def two_d(name, a):
        return a.reshape(1, -1) if a.ndim == 1 else a.reshape(-1, a.shape[-1])

    w2d = {k: two_d(k, a) for k, a in w.items()}

    names = _BIG_COLS + _BIG_ROWS + _SMALL_SHARDED
    gathered = _allgather_chips([w2d[k].astype(bf16) if k in _BIG_COLS + _BIG_ROWS else w2d[k] for k in names])
    full = dict(zip(names, gathered))
    for k in _BIG_COLS + _SMALL_SHARDED:
        full[k] = _cat_cols(full[k])
    for k in _BIG_ROWS:
        full[k] = full[k].reshape(-1, full[k].shape[-1])

    lossp, grad_x, g = _local_step(
        x, loss_target, full["meta_tokens"], w2d["attn_norm_w"], full["w_in"], full["gdn_conv_w"], w2d["gdn_a_log"],
        w2d["gdn_dt_bias"], w2d["gdn_norm_w"], full["gla_gate_w2"], w2d["gla_gate_b"], w2d["gla_norm_w"],
        full["w_out"], w2d["ffn_norm_w"], full["w_gate"], full["w_up"], full["w_down"], w2d["final_norm_w"])
    loss = lax.psum(lossp, ("x", "y", "c"))

    def slabs(k):
        a = g[k]
        if k in _BIG_COLS:
            R, C = a.shape
            return a.reshape(2, R // 2, N_CHIP, C // N_CHIP).transpose(2, 0, 1, 3).reshape(N_DEV, R // 2, C // N_CHIP)
        return a.reshape(N_DEV, a.shape[0] // N_DEV, a.shape[1])

    big = _BIG_COLS + _BIG_ROWS
    small = tuple(k for k in _WEIGHTS if k not in big)
    sizes = [g[k].size for k in small]
    total = sum(sizes)
    rows = -(-total // 1024)
    rows += (-rows) % 8
    packed = jnp.concatenate([g[k].reshape(-1) for k in small] + [jnp.zeros((rows * 1024 - total,), f32)])
    recv = _exchange_slots([slabs(k) for k in big], [packed.reshape(rows, 1024)])
    halves = _exchange_halves([_sum_slots(r, "sum_" + k) for k, r in zip(big, recv[:len(big)])])
    red = {k: h.reshape(w2d[k].shape) for k, h in zip(big, halves)}
    psum_small = _sum_slots(recv[-1], "sum_small").reshape(-1)
    off = 0
    for k, n in zip(small, sizes):
        a = psum_small[off:off + n].reshape(g[k].shape)
        off += n
        if k in _SMALL_SHARDED:
            c = w2d[k].shape[1]
            a = lax.dynamic_slice_in_dim(a, chip * c, c, axis=1)
        red[k] = a

    grads, deltas, new_m, new_v = [], [], [], []
    for k in _WEIGHTS:
        d, m2, v2 = _adamw(red[k], w2d[k], two_d(k, m[k]), two_d(k, v[k]), "adamw_" + k)
        shape = w[k].shape
        grads.append(red[k].reshape(shape))
        deltas.append(d.reshape(shape))
        new_m.append(m2.reshape(shape))
        new_v.append(v2.reshape(shape))
    return (loss, grad_x, *grads, *deltas, *new_m, *new_v)
```

```python
import functools

import jax
import jax.numpy as jnp
from jax import lax
from jax.experimental import pallas as pl
from jax.experimental.pallas import tpu as pltpu

f32 = jnp.float32
bf16 = jnp.bfloat16
HI = lax.Precision.HIGHEST
HIGH = lax.Precision.HIGH
MESH = pl.DeviceIdType.MESH

N_META = 16
CONV_K = 4
GDN_H, GDN_DK, GDN_DV, GDN_C = 8, 128, 128, 64
GLA_H, GLA_DK, GLA_DV, GLA_C = 4, 128, 256, 16
GATE_RANK = 16
GATE_NORMALIZER = 16.0
EPS = 1e-6
GDN_QK = GDN_H * GDN_DK
GDN_V = GDN_H * GDN_DV
GLA_QK = GLA_H * GLA_DK
GLA_V = GLA_H * GLA_DV
PAD = (-N_META) % GDN_C
OFF = PAD + N_META
ROWS = 64

R_QKV, R_Z, R_AB, R_G, R_GR, R_LR, R_END = 0, 3072, 4096, 4112, 6160, 7184, 7200
C_ZR, C_G, C_QKV, C_END = 0, 2048, 4096, 7168
ZR_W = GDN_V + GLA_V
G_W = 2 * GLA_QK + GLA_V
QKV_W = 2 * GDN_QK + GDN_V
Q0, K0, V0 = 0, GDN_QK, 2 * GDN_QK
SM_W = 128
SM_LR = 2 * GDN_H

ADAM_LR, ADAM_B1, ADAM_B2, ADAM_EPS, ADAM_WD, ADAM_STEP = 0.001, 0.9, 0.999, 1e-08, 0.01, 10

VMEM_LIMIT_V7X = 56 * 1024 * 1024
N_DEV = 8
N_CHIP = 4


def _params(*sem):
    return pltpu.CompilerParams(dimension_semantics=sem, vmem_limit_bytes=VMEM_LIMIT_V7X)


def _tile(n, cap, mult=16):
    best = None
    for d in range(mult, min(n, cap) + 1, mult):
        if n % d == 0:
            best = d
    assert best is not None, (n, cap, mult)
    return best


NN = ((1,), (0,))
NT = ((1,), (1,))
TN = ((0,), (0,))


def _dot(a, b, dims, prec=None):
    return lax.dot_general(a, b, (dims, ((), ())), precision=prec, preferred_element_type=f32)


def _mmb(a, b, dims):
    return _dot(a.astype(bf16), b.astype(bf16), dims)


def _sigmoid(x):
    return jax.nn.sigmoid(x)


def _silu(x):
    return x * _sigmoid(x)


def _dsilu(x):
    s = _sigmoid(x)
    return s * (1.0 + x * (1.0 - s))


def _log1p_exp_neg_abs(x):
    t = jnp.exp(-jnp.abs(x))
    u = 1.0 + t
    d = u - 1.0
    return jnp.where(d == 0.0, t, jnp.log(u) * (t / jnp.where(d == 0.0, 1.0, d)))


def _softplus(x):
    return jnp.maximum(x, 0.0) + _log1p_exp_neg_abs(x)


def _log_sigmoid(x):
    return jnp.minimum(x, 0.0) - _log1p_exp_neg_abs(x)


def _rms(x):
    r = lax.rsqrt(jnp.mean(x * x, axis=-1, keepdims=True) + EPS)
    return x * r, r


def _rms_bwd(dy, xh, r, w):
    t = dy * w
    return r * (t - xh * jnp.mean(t * xh, axis=-1, keepdims=True))


def _l2n(x):
    return x * lax.rsqrt(jnp.sum(x * x, axis=-1, keepdims=True) + EPS)


INV_LEAF = 8


def _same_block(C, b):
    sh = b.bit_length() - 1
    row = lax.broadcasted_iota(jnp.int32, (C, C), 0)
    col = lax.broadcasted_iota(jnp.int32, (C, C), 1)
    return lax.shift_right_logical(row, sh) == lax.shift_right_logical(col, sh)


def _tri_inv_impl(As):
    C = As[0].shape[0]
    R = range(len(As))
    row = lax.broadcasted_iota(jnp.int32, (C, C), 0)
    col = lax.broadcasted_iota(jnp.int32, (C, C), 1)
    eye = (row == col).astype(f32)
    b = INV_LEAF
    inner = _same_block(C, b)
    leaf = [jnp.where(inner, As[h], 0.0) for h in R]
    d = [eye - leaf[h] for h in R]
    pw = leaf
    n = 2
    while n < b:
        pw = [_dot(pw[h], pw[h], NN, HIGH) for h in R]
        d = [_dot(d[h], eye + pw[h], NN, HIGH) for h in R]
        n *= 2
    while b < C:
        outer = _same_block(C, 2 * b)
        level = jnp.logical_and(outer, jnp.logical_not(inner))
        ed = [_dot(jnp.where(level, As[h], 0.0), d[h], NN, HIGH) for h in R]
        d = [d[h] - _dot(d[h], ed[h], NN, HIGH) for h in R]
        inner = outer
        b *= 2
    return d


@jax.custom_vjp
def _tri_inv(As):
    return _tri_inv_impl(As)


def _tri_inv_fwd(As):
    d = _tri_inv_impl(As)
    return d, d


def _tri_inv_bwd(d, g):
    R = range(len(d))
    t = [_dot(d[h], g[h], TN, HIGH) for h in R]
    return ([-_dot(t[h], d[h], NT, HIGH) for h in R],)


_tri_inv.defvjp(_tri_inv_fwd, _tri_inv_bwd)


def _gdn_chunk(Ss, qrs, krs, vs, betas, gs):
    C, dk = qrs[0].shape
    R = range(len(Ss))
    row = lax.broadcasted_iota(jnp.int32, (C, C), 0)
    col = lax.broadcasted_iota(jnp.int32, (C, C), 1)
    causal = row >= col
    strict = row > col
    cf = causal.astype(f32)
    q = [_l2n(qrs[h]) * (dk ** -0.5) for h in R]
    k = [_l2n(krs[h]) for h in R]
    mc = [_dot(cf, jnp.broadcast_to(gs[h], (C, C)), NN, HI) for h in R]
    gc = [mc[h][:, 0:1] for h in R]
    decay = [jnp.where(causal, jnp.exp(jnp.where(causal, mc[h] - mc[h].T, 0.0)), 0.0) for h in R]
    kb = [k[h] * betas[h] for h in R]
    a = [jnp.where(strict, _mmb(kb[h], k[h], NT) * decay[h], 0.0) for h in R]
    p = _tri_inv(a)
    egc = [jnp.exp(gc[h]) for h in R]
    u = [_mmb(p[h], vs[h] * betas[h], NN) for h in R]
    w = [_mmb(p[h], kb[h] * egc[h], NN) for h in R]
    qk = [jnp.where(causal, _mmb(q[h], k[h], NT) * decay[h], 0.0) for h in R]
    v_new = [u[h] - _mmb(w[h], Ss[h], NN) for h in R]
    o = [_mmb(q[h] * egc[h], Ss[h], NN) + _mmb(qk[h], v_new[h], NN) for h in R]
    gl = [gc[h][C - 1:C, :] for h in R]
    kd = [k[h] * jnp.exp(gl[h] - gc[h]) for h in R]
    S2 = [Ss[h] * jnp.exp(gl[h]) + _mmb(kd[h], v_new[h], TN) for h in R]
    return S2, o


def _gla_chunk(Sts, qrs, ks, vs, las):
    C, dk = qrs[0].shape
    R = range(len(Sts))
    row = lax.broadcasted_iota(jnp.int32, (C, C), 0)
    col = lax.broadcasted_iota(jnp.int32, (C, C), 1)
    cf = (row >= col).astype(f32)
    q = [qrs[h] * (dk ** -0.5) for h in R]
    b = [_dot(cf, las[h], NN, HI) for h in R]
    o = [_mmb(q[h] * jnp.exp(b[h]), Sts[h], NT) for h in R]
    ri = lax.broadcasted_iota(jnp.int32, (C, dk), 0)
    for j in range(C):
        m = ri >= j
        e = [jnp.where(m, jnp.exp(jnp.where(m, b[h] - b[h][j:j + 1, :], 0.0)), 0.0) for h in R]
        s = [jnp.sum(q[h] * ks[h][j:j + 1, :] * e[h], axis=-1, keepdims=True) for h in R]
        o = [o[h] + s[h] * vs[h][j:j + 1, :] for h in R]
    bl = [b[h][C - 1:C, :] for h in R]
    St2 = [Sts[h] * jnp.exp(bl[h]) + _mmb(vs[h], ks[h] * jnp.exp(bl[h] - b[h]), TN) for h in R]
    return St2, o


_ANY = pl.BlockSpec(memory_space=pl.ANY)


class _Gather:
    def __init__(self, arrays):
        self.arrays = list(arrays)
        self.n = len(self.arrays)
        self.out_shape = [jax.ShapeDtypeStruct((N_CHIP,) + a.shape, a.dtype) for a in self.arrays]
        self.sems = [pltpu.SemaphoreType.DMA((self.n, 3)), pltpu.SemaphoreType.DMA((self.n, 3)),
                     pltpu.SemaphoreType.DMA((self.n,))]

    def hooks(self, ins, outs, send, recv, lsem):
        def copies():
            x, y, c = lax.axis_index("x"), lax.axis_index("y"), lax.axis_index("c")
            me = 2 * x + y
            out = []
            for a in range(self.n):
                out.append((pltpu.make_async_copy(ins[a], outs[a].at[me], lsem.at[a]), None))
                for j, (px, py) in enumerate([(1 - x, y), (x, 1 - y), (1 - x, 1 - y)]):
                    mk = lambda dst, a=a, j=j, px=px, py=py: pltpu.make_async_remote_copy(
                        src_ref=ins[a], dst_ref=dst, send_sem=send.at[a, j], recv_sem=recv.at[a, j],
                        device_id=(px, py, c), device_id_type=MESH)
                    out.append((mk(outs[a].at[me]), mk(outs[a].at[2 * px + py])))
            return out

        return _start_wait(copies)


class _Exchange:
    def __init__(self, slotted, shared=(), by_chip=()):
        self.arrays = list(slotted) + list(by_chip) + list(shared)
        self.ns, self.nc = len(slotted), len(by_chip)
        self.n = len(self.arrays)
        self.out_shape = [jax.ShapeDtypeStruct(a.shape, a.dtype) for a in slotted]
        self.out_shape += [jax.ShapeDtypeStruct((N_DEV,) + a.shape[1:], a.dtype) for a in by_chip]
        self.out_shape += [jax.ShapeDtypeStruct((N_DEV,) + b.shape, b.dtype) for b in shared]
        self.sems = [pltpu.SemaphoreType.DMA((self.n, N_DEV - 1)), pltpu.SemaphoreType.DMA((self.n, N_DEV - 1)),
                     pltpu.SemaphoreType.DMA((self.n,))]

    def hooks(self, ins, outs, send, recv, lsem):
        def copies():
            x, y, c = lax.axis_index("x"), lax.axis_index("y"), lax.axis_index("c")
            me = 4 * x + 2 * y + c

            def src(a, dev):
                tx, ty, tc = dev
                if a < self.ns:
                    return ins[a].at[4 * tx + 2 * ty + tc]
                return ins[a].at[2 * tx + ty] if a < self.ns + self.nc else ins[a]

            out = []
            for a in range(self.n):
                out.append((pltpu.make_async_copy(src(a, (x, y, c)), outs[a].at[me], lsem.at[a]), None))
                for o in range(1, N_DEV):
                    dev = (1 - x if o & 4 else x, 1 - y if o & 2 else y, 1 - c if o & 1 else c)
                    t = 4 * dev[0] + 2 * dev[1] + dev[2]
                    mk = lambda dst, a=a, o=o, dev=dev: pltpu.make_async_remote_copy(
                        src_ref=src(a, dev), dst_ref=dst, send_sem=send.at[a, o - 1], recv_sem=recv.at[a, o - 1],
                        device_id=dev, device_id_type=MESH)
                    out.append((mk(outs[a].at[me]), mk(outs[a].at[t])))
            return out

        return _start_wait(copies)


def _start_wait(copies):
    def start():
        for s, _ in copies():
            s.start()

    def wait():
        for s, w in copies():
            (s if w is None else w).wait()

    return start, wait


def _call(body, *, name, grid, in_specs, out_specs, out_shape, args, sem, scratch_shapes=(), aliases=None, side=None):
    in_specs, out_specs, out_shape, args = list(in_specs), list(out_specs), list(out_shape), list(args)
    scratch_shapes = list(scratch_shapes)
    aliases = aliases or {}
    if side is None:
        return pl.pallas_call(
            body, name=name, grid=grid, in_specs=in_specs, out_specs=out_specs, out_shape=out_shape,
            scratch_shapes=scratch_shapes, input_output_aliases=aliases, compiler_params=_params(*sem))(*args)
    n_in, n_out, n_scr, ns = len(in_specs), len(out_specs), len(scratch_shapes), side.n

    def full_body(*refs):
        ins, refs = refs[:n_in], refs[n_in:]
        s_in, refs = refs[:ns], refs[ns:]
        outs, refs = refs[:n_out], refs[n_out:]
        s_out, refs = refs[:ns], refs[ns:]
        scr, sems = refs[:n_scr], refs[n_scr:]
        start, wait = side.hooks(s_in, s_out, *sems)
        ids = [pl.program_id(d) for d in range(len(grid))]
        first = functools.reduce(jnp.logical_and, [i == 0 for i in ids])
        last = functools.reduce(jnp.logical_and, [i == g - 1 for i, g in zip(ids, grid)])
        pl.when(first)(start)
        body(*ins, *outs, *scr)
        pl.when(last)(wait)

    return pl.pallas_call(
        full_body, name=name, grid=grid, in_specs=in_specs + [_ANY] * ns, out_specs=out_specs + [_ANY] * ns,
        out_shape=out_shape + side.out_shape, scratch_shapes=scratch_shapes + side.sems,
        input_output_aliases=aliases, compiler_params=_params(*(["arbitrary"] * len(grid))))(*args, *side.arrays)


def _mm(a, b, mode, name, *, tm_cap=1408, tn_cap=1024, tk_cap=2048, out_dtype=f32, acc_in=None, side=None,
        col_slabs=False):
    if mode == "nn":
        (M, K), (K2, N) = a.shape, b.shape
    elif mode == "nt":
        (M, K), (N, K2) = a.shape, b.shape
    else:
        (K, M), (K2, N) = a.shape, b.shape
    assert K == K2, (name, a.shape, b.shape)
    tm = _tile(M // 2 if col_slabs else M, tm_cap)
    tn = _tile(N // N_CHIP if col_slabs else N, tn_cap, 128)
    tk = _tile(K, tk_cap, 128 if K % 128 == 0 else 16)
    nk = K // tk
    dims = {"nn": NN, "nt": NT, "tn": TN}[mode]
    use_scratch = nk > 1 and out_dtype != f32

    def body(*refs):
        if acc_in is not None:
            a_ref, b_ref, c_ref, o_ref, *scr = refs
        else:
            a_ref, b_ref, o_ref, *scr = refs
            c_ref = None
        p = _mmb(a_ref[...], b_ref[...], dims)
        if nk == 1:
            if c_ref is not None:
                p = p + c_ref[...]
            o_ref[...] = p.astype(out_dtype)
            return
        k = pl.program_id(2)
        acc = scr[0] if use_scratch else o_ref

        @pl.when(k == 0)
        def _():
            acc[...] = p if c_ref is None else p + c_ref[...]

        @pl.when(k > 0)
        def _():
            acc[...] += p

        if use_scratch:
            @pl.when(k == nk - 1)
            def _():
                o_ref[...] = acc[...].astype(out_dtype)

    if mode == "tn":
        a_spec = pl.BlockSpec((tk, tm), lambda i, j, k: (k, i))
    else:
        a_spec = pl.BlockSpec((tm, tk), lambda i, j, k: (i, k))
    if mode == "nt":
        b_spec = pl.BlockSpec((tn, tk), lambda i, j, k: (j, k))
    else:
        b_spec = pl.BlockSpec((tk, tn), lambda i, j, k: (k, j))
    if col_slabs:
        assert acc_in is None
        ni, nj = M // 2 // tm, N // N_CHIP // tn
        o_spec = pl.BlockSpec((None, tm, tn), lambda i, j, k: (2 * (j // nj) + i // ni, i % ni, j % nj))
        o_shape = jax.ShapeDtypeStruct((N_DEV, M // 2, N // N_CHIP), out_dtype)
    else:
        o_spec = pl.BlockSpec((tm, tn), lambda i, j, k: (i, j))
        o_shape = jax.ShapeDtypeStruct((M, N), out_dtype)
    in_specs = [a_spec, b_spec]
    args = [a, b]
    if acc_in is not None:
        in_specs.append(o_spec)
        args.append(acc_in)
    out = _call(body, name=name, grid=(M // tm, N // tn, nk), in_specs=in_specs, out_specs=[o_spec],
                out_shape=[o_shape], args=args,
                scratch_shapes=[pltpu.VMEM((tm, tn), f32)] if use_scratch else [],
                sem=("parallel", "parallel", "arbitrary"), side=side)
    return out[0] if side is None else (out[0], out[1:])


def _embed_norm(x3, m64, w):
    _, S, D = x3.shape
    Lp = OFF + S

    def body(x_ref, m_ref, w_ref, h_ref, n_ref):
        i = pl.program_id(0)
        h = jnp.where(i == 0, m_ref[...], x_ref[...])
        h_ref[...] = h
        xh, _ = _rms(h)
        n_ref[...] = (xh * w_ref[...]).astype(bf16)

    row = pl.BlockSpec((ROWS, D), lambda i: (i, 0))
    return pl.pallas_call(
        body, name="embed_norm", grid=(Lp // ROWS,),
        in_specs=[pl.BlockSpec((None, ROWS, D), lambda i: (0, jnp.maximum(i - 1, 0), 0)),
                  pl.BlockSpec((ROWS, D), lambda i: (0, 0)),
                  pl.BlockSpec((1, D), lambda i: (0, 0))],
        out_specs=[row, row],
        out_shape=[jax.ShapeDtypeStruct((Lp, D), f32), jax.ShapeDtypeStruct((Lp, D), bf16)],
        compiler_params=_params("parallel"),
    )(x3, m64, w)


def _add_norm(h, d, w):
    Lp, D = h.shape
    tr = _tile(Lp, 256)

    def body(h_ref, d_ref, w_ref, o_ref, n_ref):
        h1 = h_ref[...] + d_ref[...]
        o_ref[...] = h1
        xh, _ = _rms(h1)
        n_ref[...] = (xh * w_ref[...]).astype(bf16)

    row = pl.BlockSpec((tr, D), lambda i: (i, 0))
    return pl.pallas_call(
        body, name="add_norm", grid=(Lp // tr,),
        in_specs=[row, row, pl.BlockSpec((1, D), lambda i: (0, 0))], out_specs=[row, row],
        out_shape=[jax.ShapeDtypeStruct((Lp, D), f32), jax.ShapeDtypeStruct((Lp, D), bf16)],
        compiler_params=_params("parallel"),
    )(h, d, w)


def _norm_bwd(dn, h, dh, w):
    Lp, D = h.shape
    tr = _tile(Lp, 256)

    def body(dn_ref, h_ref, dh_ref, w_ref, o_ref, ob_ref, gw_ref):
        i = pl.program_id(0)
        xh, r = _rms(h_ref[...])
        dn_ = dn_ref[...]
        o = dh_ref[...] + _rms_bwd(dn_, xh, r, w_ref[...])
        o_ref[...] = o
        ob_ref[...] = o.astype(bf16)
        gw = jnp.sum(dn_ * xh, axis=0, keepdims=True)

        @pl.when(i == 0)
        def _():
            gw_ref[...] = gw

        @pl.when(i > 0)
        def _():
            gw_ref[...] += gw

    row = pl.BlockSpec((tr, D), lambda i: (i, 0))
    vec = pl.BlockSpec((1, D), lambda i: (0, 0))
    return pl.pallas_call(
        body, name="norm_bwd", grid=(Lp // tr,), in_specs=[row, row, row, vec], out_specs=[row, row, vec],
        out_shape=[jax.ShapeDtypeStruct((Lp, D), f32), jax.ShapeDtypeStruct((Lp, D), bf16),
                   jax.ShapeDtypeStruct((1, D), f32)],
        compiler_params=_params("arbitrary"),
    )(dn, h, dh, w)


def _embed_norm_bwd(dn, h, dh, w, S):
    Lp, D = h.shape

    def body(dn_ref, h_ref, dh_ref, w_ref, gx_ref, gm_ref, gw_ref):
        i = pl.program_id(0)
        xh, r = _rms(h_ref[...])
        dn_ = dn_ref[...]
        d0 = dh_ref[...] + _rms_bwd(dn_, xh, r, w_ref[...])
        gx_ref[...] = d0
        gw = jnp.sum(dn_ * xh, axis=0, keepdims=True)

        @pl.when(i == 0)
        def _():
            gm_ref[...] = d0[PAD:OFF, :]
            gw_ref[...] = gw

        @pl.when(i > 0)
        def _():
            gw_ref[...] += gw

    row = pl.BlockSpec((ROWS, D), lambda i: (i, 0))
    vec = pl.BlockSpec((1, D), lambda i: (0, 0))
    return pl.pallas_call(
        body, name="embed_norm_bwd", grid=(Lp // ROWS,), in_specs=[row, row, row, vec],
        out_specs=[pl.BlockSpec((None, ROWS, D), lambda i: (0, jnp.maximum(i - 1, 0), 0)),
                   pl.BlockSpec((N_META, D), lambda i: (0, 0)), vec],
        out_shape=[jax.ShapeDtypeStruct((1, S, D), f32), jax.ShapeDtypeStruct((N_META, D), f32),
                   jax.ShapeDtypeStruct((1, D), f32)],
        compiler_params=_params("arbitrary"),
    )(dn, h, dh, w)


def _final(h1, ffn, tgt3, w):
    Lp, D = h1.shape

    def body(h_ref, f_ref, t_ref, w_ref, d_ref, db_ref, l_ref, gw_ref):
        i = pl.program_id(0)
        h2 = h_ref[...] + f_ref[...]
        xh, r = _rms(h2)
        w_ = w_ref[...]
        e = xh * w_ - t_ref[...]
        valid = (i > 0).astype(f32)
        loss = 0.5 * jnp.sum(jnp.mean(e * e, axis=-1, keepdims=True), axis=0, keepdims=True) * valid
        dy = e * (valid / D)
        d = _rms_bwd(dy, xh, r, w_)
        d_ref[...] = d
        db_ref[...] = d.astype(bf16)
        gw = jnp.sum(dy * xh, axis=0, keepdims=True)

        @pl.when(i == 0)
        def _():
            l_ref[...] = jnp.zeros_like(l_ref)
            gw_ref[...] = jnp.zeros_like(gw_ref)

        l_ref[...] += jnp.broadcast_to(loss, l_ref.shape)
        gw_ref[...] += gw

    row = pl.BlockSpec((ROWS, D), lambda i: (i, 0))
    vec = pl.BlockSpec((1, D), lambda i: (0, 0))
    return pl.pallas_call(
        body, name="final_loss", grid=(Lp // ROWS,),
        in_specs=[row, row, pl.BlockSpec((None, ROWS, D), lambda i: (0, jnp.maximum(i - 1, 0), 0)), vec],
        out_specs=[row, row, pl.BlockSpec((8, 128), lambda i: (0, 0)), vec],
        out_shape=[jax.ShapeDtypeStruct((Lp, D), f32), jax.ShapeDtypeStruct((Lp, D), bf16),
                   jax.ShapeDtypeStruct((8, 128), f32), jax.ShapeDtypeStruct((1, D), f32)],
        compiler_params=_params("arbitrary"),
    )(h1, ffn, tgt3, w)


def _swiglu(g, u):
    Lp, F = g.shape
    tr = _tile(Lp, 256)
    tc = _tile(F, 512, 128)

    def body(g_ref, u_ref, o_ref):
        o_ref[...] = (_silu(g_ref[...]) * u_ref[...]).astype(bf16)

    blk = pl.BlockSpec((tr, tc), lambda i, j: (i, j))
    return pl.pallas_call(
        body, name="swiglu", grid=(Lp // tr, F // tc), in_specs=[blk, blk], out_specs=blk,
        out_shape=jax.ShapeDtypeStruct((Lp, F), bf16),
        compiler_params=_params("parallel", "parallel"),
    )(g, u)


def _swiglu_bwd(g, u, dact):
    Lp, F = g.shape
    tr = _tile(Lp, 256)
    tc = _tile(F, 512, 128)

    def body(g_ref, u_ref, d_ref, dg_ref, du_ref):
        g_ = g_ref[...]
        d = d_ref[...]
        dg_ref[...] = (d * u_ref[...] * _dsilu(g_)).astype(bf16)
        du_ref[...] = (d * _silu(g_)).astype(bf16)

    blk = pl.BlockSpec((tr, tc), lambda i, j: (i, j))
    return pl.pallas_call(
        body, name="swiglu_bwd", grid=(Lp // tr, F // tc), in_specs=[blk, blk, blk], out_specs=[blk, blk],
        out_shape=[jax.ShapeDtypeStruct((Lp, F), bf16)] * 2,
        compiler_params=_params("parallel", "parallel"),
    )(g, u, dact)


def _gates(psm, w2p, gate_b, alog, dtb):
    Lp = psm.shape[0]
    tr = _tile(Lp, 256)

    def body(p_ref, w_ref, b_ref, a_ref, t_ref, gb_ref, la_ref):
        i = pl.program_id(0)
        psm_ = p_ref[...]
        lane = lax.broadcasted_iota(jnp.int32, psm_.shape, 1)
        rowi = lax.broadcasted_iota(jnp.int32, (tr, 1), 0) + i * tr
        g = -jnp.exp(a_ref[...]) * _softplus(psm_ + t_ref[...])
        beta = _sigmoid(psm_)
        gb = jnp.where(lane < GDN_H, g, jnp.where(lane < 2 * GDN_H, beta, 0.0))
        gb_ref[...] = gb * (rowi >= PAD).astype(f32)
        logit = _mmb(psm_, w_ref[...], NN) + b_ref[...]
        la_ref[...] = _log_sigmoid(logit) * (1.0 / GATE_NORMALIZER)

    row = pl.BlockSpec((tr, SM_W), lambda i: (i, 0))
    return pl.pallas_call(
        body, name="gates", grid=(Lp // tr,),
        in_specs=[row, pl.BlockSpec((SM_W, GLA_QK), lambda i: (0, 0)), pl.BlockSpec((1, GLA_QK), lambda i: (0, 0)),
                  pl.BlockSpec((1, SM_W), lambda i: (0, 0)), pl.BlockSpec((1, SM_W), lambda i: (0, 0))],
        out_specs=[row, pl.BlockSpec((tr, GLA_QK), lambda i: (i, 0))],
        out_shape=[jax.ShapeDtypeStruct((Lp, SM_W), f32), jax.ShapeDtypeStruct((Lp, GLA_QK), f32)],
        compiler_params=_params("parallel"),
    )(psm, w2p, gate_b, alog, dtb)


def _gates_bwd(psm, w2p, gate_b, alog, dtb, dgb, dla):
    Lp = psm.shape[0]
    tr = _tile(Lp, 256)

    def body(p_ref, w_ref, b_ref, a_ref, t_ref, dgb_ref, dla_ref, dp_ref, gw_ref, gb_ref, ga_ref, gt_ref):
        i = pl.program_id(0)
        psm_ = p_ref[...]
        lane = lax.broadcasted_iota(jnp.int32, psm_.shape, 1)
        rowi = lax.broadcasted_iota(jnp.int32, (tr, 1), 0) + i * tr
        d = dgb_ref[...] * (rowi >= PAD).astype(f32)
        ea = jnp.exp(a_ref[...])
        z = psm_ + t_ref[...]
        is_g = lane < GDN_H
        dz = jnp.where(is_g, -ea * _sigmoid(z) * d, 0.0)
        dalog = jnp.where(is_g, -ea * _softplus(z) * d, 0.0)
        beta = _sigmoid(psm_)
        dbeta = jnp.where(jnp.logical_and(lane >= GDN_H, lane < 2 * GDN_H), beta * (1.0 - beta) * d, 0.0)
        logit = _mmb(psm_, w_ref[...], NN) + b_ref[...]
        dlogit = dla_ref[...] * (_sigmoid(-logit) * (1.0 / GATE_NORMALIZER))
        dlr = _mmb(dlogit, w_ref[...], NT)
        dp_ref[...] = (dz + dbeta + dlr).astype(bf16)
        gw = _mmb(psm_, dlogit, TN)
        gb = jnp.sum(dlogit, axis=0, keepdims=True)
        ga = jnp.sum(dalog, axis=0, keepdims=True)
        gt = jnp.sum(dz, axis=0, keepdims=True)

        @pl.when(i == 0)
        def _():
            gw_ref[...] = gw
            gb_ref[...] = gb
            ga_ref[...] = ga
            gt_ref[...] = gt

        @pl.when(i > 0)
        def _():
            gw_ref[...] += gw
            gb_ref[...] += gb
            ga_ref[...] += ga
            gt_ref[...] += gt

    row = pl.BlockSpec((tr, SM_W), lambda i: (i, 0))
    wsp = pl.BlockSpec((SM_W, GLA_QK), lambda i: (0, 0))
    bsp = pl.BlockSpec((1, GLA_QK), lambda i: (0, 0))
    vsp = pl.BlockSpec((1, SM_W), lambda i: (0, 0))
    return pl.pallas_call(
        body, name="gates_bwd", grid=(Lp // tr,),
        in_specs=[row, wsp, bsp, vsp, vsp, row, pl.BlockSpec((tr, GLA_QK), lambda i: (i, 0))],
        out_specs=[row, wsp, bsp, vsp, vsp],
        out_shape=[jax.ShapeDtypeStruct((Lp, SM_W), bf16), jax.ShapeDtypeStruct((SM_W, GLA_QK), f32),
                   jax.ShapeDtypeStruct((1, GLA_QK), f32), jax.ShapeDtypeStruct((1, SM_W), f32),
                   jax.ShapeDtypeStruct((1, SM_W), f32)],
        compiler_params=_params("arbitrary"),
    )(psm, w2p, gate_b, alog, dtb, dgb, dla)


def _conv_pre(x_ext, w, n):
    rows = x_ext.shape[0]
    y = x_ext * w[CONV_K - 1:CONV_K, :]
    for s in range(1, CONV_K):
        y = y + pltpu.roll(x_ext, s, 0) * w[CONV_K - 1 - s:CONV_K - s, :]
    return y[rows - n:, :]


def _conv(proj, cw, side=None):
    Lp = proj.shape[0]
    W = cw.shape[1]
    tr = _tile(Lp, 256, 64)
    tc = _tile(W, 512, 128)
    c0 = C_QKV // tc

    def body(h_ref, x_ref, w_ref, o_ref):
        i = pl.program_id(1)
        halo = jnp.where(i == 0, 0.0, h_ref[...])
        x_ext = jnp.concatenate([halo, x_ref[...]], axis=0)
        o_ref[...] = _silu(_conv_pre(x_ext, w_ref[...], tr))

    out = _call(
        body, name="conv", grid=(W // tc, Lp // tr),
        in_specs=[pl.BlockSpec((8, tc), lambda j, i: (jnp.maximum(i * (tr // 8) - 1, 0), j + c0)),
                  pl.BlockSpec((tr, tc), lambda j, i: (i, j + c0)),
                  pl.BlockSpec((CONV_K, tc), lambda j, i: (0, j))],
        out_specs=[pl.BlockSpec((tr, tc), lambda j, i: (i, j))],
        out_shape=[jax.ShapeDtypeStruct((Lp, W), f32)], args=[proj, proj, cw],
        sem=("parallel", "parallel"), side=side)
    return out[0] if side is None else (out[0], out[1:])


def _conv_bwd(proj, cw, dy, dproj):
    Lp = proj.shape[0]
    W = cw.shape[1]
    tr = _tile(Lp, 256, 64)
    tc = _tile(W, 512, 128)
    c0 = C_QKV // tc
    nr = Lp // tr
    last8 = Lp // 8 - 1

    def body(xp_ref, x_ref, xn_ref, w_ref, d_ref, dn_ref, dproj_ref, o_ref, gw_ref):
        del dproj_ref
        i = pl.program_id(1)
        w = w_ref[...]
        xp = jnp.where(i == 0, 0.0, xp_ref[...])
        x_ext = jnp.concatenate([xp, x_ref[...], xn_ref[...]], axis=0)
        n = tr + 8
        pre = _conv_pre(x_ext, w, n)
        dn = jnp.where(i == nr - 1, 0.0, dn_ref[...])
        dpre = jnp.concatenate([d_ref[...], dn], axis=0) * _dsilu(pre)
        dx = dpre * w[CONV_K - 1:CONV_K, :]
        for s in range(1, CONV_K):
            dx = dx + pltpu.roll(dpre, n - s, 0) * w[CONV_K - 1 - s:CONV_K - s, :]
        o_ref[...] = dx[:tr, :].astype(bf16)
        dp = dpre[:tr, :]
        rows = []
        for k in range(CONV_K):
            xs = x_ext if k == CONV_K - 1 else pltpu.roll(x_ext, CONV_K - 1 - k, 0)
            rows.append(jnp.sum(dp * xs[8:8 + tr, :], axis=0, keepdims=True))
        gw = jnp.concatenate(rows, axis=0)

        @pl.when(i == 0)
        def _():
            gw_ref[...] = gw

        @pl.when(i > 0)
        def _():
            gw_ref[...] += gw

    cur = pl.BlockSpec((tr, tc), lambda j, i: (i, j))
    nxt = pl.BlockSpec((8, tc), lambda j, i: (jnp.minimum((i + 1) * (tr // 8), last8), j))
    pcur = pl.BlockSpec((tr, tc), lambda j, i: (i, j + c0))
    pprev = pl.BlockSpec((8, tc), lambda j, i: (jnp.maximum(i * (tr // 8) - 1, 0), j + c0))
    pnext = pl.BlockSpec((8, tc), lambda j, i: (jnp.minimum((i + 1) * (tr // 8), last8), j + c0))
    wsp = pl.BlockSpec((CONV_K, tc), lambda j, i: (0, j))
    return pl.pallas_call(
        body, name="conv_bwd", grid=(W // tc, nr),
        in_specs=[pprev, pcur, pnext, wsp, cur, nxt, pl.BlockSpec(memory_space=pl.ANY)],
        out_specs=[pcur, wsp],
        out_shape=[jax.ShapeDtypeStruct(dproj.shape, dproj.dtype), jax.ShapeDtypeStruct((CONV_K, W), f32)],
        input_output_aliases={6: 0},
        compiler_params=_params("parallel", "arbitrary"),
    )(proj, proj, proj, cw, dy, dy, dproj)


def _gdn_heads(x_ref, gbv):
    R = range(GDN_H)
    return ([x_ref[:, Q0 + h * GDN_DK:Q0 + (h + 1) * GDN_DK] for h in R],
            [x_ref[:, K0 + h * GDN_DK:K0 + (h + 1) * GDN_DK] for h in R],
            [x_ref[:, V0 + h * GDN_DV:V0 + (h + 1) * GDN_DV] for h in R],
            [gbv[:, GDN_H + h:GDN_H + h + 1] for h in R],
            [gbv[:, h:h + 1] for h in R])


def _gdn_fwd(qkvc, gb):
    Lp = qkvc.shape[0]
    N = Lp // GDN_C

    def body(x_ref, gb_ref, o_ref, sall_ref, s_scr):
        @pl.when(pl.program_id(0) == 0)
        def _():
            s_scr[...] = jnp.zeros_like(s_scr)

        R = range(GDN_H)
        Ss = [s_scr[h] for h in R]
        for h in R:
            sall_ref[0, h] = Ss[h]
        S2, o = _gdn_chunk(Ss, *_gdn_heads(x_ref, gb_ref[...]))
        for h in R:
            s_scr[h] = S2[h]
            o_ref[:, h * GDN_DV:(h + 1) * GDN_DV] = o[h]

    return pl.pallas_call(
        body, name="gdn_fwd", grid=(N,),
        in_specs=[pl.BlockSpec((GDN_C, QKV_W), lambda n: (n, 0)), pl.BlockSpec((GDN_C, SM_W), lambda n: (n, 0))],
        out_specs=[pl.BlockSpec((GDN_C, GDN_V), lambda n: (n, 0)),
                   pl.BlockSpec((1, GDN_H, GDN_DK, GDN_DV), lambda n: (n, 0, 0, 0))],
        out_shape=[jax.ShapeDtypeStruct((Lp, GDN_V), f32), jax.ShapeDtypeStruct((N, GDN_H, GDN_DK, GDN_DV), f32)],
        scratch_shapes=[pltpu.VMEM((GDN_H, GDN_DK, GDN_DV), f32)],
        compiler_params=_params("arbitrary"),
    )(qkvc, gb)


def _gdn_bwd(qkvc, gb, sall, do, side=None):
    Lp = qkvc.shape[0]
    N = Lp // GDN_C

    def body(x_ref, gb_ref, sall_ref, do_ref, dx_ref, dgb_ref, ds_scr):
        @pl.when(pl.program_id(0) == 0)
        def _():
            ds_scr[...] = jnp.zeros_like(ds_scr)

        R = range(GDN_H)
        lane = lax.broadcasted_iota(jnp.int32, (GDN_C, SM_W), 1)
        _, vjp = jax.vjp(_gdn_chunk, [sall_ref[0, h] for h in R], *_gdn_heads(x_ref, gb_ref[...]))
        dS, dq, dk, dv, dbeta, dg = vjp(([ds_scr[h] for h in R],
                                         [do_ref[:, h * GDN_DV:(h + 1) * GDN_DV] for h in R]))
        acc = jnp.zeros((GDN_C, SM_W), f32)
        for h in R:
            ds_scr[h] = dS[h]
            dx_ref[:, Q0 + h * GDN_DK:Q0 + (h + 1) * GDN_DK] = dq[h]
            dx_ref[:, K0 + h * GDN_DK:K0 + (h + 1) * GDN_DK] = dk[h]
            dx_ref[:, V0 + h * GDN_DV:V0 + (h + 1) * GDN_DV] = dv[h]
            acc = acc + jnp.where(lane == h, dg[h], 0.0) + jnp.where(lane == GDN_H + h, dbeta[h], 0.0)
        dgb_ref[...] = acc

    rev = lambda n: (N - 1 - n, 0)
    out = _call(
        body, name="gdn_bwd", grid=(N,),
        in_specs=[pl.BlockSpec((GDN_C, QKV_W), rev), pl.BlockSpec((GDN_C, SM_W), rev),
                  pl.BlockSpec((1, GDN_H, GDN_DK, GDN_DV), lambda n: (N - 1 - n, 0, 0, 0)),
                  pl.BlockSpec((GDN_C, GDN_V), rev)],
        out_specs=[pl.BlockSpec((GDN_C, QKV_W), rev), pl.BlockSpec((GDN_C, SM_W), rev)],
        out_shape=[jax.ShapeDtypeStruct((Lp, QKV_W), f32), jax.ShapeDtypeStruct((Lp, SM_W), f32)],
        scratch_shapes=[pltpu.VMEM((GDN_H, GDN_DK, GDN_DV), f32)], args=[qkvc, gb, sall, do],
        sem=("arbitrary",), side=side)
    return out[0], out[1], out[2:]


GLA_SUB = ROWS // GLA_C


def _gla_slices(h):
    sq = slice(h * GLA_DK, (h + 1) * GLA_DK)
    sk = slice(GLA_QK + h * GLA_DK, GLA_QK + (h + 1) * GLA_DK)
    sv = slice(2 * GLA_QK + h * GLA_DV, 2 * GLA_QK + (h + 1) * GLA_DV)
    return sq, sk, sv


def _gla_heads(x_ref, la_ref, r):
    sl = [_gla_slices(h) for h in range(GLA_H)]
    return ([x_ref[r, s[0]] for s in sl], [x_ref[r, s[1]] for s in sl], [x_ref[r, s[2]] for s in sl],
            [la_ref[r, s[0]] for s in sl])


def _gla_fwd(proj, la):
    Lp = proj.shape[0]
    NB = Lp // ROWS

    def body(x_ref, la_ref, o_ref, sall_ref, s_scr):
        @pl.when(pl.program_id(0) == 0)
        def _():
            s_scr[...] = jnp.zeros_like(s_scr)

        def sub(c, carry):
            r = pl.ds(pl.multiple_of(c * GLA_C, GLA_C), GLA_C)
            R = range(GLA_H)
            Sts = [s_scr[h] for h in R]
            for h in R:
                sall_ref[c, h] = Sts[h]
            St2, o = _gla_chunk(Sts, *_gla_heads(x_ref, la_ref, r))
            for h in R:
                s_scr[h] = St2[h]
                o_ref[r, h * GLA_DV:(h + 1) * GLA_DV] = o[h]
            return carry

        lax.fori_loop(0, GLA_SUB, sub, 0)

    return pl.pallas_call(
        body, name="gla_fwd", grid=(NB,),
        in_specs=[pl.BlockSpec((ROWS, G_W), lambda n: (n, C_G // G_W)),
                  pl.BlockSpec((ROWS, GLA_QK), lambda n: (n, 0))],
        out_specs=[pl.BlockSpec((ROWS, GLA_V), lambda n: (n, 0)),
                   pl.BlockSpec((GLA_SUB, GLA_H, GLA_DV, GLA_DK), lambda n: (n, 0, 0, 0))],
        out_shape=[jax.ShapeDtypeStruct((Lp, GLA_V), f32),
                   jax.ShapeDtypeStruct((Lp // GLA_C, GLA_H, GLA_DV, GLA_DK), f32)],
        scratch_shapes=[pltpu.VMEM((GLA_H, GLA_DV, GLA_DK), f32)],
        compiler_params=_params("arbitrary"),
    )(proj, la)


def _gla_bwd(proj, la, sall, do, dproj, side=None):
    Lp = proj.shape[0]
    NB = Lp // ROWS

    def body(x_ref, la_ref, sall_ref, do_ref, dproj_ref, dx_ref, dla_ref, ds_scr):
        del dproj_ref

        @pl.when(pl.program_id(0) == 0)
        def _():
            ds_scr[...] = jnp.zeros_like(ds_scr)

        def sub(ci, carry):
            c = GLA_SUB - 1 - ci
            r = pl.ds(pl.multiple_of(c * GLA_C, GLA_C), GLA_C)
            R = range(GLA_H)
            _, vjp = jax.vjp(_gla_chunk, [sall_ref[c, h] for h in R], *_gla_heads(x_ref, la_ref, r))
            dS, dq, dk, dv, dl = vjp(([ds_scr[h] for h in R], [do_ref[r, h * GLA_DV:(h + 1) * GLA_DV] for h in R]))
            for h in R:
                sq, sk, sv = _gla_slices(h)
                ds_scr[h] = dS[h]
                dx_ref[r, sq] = dq[h].astype(bf16)
                dx_ref[r, sk] = dk[h].astype(bf16)
                dx_ref[r, sv] = dv[h].astype(bf16)
                dla_ref[r, sq] = dl[h]
            return carry

        lax.fori_loop(0, GLA_SUB, sub, 0)

    x_spec = pl.BlockSpec((ROWS, G_W), lambda n: (NB - 1 - n, C_G // G_W))
    rev = lambda n: (NB - 1 - n, 0)
    out = _call(
        body, name="gla_bwd", grid=(NB,),
        in_specs=[x_spec, pl.BlockSpec((ROWS, GLA_QK), rev),
                  pl.BlockSpec((GLA_SUB, GLA_H, GLA_DV, GLA_DK), lambda n: (NB - 1 - n, 0, 0, 0)),
                  pl.BlockSpec((ROWS, GLA_V), rev), _ANY],
        out_specs=[x_spec, pl.BlockSpec((ROWS, GLA_QK), rev)],
        out_shape=[jax.ShapeDtypeStruct(dproj.shape, dproj.dtype), jax.ShapeDtypeStruct((Lp, GLA_QK), f32)],
        aliases={4: 0}, scratch_shapes=[pltpu.VMEM((GLA_H, GLA_DV, GLA_DK), f32)],
        args=[proj, la, sall, do, dproj], sem=("arbitrary",), side=side)
    return out[0], out[1], out[2:]


def _gated_norm_fn(og, ol, zr, wg, wl):
    outs = []
    for h in range(GDN_H):
        s = slice(h * GDN_DV, (h + 1) * GDN_DV)
        outs.append(_rms(og[:, s])[0] * wg * _silu(zr[:, s]))
    for h in range(GLA_H):
        s = slice(h * GLA_DV, (h + 1) * GLA_DV)
        sr = slice(GDN_V + h * GLA_DV, GDN_V + (h + 1) * GLA_DV)
        outs.append(_rms(ol[:, s])[0] * wl * _silu(zr[:, sr]))
    return jnp.concatenate(outs, axis=-1)


def _gated_norm(og, ol, proj, wg, wl):
    Lp = og.shape[0]
    tr = _tile(Lp, 256)

    def body(og_ref, ol_ref, zr_ref, wg_ref, wl_ref, o_ref):
        o_ref[...] = _gated_norm_fn(og_ref[...], ol_ref[...], zr_ref[...], wg_ref[...], wl_ref[...]).astype(bf16)

    return pl.pallas_call(
        body, name="gated_norm", grid=(Lp // tr,),
        in_specs=[pl.BlockSpec((tr, GDN_V), lambda i: (i, 0)), pl.BlockSpec((tr, GLA_V), lambda i: (i, 0)),
                  pl.BlockSpec((tr, ZR_W), lambda i: (i, C_ZR // ZR_W)),
                  pl.BlockSpec((1, GDN_DV), lambda i: (0, 0)), pl.BlockSpec((1, GLA_DV), lambda i: (0, 0))],
        out_specs=pl.BlockSpec((tr, ZR_W), lambda i: (i, 0)),
        out_shape=jax.ShapeDtypeStruct((Lp, ZR_W), bf16),
        compiler_params=_params("parallel"),
    )(og, ol, proj, wg, wl)


def _gated_norm_bwd(og, ol, proj, wg, wl, dmix):
    Lp = og.shape[0]
    tr = _tile(Lp, 128)

    def body(og_ref, ol_ref, zr_ref, wg_ref, wl_ref, d_ref, dog_ref, dol_ref, dzr_ref, gwg_ref, gwl_ref):
        i = pl.program_id(0)
        _, vjp = jax.vjp(_gated_norm_fn, og_ref[...], ol_ref[...], zr_ref[...], wg_ref[...], wl_ref[...])
        dog, dol, dzr, gwg, gwl = vjp(d_ref[...])
        dog_ref[...] = dog
        dol_ref[...] = dol
        dzr_ref[...] = dzr.astype(bf16)

        @pl.when(i == 0)
        def _():
            gwg_ref[...] = gwg
            gwl_ref[...] = gwl

        @pl.when(i > 0)
        def _():
            gwg_ref[...] += gwg
            gwl_ref[...] += gwl

    og_spec = pl.BlockSpec((tr, GDN_V), lambda i: (i, 0))
    ol_spec = pl.BlockSpec((tr, GLA_V), lambda i: (i, 0))
    zr_spec = pl.BlockSpec((tr, ZR_W), lambda i: (i, C_ZR // ZR_W))
    vg = pl.BlockSpec((1, GDN_DV), lambda i: (0, 0))
    vl = pl.BlockSpec((1, GLA_DV), lambda i: (0, 0))
    return pl.pallas_call(
        body, name="gated_norm_bwd", grid=(Lp // tr,),
        in_specs=[og_spec, ol_spec, zr_spec, vg, vl, pl.BlockSpec((tr, ZR_W), lambda i: (i, 0))],
        out_specs=[og_spec, ol_spec, zr_spec, vg, vl],
        out_shape=[jax.ShapeDtypeStruct((Lp, GDN_V), f32), jax.ShapeDtypeStruct((Lp, GLA_V), f32),
                   jax.ShapeDtypeStruct((Lp, C_END), bf16),
                   jax.ShapeDtypeStruct((1, GDN_DV), f32), jax.ShapeDtypeStruct((1, GLA_DV), f32)],
        compiler_params=_params("arbitrary"),
    )(og, ol, proj, wg, wl, dmix)


def _adamw(g, w, m, v, name):
    R, C = g.shape
    tr = _tile(R, 256, 8) if R % 8 == 0 and R > 256 else R
    c1 = 1.0 - ADAM_B1 ** ADAM_STEP
    c2 = 1.0 - ADAM_B2 ** ADAM_STEP

    def body(g_ref, w_ref, m_ref, v_ref, d_ref, mo_ref, vo_ref):
        g_ = g_ref[...]
        m2 = ADAM_B1 * m_ref[...] + (1.0 - ADAM_B1) * g_
        v2 = ADAM_B2 * v_ref[...] + (1.0 - ADAM_B2) * (g_ * g_)
        mo_ref[...] = m2
        vo_ref[...] = v2
        d_ref[...] = -ADAM_LR * ((m2 / c1) / (jnp.sqrt(v2 / c2) + ADAM_EPS) + ADAM_WD * w_ref[...])

    blk = pl.BlockSpec((tr, C), lambda i: (i, 0))
    return pl.pallas_call(
        body, name=name, grid=(R // tr,), in_specs=[blk] * 4, out_specs=[blk] * 3,
        out_shape=[jax.ShapeDtypeStruct((R, C), f32)] * 3,
        compiler_params=_params("parallel"),
    )(g, w, m, v)


def _sum_slots(r, name):
    n, R, C = r.shape
    tr = _tile(R, 128, 16) if R % 16 == 0 and R > 128 else R

    def body(r_ref, o_ref):
        acc = r_ref[0].astype(f32)
        for s in range(1, n):
            acc = acc + r_ref[s].astype(f32)
        o_ref[...] = acc

    return pl.pallas_call(
        body, name=name, grid=(R // tr,),
        in_specs=[pl.BlockSpec((n, tr, C), lambda i: (0, i, 0))],
        out_specs=pl.BlockSpec((tr, C), lambda i: (i, 0)),
        out_shape=jax.ShapeDtypeStruct((R, C), f32),
        compiler_params=_params("parallel"),
    )(r)


class _Siblings:
    def __init__(self, arrays):
        self.arrays = list(arrays)
        self.n = len(self.arrays)
        self.out_shape = [jax.ShapeDtypeStruct((2,) + a.shape, a.dtype) for a in self.arrays]
        self.sems = [pltpu.SemaphoreType.DMA((self.n,)), pltpu.SemaphoreType.DMA((self.n,)),
                     pltpu.SemaphoreType.DMA((self.n,))]

    def hooks(self, ins, outs, send, recv, lsem):
        def copies():
            x, y, c = lax.axis_index("x"), lax.axis_index("y"), lax.axis_index("c")
            out = []
            for a in range(self.n):
                out.append((pltpu.make_async_copy(ins[a], outs[a].at[c], lsem.at[a]), None))
                mk = lambda dst, a=a: pltpu.make_async_remote_copy(
                    src_ref=ins[a], dst_ref=dst, send_sem=send.at[a], recv_sem=recv.at[a],
                    device_id=(x, y, 1 - c), device_id_type=MESH)
                out.append((mk(outs[a].at[c]), mk(outs[a].at[1 - c])))
            return out

        return _start_wait(copies)


def _comm_now(name, sides):
    total = sum(s.n for s in sides)

    def body(*refs):
        ins, outs, sems = refs[:total], refs[total:2 * total], refs[2 * total:]
        hooks, o = [], 0
        for i, s in enumerate(sides):
            hooks.append(s.hooks(ins[o:o + s.n], outs[o:o + s.n], *sems[3 * i:3 * i + 3]))
            o += s.n
        for start, _ in hooks:
            start()
        for _, wait in hooks:
            wait()

    out = pl.pallas_call(
        body, name=name, in_specs=[_ANY] * total, out_specs=[_ANY] * total,
        out_shape=[sh for s in sides for sh in s.out_shape], scratch_shapes=[sm for s in sides for sm in s.sems],
    )(*[a for s in sides for a in s.arrays])
    res, o = [], 0
    for s in sides:
        res.append(list(out[o:o + s.n]))
        o += s.n
    return res


def _cat_cols(g):
    return jnp.concatenate([g[i] for i in range(N_CHIP)], axis=-1)


def _row_slabs(a):
    return a.reshape(N_DEV, a.shape[0] // N_DEV, a.shape[1])


def _w_in_columns(g_wp, g_wsm):
    return jnp.concatenate([g_wp[:, C_QKV:C_END], g_wp[:, C_ZR:C_ZR + GDN_V], g_wsm[:, :SM_LR],
                            g_wp[:, C_G:C_G + G_W], g_wp[:, C_ZR + GDN_V:C_ZR + ZR_W],
                            g_wsm[:, SM_LR:SM_LR + GATE_RANK]], axis=1)


def _step(x, loss_target, p, w_in, meta, conv_w, w2, shard):
    _, S, D = x.shape
    wp = jnp.concatenate([w_in[:, R_Z:R_AB], w_in[:, R_GR:R_LR], w_in[:, R_G:R_GR], w_in[:, R_QKV:R_Z]], axis=1)
    wsm = jnp.concatenate([w_in[:, R_AB:R_G], w_in[:, R_LR:R_END],
                           jnp.zeros((D, SM_W - SM_LR - GATE_RANK), w_in.dtype)], axis=1)
    w2p = jnp.pad(w2, ((SM_LR, SM_W - SM_LR - GATE_RANK), (0, 0)))
    alog_p = jnp.pad(p["gdn_a_log"], ((0, 0), (0, SM_W - GDN_H)))
    dtb_p = jnp.pad(p["gdn_dt_bias"], ((0, 0), (0, SM_W - GDN_H)))
    m64 = jnp.concatenate([jnp.zeros((PAD, D), f32), meta], axis=0)
    gate_b, gdn_norm_w, gla_norm_w = p["gla_gate_b"], p["gdn_norm_w"], p["gla_norm_w"]

    h0, n1 = _embed_norm(x, m64, p["attn_norm_w"])
    proj, (w_out4, w_gate4) = _mm(n1, wp, "nn", "proj", side=_Gather([shard["w_out"], shard["w_gate"]]))
    w_out, w_gate = w_out4.reshape(-1, D), _cat_cols(w_gate4)
    psm = _mm(n1, wsm, "nn", "proj_small")
    gb, la = _gates(psm, w2p, gate_b, alog_p, dtb_p)
    qkvc, (w_up4,) = _conv(proj, conv_w, side=_Gather([shard["w_up"]]))
    w_up = _cat_cols(w_up4)
    og, sall = _gdn_fwd(qkvc, gb)
    ol, stall = _gla_fwd(proj, la)
    mixed = _gated_norm(og, ol, proj, gdn_norm_w, gla_norm_w)
    attn = _mm(mixed, w_out, "nn", "out_proj")
    h1, n2 = _add_norm(h0, attn, p["ffn_norm_w"])
    gate = _mm(n2, w_gate, "nn", "ffn_gate")
    up, (w_down4,) = _mm(n2, w_up, "nn", "ffn_up", side=_Gather([shard["w_down"]]))
    w_down = w_down4.reshape(-1, D)
    act = _swiglu(gate, up)
    ffn = _mm(act, w_down, "nn", "ffn_down", tk_cap=1408)
    dh2, dh2b, lossp, g_final = _final(h1, ffn, loss_target, p["final_norm_w"])

    dact = _mm(dh2b, w_down, "nt", "d_act")
    g_down = _mm(act, dh2b, "tn", "g_w_down", tk_cap=1408, out_dtype=bf16)
    dg, du = _swiglu_bwd(gate, up, dact)
    g_gate = _mm(n2, dg, "tn", "g_w_gate", tn_cap=1408, tk_cap=1408, out_dtype=bf16, col_slabs=True)
    g_up = _mm(n2, du, "tn", "g_w_up", tn_cap=1408, tk_cap=1408, out_dtype=bf16, col_slabs=True)
    dn2 = _mm(dg, w_gate, "nt", "d_n2_gate", tk_cap=1408)
    dn2 = _mm(du, w_up, "nt", "d_n2_up", tk_cap=1408, acc_in=dn2)
    dh1, dh1b, g_ffn_norm = _norm_bwd(dn2, h1, dh2, p["ffn_norm_w"])
    dmix = _mm(dh1b, w_out, "nt", "d_mixed")
    g_out = _mm(mixed, dh1b, "tn", "g_w_out", tk_cap=1408, out_dtype=bf16)
    dog, dol, dproj, g_gdn_norm, g_gla_norm = _gated_norm_bwd(og, ol, proj, gdn_norm_w, gla_norm_w, dmix)
    dproj, dla, (r_down, r_gate) = _gla_bwd(proj, la, stall, dol, dproj,
                                            side=_Exchange([_row_slabs(g_down), g_gate]))
    dqkvc, dgb, (r_up, r_out) = _gdn_bwd(qkvc, gb, sall, dog, side=_Exchange([g_up, _row_slabs(g_out)]))
    dproj, g_conv = _conv_bwd(proj, conv_w, dqkvc, dproj)
    dpsm, g_w2p, g_gate_b, g_alog, g_dtb = _gates_bwd(psm, w2p, gate_b, alog_p, dtb_p, dgb, dla)
    g_wp = _mm(n1, dproj, "tn", "g_w_in", tm_cap=2048, tk_cap=1408, out_dtype=bf16)
    g_wsm = _mm(n1, dpsm, "tn", "g_w_in_small", tm_cap=2048, tk_cap=1408, out_dtype=bf16)
    dn1, r_in = _mm(dproj, wp, "nt", "d_n1", tk_cap=1024, side=_Exchange([_row_slabs(g_wp), _row_slabs(g_wsm)]))
    dn1 = _mm(dpsm, wsm, "nt", "d_n1_small", acc_in=dn1)
    grad_x, g_meta, g_attn_norm = _embed_norm_bwd(dn1, h0, dh1, p["attn_norm_w"], S)

    received = dict(w_in=tuple(r_in), w_gate=r_gate, w_up=r_up, w_out=r_out, w_down=r_down)
    small = dict(
        meta_tokens=g_meta, attn_norm_w=g_attn_norm, gdn_conv_w=g_conv, gdn_a_log=g_alog[:, :GDN_H],
        gdn_dt_bias=g_dtb[:, :GDN_H], gdn_norm_w=g_gdn_norm, gla_gate_w2=g_w2p[SM_LR:SM_LR + GATE_RANK],
        gla_gate_b=g_gate_b, gla_norm_w=g_gla_norm, ffn_norm_w=g_ffn_norm, final_norm_w=g_final)
    return lossp[0, 0], grad_x, received, small


_WEIGHTS = ("meta_tokens", "attn_norm_w", "w_in", "gdn_conv_w", "gdn_a_log", "gdn_dt_bias", "gdn_norm_w",
            "gla_gate_w2", "gla_gate_b", "gla_norm_w", "w_out", "ffn_norm_w", "w_gate", "w_up", "w_down",
            "final_norm_w")
_BIG_COLS = ("w_in", "w_gate", "w_up")
_BIG_ROWS = ("w_out", "w_down")
_SMALL_SHARDED = ("meta_tokens", "gdn_conv_w", "gla_gate_w2")


def kernel(x, meta_tokens, attn_norm_w, w_in, gdn_conv_w, gdn_a_log, gdn_dt_bias, gdn_norm_w, gla_gate_w2, gla_gate_b, gla_norm_w, w_out, ffn_norm_w, w_gate, w_up, w_down, final_norm_w, loss_target, m_meta_tokens, m_attn_norm_w, m_w_in, m_gdn_conv_w, m_gdn_a_log, m_gdn_dt_bias, m_gdn_norm_w, m_gla_gate_w2, m_gla_gate_b, m_gla_norm_w, m_w_out, m_ffn_norm_w, m_w_gate, m_w_up, m_w_down, m_final_norm_w, v_meta_tokens, v_attn_norm_w, v_w_in, v_gdn_conv_w, v_gdn_a_log, v_gdn_dt_bias, v_gdn_norm_w, v_gla_gate_w2, v_gla_gate_b, v_gla_norm_w, v_w_out, v_ffn_norm_w, v_w_gate, v_w_up, v_w_down, v_final_norm_w):
    w = dict(meta_tokens=meta_tokens, attn_norm_w=attn_norm_w, w_in=w_in, gdn_conv_w=gdn_conv_w, gdn_a_log=gdn_a_log,
             gdn_dt_bias=gdn_dt_bias, gdn_norm_w=gdn_norm_w, gla_gate_w2=gla_gate_w2, gla_gate_b=gla_gate_b,
             gla_norm_w=gla_norm_w, w_out=w_out, ffn_norm_w=ffn_norm_w, w_gate=w_gate, w_up=w_up, w_down=w_down,
             final_norm_w=final_norm_w)
    m = dict(meta_tokens=m_meta_tokens, attn_norm_w=m_attn_norm_w, w_in=m_w_in, gdn_conv_w=m_gdn_conv_w,
             gdn_a_log=m_gdn_a_log, gdn_dt_bias=m_gdn_dt_bias, gdn_norm_w=m_gdn_norm_w, gla_gate_w2=m_gla_gate_w2,
             gla_gate_b=m_gla_gate_b, gla_norm_w=m_gla_norm_w, w_out=m_w_out, ffn_norm_w=m_ffn_norm_w,
             w_gate=m_w_gate, w_up=m_w_up, w_down=m_w_down, final_norm_w=m_final_norm_w)
    v = dict(meta_tokens=v_meta_tokens, attn_norm_w=v_attn_norm_w, w_in=v_w_in, gdn_conv_w=v_gdn_conv_w,
             gdn_a_log=v_gdn_a_log, gdn_dt_bias=v_gdn_dt_bias, gdn_norm_w=v_gdn_norm_w, gla_gate_w2=v_gla_gate_w2,
             gla_gate_b=v_gla_gate_b, gla_norm_w=v_gla_norm_w, w_out=v_w_out, ffn_norm_w=v_ffn_norm_w,
             w_gate=v_w_gate, w_up=v_w_up, w_down=v_w_down, final_norm_w=v_final_norm_w)
    chip = 2 * lax.axis_index("x") + lax.axis_index("y")

    def two_d(a):
        return a.reshape(1, -1) if a.ndim == 1 else a.reshape(-1, a.shape[-1])

    w2d = {k: two_d(a) for k, a in w.items()}
    big = _BIG_COLS + _BIG_ROWS
    small = tuple(k for k in _WEIGHTS if k not in big)

    first = ("w_in",) + _SMALL_SHARDED
    gathered, = _comm_now("gather_first", [_Gather([w2d[k].astype(bf16) if k in big else w2d[k] for k in first])])
    full = {k: _cat_cols(a) for k, a in zip(first, gathered)}
    shard = {k: w2d[k].astype(bf16) for k in big if k != "w_in"}
    lossp, grad_x, received, g = _step(x, loss_target, {k: w2d[k] for k in small}, full["w_in"], full["meta_tokens"],
                                       full["gdn_conv_w"], full["gla_gate_w2"], shard)
    loss = lax.psum(lossp, ("x", "y", "c"))

    sizes = [g[k].size for k in small]
    total = sum(sizes)
    rows = -(-total // 1024)
    rows += (-rows) % 8
    packed = jnp.concatenate([g[k].reshape(-1) for k in small] + [jnp.zeros((rows * 1024 - total,), f32)])
    r_wp, r_wsm = received["w_in"]
    s_in = _w_in_columns(_sum_slots(r_wp, "sum_w_in"), _sum_slots(r_wsm, "sum_w_in_small"))
    in_by_chip = s_in.reshape(s_in.shape[0], N_CHIP, -1).transpose(1, 0, 2)
    halved = tuple(k for k in big if k != "w_in")
    halves, (in8, packed8) = _comm_now(
        "exchange_tail", [_Siblings([_sum_slots(received[k], "sum_" + k) for k in halved]),
                          _Exchange([], [packed.reshape(rows, 1024)], by_chip=[in_by_chip])])
    red = {k: h.reshape(w2d[k].shape) for k, h in zip(halved, halves)}
    red["w_in"] = in8.reshape(w2d["w_in"].shape)
    psum_small = _sum_slots(packed8, "sum_small").reshape(-1)
    off = 0
    for k, n in zip(small, sizes):
        a = psum_small[off:off + n].reshape(g[k].shape)
        off += n
        if k in _SMALL_SHARDED:
            c = w2d[k].shape[1]
            a = lax.dynamic_slice_in_dim(a, chip * c, c, axis=1)
        red[k] = a

    grads, deltas, new_m, new_v = [], [], [], []
    for k in _WEIGHTS:
        d, m2, v2 = _adamw(red[k], w2d[k], two_d(m[k]), two_d(v[k]), "adamw_" + k)
        shape = w[k].shape
        grads.append(red[k].reshape(shape))
        deltas.append(d.reshape(shape))
        new_m.append(m2.reshape(shape))
        new_v.append(v2.reshape(shape))
    return (loss, grad_x, *grads, *deltas, *new_m, *new_v)
```

```python
import functools

import jax
import jax.numpy as jnp
from jax import lax
from jax.experimental import pallas as pl
from jax.experimental.pallas import tpu as pltpu

f32 = jnp.float32
bf16 = jnp.bfloat16
HI = lax.Precision.HIGHEST
HIGH = lax.Precision.HIGH
MESH = pl.DeviceIdType.MESH

N_META = 16
CONV_K = 4
GDN_H, GDN_DK, GDN_DV, GDN_C = 8, 128, 128, 64
GLA_H, GLA_DK, GLA_DV, GLA_C = 4, 128, 256, 16
GATE_RANK = 16
GATE_NORMALIZER = 16.0
EPS = 1e-6
GDN_QK = GDN_H * GDN_DK
GDN_V = GDN_H * GDN_DV
GLA_QK = GLA_H * GLA_DK
GLA_V = GLA_H * GLA_DV
PAD = (-N_META) % GDN_C
OFF = PAD + N_META
ROWS = 64

R_QKV, R_Z, R_AB, R_G, R_GR, R_LR, R_END = 0, 3072, 4096, 4112, 6160, 7184, 7200
C_ZR, C_G, C_QKV, C_END = 0, 2048, 4096, 7168
ZR_W = GDN_V + GLA_V
G_W = 2 * GLA_QK + GLA_V
QKV_W = 2 * GDN_QK + GDN_V
Q0, K0, V0 = 0, GDN_QK, 2 * GDN_QK
SM_W = 128
SM_LR = 2 * GDN_H

ADAM_LR, ADAM_B1, ADAM_B2, ADAM_EPS, ADAM_WD, ADAM_STEP = 0.001, 0.9, 0.999, 1e-08, 0.01, 10

VMEM_LIMIT_V7X = 56 * 1024 * 1024
N_DEV = 8
N_CHIP = 4


def _params(*sem):
    return pltpu.CompilerParams(dimension_semantics=sem, vmem_limit_bytes=VMEM_LIMIT_V7X)


def _tile(n, cap, mult=16):
    best = None
    for d in range(mult, min(n, cap) + 1, mult):
        if n % d == 0:
            best = d
    assert best is not None, (n, cap, mult)
    return best


NN = ((1,), (0,))
NT = ((1,), (1,))
TN = ((0,), (0,))


def _dot(a, b, dims, prec=None):
    return lax.dot_general(a, b, (dims, ((), ())), precision=prec, preferred_element_type=f32)


def _mmb(a, b, dims):
    return _dot(a.astype(bf16), b.astype(bf16), dims)


def _sigmoid(x):
    return jax.nn.sigmoid(x)


def _silu(x):
    return x * _sigmoid(x)


def _dsilu(x):
    s = _sigmoid(x)
    return s * (1.0 + x * (1.0 - s))


def _log1p_exp_neg_abs(x):
    t = jnp.exp(-jnp.abs(x))
    u = 1.0 + t
    d = u - 1.0
    return jnp.where(d == 0.0, t, jnp.log(u) * (t / jnp.where(d == 0.0, 1.0, d)))


def _softplus(x):
    return jnp.maximum(x, 0.0) + _log1p_exp_neg_abs(x)


def _log_sigmoid(x):
    return jnp.minimum(x, 0.0) - _log1p_exp_neg_abs(x)


def _rms(x):
    r = lax.rsqrt(jnp.mean(x * x, axis=-1, keepdims=True) + EPS)
    return x * r, r


def _rms_bwd(dy, xh, r, w):
    t = dy * w
    return r * (t - xh * jnp.mean(t * xh, axis=-1, keepdims=True))


def _l2n(x):
    return x * lax.rsqrt(jnp.sum(x * x, axis=-1, keepdims=True) + EPS)


INV_LEAF = 8


def _same_block(C, b):
    sh = b.bit_length() - 1
    row = lax.broadcasted_iota(jnp.int32, (C, C), 0)
    col = lax.broadcasted_iota(jnp.int32, (C, C), 1)
    return lax.shift_right_logical(row, sh) == lax.shift_right_logical(col, sh)


def _tri_inv_impl(As):
    C = As[0].shape[0]
    R = range(len(As))
    row = lax.broadcasted_iota(jnp.int32, (C, C), 0)
    col = lax.broadcasted_iota(jnp.int32, (C, C), 1)
    eye = (row == col).astype(f32)
    b = INV_LEAF
    inner = _same_block(C, b)
    leaf = [jnp.where(inner, As[h], 0.0) for h in R]
    d = [eye - leaf[h] for h in R]
    pw = leaf
    n = 2
    while n < b:
        pw = [_dot(pw[h], pw[h], NN, HIGH) for h in R]
        d = [_dot(d[h], eye + pw[h], NN, HIGH) for h in R]
        n *= 2
    while b < C:
        outer = _same_block(C, 2 * b)
        level = jnp.logical_and(outer, jnp.logical_not(inner))
        ed = [_dot(jnp.where(level, As[h], 0.0), d[h], NN, HIGH) for h in R]
        d = [d[h] - _dot(d[h], ed[h], NN, HIGH) for h in R]
        inner = outer
        b *= 2
    return d


@jax.custom_vjp
def _tri_inv(As):
    return _tri_inv_impl(As)


def _tri_inv_fwd(As):
    d = _tri_inv_impl(As)
    return d, d


def _tri_inv_bwd(d, g):
    R = range(len(d))
    t = [_dot(d[h], g[h], TN, HIGH) for h in R]
    return ([-_dot(t[h], d[h], NT, HIGH) for h in R],)


_tri_inv.defvjp(_tri_inv_fwd, _tri_inv_bwd)


def _gdn_chunk(Ss, qrs, krs, vs, betas, gs):
    C, dk = qrs[0].shape
    R = range(len(Ss))
    row = lax.broadcasted_iota(jnp.int32, (C, C), 0)
    col = lax.broadcasted_iota(jnp.int32, (C, C), 1)
    causal = row >= col
    strict = row > col
    cf = causal.astype(f32)
    q = [_l2n(qrs[h]) * (dk ** -0.5) for h in R]
    k = [_l2n(krs[h]) for h in R]
    mc = [_dot(cf, jnp.broadcast_to(gs[h], (C, C)), NN, HI) for h in R]
    gc = [mc[h][:, 0:1] for h in R]
    decay = [jnp.where(causal, jnp.exp(jnp.where(causal, mc[h] - mc[h].T, 0.0)), 0.0) for h in R]
    kb = [k[h] * betas[h] for h in R]
    a = [jnp.where(strict, _mmb(kb[h], k[h], NT) * decay[h], 0.0) for h in R]
    p = _tri_inv(a)
    egc = [jnp.exp(gc[h]) for h in R]
    u = [_mmb(p[h], vs[h] * betas[h], NN) for h in R]
    w = [_mmb(p[h], kb[h] * egc[h], NN) for h in R]
    qk = [jnp.where(causal, _mmb(q[h], k[h], NT) * decay[h], 0.0) for h in R]
    v_new = [u[h] - _mmb(w[h], Ss[h], NN) for h in R]
    o = [_mmb(q[h] * egc[h], Ss[h], NN) + _mmb(qk[h], v_new[h], NN) for h in R]
    gl = [gc[h][C - 1:C, :] for h in R]
    kd = [k[h] * jnp.exp(gl[h] - gc[h]) for h in R]
    S2 = [Ss[h] * jnp.exp(gl[h]) + _mmb(kd[h], v_new[h], TN) for h in R]
    return S2, o


def _gla_chunk(Sts, qrs, ks, vs, las):
    C, dk = qrs[0].shape
    R = range(len(Sts))
    row = lax.broadcasted_iota(jnp.int32, (C, C), 0)
    col = lax.broadcasted_iota(jnp.int32, (C, C), 1)
    cf = (row >= col).astype(f32)
    q = [qrs[h] * (dk ** -0.5) for h in R]
    b = [_dot(cf, las[h], NN, HI) for h in R]
    o = [_mmb(q[h] * jnp.exp(b[h]), Sts[h], NT) for h in R]
    ri = lax.broadcasted_iota(jnp.int32, (C, dk), 0)
    for j in range(C):
        m = ri >= j
        e = [jnp.where(m, jnp.exp(jnp.where(m, b[h] - b[h][j:j + 1, :], 0.0)), 0.0) for h in R]
        s = [jnp.sum(q[h] * ks[h][j:j + 1, :] * e[h], axis=-1, keepdims=True) for h in R]
        o = [o[h] + s[h] * vs[h][j:j + 1, :] for h in R]
    bl = [b[h][C - 1:C, :] for h in R]
    St2 = [Sts[h] * jnp.exp(bl[h]) + _mmb(vs[h], ks[h] * jnp.exp(bl[h] - b[h]), TN) for h in R]
    return St2, o


_ANY = pl.BlockSpec(memory_space=pl.ANY)


class _Gather:
    def __init__(self, arrays):
        self.arrays = list(arrays)
        self.n = len(self.arrays)
        self.out_shape = [jax.ShapeDtypeStruct((N_CHIP,) + a.shape, a.dtype) for a in self.arrays]
        self.sems = [pltpu.SemaphoreType.DMA((self.n, 3)), pltpu.SemaphoreType.DMA((self.n, 3)),
                     pltpu.SemaphoreType.DMA((self.n,))]

    def hooks(self, ins, outs, send, recv, lsem):
        def copies():
            x, y, c = lax.axis_index("x"), lax.axis_index("y"), lax.axis_index("c")
            me = 2 * x + y
            out = []
            for a in range(self.n):
                out.append((pltpu.make_async_copy(ins[a], outs[a].at[me], lsem.at[a]), None))
                for j, (px, py) in enumerate([(1 - x, y), (x, 1 - y), (1 - x, 1 - y)]):
                    mk = lambda dst, a=a, j=j, px=px, py=py: pltpu.make_async_remote_copy(
                        src_ref=ins[a], dst_ref=dst, send_sem=send.at[a, j], recv_sem=recv.at[a, j],
                        device_id=(px, py, c), device_id_type=MESH)
                    out.append((mk(outs[a].at[me]), mk(outs[a].at[2 * px + py])))
            return out

        return _start_wait(copies)


class _Exchange:
    def __init__(self, slotted, shared=(), by_chip=()):
        self.arrays = list(slotted) + list(by_chip) + list(shared)
        self.ns, self.nc = len(slotted), len(by_chip)
        self.n = len(self.arrays)
        self.out_shape = [jax.ShapeDtypeStruct(a.shape, a.dtype) for a in slotted]
        self.out_shape += [jax.ShapeDtypeStruct((N_DEV,) + a.shape[1:], a.dtype) for a in by_chip]
        self.out_shape += [jax.ShapeDtypeStruct((N_DEV,) + b.shape, b.dtype) for b in shared]
        self.sems = [pltpu.SemaphoreType.DMA((self.n, N_DEV - 1)), pltpu.SemaphoreType.DMA((self.n, N_DEV - 1)),
                     pltpu.SemaphoreType.DMA((self.n,))]

    def hooks(self, ins, outs, send, recv, lsem):
        def copies():
            x, y, c = lax.axis_index("x"), lax.axis_index("y"), lax.axis_index("c")
            me = 4 * x + 2 * y + c

            def src(a, dev):
                tx, ty, tc = dev
                if a < self.ns:
                    return ins[a].at[4 * tx + 2 * ty + tc]
                return ins[a].at[2 * tx + ty] if a < self.ns + self.nc else ins[a]

            out = []
            for a in range(self.n):
                out.append((pltpu.make_async_copy(src(a, (x, y, c)), outs[a].at[me], lsem.at[a]), None))
                for o in range(1, N_DEV):
                    dev = (1 - x if o & 4 else x, 1 - y if o & 2 else y, 1 - c if o & 1 else c)
                    t = 4 * dev[0] + 2 * dev[1] + dev[2]
                    mk = lambda dst, a=a, o=o, dev=dev: pltpu.make_async_remote_copy(
                        src_ref=src(a, dev), dst_ref=dst, send_sem=send.at[a, o - 1], recv_sem=recv.at[a, o - 1],
                        device_id=dev, device_id_type=MESH)
                    out.append((mk(outs[a].at[me]), mk(outs[a].at[t])))
            return out

        return _start_wait(copies)


def _start_wait(copies):
    def start():
        for s, _ in copies():
            s.start()

    def wait():
        for s, w in copies():
            (s if w is None else w).wait()

    return start, wait


def _call(body, *, name, grid, in_specs, out_specs, out_shape, args, sem, scratch_shapes=(), aliases=None, side=None):
    in_specs, out_specs, out_shape, args = list(in_specs), list(out_specs), list(out_shape), list(args)
    scratch_shapes = list(scratch_shapes)
    aliases = aliases or {}
    if side is None:
        return pl.pallas_call(
            body, name=name, grid=grid, in_specs=in_specs, out_specs=out_specs, out_shape=out_shape,
            scratch_shapes=scratch_shapes, input_output_aliases=aliases, compiler_params=_params(*sem))(*args)
    n_in, n_out, n_scr, ns = len(in_specs), len(out_specs), len(scratch_shapes), side.n

    def full_body(*refs):
        ins, refs = refs[:n_in], refs[n_in:]
        s_in, refs = refs[:ns], refs[ns:]
        outs, refs = refs[:n_out], refs[n_out:]
        s_out, refs = refs[:ns], refs[ns:]
        scr, sems = refs[:n_scr], refs[n_scr:]
        start, wait = side.hooks(s_in, s_out, *sems)
        ids = [pl.program_id(d) for d in range(len(grid))]
        first = functools.reduce(jnp.logical_and, [i == 0 for i in ids])
        last = functools.reduce(jnp.logical_and, [i == g - 1 for i, g in zip(ids, grid)])
        pl.when(first)(start)
        body(*ins, *outs, *scr)
        pl.when(last)(wait)

    return pl.pallas_call(
        full_body, name=name, grid=grid, in_specs=in_specs + [_ANY] * ns, out_specs=out_specs + [_ANY] * ns,
        out_shape=out_shape + side.out_shape, scratch_shapes=scratch_shapes + side.sems,
        input_output_aliases=aliases, compiler_params=_params(*(["arbitrary"] * len(grid))))(*args, *side.arrays)


def _mm(a, b, mode, name, *, tm_cap=1408, tn_cap=1024, tk_cap=2048, out_dtype=f32, acc_in=None, side=None,
        col_slabs=False):
    if mode == "nn":
        (M, K), (K2, N) = a.shape, b.shape
    elif mode == "nt":
        (M, K), (N, K2) = a.shape, b.shape
    else:
        (K, M), (K2, N) = a.shape, b.shape
    assert K == K2, (name, a.shape, b.shape)
    tm = _tile(M // 2 if col_slabs else M, tm_cap)
    tn = _tile(N // N_CHIP if col_slabs else N, tn_cap, 128)
    tk = _tile(K, tk_cap, 128 if K % 128 == 0 else 16)
    nk = K // tk
    dims = {"nn": NN, "nt": NT, "tn": TN}[mode]
    use_scratch = nk > 1 and out_dtype != f32

    def body(*refs):
        if acc_in is not None:
            a_ref, b_ref, c_ref, o_ref, *scr = refs
        else:
            a_ref, b_ref, o_ref, *scr = refs
            c_ref = None
        p = _mmb(a_ref[...], b_ref[...], dims)
        if nk == 1:
            if c_ref is not None:
                p = p + c_ref[...]
            o_ref[...] = p.astype(out_dtype)
            return
        k = pl.program_id(2)
        acc = scr[0] if use_scratch else o_ref

        @pl.when(k == 0)
        def _():
            acc[...] = p if c_ref is None else p + c_ref[...]

        @pl.when(k > 0)
        def _():
            acc[...] += p

        if use_scratch:
            @pl.when(k == nk - 1)
            def _():
                o_ref[...] = acc[...].astype(out_dtype)

    if mode == "tn":
        a_spec = pl.BlockSpec((tk, tm), lambda i, j, k: (k, i))
    else:
        a_spec = pl.BlockSpec((tm, tk), lambda i, j, k: (i, k))
    if mode == "nt":
        b_spec = pl.BlockSpec((tn, tk), lambda i, j, k: (j, k))
    else:
        b_spec = pl.BlockSpec((tk, tn), lambda i, j, k: (k, j))
    if col_slabs:
        assert acc_in is None
        ni, nj = M // 2 // tm, N // N_CHIP // tn
        o_spec = pl.BlockSpec((None, tm, tn), lambda i, j, k: (2 * (j // nj) + i // ni, i % ni, j % nj))
        o_shape = jax.ShapeDtypeStruct((N_DEV, M // 2, N // N_CHIP), out_dtype)
    else:
        o_spec = pl.BlockSpec((tm, tn), lambda i, j, k: (i, j))
        o_shape = jax.ShapeDtypeStruct((M, N), out_dtype)
    in_specs = [a_spec, b_spec]
    args = [a, b]
    if acc_in is not None:
        in_specs.append(o_spec)
        args.append(acc_in)
    out = _call(body, name=name, grid=(M // tm, N // tn, nk), in_specs=in_specs, out_specs=[o_spec],
                out_shape=[o_shape], args=args,
                scratch_shapes=[pltpu.VMEM((tm, tn), f32)] if use_scratch else [],
                sem=("parallel", "parallel", "arbitrary"), side=side)
    return out[0] if side is None else (out[0], out[1:])


def _embed_norm(x3, m64, w):
    _, S, D = x3.shape
    Lp = OFF + S

    def body(x_ref, m_ref, w_ref, h_ref, n_ref):
        i = pl.program_id(0)
        h = jnp.where(i == 0, m_ref[...], x_ref[...])
        h_ref[...] = h
        xh, _ = _rms(h)
        n_ref[...] = (xh * w_ref[...]).astype(bf16)

    row = pl.BlockSpec((ROWS, D), lambda i: (i, 0))
    return pl.pallas_call(
        body, name="embed_norm", grid=(Lp // ROWS,),
        in_specs=[pl.BlockSpec((None, ROWS, D), lambda i: (0, jnp.maximum(i - 1, 0), 0)),
                  pl.BlockSpec((ROWS, D), lambda i: (0, 0)),
                  pl.BlockSpec((1, D), lambda i: (0, 0))],
        out_specs=[row, row],
        out_shape=[jax.ShapeDtypeStruct((Lp, D), f32), jax.ShapeDtypeStruct((Lp, D), bf16)],
        compiler_params=_params("parallel"),
    )(x3, m64, w)


def _add_norm(h, d, w):
    Lp, D = h.shape
    tr = _tile(Lp, 256)

    def body(h_ref, d_ref, w_ref, o_ref, n_ref):
        h1 = h_ref[...] + d_ref[...]
        o_ref[...] = h1
        xh, _ = _rms(h1)
        n_ref[...] = (xh * w_ref[...]).astype(bf16)

    row = pl.BlockSpec((tr, D), lambda i: (i, 0))
    return pl.pallas_call(
        body, name="add_norm", grid=(Lp // tr,),
        in_specs=[row, row, pl.BlockSpec((1, D), lambda i: (0, 0))], out_specs=[row, row],
        out_shape=[jax.ShapeDtypeStruct((Lp, D), f32), jax.ShapeDtypeStruct((Lp, D), bf16)],
        compiler_params=_params("parallel"),
    )(h, d, w)


def _norm_bwd(dn, h, dh, w):
    Lp, D = h.shape
    tr = _tile(Lp, 256)

    def body(dn_ref, h_ref, dh_ref, w_ref, o_ref, ob_ref, gw_ref):
        i = pl.program_id(0)
        xh, r = _rms(h_ref[...])
        dn_ = dn_ref[...]
        o = dh_ref[...] + _rms_bwd(dn_, xh, r, w_ref[...])
        o_ref[...] = o
        ob_ref[...] = o.astype(bf16)
        gw = jnp.sum(dn_ * xh, axis=0, keepdims=True)

        @pl.when(i == 0)
        def _():
            gw_ref[...] = gw

        @pl.when(i > 0)
        def _():
            gw_ref[...] += gw

    row = pl.BlockSpec((tr, D), lambda i: (i, 0))
    vec = pl.BlockSpec((1, D), lambda i: (0, 0))
    return pl.pallas_call(
        body, name="norm_bwd", grid=(Lp // tr,), in_specs=[row, row, row, vec], out_specs=[row, row, vec],
        out_shape=[jax.ShapeDtypeStruct((Lp, D), f32), jax.ShapeDtypeStruct((Lp, D), bf16),
                   jax.ShapeDtypeStruct((1, D), f32)],
        compiler_params=_params("arbitrary"),
    )(dn, h, dh, w)


def _embed_norm_bwd(dn, h, dh, w, S):
    Lp, D = h.shape

    def body(dn_ref, h_ref, dh_ref, w_ref, gx_ref, gm_ref, gw_ref):
        i = pl.program_id(0)
        xh, r = _rms(h_ref[...])
        dn_ = dn_ref[...]
        d0 = dh_ref[...] + _rms_bwd(dn_, xh, r, w_ref[...])
        gx_ref[...] = d0
        gw = jnp.sum(dn_ * xh, axis=0, keepdims=True)

        @pl.when(i == 0)
        def _():
            gm_ref[...] = d0[PAD:OFF, :]
            gw_ref[...] = gw

        @pl.when(i > 0)
        def _():
            gw_ref[...] += gw

    row = pl.BlockSpec((ROWS, D), lambda i: (i, 0))
    vec = pl.BlockSpec((1, D), lambda i: (0, 0))
    return pl.pallas_call(
        body, name="embed_norm_bwd", grid=(Lp // ROWS,), in_specs=[row, row, row, vec],
        out_specs=[pl.BlockSpec((None, ROWS, D), lambda i: (0, jnp.maximum(i - 1, 0), 0)),
                   pl.BlockSpec((N_META, D), lambda i: (0, 0)), vec],
        out_shape=[jax.ShapeDtypeStruct((1, S, D), f32), jax.ShapeDtypeStruct((N_META, D), f32),
                   jax.ShapeDtypeStruct((1, D), f32)],
        compiler_params=_params("arbitrary"),
    )(dn, h, dh, w)


def _final(h1, ffn, tgt3, w):
    Lp, D = h1.shape

    def body(h_ref, f_ref, t_ref, w_ref, d_ref, db_ref, l_ref, gw_ref):
        i = pl.program_id(0)
        h2 = h_ref[...] + f_ref[...]
        xh, r = _rms(h2)
        w_ = w_ref[...]
        e = xh * w_ - t_ref[...]
        valid = (i > 0).astype(f32)
        loss = 0.5 * jnp.sum(jnp.mean(e * e, axis=-1, keepdims=True), axis=0, keepdims=True) * valid
        dy = e * (valid / D)
        d = _rms_bwd(dy, xh, r, w_)
        d_ref[...] = d
        db_ref[...] = d.astype(bf16)
        gw = jnp.sum(dy * xh, axis=0, keepdims=True)

        @pl.when(i == 0)
        def _():
            l_ref[...] = jnp.zeros_like(l_ref)
            gw_ref[...] = jnp.zeros_like(gw_ref)

        l_ref[...] += jnp.broadcast_to(loss, l_ref.shape)
        gw_ref[...] += gw

    row = pl.BlockSpec((ROWS, D), lambda i: (i, 0))
    vec = pl.BlockSpec((1, D), lambda i: (0, 0))
    return pl.pallas_call(
        body, name="final_loss", grid=(Lp // ROWS,),
        in_specs=[row, row, pl.BlockSpec((None, ROWS, D), lambda i: (0, jnp.maximum(i - 1, 0), 0)), vec],
        out_specs=[row, row, pl.BlockSpec((8, 128), lambda i: (0, 0)), vec],
        out_shape=[jax.ShapeDtypeStruct((Lp, D), f32), jax.ShapeDtypeStruct((Lp, D), bf16),
                   jax.ShapeDtypeStruct((8, 128), f32), jax.ShapeDtypeStruct((1, D), f32)],
        compiler_params=_params("arbitrary"),
    )(h1, ffn, tgt3, w)


def _ffn_in(n, w_gate, w_up, side=None):
    M, K = n.shape
    F = w_gate.shape[1]
    tm = _tile(M, 1408)
    tn = _tile(F, 512, 128)

    def body(a_ref, bg_ref, bu_ref, act_ref, g_ref, u_ref):
        a = a_ref[...]
        g = _mmb(a, bg_ref[...], NN)
        u = _mmb(a, bu_ref[...], NN)
        act_ref[...] = (_silu(g) * u).astype(bf16)
        g_ref[...] = g.astype(bf16)
        u_ref[...] = u.astype(bf16)

    wsp = pl.BlockSpec((K, tn), lambda i, j: (0, j))
    osp = pl.BlockSpec((tm, tn), lambda i, j: (i, j))
    out = _call(body, name="ffn_in", grid=(M // tm, F // tn),
                in_specs=[pl.BlockSpec((tm, K), lambda i, j: (i, 0)), wsp, wsp], out_specs=[osp] * 3,
                out_shape=[jax.ShapeDtypeStruct((M, F), bf16)] * 3, args=[n, w_gate, w_up],
                sem=("parallel", "parallel"), side=side)
    return out[0], out[1], out[2], out[3:]


def _ffn_dact(d, w_down, g, u):
    M, K = d.shape
    F = w_down.shape[0]
    tm = _tile(M, 1408)
    tn = _tile(F, 512, 128)

    def body(d_ref, w_ref, g_ref, u_ref, dg_ref, du_ref):
        da = _mmb(d_ref[...], w_ref[...], NT)
        g_ = g_ref[...].astype(f32)
        dg_ref[...] = (da * u_ref[...].astype(f32) * _dsilu(g_)).astype(bf16)
        du_ref[...] = (da * _silu(g_)).astype(bf16)

    osp = pl.BlockSpec((tm, tn), lambda i, j: (i, j))
    return pl.pallas_call(
        body, name="ffn_dact", grid=(M // tm, F // tn),
        in_specs=[pl.BlockSpec((tm, K), lambda i, j: (i, 0)), pl.BlockSpec((tn, K), lambda i, j: (j, 0)), osp, osp],
        out_specs=[osp, osp], out_shape=[jax.ShapeDtypeStruct((M, F), bf16)] * 2,
        compiler_params=_params("parallel", "parallel"),
    )(d, w_down, g, u)


def _gates(psm, w2p, gate_b, alog, dtb):
    Lp = psm.shape[0]
    tr = _tile(Lp, 256)

    def body(p_ref, w_ref, b_ref, a_ref, t_ref, gb_ref, la_ref):
        i = pl.program_id(0)
        psm_ = p_ref[...]
        lane = lax.broadcasted_iota(jnp.int32, psm_.shape, 1)
        rowi = lax.broadcasted_iota(jnp.int32, (tr, 1), 0) + i * tr
        g = -jnp.exp(a_ref[...]) * _softplus(psm_ + t_ref[...])
        beta = _sigmoid(psm_)
        gb = jnp.where(lane < GDN_H, g, jnp.where(lane < 2 * GDN_H, beta, 0.0))
        gb_ref[...] = gb * (rowi >= PAD).astype(f32)
        logit = _mmb(psm_, w_ref[...], NN) + b_ref[...]
        la_ref[...] = _log_sigmoid(logit) * (1.0 / GATE_NORMALIZER)

    row = pl.BlockSpec((tr, SM_W), lambda i: (i, 0))
    return pl.pallas_call(
        body, name="gates", grid=(Lp // tr,),
        in_specs=[row, pl.BlockSpec((SM_W, GLA_QK), lambda i: (0, 0)), pl.BlockSpec((1, GLA_QK), lambda i: (0, 0)),
                  pl.BlockSpec((1, SM_W), lambda i: (0, 0)), pl.BlockSpec((1, SM_W), lambda i: (0, 0))],
        out_specs=[row, pl.BlockSpec((tr, GLA_QK), lambda i: (i, 0))],
        out_shape=[jax.ShapeDtypeStruct((Lp, SM_W), f32), jax.ShapeDtypeStruct((Lp, GLA_QK), f32)],
        compiler_params=_params("parallel"),
    )(psm, w2p, gate_b, alog, dtb)


def _gates_bwd(psm, w2p, gate_b, alog, dtb, dgb, dla):
    Lp = psm.shape[0]
    tr = _tile(Lp, 256)

    def body(p_ref, w_ref, b_ref, a_ref, t_ref, dgb_ref, dla_ref, dp_ref, gw_ref, gb_ref, ga_ref, gt_ref):
        i = pl.program_id(0)
        psm_ = p_ref[...]
        lane = lax.broadcasted_iota(jnp.int32, psm_.shape, 1)
        rowi = lax.broadcasted_iota(jnp.int32, (tr, 1), 0) + i * tr
        d = dgb_ref[...] * (rowi >= PAD).astype(f32)
        ea = jnp.exp(a_ref[...])
        z = psm_ + t_ref[...]
        is_g = lane < GDN_H
        dz = jnp.where(is_g, -ea * _sigmoid(z) * d, 0.0)
        dalog = jnp.where(is_g, -ea * _softplus(z) * d, 0.0)
        beta = _sigmoid(psm_)
        dbeta = jnp.where(jnp.logical_and(lane >= GDN_H, lane < 2 * GDN_H), beta * (1.0 - beta) * d, 0.0)
        logit = _mmb(psm_, w_ref[...], NN) + b_ref[...]
        dlogit = dla_ref[...] * (_sigmoid(-logit) * (1.0 / GATE_NORMALIZER))
        dlr = _mmb(dlogit, w_ref[...], NT)
        dp_ref[...] = (dz + dbeta + dlr).astype(bf16)
        gw = _mmb(psm_, dlogit, TN)
        gb = jnp.sum(dlogit, axis=0, keepdims=True)
        ga = jnp.sum(dalog, axis=0, keepdims=True)
        gt = jnp.sum(dz, axis=0, keepdims=True)

        @pl.when(i == 0)
        def _():
            gw_ref[...] = gw
            gb_ref[...] = gb
            ga_ref[...] = ga
            gt_ref[...] = gt

        @pl.when(i > 0)
        def _():
            gw_ref[...] += gw
            gb_ref[...] += gb
            ga_ref[...] += ga
            gt_ref[...] += gt

    row = pl.BlockSpec((tr, SM_W), lambda i: (i, 0))
    wsp = pl.BlockSpec((SM_W, GLA_QK), lambda i: (0, 0))
    bsp = pl.BlockSpec((1, GLA_QK), lambda i: (0, 0))
    vsp = pl.BlockSpec((1, SM_W), lambda i: (0, 0))
    return pl.pallas_call(
        body, name="gates_bwd", grid=(Lp // tr,),
        in_specs=[row, wsp, bsp, vsp, vsp, row, pl.BlockSpec((tr, GLA_QK), lambda i: (i, 0))],
        out_specs=[row, wsp, bsp, vsp, vsp],
        out_shape=[jax.ShapeDtypeStruct((Lp, SM_W), bf16), jax.ShapeDtypeStruct((SM_W, GLA_QK), f32),
                   jax.ShapeDtypeStruct((1, GLA_QK), f32), jax.ShapeDtypeStruct((1, SM_W), f32),
                   jax.ShapeDtypeStruct((1, SM_W), f32)],
        compiler_params=_params("arbitrary"),
    )(psm, w2p, gate_b, alog, dtb, dgb, dla)


def _conv_pre(x_ext, w, n):
    rows = x_ext.shape[0]
    y = x_ext * w[CONV_K - 1:CONV_K, :]
    for s in range(1, CONV_K):
        y = y + pltpu.roll(x_ext, s, 0) * w[CONV_K - 1 - s:CONV_K - s, :]
    return y[rows - n:, :]


def _conv(proj, cw, side=None):
    Lp = proj.shape[0]
    W = cw.shape[1]
    tr = _tile(Lp, 256, 64)
    tc = _tile(W, 1024, 128)
    c0 = C_QKV // tc

    def body(h_ref, x_ref, w_ref, o_ref):
        i = pl.program_id(1)
        halo = jnp.where(i == 0, 0.0, h_ref[...])
        x_ext = jnp.concatenate([halo, x_ref[...]], axis=0)
        o_ref[...] = _silu(_conv_pre(x_ext, w_ref[...], tr))

    out = _call(
        body, name="conv", grid=(W // tc, Lp // tr),
        in_specs=[pl.BlockSpec((8, tc), lambda j, i: (jnp.maximum(i * (tr // 8) - 1, 0), j + c0)),
                  pl.BlockSpec((tr, tc), lambda j, i: (i, j + c0)),
                  pl.BlockSpec((CONV_K, tc), lambda j, i: (0, j))],
        out_specs=[pl.BlockSpec((tr, tc), lambda j, i: (i, j))],
        out_shape=[jax.ShapeDtypeStruct((Lp, W), f32)], args=[proj, proj, cw],
        sem=("parallel", "parallel"), side=side)
    return out[0] if side is None else (out[0], out[1:])


def _conv_bwd(proj, cw, dy, dproj, side=None):
    Lp = proj.shape[0]
    W = cw.shape[1]
    tr = _tile(Lp, 256, 64)
    tc = _tile(W, 1024, 128)
    c0 = C_QKV // tc
    nr = Lp // tr
    last8 = Lp // 8 - 1

    def body(xp_ref, x_ref, xn_ref, w_ref, d_ref, dn_ref, dproj_ref, o_ref, gw_ref):
        del dproj_ref
        i = pl.program_id(1)
        w = w_ref[...]
        xp = jnp.where(i == 0, 0.0, xp_ref[...])
        x_ext = jnp.concatenate([xp, x_ref[...], xn_ref[...]], axis=0)
        n = tr + 8
        pre = _conv_pre(x_ext, w, n)
        dn = jnp.where(i == nr - 1, 0.0, dn_ref[...])
        dpre = jnp.concatenate([d_ref[...], dn], axis=0) * _dsilu(pre)
        dx = dpre * w[CONV_K - 1:CONV_K, :]
        for s in range(1, CONV_K):
            dx = dx + pltpu.roll(dpre, n - s, 0) * w[CONV_K - 1 - s:CONV_K - s, :]
        o_ref[...] = dx[:tr, :].astype(bf16)
        dp = dpre[:tr, :]
        rows = []
        for k in range(CONV_K):
            xs = x_ext if k == CONV_K - 1 else pltpu.roll(x_ext, CONV_K - 1 - k, 0)
            rows.append(jnp.sum(dp * xs[8:8 + tr, :], axis=0, keepdims=True))
        gw = jnp.concatenate(rows, axis=0)

        @pl.when(i == 0)
        def _():
            gw_ref[...] = gw

        @pl.when(i > 0)
        def _():
            gw_ref[...] += gw

    cur = pl.BlockSpec((tr, tc), lambda j, i: (i, j))
    nxt = pl.BlockSpec((8, tc), lambda j, i: (jnp.minimum((i + 1) * (tr // 8), last8), j))
    pcur = pl.BlockSpec((tr, tc), lambda j, i: (i, j + c0))
    pprev = pl.BlockSpec((8, tc), lambda j, i: (jnp.maximum(i * (tr // 8) - 1, 0), j + c0))
    pnext = pl.BlockSpec((8, tc), lambda j, i: (jnp.minimum((i + 1) * (tr // 8), last8), j + c0))
    wsp = pl.BlockSpec((CONV_K, tc), lambda j, i: (0, j))
    out = _call(
        body, name="conv_bwd", grid=(W // tc, nr),
        in_specs=[pprev, pcur, pnext, wsp, cur, nxt, _ANY], out_specs=[pcur, wsp],
        out_shape=[jax.ShapeDtypeStruct(dproj.shape, dproj.dtype), jax.ShapeDtypeStruct((CONV_K, W), f32)],
        aliases={6: 0}, args=[proj, proj, proj, cw, dy, dy, dproj], sem=("parallel", "arbitrary"), side=side)
    return out[0], out[1], out[2:]


def _gdn_heads(x_ref, gbv):
    R = range(GDN_H)
    return ([x_ref[:, Q0 + h * GDN_DK:Q0 + (h + 1) * GDN_DK] for h in R],
            [x_ref[:, K0 + h * GDN_DK:K0 + (h + 1) * GDN_DK] for h in R],
            [x_ref[:, V0 + h * GDN_DV:V0 + (h + 1) * GDN_DV] for h in R],
            [gbv[:, GDN_H + h:GDN_H + h + 1] for h in R],
            [gbv[:, h:h + 1] for h in R])


def _gdn_fwd(qkvc, gb, side=None):
    Lp = qkvc.shape[0]
    N = Lp // GDN_C

    def body(x_ref, gb_ref, o_ref, sall_ref, s_scr):
        @pl.when(pl.program_id(0) == 0)
        def _():
            s_scr[...] = jnp.zeros_like(s_scr)

        R = range(GDN_H)
        Ss = [s_scr[h] for h in R]
        for h in R:
            sall_ref[0, h] = Ss[h]
        S2, o = _gdn_chunk(Ss, *_gdn_heads(x_ref, gb_ref[...]))
        for h in R:
            s_scr[h] = S2[h]
            o_ref[:, h * GDN_DV:(h + 1) * GDN_DV] = o[h]

    out = _call(
        body, name="gdn_fwd", grid=(N,),
        in_specs=[pl.BlockSpec((GDN_C, QKV_W), lambda n: (n, 0)), pl.BlockSpec((GDN_C, SM_W), lambda n: (n, 0))],
        out_specs=[pl.BlockSpec((GDN_C, GDN_V), lambda n: (n, 0)),
                   pl.BlockSpec((1, GDN_H, GDN_DK, GDN_DV), lambda n: (n, 0, 0, 0))],
        out_shape=[jax.ShapeDtypeStruct((Lp, GDN_V), f32), jax.ShapeDtypeStruct((N, GDN_H, GDN_DK, GDN_DV), f32)],
        scratch_shapes=[pltpu.VMEM((GDN_H, GDN_DK, GDN_DV), f32)], args=[qkvc, gb], sem=("arbitrary",), side=side)
    return out[0], out[1], out[2:]


def _gdn_bwd(qkvc, gb, sall, do, side=None):
    Lp = qkvc.shape[0]
    N = Lp // GDN_C

    def body(x_ref, gb_ref, sall_ref, do_ref, dx_ref, dgb_ref, ds_scr):
        @pl.when(pl.program_id(0) == 0)
        def _():
            ds_scr[...] = jnp.zeros_like(ds_scr)

        R = range(GDN_H)
        lane = lax.broadcasted_iota(jnp.int32, (GDN_C, SM_W), 1)
        _, vjp = jax.vjp(_gdn_chunk, [sall_ref[0, h] for h in R], *_gdn_heads(x_ref, gb_ref[...]))
        dS, dq, dk, dv, dbeta, dg = vjp(([ds_scr[h] for h in R],
                                         [do_ref[:, h * GDN_DV:(h + 1) * GDN_DV] for h in R]))
        acc = jnp.zeros((GDN_C, SM_W), f32)
        for h in R:
            ds_scr[h] = dS[h]
            dx_ref[:, Q0 + h * GDN_DK:Q0 + (h + 1) * GDN_DK] = dq[h]
            dx_ref[:, K0 + h * GDN_DK:K0 + (h + 1) * GDN_DK] = dk[h]
            dx_ref[:, V0 + h * GDN_DV:V0 + (h + 1) * GDN_DV] = dv[h]
            acc = acc + jnp.where(lane == h, dg[h], 0.0) + jnp.where(lane == GDN_H + h, dbeta[h], 0.0)
        dgb_ref[...] = acc

    rev = lambda n: (N - 1 - n, 0)
    out = _call(
        body, name="gdn_bwd", grid=(N,),
        in_specs=[pl.BlockSpec((GDN_C, QKV_W), rev), pl.BlockSpec((GDN_C, SM_W), rev),
                  pl.BlockSpec((1, GDN_H, GDN_DK, GDN_DV), lambda n: (N - 1 - n, 0, 0, 0)),
                  pl.BlockSpec((GDN_C, GDN_V), rev)],
        out_specs=[pl.BlockSpec((GDN_C, QKV_W), rev), pl.BlockSpec((GDN_C, SM_W), rev)],
        out_shape=[jax.ShapeDtypeStruct((Lp, QKV_W), f32), jax.ShapeDtypeStruct((Lp, SM_W), f32)],
        scratch_shapes=[pltpu.VMEM((GDN_H, GDN_DK, GDN_DV), f32)], args=[qkvc, gb, sall, do],
        sem=("arbitrary",), side=side)
    return out[0], out[1], out[2:]


GLA_SUB = ROWS // GLA_C


def _gla_slices(h):
    sq = slice(h * GLA_DK, (h + 1) * GLA_DK)
    sk = slice(GLA_QK + h * GLA_DK, GLA_QK + (h + 1) * GLA_DK)
    sv = slice(2 * GLA_QK + h * GLA_DV, 2 * GLA_QK + (h + 1) * GLA_DV)
    return sq, sk, sv


def _gla_heads(x_ref, la_ref, r):
    sl = [_gla_slices(h) for h in range(GLA_H)]
    return ([x_ref[r, s[0]] for s in sl], [x_ref[r, s[1]] for s in sl], [x_ref[r, s[2]] for s in sl],
            [la_ref[r, s[0]] for s in sl])


def _gla_fwd(proj, la):
    Lp = proj.shape[0]
    NB = Lp // ROWS

    def body(x_ref, la_ref, o_ref, sall_ref, s_scr):
        @pl.when(pl.program_id(0) == 0)
        def _():
            s_scr[...] = jnp.zeros_like(s_scr)

        def sub(c, carry):
            r = pl.ds(pl.multiple_of(c * GLA_C, GLA_C), GLA_C)
            R = range(GLA_H)
            Sts = [s_scr[h] for h in R]
            for h in R:
                sall_ref[c, h] = Sts[h]
            St2, o = _gla_chunk(Sts, *_gla_heads(x_ref, la_ref, r))
            for h in R:
                s_scr[h] = St2[h]
                o_ref[r, h * GLA_DV:(h + 1) * GLA_DV] = o[h]
            return carry

        lax.fori_loop(0, GLA_SUB, sub, 0)

    return pl.pallas_call(
        body, name="gla_fwd", grid=(NB,),
        in_specs=[pl.BlockSpec((ROWS, G_W), lambda n: (n, C_G // G_W)),
                  pl.BlockSpec((ROWS, GLA_QK), lambda n: (n, 0))],
        out_specs=[pl.BlockSpec((ROWS, GLA_V), lambda n: (n, 0)),
                   pl.BlockSpec((GLA_SUB, GLA_H, GLA_DV, GLA_DK), lambda n: (n, 0, 0, 0))],
        out_shape=[jax.ShapeDtypeStruct((Lp, GLA_V), f32),
                   jax.ShapeDtypeStruct((Lp // GLA_C, GLA_H, GLA_DV, GLA_DK), f32)],
        scratch_shapes=[pltpu.VMEM((GLA_H, GLA_DV, GLA_DK), f32)],
        compiler_params=_params("arbitrary"),
    )(proj, la)


def _gla_bwd(proj, la, sall, do, dproj, side=None):
    Lp = proj.shape[0]
    NB = Lp // ROWS

    def body(x_ref, la_ref, sall_ref, do_ref, dproj_ref, dx_ref, dla_ref, ds_scr):
        del dproj_ref

        @pl.when(pl.program_id(0) == 0)
        def _():
            ds_scr[...] = jnp.zeros_like(ds_scr)

        def sub(ci, carry):
            c = GLA_SUB - 1 - ci
            r = pl.ds(pl.multiple_of(c * GLA_C, GLA_C), GLA_C)
            R = range(GLA_H)
            _, vjp = jax.vjp(_gla_chunk, [sall_ref[c, h] for h in R], *_gla_heads(x_ref, la_ref, r))
            dS, dq, dk, dv, dl = vjp(([ds_scr[h] for h in R], [do_ref[r, h * GLA_DV:(h + 1) * GLA_DV] for h in R]))
            for h in R:
                sq, sk, sv = _gla_slices(h)
                ds_scr[h] = dS[h]
                dx_ref[r, sq] = dq[h].astype(bf16)
                dx_ref[r, sk] = dk[h].astype(bf16)
                dx_ref[r, sv] = dv[h].astype(bf16)
                dla_ref[r, sq] = dl[h]
            return carry

        lax.fori_loop(0, GLA_SUB, sub, 0)

    x_spec = pl.BlockSpec((ROWS, G_W), lambda n: (NB - 1 - n, C_G // G_W))
    rev = lambda n: (NB - 1 - n, 0)
    out = _call(
        body, name="gla_bwd", grid=(NB,),
        in_specs=[x_spec, pl.BlockSpec((ROWS, GLA_QK), rev),
                  pl.BlockSpec((GLA_SUB, GLA_H, GLA_DV, GLA_DK), lambda n: (NB - 1 - n, 0, 0, 0)),
                  pl.BlockSpec((ROWS, GLA_V), rev), _ANY],
        out_specs=[x_spec, pl.BlockSpec((ROWS, GLA_QK), rev)],
        out_shape=[jax.ShapeDtypeStruct(dproj.shape, dproj.dtype), jax.ShapeDtypeStruct((Lp, GLA_QK), f32)],
        aliases={4: 0}, scratch_shapes=[pltpu.VMEM((GLA_H, GLA_DV, GLA_DK), f32)],
        args=[proj, la, sall, do, dproj], sem=("arbitrary",), side=side)
    return out[0], out[1], out[2:]


def _gated_norm_fn(og, ol, zr, wg, wl):
    outs = []
    for h in range(GDN_H):
        s = slice(h * GDN_DV, (h + 1) * GDN_DV)
        outs.append(_rms(og[:, s])[0] * wg * _silu(zr[:, s]))
    for h in range(GLA_H):
        s = slice(h * GLA_DV, (h + 1) * GLA_DV)
        sr = slice(GDN_V + h * GLA_DV, GDN_V + (h + 1) * GLA_DV)
        outs.append(_rms(ol[:, s])[0] * wl * _silu(zr[:, sr]))
    return jnp.concatenate(outs, axis=-1)


def _gated_norm(og, ol, proj, wg, wl):
    Lp = og.shape[0]
    tr = _tile(Lp, 256)

    def body(og_ref, ol_ref, zr_ref, wg_ref, wl_ref, o_ref):
        o_ref[...] = _gated_norm_fn(og_ref[...], ol_ref[...], zr_ref[...], wg_ref[...], wl_ref[...]).astype(bf16)

    return pl.pallas_call(
        body, name="gated_norm", grid=(Lp // tr,),
        in_specs=[pl.BlockSpec((tr, GDN_V), lambda i: (i, 0)), pl.BlockSpec((tr, GLA_V), lambda i: (i, 0)),
                  pl.BlockSpec((tr, ZR_W), lambda i: (i, C_ZR // ZR_W)),
                  pl.BlockSpec((1, GDN_DV), lambda i: (0, 0)), pl.BlockSpec((1, GLA_DV), lambda i: (0, 0))],
        out_specs=pl.BlockSpec((tr, ZR_W), lambda i: (i, 0)),
        out_shape=jax.ShapeDtypeStruct((Lp, ZR_W), bf16),
        compiler_params=_params("parallel"),
    )(og, ol, proj, wg, wl)


def _gated_norm_bwd(og, ol, proj, wg, wl, dmix):
    Lp = og.shape[0]
    tr = _tile(Lp, 128)

    def body(og_ref, ol_ref, zr_ref, wg_ref, wl_ref, d_ref, dog_ref, dol_ref, dzr_ref, gwg_ref, gwl_ref):
        i = pl.program_id(0)
        _, vjp = jax.vjp(_gated_norm_fn, og_ref[...], ol_ref[...], zr_ref[...], wg_ref[...], wl_ref[...])
        dog, dol, dzr, gwg, gwl = vjp(d_ref[...])
        dog_ref[...] = dog
        dol_ref[...] = dol
        dzr_ref[...] = dzr.astype(bf16)

        @pl.when(i == 0)
        def _():
            gwg_ref[...] = gwg
            gwl_ref[...] = gwl

        @pl.when(i > 0)
        def _():
            gwg_ref[...] += gwg
            gwl_ref[...] += gwl

    og_spec = pl.BlockSpec((tr, GDN_V), lambda i: (i, 0))
    ol_spec = pl.BlockSpec((tr, GLA_V), lambda i: (i, 0))
    zr_spec = pl.BlockSpec((tr, ZR_W), lambda i: (i, C_ZR // ZR_W))
    vg = pl.BlockSpec((1, GDN_DV), lambda i: (0, 0))
    vl = pl.BlockSpec((1, GLA_DV), lambda i: (0, 0))
    return pl.pallas_call(
        body, name="gated_norm_bwd", grid=(Lp // tr,),
        in_specs=[og_spec, ol_spec, zr_spec, vg, vl, pl.BlockSpec((tr, ZR_W), lambda i: (i, 0))],
        out_specs=[og_spec, ol_spec, zr_spec, vg, vl],
        out_shape=[jax.ShapeDtypeStruct((Lp, GDN_V), f32), jax.ShapeDtypeStruct((Lp, GLA_V), f32),
                   jax.ShapeDtypeStruct((Lp, C_END), bf16),
                   jax.ShapeDtypeStruct((1, GDN_DV), f32), jax.ShapeDtypeStruct((1, GLA_DV), f32)],
        compiler_params=_params("arbitrary"),
    )(og, ol, proj, wg, wl, dmix)


def _adamw(g, w, m, v, name):
    R, C = g.shape
    tr = _tile(R, 256, 8) if R % 8 == 0 and R > 256 else R
    c1 = 1.0 - ADAM_B1 ** ADAM_STEP
    c2 = 1.0 - ADAM_B2 ** ADAM_STEP

    def body(g_ref, w_ref, m_ref, v_ref, d_ref, mo_ref, vo_ref):
        g_ = g_ref[...]
        m2 = ADAM_B1 * m_ref[...] + (1.0 - ADAM_B1) * g_
        v2 = ADAM_B2 * v_ref[...] + (1.0 - ADAM_B2) * (g_ * g_)
        mo_ref[...] = m2
        vo_ref[...] = v2
        d_ref[...] = -ADAM_LR * ((m2 / c1) / (jnp.sqrt(v2 / c2) + ADAM_EPS) + ADAM_WD * w_ref[...])

    blk = pl.BlockSpec((tr, C), lambda i: (i, 0))
    return pl.pallas_call(
        body, name=name, grid=(R // tr,), in_specs=[blk] * 4, out_specs=[blk] * 3,
        out_shape=[jax.ShapeDtypeStruct((R, C), f32)] * 3,
        compiler_params=_params("parallel"),
    )(g, w, m, v)


def _sum_slots(r, name):
    n, R, C = r.shape
    tr = _tile(R, 128, 16) if R % 16 == 0 and R > 128 else R

    def body(r_ref, o_ref):
        acc = r_ref[0].astype(f32)
        for s in range(1, n):
            acc = acc + r_ref[s].astype(f32)
        o_ref[...] = acc

    return pl.pallas_call(
        body, name=name, grid=(R // tr,),
        in_specs=[pl.BlockSpec((n, tr, C), lambda i: (0, i, 0))],
        out_specs=pl.BlockSpec((tr, C), lambda i: (i, 0)),
        out_shape=jax.ShapeDtypeStruct((R, C), f32),
        compiler_params=_params("parallel"),
    )(r)


class _Siblings:
    def __init__(self, arrays):
        self.arrays = list(arrays)
        self.n = len(self.arrays)
        self.out_shape = [jax.ShapeDtypeStruct((2,) + a.shape, a.dtype) for a in self.arrays]
        self.sems = [pltpu.SemaphoreType.DMA((self.n,)), pltpu.SemaphoreType.DMA((self.n,)),
                     pltpu.SemaphoreType.DMA((self.n,))]

    def hooks(self, ins, outs, send, recv, lsem):
        def copies():
            x, y, c = lax.axis_index("x"), lax.axis_index("y"), lax.axis_index("c")
            out = []
            for a in range(self.n):
                out.append((pltpu.make_async_copy(ins[a], outs[a].at[c], lsem.at[a]), None))
                mk = lambda dst, a=a: pltpu.make_async_remote_copy(
                    src_ref=ins[a], dst_ref=dst, send_sem=send.at[a], recv_sem=recv.at[a],
                    device_id=(x, y, 1 - c), device_id_type=MESH)
                out.append((mk(outs[a].at[c]), mk(outs[a].at[1 - c])))
            return out

        return _start_wait(copies)


def _comm_now(name, sides):
    total = sum(s.n for s in sides)

    def body(*refs):
        ins, outs, sems = refs[:total], refs[total:2 * total], refs[2 * total:]
        hooks, o = [], 0
        for i, s in enumerate(sides):
            hooks.append(s.hooks(ins[o:o + s.n], outs[o:o + s.n], *sems[3 * i:3 * i + 3]))
            o += s.n
        for start, _ in hooks:
            start()
        for _, wait in hooks:
            wait()

    out = pl.pallas_call(
        body, name=name, in_specs=[_ANY] * total, out_specs=[_ANY] * total,
        out_shape=[sh for s in sides for sh in s.out_shape], scratch_shapes=[sm for s in sides for sm in s.sems],
    )(*[a for s in sides for a in s.arrays])
    res, o = [], 0
    for s in sides:
        res.append(list(out[o:o + s.n]))
        o += s.n
    return res


def _cat_cols(g):
    return jnp.concatenate([g[i] for i in range(N_CHIP)], axis=-1)


def _row_slabs(a):
    return a.reshape(N_DEV, a.shape[0] // N_DEV, a.shape[1])


def _w_in_columns(g_wp, g_wsm):
    return jnp.concatenate([g_wp[:, C_QKV:C_END], g_wp[:, C_ZR:C_ZR + GDN_V], g_wsm[:, :SM_LR],
                            g_wp[:, C_G:C_G + G_W], g_wp[:, C_ZR + GDN_V:C_ZR + ZR_W],
                            g_wsm[:, SM_LR:SM_LR + GATE_RANK]], axis=1)


def _step(x, loss_target, p, w_in, meta, conv_w, w2, shard):
    _, S, D = x.shape
    wp = jnp.concatenate([w_in[:, R_Z:R_AB], w_in[:, R_GR:R_LR], w_in[:, R_G:R_GR], w_in[:, R_QKV:R_Z]], axis=1)
    wsm = jnp.concatenate([w_in[:, R_AB:R_G], w_in[:, R_LR:R_END],
                           jnp.zeros((D, SM_W - SM_LR - GATE_RANK), w_in.dtype)], axis=1)
    w2p = jnp.pad(w2, ((SM_LR, SM_W - SM_LR - GATE_RANK), (0, 0)))
    alog_p = jnp.pad(p["gdn_a_log"], ((0, 0), (0, SM_W - GDN_H)))
    dtb_p = jnp.pad(p["gdn_dt_bias"], ((0, 0), (0, SM_W - GDN_H)))
    m64 = jnp.concatenate([jnp.zeros((PAD, D), f32), meta], axis=0)
    gate_b, gdn_norm_w, gla_norm_w = p["gla_gate_b"], p["gdn_norm_w"], p["gla_norm_w"]

    h0, n1 = _embed_norm(x, m64, p["attn_norm_w"])
    proj, (w_out4,) = _mm(n1, wp, "nn", "proj", side=_Gather([shard["w_out"]]))
    w_out = w_out4.reshape(-1, D)
    psm = _mm(n1, wsm, "nn", "proj_small")
    gb, la = _gates(psm, w2p, gate_b, alog_p, dtb_p)
    qkvc, (w_up4,) = _conv(proj, conv_w, side=_Gather([shard["w_up"]]))
    w_up = _cat_cols(w_up4)
    og, sall, (w_gate4,) = _gdn_fwd(qkvc, gb, side=_Gather([shard["w_gate"]]))
    w_gate = _cat_cols(w_gate4)
    ol, stall = _gla_fwd(proj, la)
    mixed = _gated_norm(og, ol, proj, gdn_norm_w, gla_norm_w)
    attn = _mm(mixed, w_out, "nn", "out_proj")
    h1, n2 = _add_norm(h0, attn, p["ffn_norm_w"])
    act, gate, up, (w_down4,) = _ffn_in(n2, w_gate, w_up, side=_Gather([shard["w_down"]]))
    w_down = w_down4.reshape(-1, D)
    ffn = _mm(act, w_down, "nn", "ffn_down", tk_cap=1408)
    dh2, dh2b, lossp, g_final = _final(h1, ffn, loss_target, p["final_norm_w"])

    g_down = _mm(act, dh2b, "tn", "g_w_down", tk_cap=1408, out_dtype=bf16)
    dg, du = _ffn_dact(dh2b, w_down, gate, up)
    g_gate = _mm(n2, dg, "tn", "g_w_gate", tn_cap=1408, tk_cap=1408, out_dtype=bf16, col_slabs=True)
    g_up = _mm(n2, du, "tn", "g_w_up", tn_cap=1408, tk_cap=1408, out_dtype=bf16, col_slabs=True)
    dn2 = _mm(dg, w_gate, "nt", "d_n2_gate", tk_cap=1408)
    dn2 = _mm(du, w_up, "nt", "d_n2_up", tk_cap=1408, acc_in=dn2)
    dh1, dh1b, g_ffn_norm = _norm_bwd(dn2, h1, dh2, p["ffn_norm_w"])
    dmix = _mm(dh1b, w_out, "nt", "d_mixed")
    g_out = _mm(mixed, dh1b, "tn", "g_w_out", tk_cap=1408, out_dtype=bf16)
    dog, dol, dproj, g_gdn_norm, g_gla_norm = _gated_norm_bwd(og, ol, proj, gdn_norm_w, gla_norm_w, dmix)
    dproj, dla, (r_down, r_gate) = _gla_bwd(proj, la, stall, dol, dproj,
                                            side=_Exchange([_row_slabs(g_down), g_gate]))
    dqkvc, dgb, (r_up, r_out) = _gdn_bwd(qkvc, gb, sall, dog, side=_Exchange([g_up, _row_slabs(g_out)]))
    dproj, g_conv, (h_out, h_down) = _conv_bwd(
        proj, conv_w, dqkvc, dproj, side=_Siblings([_sum_slots(r_out, "sum_w_out"), _sum_slots(r_down, "sum_w_down")]))
    dpsm, g_w2p, g_gate_b, g_alog, g_dtb = _gates_bwd(psm, w2p, gate_b, alog_p, dtb_p, dgb, dla)
    g_wp, (h_gate, h_up) = _mm(
        n1, dproj, "tn", "g_w_in", tm_cap=2048, tk_cap=1408, out_dtype=bf16,
        side=_Siblings([_sum_slots(r_gate, "sum_w_gate"), _sum_slots(r_up, "sum_w_up")]))
    g_wsm = _mm(n1, dpsm, "tn", "g_w_in_small", tm_cap=2048, tk_cap=1408, out_dtype=bf16)
    dn1, r_in = _mm(dproj, wp, "nt", "d_n1", tk_cap=1024, side=_Exchange([_row_slabs(g_wp), _row_slabs(g_wsm)]))
    dn1 = _mm(dpsm, wsm, "nt", "d_n1_small", acc_in=dn1)
    grad_x, g_meta, g_attn_norm = _embed_norm_bwd(dn1, h0, dh1, p["attn_norm_w"], S)

    received = dict(w_in=tuple(r_in), w_gate=h_gate, w_up=h_up, w_out=h_out, w_down=h_down)
    small = dict(
        meta_tokens=g_meta, attn_norm_w=g_attn_norm, gdn_conv_w=g_conv, gdn_a_log=g_alog[:, :GDN_H],
        gdn_dt_bias=g_dtb[:, :GDN_H], gdn_norm_w=g_gdn_norm, gla_gate_w2=g_w2p[SM_LR:SM_LR + GATE_RANK],
        gla_gate_b=g_gate_b, gla_norm_w=g_gla_norm, ffn_norm_w=g_ffn_norm, final_norm_w=g_final)
    return lossp[0, 0], grad_x, received, small


_WEIGHTS = ("meta_tokens", "attn_norm_w", "w_in", "gdn_conv_w", "gdn_a_log", "gdn_dt_bias", "gdn_norm_w",
            "gla_gate_w2", "gla_gate_b", "gla_norm_w", "w_out", "ffn_norm_w", "w_gate", "w_up", "w_down",
            "final_norm_w")
_BIG_COLS = ("w_in", "w_gate", "w_up")
_BIG_ROWS = ("w_out", "w_down")
_SMALL_SHARDED = ("meta_tokens", "gdn_conv_w", "gla_gate_w2")


def kernel(x, meta_tokens, attn_norm_w, w_in, gdn_conv_w, gdn_a_log, gdn_dt_bias, gdn_norm_w, gla_gate_w2, gla_gate_b, gla_norm_w, w_out, ffn_norm_w, w_gate, w_up, w_down, final_norm_w, loss_target, m_meta_tokens, m_attn_norm_w, m_w_in, m_gdn_conv_w, m_gdn_a_log, m_gdn_dt_bias, m_gdn_norm_w, m_gla_gate_w2, m_gla_gate_b, m_gla_norm_w, m_w_out, m_ffn_norm_w, m_w_gate, m_w_up, m_w_down, m_final_norm_w, v_meta_tokens, v_attn_norm_w, v_w_in, v_gdn_conv_w, v_gdn_a_log, v_gdn_dt_bias, v_gdn_norm_w, v_gla_gate_w2, v_gla_gate_b, v_gla_norm_w, v_w_out, v_ffn_norm_w, v_w_gate, v_w_up, v_w_down, v_final_norm_w):
    w = dict(meta_tokens=meta_tokens, attn_norm_w=attn_norm_w, w_in=w_in, gdn_conv_w=gdn_conv_w, gdn_a_log=gdn_a_log,
             gdn_dt_bias=gdn_dt_bias, gdn_norm_w=gdn_norm_w, gla_gate_w2=gla_gate_w2, gla_gate_b=gla_gate_b,
             gla_norm_w=gla_norm_w, w_out=w_out, ffn_norm_w=ffn_norm_w, w_gate=w_gate, w_up=w_up, w_down=w_down,
             final_norm_w=final_norm_w)
    m = dict(meta_tokens=m_meta_tokens, attn_norm_w=m_attn_norm_w, w_in=m_w_in, gdn_conv_w=m_gdn_conv_w,
             gdn_a_log=m_gdn_a_log, gdn_dt_bias=m_gdn_dt_bias, gdn_norm_w=m_gdn_norm_w, gla_gate_w2=m_gla_gate_w2,
             gla_gate_b=m_gla_gate_b, gla_norm_w=m_gla_norm_w, w_out=m_w_out, ffn_norm_w=m_ffn_norm_w,
             w_gate=m_w_gate, w_up=m_w_up, w_down=m_w_down, final_norm_w=m_final_norm_w)
    v = dict(meta_tokens=v_meta_tokens, attn_norm_w=v_attn_norm_w, w_in=v_w_in, gdn_conv_w=v_gdn_conv_w,
             gdn_a_log=v_gdn_a_log, gdn_dt_bias=v_gdn_dt_bias, gdn_norm_w=v_gdn_norm_w, gla_gate_w2=v_gla_gate_w2,
             gla_gate_b=v_gla_gate_b, gla_norm_w=v_gla_norm_w, w_out=v_w_out, ffn_norm_w=v_ffn_norm_w,
             w_gate=v_w_gate, w_up=v_w_up, w_down=v_w_down, final_norm_w=v_final_norm_w)
    chip = 2 * lax.axis_index("x") + lax.axis_index("y")

    def two_d(a):
        return a.reshape(1, -1) if a.ndim == 1 else a.reshape(-1, a.shape[-1])

    w2d = {k: two_d(a) for k, a in w.items()}
    big = _BIG_COLS + _BIG_ROWS
    small = tuple(k for k in _WEIGHTS if k not in big)

    first = ("w_in",) + _SMALL_SHARDED
    gathered, = _comm_now("gather_first", [_Gather([w2d[k].astype(bf16) if k in big else w2d[k] for k in first])])
    full = {k: _cat_cols(a) for k, a in zip(first, gathered)}
    shard = {k: w2d[k].astype(bf16) for k in big if k != "w_in"}
    lossp, grad_x, received, g = _step(x, loss_target, {k: w2d[k] for k in small}, full["w_in"], full["meta_tokens"],
                                       full["gdn_conv_w"], full["gla_gate_w2"], shard)
    loss = lax.psum(lossp, ("x", "y", "c"))

    sizes = [g[k].size for k in small]
    total = sum(sizes)
    rows = -(-total // 1024)
    rows += (-rows) % 8
    packed = jnp.concatenate([g[k].reshape(-1) for k in small] + [jnp.zeros((rows * 1024 - total,), f32)])
    r_wp, r_wsm = received.pop("w_in")
    s_in = _w_in_columns(_sum_slots(r_wp, "sum_w_in"), _sum_slots(r_wsm, "sum_w_in_small"))
    in_by_chip = s_in.reshape(s_in.shape[0], N_CHIP, -1).transpose(1, 0, 2)
    (in8, packed8), = _comm_now("exchange_tail", [_Exchange([], [packed.reshape(rows, 1024)], by_chip=[in_by_chip])])
    red = {k: h.reshape(w2d[k].shape) for k, h in received.items()}
    red["w_in"] = in8.reshape(w2d["w_in"].shape)
    psum_small = _sum_slots(packed8, "sum_small").reshape(-1)
    off = 0
    for k, n in zip(small, sizes):
        a = psum_small[off:off + n].reshape(g[k].shape)
        off += n
        if k in _SMALL_SHARDED:
            c = w2d[k].shape[1]
            a = lax.dynamic_slice_in_dim(a, chip * c, c, axis=1)
        red[k] = a

    grads, deltas, new_m, new_v = [], [], [], []
    for k in _WEIGHTS:
        d, m2, v2 = _adamw(red[k], w2d[k], two_d(m[k]), two_d(v[k]), "adamw_" + k)
        shape = w[k].shape
        grads.append(red[k].reshape(shape))
        deltas.append(d.reshape(shape))
        new_m.append(m2.reshape(shape))
        new_v.append(v2.reshape(shape))
    return (loss, grad_x, *grads, *deltas, *new_m, *new_v)
```

```python
import functools

import jax
import jax.numpy as jnp
from jax import lax
from jax.experimental import pallas as pl
from jax.experimental.pallas import tpu as pltpu

f32 = jnp.float32
bf16 = jnp.bfloat16
HI = lax.Precision.HIGHEST
HIGH = lax.Precision.HIGH
MESH = pl.DeviceIdType.MESH

N_META = 16
CONV_K = 4
GDN_H, GDN_DK, GDN_DV, GDN_C = 8, 128, 128, 64
GLA_H, GLA_DK, GLA_DV, GLA_C = 4, 128, 256, 16
GATE_RANK = 16
GATE_NORMALIZER = 16.0
EPS = 1e-6
GDN_QK = GDN_H * GDN_DK
GDN_V = GDN_H * GDN_DV
GLA_QK = GLA_H * GLA_DK
GLA_V = GLA_H * GLA_DV
PAD = (-N_META) % GDN_C
OFF = PAD + N_META
ROWS = 64

R_QKV, R_Z, R_AB, R_G, R_GR, R_LR, R_END = 0, 3072, 4096, 4112, 6160, 7184, 7200
C_ZR, C_G, C_QKV, C_END = 0, 2048, 4096, 7168
ZR_W = GDN_V + GLA_V
G_W = 2 * GLA_QK + GLA_V
QKV_W = 2 * GDN_QK + GDN_V
Q0, K0, V0 = 0, GDN_QK, 2 * GDN_QK
SM_W = 128
SM_LR = 2 * GDN_H

ADAM_LR, ADAM_B1, ADAM_B2, ADAM_EPS, ADAM_WD, ADAM_STEP = 0.001, 0.9, 0.999, 1e-08, 0.01, 10

VMEM_LIMIT_V7X = 56 * 1024 * 1024
N_DEV = 8
N_CHIP = 4


def _params(*sem):
    return pltpu.CompilerParams(dimension_semantics=sem, vmem_limit_bytes=VMEM_LIMIT_V7X)


def _tile(n, cap, mult=16):
    best = None
    for d in range(mult, min(n, cap) + 1, mult):
        if n % d == 0:
            best = d
    assert best is not None, (n, cap, mult)
    return best


NN = ((1,), (0,))
NT = ((1,), (1,))
TN = ((0,), (0,))


def _dot(a, b, dims, prec=None):
    return lax.dot_general(a, b, (dims, ((), ())), precision=prec, preferred_element_type=f32)


def _mmb(a, b, dims):
    return _dot(a.astype(bf16), b.astype(bf16), dims)


def _sigmoid(x):
    return jax.nn.sigmoid(x)


def _silu(x):
    return x * _sigmoid(x)


def _dsilu(x):
    s = _sigmoid(x)
    return s * (1.0 + x * (1.0 - s))


def _log1p_exp_neg_abs(x):
    t = jnp.exp(-jnp.abs(x))
    u = 1.0 + t
    d = u - 1.0
    return jnp.where(d == 0.0, t, jnp.log(u) * (t / jnp.where(d == 0.0, 1.0, d)))


def _softplus(x):
    return jnp.maximum(x, 0.0) + _log1p_exp_neg_abs(x)


def _log_sigmoid(x):
    return jnp.minimum(x, 0.0) - _log1p_exp_neg_abs(x)


def _rms(x):
    r = lax.rsqrt(jnp.mean(x * x, axis=-1, keepdims=True) + EPS)
    return x * r, r


def _rms_bwd(dy, xh, r, w):
    t = dy * w
    return r * (t - xh * jnp.mean(t * xh, axis=-1, keepdims=True))


def _l2n(x):
    return x * lax.rsqrt(jnp.sum(x * x, axis=-1, keepdims=True) + EPS)


INV_LEAF = 8


def _same_block(C, b):
    sh = b.bit_length() - 1
    row = lax.broadcasted_iota(jnp.int32, (C, C), 0)
    col = lax.broadcasted_iota(jnp.int32, (C, C), 1)
    return lax.shift_right_logical(row, sh) == lax.shift_right_logical(col, sh)


def _tri_inv_impl(As):
    C = As[0].shape[0]
    R = range(len(As))
    row = lax.broadcasted_iota(jnp.int32, (C, C), 0)
    col = lax.broadcasted_iota(jnp.int32, (C, C), 1)
    eye = (row == col).astype(f32)
    b = INV_LEAF
    inner = _same_block(C, b)
    leaf = [jnp.where(inner, As[h], 0.0) for h in R]
    d = [eye - leaf[h] for h in R]
    pw = leaf
    n = 2
    while n < b:
        pw = [_dot(pw[h], pw[h], NN, HIGH) for h in R]
        d = [_dot(d[h], eye + pw[h], NN, HIGH) for h in R]
        n *= 2
    while b < C:
        outer = _same_block(C, 2 * b)
        level = jnp.logical_and(outer, jnp.logical_not(inner))
        ed = [_dot(jnp.where(level, As[h], 0.0), d[h], NN, HIGH) for h in R]
        d = [d[h] - _dot(d[h], ed[h], NN, HIGH) for h in R]
        inner = outer
        b *= 2
    return d


@jax.custom_vjp
def _tri_inv(As):
    return _tri_inv_impl(As)


def _tri_inv_fwd(As):
    d = _tri_inv_impl(As)
    return d, d


def _tri_inv_bwd(d, g):
    R = range(len(d))
    t = [_dot(d[h], g[h], TN, HIGH) for h in R]
    return ([-_dot(t[h], d[h], NT, HIGH) for h in R],)


_tri_inv.defvjp(_tri_inv_fwd, _tri_inv_bwd)


@jax.custom_vjp
def _tri_inv_known(As, Ps):
    del As
    return Ps


def _tri_inv_known_fwd(As, Ps):
    del As
    return Ps, Ps


def _tri_inv_known_bwd(d, g):
    return _tri_inv_bwd(d, g)[0], [jnp.zeros_like(x) for x in d]


_tri_inv_known.defvjp(_tri_inv_known_fwd, _tri_inv_known_bwd)


def _gdn_chunk(Ss, qrs, krs, vs, betas, gs, Ps=None):
    C, dk = qrs[0].shape
    R = range(len(Ss))
    row = lax.broadcasted_iota(jnp.int32, (C, C), 0)
    col = lax.broadcasted_iota(jnp.int32, (C, C), 1)
    causal = row >= col
    strict = row > col
    cf = causal.astype(f32)
    q = [_l2n(qrs[h]) * (dk ** -0.5) for h in R]
    k = [_l2n(krs[h]) for h in R]
    mc = [_dot(cf, jnp.broadcast_to(gs[h], (C, C)), NN, HI) for h in R]
    gc = [mc[h][:, 0:1] for h in R]
    decay = [jnp.where(causal, jnp.exp(jnp.where(causal, mc[h] - mc[h].T, 0.0)), 0.0) for h in R]
    kb = [k[h] * betas[h] for h in R]
    a = [jnp.where(strict, _mmb(kb[h], k[h], NT) * decay[h], 0.0) for h in R]
    p = _tri_inv(a) if Ps is None else _tri_inv_known(a, Ps)
    egc = [jnp.exp(gc[h]) for h in R]
    u = [_mmb(p[h], vs[h] * betas[h], NN) for h in R]
    w = [_mmb(p[h], kb[h] * egc[h], NN) for h in R]
    qk = [jnp.where(causal, _mmb(q[h], k[h], NT) * decay[h], 0.0) for h in R]
    v_new = [u[h] - _mmb(w[h], Ss[h], NN) for h in R]
    o = [_mmb(q[h] * egc[h], Ss[h], NN) + _mmb(qk[h], v_new[h], NN) for h in R]
    gl = [gc[h][C - 1:C, :] for h in R]
    kd = [k[h] * jnp.exp(gl[h] - gc[h]) for h in R]
    S2 = [Ss[h] * jnp.exp(gl[h]) + _mmb(kd[h], v_new[h], TN) for h in R]
    return S2, o, p


def _gla_chunk(Sts, qrs, ks, vs, las):
    C, dk = qrs[0].shape
    R = range(len(Sts))
    row = lax.broadcasted_iota(jnp.int32, (C, C), 0)
    col = lax.broadcasted_iota(jnp.int32, (C, C), 1)
    cf = (row >= col).astype(f32)
    q = [qrs[h] * (dk ** -0.5) for h in R]
    b = [_dot(cf, las[h], NN, HI) for h in R]
    o = [_mmb(q[h] * jnp.exp(b[h]), Sts[h], NT) for h in R]
    ri = lax.broadcasted_iota(jnp.int32, (C, dk), 0)
    for j in range(C):
        m = ri >= j
        e = [jnp.where(m, jnp.exp(jnp.where(m, b[h] - b[h][j:j + 1, :], 0.0)), 0.0) for h in R]
        s = [jnp.sum(q[h] * ks[h][j:j + 1, :] * e[h], axis=-1, keepdims=True) for h in R]
        o = [o[h] + s[h] * vs[h][j:j + 1, :] for h in R]
    bl = [b[h][C - 1:C, :] for h in R]
    St2 = [Sts[h] * jnp.exp(bl[h]) + _mmb(vs[h], ks[h] * jnp.exp(bl[h] - b[h]), TN) for h in R]
    return St2, o


_ANY = pl.BlockSpec(memory_space=pl.ANY)


class _Gather:
    def __init__(self, arrays):
        self.arrays = list(arrays)
        self.n = len(self.arrays)
        self.out_shape = [jax.ShapeDtypeStruct((N_CHIP,) + a.shape, a.dtype) for a in self.arrays]
        self.sems = [pltpu.SemaphoreType.DMA((self.n, 3)), pltpu.SemaphoreType.DMA((self.n, 3)),
                     pltpu.SemaphoreType.DMA((self.n,))]

    def hooks(self, ins, outs, send, recv, lsem):
        def copies():
            x, y, c = lax.axis_index("x"), lax.axis_index("y"), lax.axis_index("c")
            me = 2 * x + y
            out = []
            for a in range(self.n):
                out.append((pltpu.make_async_copy(ins[a], outs[a].at[me], lsem.at[a]), None))
                for j, (px, py) in enumerate([(1 - x, y), (x, 1 - y), (1 - x, 1 - y)]):
                    mk = lambda dst, a=a, j=j, px=px, py=py: pltpu.make_async_remote_copy(
                        src_ref=ins[a], dst_ref=dst, send_sem=send.at[a, j], recv_sem=recv.at[a, j],
                        device_id=(px, py, c), device_id_type=MESH)
                    out.append((mk(outs[a].at[me]), mk(outs[a].at[2 * px + py])))
            return out

        return _start_wait(copies)


class _Exchange:
    def __init__(self, slotted, shared=(), by_chip=()):
        self.arrays = list(slotted) + list(by_chip) + list(shared)
        self.ns, self.nc = len(slotted), len(by_chip)
        self.n = len(self.arrays)
        self.out_shape = [jax.ShapeDtypeStruct(a.shape, a.dtype) for a in slotted]
        self.out_shape += [jax.ShapeDtypeStruct((N_DEV,) + a.shape[1:], a.dtype) for a in by_chip]
        self.out_shape += [jax.ShapeDtypeStruct((N_DEV,) + b.shape, b.dtype) for b in shared]
        self.sems = [pltpu.SemaphoreType.DMA((self.n, N_DEV - 1)), pltpu.SemaphoreType.DMA((self.n, N_DEV - 1)),
                     pltpu.SemaphoreType.DMA((self.n,))]

    def hooks(self, ins, outs, send, recv, lsem):
        def copies():
            x, y, c = lax.axis_index("x"), lax.axis_index("y"), lax.axis_index("c")
            me = 4 * x + 2 * y + c

            def src(a, dev):
                tx, ty, tc = dev
                if a < self.ns:
                    return ins[a].at[4 * tx + 2 * ty + tc]
                return ins[a].at[2 * tx + ty] if a < self.ns + self.nc else ins[a]

            out = []
            for a in range(self.n):
                out.append((pltpu.make_async_copy(src(a, (x, y, c)), outs[a].at[me], lsem.at[a]), None))
                for o in range(1, N_DEV):
                    dev = (1 - x if o & 4 else x, 1 - y if o & 2 else y, 1 - c if o & 1 else c)
                    t = 4 * dev[0] + 2 * dev[1] + dev[2]
                    mk = lambda dst, a=a, o=o, dev=dev: pltpu.make_async_remote_copy(
                        src_ref=src(a, dev), dst_ref=dst, send_sem=send.at[a, o - 1], recv_sem=recv.at[a, o - 1],
                        device_id=dev, device_id_type=MESH)
                    out.append((mk(outs[a].at[me]), mk(outs[a].at[t])))
            return out

        return _start_wait(copies)


def _start_wait(copies):
    def start():
        for s, _ in copies():
            s.start()

    def wait():
        for s, w in copies():
            (s if w is None else w).wait()

    return start, wait


def _call(body, *, name, grid, in_specs, out_specs, out_shape, args, sem, scratch_shapes=(), aliases=None, side=None):
    in_specs, out_specs, out_shape, args = list(in_specs), list(out_specs), list(out_shape), list(args)
    scratch_shapes = list(scratch_shapes)
    aliases = aliases or {}
    if side is None:
        return pl.pallas_call(
            body, name=name, grid=grid, in_specs=in_specs, out_specs=out_specs, out_shape=out_shape,
            scratch_shapes=scratch_shapes, input_output_aliases=aliases, compiler_params=_params(*sem))(*args)
    n_in, n_out, n_scr, ns = len(in_specs), len(out_specs), len(scratch_shapes), side.n

    def full_body(*refs):
        ins, refs = refs[:n_in], refs[n_in:]
        s_in, refs = refs[:ns], refs[ns:]
        outs, refs = refs[:n_out], refs[n_out:]
        s_out, refs = refs[:ns], refs[ns:]
        scr, sems = refs[:n_scr], refs[n_scr:]
        start, wait = side.hooks(s_in, s_out, *sems)
        ids = [pl.program_id(d) for d in range(len(grid))]
        first = functools.reduce(jnp.logical_and, [i == 0 for i in ids])
        last = functools.reduce(jnp.logical_and, [i == g - 1 for i, g in zip(ids, grid)])
        pl.when(first)(start)
        body(*ins, *outs, *scr)
        pl.when(last)(wait)

    return pl.pallas_call(
        full_body, name=name, grid=grid, in_specs=in_specs + [_ANY] * ns, out_specs=out_specs + [_ANY] * ns,
        out_shape=out_shape + side.out_shape, scratch_shapes=scratch_shapes + side.sems,
        input_output_aliases=aliases, compiler_params=_params(*(["arbitrary"] * len(grid))))(*args, *side.arrays)


def _mm(a, b, mode, name, *, tm_cap=1408, tn_cap=1024, tk_cap=2048, out_dtype=f32, acc_in=None, side=None,
        col_slabs=False):
    if mode == "nn":
        (M, K), (K2, N) = a.shape, b.shape
    elif mode == "nt":
        (M, K), (N, K2) = a.shape, b.shape
    else:
        (K, M), (K2, N) = a.shape, b.shape
    assert K == K2, (name, a.shape, b.shape)
    tm = _tile(M // 2 if col_slabs else M, tm_cap)
    tn = _tile(N // N_CHIP if col_slabs else N, tn_cap, 128)
    tk = _tile(K, tk_cap, 128 if K % 128 == 0 else 16)
    nk = K // tk
    dims = {"nn": NN, "nt": NT, "tn": TN}[mode]
    use_scratch = nk > 1 and out_dtype != f32

    def body(*refs):
        if acc_in is not None:
            a_ref, b_ref, c_ref, o_ref, *scr = refs
        else:
            a_ref, b_ref, o_ref, *scr = refs
            c_ref = None
        p = _mmb(a_ref[...], b_ref[...], dims)
        if nk == 1:
            if c_ref is not None:
                p = p + c_ref[...]
            o_ref[...] = p.astype(out_dtype)
            return
        k = pl.program_id(2)
        acc = scr[0] if use_scratch else o_ref

        @pl.when(k == 0)
        def _():
            acc[...] = p if c_ref is None else p + c_ref[...]

        @pl.when(k > 0)
        def _():
            acc[...] += p

        if use_scratch:
            @pl.when(k == nk - 1)
            def _():
                o_ref[...] = acc[...].astype(out_dtype)

    if mode == "tn":
        a_spec = pl.BlockSpec((tk, tm), lambda i, j, k: (k, i))
    else:
        a_spec = pl.BlockSpec((tm, tk), lambda i, j, k: (i, k))
    if mode == "nt":
        b_spec = pl.BlockSpec((tn, tk), lambda i, j, k: (j, k))
    else:
        b_spec = pl.BlockSpec((tk, tn), lambda i, j, k: (k, j))
    if col_slabs:
        assert acc_in is None
        ni, nj = M // 2 // tm, N // N_CHIP // tn
        o_spec = pl.BlockSpec((None, tm, tn), lambda i, j, k: (2 * (j // nj) + i // ni, i % ni, j % nj))
        o_shape = jax.ShapeDtypeStruct((N_DEV, M // 2, N // N_CHIP), out_dtype)
    else:
        o_spec = pl.BlockSpec((tm, tn), lambda i, j, k: (i, j))
        o_shape = jax.ShapeDtypeStruct((M, N), out_dtype)
    in_specs = [a_spec, b_spec]
    args = [a, b]
    if acc_in is not None:
        in_specs.append(o_spec)
        args.append(acc_in)
    out = _call(body, name=name, grid=(M // tm, N // tn, nk), in_specs=in_specs, out_specs=[o_spec],
                out_shape=[o_shape], args=args,
                scratch_shapes=[pltpu.VMEM((tm, tn), f32)] if use_scratch else [],
                sem=("parallel", "parallel", "arbitrary"), side=side)
    return out[0] if side is None else (out[0], out[1:])


def _embed_norm(x3, m64, w, side=None):
    _, S, D = x3.shape
    Lp = OFF + S

    def body(x_ref, m_ref, w_ref, h_ref, n_ref):
        i = pl.program_id(0)
        h = jnp.where(i == 0, m_ref[...], x_ref[...])
        h_ref[...] = h
        xh, _ = _rms(h)
        n_ref[...] = (xh * w_ref[...]).astype(bf16)

    row = pl.BlockSpec((ROWS, D), lambda i: (i, 0))
    out = _call(
        body, name="embed_norm", grid=(Lp // ROWS,),
        in_specs=[pl.BlockSpec((None, ROWS, D), lambda i: (0, jnp.maximum(i - 1, 0), 0)),
                  pl.BlockSpec((ROWS, D), lambda i: (0, 0)),
                  pl.BlockSpec((1, D), lambda i: (0, 0))],
        out_specs=[row, row],
        out_shape=[jax.ShapeDtypeStruct((Lp, D), f32), jax.ShapeDtypeStruct((Lp, D), bf16)],
        args=[x3, m64, w], sem=("parallel",), side=side)
    return out[0], out[1], out[2:]


def _add_norm(h, d, w):
    Lp, D = h.shape
    tr = _tile(Lp, 256)

    def body(h_ref, d_ref, w_ref, o_ref, n_ref):
        h1 = h_ref[...] + d_ref[...]
        o_ref[...] = h1
        xh, _ = _rms(h1)
        n_ref[...] = (xh * w_ref[...]).astype(bf16)

    row = pl.BlockSpec((tr, D), lambda i: (i, 0))
    return pl.pallas_call(
        body, name="add_norm", grid=(Lp // tr,),
        in_specs=[row, row, pl.BlockSpec((1, D), lambda i: (0, 0))], out_specs=[row, row],
        out_shape=[jax.ShapeDtypeStruct((Lp, D), f32), jax.ShapeDtypeStruct((Lp, D), bf16)],
        compiler_params=_params("parallel"),
    )(h, d, w)


def _norm_bwd(dn, h, dh, w):
    Lp, D = h.shape
    tr = _tile(Lp, 256)

    def body(dn_ref, h_ref, dh_ref, w_ref, o_ref, ob_ref, gw_ref):
        i = pl.program_id(0)
        xh, r = _rms(h_ref[...])
        dn_ = dn_ref[...]
        o = dh_ref[...] + _rms_bwd(dn_, xh, r, w_ref[...])
        o_ref[...] = o
        ob_ref[...] = o.astype(bf16)
        gw = jnp.sum(dn_ * xh, axis=0, keepdims=True)

        @pl.when(i == 0)
        def _():
            gw_ref[...] = gw

        @pl.when(i > 0)
        def _():
            gw_ref[...] += gw

    row = pl.BlockSpec((tr, D), lambda i: (i, 0))
    vec = pl.BlockSpec((1, D), lambda i: (0, 0))
    return pl.pallas_call(
        body, name="norm_bwd", grid=(Lp // tr,), in_specs=[row, row, row, vec], out_specs=[row, row, vec],
        out_shape=[jax.ShapeDtypeStruct((Lp, D), f32), jax.ShapeDtypeStruct((Lp, D), bf16),
                   jax.ShapeDtypeStruct((1, D), f32)],
        compiler_params=_params("arbitrary"),
    )(dn, h, dh, w)


def _embed_norm_bwd(dn, h, dh, w, S):
    Lp, D = h.shape

    def body(dn_ref, h_ref, dh_ref, w_ref, gx_ref, gm_ref, gw_ref):
        i = pl.program_id(0)
        xh, r = _rms(h_ref[...])
        dn_ = dn_ref[...]
        d0 = dh_ref[...] + _rms_bwd(dn_, xh, r, w_ref[...])
        gx_ref[...] = d0
        gw = jnp.sum(dn_ * xh, axis=0, keepdims=True)

        @pl.when(i == 0)
        def _():
            gm_ref[...] = d0[PAD:OFF, :]
            gw_ref[...] = gw

        @pl.when(i > 0)
        def _():
            gw_ref[...] += gw

    row = pl.BlockSpec((ROWS, D), lambda i: (i, 0))
    vec = pl.BlockSpec((1, D), lambda i: (0, 0))
    return pl.pallas_call(
        body, name="embed_norm_bwd", grid=(Lp // ROWS,), in_specs=[row, row, row, vec],
        out_specs=[pl.BlockSpec((None, ROWS, D), lambda i: (0, jnp.maximum(i - 1, 0), 0)),
                   pl.BlockSpec((N_META, D), lambda i: (0, 0)), vec],
        out_shape=[jax.ShapeDtypeStruct((1, S, D), f32), jax.ShapeDtypeStruct((N_META, D), f32),
                   jax.ShapeDtypeStruct((1, D), f32)],
        compiler_params=_params("arbitrary"),
    )(dn, h, dh, w)


def _final(h1, ffn, tgt3, w):
    Lp, D = h1.shape

    def body(h_ref, f_ref, t_ref, w_ref, d_ref, db_ref, l_ref, gw_ref):
        i = pl.program_id(0)
        h2 = h_ref[...] + f_ref[...]
        xh, r = _rms(h2)
        w_ = w_ref[...]
        e = xh * w_ - t_ref[...]
        valid = (i > 0).astype(f32)
        loss = 0.5 * jnp.sum(jnp.mean(e * e, axis=-1, keepdims=True), axis=0, keepdims=True) * valid
        dy = e * (valid / D)
        d = _rms_bwd(dy, xh, r, w_)
        d_ref[...] = d
        db_ref[...] = d.astype(bf16)
        gw = jnp.sum(dy * xh, axis=0, keepdims=True)

        @pl.when(i == 0)
        def _():
            l_ref[...] = jnp.zeros_like(l_ref)
            gw_ref[...] = jnp.zeros_like(gw_ref)

        l_ref[...] += jnp.broadcast_to(loss, l_ref.shape)
        gw_ref[...] += gw

    row = pl.BlockSpec((ROWS, D), lambda i: (i, 0))
    vec = pl.BlockSpec((1, D), lambda i: (0, 0))
    return pl.pallas_call(
        body, name="final_loss", grid=(Lp // ROWS,),
        in_specs=[row, row, pl.BlockSpec((None, ROWS, D), lambda i: (0, jnp.maximum(i - 1, 0), 0)), vec],
        out_specs=[row, row, pl.BlockSpec((8, 128), lambda i: (0, 0)), vec],
        out_shape=[jax.ShapeDtypeStruct((Lp, D), f32), jax.ShapeDtypeStruct((Lp, D), bf16),
                   jax.ShapeDtypeStruct((8, 128), f32), jax.ShapeDtypeStruct((1, D), f32)],
        compiler_params=_params("arbitrary"),
    )(h1, ffn, tgt3, w)


def _ffn_in(n, w_gate, w_up, side=None):
    M, K = n.shape
    F = w_gate.shape[1]
    tm = _tile(M, 1408)
    tn = _tile(F, 512, 128)

    def body(a_ref, bg_ref, bu_ref, act_ref, g_ref, u_ref):
        a = a_ref[...]
        g = _mmb(a, bg_ref[...], NN)
        u = _mmb(a, bu_ref[...], NN)
        act_ref[...] = (_silu(g) * u).astype(bf16)
        g_ref[...] = g.astype(bf16)
        u_ref[...] = u.astype(bf16)

    wsp = pl.BlockSpec((K, tn), lambda i, j: (0, j))
    osp = pl.BlockSpec((tm, tn), lambda i, j: (i, j))
    out = _call(body, name="ffn_in", grid=(M // tm, F // tn),
                in_specs=[pl.BlockSpec((tm, K), lambda i, j: (i, 0)), wsp, wsp], out_specs=[osp] * 3,
                out_shape=[jax.ShapeDtypeStruct((M, F), bf16)] * 3, args=[n, w_gate, w_up],
                sem=("parallel", "parallel"), side=side)
    return out[0], out[1], out[2], out[3:]


def _ffn_dact(d, w_down, g, u):
    M, K = d.shape
    F = w_down.shape[0]
    tm = _tile(M, 1408)
    tn = _tile(F, 512, 128)

    def body(d_ref, w_ref, g_ref, u_ref, dg_ref, du_ref):
        da = _mmb(d_ref[...], w_ref[...], NT)
        g_ = g_ref[...].astype(f32)
        dg_ref[...] = (da * u_ref[...].astype(f32) * _dsilu(g_)).astype(bf16)
        du_ref[...] = (da * _silu(g_)).astype(bf16)

    osp = pl.BlockSpec((tm, tn), lambda i, j: (i, j))
    return pl.pallas_call(
        body, name="ffn_dact", grid=(M // tm, F // tn),
        in_specs=[pl.BlockSpec((tm, K), lambda i, j: (i, 0)), pl.BlockSpec((tn, K), lambda i, j: (j, 0)), osp, osp],
        out_specs=[osp, osp], out_shape=[jax.ShapeDtypeStruct((M, F), bf16)] * 2,
        compiler_params=_params("parallel", "parallel"),
    )(d, w_down, g, u)


def _gates(psm, w2p, gate_b, alog, dtb):
    Lp = psm.shape[0]
    tr = _tile(Lp, 256)

    def body(p_ref, w_ref, b_ref, a_ref, t_ref, gb_ref, la_ref):
        i = pl.program_id(0)
        psm_ = p_ref[...]
        lane = lax.broadcasted_iota(jnp.int32, psm_.shape, 1)
        rowi = lax.broadcasted_iota(jnp.int32, (tr, 1), 0) + i * tr
        g = -jnp.exp(a_ref[...]) * _softplus(psm_ + t_ref[...])
        beta = _sigmoid(psm_)
        gb = jnp.where(lane < GDN_H, g, jnp.where(lane < 2 * GDN_H, beta, 0.0))
        gb_ref[...] = gb * (rowi >= PAD).astype(f32)
        logit = _mmb(psm_, w_ref[...], NN) + b_ref[...]
        la_ref[...] = _log_sigmoid(logit) * (1.0 / GATE_NORMALIZER)

    row = pl.BlockSpec((tr, SM_W), lambda i: (i, 0))
    return pl.pallas_call(
        body, name="gates", grid=(Lp // tr,),
        in_specs=[row, pl.BlockSpec((SM_W, GLA_QK), lambda i: (0, 0)), pl.BlockSpec((1, GLA_QK), lambda i: (0, 0)),
                  pl.BlockSpec((1, SM_W), lambda i: (0, 0)), pl.BlockSpec((1, SM_W), lambda i: (0, 0))],
        out_specs=[row, pl.BlockSpec((tr, GLA_QK), lambda i: (i, 0))],
        out_shape=[jax.ShapeDtypeStruct((Lp, SM_W), f32), jax.ShapeDtypeStruct((Lp, GLA_QK), f32)],
        compiler_params=_params("parallel"),
    )(psm, w2p, gate_b, alog, dtb)


def _gates_bwd(psm, w2p, gate_b, alog, dtb, dgb, dla):
    Lp = psm.shape[0]
    tr = _tile(Lp, 256)

    def body(p_ref, w_ref, b_ref, a_ref, t_ref, dgb_ref, dla_ref, dp_ref, gw_ref, gb_ref, ga_ref, gt_ref):
        i = pl.program_id(0)
        psm_ = p_ref[...]
        lane = lax.broadcasted_iota(jnp.int32, psm_.shape, 1)
        rowi = lax.broadcasted_iota(jnp.int32, (tr, 1), 0) + i * tr
        d = dgb_ref[...] * (rowi >= PAD).astype(f32)
        ea = jnp.exp(a_ref[...])
        z = psm_ + t_ref[...]
        is_g = lane < GDN_H
        dz = jnp.where(is_g, -ea * _sigmoid(z) * d, 0.0)
        dalog = jnp.where(is_g, -ea * _softplus(z) * d, 0.0)
        beta = _sigmoid(psm_)
        dbeta = jnp.where(jnp.logical_and(lane >= GDN_H, lane < 2 * GDN_H), beta * (1.0 - beta) * d, 0.0)
        logit = _mmb(psm_, w_ref[...], NN) + b_ref[...]
        dlogit = dla_ref[...] * (_sigmoid(-logit) * (1.0 / GATE_NORMALIZER))
        dlr = _mmb(dlogit, w_ref[...], NT)
        dp_ref[...] = (dz + dbeta + dlr).astype(bf16)
        gw = _mmb(psm_, dlogit, TN)
        gb = jnp.sum(dlogit, axis=0, keepdims=True)
        ga = jnp.sum(dalog, axis=0, keepdims=True)
        gt = jnp.sum(dz, axis=0, keepdims=True)

        @pl.when(i == 0)
        def _():
            gw_ref[...] = gw
            gb_ref[...] = gb
            ga_ref[...] = ga
            gt_ref[...] = gt

        @pl.when(i > 0)
        def _():
            gw_ref[...] += gw
            gb_ref[...] += gb
            ga_ref[...] += ga
            gt_ref[...] += gt

    row = pl.BlockSpec((tr, SM_W), lambda i: (i, 0))
    wsp = pl.BlockSpec((SM_W, GLA_QK), lambda i: (0, 0))
    bsp = pl.BlockSpec((1, GLA_QK), lambda i: (0, 0))
    vsp = pl.BlockSpec((1, SM_W), lambda i: (0, 0))
    return pl.pallas_call(
        body, name="gates_bwd", grid=(Lp // tr,),
        in_specs=[row, wsp, bsp, vsp, vsp, row, pl.BlockSpec((tr, GLA_QK), lambda i: (i, 0))],
        out_specs=[row, wsp, bsp, vsp, vsp],
        out_shape=[jax.ShapeDtypeStruct((Lp, SM_W), bf16), jax.ShapeDtypeStruct((SM_W, GLA_QK), f32),
                   jax.ShapeDtypeStruct((1, GLA_QK), f32), jax.ShapeDtypeStruct((1, SM_W), f32),
                   jax.ShapeDtypeStruct((1, SM_W), f32)],
        compiler_params=_params("arbitrary"),
    )(psm, w2p, gate_b, alog, dtb, dgb, dla)


def _conv_pre(x_ext, w, n):
    rows = x_ext.shape[0]
    y = x_ext * w[CONV_K - 1:CONV_K, :]
    for s in range(1, CONV_K):
        y = y + pltpu.roll(x_ext, s, 0) * w[CONV_K - 1 - s:CONV_K - s, :]
    return y[rows - n:, :]


def _conv(proj, cw, side=None):
    Lp = proj.shape[0]
    W = cw.shape[1]
    tr = _tile(Lp, 256, 64)
    tc = _tile(W, 1024, 128)
    c0 = C_QKV // tc

    def body(h_ref, x_ref, w_ref, o_ref):
        i = pl.program_id(1)
        halo = jnp.where(i == 0, 0.0, h_ref[...])
        x_ext = jnp.concatenate([halo, x_ref[...]], axis=0)
        o_ref[...] = _silu(_conv_pre(x_ext, w_ref[...], tr))

    out = _call(
        body, name="conv", grid=(W // tc, Lp // tr),
        in_specs=[pl.BlockSpec((8, tc), lambda j, i: (jnp.maximum(i * (tr // 8) - 1, 0), j + c0)),
                  pl.BlockSpec((tr, tc), lambda j, i: (i, j + c0)),
                  pl.BlockSpec((CONV_K, tc), lambda j, i: (0, j))],
        out_specs=[pl.BlockSpec((tr, tc), lambda j, i: (i, j))],
        out_shape=[jax.ShapeDtypeStruct((Lp, W), f32)], args=[proj, proj, cw],
        sem=("parallel", "parallel"), side=side)
    return out[0] if side is None else (out[0], out[1:])


def _conv_bwd(proj, cw, dy, dproj, side=None):
    Lp = proj.shape[0]
    W = cw.shape[1]
    tr = _tile(Lp, 256, 64)
    tc = _tile(W, 1024, 128)
    c0 = C_QKV // tc
    nr = Lp // tr
    last8 = Lp // 8 - 1

    def body(xp_ref, x_ref, xn_ref, w_ref, d_ref, dn_ref, dproj_ref, o_ref, gw_ref):
        del dproj_ref
        i = pl.program_id(1)
        w = w_ref[...]
        xp = jnp.where(i == 0, 0.0, xp_ref[...])
        x_ext = jnp.concatenate([xp, x_ref[...], xn_ref[...]], axis=0)
        n = tr + 8
        pre = _conv_pre(x_ext, w, n)
        dn = jnp.where(i == nr - 1, 0.0, dn_ref[...])
        dpre = jnp.concatenate([d_ref[...], dn], axis=0) * _dsilu(pre)
        dx = dpre * w[CONV_K - 1:CONV_K, :]
        for s in range(1, CONV_K):
            dx = dx + pltpu.roll(dpre, n - s, 0) * w[CONV_K - 1 - s:CONV_K - s, :]
        o_ref[...] = dx[:tr, :].astype(bf16)
        dp = dpre[:tr, :]
        rows = []
        for k in range(CONV_K):
            xs = x_ext if k == CONV_K - 1 else pltpu.roll(x_ext, CONV_K - 1 - k, 0)
            rows.append(jnp.sum(dp * xs[8:8 + tr, :], axis=0, keepdims=True))
        gw = jnp.concatenate(rows, axis=0)

        @pl.when(i == 0)
        def _():
            gw_ref[...] = gw

        @pl.when(i > 0)
        def _():
            gw_ref[...] += gw

    cur = pl.BlockSpec((tr, tc), lambda j, i: (i, j))
    nxt = pl.BlockSpec((8, tc), lambda j, i: (jnp.minimum((i + 1) * (tr // 8), last8), j))
    pcur = pl.BlockSpec((tr, tc), lambda j, i: (i, j + c0))
    pprev = pl.BlockSpec((8, tc), lambda j, i: (jnp.maximum(i * (tr // 8) - 1, 0), j + c0))
    pnext = pl.BlockSpec((8, tc), lambda j, i: (jnp.minimum((i + 1) * (tr // 8), last8), j + c0))
    wsp = pl.BlockSpec((CONV_K, tc), lambda j, i: (0, j))
    out = _call(
        body, name="conv_bwd", grid=(W // tc, nr),
        in_specs=[pprev, pcur, pnext, wsp, cur, nxt, _ANY], out_specs=[pcur, wsp],
        out_shape=[jax.ShapeDtypeStruct(dproj.shape, dproj.dtype), jax.ShapeDtypeStruct((CONV_K, W), f32)],
        aliases={6: 0}, args=[proj, proj, proj, cw, dy, dy, dproj], sem=("parallel", "arbitrary"), side=side)
    return out[0], out[1], out[2:]


def _gdn_heads(x_ref, gbv):
    R = range(GDN_H)
    return ([x_ref[:, Q0 + h * GDN_DK:Q0 + (h + 1) * GDN_DK] for h in R],
            [x_ref[:, K0 + h * GDN_DK:K0 + (h + 1) * GDN_DK] for h in R],
            [x_ref[:, V0 + h * GDN_DV:V0 + (h + 1) * GDN_DV] for h in R],
            [gbv[:, GDN_H + h:GDN_H + h + 1] for h in R],
            [gbv[:, h:h + 1] for h in R])


def _gdn_fwd(qkvc, gb, side=None):
    Lp = qkvc.shape[0]
    N = Lp // GDN_C

    def body(x_ref, gb_ref, o_ref, sall_ref, pall_ref, s_scr):
        @pl.when(pl.program_id(0) == 0)
        def _():
            s_scr[...] = jnp.zeros_like(s_scr)

        R = range(GDN_H)
        Ss = [s_scr[h] for h in R]
        for h in R:
            sall_ref[0, h] = Ss[h]
        S2, o, p = _gdn_chunk(Ss, *_gdn_heads(x_ref, gb_ref[...]))
        for h in R:
            s_scr[h] = S2[h]
            o_ref[:, h * GDN_DV:(h + 1) * GDN_DV] = o[h]
            pall_ref[0, h] = p[h]

    out = _call(
        body, name="gdn_fwd", grid=(N,),
        in_specs=[pl.BlockSpec((GDN_C, QKV_W), lambda n: (n, 0)), pl.BlockSpec((GDN_C, SM_W), lambda n: (n, 0))],
        out_specs=[pl.BlockSpec((GDN_C, GDN_V), lambda n: (n, 0)),
                   pl.BlockSpec((1, GDN_H, GDN_DK, GDN_DV), lambda n: (n, 0, 0, 0)),
                   pl.BlockSpec((1, GDN_H, GDN_C, GDN_C), lambda n: (n, 0, 0, 0))],
        out_shape=[jax.ShapeDtypeStruct((Lp, GDN_V), f32), jax.ShapeDtypeStruct((N, GDN_H, GDN_DK, GDN_DV), f32),
                   jax.ShapeDtypeStruct((N, GDN_H, GDN_C, GDN_C), f32)],
        scratch_shapes=[pltpu.VMEM((GDN_H, GDN_DK, GDN_DV), f32)], args=[qkvc, gb], sem=("arbitrary",), side=side)
    return out[0], out[1], out[2], out[3:]


def _gdn_bwd(qkvc, gb, sall, pall, do, side=None):
    Lp = qkvc.shape[0]
    N = Lp // GDN_C

    def body(x_ref, gb_ref, sall_ref, pall_ref, do_ref, dx_ref, dgb_ref, ds_scr):
        @pl.when(pl.program_id(0) == 0)
        def _():
            ds_scr[...] = jnp.zeros_like(ds_scr)

        R = range(GDN_H)
        lane = lax.broadcasted_iota(jnp.int32, (GDN_C, SM_W), 1)
        ps = [pall_ref[0, h] for h in R]
        _, vjp = jax.vjp(lambda *a: _gdn_chunk(*a, Ps=ps)[:2],
                         [sall_ref[0, h] for h in R], *_gdn_heads(x_ref, gb_ref[...]))
        dS, dq, dk, dv, dbeta, dg = vjp(([ds_scr[h] for h in R],
                                         [do_ref[:, h * GDN_DV:(h + 1) * GDN_DV] for h in R]))
        acc = jnp.zeros((GDN_C, SM_W), f32)
        for h in R:
            ds_scr[h] = dS[h]
            dx_ref[:, Q0 + h * GDN_DK:Q0 + (h + 1) * GDN_DK] = dq[h]
            dx_ref[:, K0 + h * GDN_DK:K0 + (h + 1) * GDN_DK] = dk[h]
            dx_ref[:, V0 + h * GDN_DV:V0 + (h + 1) * GDN_DV] = dv[h]
            acc = acc + jnp.where(lane == h, dg[h], 0.0) + jnp.where(lane == GDN_H + h, dbeta[h], 0.0)
        dgb_ref[...] = acc

    rev = lambda n: (N - 1 - n, 0)
    out = _call(
        body, name="gdn_bwd", grid=(N,),
        in_specs=[pl.BlockSpec((GDN_C, QKV_W), rev), pl.BlockSpec((GDN_C, SM_W), rev),
                  pl.BlockSpec((1, GDN_H, GDN_DK, GDN_DV), lambda n: (N - 1 - n, 0, 0, 0)),
                  pl.BlockSpec((1, GDN_H, GDN_C, GDN_C), lambda n: (N - 1 - n, 0, 0, 0)),
                  pl.BlockSpec((GDN_C, GDN_V), rev)],
        out_specs=[pl.BlockSpec((GDN_C, QKV_W), rev), pl.BlockSpec((GDN_C, SM_W), rev)],
        out_shape=[jax.ShapeDtypeStruct((Lp, QKV_W), f32), jax.ShapeDtypeStruct((Lp, SM_W), f32)],
        scratch_shapes=[pltpu.VMEM((GDN_H, GDN_DK, GDN_DV), f32)], args=[qkvc, gb, sall, pall, do],
        sem=("arbitrary",), side=side)
    return out[0], out[1], out[2:]


GLA_SUB = ROWS // GLA_C


def _gla_slices(h):
    sq = slice(h * GLA_DK, (h + 1) * GLA_DK)
    sk = slice(GLA_QK + h * GLA_DK, GLA_QK + (h + 1) * GLA_DK)
    sv = slice(2 * GLA_QK + h * GLA_DV, 2 * GLA_QK + (h + 1) * GLA_DV)
    return sq, sk, sv


def _gla_heads(x_ref, la_ref, r):
    sl = [_gla_slices(h) for h in range(GLA_H)]
    return ([x_ref[r, s[0]] for s in sl], [x_ref[r, s[1]] for s in sl], [x_ref[r, s[2]] for s in sl],
            [la_ref[r, s[0]] for s in sl])


def _gla_fwd(proj, la):
    Lp = proj.shape[0]
    NB = Lp // ROWS

    def body(x_ref, la_ref, o_ref, sall_ref, s_scr):
        @pl.when(pl.program_id(0) == 0)
        def _():
            s_scr[...] = jnp.zeros_like(s_scr)

        def sub(c, carry):
            r = pl.ds(pl.multiple_of(c * GLA_C, GLA_C), GLA_C)
            R = range(GLA_H)
            Sts = [s_scr[h] for h in R]
            for h in R:
                sall_ref[c, h] = Sts[h]
            St2, o = _gla_chunk(Sts, *_gla_heads(x_ref, la_ref, r))
            for h in R:
                s_scr[h] = St2[h]
                o_ref[r, h * GLA_DV:(h + 1) * GLA_DV] = o[h]
            return carry

        lax.fori_loop(0, GLA_SUB, sub, 0)

    return pl.pallas_call(
        body, name="gla_fwd", grid=(NB,),
        in_specs=[pl.BlockSpec((ROWS, G_W), lambda n: (n, C_G // G_W)),
                  pl.BlockSpec((ROWS, GLA_QK), lambda n: (n, 0))],
        out_specs=[pl.BlockSpec((ROWS, GLA_V), lambda n: (n, 0)),
                   pl.BlockSpec((GLA_SUB, GLA_H, GLA_DV, GLA_DK), lambda n: (n, 0, 0, 0))],
        out_shape=[jax.ShapeDtypeStruct((Lp, GLA_V), f32),
                   jax.ShapeDtypeStruct((Lp // GLA_C, GLA_H, GLA_DV, GLA_DK), f32)],
        scratch_shapes=[pltpu.VMEM((GLA_H, GLA_DV, GLA_DK), f32)],
        compiler_params=_params("arbitrary"),
    )(proj, la)


def _gla_bwd(proj, la, sall, do, dproj, side=None):
    Lp = proj.shape[0]
    NB = Lp // ROWS

    def body(x_ref, la_ref, sall_ref, do_ref, dproj_ref, dx_ref, dla_ref, ds_scr):
        del dproj_ref

        @pl.when(pl.program_id(0) == 0)
        def _():
            ds_scr[...] = jnp.zeros_like(ds_scr)

        def sub(ci, carry):
            c = GLA_SUB - 1 - ci
            r = pl.ds(pl.multiple_of(c * GLA_C, GLA_C), GLA_C)
            R = range(GLA_H)
            _, vjp = jax.vjp(_gla_chunk, [sall_ref[c, h] for h in R], *_gla_heads(x_ref, la_ref, r))
            dS, dq, dk, dv, dl = vjp(([ds_scr[h] for h in R], [do_ref[r, h * GLA_DV:(h + 1) * GLA_DV] for h in R]))
            for h in R:
                sq, sk, sv = _gla_slices(h)
                ds_scr[h] = dS[h]
                dx_ref[r, sq] = dq[h].astype(bf16)
                dx_ref[r, sk] = dk[h].astype(bf16)
                dx_ref[r, sv] = dv[h].astype(bf16)
                dla_ref[r, sq] = dl[h]
            return carry

        lax.fori_loop(0, GLA_SUB, sub, 0)

    x_spec = pl.BlockSpec((ROWS, G_W), lambda n: (NB - 1 - n, C_G // G_W))
    rev = lambda n: (NB - 1 - n, 0)
    out = _call(
        body, name="gla_bwd", grid=(NB,),
        in_specs=[x_spec, pl.BlockSpec((ROWS, GLA_QK), rev),
                  pl.BlockSpec((GLA_SUB, GLA_H, GLA_DV, GLA_DK), lambda n: (NB - 1 - n, 0, 0, 0)),
                  pl.BlockSpec((ROWS, GLA_V), rev), _ANY],
        out_specs=[x_spec, pl.BlockSpec((ROWS, GLA_QK), rev)],
        out_shape=[jax.ShapeDtypeStruct(dproj.shape, dproj.dtype), jax.ShapeDtypeStruct((Lp, GLA_QK), f32)],
        aliases={4: 0}, scratch_shapes=[pltpu.VMEM((GLA_H, GLA_DV, GLA_DK), f32)],
        args=[proj, la, sall, do, dproj], sem=("arbitrary",), side=side)
    return out[0], out[1], out[2:]


def _gated_norm_fn(og, ol, zr, wg, wl):
    outs = []
    for h in range(GDN_H):
        s = slice(h * GDN_DV, (h + 1) * GDN_DV)
        outs.append(_rms(og[:, s])[0] * wg * _silu(zr[:, s]))
    for h in range(GLA_H):
        s = slice(h * GLA_DV, (h + 1) * GLA_DV)
        sr = slice(GDN_V + h * GLA_DV, GDN_V + (h + 1) * GLA_DV)
        outs.append(_rms(ol[:, s])[0] * wl * _silu(zr[:, sr]))
    return jnp.concatenate(outs, axis=-1)


def _gated_norm(og, ol, proj, wg, wl):
    Lp = og.shape[0]
    tr = _tile(Lp, 256)

    def body(og_ref, ol_ref, zr_ref, wg_ref, wl_ref, o_ref):
        o_ref[...] = _gated_norm_fn(og_ref[...], ol_ref[...], zr_ref[...], wg_ref[...], wl_ref[...]).astype(bf16)

    return pl.pallas_call(
        body, name="gated_norm", grid=(Lp // tr,),
        in_specs=[pl.BlockSpec((tr, GDN_V), lambda i: (i, 0)), pl.BlockSpec((tr, GLA_V), lambda i: (i, 0)),
                  pl.BlockSpec((tr, ZR_W), lambda i: (i, C_ZR // ZR_W)),
                  pl.BlockSpec((1, GDN_DV), lambda i: (0, 0)), pl.BlockSpec((1, GLA_DV), lambda i: (0, 0))],
        out_specs=pl.BlockSpec((tr, ZR_W), lambda i: (i, 0)),
        out_shape=jax.ShapeDtypeStruct((Lp, ZR_W), bf16),
        compiler_params=_params("parallel"),
    )(og, ol, proj, wg, wl)


def _gated_norm_bwd(og, ol, proj, wg, wl, dmix):
    Lp = og.shape[0]
    tr = _tile(Lp, 128)

    def body(og_ref, ol_ref, zr_ref, wg_ref, wl_ref, d_ref, dog_ref, dol_ref, dzr_ref, gwg_ref, gwl_ref):
        i = pl.program_id(0)
        _, vjp = jax.vjp(_gated_norm_fn, og_ref[...], ol_ref[...], zr_ref[...], wg_ref[...], wl_ref[...])
        dog, dol, dzr, gwg, gwl = vjp(d_ref[...])
        dog_ref[...] = dog
        dol_ref[...] = dol
        dzr_ref[...] = dzr.astype(bf16)

        @pl.when(i == 0)
        def _():
            gwg_ref[...] = gwg
            gwl_ref[...] = gwl

        @pl.when(i > 0)
        def _():
            gwg_ref[...] += gwg
            gwl_ref[...] += gwl

    og_spec = pl.BlockSpec((tr, GDN_V), lambda i: (i, 0))
    ol_spec = pl.BlockSpec((tr, GLA_V), lambda i: (i, 0))
    zr_spec = pl.BlockSpec((tr, ZR_W), lambda i: (i, C_ZR // ZR_W))
    vg = pl.BlockSpec((1, GDN_DV), lambda i: (0, 0))
    vl = pl.BlockSpec((1, GLA_DV), lambda i: (0, 0))
    return pl.pallas_call(
        body, name="gated_norm_bwd", grid=(Lp // tr,),
        in_specs=[og_spec, ol_spec, zr_spec, vg, vl, pl.BlockSpec((tr, ZR_W), lambda i: (i, 0))],
        out_specs=[og_spec, ol_spec, zr_spec, vg, vl],
        out_shape=[jax.ShapeDtypeStruct((Lp, GDN_V), f32), jax.ShapeDtypeStruct((Lp, GLA_V), f32),
                   jax.ShapeDtypeStruct((Lp, C_END), bf16),
                   jax.ShapeDtypeStruct((1, GDN_DV), f32), jax.ShapeDtypeStruct((1, GLA_DV), f32)],
        compiler_params=_params("arbitrary"),
    )(og, ol, proj, wg, wl, dmix)


def _adamw(g, w, m, v, name):
    R, C = g.shape
    tr = _tile(R, 256, 8) if R % 8 == 0 and R > 256 else R
    c1 = 1.0 - ADAM_B1 ** ADAM_STEP
    c2 = 1.0 - ADAM_B2 ** ADAM_STEP

    def body(g_ref, w_ref, m_ref, v_ref, d_ref, mo_ref, vo_ref):
        g_ = g_ref[...]
        m2 = ADAM_B1 * m_ref[...] + (1.0 - ADAM_B1) * g_
        v2 = ADAM_B2 * v_ref[...] + (1.0 - ADAM_B2) * (g_ * g_)
        mo_ref[...] = m2
        vo_ref[...] = v2
        d_ref[...] = -ADAM_LR * ((m2 / c1) / (jnp.sqrt(v2 / c2) + ADAM_EPS) + ADAM_WD * w_ref[...])

    blk = pl.BlockSpec((tr, C), lambda i: (i, 0))
    return pl.pallas_call(
        body, name=name, grid=(R // tr,), in_specs=[blk] * 4, out_specs=[blk] * 3,
        out_shape=[jax.ShapeDtypeStruct((R, C), f32)] * 3,
        compiler_params=_params("parallel"),
    )(g, w, m, v)


def _sum_slots(r, name):
    n, R, C = r.shape
    tr = _tile(R, 128, 16) if R % 16 == 0 and R > 128 else R

    def body(r_ref, o_ref):
        acc = r_ref[0].astype(f32)
        for s in range(1, n):
            acc = acc + r_ref[s].astype(f32)
        o_ref[...] = acc

    return pl.pallas_call(
        body, name=name, grid=(R // tr,),
        in_specs=[pl.BlockSpec((n, tr, C), lambda i: (0, i, 0))],
        out_specs=pl.BlockSpec((tr, C), lambda i: (i, 0)),
        out_shape=jax.ShapeDtypeStruct((R, C), f32),
        compiler_params=_params("parallel"),
    )(r)


class _Siblings:
    def __init__(self, arrays):
        self.arrays = list(arrays)
        self.n = len(self.arrays)
        self.out_shape = [jax.ShapeDtypeStruct((2,) + a.shape, a.dtype) for a in self.arrays]
        self.sems = [pltpu.SemaphoreType.DMA((self.n,)), pltpu.SemaphoreType.DMA((self.n,)),
                     pltpu.SemaphoreType.DMA((self.n,))]

    def hooks(self, ins, outs, send, recv, lsem):
        def copies():
            x, y, c = lax.axis_index("x"), lax.axis_index("y"), lax.axis_index("c")
            out = []
            for a in range(self.n):
                out.append((pltpu.make_async_copy(ins[a], outs[a].at[c], lsem.at[a]), None))
                mk = lambda dst, a=a: pltpu.make_async_remote_copy(
                    src_ref=ins[a], dst_ref=dst, send_sem=send.at[a], recv_sem=recv.at[a],
                    device_id=(x, y, 1 - c), device_id_type=MESH)
                out.append((mk(outs[a].at[c]), mk(outs[a].at[1 - c])))
            return out

        return _start_wait(copies)


def _comm_now(name, sides):
    total = sum(s.n for s in sides)

    def body(*refs):
        ins, outs, sems = refs[:total], refs[total:2 * total], refs[2 * total:]
        hooks, o = [], 0
        for i, s in enumerate(sides):
            hooks.append(s.hooks(ins[o:o + s.n], outs[o:o + s.n], *sems[3 * i:3 * i + 3]))
            o += s.n
        for start, _ in hooks:
            start()
        for _, wait in hooks:
            wait()

    out = pl.pallas_call(
        body, name=name, in_specs=[_ANY] * total, out_specs=[_ANY] * total,
        out_shape=[sh for s in sides for sh in s.out_shape], scratch_shapes=[sm for s in sides for sm in s.sems],
    )(*[a for s in sides for a in s.arrays])
    res, o = [], 0
    for s in sides:
        res.append(list(out[o:o + s.n]))
        o += s.n
    return res


def _cat_cols(g):
    return jnp.concatenate([g[i] for i in range(N_CHIP)], axis=-1)


def _row_slabs(a):
    return a.reshape(N_DEV, a.shape[0] // N_DEV, a.shape[1])


def _w_in_columns(g_wp, g_wsm):
    return jnp.concatenate([g_wp[:, C_QKV:C_END], g_wp[:, C_ZR:C_ZR + GDN_V], g_wsm[:, :SM_LR],
                            g_wp[:, C_G:C_G + G_W], g_wp[:, C_ZR + GDN_V:C_ZR + ZR_W],
                            g_wsm[:, SM_LR:SM_LR + GATE_RANK]], axis=1)


def _step(x, loss_target, p, meta, shard):
    _, S, D = x.shape
    alog_p = jnp.pad(p["gdn_a_log"], ((0, 0), (0, SM_W - GDN_H)))
    dtb_p = jnp.pad(p["gdn_dt_bias"], ((0, 0), (0, SM_W - GDN_H)))
    m64 = jnp.concatenate([jnp.zeros((PAD, D), f32), meta], axis=0)
    gate_b, gdn_norm_w, gla_norm_w = p["gla_gate_b"], p["gdn_norm_w"], p["gla_norm_w"]
    half = shard["w_up"].shape[0] // 2

    h0, n1, (w_in4, conv4, w24) = _embed_norm(
        x, m64, p["attn_norm_w"], side=_Gather([shard["w_in"], shard["gdn_conv_w"], shard["gla_gate_w2"]]))
    w_in, conv_w, w2 = _cat_cols(w_in4), _cat_cols(conv4), _cat_cols(w24)
    wp = jnp.concatenate([w_in[:, R_Z:R_AB], w_in[:, R_GR:R_LR], w_in[:, R_G:R_GR], w_in[:, R_QKV:R_Z]], axis=1)
    wsm = jnp.concatenate([w_in[:, R_AB:R_G], w_in[:, R_LR:R_END],
                           jnp.zeros((D, SM_W - SM_LR - GATE_RANK), w_in.dtype)], axis=1)
    w2p = jnp.pad(w2, ((SM_LR, SM_W - SM_LR - GATE_RANK), (0, 0)))
    proj, (w_out4, w_up4a) = _mm(n1, wp, "nn", "proj", side=_Gather([shard["w_out"], shard["w_up"][:half]]))
    w_out = w_out4.reshape(-1, D)
    psm = _mm(n1, wsm, "nn", "proj_small")
    gb, la = _gates(psm, w2p, gate_b, alog_p, dtb_p)
    qkvc, (w_up4b,) = _conv(proj, conv_w, side=_Gather([shard["w_up"][half:]]))
    w_up = jnp.concatenate([_cat_cols(w_up4a), _cat_cols(w_up4b)], axis=0)
    og, sall, pall, (w_gate4,) = _gdn_fwd(qkvc, gb, side=_Gather([shard["w_gate"]]))
    w_gate = _cat_cols(w_gate4)
    ol, stall = _gla_fwd(proj, la)
    mixed = _gated_norm(og, ol, proj, gdn_norm_w, gla_norm_w)
    attn = _mm(mixed, w_out, "nn", "out_proj")
    h1, n2 = _add_norm(h0, attn, p["ffn_norm_w"])
    act, gate, up, (w_down4,) = _ffn_in(n2, w_gate, w_up, side=_Gather([shard["w_down"]]))
    w_down = w_down4.reshape(-1, D)
    ffn = _mm(act, w_down, "nn", "ffn_down", tk_cap=1408)
    dh2, dh2b, lossp, g_final = _final(h1, ffn, loss_target, p["final_norm_w"])

    g_down = _mm(act, dh2b, "tn", "g_w_down", tk_cap=1408, out_dtype=bf16)
    dg, du = _ffn_dact(dh2b, w_down, gate, up)
    g_gate = _mm(n2, dg, "tn", "g_w_gate", tn_cap=1408, tk_cap=1408, out_dtype=bf16, col_slabs=True)
    g_up = _mm(n2, du, "tn", "g_w_up", tn_cap=1408, tk_cap=1408, out_dtype=bf16, col_slabs=True)
    dn2 = _mm(dg, w_gate, "nt", "d_n2_gate", tk_cap=1408)
    dn2 = _mm(du, w_up, "nt", "d_n2_up", tk_cap=1408, acc_in=dn2)
    dh1, dh1b, g_ffn_norm = _norm_bwd(dn2, h1, dh2, p["ffn_norm_w"])
    dmix = _mm(dh1b, w_out, "nt", "d_mixed")
    g_out = _mm(mixed, dh1b, "tn", "g_w_out", tk_cap=1408, out_dtype=bf16)
    dog, dol, dproj, g_gdn_norm, g_gla_norm = _gated_norm_bwd(og, ol, proj, gdn_norm_w, gla_norm_w, dmix)
    dproj, dla, (r_down, r_gate) = _gla_bwd(proj, la, stall, dol, dproj,
                                            side=_Exchange([_row_slabs(g_down), g_gate]))
    dqkvc, dgb, (r_up, r_out) = _gdn_bwd(qkvc, gb, sall, pall, dog, side=_Exchange([g_up, _row_slabs(g_out)]))
    dproj, g_conv, (h_out, h_down) = _conv_bwd(
        proj, conv_w, dqkvc, dproj, side=_Siblings([_sum_slots(r_out, "sum_w_out"), _sum_slots(r_down, "sum_w_down")]))
    dpsm, g_w2p, g_gate_b, g_alog, g_dtb = _gates_bwd(psm, w2p, gate_b, alog_p, dtb_p, dgb, dla)
    g_wp, (h_gate, h_up) = _mm(
        n1, dproj, "tn", "g_w_in", tm_cap=2048, tk_cap=1408, out_dtype=bf16,
        side=_Siblings([_sum_slots(r_gate, "sum_w_gate"), _sum_slots(r_up, "sum_w_up")]))
    g_wsm = _mm(n1, dpsm, "tn", "g_w_in_small", tm_cap=2048, tk_cap=1408, out_dtype=bf16)
    dn1, r_in = _mm(dproj, wp, "nt", "d_n1", tk_cap=1024, side=_Exchange([_row_slabs(g_wp), _row_slabs(g_wsm)]))
    dn1 = _mm(dpsm, wsm, "nt", "d_n1_small", acc_in=dn1)
    grad_x, g_meta, g_attn_norm = _embed_norm_bwd(dn1, h0, dh1, p["attn_norm_w"], S)

    received = dict(w_in=tuple(r_in), w_gate=h_gate, w_up=h_up, w_out=h_out, w_down=h_down)
    small = dict(
        meta_tokens=g_meta, attn_norm_w=g_attn_norm, gdn_conv_w=g_conv, gdn_a_log=g_alog[:, :GDN_H],
        gdn_dt_bias=g_dtb[:, :GDN_H], gdn_norm_w=g_gdn_norm, gla_gate_w2=g_w2p[SM_LR:SM_LR + GATE_RANK],
        gla_gate_b=g_gate_b, gla_norm_w=g_gla_norm, ffn_norm_w=g_ffn_norm, final_norm_w=g_final)
    return lossp[0, 0], grad_x, received, small


_WEIGHTS = ("meta_tokens", "attn_norm_w", "w_in", "gdn_conv_w", "gdn_a_log", "gdn_dt_bias", "gdn_norm_w",
            "gla_gate_w2", "gla_gate_b", "gla_norm_w", "w_out", "ffn_norm_w", "w_gate", "w_up", "w_down",
            "final_norm_w")
_BIG_COLS = ("w_in", "w_gate", "w_up")
_BIG_ROWS = ("w_out", "w_down")
_SMALL_SHARDED = ("meta_tokens", "gdn_conv_w", "gla_gate_w2")


def kernel(x, meta_tokens, attn_norm_w, w_in, gdn_conv_w, gdn_a_log, gdn_dt_bias, gdn_norm_w, gla_gate_w2, gla_gate_b, gla_norm_w, w_out, ffn_norm_w, w_gate, w_up, w_down, final_norm_w, loss_target, m_meta_tokens, m_attn_norm_w, m_w_in, m_gdn_conv_w, m_gdn_a_log, m_gdn_dt_bias, m_gdn_norm_w, m_gla_gate_w2, m_gla_gate_b, m_gla_norm_w, m_w_out, m_ffn_norm_w, m_w_gate, m_w_up, m_w_down, m_final_norm_w, v_meta_tokens, v_attn_norm_w, v_w_in, v_gdn_conv_w, v_gdn_a_log, v_gdn_dt_bias, v_gdn_norm_w, v_gla_gate_w2, v_gla_gate_b, v_gla_norm_w, v_w_out, v_ffn_norm_w, v_w_gate, v_w_up, v_w_down, v_final_norm_w):
    w = dict(meta_tokens=meta_tokens, attn_norm_w=attn_norm_w, w_in=w_in, gdn_conv_w=gdn_conv_w, gdn_a_log=gdn_a_log,
             gdn_dt_bias=gdn_dt_bias, gdn_norm_w=gdn_norm_w, gla_gate_w2=gla_gate_w2, gla_gate_b=gla_gate_b,
             gla_norm_w=gla_norm_w, w_out=w_out, ffn_norm_w=ffn_norm_w, w_gate=w_gate, w_up=w_up, w_down=w_down,
             final_norm_w=final_norm_w)
    m = dict(meta_tokens=m_meta_tokens, attn_norm_w=m_attn_norm_w, w_in=m_w_in, gdn_conv_w=m_gdn_conv_w,
             gdn_a_log=m_gdn_a_log, gdn_dt_bias=m_gdn_dt_bias, gdn_norm_w=m_gdn_norm_w, gla_gate_w2=m_gla_gate_w2,
             gla_gate_b=m_gla_gate_b, gla_norm_w=m_gla_norm_w, w_out=m_w_out, ffn_norm_w=m_ffn_norm_w,
             w_gate=m_w_gate, w_up=m_w_up, w_down=m_w_down, final_norm_w=m_final_norm_w)
    v = dict(meta_tokens=v_meta_tokens, attn_norm_w=v_attn_norm_w, w_in=v_w_in, gdn_conv_w=v_gdn_conv_w,
             gdn_a_log=v_gdn_a_log, gdn_dt_bias=v_gdn_dt_bias, gdn_norm_w=v_gdn_norm_w, gla_gate_w2=v_gla_gate_w2,
             gla_gate_b=v_gla_gate_b, gla_norm_w=v_gla_norm_w, w_out=v_w_out, ffn_norm_w=v_ffn_norm_w,
             w_gate=v_w_gate, w_up=v_w_up, w_down=v_w_down, final_norm_w=v_final_norm_w)
    chip = 2 * lax.axis_index("x") + lax.axis_index("y")

    def two_d(a):
        return a.reshape(1, -1) if a.ndim == 1 else a.reshape(-1, a.shape[-1])

    w2d = {k: two_d(a) for k, a in w.items()}
    big = _BIG_COLS + _BIG_ROWS
    small = tuple(k for k in _WEIGHTS if k not in big)

    (meta4,), = _comm_now("gather_meta", [_Gather([w2d["meta_tokens"]])])
    shard = {k: w2d[k].astype(bf16) for k in big}
    shard.update({k: w2d[k] for k in ("gdn_conv_w", "gla_gate_w2")})
    lossp, grad_x, received, g = _step(x, loss_target, {k: w2d[k] for k in small}, _cat_cols(meta4), shard)
    loss = lax.psum(lossp, ("x", "y", "c"))

    sizes = [g[k].size for k in small]
    total = sum(sizes)
    rows = -(-total // 1024)
    rows += (-rows) % 8
    packed = jnp.concatenate([g[k].reshape(-1) for k in small] + [jnp.zeros((rows * 1024 - total,), f32)])
    r_wp, r_wsm = received.pop("w_in")
    s_in = _w_in_columns(_sum_slots(r_wp, "sum_w_in"), _sum_slots(r_wsm, "sum_w_in_small"))
    in_by_chip = s_in.reshape(s_in.shape[0], N_CHIP, -1).transpose(1, 0, 2)
    (in8, packed8), = _comm_now("exchange_tail", [_Exchange([], [packed.reshape(rows, 1024)], by_chip=[in_by_chip])])
    red = {k: h.reshape(w2d[k].shape) for k, h in received.items()}
    red["w_in"] = in8.reshape(w2d["w_in"].shape)
    psum_small = _sum_slots(packed8, "sum_small").reshape(-1)
    off = 0
    for k, n in zip(small, sizes):
        a = psum_small[off:off + n].reshape(g[k].shape)
        off += n
        if k in _SMALL_SHARDED:
            c = w2d[k].shape[1]
            a = lax.dynamic_slice_in_dim(a, chip * c, c, axis=1)
        red[k] = a

    grads, deltas, new_m, new_v = [], [], [], []
    for k in _WEIGHTS:
        d, m2, v2 = _adamw(red[k], w2d[k], two_d(m[k]), two_d(v[k]), "adamw_" + k)
        shape = w[k].shape
        grads.append(red[k].reshape(shape))
        deltas.append(d.reshape(shape))
        new_m.append(m2.reshape(shape))
        new_v.append(v2.reshape(shape))
    return (loss, grad_x, *grads, *deltas, *new_m, *new_v)
```

```python
import functools

import jax
import jax.numpy as jnp
from jax import lax
from jax.experimental import pallas as pl
from jax.experimental.pallas import tpu as pltpu

f32 = jnp.float32
bf16 = jnp.bfloat16
HI = lax.Precision.HIGHEST
HIGH = lax.Precision.HIGH
MESH = pl.DeviceIdType.MESH

N_META = 16
CONV_K = 4
GDN_H, GDN_DK, GDN_DV, GDN_C = 8, 128, 128, 64
GLA_H, GLA_DK, GLA_DV, GLA_C = 4, 128, 256, 16
GATE_RANK = 16
GATE_NORMALIZER = 16.0
EPS = 1e-6
GDN_QK = GDN_H * GDN_DK
GDN_V = GDN_H * GDN_DV
GLA_QK = GLA_H * GLA_DK
GLA_V = GLA_H * GLA_DV
PAD = (-N_META) % GDN_C
OFF = PAD + N_META
ROWS = 64

R_QKV, R_Z, R_AB, R_G, R_GR, R_LR, R_END = 0, 3072, 4096, 4112, 6160, 7184, 7200
C_ZR, C_G, C_QKV, C_END = 0, 2048, 4096, 7168
ZR_W = GDN_V + GLA_V
G_W = 2 * GLA_QK + GLA_V
QKV_W = 2 * GDN_QK + GDN_V
Q0, K0, V0 = 0, GDN_QK, 2 * GDN_QK
SM_W = 128
SM_LR = 2 * GDN_H

ADAM_LR, ADAM_B1, ADAM_B2, ADAM_EPS, ADAM_WD, ADAM_STEP = 0.001, 0.9, 0.999, 1e-08, 0.01, 10

VMEM_LIMIT_V7X = 56 * 1024 * 1024
N_DEV = 8
N_CHIP = 4


def _params(*sem):
    return pltpu.CompilerParams(dimension_semantics=sem, vmem_limit_bytes=VMEM_LIMIT_V7X)


def _tile(n, cap, mult=16):
    best = None
    for d in range(mult, min(n, cap) + 1, mult):
        if n % d == 0:
            best = d
    assert best is not None, (n, cap, mult)
    return best


NN = ((1,), (0,))
NT = ((1,), (1,))
TN = ((0,), (0,))


def _dot(a, b, dims, prec=None):
    return lax.dot_general(a, b, (dims, ((), ())), precision=prec, preferred_element_type=f32)


def _mmb(a, b, dims):
    return _dot(a.astype(bf16), b.astype(bf16), dims)


def _sigmoid(x):
    return jax.nn.sigmoid(x)


def _silu(x):
    return x * _sigmoid(x)


def _dsilu(x):
    s = _sigmoid(x)
    return s * (1.0 + x * (1.0 - s))


def _log1p_exp_neg_abs(x):
    t = jnp.exp(-jnp.abs(x))
    u = 1.0 + t
    d = u - 1.0
    return jnp.where(d == 0.0, t, jnp.log(u) * (t / jnp.where(d == 0.0, 1.0, d)))


def _softplus(x):
    return jnp.maximum(x, 0.0) + _log1p_exp_neg_abs(x)


def _log_sigmoid(x):
    return jnp.minimum(x, 0.0) - _log1p_exp_neg_abs(x)


def _rms(x):
    r = lax.rsqrt(jnp.mean(x * x, axis=-1, keepdims=True) + EPS)
    return x * r, r


def _rms_bwd(dy, xh, r, w):
    t = dy * w
    return r * (t - xh * jnp.mean(t * xh, axis=-1, keepdims=True))


def _l2n(x):
    return x * lax.rsqrt(jnp.sum(x * x, axis=-1, keepdims=True) + EPS)


INV_LEAF = 8


def _same_block(C, b):
    sh = b.bit_length() - 1
    row = lax.broadcasted_iota(jnp.int32, (C, C), 0)
    col = lax.broadcasted_iota(jnp.int32, (C, C), 1)
    return lax.shift_right_logical(row, sh) == lax.shift_right_logical(col, sh)


def _tri_inv_impl(As):
    C = As[0].shape[0]
    R = range(len(As))
    row = lax.broadcasted_iota(jnp.int32, (C, C), 0)
    col = lax.broadcasted_iota(jnp.int32, (C, C), 1)
    eye = (row == col).astype(f32)
    b = INV_LEAF
    inner = _same_block(C, b)
    leaf = [jnp.where(inner, As[h], 0.0) for h in R]
    d = [eye - leaf[h] for h in R]
    pw = leaf
    n = 2
    while n < b:
        pw = [_dot(pw[h], pw[h], NN, HIGH) for h in R]
        d = [_dot(d[h], eye + pw[h], NN, HIGH) for h in R]
        n *= 2
    while b < C:
        outer = _same_block(C, 2 * b)
        level = jnp.logical_and(outer, jnp.logical_not(inner))
        ed = [_dot(jnp.where(level, As[h], 0.0), d[h], NN, HIGH) for h in R]
        d = [d[h] - _dot(d[h], ed[h], NN, HIGH) for h in R]
        inner = outer
        b *= 2
    return d


@jax.custom_vjp
def _tri_inv(As):
    return _tri_inv_impl(As)


def _tri_inv_fwd(As):
    d = _tri_inv_impl(As)
    return d, d


def _tri_inv_bwd(d, g):
    R = range(len(d))
    t = [_dot(d[h], g[h], TN, HIGH) for h in R]
    return ([-_dot(t[h], d[h], NT, HIGH) for h in R],)


_tri_inv.defvjp(_tri_inv_fwd, _tri_inv_bwd)


@jax.custom_vjp
def _tri_inv_known(As, Ps):
    del As
    return Ps


def _tri_inv_known_fwd(As, Ps):
    del As
    return Ps, Ps


def _tri_inv_known_bwd(d, g):
    return _tri_inv_bwd(d, g)[0], [jnp.zeros_like(x) for x in d]


_tri_inv_known.defvjp(_tri_inv_known_fwd, _tri_inv_known_bwd)


def _gdn_chunk(Ss, qrs, krs, vs, betas, gs, Ps=None):
    C, dk = qrs[0].shape
    R = range(len(Ss))
    row = lax.broadcasted_iota(jnp.int32, (C, C), 0)
    col = lax.broadcasted_iota(jnp.int32, (C, C), 1)
    causal = row >= col
    strict = row > col
    cf = causal.astype(f32)
    q = [_l2n(qrs[h]) * (dk ** -0.5) for h in R]
    k = [_l2n(krs[h]) for h in R]
    mc = [_dot(cf, jnp.broadcast_to(gs[h], (C, C)), NN, HI) for h in R]
    gc = [mc[h][:, 0:1] for h in R]
    decay = [jnp.where(causal, jnp.exp(jnp.where(causal, mc[h] - mc[h].T, 0.0)), 0.0) for h in R]
    kb = [k[h] * betas[h] for h in R]
    a = [jnp.where(strict, _mmb(kb[h], k[h], NT) * decay[h], 0.0) for h in R]
    p = _tri_inv(a) if Ps is None else _tri_inv_known(a, Ps)
    egc = [jnp.exp(gc[h]) for h in R]
    u = [_mmb(p[h], vs[h] * betas[h], NN) for h in R]
    w = [_mmb(p[h], kb[h] * egc[h], NN) for h in R]
    qk = [jnp.where(causal, _mmb(q[h], k[h], NT) * decay[h], 0.0) for h in R]
    v_new = [u[h] - _mmb(w[h], Ss[h], NN) for h in R]
    o = [_mmb(q[h] * egc[h], Ss[h], NN) + _mmb(qk[h], v_new[h], NN) for h in R]
    gl = [gc[h][C - 1:C, :] for h in R]
    kd = [k[h] * jnp.exp(gl[h] - gc[h]) for h in R]
    S2 = [Ss[h] * jnp.exp(gl[h]) + _mmb(kd[h], v_new[h], TN) for h in R]
    return S2, o, p


def _rows_exact_impl(m01, x, dims):
    m = m01.astype(bf16)
    x1 = x.astype(bf16)
    r1 = x - x1.astype(f32)
    x2 = r1.astype(bf16)
    x3 = (r1 - x2.astype(f32)).astype(bf16)
    d = lambda y: _dot(m, y, dims)
    return d(x1) + (d(x2) + d(x3))


@jax.custom_vjp
def _rows_exact(m01, x):
    return _rows_exact_impl(m01, x, NN)


def _rows_exact_fwd(m01, x):
    return _rows_exact_impl(m01, x, NN), m01


def _rows_exact_bwd(m01, g):
    return jnp.zeros_like(m01), _rows_exact_impl(m01, g, TN)


_rows_exact.defvjp(_rows_exact_fwd, _rows_exact_bwd)


def _gla_blocks(Sts, qrs, ks, vs, las):
    H = len(Sts)
    n = len(qrs)
    C, dk = qrs[0].shape
    R = range(n)
    row = lax.broadcasted_iota(jnp.int32, (C, C), 0)
    col = lax.broadcasted_iota(jnp.int32, (C, C), 1)
    ri = lax.broadcasted_iota(jnp.int32, (C, dk), 0)
    q = [qrs[h] * (dk ** -0.5) for h in R]
    running = (row >= col).astype(f32)
    b = [_rows_exact(running, las[h]) for h in R]
    sc = [jnp.where(row == col, jnp.sum(q[h] * ks[h], axis=-1, keepdims=True), 0.0) for h in R]
    s = C // 2
    while s >= 1:
        sh = s.bit_length() - 1
        ref = lax.shift_left(lax.shift_right_logical(row, sh + 1), sh + 1) + (s - 1)
        pick = (col == ref).astype(f32)
        bref = [_rows_exact(pick, b[h]) for h in R]
        upper = (lax.shift_right_logical(ri, sh) & 1) == 1
        qt = [jnp.where(upper, q[h] * jnp.exp(jnp.where(upper, b[h] - bref[h], 0.0)), 0.0) for h in R]
        kt = [jnp.where(upper, 0.0, ks[h] * jnp.exp(jnp.where(upper, 0.0, bref[h] - b[h]))) for h in R]
        same = lax.shift_right_logical(row, sh + 1) == lax.shift_right_logical(col, sh + 1)
        sc = [sc[h] + jnp.where(same, _mmb(qt[h], kt[h], NT), 0.0) for h in R]
        s //= 2
    o = [_mmb(sc[h], vs[h], NN) for h in R]
    qe = [q[h] * jnp.exp(b[h]) for h in R]
    bl = [b[h][C - 1:C, :] for h in R]
    upd = [_mmb(vs[h], ks[h] * jnp.exp(bl[h] - b[h]), TN) for h in R]
    ebl = [jnp.exp(bl[h]) for h in R]
    St = list(Sts)
    for blk in range(n // H):
        for h in range(H):
            i = blk * H + h
            o[i] = o[i] + _mmb(qe[i], St[h], NT)
        St = [St[h] * ebl[blk * H + h] + upd[blk * H + h] for h in range(H)]
    return St, o


_ANY = pl.BlockSpec(memory_space=pl.ANY)


class _Gather:
    def __init__(self, arrays):
        self.arrays = list(arrays)
        self.n = len(self.arrays)
        self.out_shape = [jax.ShapeDtypeStruct((N_CHIP,) + a.shape, a.dtype) for a in self.arrays]
        self.sems = [pltpu.SemaphoreType.DMA((self.n, 3)), pltpu.SemaphoreType.DMA((self.n, 3)),
                     pltpu.SemaphoreType.DMA((self.n,))]

    def hooks(self, ins, outs, send, recv, lsem):
        def copies():
            x, y, c = lax.axis_index("x"), lax.axis_index("y"), lax.axis_index("c")
            me = 2 * x + y
            out = []
            for a in range(self.n):
                out.append((pltpu.make_async_copy(ins[a], outs[a].at[me], lsem.at[a]), None))
                for j, (px, py) in enumerate([(1 - x, y), (x, 1 - y), (1 - x, 1 - y)]):
                    mk = lambda dst, a=a, j=j, px=px, py=py: pltpu.make_async_remote_copy(
                        src_ref=ins[a], dst_ref=dst, send_sem=send.at[a, j], recv_sem=recv.at[a, j],
                        device_id=(px, py, c), device_id_type=MESH)
                    out.append((mk(outs[a].at[me]), mk(outs[a].at[2 * px + py])))
            return out

        return _start_wait(copies)


class _Exchange:
    def __init__(self, slotted, shared=(), by_chip=()):
        self.arrays = list(slotted) + list(by_chip) + list(shared)
        self.ns, self.nc = len(slotted), len(by_chip)
        self.n = len(self.arrays)
        self.out_shape = [jax.ShapeDtypeStruct(a.shape, a.dtype) for a in slotted]
        self.out_shape += [jax.ShapeDtypeStruct((N_DEV,) + a.shape[1:], a.dtype) for a in by_chip]
        self.out_shape += [jax.ShapeDtypeStruct((N_DEV,) + b.shape, b.dtype) for b in shared]
        self.sems = [pltpu.SemaphoreType.DMA((self.n, N_DEV - 1)), pltpu.SemaphoreType.DMA((self.n, N_DEV - 1)),
                     pltpu.SemaphoreType.DMA((self.n,))]

    def hooks(self, ins, outs, send, recv, lsem):
        def copies():
            x, y, c = lax.axis_index("x"), lax.axis_index("y"), lax.axis_index("c")
            me = 4 * x + 2 * y + c

            def src(a, dev):
                tx, ty, tc = dev
                if a < self.ns:
                    return ins[a].at[4 * tx + 2 * ty + tc]
                return ins[a].at[2 * tx + ty] if a < self.ns + self.nc else ins[a]

            out = []
            for a in range(self.n):
                out.append((pltpu.make_async_copy(src(a, (x, y, c)), outs[a].at[me], lsem.at[a]), None))
                for o in range(1, N_DEV):
                    dev = (1 - x if o & 4 else x, 1 - y if o & 2 else y, 1 - c if o & 1 else c)
                    t = 4 * dev[0] + 2 * dev[1] + dev[2]
                    mk = lambda dst, a=a, o=o, dev=dev: pltpu.make_async_remote_copy(
                        src_ref=src(a, dev), dst_ref=dst, send_sem=send.at[a, o - 1], recv_sem=recv.at[a, o - 1],
                        device_id=dev, device_id_type=MESH)
                    out.append((mk(outs[a].at[me]), mk(outs[a].at[t])))
            return out

        return _start_wait(copies)


def _start_wait(copies):
    def start():
        for s, _ in copies():
            s.start()

    def wait():
        for s, w in copies():
            (s if w is None else w).wait()

    return start, wait


def _call(body, *, name, grid, in_specs, out_specs, out_shape, args, sem, scratch_shapes=(), aliases=None, side=None):
    in_specs, out_specs, out_shape, args = list(in_specs), list(out_specs), list(out_shape), list(args)
    scratch_shapes = list(scratch_shapes)
    aliases = aliases or {}
    if side is None:
        return pl.pallas_call(
            body, name=name, grid=grid, in_specs=in_specs, out_specs=out_specs, out_shape=out_shape,
            scratch_shapes=scratch_shapes, input_output_aliases=aliases, compiler_params=_params(*sem))(*args)
    n_in, n_out, n_scr, ns = len(in_specs), len(out_specs), len(scratch_shapes), side.n

    def full_body(*refs):
        ins, refs = refs[:n_in], refs[n_in:]
        s_in, refs = refs[:ns], refs[ns:]
        outs, refs = refs[:n_out], refs[n_out:]
        s_out, refs = refs[:ns], refs[ns:]
        scr, sems = refs[:n_scr], refs[n_scr:]
        start, wait = side.hooks(s_in, s_out, *sems)
        ids = [pl.program_id(d) for d in range(len(grid))]
        first = functools.reduce(jnp.logical_and, [i == 0 for i in ids])
        last = functools.reduce(jnp.logical_and, [i == g - 1 for i, g in zip(ids, grid)])
        pl.when(first)(start)
        body(*ins, *outs, *scr)
        pl.when(last)(wait)

    return pl.pallas_call(
        full_body, name=name, grid=grid, in_specs=in_specs + [_ANY] * ns, out_specs=out_specs + [_ANY] * ns,
        out_shape=out_shape + side.out_shape, scratch_shapes=scratch_shapes + side.sems,
        input_output_aliases=aliases, compiler_params=_params(*(["arbitrary"] * len(grid))))(*args, *side.arrays)


def _mm(a, b, mode, name, *, tm_cap=1408, tn_cap=1024, tk_cap=2048, out_dtype=f32, acc_in=None, side=None,
        col_slabs=False):
    if mode == "nn":
        (M, K), (K2, N) = a.shape, b.shape
    elif mode == "nt":
        (M, K), (N, K2) = a.shape, b.shape
    else:
        (K, M), (K2, N) = a.shape, b.shape
    assert K == K2, (name, a.shape, b.shape)
    tm = _tile(M // 2 if col_slabs else M, tm_cap)
    tn = _tile(N // N_CHIP if col_slabs else N, tn_cap, 128)
    tk = _tile(K, tk_cap, 128 if K % 128 == 0 else 16)
    nk = K // tk
    dims = {"nn": NN, "nt": NT, "tn": TN}[mode]
    use_scratch = nk > 1 and out_dtype != f32

    def body(*refs):
        if acc_in is not None:
            a_ref, b_ref, c_ref, o_ref, *scr = refs
        else:
            a_ref, b_ref, o_ref, *scr = refs
            c_ref = None
        p = _mmb(a_ref[...], b_ref[...], dims)
        if nk == 1:
            if c_ref is not None:
                p = p + c_ref[...]
            o_ref[...] = p.astype(out_dtype)
            return
        k = pl.program_id(2)
        acc = scr[0] if use_scratch else o_ref

        @pl.when(k == 0)
        def _():
            acc[...] = p if c_ref is None else p + c_ref[...]

        @pl.when(k > 0)
        def _():
            acc[...] += p

        if use_scratch:
            @pl.when(k == nk - 1)
            def _():
                o_ref[...] = acc[...].astype(out_dtype)

    if mode == "tn":
        a_spec = pl.BlockSpec((tk, tm), lambda i, j, k: (k, i))
    else:
        a_spec = pl.BlockSpec((tm, tk), lambda i, j, k: (i, k))
    if mode == "nt":
        b_spec = pl.BlockSpec((tn, tk), lambda i, j, k: (j, k))
    else:
        b_spec = pl.BlockSpec((tk, tn), lambda i, j, k: (k, j))
    if col_slabs:
        assert acc_in is None
        ni, nj = M // 2 // tm, N // N_CHIP // tn
        o_spec = pl.BlockSpec((None, tm, tn), lambda i, j, k: (2 * (j // nj) + i // ni, i % ni, j % nj))
        o_shape = jax.ShapeDtypeStruct((N_DEV, M // 2, N // N_CHIP), out_dtype)
    else:
        o_spec = pl.BlockSpec((tm, tn), lambda i, j, k: (i, j))
        o_shape = jax.ShapeDtypeStruct((M, N), out_dtype)
    in_specs = [a_spec, b_spec]
    args = [a, b]
    if acc_in is not None:
        in_specs.append(o_spec)
        args.append(acc_in)
    out = _call(body, name=name, grid=(M // tm, N // tn, nk), in_specs=in_specs, out_specs=[o_spec],
                out_shape=[o_shape], args=args,
                scratch_shapes=[pltpu.VMEM((tm, tn), f32)] if use_scratch else [],
                sem=("parallel", "parallel", "arbitrary"), side=side)
    return out[0] if side is None else (out[0], out[1:])


def _embed_norm(x3, m64, w, side=None):
    _, S, D = x3.shape
    Lp = OFF + S

    def body(x_ref, m_ref, w_ref, h_ref, n_ref):
        i = pl.program_id(0)
        h = jnp.where(i == 0, m_ref[...], x_ref[...])
        h_ref[...] = h
        xh, _ = _rms(h)
        n_ref[...] = (xh * w_ref[...]).astype(bf16)

    row = pl.BlockSpec((ROWS, D), lambda i: (i, 0))
    out = _call(
        body, name="embed_norm", grid=(Lp // ROWS,),
        in_specs=[pl.BlockSpec((None, ROWS, D), lambda i: (0, jnp.maximum(i - 1, 0), 0)),
                  pl.BlockSpec((ROWS, D), lambda i: (0, 0)),
                  pl.BlockSpec((1, D), lambda i: (0, 0))],
        out_specs=[row, row],
        out_shape=[jax.ShapeDtypeStruct((Lp, D), f32), jax.ShapeDtypeStruct((Lp, D), bf16)],
        args=[x3, m64, w], sem=("parallel",), side=side)
    return out[0], out[1], out[2:]


def _add_norm(h, d, w):
    Lp, D = h.shape
    tr = _tile(Lp, 256)

    def body(h_ref, d_ref, w_ref, o_ref, n_ref):
        h1 = h_ref[...] + d_ref[...]
        o_ref[...] = h1
        xh, _ = _rms(h1)
        n_ref[...] = (xh * w_ref[...]).astype(bf16)

    row = pl.BlockSpec((tr, D), lambda i: (i, 0))
    return pl.pallas_call(
        body, name="add_norm", grid=(Lp // tr,),
        in_specs=[row, row, pl.BlockSpec((1, D), lambda i: (0, 0))], out_specs=[row, row],
        out_shape=[jax.ShapeDtypeStruct((Lp, D), f32), jax.ShapeDtypeStruct((Lp, D), bf16)],
        compiler_params=_params("parallel"),
    )(h, d, w)


def _norm_bwd(dn, h, dh, w):
    Lp, D = h.shape
    tr = _tile(Lp, 256)

    def body(dn_ref, h_ref, dh_ref, w_ref, o_ref, ob_ref, gw_ref):
        i = pl.program_id(0)
        xh, r = _rms(h_ref[...])
        dn_ = dn_ref[...]
        o = dh_ref[...] + _rms_bwd(dn_, xh, r, w_ref[...])
        o_ref[...] = o
        ob_ref[...] = o.astype(bf16)
        gw = jnp.sum(dn_ * xh, axis=0, keepdims=True)

        @pl.when(i == 0)
        def _():
            gw_ref[...] = gw

        @pl.when(i > 0)
        def _():
            gw_ref[...] += gw

    row = pl.BlockSpec((tr, D), lambda i: (i, 0))
    vec = pl.BlockSpec((1, D), lambda i: (0, 0))
    return pl.pallas_call(
        body, name="norm_bwd", grid=(Lp // tr,), in_specs=[row, row, row, vec], out_specs=[row, row, vec],
        out_shape=[jax.ShapeDtypeStruct((Lp, D), f32), jax.ShapeDtypeStruct((Lp, D), bf16),
                   jax.ShapeDtypeStruct((1, D), f32)],
        compiler_params=_params("arbitrary"),
    )(dn, h, dh, w)


def _embed_norm_bwd(dn, h, dh, w, S):
    Lp, D = h.shape

    def body(dn_ref, h_ref, dh_ref, w_ref, gx_ref, gm_ref, gw_ref):
        i = pl.program_id(0)
        xh, r = _rms(h_ref[...])
        dn_ = dn_ref[...]
        d0 = dh_ref[...] + _rms_bwd(dn_, xh, r, w_ref[...])
        gx_ref[...] = d0
        gw = jnp.sum(dn_ * xh, axis=0, keepdims=True)

        @pl.when(i == 0)
        def _():
            gm_ref[...] = d0[PAD:OFF, :]
            gw_ref[...] = gw

        @pl.when(i > 0)
        def _():
            gw_ref[...] += gw

    row = pl.BlockSpec((ROWS, D), lambda i: (i, 0))
    vec = pl.BlockSpec((1, D), lambda i: (0, 0))
    return pl.pallas_call(
        body, name="embed_norm_bwd", grid=(Lp // ROWS,), in_specs=[row, row, row, vec],
        out_specs=[pl.BlockSpec((None, ROWS, D), lambda i: (0, jnp.maximum(i - 1, 0), 0)),
                   pl.BlockSpec((N_META, D), lambda i: (0, 0)), vec],
        out_shape=[jax.ShapeDtypeStruct((1, S, D), f32), jax.ShapeDtypeStruct((N_META, D), f32),
                   jax.ShapeDtypeStruct((1, D), f32)],
        compiler_params=_params("arbitrary"),
    )(dn, h, dh, w)


def _final(h1, ffn, tgt3, w):
    Lp, D = h1.shape

    def body(h_ref, f_ref, t_ref, w_ref, d_ref, db_ref, l_ref, gw_ref):
        i = pl.program_id(0)
        h2 = h_ref[...] + f_ref[...]
        xh, r = _rms(h2)
        w_ = w_ref[...]
        e = xh * w_ - t_ref[...]
        valid = (i > 0).astype(f32)
        loss = 0.5 * jnp.sum(jnp.mean(e * e, axis=-1, keepdims=True), axis=0, keepdims=True) * valid
        dy = e * (valid / D)
        d = _rms_bwd(dy, xh, r, w_)
        d_ref[...] = d
        db_ref[...] = d.astype(bf16)
        gw = jnp.sum(dy * xh, axis=0, keepdims=True)

        @pl.when(i == 0)
        def _():
            l_ref[...] = jnp.zeros_like(l_ref)
            gw_ref[...] = jnp.zeros_like(gw_ref)

        l_ref[...] += jnp.broadcast_to(loss, l_ref.shape)
        gw_ref[...] += gw

    row = pl.BlockSpec((ROWS, D), lambda i: (i, 0))
    vec = pl.BlockSpec((1, D), lambda i: (0, 0))
    return pl.pallas_call(
        body, name="final_loss", grid=(Lp // ROWS,),
        in_specs=[row, row, pl.BlockSpec((None, ROWS, D), lambda i: (0, jnp.maximum(i - 1, 0), 0)), vec],
        out_specs=[row, row, pl.BlockSpec((8, 128), lambda i: (0, 0)), vec],
        out_shape=[jax.ShapeDtypeStruct((Lp, D), f32), jax.ShapeDtypeStruct((Lp, D), bf16),
                   jax.ShapeDtypeStruct((8, 128), f32), jax.ShapeDtypeStruct((1, D), f32)],
        compiler_params=_params("arbitrary"),
    )(h1, ffn, tgt3, w)


def _ffn_in(n, w_gate, w_up, side=None):
    M, K = n.shape
    F = w_gate.shape[1]
    tm = _tile(M, 1408)
    tn = _tile(F, 512, 128)

    def body(a_ref, bg_ref, bu_ref, act_ref, g_ref, u_ref):
        a = a_ref[...]
        g = _mmb(a, bg_ref[...], NN)
        u = _mmb(a, bu_ref[...], NN)
        act_ref[...] = (_silu(g) * u).astype(bf16)
        g_ref[...] = g.astype(bf16)
        u_ref[...] = u.astype(bf16)

    wsp = pl.BlockSpec((K, tn), lambda i, j: (0, j))
    osp = pl.BlockSpec((tm, tn), lambda i, j: (i, j))
    out = _call(body, name="ffn_in", grid=(M // tm, F // tn),
                in_specs=[pl.BlockSpec((tm, K), lambda i, j: (i, 0)), wsp, wsp], out_specs=[osp] * 3,
                out_shape=[jax.ShapeDtypeStruct((M, F), bf16)] * 3, args=[n, w_gate, w_up],
                sem=("parallel", "parallel"), side=side)
    return out[0], out[1], out[2], out[3:]


def _ffn_dact(d, w_down, g, u):
    M, K = d.shape
    F = w_down.shape[0]
    tm = _tile(M, 1408)
    tn = _tile(F, 512, 128)

    def body(d_ref, w_ref, g_ref, u_ref, dg_ref, du_ref):
        da = _mmb(d_ref[...], w_ref[...], NT)
        g_ = g_ref[...].astype(f32)
        dg_ref[...] = (da * u_ref[...].astype(f32) * _dsilu(g_)).astype(bf16)
        du_ref[...] = (da * _silu(g_)).astype(bf16)

    osp = pl.BlockSpec((tm, tn), lambda i, j: (i, j))
    return pl.pallas_call(
        body, name="ffn_dact", grid=(M // tm, F // tn),
        in_specs=[pl.BlockSpec((tm, K), lambda i, j: (i, 0)), pl.BlockSpec((tn, K), lambda i, j: (j, 0)), osp, osp],
        out_specs=[osp, osp], out_shape=[jax.ShapeDtypeStruct((M, F), bf16)] * 2,
        compiler_params=_params("parallel", "parallel"),
    )(d, w_down, g, u)


def _gates(psm, w2p, gate_b, alog, dtb):
    Lp = psm.shape[0]
    tr = _tile(Lp, 256)

    def body(p_ref, w_ref, b_ref, a_ref, t_ref, gb_ref, la_ref):
        i = pl.program_id(0)
        psm_ = p_ref[...]
        lane = lax.broadcasted_iota(jnp.int32, psm_.shape, 1)
        rowi = lax.broadcasted_iota(jnp.int32, (tr, 1), 0) + i * tr
        g = -jnp.exp(a_ref[...]) * _softplus(psm_ + t_ref[...])
        beta = _sigmoid(psm_)
        gb = jnp.where(lane < GDN_H, g, jnp.where(lane < 2 * GDN_H, beta, 0.0))
        gb_ref[...] = gb * (rowi >= PAD).astype(f32)
        logit = _mmb(psm_, w_ref[...], NN) + b_ref[...]
        la_ref[...] = _log_sigmoid(logit) * (1.0 / GATE_NORMALIZER)

    row = pl.BlockSpec((tr, SM_W), lambda i: (i, 0))
    return pl.pallas_call(
        body, name="gates", grid=(Lp // tr,),
        in_specs=[row, pl.BlockSpec((SM_W, GLA_QK), lambda i: (0, 0)), pl.BlockSpec((1, GLA_QK), lambda i: (0, 0)),
                  pl.BlockSpec((1, SM_W), lambda i: (0, 0)), pl.BlockSpec((1, SM_W), lambda i: (0, 0))],
        out_specs=[row, pl.BlockSpec((tr, GLA_QK), lambda i: (i, 0))],
        out_shape=[jax.ShapeDtypeStruct((Lp, SM_W), f32), jax.ShapeDtypeStruct((Lp, GLA_QK), f32)],
        compiler_params=_params("parallel"),
    )(psm, w2p, gate_b, alog, dtb)


def _gates_bwd(psm, w2p, gate_b, alog, dtb, dgb, dla):
    Lp = psm.shape[0]
    tr = _tile(Lp, 256)

    def body(p_ref, w_ref, b_ref, a_ref, t_ref, dgb_ref, dla_ref, dp_ref, gw_ref, gb_ref, ga_ref, gt_ref):
        i = pl.program_id(0)
        psm_ = p_ref[...]
        lane = lax.broadcasted_iota(jnp.int32, psm_.shape, 1)
        rowi = lax.broadcasted_iota(jnp.int32, (tr, 1), 0) + i * tr
        d = dgb_ref[...] * (rowi >= PAD).astype(f32)
        ea = jnp.exp(a_ref[...])
        z = psm_ + t_ref[...]
        is_g = lane < GDN_H
        dz = jnp.where(is_g, -ea * _sigmoid(z) * d, 0.0)
        dalog = jnp.where(is_g, -ea * _softplus(z) * d, 0.0)
        beta = _sigmoid(psm_)
        dbeta = jnp.where(jnp.logical_and(lane >= GDN_H, lane < 2 * GDN_H), beta * (1.0 - beta) * d, 0.0)
        logit = _mmb(psm_, w_ref[...], NN) + b_ref[...]
        dlogit = dla_ref[...] * (_sigmoid(-logit) * (1.0 / GATE_NORMALIZER))
        dlr = _mmb(dlogit, w_ref[...], NT)
        dp_ref[...] = (dz + dbeta + dlr).astype(bf16)
        gw = _mmb(psm_, dlogit, TN)
        gb = jnp.sum(dlogit, axis=0, keepdims=True)
        ga = jnp.sum(dalog, axis=0, keepdims=True)
        gt = jnp.sum(dz, axis=0, keepdims=True)

        @pl.when(i == 0)
        def _():
            gw_ref[...] = gw
            gb_ref[...] = gb
            ga_ref[...] = ga
            gt_ref[...] = gt

        @pl.when(i > 0)
        def _():
            gw_ref[...] += gw
            gb_ref[...] += gb
            ga_ref[...] += ga
            gt_ref[...] += gt

    row = pl.BlockSpec((tr, SM_W), lambda i: (i, 0))
    wsp = pl.BlockSpec((SM_W, GLA_QK), lambda i: (0, 0))
    bsp = pl.BlockSpec((1, GLA_QK), lambda i: (0, 0))
    vsp = pl.BlockSpec((1, SM_W), lambda i: (0, 0))
    return pl.pallas_call(
        body, name="gates_bwd", grid=(Lp // tr,),
        in_specs=[row, wsp, bsp, vsp, vsp, row, pl.BlockSpec((tr, GLA_QK), lambda i: (i, 0))],
        out_specs=[row, wsp, bsp, vsp, vsp],
        out_shape=[jax.ShapeDtypeStruct((Lp, SM_W), bf16), jax.ShapeDtypeStruct((SM_W, GLA_QK), f32),
                   jax.ShapeDtypeStruct((1, GLA_QK), f32), jax.ShapeDtypeStruct((1, SM_W), f32),
                   jax.ShapeDtypeStruct((1, SM_W), f32)],
        compiler_params=_params("arbitrary"),
    )(psm, w2p, gate_b, alog, dtb, dgb, dla)


def _conv_pre(x_ext, w, n):
    rows = x_ext.shape[0]
    y = x_ext * w[CONV_K - 1:CONV_K, :]
    for s in range(1, CONV_K):
        y = y + pltpu.roll(x_ext, s, 0) * w[CONV_K - 1 - s:CONV_K - s, :]
    return y[rows - n:, :]


def _conv(proj, cw, side=None):
    Lp = proj.shape[0]
    W = cw.shape[1]
    tr = _tile(Lp, 256, 64)
    tc = _tile(W, 1024, 128)
    c0 = C_QKV // tc

    def body(h_ref, x_ref, w_ref, o_ref):
        i = pl.program_id(1)
        halo = jnp.where(i == 0, 0.0, h_ref[...])
        x_ext = jnp.concatenate([halo, x_ref[...]], axis=0)
        o_ref[...] = _silu(_conv_pre(x_ext, w_ref[...], tr))

    out = _call(
        body, name="conv", grid=(W // tc, Lp // tr),
        in_specs=[pl.BlockSpec((8, tc), lambda j, i: (jnp.maximum(i * (tr // 8) - 1, 0), j + c0)),
                  pl.BlockSpec((tr, tc), lambda j, i: (i, j + c0)),
                  pl.BlockSpec((CONV_K, tc), lambda j, i: (0, j))],
        out_specs=[pl.BlockSpec((tr, tc), lambda j, i: (i, j))],
        out_shape=[jax.ShapeDtypeStruct((Lp, W), f32)], args=[proj, proj, cw],
        sem=("parallel", "parallel"), side=side)
    return out[0] if side is None else (out[0], out[1:])


def _conv_bwd(proj, cw, dy, dproj, side=None):
    Lp = proj.shape[0]
    W = cw.shape[1]
    tr = _tile(Lp, 256, 64)
    tc = _tile(W, 1024, 128)
    c0 = C_QKV // tc
    nr = Lp // tr
    last8 = Lp // 8 - 1

    def body(xp_ref, x_ref, xn_ref, w_ref, d_ref, dn_ref, dproj_ref, o_ref, gw_ref):
        del dproj_ref
        i = pl.program_id(1)
        w = w_ref[...]
        xp = jnp.where(i == 0, 0.0, xp_ref[...])
        x_ext = jnp.concatenate([xp, x_ref[...], xn_ref[...]], axis=0)
        n = tr + 8
        pre = _conv_pre(x_ext, w, n)
        dn = jnp.where(i == nr - 1, 0.0, dn_ref[...])
        dpre = jnp.concatenate([d_ref[...], dn], axis=0) * _dsilu(pre)
        dx = dpre * w[CONV_K - 1:CONV_K, :]
        for s in range(1, CONV_K):
            dx = dx + pltpu.roll(dpre, n - s, 0) * w[CONV_K - 1 - s:CONV_K - s, :]
        o_ref[...] = dx[:tr, :].astype(bf16)
        dp = dpre[:tr, :]
        rows = []
        for k in range(CONV_K):
            xs = x_ext if k == CONV_K - 1 else pltpu.roll(x_ext, CONV_K - 1 - k, 0)
            rows.append(jnp.sum(dp * xs[8:8 + tr, :], axis=0, keepdims=True))
        gw = jnp.concatenate(rows, axis=0)

        @pl.when(i == 0)
        def _():
            gw_ref[...] = gw

        @pl.when(i > 0)
        def _():
            gw_ref[...] += gw

    cur = pl.BlockSpec((tr, tc), lambda j, i: (i, j))
    nxt = pl.BlockSpec((8, tc), lambda j, i: (jnp.minimum((i + 1) * (tr // 8), last8), j))
    pcur = pl.BlockSpec((tr, tc), lambda j, i: (i, j + c0))
    pprev = pl.BlockSpec((8, tc), lambda j, i: (jnp.maximum(i * (tr // 8) - 1, 0), j + c0))
    pnext = pl.BlockSpec((8, tc), lambda j, i: (jnp.minimum((i + 1) * (tr // 8), last8), j + c0))
    wsp = pl.BlockSpec((CONV_K, tc), lambda j, i: (0, j))
    out = _call(
        body, name="conv_bwd", grid=(W // tc, nr),
        in_specs=[pprev, pcur, pnext, wsp, cur, nxt, _ANY], out_specs=[pcur, wsp],
        out_shape=[jax.ShapeDtypeStruct(dproj.shape, dproj.dtype), jax.ShapeDtypeStruct((CONV_K, W), f32)],
        aliases={6: 0}, args=[proj, proj, proj, cw, dy, dy, dproj], sem=("parallel", "arbitrary"), side=side)
    return out[0], out[1], out[2:]


def _gdn_heads(x_ref, gbv):
    R = range(GDN_H)
    return ([x_ref[:, Q0 + h * GDN_DK:Q0 + (h + 1) * GDN_DK] for h in R],
            [x_ref[:, K0 + h * GDN_DK:K0 + (h + 1) * GDN_DK] for h in R],
            [x_ref[:, V0 + h * GDN_DV:V0 + (h + 1) * GDN_DV] for h in R],
            [gbv[:, GDN_H + h:GDN_H + h + 1] for h in R],
            [gbv[:, h:h + 1] for h in R])


def _gdn_fwd(qkvc, gb, side=None):
    Lp = qkvc.shape[0]
    N = Lp // GDN_C

    def body(x_ref, gb_ref, o_ref, sall_ref, pall_ref, s_scr):
        @pl.when(pl.program_id(0) == 0)
        def _():
            s_scr[...] = jnp.zeros_like(s_scr)

        R = range(GDN_H)
        Ss = [s_scr[h] for h in R]
        for h in R:
            sall_ref[0, h] = Ss[h]
        S2, o, p = _gdn_chunk(Ss, *_gdn_heads(x_ref, gb_ref[...]))
        for h in R:
            s_scr[h] = S2[h]
            o_ref[:, h * GDN_DV:(h + 1) * GDN_DV] = o[h]
            pall_ref[0, h] = p[h]

    out = _call(
        body, name="gdn_fwd", grid=(N,),
        in_specs=[pl.BlockSpec((GDN_C, QKV_W), lambda n: (n, 0)), pl.BlockSpec((GDN_C, SM_W), lambda n: (n, 0))],
        out_specs=[pl.BlockSpec((GDN_C, GDN_V), lambda n: (n, 0)),
                   pl.BlockSpec((1, GDN_H, GDN_DK, GDN_DV), lambda n: (n, 0, 0, 0)),
                   pl.BlockSpec((1, GDN_H, GDN_C, GDN_C), lambda n: (n, 0, 0, 0))],
        out_shape=[jax.ShapeDtypeStruct((Lp, GDN_V), f32), jax.ShapeDtypeStruct((N, GDN_H, GDN_DK, GDN_DV), f32),
                   jax.ShapeDtypeStruct((N, GDN_H, GDN_C, GDN_C), f32)],
        scratch_shapes=[pltpu.VMEM((GDN_H, GDN_DK, GDN_DV), f32)], args=[qkvc, gb], sem=("arbitrary",), side=side)
    return out[0], out[1], out[2], out[3:]


def _gdn_bwd(qkvc, gb, sall, pall, do, side=None):
    Lp = qkvc.shape[0]
    N = Lp // GDN_C

    def body(x_ref, gb_ref, sall_ref, pall_ref, do_ref, dx_ref, dgb_ref, ds_scr):
        @pl.when(pl.program_id(0) == 0)
        def _():
            ds_scr[...] = jnp.zeros_like(ds_scr)

        R = range(GDN_H)
        lane = lax.broadcasted_iota(jnp.int32, (GDN_C, SM_W), 1)
        ps = [pall_ref[0, h] for h in R]
        _, vjp = jax.vjp(lambda *a: _gdn_chunk(*a, Ps=ps)[:2],
                         [sall_ref[0, h] for h in R], *_gdn_heads(x_ref, gb_ref[...]))
        dS, dq, dk, dv, dbeta, dg = vjp(([ds_scr[h] for h in R],
                                         [do_ref[:, h * GDN_DV:(h + 1) * GDN_DV] for h in R]))
        acc = jnp.zeros((GDN_C, SM_W), f32)
        for h in R:
            ds_scr[h] = dS[h]
            dx_ref[:, Q0 + h * GDN_DK:Q0 + (h + 1) * GDN_DK] = dq[h]
            dx_ref[:, K0 + h * GDN_DK:K0 + (h + 1) * GDN_DK] = dk[h]
            dx_ref[:, V0 + h * GDN_DV:V0 + (h + 1) * GDN_DV] = dv[h]
            acc = acc + jnp.where(lane == h, dg[h], 0.0) + jnp.where(lane == GDN_H + h, dbeta[h], 0.0)
        dgb_ref[...] = acc

    rev = lambda n: (N - 1 - n, 0)
    out = _call(
        body, name="gdn_bwd", grid=(N,),
        in_specs=[pl.BlockSpec((GDN_C, QKV_W), rev), pl.BlockSpec((GDN_C, SM_W), rev),
                  pl.BlockSpec((1, GDN_H, GDN_DK, GDN_DV), lambda n: (N - 1 - n, 0, 0, 0)),
                  pl.BlockSpec((1, GDN_H, GDN_C, GDN_C), lambda n: (N - 1 - n, 0, 0, 0)),
                  pl.BlockSpec((GDN_C, GDN_V), rev)],
        out_specs=[pl.BlockSpec((GDN_C, QKV_W), rev), pl.BlockSpec((GDN_C, SM_W), rev)],
        out_shape=[jax.ShapeDtypeStruct((Lp, QKV_W), f32), jax.ShapeDtypeStruct((Lp, SM_W), f32)],
        scratch_shapes=[pltpu.VMEM((GDN_H, GDN_DK, GDN_DV), f32)], args=[qkvc, gb, sall, pall, do],
        sem=("arbitrary",), side=side)
    return out[0], out[1], out[2:]


GLA_BLOCK = 64


def _gla_group(Lp):
    nb = Lp // GLA_BLOCK
    return next(g for g in (3, 2, 1) if nb % g == 0)


def _gla_slices(h):
    sq = slice(h * GLA_DK, (h + 1) * GLA_DK)
    sk = slice(GLA_QK + h * GLA_DK, GLA_QK + (h + 1) * GLA_DK)
    sv = slice(2 * GLA_QK + h * GLA_DV, 2 * GLA_QK + (h + 1) * GLA_DV)
    return sq, sk, sv


def _gla_heads(x_ref, la_ref, group):
    qs, ks, vs, ls = [], [], [], []
    for blk in range(group):
        r = slice(blk * GLA_BLOCK, (blk + 1) * GLA_BLOCK)
        for h in range(GLA_H):
            sq, sk, sv = _gla_slices(h)
            qs.append(x_ref[r, sq])
            ks.append(x_ref[r, sk])
            vs.append(x_ref[r, sv])
            ls.append(la_ref[r, sq])
    return qs, ks, vs, ls


def _gla_fwd(proj, la):
    Lp = proj.shape[0]
    group = _gla_group(Lp)
    rows = group * GLA_BLOCK
    steps = Lp // rows
    R = range(GLA_H)

    def body(x_ref, la_ref, o_ref, sall_ref, s_scr):
        @pl.when(pl.program_id(0) == 0)
        def _():
            s_scr[...] = jnp.zeros_like(s_scr)

        Sts = [s_scr[h] for h in R]
        for h in R:
            sall_ref[0, h] = Sts[h]
        St2, o = _gla_blocks(Sts, *_gla_heads(x_ref, la_ref, group))
        for h in R:
            s_scr[h] = St2[h]
        for blk in range(group):
            for h in R:
                o_ref[blk * GLA_BLOCK:(blk + 1) * GLA_BLOCK, h * GLA_DV:(h + 1) * GLA_DV] = o[blk * GLA_H + h]

    return pl.pallas_call(
        body, name="gla_fwd", grid=(steps,),
        in_specs=[pl.BlockSpec((rows, G_W), lambda n: (n, C_G // G_W)),
                  pl.BlockSpec((rows, GLA_QK), lambda n: (n, 0))],
        out_specs=[pl.BlockSpec((rows, GLA_V), lambda n: (n, 0)),
                   pl.BlockSpec((1, GLA_H, GLA_DV, GLA_DK), lambda n: (n, 0, 0, 0))],
        out_shape=[jax.ShapeDtypeStruct((Lp, GLA_V), f32),
                   jax.ShapeDtypeStruct((steps, GLA_H, GLA_DV, GLA_DK), f32)],
        scratch_shapes=[pltpu.VMEM((GLA_H, GLA_DV, GLA_DK), f32)],
        compiler_params=_params("arbitrary"),
    )(proj, la)


def _gla_bwd(proj, la, sall, do, dproj, side=None):
    Lp = proj.shape[0]
    group = _gla_group(Lp)
    rows = group * GLA_BLOCK
    steps = Lp // rows
    R = range(GLA_H)

    def body(x_ref, la_ref, sall_ref, do_ref, dproj_ref, dx_ref, dla_ref, ds_scr):
        del dproj_ref

        @pl.when(pl.program_id(0) == 0)
        def _():
            ds_scr[...] = jnp.zeros_like(ds_scr)

        _, vjp = jax.vjp(_gla_blocks, [sall_ref[0, h] for h in R], *_gla_heads(x_ref, la_ref, group))
        do = [do_ref[blk * GLA_BLOCK:(blk + 1) * GLA_BLOCK, h * GLA_DV:(h + 1) * GLA_DV]
              for blk in range(group) for h in R]
        dS, dq, dk, dv, dl = vjp(([ds_scr[h] for h in R], do))
        for h in R:
            ds_scr[h] = dS[h]
        for blk in range(group):
            r = slice(blk * GLA_BLOCK, (blk + 1) * GLA_BLOCK)
            for h in R:
                sq, sk, sv = _gla_slices(h)
                i = blk * GLA_H + h
                dx_ref[r, sq] = dq[i].astype(bf16)
                dx_ref[r, sk] = dk[i].astype(bf16)
                dx_ref[r, sv] = dv[i].astype(bf16)
                dla_ref[r, sq] = dl[i]

    x_spec = pl.BlockSpec((rows, G_W), lambda n: (steps - 1 - n, C_G // G_W))
    rev = lambda n: (steps - 1 - n, 0)
    out = _call(
        body, name="gla_bwd", grid=(steps,),
        in_specs=[x_spec, pl.BlockSpec((rows, GLA_QK), rev),
                  pl.BlockSpec((1, GLA_H, GLA_DV, GLA_DK), lambda n: (steps - 1 - n, 0, 0, 0)),
                  pl.BlockSpec((rows, GLA_V), rev), _ANY],
        out_specs=[x_spec, pl.BlockSpec((rows, GLA_QK), rev)],
        out_shape=[jax.ShapeDtypeStruct(dproj.shape, dproj.dtype), jax.ShapeDtypeStruct((Lp, GLA_QK), f32)],
        aliases={4: 0}, scratch_shapes=[pltpu.VMEM((GLA_H, GLA_DV, GLA_DK), f32)],
        args=[proj, la, sall, do, dproj], sem=("arbitrary",), side=side)
    return out[0], out[1], out[2:]


def _gated_norm_fn(og, ol, zr, wg, wl):
    outs = []
    for h in range(GDN_H):
        s = slice(h * GDN_DV, (h + 1) * GDN_DV)
        outs.append(_rms(og[:, s])[0] * wg * _silu(zr[:, s]))
    for h in range(GLA_H):
        s = slice(h * GLA_DV, (h + 1) * GLA_DV)
        sr = slice(GDN_V + h * GLA_DV, GDN_V + (h + 1) * GLA_DV)
        outs.append(_rms(ol[:, s])[0] * wl * _silu(zr[:, sr]))
    return jnp.concatenate(outs, axis=-1)


def _gated_norm(og, ol, proj, wg, wl):
    Lp = og.shape[0]
    tr = _tile(Lp, 256)

    def body(og_ref, ol_ref, zr_ref, wg_ref, wl_ref, o_ref):
        o_ref[...] = _gated_norm_fn(og_ref[...], ol_ref[...], zr_ref[...], wg_ref[...], wl_ref[...]).astype(bf16)

    return pl.pallas_call(
        body, name="gated_norm", grid=(Lp // tr,),
        in_specs=[pl.BlockSpec((tr, GDN_V), lambda i: (i, 0)), pl.BlockSpec((tr, GLA_V), lambda i: (i, 0)),
                  pl.BlockSpec((tr, ZR_W), lambda i: (i, C_ZR // ZR_W)),
                  pl.BlockSpec((1, GDN_DV), lambda i: (0, 0)), pl.BlockSpec((1, GLA_DV), lambda i: (0, 0))],
        out_specs=pl.BlockSpec((tr, ZR_W), lambda i: (i, 0)),
        out_shape=jax.ShapeDtypeStruct((Lp, ZR_W), bf16),
        compiler_params=_params("parallel"),
    )(og, ol, proj, wg, wl)


def _gated_norm_bwd(og, ol, proj, wg, wl, dmix):
    Lp = og.shape[0]
    tr = _tile(Lp, 128)

    def body(og_ref, ol_ref, zr_ref, wg_ref, wl_ref, d_ref, dog_ref, dol_ref, dzr_ref, gwg_ref, gwl_ref):
        i = pl.program_id(0)
        _, vjp = jax.vjp(_gated_norm_fn, og_ref[...], ol_ref[...], zr_ref[...], wg_ref[...], wl_ref[...])
        dog, dol, dzr, gwg, gwl = vjp(d_ref[...])
        dog_ref[...] = dog
        dol_ref[...] = dol
        dzr_ref[...] = dzr.astype(bf16)

        @pl.when(i == 0)
        def _():
            gwg_ref[...] = gwg
            gwl_ref[...] = gwl

        @pl.when(i > 0)
        def _():
            gwg_ref[...] += gwg
            gwl_ref[...] += gwl

    og_spec = pl.BlockSpec((tr, GDN_V), lambda i: (i, 0))
    ol_spec = pl.BlockSpec((tr, GLA_V), lambda i: (i, 0))
    zr_spec = pl.BlockSpec((tr, ZR_W), lambda i: (i, C_ZR // ZR_W))
    vg = pl.BlockSpec((1, GDN_DV), lambda i: (0, 0))
    vl = pl.BlockSpec((1, GLA_DV), lambda i: (0, 0))
    return pl.pallas_call(
        body, name="gated_norm_bwd", grid=(Lp // tr,),
        in_specs=[og_spec, ol_spec, zr_spec, vg, vl, pl.BlockSpec((tr, ZR_W), lambda i: (i, 0))],
        out_specs=[og_spec, ol_spec, zr_spec, vg, vl],
        out_shape=[jax.ShapeDtypeStruct((Lp, GDN_V), f32), jax.ShapeDtypeStruct((Lp, GLA_V), f32),
                   jax.ShapeDtypeStruct((Lp, C_END), bf16),
                   jax.ShapeDtypeStruct((1, GDN_DV), f32), jax.ShapeDtypeStruct((1, GLA_DV), f32)],
        compiler_params=_params("arbitrary"),
    )(og, ol, proj, wg, wl, dmix)


def _adamw(g, w, m, v, name):
    R, C = g.shape
    tr = _tile(R, 256, 8) if R % 8 == 0 and R > 256 else R
    c1 = 1.0 - ADAM_B1 ** ADAM_STEP
    c2 = 1.0 - ADAM_B2 ** ADAM_STEP

    def body(g_ref, w_ref, m_ref, v_ref, d_ref, mo_ref, vo_ref):
        g_ = g_ref[...]
        m2 = ADAM_B1 * m_ref[...] + (1.0 - ADAM_B1) * g_
        v2 = ADAM_B2 * v_ref[...] + (1.0 - ADAM_B2) * (g_ * g_)
        mo_ref[...] = m2
        vo_ref[...] = v2
        d_ref[...] = -ADAM_LR * ((m2 / c1) / (jnp.sqrt(v2 / c2) + ADAM_EPS) + ADAM_WD * w_ref[...])

    blk = pl.BlockSpec((tr, C), lambda i: (i, 0))
    return pl.pallas_call(
        body, name=name, grid=(R // tr,), in_specs=[blk] * 4, out_specs=[blk] * 3,
        out_shape=[jax.ShapeDtypeStruct((R, C), f32)] * 3,
        compiler_params=_params("parallel"),
    )(g, w, m, v)


def _sum_slots(r, name):
    n, R, C = r.shape
    tr = _tile(R, 128, 16) if R % 16 == 0 and R > 128 else R

    def body(r_ref, o_ref):
        acc = r_ref[0].astype(f32)
        for s in range(1, n):
            acc = acc + r_ref[s].astype(f32)
        o_ref[...] = acc

    return pl.pallas_call(
        body, name=name, grid=(R // tr,),
        in_specs=[pl.BlockSpec((n, tr, C), lambda i: (0, i, 0))],
        out_specs=pl.BlockSpec((tr, C), lambda i: (i, 0)),
        out_shape=jax.ShapeDtypeStruct((R, C), f32),
        compiler_params=_params("parallel"),
    )(r)


class _Siblings:
    def __init__(self, arrays):
        self.arrays = list(arrays)
        self.n = len(self.arrays)
        self.out_shape = [jax.ShapeDtypeStruct((2,) + a.shape, a.dtype) for a in self.arrays]
        self.sems = [pltpu.SemaphoreType.DMA((self.n,)), pltpu.SemaphoreType.DMA((self.n,)),
                     pltpu.SemaphoreType.DMA((self.n,))]

    def hooks(self, ins, outs, send, recv, lsem):
        def copies():
            x, y, c = lax.axis_index("x"), lax.axis_index("y"), lax.axis_index("c")
            out = []
            for a in range(self.n):
                out.append((pltpu.make_async_copy(ins[a], outs[a].at[c], lsem.at[a]), None))
                mk = lambda dst, a=a: pltpu.make_async_remote_copy(
                    src_ref=ins[a], dst_ref=dst, send_sem=send.at[a], recv_sem=recv.at[a],
                    device_id=(x, y, 1 - c), device_id_type=MESH)
                out.append((mk(outs[a].at[c]), mk(outs[a].at[1 - c])))
            return out

        return _start_wait(copies)


def _comm_now(name, sides):
    total = sum(s.n for s in sides)

    def body(*refs):
        ins, outs, sems = refs[:total], refs[total:2 * total], refs[2 * total:]
        hooks, o = [], 0
        for i, s in enumerate(sides):
            hooks.append(s.hooks(ins[o:o + s.n], outs[o:o + s.n], *sems[3 * i:3 * i + 3]))
            o += s.n
        for start, _ in hooks:
            start()
        for _, wait in hooks:
            wait()

    out = pl.pallas_call(
        body, name=name, in_specs=[_ANY] * total, out_specs=[_ANY] * total,
        out_shape=[sh for s in sides for sh in s.out_shape], scratch_shapes=[sm for s in sides for sm in s.sems],
    )(*[a for s in sides for a in s.arrays])
    res, o = [], 0
    for s in sides:
        res.append(list(out[o:o + s.n]))
        o += s.n
    return res


def _cat_cols(g):
    return jnp.concatenate([g[i] for i in range(N_CHIP)], axis=-1)


def _row_slabs(a):
    return a.reshape(N_DEV, a.shape[0] // N_DEV, a.shape[1])


def _w_in_columns(g_wp, g_wsm):
    return jnp.concatenate([g_wp[:, C_QKV:C_END], g_wp[:, C_ZR:C_ZR + GDN_V], g_wsm[:, :SM_LR],
                            g_wp[:, C_G:C_G + G_W], g_wp[:, C_ZR + GDN_V:C_ZR + ZR_W],
                            g_wsm[:, SM_LR:SM_LR + GATE_RANK]], axis=1)


def _step(x, loss_target, p, meta, shard):
    _, S, D = x.shape
    alog_p = jnp.pad(p["gdn_a_log"], ((0, 0), (0, SM_W - GDN_H)))
    dtb_p = jnp.pad(p["gdn_dt_bias"], ((0, 0), (0, SM_W - GDN_H)))
    m64 = jnp.concatenate([jnp.zeros((PAD, D), f32), meta], axis=0)
    gate_b, gdn_norm_w, gla_norm_w = p["gla_gate_b"], p["gdn_norm_w"], p["gla_norm_w"]
    half = shard["w_up"].shape[0] // 2

    h0, n1, (w_in4, conv4, w24) = _embed_norm(
        x, m64, p["attn_norm_w"], side=_Gather([shard["w_in"], shard["gdn_conv_w"], shard["gla_gate_w2"]]))
    w_in, conv_w, w2 = _cat_cols(w_in4), _cat_cols(conv4), _cat_cols(w24)
    wp = jnp.concatenate([w_in[:, R_Z:R_AB], w_in[:, R_GR:R_LR], w_in[:, R_G:R_GR], w_in[:, R_QKV:R_Z]], axis=1)
    wsm = jnp.concatenate([w_in[:, R_AB:R_G], w_in[:, R_LR:R_END],
                           jnp.zeros((D, SM_W - SM_LR - GATE_RANK), w_in.dtype)], axis=1)
    w2p = jnp.pad(w2, ((SM_LR, SM_W - SM_LR - GATE_RANK), (0, 0)))
    proj, (w_out4, w_up4a) = _mm(n1, wp, "nn", "proj", side=_Gather([shard["w_out"], shard["w_up"][:half]]))
    w_out = w_out4.reshape(-1, D)
    psm = _mm(n1, wsm, "nn", "proj_small")
    gb, la = _gates(psm, w2p, gate_b, alog_p, dtb_p)
    qkvc, (w_up4b,) = _conv(proj, conv_w, side=_Gather([shard["w_up"][half:]]))
    w_up = jnp.concatenate([_cat_cols(w_up4a), _cat_cols(w_up4b)], axis=0)
    og, sall, pall, (w_gate4,) = _gdn_fwd(qkvc, gb, side=_Gather([shard["w_gate"]]))
    w_gate = _cat_cols(w_gate4)
    ol, stall = _gla_fwd(proj, la)
    mixed = _gated_norm(og, ol, proj, gdn_norm_w, gla_norm_w)
    attn = _mm(mixed, w_out, "nn", "out_proj")
    h1, n2 = _add_norm(h0, attn, p["ffn_norm_w"])
    act, gate, up, (w_down4,) = _ffn_in(n2, w_gate, w_up, side=_Gather([shard["w_down"]]))
    w_down = w_down4.reshape(-1, D)
    ffn = _mm(act, w_down, "nn", "ffn_down", tk_cap=1408)
    dh2, dh2b, lossp, g_final = _final(h1, ffn, loss_target, p["final_norm_w"])

    g_down = _mm(act, dh2b, "tn", "g_w_down", tk_cap=1408, out_dtype=bf16)
    dg, du = _ffn_dact(dh2b, w_down, gate, up)
    g_gate = _mm(n2, dg, "tn", "g_w_gate", tn_cap=1408, tk_cap=1408, out_dtype=bf16, col_slabs=True)
    g_up = _mm(n2, du, "tn", "g_w_up", tn_cap=1408, tk_cap=1408, out_dtype=bf16, col_slabs=True)
    dn2 = _mm(dg, w_gate, "nt", "d_n2_gate", tk_cap=1408)
    dn2 = _mm(du, w_up, "nt", "d_n2_up", tk_cap=1408, acc_in=dn2)
    dh1, dh1b, g_ffn_norm = _norm_bwd(dn2, h1, dh2, p["ffn_norm_w"])
    dmix = _mm(dh1b, w_out, "nt", "d_mixed")
    g_out = _mm(mixed, dh1b, "tn", "g_w_out", tk_cap=1408, out_dtype=bf16)
    dog, dol, dproj, g_gdn_norm, g_gla_norm = _gated_norm_bwd(og, ol, proj, gdn_norm_w, gla_norm_w, dmix)
    dproj, dla, (r_down,) = _gla_bwd(proj, la, stall, dol, dproj, side=_Exchange([_row_slabs(g_down)]))
    dqkvc, dgb, (r_gate, r_up, r_out) = _gdn_bwd(qkvc, gb, sall, pall, dog,
                                                 side=_Exchange([g_gate, g_up, _row_slabs(g_out)]))
    dproj, g_conv, (h_out, h_down) = _conv_bwd(
        proj, conv_w, dqkvc, dproj, side=_Siblings([_sum_slots(r_out, "sum_w_out"), _sum_slots(r_down, "sum_w_down")]))
    dpsm, g_w2p, g_gate_b, g_alog, g_dtb = _gates_bwd(psm, w2p, gate_b, alog_p, dtb_p, dgb, dla)
    g_wp, (h_gate, h_up) = _mm(
        n1, dproj, "tn", "g_w_in", tm_cap=2048, tk_cap=1408, out_dtype=bf16,
        side=_Siblings([_sum_slots(r_gate, "sum_w_gate"), _sum_slots(r_up, "sum_w_up")]))
    g_wsm = _mm(n1, dpsm, "tn", "g_w_in_small", tm_cap=2048, tk_cap=1408, out_dtype=bf16)
    dn1, r_in = _mm(dproj, wp, "nt", "d_n1", tk_cap=1024, side=_Exchange([_row_slabs(g_wp), _row_slabs(g_wsm)]))
    dn1 = _mm(dpsm, wsm, "nt", "d_n1_small", acc_in=dn1)
    grad_x, g_meta, g_attn_norm = _embed_norm_bwd(dn1, h0, dh1, p["attn_norm_w"], S)

    received = dict(w_in=tuple(r_in), w_gate=h_gate, w_up=h_up, w_out=h_out, w_down=h_down)
    small = dict(
        meta_tokens=g_meta, attn_norm_w=g_attn_norm, gdn_conv_w=g_conv, gdn_a_log=g_alog[:, :GDN_H],
        gdn_dt_bias=g_dtb[:, :GDN_H], gdn_norm_w=g_gdn_norm, gla_gate_w2=g_w2p[SM_LR:SM_LR + GATE_RANK],
        gla_gate_b=g_gate_b, gla_norm_w=g_gla_norm, ffn_norm_w=g_ffn_norm, final_norm_w=g_final)
    return lossp[0, 0], grad_x, received, small


_WEIGHTS = ("meta_tokens", "attn_norm_w", "w_in", "gdn_conv_w", "gdn_a_log", "gdn_dt_bias", "gdn_norm_w",
            "gla_gate_w2", "gla_gate_b", "gla_norm_w", "w_out", "ffn_norm_w", "w_gate", "w_up", "w_down",
            "final_norm_w")
_BIG_COLS = ("w_in", "w_gate", "w_up")
_BIG_ROWS = ("w_out", "w_down")
_SMALL_SHARDED = ("meta_tokens", "gdn_conv_w", "gla_gate_w2")


def kernel(x, meta_tokens, attn_norm_w, w_in, gdn_conv_w, gdn_a_log, gdn_dt_bias, gdn_norm_w, gla_gate_w2, gla_gate_b, gla_norm_w, w_out, ffn_norm_w, w_gate, w_up, w_down, final_norm_w, loss_target, m_meta_tokens, m_attn_norm_w, m_w_in, m_gdn_conv_w, m_gdn_a_log, m_gdn_dt_bias, m_gdn_norm_w, m_gla_gate_w2, m_gla_gate_b, m_gla_norm_w, m_w_out, m_ffn_norm_w, m_w_gate, m_w_up, m_w_down, m_final_norm_w, v_meta_tokens, v_attn_norm_w, v_w_in, v_gdn_conv_w, v_gdn_a_log, v_gdn_dt_bias, v_gdn_norm_w, v_gla_gate_w2, v_gla_gate_b, v_gla_norm_w, v_w_out, v_ffn_norm_w, v_w_gate, v_w_up, v_w_down, v_final_norm_w):
    w = dict(meta_tokens=meta_tokens, attn_norm_w=attn_norm_w, w_in=w_in, gdn_conv_w=gdn_conv_w, gdn_a_log=gdn_a_log,
             gdn_dt_bias=gdn_dt_bias, gdn_norm_w=gdn_norm_w, gla_gate_w2=gla_gate_w2, gla_gate_b=gla_gate_b,
             gla_norm_w=gla_norm_w, w_out=w_out, ffn_norm_w=ffn_norm_w, w_gate=w_gate, w_up=w_up, w_down=w_down,
             final_norm_w=final_norm_w)
    m = dict(meta_tokens=m_meta_tokens, attn_norm_w=m_attn_norm_w, w_in=m_w_in, gdn_conv_w=m_gdn_conv_w,
             gdn_a_log=m_gdn_a_log, gdn_dt_bias=m_gdn_dt_bias, gdn_norm_w=m_gdn_norm_w, gla_gate_w2=m_gla_gate_w2,
             gla_gate_b=m_gla_gate_b, gla_norm_w=m_gla_norm_w, w_out=m_w_out, ffn_norm_w=m_ffn_norm_w,
             w_gate=m_w_gate, w_up=m_w_up, w_down=m_w_down, final_norm_w=m_final_norm_w)
    v = dict(meta_tokens=v_meta_tokens, attn_norm_w=v_attn_norm_w, w_in=v_w_in, gdn_conv_w=v_gdn_conv_w,
             gdn_a_log=v_gdn_a_log, gdn_dt_bias=v_gdn_dt_bias, gdn_norm_w=v_gdn_norm_w, gla_gate_w2=v_gla_gate_w2,
             gla_gate_b=v_gla_gate_b, gla_norm_w=v_gla_norm_w, w_out=v_w_out, ffn_norm_w=v_ffn_norm_w,
             w_gate=v_w_gate, w_up=v_w_up, w_down=v_w_down, final_norm_w=v_final_norm_w)
    chip = 2 * lax.axis_index("x") + lax.axis_index("y")

    def two_d(a):
        return a.reshape(1, -1) if a.ndim == 1 else a.reshape(-1, a.shape[-1])

    w2d = {k: two_d(a) for k, a in w.items()}
    big = _BIG_COLS + _BIG_ROWS
    small = tuple(k for k in _WEIGHTS if k not in big)

    (meta4,), = _comm_now("gather_meta", [_Gather([w2d["meta_tokens"]])])
    shard = {k: w2d[k].astype(bf16) for k in big}
    shard.update({k: w2d[k] for k in ("gdn_conv_w", "gla_gate_w2")})
    lossp, grad_x, received, g = _step(x, loss_target, {k: w2d[k] for k in small}, _cat_cols(meta4), shard)
    loss = lax.psum(lossp, ("x", "y", "c"))

    sizes = [g[k].size for k in small]
    total = sum(sizes)
    rows = -(-total // 1024)
    rows += (-rows) % 8
    packed = jnp.concatenate([g[k].reshape(-1) for k in small] + [jnp.zeros((rows * 1024 - total,), f32)])
    r_wp, r_wsm = received.pop("w_in")
    s_in = _w_in_columns(_sum_slots(r_wp, "sum_w_in"), _sum_slots(r_wsm, "sum_w_in_small"))
    in_by_chip = s_in.reshape(s_in.shape[0], N_CHIP, -1).transpose(1, 0, 2)
    (in8, packed8), = _comm_now("exchange_tail", [_Exchange([], [packed.reshape(rows, 1024)], by_chip=[in_by_chip])])
    red = {k: h.reshape(w2d[k].shape) for k, h in received.items()}
    red["w_in"] = in8.reshape(w2d["w_in"].shape)
    psum_small = _sum_slots(packed8, "sum_small").reshape(-1)
    off = 0
    for k, n in zip(small, sizes):
        a = psum_small[off:off + n].reshape(g[k].shape)
        off += n
        if k in _SMALL_SHARDED:
            c = w2d[k].shape[1]
            a = lax.dynamic_slice_in_dim(a, chip * c, c, axis=1)
        red[k] = a

    grads, deltas, new_m, new_v = [], [], [], []
    for k in _WEIGHTS:
        d, m2, v2 = _adamw(red[k], w2d[k], two_d(m[k]), two_d(v[k]), "adamw_" + k)
        shape = w[k].shape
        grads.append(red[k].reshape(shape))
        deltas.append(d.reshape(shape))
        new_m.append(m2.reshape(shape))
        new_v.append(v2.reshape(shape))
    return (loss, grad_x, *grads, *deltas, *new_m, *new_v)
```

```python
import functools

import jax
import jax.numpy as jnp
from jax import lax
from jax.experimental import pallas as pl
from jax.experimental.pallas import tpu as pltpu

f32 = jnp.float32
bf16 = jnp.bfloat16
HIGH = lax.Precision.HIGH
MESH = pl.DeviceIdType.MESH

N_META = 16
CONV_K = 4
GDN_H, GDN_DK, GDN_DV, GDN_C = 8, 128, 128, 64
GLA_H, GLA_DK, GLA_DV, GLA_C = 4, 128, 256, 16
GATE_RANK = 16
GATE_NORMALIZER = 16.0
EPS = 1e-6
GDN_QK = GDN_H * GDN_DK
GDN_V = GDN_H * GDN_DV
GLA_QK = GLA_H * GLA_DK
GLA_V = GLA_H * GLA_DV
PAD = (-N_META) % GDN_C
OFF = PAD + N_META
ROWS = 64

R_QKV, R_Z, R_AB, R_G, R_GR, R_LR, R_END = 0, 3072, 4096, 4112, 6160, 7184, 7200
C_ZR, C_G, C_QKV, C_END = 0, 2048, 4096, 7168
ZR_W = GDN_V + GLA_V
G_W = 2 * GLA_QK + GLA_V
QKV_W = 2 * GDN_QK + GDN_V
Q0, K0, V0 = 0, GDN_QK, 2 * GDN_QK
SM_W = 128
SM_LR = 2 * GDN_H

ADAM_LR, ADAM_B1, ADAM_B2, ADAM_EPS, ADAM_WD, ADAM_STEP = 0.001, 0.9, 0.999, 1e-08, 0.01, 10

VMEM_LIMIT_V7X = 56 * 1024 * 1024
N_DEV = 8
N_CHIP = 4


def _params(*sem):
    return pltpu.CompilerParams(dimension_semantics=sem, vmem_limit_bytes=VMEM_LIMIT_V7X)


def _tile(n, cap, mult=16):
    best = None
    for d in range(mult, min(n, cap) + 1, mult):
        if n % d == 0:
            best = d
    assert best is not None, (n, cap, mult)
    return best


NN = ((1,), (0,))
NT = ((1,), (1,))
TN = ((0,), (0,))


def _dot(a, b, dims, prec=None):
    return lax.dot_general(a, b, (dims, ((), ())), precision=prec, preferred_element_type=f32)


def _mmb(a, b, dims):
    return _dot(a.astype(bf16), b.astype(bf16), dims)


def _sigmoid(x):
    return jax.nn.sigmoid(x)


def _silu(x):
    return x * _sigmoid(x)


def _dsilu(x):
    s = _sigmoid(x)
    return s * (1.0 + x * (1.0 - s))


def _log1p_exp_neg_abs(x):
    t = jnp.exp(-jnp.abs(x))
    u = 1.0 + t
    d = u - 1.0
    return jnp.where(d == 0.0, t, jnp.log(u) * (t / jnp.where(d == 0.0, 1.0, d)))


def _softplus(x):
    return jnp.maximum(x, 0.0) + _log1p_exp_neg_abs(x)


def _log_sigmoid(x):
    return jnp.minimum(x, 0.0) - _log1p_exp_neg_abs(x)


def _rms(x):
    r = lax.rsqrt(jnp.mean(x * x, axis=-1, keepdims=True) + EPS)
    return x * r, r


def _rms_bwd(dy, xh, r, w):
    t = dy * w
    return r * (t - xh * jnp.mean(t * xh, axis=-1, keepdims=True))


def _l2n(x):
    return x * lax.rsqrt(jnp.sum(x * x, axis=-1, keepdims=True) + EPS)


INV_LEAF = 8


def _same_block(C, b):
    sh = b.bit_length() - 1
    row = lax.broadcasted_iota(jnp.int32, (C, C), 0)
    col = lax.broadcasted_iota(jnp.int32, (C, C), 1)
    return lax.shift_right_logical(row, sh) == lax.shift_right_logical(col, sh)


def _tri_inv_impl(As):
    C = As[0].shape[0]
    R = range(len(As))
    row = lax.broadcasted_iota(jnp.int32, (C, C), 0)
    col = lax.broadcasted_iota(jnp.int32, (C, C), 1)
    eye = (row == col).astype(f32)
    b = INV_LEAF
    inner = _same_block(C, b)
    leaf = [jnp.where(inner, As[h], 0.0) for h in R]
    d = [eye - leaf[h] for h in R]
    pw = leaf
    n = 2
    while n < b:
        pw = [_dot(pw[h], pw[h], NN, HIGH) for h in R]
        d = [_dot(d[h], eye + pw[h], NN, HIGH) for h in R]
        n *= 2
    while b < C:
        outer = _same_block(C, 2 * b)
        level = jnp.logical_and(outer, jnp.logical_not(inner))
        ed = [_dot(jnp.where(level, As[h], 0.0), d[h], NN, HIGH) for h in R]
        d = [d[h] - _dot(d[h], ed[h], NN, HIGH) for h in R]
        inner = outer
        b *= 2
    return d


@jax.custom_vjp
def _tri_inv(As):
    return _tri_inv_impl(As)


def _tri_inv_fwd(As):
    d = _tri_inv_impl(As)
    return d, d


def _tri_inv_bwd(d, g):
    R = range(len(d))
    t = [_dot(d[h], g[h], TN, HIGH) for h in R]
    return ([-_dot(t[h], d[h], NT, HIGH) for h in R],)


_tri_inv.defvjp(_tri_inv_fwd, _tri_inv_bwd)


@jax.custom_vjp
def _tri_inv_known(As, Ps):
    del As
    return Ps


def _tri_inv_known_fwd(As, Ps):
    del As
    return Ps, Ps


def _tri_inv_known_bwd(d, g):
    return _tri_inv_bwd(d, g)[0], [jnp.zeros_like(x) for x in d]


_tri_inv_known.defvjp(_tri_inv_known_fwd, _tri_inv_known_bwd)


def _gdn_chunk(Ss, qrs, krs, vs, betas, gs, Ps=None):
    C, dk = qrs[0].shape
    R = range(len(Ss))
    row = lax.broadcasted_iota(jnp.int32, (C, C), 0)
    col = lax.broadcasted_iota(jnp.int32, (C, C), 1)
    causal = row >= col
    strict = row > col
    cf = causal.astype(f32)
    q = [_l2n(qrs[h]) * (dk ** -0.5) for h in R]
    k = [_l2n(krs[h]) for h in R]
    mc = [_rows_exact(cf, jnp.broadcast_to(gs[h], (C, C))) for h in R]
    gc = [mc[h][:, 0:1] for h in R]
    decay = [jnp.where(causal, jnp.exp(jnp.where(causal, mc[h] - mc[h].T, 0.0)), 0.0) for h in R]
    kb = [k[h] * betas[h] for h in R]
    a = [jnp.where(strict, _mmb(kb[h], k[h], NT) * decay[h], 0.0) for h in R]
    p = _tri_inv(a) if Ps is None else _tri_inv_known(a, Ps)
    egc = [jnp.exp(gc[h]) for h in R]
    u = [_mmb(p[h], vs[h] * betas[h], NN) for h in R]
    w = [_mmb(p[h], kb[h] * egc[h], NN) for h in R]
    qk = [jnp.where(causal, _mmb(q[h], k[h], NT) * decay[h], 0.0) for h in R]
    v_new = [u[h] - _mmb(w[h], Ss[h], NN) for h in R]
    o = [_mmb(q[h] * egc[h], Ss[h], NN) + _mmb(qk[h], v_new[h], NN) for h in R]
    gl = [gc[h][C - 1:C, :] for h in R]
    kd = [k[h] * jnp.exp(gl[h] - gc[h]) for h in R]
    S2 = [Ss[h] * jnp.exp(gl[h]) + _mmb(kd[h], v_new[h], TN) for h in R]
    return S2, o, p


def _rows_exact_impl(m01, x, dims):
    m = m01.astype(bf16)
    x1 = x.astype(bf16)
    r1 = x - x1.astype(f32)
    x2 = r1.astype(bf16)
    x3 = (r1 - x2.astype(f32)).astype(bf16)
    d = lambda y: _dot(m, y, dims)
    return d(x1) + (d(x2) + d(x3))


@jax.custom_vjp
def _rows_exact(m01, x):
    return _rows_exact_impl(m01, x, NN)


def _rows_exact_fwd(m01, x):
    return _rows_exact_impl(m01, x, NN), m01


def _rows_exact_bwd(m01, g):
    return jnp.zeros_like(m01), _rows_exact_impl(m01, g, TN)


_rows_exact.defvjp(_rows_exact_fwd, _rows_exact_bwd)


def _gla_blocks(Sts, qrs, ks, vs, las):
    H = len(Sts)
    n = len(qrs)
    C, dk = qrs[0].shape
    R = range(n)
    row = lax.broadcasted_iota(jnp.int32, (C, C), 0)
    col = lax.broadcasted_iota(jnp.int32, (C, C), 1)
    ri = lax.broadcasted_iota(jnp.int32, (C, dk), 0)
    q = [qrs[h] * (dk ** -0.5) for h in R]
    running = (row >= col).astype(f32)
    b = [_rows_exact(running, las[h]) for h in R]
    sc = [jnp.where(row == col, jnp.sum(q[h] * ks[h], axis=-1, keepdims=True), 0.0) for h in R]
    s = C // 2
    while s >= 1:
        sh = s.bit_length() - 1
        ref = lax.shift_left(lax.shift_right_logical(row, sh + 1), sh + 1) + (s - 1)
        pick = (col == ref).astype(f32)
        bref = [_rows_exact(pick, b[h]) for h in R]
        upper = (lax.shift_right_logical(ri, sh) & 1) == 1
        qt = [jnp.where(upper, q[h] * jnp.exp(jnp.where(upper, b[h] - bref[h], 0.0)), 0.0) for h in R]
        kt = [jnp.where(upper, 0.0, ks[h] * jnp.exp(jnp.where(upper, 0.0, bref[h] - b[h]))) for h in R]
        same = lax.shift_right_logical(row, sh + 1) == lax.shift_right_logical(col, sh + 1)
        sc = [sc[h] + jnp.where(same, _mmb(qt[h], kt[h], NT), 0.0) for h in R]
        s //= 2
    o = [_mmb(sc[h], vs[h], NN) for h in R]
    qe = [q[h] * jnp.exp(b[h]) for h in R]
    bl = [b[h][C - 1:C, :] for h in R]
    upd = [_mmb(vs[h], ks[h] * jnp.exp(bl[h] - b[h]), TN) for h in R]
    ebl = [jnp.exp(bl[h]) for h in R]
    St = list(Sts)
    for blk in range(n // H):
        for h in range(H):
            i = blk * H + h
            o[i] = o[i] + _mmb(qe[i], St[h], NT)
        St = [St[h] * ebl[blk * H + h] + upd[blk * H + h] for h in range(H)]
    return St, o


_ANY = pl.BlockSpec(memory_space=pl.ANY)


class _Gather:
    def __init__(self, arrays):
        self.arrays = list(arrays)
        self.n = len(self.arrays)
        self.out_shape = [jax.ShapeDtypeStruct((N_CHIP,) + a.shape, a.dtype) for a in self.arrays]
        self.sems = [pltpu.SemaphoreType.DMA((self.n, 3)), pltpu.SemaphoreType.DMA((self.n, 3)),
                     pltpu.SemaphoreType.DMA((self.n,))]

    def hooks(self, ins, outs, send, recv, lsem):
        def copies():
            x, y, c = lax.axis_index("x"), lax.axis_index("y"), lax.axis_index("c")
            me = 2 * x + y
            out = []
            for a in range(self.n):
                out.append((pltpu.make_async_copy(ins[a], outs[a].at[me], lsem.at[a]), None))
                for j, (px, py) in enumerate([(1 - x, y), (x, 1 - y), (1 - x, 1 - y)]):
                    mk = lambda dst, a=a, j=j, px=px, py=py: pltpu.make_async_remote_copy(
                        src_ref=ins[a], dst_ref=dst, send_sem=send.at[a, j], recv_sem=recv.at[a, j],
                        device_id=(px, py, c), device_id_type=MESH)
                    out.append((mk(outs[a].at[me]), mk(outs[a].at[2 * px + py])))
            return out

        return _start_wait(copies)


class _Exchange:
    def __init__(self, slotted, shared=(), by_chip=()):
        self.arrays = list(slotted) + list(by_chip) + list(shared)
        self.ns, self.nc = len(slotted), len(by_chip)
        self.n = len(self.arrays)
        self.out_shape = [jax.ShapeDtypeStruct(a.shape, a.dtype) for a in slotted]
        self.out_shape += [jax.ShapeDtypeStruct((N_DEV,) + a.shape[1:], a.dtype) for a in by_chip]
        self.out_shape += [jax.ShapeDtypeStruct((N_DEV,) + b.shape, b.dtype) for b in shared]
        self.sems = [pltpu.SemaphoreType.DMA((self.n, N_DEV - 1)), pltpu.SemaphoreType.DMA((self.n, N_DEV - 1)),
                     pltpu.SemaphoreType.DMA((self.n,))]

    def hooks(self, ins, outs, send, recv, lsem):
        def copies():
            x, y, c = lax.axis_index("x"), lax.axis_index("y"), lax.axis_index("c")
            me = 4 * x + 2 * y + c

            def src(a, dev):
                tx, ty, tc = dev
                if a < self.ns:
                    return ins[a].at[4 * tx + 2 * ty + tc]
                return ins[a].at[2 * tx + ty] if a < self.ns + self.nc else ins[a]

            out = []
            for a in range(self.n):
                out.append((pltpu.make_async_copy(src(a, (x, y, c)), outs[a].at[me], lsem.at[a]), None))
                for o in range(1, N_DEV):
                    dev = (1 - x if o & 4 else x, 1 - y if o & 2 else y, 1 - c if o & 1 else c)
                    t = 4 * dev[0] + 2 * dev[1] + dev[2]
                    mk = lambda dst, a=a, o=o, dev=dev: pltpu.make_async_remote_copy(
                        src_ref=src(a, dev), dst_ref=dst, send_sem=send.at[a, o - 1], recv_sem=recv.at[a, o - 1],
                        device_id=dev, device_id_type=MESH)
                    out.append((mk(outs[a].at[me]), mk(outs[a].at[t])))
            return out

        return _start_wait(copies)


def _start_wait(copies):
    def start():
        for s, _ in copies():
            s.start()

    def wait():
        for s, w in copies():
            (s if w is None else w).wait()

    return start, wait


def _call(body, *, name, grid, in_specs, out_specs, out_shape, args, sem, scratch_shapes=(), aliases=None, side=None):
    in_specs, out_specs, out_shape, args = list(in_specs), list(out_specs), list(out_shape), list(args)
    scratch_shapes = list(scratch_shapes)
    aliases = aliases or {}
    if side is None:
        return pl.pallas_call(
            body, name=name, grid=grid, in_specs=in_specs, out_specs=out_specs, out_shape=out_shape,
            scratch_shapes=scratch_shapes, input_output_aliases=aliases, compiler_params=_params(*sem))(*args)
    n_in, n_out, n_scr, ns = len(in_specs), len(out_specs), len(scratch_shapes), side.n

    def full_body(*refs):
        ins, refs = refs[:n_in], refs[n_in:]
        s_in, refs = refs[:ns], refs[ns:]
        outs, refs = refs[:n_out], refs[n_out:]
        s_out, refs = refs[:ns], refs[ns:]
        scr, sems = refs[:n_scr], refs[n_scr:]
        start, wait = side.hooks(s_in, s_out, *sems)
        ids = [pl.program_id(d) for d in range(len(grid))]
        first = functools.reduce(jnp.logical_and, [i == 0 for i in ids])
        last = functools.reduce(jnp.logical_and, [i == g - 1 for i, g in zip(ids, grid)])
        pl.when(first)(start)
        body(*ins, *outs, *scr)
        pl.when(last)(wait)

    return pl.pallas_call(
        full_body, name=name, grid=grid, in_specs=in_specs + [_ANY] * ns, out_specs=out_specs + [_ANY] * ns,
        out_shape=out_shape + side.out_shape, scratch_shapes=scratch_shapes + side.sems,
        input_output_aliases=aliases, compiler_params=_params(*(["arbitrary"] * len(grid))))(*args, *side.arrays)


WHOLE_K = dict(tm_cap=688, tn_cap=512, tk_cap=1 << 20)
WHOLE_K_T = dict(tm_cap=512, tn_cap=512, tk_cap=1 << 20)

def _mm(a, b, mode, name, *, tm_cap=1408, tn_cap=1024, tk_cap=2048, out_dtype=f32, acc_in=None, side=None,
        col_slabs=False):
    if mode == "nn":
        (M, K), (K2, N) = a.shape, b.shape
    elif mode == "nt":
        (M, K), (N, K2) = a.shape, b.shape
    else:
        (K, M), (K2, N) = a.shape, b.shape
    assert K == K2, (name, a.shape, b.shape)
    tm = _tile(M // 2 if col_slabs else M, tm_cap)
    tn = _tile(N // N_CHIP if col_slabs else N, tn_cap, 128)
    tk = _tile(K, tk_cap, 128 if K % 128 == 0 else 16)
    nk = K // tk
    dims = {"nn": NN, "nt": NT, "tn": TN}[mode]
    use_scratch = nk > 1 and out_dtype != f32

    def body(*refs):
        if acc_in is not None:
            a_ref, b_ref, c_ref, o_ref, *scr = refs
        else:
            a_ref, b_ref, o_ref, *scr = refs
            c_ref = None
        p = _mmb(a_ref[...], b_ref[...], dims)
        if nk == 1:
            if c_ref is not None:
                p = p + c_ref[...]
            o_ref[...] = p.astype(out_dtype)
            return
        k = pl.program_id(2)
        acc = scr[0] if use_scratch else o_ref

        @pl.when(k == 0)
        def _():
            acc[...] = p if c_ref is None else p + c_ref[...]

        @pl.when(k > 0)
        def _():
            acc[...] += p

        if use_scratch:
            @pl.when(k == nk - 1)
            def _():
                o_ref[...] = acc[...].astype(out_dtype)

    if mode == "tn":
        a_spec = pl.BlockSpec((tk, tm), lambda i, j, k: (k, i))
    else:
        a_spec = pl.BlockSpec((tm, tk), lambda i, j, k: (i, k))
    if mode == "nt":
        b_spec = pl.BlockSpec((tn, tk), lambda i, j, k: (j, k))
    else:
        b_spec = pl.BlockSpec((tk, tn), lambda i, j, k: (k, j))
    if col_slabs:
        assert acc_in is None
        ni, nj = M // 2 // tm, N // N_CHIP // tn
        o_spec = pl.BlockSpec((None, tm, tn), lambda i, j, k: (2 * (j // nj) + i // ni, i % ni, j % nj))
        o_shape = jax.ShapeDtypeStruct((N_DEV, M // 2, N // N_CHIP), out_dtype)
    else:
        o_spec = pl.BlockSpec((tm, tn), lambda i, j, k: (i, j))
        o_shape = jax.ShapeDtypeStruct((M, N), out_dtype)
    in_specs = [a_spec, b_spec]
    args = [a, b]
    if acc_in is not None:
        in_specs.append(o_spec)
        args.append(acc_in)
    out = _call(body, name=name, grid=(M // tm, N // tn, nk), in_specs=in_specs, out_specs=[o_spec],
                out_shape=[o_shape], args=args,
                scratch_shapes=[pltpu.VMEM((tm, tn), f32)] if use_scratch else [],
                sem=("parallel", "parallel", "arbitrary"), side=side)
    return out[0] if side is None else (out[0], out[1:])


def _embed_norm(x3, m64, w, side=None):
    _, S, D = x3.shape
    Lp = OFF + S

    def body(x_ref, m_ref, w_ref, h_ref, n_ref):
        i = pl.program_id(0)
        h = jnp.where(i == 0, m_ref[...], x_ref[...])
        h_ref[...] = h
        xh, _ = _rms(h)
        n_ref[...] = (xh * w_ref[...]).astype(bf16)

    row = pl.BlockSpec((ROWS, D), lambda i: (i, 0))
    out = _call(
        body, name="embed_norm", grid=(Lp // ROWS,),
        in_specs=[pl.BlockSpec((None, ROWS, D), lambda i: (0, jnp.maximum(i - 1, 0), 0)),
                  pl.BlockSpec((ROWS, D), lambda i: (0, 0)),
                  pl.BlockSpec((1, D), lambda i: (0, 0))],
        out_specs=[row, row],
        out_shape=[jax.ShapeDtypeStruct((Lp, D), f32), jax.ShapeDtypeStruct((Lp, D), bf16)],
        args=[x3, m64, w], sem=("parallel",), side=side)
    return out[0], out[1], out[2:]


def _add_norm(h, d, w):
    Lp, D = h.shape
    tr = _tile(Lp, 256)

    def body(h_ref, d_ref, w_ref, o_ref, n_ref):
        h1 = h_ref[...] + d_ref[...]
        o_ref[...] = h1
        xh, _ = _rms(h1)
        n_ref[...] = (xh * w_ref[...]).astype(bf16)

    row = pl.BlockSpec((tr, D), lambda i: (i, 0))
    return pl.pallas_call(
        body, name="add_norm", grid=(Lp // tr,),
        in_specs=[row, row, pl.BlockSpec((1, D), lambda i: (0, 0))], out_specs=[row, row],
        out_shape=[jax.ShapeDtypeStruct((Lp, D), f32), jax.ShapeDtypeStruct((Lp, D), bf16)],
        compiler_params=_params("parallel"),
    )(h, d, w)


def _norm_bwd(dn, h, dh, w):
    Lp, D = h.shape
    tr = _tile(Lp, 256)

    def body(dn_ref, h_ref, dh_ref, w_ref, o_ref, ob_ref, gw_ref):
        i = pl.program_id(0)
        xh, r = _rms(h_ref[...])
        dn_ = dn_ref[...]
        o = dh_ref[...] + _rms_bwd(dn_, xh, r, w_ref[...])
        o_ref[...] = o
        ob_ref[...] = o.astype(bf16)
        gw = jnp.sum(dn_ * xh, axis=0, keepdims=True)

        @pl.when(i == 0)
        def _():
            gw_ref[...] = gw

        @pl.when(i > 0)
        def _():
            gw_ref[...] += gw

    row = pl.BlockSpec((tr, D), lambda i: (i, 0))
    vec = pl.BlockSpec((1, D), lambda i: (0, 0))
    return pl.pallas_call(
        body, name="norm_bwd", grid=(Lp // tr,), in_specs=[row, row, row, vec], out_specs=[row, row, vec],
        out_shape=[jax.ShapeDtypeStruct((Lp, D), f32), jax.ShapeDtypeStruct((Lp, D), bf16),
                   jax.ShapeDtypeStruct((1, D), f32)],
        compiler_params=_params("arbitrary"),
    )(dn, h, dh, w)


def _embed_norm_bwd(dn, h, dh, w, S, side=None):
    Lp, D = h.shape

    def body(dn_ref, h_ref, dh_ref, w_ref, gx_ref, gm_ref, gw_ref):
        i = pl.program_id(0)
        xh, r = _rms(h_ref[...])
        dn_ = dn_ref[...]
        d0 = dh_ref[...] + _rms_bwd(dn_, xh, r, w_ref[...])
        gx_ref[...] = d0
        gw = jnp.sum(dn_ * xh, axis=0, keepdims=True)

        @pl.when(i == 0)
        def _():
            gm_ref[...] = d0[PAD:OFF, :]
            gw_ref[...] = gw

        @pl.when(i > 0)
        def _():
            gw_ref[...] += gw

    row = pl.BlockSpec((ROWS, D), lambda i: (i, 0))
    vec = pl.BlockSpec((1, D), lambda i: (0, 0))
    out = _call(
        body, name="embed_norm_bwd", grid=(Lp // ROWS,), in_specs=[row, row, row, vec],
        out_specs=[pl.BlockSpec((None, ROWS, D), lambda i: (0, jnp.maximum(i - 1, 0), 0)),
                   pl.BlockSpec((N_META, D), lambda i: (0, 0)), vec],
        out_shape=[jax.ShapeDtypeStruct((1, S, D), f32), jax.ShapeDtypeStruct((N_META, D), f32),
                   jax.ShapeDtypeStruct((1, D), f32)],
        args=[dn, h, dh, w], sem=("arbitrary",), side=side)
    return out[0], out[1], out[2], out[3:]


def _final(h1, ffn, tgt3, w):
    Lp, D = h1.shape

    def body(h_ref, f_ref, t_ref, w_ref, d_ref, db_ref, l_ref, gw_ref):
        i = pl.program_id(0)
        h2 = h_ref[...] + f_ref[...]
        xh, r = _rms(h2)
        w_ = w_ref[...]
        e = xh * w_ - t_ref[...]
        valid = (i > 0).astype(f32)
        loss = 0.5 * jnp.sum(jnp.mean(e * e, axis=-1, keepdims=True), axis=0, keepdims=True) * valid
        dy = e * (valid / D)
        d = _rms_bwd(dy, xh, r, w_)
        d_ref[...] = d
        db_ref[...] = d.astype(bf16)
        gw = jnp.sum(dy * xh, axis=0, keepdims=True)

        @pl.when(i == 0)
        def _():
            l_ref[...] = jnp.zeros_like(l_ref)
            gw_ref[...] = jnp.zeros_like(gw_ref)

        l_ref[...] += jnp.broadcast_to(loss, l_ref.shape)
        gw_ref[...] += gw

    row = pl.BlockSpec((ROWS, D), lambda i: (i, 0))
    vec = pl.BlockSpec((1, D), lambda i: (0, 0))
    return pl.pallas_call(
        body, name="final_loss", grid=(Lp // ROWS,),
        in_specs=[row, row, pl.BlockSpec((None, ROWS, D), lambda i: (0, jnp.maximum(i - 1, 0), 0)), vec],
        out_specs=[row, row, pl.BlockSpec((8, 128), lambda i: (0, 0)), vec],
        out_shape=[jax.ShapeDtypeStruct((Lp, D), f32), jax.ShapeDtypeStruct((Lp, D), bf16),
                   jax.ShapeDtypeStruct((8, 128), f32), jax.ShapeDtypeStruct((1, D), f32)],
        compiler_params=_params("arbitrary"),
    )(h1, ffn, tgt3, w)


def _ffn_in(n, w_gate, w_up, side=None):
    M, K = n.shape
    F = w_gate.shape[1]
    tm = _tile(M, 1408)
    tn = _tile(F, 512, 128)

    def body(a_ref, bg_ref, bu_ref, act_ref, g_ref, u_ref):
        a = a_ref[...]
        g = _mmb(a, bg_ref[...], NN)
        u = _mmb(a, bu_ref[...], NN)
        act_ref[...] = (_silu(g) * u).astype(bf16)
        g_ref[...] = g.astype(bf16)
        u_ref[...] = u.astype(bf16)

    wsp = pl.BlockSpec((K, tn), lambda i, j: (0, j))
    osp = pl.BlockSpec((tm, tn), lambda i, j: (i, j))
    out = _call(body, name="ffn_in", grid=(M // tm, F // tn),
                in_specs=[pl.BlockSpec((tm, K), lambda i, j: (i, 0)), wsp, wsp], out_specs=[osp] * 3,
                out_shape=[jax.ShapeDtypeStruct((M, F), bf16)] * 3, args=[n, w_gate, w_up],
                sem=("parallel", "parallel"), side=side)
    return out[0], out[1], out[2], out[3:]


def _ffn_dact(d, w_down, g, u):
    M, K = d.shape
    F = w_down.shape[0]
    tm = _tile(M, 1408)
    tn = _tile(F, 512, 128)

    def body(d_ref, w_ref, g_ref, u_ref, dg_ref, du_ref):
        da = _mmb(d_ref[...], w_ref[...], NT)
        g_ = g_ref[...].astype(f32)
        dg_ref[...] = (da * u_ref[...].astype(f32) * _dsilu(g_)).astype(bf16)
        du_ref[...] = (da * _silu(g_)).astype(bf16)

    osp = pl.BlockSpec((tm, tn), lambda i, j: (i, j))
    return pl.pallas_call(
        body, name="ffn_dact", grid=(M // tm, F // tn),
        in_specs=[pl.BlockSpec((tm, K), lambda i, j: (i, 0)), pl.BlockSpec((tn, K), lambda i, j: (j, 0)), osp, osp],
        out_specs=[osp, osp], out_shape=[jax.ShapeDtypeStruct((M, F), bf16)] * 2,
        compiler_params=_params("parallel", "parallel"),
    )(d, w_down, g, u)


def _gates(psm, w2p, gate_b, alog, dtb):
    Lp = psm.shape[0]
    tr = _tile(Lp, 256)

    def body(p_ref, w_ref, b_ref, a_ref, t_ref, gb_ref, la_ref):
        i = pl.program_id(0)
        psm_ = p_ref[...]
        lane = lax.broadcasted_iota(jnp.int32, psm_.shape, 1)
        rowi = lax.broadcasted_iota(jnp.int32, (tr, 1), 0) + i * tr
        g = -jnp.exp(a_ref[...]) * _softplus(psm_ + t_ref[...])
        beta = _sigmoid(psm_)
        gb = jnp.where(lane < GDN_H, g, jnp.where(lane < 2 * GDN_H, beta, 0.0))
        gb_ref[...] = gb * (rowi >= PAD).astype(f32)
        logit = _mmb(psm_, w_ref[...], NN) + b_ref[...]
        la_ref[...] = _log_sigmoid(logit) * (1.0 / GATE_NORMALIZER)

    row = pl.BlockSpec((tr, SM_W), lambda i: (i, 0))
    return pl.pallas_call(
        body, name="gates", grid=(Lp // tr,),
        in_specs=[row, pl.BlockSpec((SM_W, GLA_QK), lambda i: (0, 0)), pl.BlockSpec((1, GLA_QK), lambda i: (0, 0)),
                  pl.BlockSpec((1, SM_W), lambda i: (0, 0)), pl.BlockSpec((1, SM_W), lambda i: (0, 0))],
        out_specs=[row, pl.BlockSpec((tr, GLA_QK), lambda i: (i, 0))],
        out_shape=[jax.ShapeDtypeStruct((Lp, SM_W), f32), jax.ShapeDtypeStruct((Lp, GLA_QK), f32)],
        compiler_params=_params("parallel"),
    )(psm, w2p, gate_b, alog, dtb)


def _gates_bwd(psm, w2p, gate_b, alog, dtb, dgb, dla):
    Lp = psm.shape[0]
    tr = _tile(Lp, 256)

    def body(p_ref, w_ref, b_ref, a_ref, t_ref, dgb_ref, dla_ref, dp_ref, gw_ref, gb_ref, ga_ref, gt_ref):
        i = pl.program_id(0)
        psm_ = p_ref[...]
        lane = lax.broadcasted_iota(jnp.int32, psm_.shape, 1)
        rowi = lax.broadcasted_iota(jnp.int32, (tr, 1), 0) + i * tr
        d = dgb_ref[...] * (rowi >= PAD).astype(f32)
        ea = jnp.exp(a_ref[...])
        z = psm_ + t_ref[...]
        is_g = lane < GDN_H
        dz = jnp.where(is_g, -ea * _sigmoid(z) * d, 0.0)
        dalog = jnp.where(is_g, -ea * _softplus(z) * d, 0.0)
        beta = _sigmoid(psm_)
        dbeta = jnp.where(jnp.logical_and(lane >= GDN_H, lane < 2 * GDN_H), beta * (1.0 - beta) * d, 0.0)
        logit = _mmb(psm_, w_ref[...], NN) + b_ref[...]
        dlogit = dla_ref[...] * (_sigmoid(-logit) * (1.0 / GATE_NORMALIZER))
        dlr = _mmb(dlogit, w_ref[...], NT)
        dp_ref[...] = (dz + dbeta + dlr).astype(bf16)
        gw = _mmb(psm_, dlogit, TN)
        gb = jnp.sum(dlogit, axis=0, keepdims=True)
        ga = jnp.sum(dalog, axis=0, keepdims=True)
        gt = jnp.sum(dz, axis=0, keepdims=True)

        @pl.when(i == 0)
        def _():
            gw_ref[...] = gw
            gb_ref[...] = gb
            ga_ref[...] = ga
            gt_ref[...] = gt

        @pl.when(i > 0)
        def _():
            gw_ref[...] += gw
            gb_ref[...] += gb
            ga_ref[...] += ga
            gt_ref[...] += gt

    row = pl.BlockSpec((tr, SM_W), lambda i: (i, 0))
    wsp = pl.BlockSpec((SM_W, GLA_QK), lambda i: (0, 0))
    bsp = pl.BlockSpec((1, GLA_QK), lambda i: (0, 0))
    vsp = pl.BlockSpec((1, SM_W), lambda i: (0, 0))
    return pl.pallas_call(
        body, name="gates_bwd", grid=(Lp // tr,),
        in_specs=[row, wsp, bsp, vsp, vsp, row, pl.BlockSpec((tr, GLA_QK), lambda i: (i, 0))],
        out_specs=[row, wsp, bsp, vsp, vsp],
        out_shape=[jax.ShapeDtypeStruct((Lp, SM_W), bf16), jax.ShapeDtypeStruct((SM_W, GLA_QK), f32),
                   jax.ShapeDtypeStruct((1, GLA_QK), f32), jax.ShapeDtypeStruct((1, SM_W), f32),
                   jax.ShapeDtypeStruct((1, SM_W), f32)],
        compiler_params=_params("arbitrary"),
    )(psm, w2p, gate_b, alog, dtb, dgb, dla)


def _conv_pre(x_ext, w, n):
    rows = x_ext.shape[0]
    y = x_ext * w[CONV_K - 1:CONV_K, :]
    for s in range(1, CONV_K):
        y = y + pltpu.roll(x_ext, s, 0) * w[CONV_K - 1 - s:CONV_K - s, :]
    return y[rows - n:, :]


def _conv(proj, cw, side=None):
    Lp = proj.shape[0]
    W = cw.shape[1]
    tr = _tile(Lp, 256, 64)
    tc = _tile(W, 1024, 128)
    c0 = C_QKV // tc

    def body(h_ref, x_ref, w_ref, o_ref):
        i = pl.program_id(1)
        halo = jnp.where(i == 0, 0.0, h_ref[...])
        x_ext = jnp.concatenate([halo, x_ref[...]], axis=0)
        o_ref[...] = _silu(_conv_pre(x_ext, w_ref[...], tr))

    out = _call(
        body, name="conv", grid=(W // tc, Lp // tr),
        in_specs=[pl.BlockSpec((8, tc), lambda j, i: (jnp.maximum(i * (tr // 8) - 1, 0), j + c0)),
                  pl.BlockSpec((tr, tc), lambda j, i: (i, j + c0)),
                  pl.BlockSpec((CONV_K, tc), lambda j, i: (0, j))],
        out_specs=[pl.BlockSpec((tr, tc), lambda j, i: (i, j))],
        out_shape=[jax.ShapeDtypeStruct((Lp, W), f32)], args=[proj, proj, cw],
        sem=("parallel", "parallel"), side=side)
    return out[0] if side is None else (out[0], out[1:])


def _conv_bwd(proj, cw, dy, dproj, side=None):
    Lp = proj.shape[0]
    W = cw.shape[1]
    tr = _tile(Lp, 256, 64)
    tc = _tile(W, 1024, 128)
    c0 = C_QKV // tc
    nr = Lp // tr
    last8 = Lp // 8 - 1

    def body(xp_ref, x_ref, xn_ref, w_ref, d_ref, dn_ref, dproj_ref, o_ref, gw_ref):
        del dproj_ref
        i = pl.program_id(1)
        w = w_ref[...]
        xp = jnp.where(i == 0, 0.0, xp_ref[...])
        x_ext = jnp.concatenate([xp, x_ref[...], xn_ref[...]], axis=0)
        n = tr + 8
        pre = _conv_pre(x_ext, w, n)
        dn = jnp.where(i == nr - 1, 0.0, dn_ref[...])
        dpre = jnp.concatenate([d_ref[...], dn], axis=0) * _dsilu(pre)
        dx = dpre * w[CONV_K - 1:CONV_K, :]
        for s in range(1, CONV_K):
            dx = dx + pltpu.roll(dpre, n - s, 0) * w[CONV_K - 1 - s:CONV_K - s, :]
        o_ref[...] = dx[:tr, :].astype(bf16)
        dp = dpre[:tr, :]
        rows = []
        for k in range(CONV_K):
            xs = x_ext if k == CONV_K - 1 else pltpu.roll(x_ext, CONV_K - 1 - k, 0)
            rows.append(jnp.sum(dp * xs[8:8 + tr, :], axis=0, keepdims=True))
        gw = jnp.concatenate(rows, axis=0)

        @pl.when(i == 0)
        def _():
            gw_ref[...] = gw

        @pl.when(i > 0)
        def _():
            gw_ref[...] += gw

    cur = pl.BlockSpec((tr, tc), lambda j, i: (i, j))
    nxt = pl.BlockSpec((8, tc), lambda j, i: (jnp.minimum((i + 1) * (tr // 8), last8), j))
    pcur = pl.BlockSpec((tr, tc), lambda j, i: (i, j + c0))
    pprev = pl.BlockSpec((8, tc), lambda j, i: (jnp.maximum(i * (tr // 8) - 1, 0), j + c0))
    pnext = pl.BlockSpec((8, tc), lambda j, i: (jnp.minimum((i + 1) * (tr // 8), last8), j + c0))
    wsp = pl.BlockSpec((CONV_K, tc), lambda j, i: (0, j))
    out = _call(
        body, name="conv_bwd", grid=(W // tc, nr),
        in_specs=[pprev, pcur, pnext, wsp, cur, nxt, _ANY], out_specs=[pcur, wsp],
        out_shape=[jax.ShapeDtypeStruct(dproj.shape, dproj.dtype), jax.ShapeDtypeStruct((CONV_K, W), f32)],
        aliases={6: 0}, args=[proj, proj, proj, cw, dy, dy, dproj], sem=("parallel", "arbitrary"), side=side)
    return out[0], out[1], out[2:]


def _gdn_heads(x_ref, gbv):
    R = range(GDN_H)
    return ([x_ref[:, Q0 + h * GDN_DK:Q0 + (h + 1) * GDN_DK] for h in R],
            [x_ref[:, K0 + h * GDN_DK:K0 + (h + 1) * GDN_DK] for h in R],
            [x_ref[:, V0 + h * GDN_DV:V0 + (h + 1) * GDN_DV] for h in R],
            [gbv[:, GDN_H + h:GDN_H + h + 1] for h in R],
            [gbv[:, h:h + 1] for h in R])


def _gdn_fwd(qkvc, gb, side=None):
    Lp = qkvc.shape[0]
    N = Lp // GDN_C

    def body(x_ref, gb_ref, o_ref, sall_ref, pall_ref, s_scr):
        @pl.when(pl.program_id(0) == 0)
        def _():
            s_scr[...] = jnp.zeros_like(s_scr)

        R = range(GDN_H)
        Ss = [s_scr[h] for h in R]
        for h in R:
            sall_ref[0, h] = Ss[h]
        S2, o, p = _gdn_chunk(Ss, *_gdn_heads(x_ref, gb_ref[...]))
        for h in R:
            s_scr[h] = S2[h]
            o_ref[:, h * GDN_DV:(h + 1) * GDN_DV] = o[h]
            pall_ref[0, h] = p[h]

    out = _call(
        body, name="gdn_fwd", grid=(N,),
        in_specs=[pl.BlockSpec((GDN_C, QKV_W), lambda n: (n, 0)), pl.BlockSpec((GDN_C, SM_W), lambda n: (n, 0))],
        out_specs=[pl.BlockSpec((GDN_C, GDN_V), lambda n: (n, 0)),
                   pl.BlockSpec((1, GDN_H, GDN_DK, GDN_DV), lambda n: (n, 0, 0, 0)),
                   pl.BlockSpec((1, GDN_H, GDN_C, GDN_C), lambda n: (n, 0, 0, 0))],
        out_shape=[jax.ShapeDtypeStruct((Lp, GDN_V), f32), jax.ShapeDtypeStruct((N, GDN_H, GDN_DK, GDN_DV), f32),
                   jax.ShapeDtypeStruct((N, GDN_H, GDN_C, GDN_C), f32)],
        scratch_shapes=[pltpu.VMEM((GDN_H, GDN_DK, GDN_DV), f32)], args=[qkvc, gb], sem=("arbitrary",), side=side)
    return out[0], out[1], out[2], out[3:]


def _gdn_bwd(qkvc, gb, sall, pall, do, side=None):
    Lp = qkvc.shape[0]
    N = Lp // GDN_C

    def body(x_ref, gb_ref, sall_ref, pall_ref, do_ref, dx_ref, dgb_ref, ds_scr):
        @pl.when(pl.program_id(0) == 0)
        def _():
            ds_scr[...] = jnp.zeros_like(ds_scr)

        R = range(GDN_H)
        lane = lax.broadcasted_iota(jnp.int32, (GDN_C, SM_W), 1)
        ps = [pall_ref[0, h] for h in R]
        _, vjp = jax.vjp(lambda *a: _gdn_chunk(*a, Ps=ps)[:2],
                         [sall_ref[0, h] for h in R], *_gdn_heads(x_ref, gb_ref[...]))
        dS, dq, dk, dv, dbeta, dg = vjp(([ds_scr[h] for h in R],
                                         [do_ref[:, h * GDN_DV:(h + 1) * GDN_DV] for h in R]))
        acc = jnp.zeros((GDN_C, SM_W), f32)
        for h in R:
            ds_scr[h] = dS[h]
            dx_ref[:, Q0 + h * GDN_DK:Q0 + (h + 1) * GDN_DK] = dq[h]
            dx_ref[:, K0 + h * GDN_DK:K0 + (h + 1) * GDN_DK] = dk[h]
            dx_ref[:, V0 + h * GDN_DV:V0 + (h + 1) * GDN_DV] = dv[h]
            acc = acc + jnp.where(lane == h, dg[h], 0.0) + jnp.where(lane == GDN_H + h, dbeta[h], 0.0)
        dgb_ref[...] = acc

    rev = lambda n: (N - 1 - n, 0)
    out = _call(
        body, name="gdn_bwd", grid=(N,),
        in_specs=[pl.BlockSpec((GDN_C, QKV_W), rev), pl.BlockSpec((GDN_C, SM_W), rev),
                  pl.BlockSpec((1, GDN_H, GDN_DK, GDN_DV), lambda n: (N - 1 - n, 0, 0, 0)),
                  pl.BlockSpec((1, GDN_H, GDN_C, GDN_C), lambda n: (N - 1 - n, 0, 0, 0)),
                  pl.BlockSpec((GDN_C, GDN_V), rev)],
        out_specs=[pl.BlockSpec((GDN_C, QKV_W), rev), pl.BlockSpec((GDN_C, SM_W), rev)],
        out_shape=[jax.ShapeDtypeStruct((Lp, QKV_W), f32), jax.ShapeDtypeStruct((Lp, SM_W), f32)],
        scratch_shapes=[pltpu.VMEM((GDN_H, GDN_DK, GDN_DV), f32)], args=[qkvc, gb, sall, pall, do],
        sem=("arbitrary",), side=side)
    return out[0], out[1], out[2:]


GLA_BLOCK = 64


def _gla_group(Lp):
    nb = Lp // GLA_BLOCK
    return next(g for g in (3, 2, 1) if nb % g == 0)


def _gla_slices(h):
    sq = slice(h * GLA_DK, (h + 1) * GLA_DK)
    sk = slice(GLA_QK + h * GLA_DK, GLA_QK + (h + 1) * GLA_DK)
    sv = slice(2 * GLA_QK + h * GLA_DV, 2 * GLA_QK + (h + 1) * GLA_DV)
    return sq, sk, sv


def _gla_heads(x_ref, la_ref, group):
    qs, ks, vs, ls = [], [], [], []
    for blk in range(group):
        r = slice(blk * GLA_BLOCK, (blk + 1) * GLA_BLOCK)
        for h in range(GLA_H):
            sq, sk, sv = _gla_slices(h)
            qs.append(x_ref[r, sq])
            ks.append(x_ref[r, sk])
            vs.append(x_ref[r, sv])
            ls.append(la_ref[r, sq])
    return qs, ks, vs, ls


def _gla_fwd(proj, la):
    Lp = proj.shape[0]
    group = _gla_group(Lp)
    rows = group * GLA_BLOCK
    steps = Lp // rows
    R = range(GLA_H)

    def body(x_ref, la_ref, o_ref, sall_ref, s_scr):
        @pl.when(pl.program_id(0) == 0)
        def _():
            s_scr[...] = jnp.zeros_like(s_scr)

        Sts = [s_scr[h] for h in R]
        for h in R:
            sall_ref[0, h] = Sts[h]
        St2, o = _gla_blocks(Sts, *_gla_heads(x_ref, la_ref, group))
        for h in R:
            s_scr[h] = St2[h]
        for blk in range(group):
            for h in R:
                o_ref[blk * GLA_BLOCK:(blk + 1) * GLA_BLOCK, h * GLA_DV:(h + 1) * GLA_DV] = o[blk * GLA_H + h]

    return pl.pallas_call(
        body, name="gla_fwd", grid=(steps,),
        in_specs=[pl.BlockSpec((rows, G_W), lambda n: (n, C_G // G_W)),
                  pl.BlockSpec((rows, GLA_QK), lambda n: (n, 0))],
        out_specs=[pl.BlockSpec((rows, GLA_V), lambda n: (n, 0)),
                   pl.BlockSpec((1, GLA_H, GLA_DV, GLA_DK), lambda n: (n, 0, 0, 0))],
        out_shape=[jax.ShapeDtypeStruct((Lp, GLA_V), f32),
                   jax.ShapeDtypeStruct((steps, GLA_H, GLA_DV, GLA_DK), f32)],
        scratch_shapes=[pltpu.VMEM((GLA_H, GLA_DV, GLA_DK), f32)],
        compiler_params=_params("arbitrary"),
    )(proj, la)


def _gla_bwd(proj, la, sall, do, dproj, side=None):
    Lp = proj.shape[0]
    group = _gla_group(Lp)
    rows = group * GLA_BLOCK
    steps = Lp // rows
    R = range(GLA_H)

    def body(x_ref, la_ref, sall_ref, do_ref, dproj_ref, dx_ref, dla_ref, ds_scr):
        del dproj_ref

        @pl.when(pl.program_id(0) == 0)
        def _():
            ds_scr[...] = jnp.zeros_like(ds_scr)

        _, vjp = jax.vjp(_gla_blocks, [sall_ref[0, h] for h in R], *_gla_heads(x_ref, la_ref, group))
        do = [do_ref[blk * GLA_BLOCK:(blk + 1) * GLA_BLOCK, h * GLA_DV:(h + 1) * GLA_DV]
              for blk in range(group) for h in R]
        dS, dq, dk, dv, dl = vjp(([ds_scr[h] for h in R], do))
        for h in R:
            ds_scr[h] = dS[h]
        for blk in range(group):
            r = slice(blk * GLA_BLOCK, (blk + 1) * GLA_BLOCK)
            for h in R:
                sq, sk, sv = _gla_slices(h)
                i = blk * GLA_H + h
                dx_ref[r, sq] = dq[i].astype(bf16)
                dx_ref[r, sk] = dk[i].astype(bf16)
                dx_ref[r, sv] = dv[i].astype(bf16)
                dla_ref[r, sq] = dl[i]

    x_spec = pl.BlockSpec((rows, G_W), lambda n: (steps - 1 - n, C_G // G_W))
    rev = lambda n: (steps - 1 - n, 0)
    out = _call(
        body, name="gla_bwd", grid=(steps,),
        in_specs=[x_spec, pl.BlockSpec((rows, GLA_QK), rev),
                  pl.BlockSpec((1, GLA_H, GLA_DV, GLA_DK), lambda n: (steps - 1 - n, 0, 0, 0)),
                  pl.BlockSpec((rows, GLA_V), rev), _ANY],
        out_specs=[x_spec, pl.BlockSpec((rows, GLA_QK), rev)],
        out_shape=[jax.ShapeDtypeStruct(dproj.shape, dproj.dtype), jax.ShapeDtypeStruct((Lp, GLA_QK), f32)],
        aliases={4: 0}, scratch_shapes=[pltpu.VMEM((GLA_H, GLA_DV, GLA_DK), f32)],
        args=[proj, la, sall, do, dproj], sem=("arbitrary",), side=side)
    return out[0], out[1], out[2:]


def _gated_norm_fn(og, ol, zr, wg, wl):
    outs = []
    for h in range(GDN_H):
        s = slice(h * GDN_DV, (h + 1) * GDN_DV)
        outs.append(_rms(og[:, s])[0] * wg * _silu(zr[:, s]))
    for h in range(GLA_H):
        s = slice(h * GLA_DV, (h + 1) * GLA_DV)
        sr = slice(GDN_V + h * GLA_DV, GDN_V + (h + 1) * GLA_DV)
        outs.append(_rms(ol[:, s])[0] * wl * _silu(zr[:, sr]))
    return jnp.concatenate(outs, axis=-1)


def _gated_norm(og, ol, proj, wg, wl):
    Lp = og.shape[0]
    tr = _tile(Lp, 256)

    def body(og_ref, ol_ref, zr_ref, wg_ref, wl_ref, o_ref):
        o_ref[...] = _gated_norm_fn(og_ref[...], ol_ref[...], zr_ref[...], wg_ref[...], wl_ref[...]).astype(bf16)

    return pl.pallas_call(
        body, name="gated_norm", grid=(Lp // tr,),
        in_specs=[pl.BlockSpec((tr, GDN_V), lambda i: (i, 0)), pl.BlockSpec((tr, GLA_V), lambda i: (i, 0)),
                  pl.BlockSpec((tr, ZR_W), lambda i: (i, C_ZR // ZR_W)),
                  pl.BlockSpec((1, GDN_DV), lambda i: (0, 0)), pl.BlockSpec((1, GLA_DV), lambda i: (0, 0))],
        out_specs=pl.BlockSpec((tr, ZR_W), lambda i: (i, 0)),
        out_shape=jax.ShapeDtypeStruct((Lp, ZR_W), bf16),
        compiler_params=_params("parallel"),
    )(og, ol, proj, wg, wl)


def _gated_norm_bwd(og, ol, proj, wg, wl, dmix):
    Lp = og.shape[0]
    tr = _tile(Lp, 128)

    def body(og_ref, ol_ref, zr_ref, wg_ref, wl_ref, d_ref, dog_ref, dol_ref, dzr_ref, gwg_ref, gwl_ref):
        i = pl.program_id(0)
        _, vjp = jax.vjp(_gated_norm_fn, og_ref[...], ol_ref[...], zr_ref[...], wg_ref[...], wl_ref[...])
        dog, dol, dzr, gwg, gwl = vjp(d_ref[...])
        dog_ref[...] = dog
        dol_ref[...] = dol
        dzr_ref[...] = dzr.astype(bf16)

        @pl.when(i == 0)
        def _():
            gwg_ref[...] = gwg
            gwl_ref[...] = gwl

        @pl.when(i > 0)
        def _():
            gwg_ref[...] += gwg
            gwl_ref[...] += gwl

    og_spec = pl.BlockSpec((tr, GDN_V), lambda i: (i, 0))
    ol_spec = pl.BlockSpec((tr, GLA_V), lambda i: (i, 0))
    zr_spec = pl.BlockSpec((tr, ZR_W), lambda i: (i, C_ZR // ZR_W))
    vg = pl.BlockSpec((1, GDN_DV), lambda i: (0, 0))
    vl = pl.BlockSpec((1, GLA_DV), lambda i: (0, 0))
    return pl.pallas_call(
        body, name="gated_norm_bwd", grid=(Lp // tr,),
        in_specs=[og_spec, ol_spec, zr_spec, vg, vl, pl.BlockSpec((tr, ZR_W), lambda i: (i, 0))],
        out_specs=[og_spec, ol_spec, zr_spec, vg, vl],
        out_shape=[jax.ShapeDtypeStruct((Lp, GDN_V), f32), jax.ShapeDtypeStruct((Lp, GLA_V), f32),
                   jax.ShapeDtypeStruct((Lp, C_END), bf16),
                   jax.ShapeDtypeStruct((1, GDN_DV), f32), jax.ShapeDtypeStruct((1, GLA_DV), f32)],
        compiler_params=_params("arbitrary"),
    )(og, ol, proj, wg, wl, dmix)


def _adamw(g, w, m, v, name):
    R, C = g.shape
    tr = _tile(R, 256, 8) if R % 8 == 0 and R > 256 else R
    c1 = 1.0 - ADAM_B1 ** ADAM_STEP
    c2 = 1.0 - ADAM_B2 ** ADAM_STEP

    def body(g_ref, w_ref, m_ref, v_ref, d_ref, mo_ref, vo_ref):
        g_ = g_ref[...]
        m2 = ADAM_B1 * m_ref[...] + (1.0 - ADAM_B1) * g_
        v2 = ADAM_B2 * v_ref[...] + (1.0 - ADAM_B2) * (g_ * g_)
        mo_ref[...] = m2
        vo_ref[...] = v2
        d_ref[...] = -ADAM_LR * ((m2 / c1) / (jnp.sqrt(v2 / c2) + ADAM_EPS) + ADAM_WD * w_ref[...])

    blk = pl.BlockSpec((tr, C), lambda i: (i, 0))
    return pl.pallas_call(
        body, name=name, grid=(R // tr,), in_specs=[blk] * 4, out_specs=[blk] * 3,
        out_shape=[jax.ShapeDtypeStruct((R, C), f32)] * 3,
        compiler_params=_params("parallel"),
    )(g, w, m, v)


def _sum_slots(r, name):
    n, R, C = r.shape
    tr = _tile(R, 128, 16) if R % 16 == 0 and R > 128 else R

    def body(r_ref, o_ref):
        acc = r_ref[0].astype(f32)
        for s in range(1, n):
            acc = acc + r_ref[s].astype(f32)
        o_ref[...] = acc

    return pl.pallas_call(
        body, name=name, grid=(R // tr,),
        in_specs=[pl.BlockSpec((n, tr, C), lambda i: (0, i, 0))],
        out_specs=pl.BlockSpec((tr, C), lambda i: (i, 0)),
        out_shape=jax.ShapeDtypeStruct((R, C), f32),
        compiler_params=_params("parallel"),
    )(r)


class _Siblings:
    def __init__(self, arrays):
        self.arrays = list(arrays)
        self.n = len(self.arrays)
        self.out_shape = [jax.ShapeDtypeStruct((2,) + a.shape, a.dtype) for a in self.arrays]
        self.sems = [pltpu.SemaphoreType.DMA((self.n,)), pltpu.SemaphoreType.DMA((self.n,)),
                     pltpu.SemaphoreType.DMA((self.n,))]

    def hooks(self, ins, outs, send, recv, lsem):
        def copies():
            x, y, c = lax.axis_index("x"), lax.axis_index("y"), lax.axis_index("c")
            out = []
            for a in range(self.n):
                out.append((pltpu.make_async_copy(ins[a], outs[a].at[c], lsem.at[a]), None))
                mk = lambda dst, a=a: pltpu.make_async_remote_copy(
                    src_ref=ins[a], dst_ref=dst, send_sem=send.at[a], recv_sem=recv.at[a],
                    device_id=(x, y, 1 - c), device_id_type=MESH)
                out.append((mk(outs[a].at[c]), mk(outs[a].at[1 - c])))
            return out

        return _start_wait(copies)


def _comm_now(name, sides):
    total = sum(s.n for s in sides)

    def body(*refs):
        ins, outs, sems = refs[:total], refs[total:2 * total], refs[2 * total:]
        hooks, o = [], 0
        for i, s in enumerate(sides):
            hooks.append(s.hooks(ins[o:o + s.n], outs[o:o + s.n], *sems[3 * i:3 * i + 3]))
            o += s.n
        for start, _ in hooks:
            start()
        for _, wait in hooks:
            wait()

    out = pl.pallas_call(
        body, name=name, in_specs=[_ANY] * total, out_specs=[_ANY] * total,
        out_shape=[sh for s in sides for sh in s.out_shape], scratch_shapes=[sm for s in sides for sm in s.sems],
    )(*[a for s in sides for a in s.arrays])
    res, o = [], 0
    for s in sides:
        res.append(list(out[o:o + s.n]))
        o += s.n
    return res


def _cat_cols(g):
    return jnp.concatenate([g[i] for i in range(N_CHIP)], axis=-1)


def _row_slabs(a):
    return a.reshape(N_DEV, a.shape[0] // N_DEV, a.shape[1])


def _w_in_columns(g_wp, g_wsm):
    return jnp.concatenate([g_wp[:, C_QKV:C_END], g_wp[:, C_ZR:C_ZR + GDN_V], g_wsm[:, :SM_LR],
                            g_wp[:, C_G:C_G + G_W], g_wp[:, C_ZR + GDN_V:C_ZR + ZR_W],
                            g_wsm[:, SM_LR:SM_LR + GATE_RANK]], axis=1)


def _step(x, loss_target, p, meta, shard):
    _, S, D = x.shape
    alog_p = jnp.pad(p["gdn_a_log"], ((0, 0), (0, SM_W - GDN_H)))
    dtb_p = jnp.pad(p["gdn_dt_bias"], ((0, 0), (0, SM_W - GDN_H)))
    m64 = jnp.concatenate([jnp.zeros((PAD, D), f32), meta], axis=0)
    gate_b, gdn_norm_w, gla_norm_w = p["gla_gate_b"], p["gdn_norm_w"], p["gla_norm_w"]
    half = shard["w_up"].shape[0] // 2

    h0, n1, (w_in4, conv4, w24) = _embed_norm(
        x, m64, p["attn_norm_w"], side=_Gather([shard["w_in"], shard["gdn_conv_w"], shard["gla_gate_w2"]]))
    w_in, conv_w, w2 = _cat_cols(w_in4), _cat_cols(conv4), _cat_cols(w24)
    wp = jnp.concatenate([w_in[:, R_Z:R_AB], w_in[:, R_GR:R_LR], w_in[:, R_G:R_GR], w_in[:, R_QKV:R_Z]], axis=1)
    wsm = jnp.concatenate([w_in[:, R_AB:R_G], w_in[:, R_LR:R_END],
                           jnp.zeros((D, SM_W - SM_LR - GATE_RANK), w_in.dtype)], axis=1)
    w2p = jnp.pad(w2, ((SM_LR, SM_W - SM_LR - GATE_RANK), (0, 0)))
    proj, (w_out4, w_up4a) = _mm(n1, wp, "nn", "proj", side=_Gather([shard["w_out"], shard["w_up"][:half]]))
    w_out = w_out4.reshape(-1, D)
    psm = _mm(n1, wsm, "nn", "proj_small")
    gb, la = _gates(psm, w2p, gate_b, alog_p, dtb_p)
    qkvc, (w_up4b,) = _conv(proj, conv_w, side=_Gather([shard["w_up"][half:]]))
    w_up = jnp.concatenate([_cat_cols(w_up4a), _cat_cols(w_up4b)], axis=0)
    og, sall, pall, (w_gate4,) = _gdn_fwd(qkvc, gb, side=_Gather([shard["w_gate"]]))
    w_gate = _cat_cols(w_gate4)
    ol, stall = _gla_fwd(proj, la)
    mixed = _gated_norm(og, ol, proj, gdn_norm_w, gla_norm_w)
    attn = _mm(mixed, w_out, "nn", "out_proj")
    h1, n2 = _add_norm(h0, attn, p["ffn_norm_w"])
    act, gate, up, (w_down4,) = _ffn_in(n2, w_gate, w_up, side=_Gather([shard["w_down"]]))
    w_down = w_down4.reshape(-1, D)
    ffn = _mm(act, w_down, "nn", "ffn_down", **WHOLE_K)
    dh2, dh2b, lossp, g_final = _final(h1, ffn, loss_target, p["final_norm_w"])

    g_down = _mm(act, dh2b, "tn", "g_w_down", out_dtype=bf16, **WHOLE_K_T)
    dg, du = _ffn_dact(dh2b, w_down, gate, up)
    g_gate = _mm(n2, dg, "tn", "g_w_gate", tm_cap=512, tn_cap=1408, tk_cap=2752, out_dtype=bf16, col_slabs=True)
    g_up = _mm(n2, du, "tn", "g_w_up", tm_cap=512, tn_cap=1408, tk_cap=2752, out_dtype=bf16, col_slabs=True)
    dn2 = _mm(dg, w_gate, "nt", "d_n2_gate", **WHOLE_K)
    dn2 = _mm(du, w_up, "nt", "d_n2_up", acc_in=dn2, **WHOLE_K)
    dh1, dh1b, g_ffn_norm = _norm_bwd(dn2, h1, dh2, p["ffn_norm_w"])
    dmix = _mm(dh1b, w_out, "nt", "d_mixed")
    g_out = _mm(mixed, dh1b, "tn", "g_w_out", out_dtype=bf16, **WHOLE_K_T)
    dog, dol, dproj, g_gdn_norm, g_gla_norm = _gated_norm_bwd(og, ol, proj, gdn_norm_w, gla_norm_w, dmix)
    dproj, dla, (r_down,) = _gla_bwd(proj, la, stall, dol, dproj, side=_Exchange([_row_slabs(g_down)]))
    dqkvc, dgb, (r_gate, r_up, r_out) = _gdn_bwd(qkvc, gb, sall, pall, dog,
                                                 side=_Exchange([g_gate, g_up, _row_slabs(g_out)]))
    dproj, g_conv, (h_out, h_down) = _conv_bwd(
        proj, conv_w, dqkvc, dproj, side=_Siblings([_sum_slots(r_out, "sum_w_out"), _sum_slots(r_down, "sum_w_down")]))
    dpsm, g_w2p, g_gate_b, g_alog, g_dtb = _gates_bwd(psm, w2p, gate_b, alog_p, dtb_p, dgb, dla)
    g_wp, (h_gate, h_up) = _mm(
        n1, dproj, "tn", "g_w_in", out_dtype=bf16,
        side=_Siblings([_sum_slots(r_gate, "sum_w_gate"), _sum_slots(r_up, "sum_w_up")]), **WHOLE_K_T)
    g_wsm = _mm(n1, dpsm, "tn", "g_w_in_small", out_dtype=bf16, **WHOLE_K_T)
    dn1, r_in = _mm(dproj, wp, "nt", "d_n1", side=_Exchange([_row_slabs(g_wp), _row_slabs(g_wsm)]), **WHOLE_K)
    dn1 = _mm(dpsm, wsm, "nt", "d_n1_small", acc_in=dn1)
    s_in = _w_in_columns(_sum_slots(r_in[0], "sum_w_in"), _sum_slots(r_in[1], "sum_w_in_small"))
    in_by_chip = s_in.reshape(s_in.shape[0], N_CHIP, -1).transpose(1, 0, 2)
    grad_x, g_meta, g_attn_norm, (h_in,) = _embed_norm_bwd(dn1, h0, dh1, p["attn_norm_w"], S,
                                                           side=_Exchange([], by_chip=[in_by_chip]))

    received = dict(w_in=h_in, w_gate=h_gate, w_up=h_up, w_out=h_out, w_down=h_down)
    small = dict(
        meta_tokens=g_meta, attn_norm_w=g_attn_norm, gdn_conv_w=g_conv, gdn_a_log=g_alog[:, :GDN_H],
        gdn_dt_bias=g_dtb[:, :GDN_H], gdn_norm_w=g_gdn_norm, gla_gate_w2=g_w2p[SM_LR:SM_LR + GATE_RANK],
        gla_gate_b=g_gate_b, gla_norm_w=g_gla_norm, ffn_norm_w=g_ffn_norm, final_norm_w=g_final)
    return lossp[0, 0], grad_x, received, small


_WEIGHTS = ("meta_tokens", "attn_norm_w", "w_in", "gdn_conv_w", "gdn_a_log", "gdn_dt_bias", "gdn_norm_w",
            "gla_gate_w2", "gla_gate_b", "gla_norm_w", "w_out", "ffn_norm_w", "w_gate", "w_up", "w_down",
            "final_norm_w")
_BIG_COLS = ("w_in", "w_gate", "w_up")
_BIG_ROWS = ("w_out", "w_down")
_SMALL_SHARDED = ("meta_tokens", "gdn_conv_w", "gla_gate_w2")


def kernel(x, meta_tokens, attn_norm_w, w_in, gdn_conv_w, gdn_a_log, gdn_dt_bias, gdn_norm_w, gla_gate_w2, gla_gate_b, gla_norm_w, w_out, ffn_norm_w, w_gate, w_up, w_down, final_norm_w, loss_target, m_meta_tokens, m_attn_norm_w, m_w_in, m_gdn_conv_w, m_gdn_a_log, m_gdn_dt_bias, m_gdn_norm_w, m_gla_gate_w2, m_gla_gate_b, m_gla_norm_w, m_w_out, m_ffn_norm_w, m_w_gate, m_w_up, m_w_down, m_final_norm_w, v_meta_tokens, v_attn_norm_w, v_w_in, v_gdn_conv_w, v_gdn_a_log, v_gdn_dt_bias, v_gdn_norm_w, v_gla_gate_w2, v_gla_gate_b, v_gla_norm_w, v_w_out, v_ffn_norm_w, v_w_gate, v_w_up, v_w_down, v_final_norm_w):
    w = dict(meta_tokens=meta_tokens, attn_norm_w=attn_norm_w, w_in=w_in, gdn_conv_w=gdn_conv_w, gdn_a_log=gdn_a_log,
             gdn_dt_bias=gdn_dt_bias, gdn_norm_w=gdn_norm_w, gla_gate_w2=gla_gate_w2, gla_gate_b=gla_gate_b,
             gla_norm_w=gla_norm_w, w_out=w_out, ffn_norm_w=ffn_norm_w, w_gate=w_gate, w_up=w_up, w_down=w_down,
             final_norm_w=final_norm_w)
    m = dict(meta_tokens=m_meta_tokens, attn_norm_w=m_attn_norm_w, w_in=m_w_in, gdn_conv_w=m_gdn_conv_w,
             gdn_a_log=m_gdn_a_log, gdn_dt_bias=m_gdn_dt_bias, gdn_norm_w=m_gdn_norm_w, gla_gate_w2=m_gla_gate_w2,
             gla_gate_b=m_gla_gate_b, gla_norm_w=m_gla_norm_w, w_out=m_w_out, ffn_norm_w=m_ffn_norm_w,
             w_gate=m_w_gate, w_up=m_w_up, w_down=m_w_down, final_norm_w=m_final_norm_w)
    v = dict(meta_tokens=v_meta_tokens, attn_norm_w=v_attn_norm_w, w_in=v_w_in, gdn_conv_w=v_gdn_conv_w,
             gdn_a_log=v_gdn_a_log, gdn_dt_bias=v_gdn_dt_bias, gdn_norm_w=v_gdn_norm_w, gla_gate_w2=v_gla_gate_w2,
             gla_gate_b=v_gla_gate_b, gla_norm_w=v_gla_norm_w, w_out=v_w_out, ffn_norm_w=v_ffn_norm_w,
             w_gate=v_w_gate, w_up=v_w_up, w_down=v_w_down, final_norm_w=v_final_norm_w)
    chip = 2 * lax.axis_index("x") + lax.axis_index("y")

    def two_d(a):
        return a.reshape(1, -1) if a.ndim == 1 else a.reshape(-1, a.shape[-1])

    w2d = {k: two_d(a) for k, a in w.items()}
    big = _BIG_COLS + _BIG_ROWS
    small = tuple(k for k in _WEIGHTS if k not in big)

    (meta4,), = _comm_now("gather_meta", [_Gather([w2d["meta_tokens"]])])
    shard = {k: w2d[k].astype(bf16) for k in big}
    shard.update({k: w2d[k] for k in ("gdn_conv_w", "gla_gate_w2")})
    lossp, grad_x, received, g = _step(x, loss_target, {k: w2d[k] for k in small}, _cat_cols(meta4), shard)
    loss = lax.psum(lossp, ("x", "y", "c"))

    sizes = [g[k].size for k in small]
    total = sum(sizes)
    rows = -(-total // 1024)
    rows += (-rows) % 8
    packed = jnp.concatenate([g[k].reshape(-1) for k in small] + [jnp.zeros((rows * 1024 - total,), f32)])
    (packed8,), = _comm_now("exchange_small", [_Exchange([], [packed.reshape(rows, 1024)])])
    red = {k: h.reshape(w2d[k].shape) for k, h in received.items()}
    psum_small = _sum_slots(packed8, "sum_small").reshape(-1)
    off = 0
    for k, n in zip(small, sizes):
        a = psum_small[off:off + n].reshape(g[k].shape)
        off += n
        if k in _SMALL_SHARDED:
            c = w2d[k].shape[1]
            a = lax.dynamic_slice_in_dim(a, chip * c, c, axis=1)
        red[k] = a

    grads, deltas, new_m, new_v = [], [], [], []
    for k in _WEIGHTS:
        d, m2, v2 = _adamw(red[k], w2d[k], two_d(m[k]), two_d(v[k]), "adamw_" + k)
        shape = w[k].shape
        grads.append(red[k].reshape(shape))
        deltas.append(d.reshape(shape))
        new_m.append(m2.reshape(shape))
        new_v.append(v2.reshape(shape))
    return (loss, grad_x, *grads, *deltas, *new_m, *new_v)
```

```python
import functools

import jax
import jax.numpy as jnp
from jax import lax
from jax.experimental import pallas as pl
from jax.experimental.pallas import tpu as pltpu

f32 = jnp.float32
bf16 = jnp.bfloat16
HIGH = lax.Precision.HIGH
MESH = pl.DeviceIdType.MESH

N_META = 16
CONV_K = 4
GDN_H, GDN_DK, GDN_DV, GDN_C = 8, 128, 128, 64
GLA_H, GLA_DK, GLA_DV, GLA_C = 4, 128, 256, 16
GATE_RANK = 16
GATE_NORMALIZER = 16.0
EPS = 1e-6
GDN_QK = GDN_H * GDN_DK
GDN_V = GDN_H * GDN_DV
GLA_QK = GLA_H * GLA_DK
GLA_V = GLA_H * GLA_DV
PAD = (-N_META) % GDN_C
OFF = PAD + N_META
ROWS = 64

R_QKV, R_Z, R_AB, R_G, R_GR, R_LR, R_END = 0, 3072, 4096, 4112, 6160, 7184, 7200
C_ZR, C_G, C_QKV, C_END = 0, 2048, 4096, 7168
ZR_W = GDN_V + GLA_V
G_W = 2 * GLA_QK + GLA_V
QKV_W = 2 * GDN_QK + GDN_V
Q0, K0, V0 = 0, GDN_QK, 2 * GDN_QK
SM_W = 128
SM_LR = 2 * GDN_H

ADAM_LR, ADAM_B1, ADAM_B2, ADAM_EPS, ADAM_WD, ADAM_STEP = 0.001, 0.9, 0.999, 1e-08, 0.01, 10

VMEM_LIMIT_V7X = 56 * 1024 * 1024
N_DEV = 8
N_CHIP = 4


def _params(*sem):
    return pltpu.CompilerParams(dimension_semantics=sem, vmem_limit_bytes=VMEM_LIMIT_V7X)


def _tile(n, cap, mult=16):
    best = None
    for d in range(mult, min(n, cap) + 1, mult):
        if n % d == 0:
            best = d
    assert best is not None, (n, cap, mult)
    return best


NN = ((1,), (0,))
NT = ((1,), (1,))
TN = ((0,), (0,))


def _dot(a, b, dims, prec=None):
    return lax.dot_general(a, b, (dims, ((), ())), precision=prec, preferred_element_type=f32)


def _mmb(a, b, dims):
    return _dot(a.astype(bf16), b.astype(bf16), dims)


def _sigmoid(x):
    return jax.nn.sigmoid(x)


def _silu(x):
    return x * _sigmoid(x)


def _dsilu(x):
    s = _sigmoid(x)
    return s * (1.0 + x * (1.0 - s))


def _log1p_exp_neg_abs(x):
    t = jnp.exp(-jnp.abs(x))
    u = 1.0 + t
    d = u - 1.0
    return jnp.where(d == 0.0, t, jnp.log(u) * (t / jnp.where(d == 0.0, 1.0, d)))


def _softplus(x):
    return jnp.maximum(x, 0.0) + _log1p_exp_neg_abs(x)


def _log_sigmoid(x):
    return jnp.minimum(x, 0.0) - _log1p_exp_neg_abs(x)


def _rms(x):
    r = lax.rsqrt(jnp.mean(x * x, axis=-1, keepdims=True) + EPS)
    return x * r, r


def _rms_bwd(dy, xh, r, w):
    t = dy * w
    return r * (t - xh * jnp.mean(t * xh, axis=-1, keepdims=True))


def _l2n(x):
    return x * lax.rsqrt(jnp.sum(x * x, axis=-1, keepdims=True) + EPS)


INV_LEAF = 8


def _same_block(C, b):
    sh = b.bit_length() - 1
    row = lax.broadcasted_iota(jnp.int32, (C, C), 0)
    col = lax.broadcasted_iota(jnp.int32, (C, C), 1)
    return lax.shift_right_logical(row, sh) == lax.shift_right_logical(col, sh)


def _tri_inv_impl(As):
    C = As[0].shape[0]
    R = range(len(As))
    row = lax.broadcasted_iota(jnp.int32, (C, C), 0)
    col = lax.broadcasted_iota(jnp.int32, (C, C), 1)
    eye = (row == col).astype(f32)
    b = INV_LEAF
    inner = _same_block(C, b)
    leaf = [jnp.where(inner, As[h], 0.0) for h in R]
    d = [eye - leaf[h] for h in R]
    pw = leaf
    n = 2
    while n < b:
        pw = [_dot(pw[h], pw[h], NN, HIGH) for h in R]
        d = [_dot(d[h], eye + pw[h], NN, HIGH) for h in R]
        n *= 2
    while b < C:
        outer = _same_block(C, 2 * b)
        level = jnp.logical_and(outer, jnp.logical_not(inner))
        ed = [_dot(jnp.where(level, As[h], 0.0), d[h], NN, HIGH) for h in R]
        d = [d[h] - _dot(d[h], ed[h], NN, HIGH) for h in R]
        inner = outer
        b *= 2
    return d


@jax.custom_vjp
def _tri_inv(As):
    return _tri_inv_impl(As)


def _tri_inv_fwd(As):
    d = _tri_inv_impl(As)
    return d, d


def _tri_inv_bwd(d, g):
    R = range(len(d))
    t = [_dot(d[h], g[h], TN, HIGH) for h in R]
    return ([-_dot(t[h], d[h], NT, HIGH) for h in R],)


_tri_inv.defvjp(_tri_inv_fwd, _tri_inv_bwd)


@jax.custom_vjp
def _tri_inv_known(As, Ps):
    del As
    return Ps


def _tri_inv_known_fwd(As, Ps):
    del As
    return Ps, Ps


def _tri_inv_known_bwd(d, g):
    return _tri_inv_bwd(d, g)[0], [jnp.zeros_like(x) for x in d]


_tri_inv_known.defvjp(_tri_inv_known_fwd, _tri_inv_known_bwd)


def _gdn_chunk(Ss, qrs, krs, vs, betas, gs, Ps=None):
    H = len(Ss)
    C, dk = qrs[0].shape
    R = range(len(qrs))
    row = lax.broadcasted_iota(jnp.int32, (C, C), 0)
    col = lax.broadcasted_iota(jnp.int32, (C, C), 1)
    causal = row >= col
    strict = row > col
    cf = causal.astype(f32)
    q = [_l2n(qrs[h]) * (dk ** -0.5) for h in R]
    k = [_l2n(krs[h]) for h in R]
    mc = [_rows_exact(cf, jnp.broadcast_to(gs[h], (C, C))) for h in R]
    gc = [mc[h][:, 0:1] for h in R]
    decay = [jnp.where(causal, jnp.exp(jnp.where(causal, mc[h] - mc[h].T, 0.0)), 0.0) for h in R]
    kb = [k[h] * betas[h] for h in R]
    a = [jnp.where(strict, _mmb(kb[h], k[h], NT) * decay[h], 0.0) for h in R]
    p = _tri_inv(a) if Ps is None else _tri_inv_known(a, Ps)
    egc = [jnp.exp(gc[h]) for h in R]
    u = [_mmb(p[h], vs[h] * betas[h], NN) for h in R]
    w = [_mmb(p[h], kb[h] * egc[h], NN) for h in R]
    qk = [jnp.where(causal, _mmb(q[h], k[h], NT) * decay[h], 0.0) for h in R]
    qe = [q[h] * egc[h] for h in R]
    gl = [gc[h][C - 1:C, :] for h in R]
    kd = [k[h] * jnp.exp(gl[h] - gc[h]) for h in R]
    egl = [jnp.exp(gl[h]) for h in R]
    S, o, entering = list(Ss), [], []
    for chunk in range(len(qrs) // H):
        idx = [chunk * H + h for h in range(H)]
        entering += S
        v_new = [u[i] - _mmb(w[i], S[h], NN) for h, i in enumerate(idx)]
        o += [_mmb(qe[i], S[h], NN) + _mmb(qk[i], v_new[h], NN) for h, i in enumerate(idx)]
        S = [S[h] * egl[i] + _mmb(kd[i], v_new[h], TN) for h, i in enumerate(idx)]
    return S, o, p, entering


def _rows_exact_impl(m01, x, dims):
    m = m01.astype(bf16)
    x1 = x.astype(bf16)
    r1 = x - x1.astype(f32)
    x2 = r1.astype(bf16)
    x3 = (r1 - x2.astype(f32)).astype(bf16)
    d = lambda y: _dot(m, y, dims)
    return d(x1) + (d(x2) + d(x3))


@jax.custom_vjp
def _rows_exact(m01, x):
    return _rows_exact_impl(m01, x, NN)


def _rows_exact_fwd(m01, x):
    return _rows_exact_impl(m01, x, NN), m01


def _rows_exact_bwd(m01, g):
    return jnp.zeros_like(m01), _rows_exact_impl(m01, g, TN)


_rows_exact.defvjp(_rows_exact_fwd, _rows_exact_bwd)


def _gla_blocks(Sts, qrs, ks, vs, las):
    H = len(Sts)
    n = len(qrs)
    C, dk = qrs[0].shape
    R = range(n)
    row = lax.broadcasted_iota(jnp.int32, (C, C), 0)
    col = lax.broadcasted_iota(jnp.int32, (C, C), 1)
    ri = lax.broadcasted_iota(jnp.int32, (C, dk), 0)
    q = [qrs[h] * (dk ** -0.5) for h in R]
    running = (row >= col).astype(f32)
    b = [_rows_exact(running, las[h]) for h in R]
    sc = [jnp.where(row == col, jnp.sum(q[h] * ks[h], axis=-1, keepdims=True), 0.0) for h in R]
    s = C // 2
    while s >= 1:
        sh = s.bit_length() - 1
        ref = lax.shift_left(lax.shift_right_logical(row, sh + 1), sh + 1) + (s - 1)
        pick = (col == ref).astype(f32)
        bref = [_rows_exact(pick, b[h]) for h in R]
        upper = (lax.shift_right_logical(ri, sh) & 1) == 1
        qt = [jnp.where(upper, q[h] * jnp.exp(jnp.where(upper, b[h] - bref[h], 0.0)), 0.0) for h in R]
        kt = [jnp.where(upper, 0.0, ks[h] * jnp.exp(jnp.where(upper, 0.0, bref[h] - b[h]))) for h in R]
        same = lax.shift_right_logical(row, sh + 1) == lax.shift_right_logical(col, sh + 1)
        sc = [sc[h] + jnp.where(same, _mmb(qt[h], kt[h], NT), 0.0) for h in R]
        s //= 2
    o = [_mmb(sc[h], vs[h], NN) for h in R]
    qe = [q[h] * jnp.exp(b[h]) for h in R]
    bl = [b[h][C - 1:C, :] for h in R]
    upd = [_mmb(vs[h], ks[h] * jnp.exp(bl[h] - b[h]), TN) for h in R]
    ebl = [jnp.exp(bl[h]) for h in R]
    St = list(Sts)
    for blk in range(n // H):
        for h in range(H):
            i = blk * H + h
            o[i] = o[i] + _mmb(qe[i], St[h], NT)
        St = [St[h] * ebl[blk * H + h] + upd[blk * H + h] for h in range(H)]
    return St, o


_ANY = pl.BlockSpec(memory_space=pl.ANY)


class _Gather:
    def __init__(self, arrays):
        self.arrays = list(arrays)
        self.n = len(self.arrays)
        self.out_shape = [jax.ShapeDtypeStruct((N_CHIP,) + a.shape, a.dtype) for a in self.arrays]
        self.sems = [pltpu.SemaphoreType.DMA((self.n, 3)), pltpu.SemaphoreType.DMA((self.n, 3)),
                     pltpu.SemaphoreType.DMA((self.n,))]

    def hooks(self, ins, outs, send, recv, lsem):
        def copies():
            x, y, c = lax.axis_index("x"), lax.axis_index("y"), lax.axis_index("c")
            me = 2 * x + y
            out = []
            for a in range(self.n):
                out.append((pltpu.make_async_copy(ins[a], outs[a].at[me], lsem.at[a]), None))
                for j, (px, py) in enumerate([(1 - x, y), (x, 1 - y), (1 - x, 1 - y)]):
                    mk = lambda dst, a=a, j=j, px=px, py=py: pltpu.make_async_remote_copy(
                        src_ref=ins[a], dst_ref=dst, send_sem=send.at[a, j], recv_sem=recv.at[a, j],
                        device_id=(px, py, c), device_id_type=MESH)
                    out.append((mk(outs[a].at[me]), mk(outs[a].at[2 * px + py])))
            return out

        return _start_wait(copies)


class _Exchange:
    def __init__(self, slotted, shared=(), by_chip=()):
        self.arrays = list(slotted) + list(by_chip) + list(shared)
        self.ns, self.nc = len(slotted), len(by_chip)
        self.n = len(self.arrays)
        self.out_shape = [jax.ShapeDtypeStruct(a.shape, a.dtype) for a in slotted]
        self.out_shape += [jax.ShapeDtypeStruct((N_DEV,) + a.shape[1:], a.dtype) for a in by_chip]
        self.out_shape += [jax.ShapeDtypeStruct((N_DEV,) + b.shape, b.dtype) for b in shared]
        self.sems = [pltpu.SemaphoreType.DMA((self.n, N_DEV - 1)), pltpu.SemaphoreType.DMA((self.n, N_DEV - 1)),
                     pltpu.SemaphoreType.DMA((self.n,))]

    def hooks(self, ins, outs, send, recv, lsem):
        def copies():
            x, y, c = lax.axis_index("x"), lax.axis_index("y"), lax.axis_index("c")
            me = 4 * x + 2 * y + c

            def src(a, dev):
                tx, ty, tc = dev
                if a < self.ns:
                    return ins[a].at[4 * tx + 2 * ty + tc]
                return ins[a].at[2 * tx + ty] if a < self.ns + self.nc else ins[a]

            out = []
            for a in range(self.n):
                out.append((pltpu.make_async_copy(src(a, (x, y, c)), outs[a].at[me], lsem.at[a]), None))
                for o in range(1, N_DEV):
                    dev = (1 - x if o & 4 else x, 1 - y if o & 2 else y, 1 - c if o & 1 else c)
                    t = 4 * dev[0] + 2 * dev[1] + dev[2]
                    mk = lambda dst, a=a, o=o, dev=dev: pltpu.make_async_remote_copy(
                        src_ref=src(a, dev), dst_ref=dst, send_sem=send.at[a, o - 1], recv_sem=recv.at[a, o - 1],
                        device_id=dev, device_id_type=MESH)
                    out.append((mk(outs[a].at[me]), mk(outs[a].at[t])))
            return out

        return _start_wait(copies)


def _start_wait(copies):
    def start():
        for s, _ in copies():
            s.start()

    def wait():
        for s, w in copies():
            (s if w is None else w).wait()

    return start, wait


def _call(body, *, name, grid, in_specs, out_specs, out_shape, args, sem, scratch_shapes=(), aliases=None, side=None):
    in_specs, out_specs, out_shape, args = list(in_specs), list(out_specs), list(out_shape), list(args)
    scratch_shapes = list(scratch_shapes)
    aliases = aliases or {}
    if side is None:
        return pl.pallas_call(
            body, name=name, grid=grid, in_specs=in_specs, out_specs=out_specs, out_shape=out_shape,
            scratch_shapes=scratch_shapes, input_output_aliases=aliases, compiler_params=_params(*sem))(*args)
    n_in, n_out, n_scr, ns = len(in_specs), len(out_specs), len(scratch_shapes), side.n

    def full_body(*refs):
        ins, refs = refs[:n_in], refs[n_in:]
        s_in, refs = refs[:ns], refs[ns:]
        outs, refs = refs[:n_out], refs[n_out:]
        s_out, refs = refs[:ns], refs[ns:]
        scr, sems = refs[:n_scr], refs[n_scr:]
        start, wait = side.hooks(s_in, s_out, *sems)
        ids = [pl.program_id(d) for d in range(len(grid))]
        first = functools.reduce(jnp.logical_and, [i == 0 for i in ids])
        last = functools.reduce(jnp.logical_and, [i == g - 1 for i, g in zip(ids, grid)])
        pl.when(first)(start)
        body(*ins, *outs, *scr)
        pl.when(last)(wait)

    return pl.pallas_call(
        full_body, name=name, grid=grid, in_specs=in_specs + [_ANY] * ns, out_specs=out_specs + [_ANY] * ns,
        out_shape=out_shape + side.out_shape, scratch_shapes=scratch_shapes + side.sems,
        input_output_aliases=aliases, compiler_params=_params(*(["arbitrary"] * len(grid))))(*args, *side.arrays)


WHOLE_K = dict(tm_cap=688, tn_cap=512, tk_cap=1 << 20)
WHOLE_K_T = dict(tm_cap=512, tn_cap=512, tk_cap=1 << 20)

def _mm(a, b, mode, name, *, tm_cap=1408, tn_cap=1024, tk_cap=2048, out_dtype=f32, acc_in=None, side=None,
        col_slabs=False):
    if mode == "nn":
        (M, K), (K2, N) = a.shape, b.shape
    elif mode == "nt":
        (M, K), (N, K2) = a.shape, b.shape
    else:
        (K, M), (K2, N) = a.shape, b.shape
    assert K == K2, (name, a.shape, b.shape)
    tm = _tile(M // 2 if col_slabs else M, tm_cap)
    tn = _tile(N // N_CHIP if col_slabs else N, tn_cap, 128)
    tk = _tile(K, tk_cap, 128 if K % 128 == 0 else 16)
    nk = K // tk
    dims = {"nn": NN, "nt": NT, "tn": TN}[mode]
    use_scratch = nk > 1 and out_dtype != f32

    def body(*refs):
        if acc_in is not None:
            a_ref, b_ref, c_ref, o_ref, *scr = refs
        else:
            a_ref, b_ref, o_ref, *scr = refs
            c_ref = None
        p = _mmb(a_ref[...], b_ref[...], dims)
        if nk == 1:
            if c_ref is not None:
                p = p + c_ref[...]
            o_ref[...] = p.astype(out_dtype)
            return
        k = pl.program_id(2)
        acc = scr[0] if use_scratch else o_ref

        @pl.when(k == 0)
        def _():
            acc[...] = p if c_ref is None else p + c_ref[...]

        @pl.when(k > 0)
        def _():
            acc[...] += p

        if use_scratch:
            @pl.when(k == nk - 1)
            def _():
                o_ref[...] = acc[...].astype(out_dtype)

    if mode == "tn":
        a_spec = pl.BlockSpec((tk, tm), lambda i, j, k: (k, i))
    else:
        a_spec = pl.BlockSpec((tm, tk), lambda i, j, k: (i, k))
    if mode == "nt":
        b_spec = pl.BlockSpec((tn, tk), lambda i, j, k: (j, k))
    else:
        b_spec = pl.BlockSpec((tk, tn), lambda i, j, k: (k, j))
    if col_slabs:
        assert acc_in is None
        ni, nj = M // 2 // tm, N // N_CHIP // tn
        o_spec = pl.BlockSpec((None, tm, tn), lambda i, j, k: (2 * (j // nj) + i // ni, i % ni, j % nj))
        o_shape = jax.ShapeDtypeStruct((N_DEV, M // 2, N // N_CHIP), out_dtype)
    else:
        o_spec = pl.BlockSpec((tm, tn), lambda i, j, k: (i, j))
        o_shape = jax.ShapeDtypeStruct((M, N), out_dtype)
    in_specs = [a_spec, b_spec]
    args = [a, b]
    if acc_in is not None:
        in_specs.append(o_spec)
        args.append(acc_in)
    out = _call(body, name=name, grid=(M // tm, N // tn, nk), in_specs=in_specs, out_specs=[o_spec],
                out_shape=[o_shape], args=args,
                scratch_shapes=[pltpu.VMEM((tm, tn), f32)] if use_scratch else [],
                sem=("parallel", "parallel", "arbitrary"), side=side)
    return out[0] if side is None else (out[0], out[1:])


def _embed_norm(x3, m64, w, side=None):
    _, S, D = x3.shape
    Lp = OFF + S

    def body(x_ref, m_ref, w_ref, h_ref, n_ref):
        i = pl.program_id(0)
        h = jnp.where(i == 0, m_ref[...], x_ref[...])
        h_ref[...] = h
        xh, _ = _rms(h)
        n_ref[...] = (xh * w_ref[...]).astype(bf16)

    row = pl.BlockSpec((ROWS, D), lambda i: (i, 0))
    out = _call(
        body, name="embed_norm", grid=(Lp // ROWS,),
        in_specs=[pl.BlockSpec((None, ROWS, D), lambda i: (0, jnp.maximum(i - 1, 0), 0)),
                  pl.BlockSpec((ROWS, D), lambda i: (0, 0)),
                  pl.BlockSpec((1, D), lambda i: (0, 0))],
        out_specs=[row, row],
        out_shape=[jax.ShapeDtypeStruct((Lp, D), f32), jax.ShapeDtypeStruct((Lp, D), bf16)],
        args=[x3, m64, w], sem=("parallel",), side=side)
    return out[0], out[1], out[2:]


def _add_norm(h, d, w):
    Lp, D = h.shape
    tr = _tile(Lp, 256)

    def body(h_ref, d_ref, w_ref, o_ref, n_ref):
        h1 = h_ref[...] + d_ref[...]
        o_ref[...] = h1
        xh, _ = _rms(h1)
        n_ref[...] = (xh * w_ref[...]).astype(bf16)

    row = pl.BlockSpec((tr, D), lambda i: (i, 0))
    return pl.pallas_call(
        body, name="add_norm", grid=(Lp // tr,),
        in_specs=[row, row, pl.BlockSpec((1, D), lambda i: (0, 0))], out_specs=[row, row],
        out_shape=[jax.ShapeDtypeStruct((Lp, D), f32), jax.ShapeDtypeStruct((Lp, D), bf16)],
        compiler_params=_params("parallel"),
    )(h, d, w)


def _norm_bwd(dn, h, dh, w):
    Lp, D = h.shape
    tr = _tile(Lp, 256)

    def body(dn_ref, h_ref, dh_ref, w_ref, o_ref, ob_ref, gw_ref):
        i = pl.program_id(0)
        xh, r = _rms(h_ref[...])
        dn_ = dn_ref[...]
        o = dh_ref[...] + _rms_bwd(dn_, xh, r, w_ref[...])
        o_ref[...] = o
        ob_ref[...] = o.astype(bf16)
        gw = jnp.sum(dn_ * xh, axis=0, keepdims=True)

        @pl.when(i == 0)
        def _():
            gw_ref[...] = gw

        @pl.when(i > 0)
        def _():
            gw_ref[...] += gw

    row = pl.BlockSpec((tr, D), lambda i: (i, 0))
    vec = pl.BlockSpec((1, D), lambda i: (0, 0))
    return pl.pallas_call(
        body, name="norm_bwd", grid=(Lp // tr,), in_specs=[row, row, row, vec], out_specs=[row, row, vec],
        out_shape=[jax.ShapeDtypeStruct((Lp, D), f32), jax.ShapeDtypeStruct((Lp, D), bf16),
                   jax.ShapeDtypeStruct((1, D), f32)],
        compiler_params=_params("arbitrary"),
    )(dn, h, dh, w)


def _embed_norm_bwd(dn, h, dh, w, S, side=None):
    Lp, D = h.shape

    def body(dn_ref, h_ref, dh_ref, w_ref, gx_ref, gm_ref, gw_ref):
        i = pl.program_id(0)
        xh, r = _rms(h_ref[...])
        dn_ = dn_ref[...]
        d0 = dh_ref[...] + _rms_bwd(dn_, xh, r, w_ref[...])
        gx_ref[...] = d0
        gw = jnp.sum(dn_ * xh, axis=0, keepdims=True)

        @pl.when(i == 0)
        def _():
            gm_ref[...] = d0[PAD:OFF, :]
            gw_ref[...] = gw

        @pl.when(i > 0)
        def _():
            gw_ref[...] += gw

    row = pl.BlockSpec((ROWS, D), lambda i: (i, 0))
    vec = pl.BlockSpec((1, D), lambda i: (0, 0))
    out = _call(
        body, name="embed_norm_bwd", grid=(Lp // ROWS,), in_specs=[row, row, row, vec],
        out_specs=[pl.BlockSpec((None, ROWS, D), lambda i: (0, jnp.maximum(i - 1, 0), 0)),
                   pl.BlockSpec((N_META, D), lambda i: (0, 0)), vec],
        out_shape=[jax.ShapeDtypeStruct((1, S, D), f32), jax.ShapeDtypeStruct((N_META, D), f32),
                   jax.ShapeDtypeStruct((1, D), f32)],
        args=[dn, h, dh, w], sem=("arbitrary",), side=side)
    return out[0], out[1], out[2], out[3:]


def _final(h1, ffn, tgt3, w):
    Lp, D = h1.shape

    def body(h_ref, f_ref, t_ref, w_ref, d_ref, db_ref, l_ref, gw_ref):
        i = pl.program_id(0)
        h2 = h_ref[...] + f_ref[...]
        xh, r = _rms(h2)
        w_ = w_ref[...]
        e = xh * w_ - t_ref[...]
        valid = (i > 0).astype(f32)
        loss = 0.5 * jnp.sum(jnp.mean(e * e, axis=-1, keepdims=True), axis=0, keepdims=True) * valid
        dy = e * (valid / D)
        d = _rms_bwd(dy, xh, r, w_)
        d_ref[...] = d
        db_ref[...] = d.astype(bf16)
        gw = jnp.sum(dy * xh, axis=0, keepdims=True)

        @pl.when(i == 0)
        def _():
            l_ref[...] = jnp.zeros_like(l_ref)
            gw_ref[...] = jnp.zeros_like(gw_ref)

        l_ref[...] += jnp.broadcast_to(loss, l_ref.shape)
        gw_ref[...] += gw

    row = pl.BlockSpec((ROWS, D), lambda i: (i, 0))
    vec = pl.BlockSpec((1, D), lambda i: (0, 0))
    return pl.pallas_call(
        body, name="final_loss", grid=(Lp // ROWS,),
        in_specs=[row, row, pl.BlockSpec((None, ROWS, D), lambda i: (0, jnp.maximum(i - 1, 0), 0)), vec],
        out_specs=[row, row, pl.BlockSpec((8, 128), lambda i: (0, 0)), vec],
        out_shape=[jax.ShapeDtypeStruct((Lp, D), f32), jax.ShapeDtypeStruct((Lp, D), bf16),
                   jax.ShapeDtypeStruct((8, 128), f32), jax.ShapeDtypeStruct((1, D), f32)],
        compiler_params=_params("arbitrary"),
    )(h1, ffn, tgt3, w)


def _ffn_in(n, w_gate, w_up, side=None):
    M, K = n.shape
    F = w_gate.shape[1]
    tm = _tile(M, 1408)
    tn = _tile(F, 512, 128)

    def body(a_ref, bg_ref, bu_ref, act_ref, g_ref, u_ref):
        a = a_ref[...]
        g = _mmb(a, bg_ref[...], NN)
        u = _mmb(a, bu_ref[...], NN)
        act_ref[...] = (_silu(g) * u).astype(bf16)
        g_ref[...] = g.astype(bf16)
        u_ref[...] = u.astype(bf16)

    wsp = pl.BlockSpec((K, tn), lambda i, j: (0, j))
    osp = pl.BlockSpec((tm, tn), lambda i, j: (i, j))
    out = _call(body, name="ffn_in", grid=(M // tm, F // tn),
                in_specs=[pl.BlockSpec((tm, K), lambda i, j: (i, 0)), wsp, wsp], out_specs=[osp] * 3,
                out_shape=[jax.ShapeDtypeStruct((M, F), bf16)] * 3, args=[n, w_gate, w_up],
                sem=("parallel", "parallel"), side=side)
    return out[0], out[1], out[2], out[3:]


def _ffn_dact(d, w_down, g, u):
    M, K = d.shape
    F = w_down.shape[0]
    tm = _tile(M, 1408)
    tn = _tile(F, 512, 128)

    def body(d_ref, w_ref, g_ref, u_ref, dg_ref, du_ref):
        da = _mmb(d_ref[...], w_ref[...], NT)
        g_ = g_ref[...].astype(f32)
        dg_ref[...] = (da * u_ref[...].astype(f32) * _dsilu(g_)).astype(bf16)
        du_ref[...] = (da * _silu(g_)).astype(bf16)

    osp = pl.BlockSpec((tm, tn), lambda i, j: (i, j))
    return pl.pallas_call(
        body, name="ffn_dact", grid=(M // tm, F // tn),
        in_specs=[pl.BlockSpec((tm, K), lambda i, j: (i, 0)), pl.BlockSpec((tn, K), lambda i, j: (j, 0)), osp, osp],
        out_specs=[osp, osp], out_shape=[jax.ShapeDtypeStruct((M, F), bf16)] * 2,
        compiler_params=_params("parallel", "parallel"),
    )(d, w_down, g, u)


def _gates(psm, w2p, gate_b, alog, dtb):
    Lp = psm.shape[0]
    tr = _tile(Lp, 256)

    def body(p_ref, w_ref, b_ref, a_ref, t_ref, gb_ref, la_ref):
        i = pl.program_id(0)
        psm_ = p_ref[...]
        lane = lax.broadcasted_iota(jnp.int32, psm_.shape, 1)
        rowi = lax.broadcasted_iota(jnp.int32, (tr, 1), 0) + i * tr
        g = -jnp.exp(a_ref[...]) * _softplus(psm_ + t_ref[...])
        beta = _sigmoid(psm_)
        gb = jnp.where(lane < GDN_H, g, jnp.where(lane < 2 * GDN_H, beta, 0.0))
        gb_ref[...] = gb * (rowi >= PAD).astype(f32)
        logit = _mmb(psm_, w_ref[...], NN) + b_ref[...]
        la_ref[...] = _log_sigmoid(logit) * (1.0 / GATE_NORMALIZER)

    row = pl.BlockSpec((tr, SM_W), lambda i: (i, 0))
    return pl.pallas_call(
        body, name="gates", grid=(Lp // tr,),
        in_specs=[row, pl.BlockSpec((SM_W, GLA_QK), lambda i: (0, 0)), pl.BlockSpec((1, GLA_QK), lambda i: (0, 0)),
                  pl.BlockSpec((1, SM_W), lambda i: (0, 0)), pl.BlockSpec((1, SM_W), lambda i: (0, 0))],
        out_specs=[row, pl.BlockSpec((tr, GLA_QK), lambda i: (i, 0))],
        out_shape=[jax.ShapeDtypeStruct((Lp, SM_W), f32), jax.ShapeDtypeStruct((Lp, GLA_QK), f32)],
        compiler_params=_params("parallel"),
    )(psm, w2p, gate_b, alog, dtb)


def _gates_bwd(psm, w2p, gate_b, alog, dtb, dgb, dla):
    Lp = psm.shape[0]
    tr = _tile(Lp, 256)

    def body(p_ref, w_ref, b_ref, a_ref, t_ref, dgb_ref, dla_ref, dp_ref, gw_ref, gb_ref, ga_ref, gt_ref):
        i = pl.program_id(0)
        psm_ = p_ref[...]
        lane = lax.broadcasted_iota(jnp.int32, psm_.shape, 1)
        rowi = lax.broadcasted_iota(jnp.int32, (tr, 1), 0) + i * tr
        d = dgb_ref[...] * (rowi >= PAD).astype(f32)
        ea = jnp.exp(a_ref[...])
        z = psm_ + t_ref[...]
        is_g = lane < GDN_H
        dz = jnp.where(is_g, -ea * _sigmoid(z) * d, 0.0)
        dalog = jnp.where(is_g, -ea * _softplus(z) * d, 0.0)
        beta = _sigmoid(psm_)
        dbeta = jnp.where(jnp.logical_and(lane >= GDN_H, lane < 2 * GDN_H), beta * (1.0 - beta) * d, 0.0)
        logit = _mmb(psm_, w_ref[...], NN) + b_ref[...]
        dlogit = dla_ref[...] * (_sigmoid(-logit) * (1.0 / GATE_NORMALIZER))
        dlr = _mmb(dlogit, w_ref[...], NT)
        dp_ref[...] = (dz + dbeta + dlr).astype(bf16)
        gw = _mmb(psm_, dlogit, TN)
        gb = jnp.sum(dlogit, axis=0, keepdims=True)
        ga = jnp.sum(dalog, axis=0, keepdims=True)
        gt = jnp.sum(dz, axis=0, keepdims=True)

        @pl.when(i == 0)
        def _():
            gw_ref[...] = gw
            gb_ref[...] = gb
            ga_ref[...] = ga
            gt_ref[...] = gt

        @pl.when(i > 0)
        def _():
            gw_ref[...] += gw
            gb_ref[...] += gb
            ga_ref[...] += ga
            gt_ref[...] += gt

    row = pl.BlockSpec((tr, SM_W), lambda i: (i, 0))
    wsp = pl.BlockSpec((SM_W, GLA_QK), lambda i: (0, 0))
    bsp = pl.BlockSpec((1, GLA_QK), lambda i: (0, 0))
    vsp = pl.BlockSpec((1, SM_W), lambda i: (0, 0))
    return pl.pallas_call(
        body, name="gates_bwd", grid=(Lp // tr,),
        in_specs=[row, wsp, bsp, vsp, vsp, row, pl.BlockSpec((tr, GLA_QK), lambda i: (i, 0))],
        out_specs=[row, wsp, bsp, vsp, vsp],
        out_shape=[jax.ShapeDtypeStruct((Lp, SM_W), bf16), jax.ShapeDtypeStruct((SM_W, GLA_QK), f32),
                   jax.ShapeDtypeStruct((1, GLA_QK), f32), jax.ShapeDtypeStruct((1, SM_W), f32),
                   jax.ShapeDtypeStruct((1, SM_W), f32)],
        compiler_params=_params("arbitrary"),
    )(psm, w2p, gate_b, alog, dtb, dgb, dla)


def _conv_pre(x_ext, w, n):
    rows = x_ext.shape[0]
    y = x_ext * w[CONV_K - 1:CONV_K, :]
    for s in range(1, CONV_K):
        y = y + pltpu.roll(x_ext, s, 0) * w[CONV_K - 1 - s:CONV_K - s, :]
    return y[rows - n:, :]


def _conv(proj, cw, side=None):
    Lp = proj.shape[0]
    W = cw.shape[1]
    tr = _tile(Lp, 256, 64)
    tc = _tile(W, 1024, 128)
    c0 = C_QKV // tc

    def body(h_ref, x_ref, w_ref, o_ref):
        i = pl.program_id(1)
        halo = jnp.where(i == 0, 0.0, h_ref[...])
        x_ext = jnp.concatenate([halo, x_ref[...]], axis=0)
        o_ref[...] = _silu(_conv_pre(x_ext, w_ref[...], tr))

    out = _call(
        body, name="conv", grid=(W // tc, Lp // tr),
        in_specs=[pl.BlockSpec((8, tc), lambda j, i: (jnp.maximum(i * (tr // 8) - 1, 0), j + c0)),
                  pl.BlockSpec((tr, tc), lambda j, i: (i, j + c0)),
                  pl.BlockSpec((CONV_K, tc), lambda j, i: (0, j))],
        out_specs=[pl.BlockSpec((tr, tc), lambda j, i: (i, j))],
        out_shape=[jax.ShapeDtypeStruct((Lp, W), f32)], args=[proj, proj, cw],
        sem=("parallel", "parallel"), side=side)
    return out[0] if side is None else (out[0], out[1:])


def _conv_bwd(proj, cw, dy, dproj, side=None):
    Lp = proj.shape[0]
    W = cw.shape[1]
    tr = _tile(Lp, 256, 64)
    tc = _tile(W, 1024, 128)
    c0 = C_QKV // tc
    nr = Lp // tr
    last8 = Lp // 8 - 1

    def body(xp_ref, x_ref, xn_ref, w_ref, d_ref, dn_ref, dproj_ref, o_ref, gw_ref):
        del dproj_ref
        i = pl.program_id(1)
        w = w_ref[...]
        xp = jnp.where(i == 0, 0.0, xp_ref[...])
        x_ext = jnp.concatenate([xp, x_ref[...], xn_ref[...]], axis=0)
        n = tr + 8
        pre = _conv_pre(x_ext, w, n)
        dn = jnp.where(i == nr - 1, 0.0, dn_ref[...])
        dpre = jnp.concatenate([d_ref[...], dn], axis=0) * _dsilu(pre)
        dx = dpre * w[CONV_K - 1:CONV_K, :]
        for s in range(1, CONV_K):
            dx = dx + pltpu.roll(dpre, n - s, 0) * w[CONV_K - 1 - s:CONV_K - s, :]
        o_ref[...] = dx[:tr, :].astype(bf16)
        dp = dpre[:tr, :]
        rows = []
        for k in range(CONV_K):
            xs = x_ext if k == CONV_K - 1 else pltpu.roll(x_ext, CONV_K - 1 - k, 0)
            rows.append(jnp.sum(dp * xs[8:8 + tr, :], axis=0, keepdims=True))
        gw = jnp.concatenate(rows, axis=0)

        @pl.when(i == 0)
        def _():
            gw_ref[...] = gw

        @pl.when(i > 0)
        def _():
            gw_ref[...] += gw

    cur = pl.BlockSpec((tr, tc), lambda j, i: (i, j))
    nxt = pl.BlockSpec((8, tc), lambda j, i: (jnp.minimum((i + 1) * (tr // 8), last8), j))
    pcur = pl.BlockSpec((tr, tc), lambda j, i: (i, j + c0))
    pprev = pl.BlockSpec((8, tc), lambda j, i: (jnp.maximum(i * (tr // 8) - 1, 0), j + c0))
    pnext = pl.BlockSpec((8, tc), lambda j, i: (jnp.minimum((i + 1) * (tr // 8), last8), j + c0))
    wsp = pl.BlockSpec((CONV_K, tc), lambda j, i: (0, j))
    out = _call(
        body, name="conv_bwd", grid=(W // tc, nr),
        in_specs=[pprev, pcur, pnext, wsp, cur, nxt, _ANY], out_specs=[pcur, wsp],
        out_shape=[jax.ShapeDtypeStruct(dproj.shape, dproj.dtype), jax.ShapeDtypeStruct((CONV_K, W), f32)],
        aliases={6: 0}, args=[proj, proj, proj, cw, dy, dy, dproj], sem=("parallel", "arbitrary"), side=side)
    return out[0], out[1], out[2:]


GDN_FWD_GROUP = 3


def _gdn_group(Lp, most):
    n = Lp // GDN_C
    return next(g for g in range(most, 0, -1) if n % g == 0)


def _gdn_heads(x_ref, gb_ref, group):
    qs, ks, vs, bs, gs = [], [], [], [], []
    for chunk in range(group):
        r = slice(chunk * GDN_C, (chunk + 1) * GDN_C)
        gbv = gb_ref[r, :]
        for h in range(GDN_H):
            qs.append(x_ref[r, Q0 + h * GDN_DK:Q0 + (h + 1) * GDN_DK])
            ks.append(x_ref[r, K0 + h * GDN_DK:K0 + (h + 1) * GDN_DK])
            vs.append(x_ref[r, V0 + h * GDN_DV:V0 + (h + 1) * GDN_DV])
            bs.append(gbv[:, GDN_H + h:GDN_H + h + 1])
            gs.append(gbv[:, h:h + 1])
    return qs, ks, vs, bs, gs


def _gdn_fwd(qkvc, gb, side=None):
    Lp = qkvc.shape[0]
    group = _gdn_group(Lp, GDN_FWD_GROUP)
    rows = group * GDN_C
    steps = Lp // rows
    R = range(GDN_H)

    def body(x_ref, gb_ref, o_ref, sall_ref, pall_ref, s_scr):
        @pl.when(pl.program_id(0) == 0)
        def _():
            s_scr[...] = jnp.zeros_like(s_scr)

        S2, o, p, entering = _gdn_chunk([s_scr[h] for h in R], *_gdn_heads(x_ref, gb_ref, group))
        for h in R:
            s_scr[h] = S2[h]
        for chunk in range(group):
            for h in R:
                i = chunk * GDN_H + h
                o_ref[chunk * GDN_C:(chunk + 1) * GDN_C, h * GDN_DV:(h + 1) * GDN_DV] = o[i]
                pall_ref[chunk, h] = p[i]
                sall_ref[chunk, h] = entering[i]

    out = _call(
        body, name="gdn_fwd", grid=(steps,),
        in_specs=[pl.BlockSpec((rows, QKV_W), lambda n: (n, 0)), pl.BlockSpec((rows, SM_W), lambda n: (n, 0))],
        out_specs=[pl.BlockSpec((rows, GDN_V), lambda n: (n, 0)),
                   pl.BlockSpec((group, GDN_H, GDN_DK, GDN_DV), lambda n: (n, 0, 0, 0)),
                   pl.BlockSpec((group, GDN_H, GDN_C, GDN_C), lambda n: (n, 0, 0, 0))],
        out_shape=[jax.ShapeDtypeStruct((Lp, GDN_V), f32),
                   jax.ShapeDtypeStruct((Lp // GDN_C, GDN_H, GDN_DK, GDN_DV), f32),
                   jax.ShapeDtypeStruct((Lp // GDN_C, GDN_H, GDN_C, GDN_C), f32)],
        scratch_shapes=[pltpu.VMEM((GDN_H, GDN_DK, GDN_DV), f32)], args=[qkvc, gb], sem=("arbitrary",), side=side)
    return out[0], out[1], out[2], out[3:]


def _gdn_bwd(qkvc, gb, sall, pall, do, side=None):
    Lp = qkvc.shape[0]
    group = 1
    rows = group * GDN_C
    steps = Lp // rows
    R = range(GDN_H)

    def body(x_ref, gb_ref, sall_ref, pall_ref, do_ref, dx_ref, dgb_ref, ds_scr):
        @pl.when(pl.program_id(0) == 0)
        def _():
            ds_scr[...] = jnp.zeros_like(ds_scr)

        lane = lax.broadcasted_iota(jnp.int32, (GDN_C, SM_W), 1)
        ps = [pall_ref[chunk, h] for chunk in range(group) for h in R]
        _, vjp = jax.vjp(lambda *a: _gdn_chunk(*a, Ps=ps)[:2],
                         [sall_ref[0, h] for h in R], *_gdn_heads(x_ref, gb_ref, group))
        do = [do_ref[chunk * GDN_C:(chunk + 1) * GDN_C, h * GDN_DV:(h + 1) * GDN_DV]
              for chunk in range(group) for h in R]
        dS, dq, dk, dv, dbeta, dg = vjp(([ds_scr[h] for h in R], do))
        for h in R:
            ds_scr[h] = dS[h]
        for chunk in range(group):
            r = slice(chunk * GDN_C, (chunk + 1) * GDN_C)
            acc = jnp.zeros((GDN_C, SM_W), f32)
            for h in R:
                i = chunk * GDN_H + h
                dx_ref[r, Q0 + h * GDN_DK:Q0 + (h + 1) * GDN_DK] = dq[i]
                dx_ref[r, K0 + h * GDN_DK:K0 + (h + 1) * GDN_DK] = dk[i]
                dx_ref[r, V0 + h * GDN_DV:V0 + (h + 1) * GDN_DV] = dv[i]
                acc = acc + jnp.where(lane == h, dg[i], 0.0) + jnp.where(lane == GDN_H + h, dbeta[i], 0.0)
            dgb_ref[r, :] = acc

    rev = lambda n: (steps - 1 - n, 0)
    out = _call(
        body, name="gdn_bwd", grid=(steps,),
        in_specs=[pl.BlockSpec((rows, QKV_W), rev), pl.BlockSpec((rows, SM_W), rev),
                  pl.BlockSpec((1, GDN_H, GDN_DK, GDN_DV), lambda n: (steps - 1 - n, 0, 0, 0)),
                  pl.BlockSpec((group, GDN_H, GDN_C, GDN_C), lambda n: (steps - 1 - n, 0, 0, 0)),
                  pl.BlockSpec((rows, GDN_V), rev)],
        out_specs=[pl.BlockSpec((rows, QKV_W), rev), pl.BlockSpec((rows, SM_W), rev)],
        out_shape=[jax.ShapeDtypeStruct((Lp, QKV_W), f32), jax.ShapeDtypeStruct((Lp, SM_W), f32)],
        scratch_shapes=[pltpu.VMEM((GDN_H, GDN_DK, GDN_DV), f32)], args=[qkvc, gb, sall, pall, do],
        sem=("arbitrary",), side=side)
    return out[0], out[1], out[2:]


GLA_BLOCK = 64


def _gla_group(Lp):
    nb = Lp // GLA_BLOCK
    return next(g for g in (3, 2, 1) if nb % g == 0)


def _gla_slices(h):
    sq = slice(h * GLA_DK, (h + 1) * GLA_DK)
    sk = slice(GLA_QK + h * GLA_DK, GLA_QK + (h + 1) * GLA_DK)
    sv = slice(2 * GLA_QK + h * GLA_DV, 2 * GLA_QK + (h + 1) * GLA_DV)
    return sq, sk, sv


def _gla_heads(x_ref, la_ref, group):
    qs, ks, vs, ls = [], [], [], []
    for blk in range(group):
        r = slice(blk * GLA_BLOCK, (blk + 1) * GLA_BLOCK)
        for h in range(GLA_H):
            sq, sk, sv = _gla_slices(h)
            qs.append(x_ref[r, sq])
            ks.append(x_ref[r, sk])
            vs.append(x_ref[r, sv])
            ls.append(la_ref[r, sq])
    return qs, ks, vs, ls


def _gla_fwd(proj, la):
    Lp = proj.shape[0]
    group = _gla_group(Lp)
    rows = group * GLA_BLOCK
    steps = Lp // rows
    R = range(GLA_H)

    def body(x_ref, la_ref, o_ref, sall_ref, s_scr):
        @pl.when(pl.program_id(0) == 0)
        def _():
            s_scr[...] = jnp.zeros_like(s_scr)

        Sts = [s_scr[h] for h in R]
        for h in R:
            sall_ref[0, h] = Sts[h]
        St2, o = _gla_blocks(Sts, *_gla_heads(x_ref, la_ref, group))
        for h in R:
            s_scr[h] = St2[h]
        for blk in range(group):
            for h in R:
                o_ref[blk * GLA_BLOCK:(blk + 1) * GLA_BLOCK, h * GLA_DV:(h + 1) * GLA_DV] = o[blk * GLA_H + h]

    return pl.pallas_call(
        body, name="gla_fwd", grid=(steps,),
        in_specs=[pl.BlockSpec((rows, G_W), lambda n: (n, C_G // G_W)),
                  pl.BlockSpec((rows, GLA_QK), lambda n: (n, 0))],
        out_specs=[pl.BlockSpec((rows, GLA_V), lambda n: (n, 0)),
                   pl.BlockSpec((1, GLA_H, GLA_DV, GLA_DK), lambda n: (n, 0, 0, 0))],
        out_shape=[jax.ShapeDtypeStruct((Lp, GLA_V), f32),
                   jax.ShapeDtypeStruct((steps, GLA_H, GLA_DV, GLA_DK), f32)],
        scratch_shapes=[pltpu.VMEM((GLA_H, GLA_DV, GLA_DK), f32)],
        compiler_params=_params("arbitrary"),
    )(proj, la)


def _gla_bwd(proj, la, sall, do, dproj, side=None):
    Lp = proj.shape[0]
    group = _gla_group(Lp)
    rows = group * GLA_BLOCK
    steps = Lp // rows
    R = range(GLA_H)

    def body(x_ref, la_ref, sall_ref, do_ref, dproj_ref, dx_ref, dla_ref, ds_scr):
        del dproj_ref

        @pl.when(pl.program_id(0) == 0)
        def _():
            ds_scr[...] = jnp.zeros_like(ds_scr)

        _, vjp = jax.vjp(_gla_blocks, [sall_ref[0, h] for h in R], *_gla_heads(x_ref, la_ref, group))
        do = [do_ref[blk * GLA_BLOCK:(blk + 1) * GLA_BLOCK, h * GLA_DV:(h + 1) * GLA_DV]
              for blk in range(group) for h in R]
        dS, dq, dk, dv, dl = vjp(([ds_scr[h] for h in R], do))
        for h in R:
            ds_scr[h] = dS[h]
        for blk in range(group):
            r = slice(blk * GLA_BLOCK, (blk + 1) * GLA_BLOCK)
            for h in R:
                sq, sk, sv = _gla_slices(h)
                i = blk * GLA_H + h
                dx_ref[r, sq] = dq[i].astype(bf16)
                dx_ref[r, sk] = dk[i].astype(bf16)
                dx_ref[r, sv] = dv[i].astype(bf16)
                dla_ref[r, sq] = dl[i]

    x_spec = pl.BlockSpec((rows, G_W), lambda n: (steps - 1 - n, C_G // G_W))
    rev = lambda n: (steps - 1 - n, 0)
    out = _call(
        body, name="gla_bwd", grid=(steps,),
        in_specs=[x_spec, pl.BlockSpec((rows, GLA_QK), rev),
                  pl.BlockSpec((1, GLA_H, GLA_DV, GLA_DK), lambda n: (steps - 1 - n, 0, 0, 0)),
                  pl.BlockSpec((rows, GLA_V), rev), _ANY],
        out_specs=[x_spec, pl.BlockSpec((rows, GLA_QK), rev)],
        out_shape=[jax.ShapeDtypeStruct(dproj.shape, dproj.dtype), jax.ShapeDtypeStruct((Lp, GLA_QK), f32)],
        aliases={4: 0}, scratch_shapes=[pltpu.VMEM((GLA_H, GLA_DV, GLA_DK), f32)],
        args=[proj, la, sall, do, dproj], sem=("arbitrary",), side=side)
    return out[0], out[1], out[2:]


def _gated_norm_fn(og, ol, zr, wg, wl):
    outs = []
    for h in range(GDN_H):
        s = slice(h * GDN_DV, (h + 1) * GDN_DV)
        outs.append(_rms(og[:, s])[0] * wg * _silu(zr[:, s]))
    for h in range(GLA_H):
        s = slice(h * GLA_DV, (h + 1) * GLA_DV)
        sr = slice(GDN_V + h * GLA_DV, GDN_V + (h + 1) * GLA_DV)
        outs.append(_rms(ol[:, s])[0] * wl * _silu(zr[:, sr]))
    return jnp.concatenate(outs, axis=-1)


def _gated_norm(og, ol, proj, wg, wl):
    Lp = og.shape[0]
    tr = _tile(Lp, 256)

    def body(og_ref, ol_ref, zr_ref, wg_ref, wl_ref, o_ref):
        o_ref[...] = _gated_norm_fn(og_ref[...], ol_ref[...], zr_ref[...], wg_ref[...], wl_ref[...]).astype(bf16)

    return pl.pallas_call(
        body, name="gated_norm", grid=(Lp // tr,),
        in_specs=[pl.BlockSpec((tr, GDN_V), lambda i: (i, 0)), pl.BlockSpec((tr, GLA_V), lambda i: (i, 0)),
                  pl.BlockSpec((tr, ZR_W), lambda i: (i, C_ZR // ZR_W)),
                  pl.BlockSpec((1, GDN_DV), lambda i: (0, 0)), pl.BlockSpec((1, GLA_DV), lambda i: (0, 0))],
        out_specs=pl.BlockSpec((tr, ZR_W), lambda i: (i, 0)),
        out_shape=jax.ShapeDtypeStruct((Lp, ZR_W), bf16),
        compiler_params=_params("parallel"),
    )(og, ol, proj, wg, wl)


def _gated_norm_bwd(og, ol, proj, wg, wl, dmix):
    Lp = og.shape[0]
    tr = _tile(Lp, 128)

    def body(og_ref, ol_ref, zr_ref, wg_ref, wl_ref, d_ref, dog_ref, dol_ref, dzr_ref, gwg_ref, gwl_ref):
        i = pl.program_id(0)
        _, vjp = jax.vjp(_gated_norm_fn, og_ref[...], ol_ref[...], zr_ref[...], wg_ref[...], wl_ref[...])
        dog, dol, dzr, gwg, gwl = vjp(d_ref[...])
        dog_ref[...] = dog
        dol_ref[...] = dol
        dzr_ref[...] = dzr.astype(bf16)

        @pl.when(i == 0)
        def _():
            gwg_ref[...] = gwg
            gwl_ref[...] = gwl

        @pl.when(i > 0)
        def _():
            gwg_ref[...] += gwg
            gwl_ref[...] += gwl

    og_spec = pl.BlockSpec((tr, GDN_V), lambda i: (i, 0))
    ol_spec = pl.BlockSpec((tr, GLA_V), lambda i: (i, 0))
    zr_spec = pl.BlockSpec((tr, ZR_W), lambda i: (i, C_ZR // ZR_W))
    vg = pl.BlockSpec((1, GDN_DV), lambda i: (0, 0))
    vl = pl.BlockSpec((1, GLA_DV), lambda i: (0, 0))
    return pl.pallas_call(
        body, name="gated_norm_bwd", grid=(Lp // tr,),
        in_specs=[og_spec, ol_spec, zr_spec, vg, vl, pl.BlockSpec((tr, ZR_W), lambda i: (i, 0))],
        out_specs=[og_spec, ol_spec, zr_spec, vg, vl],
        out_shape=[jax.ShapeDtypeStruct((Lp, GDN_V), f32), jax.ShapeDtypeStruct((Lp, GLA_V), f32),
                   jax.ShapeDtypeStruct((Lp, C_END), bf16),
                   jax.ShapeDtypeStruct((1, GDN_DV), f32), jax.ShapeDtypeStruct((1, GLA_DV), f32)],
        compiler_params=_params("arbitrary"),
    )(og, ol, proj, wg, wl, dmix)


def _adamw(g, w, m, v, name):
    R, C = g.shape
    tr = _tile(R, 256, 8) if R % 8 == 0 and R > 256 else R
    c1 = 1.0 - ADAM_B1 ** ADAM_STEP
    c2 = 1.0 - ADAM_B2 ** ADAM_STEP

    def body(g_ref, w_ref, m_ref, v_ref, d_ref, mo_ref, vo_ref):
        g_ = g_ref[...]
        m2 = ADAM_B1 * m_ref[...] + (1.0 - ADAM_B1) * g_
        v2 = ADAM_B2 * v_ref[...] + (1.0 - ADAM_B2) * (g_ * g_)
        mo_ref[...] = m2
        vo_ref[...] = v2
        d_ref[...] = -ADAM_LR * ((m2 / c1) / (jnp.sqrt(v2 / c2) + ADAM_EPS) + ADAM_WD * w_ref[...])

    blk = pl.BlockSpec((tr, C), lambda i: (i, 0))
    return pl.pallas_call(
        body, name=name, grid=(R // tr,), in_specs=[blk] * 4, out_specs=[blk] * 3,
        out_shape=[jax.ShapeDtypeStruct((R, C), f32)] * 3,
        compiler_params=_params("parallel"),
    )(g, w, m, v)


def _sum_slots(r, name):
    n, R, C = r.shape
    tr = _tile(R, 128, 16) if R % 16 == 0 and R > 128 else R

    def body(r_ref, o_ref):
        acc = r_ref[0].astype(f32)
        for s in range(1, n):
            acc = acc + r_ref[s].astype(f32)
        o_ref[...] = acc

    return pl.pallas_call(
        body, name=name, grid=(R // tr,),
        in_specs=[pl.BlockSpec((n, tr, C), lambda i: (0, i, 0))],
        out_specs=pl.BlockSpec((tr, C), lambda i: (i, 0)),
        out_shape=jax.ShapeDtypeStruct((R, C), f32),
        compiler_params=_params("parallel"),
    )(r)


SIBLING_PARTS = 8


class _Siblings:
    def __init__(self, arrays):
        self.arrays = list(arrays)
        self.n = len(self.arrays)
        self.parts = [next(p for p in range(SIBLING_PARTS, 0, -1) if a.shape[0] % (8 * p) == 0 or p == 1)
                      for a in self.arrays]
        total = sum(self.parts)
        self.out_shape = [jax.ShapeDtypeStruct((2,) + a.shape, a.dtype) for a in self.arrays]
        self.sems = [pltpu.SemaphoreType.DMA((total,)), pltpu.SemaphoreType.DMA((total,)),
                     pltpu.SemaphoreType.DMA((self.n,))]

    def hooks(self, ins, outs, send, recv, lsem):
        def copies():
            x, y, c = lax.axis_index("x"), lax.axis_index("y"), lax.axis_index("c")
            out, k = [], 0
            for a in range(self.n):
                out.append((pltpu.make_async_copy(ins[a], outs[a].at[c], lsem.at[a]), None))
                rows = self.arrays[a].shape[0] // self.parts[a]
                for part in range(self.parts[a]):
                    r = pl.ds(part * rows, rows)
                    mk = lambda dst, a=a, r=r, k=k: pltpu.make_async_remote_copy(
                        src_ref=ins[a].at[r], dst_ref=dst.at[r], send_sem=send.at[k], recv_sem=recv.at[k],
                        device_id=(x, y, 1 - c), device_id_type=MESH)
                    out.append((mk(outs[a].at[c]), mk(outs[a].at[1 - c])))
                    k += 1
            return out

        return _start_wait(copies)


def _comm_now(name, sides):
    total = sum(s.n for s in sides)

    def body(*refs):
        ins, outs, sems = refs[:total], refs[total:2 * total], refs[2 * total:]
        hooks, o = [], 0
        for i, s in enumerate(sides):
            hooks.append(s.hooks(ins[o:o + s.n], outs[o:o + s.n], *sems[3 * i:3 * i + 3]))
            o += s.n
        for start, _ in hooks:
            start()
        for _, wait in hooks:
            wait()

    out = pl.pallas_call(
        body, name=name, in_specs=[_ANY] * total, out_specs=[_ANY] * total,
        out_shape=[sh for s in sides for sh in s.out_shape], scratch_shapes=[sm for s in sides for sm in s.sems],
    )(*[a for s in sides for a in s.arrays])
    res, o = [], 0
    for s in sides:
        res.append(list(out[o:o + s.n]))
        o += s.n
    return res


def _cat_cols(g):
    return jnp.concatenate([g[i] for i in range(N_CHIP)], axis=-1)


def _row_slabs(a):
    return a.reshape(N_DEV, a.shape[0] // N_DEV, a.shape[1])


def _w_in_columns(g_wp, g_wsm):
    return jnp.concatenate([g_wp[:, C_QKV:C_END], g_wp[:, C_ZR:C_ZR + GDN_V], g_wsm[:, :SM_LR],
                            g_wp[:, C_G:C_G + G_W], g_wp[:, C_ZR + GDN_V:C_ZR + ZR_W],
                            g_wsm[:, SM_LR:SM_LR + GATE_RANK]], axis=1)


def _step(x, loss_target, p, meta, shard):
    _, S, D = x.shape
    alog_p = jnp.pad(p["gdn_a_log"], ((0, 0), (0, SM_W - GDN_H)))
    dtb_p = jnp.pad(p["gdn_dt_bias"], ((0, 0), (0, SM_W - GDN_H)))
    m64 = jnp.concatenate([jnp.zeros((PAD, D), f32), meta], axis=0)
    gate_b, gdn_norm_w, gla_norm_w = p["gla_gate_b"], p["gdn_norm_w"], p["gla_norm_w"]
    half = shard["w_up"].shape[0] // 2

    h0, n1, (w_in4, conv4, w24) = _embed_norm(
        x, m64, p["attn_norm_w"], side=_Gather([shard["w_in"], shard["gdn_conv_w"], shard["gla_gate_w2"]]))
    w_in, conv_w, w2 = _cat_cols(w_in4), _cat_cols(conv4), _cat_cols(w24)
    wp = jnp.concatenate([w_in[:, R_Z:R_AB], w_in[:, R_GR:R_LR], w_in[:, R_G:R_GR], w_in[:, R_QKV:R_Z]], axis=1)
    wsm = jnp.concatenate([w_in[:, R_AB:R_G], w_in[:, R_LR:R_END],
                           jnp.zeros((D, SM_W - SM_LR - GATE_RANK), w_in.dtype)], axis=1)
    w2p = jnp.pad(w2, ((SM_LR, SM_W - SM_LR - GATE_RANK), (0, 0)))
    proj, (w_out4, w_up4a) = _mm(n1, wp, "nn", "proj", side=_Gather([shard["w_out"], shard["w_up"][:half]]))
    w_out = w_out4.reshape(-1, D)
    psm = _mm(n1, wsm, "nn", "proj_small")
    gb, la = _gates(psm, w2p, gate_b, alog_p, dtb_p)
    qkvc, (w_up4b,) = _conv(proj, conv_w, side=_Gather([shard["w_up"][half:]]))
    w_up = jnp.concatenate([_cat_cols(w_up4a), _cat_cols(w_up4b)], axis=0)
    og, sall, pall, (w_gate4,) = _gdn_fwd(qkvc, gb, side=_Gather([shard["w_gate"]]))
    w_gate = _cat_cols(w_gate4)
    ol, stall = _gla_fwd(proj, la)
    mixed = _gated_norm(og, ol, proj, gdn_norm_w, gla_norm_w)
    attn = _mm(mixed, w_out, "nn", "out_proj")
    h1, n2 = _add_norm(h0, attn, p["ffn_norm_w"])
    act, gate, up, (w_down4,) = _ffn_in(n2, w_gate, w_up, side=_Gather([shard["w_down"]]))
    w_down = w_down4.reshape(-1, D)
    ffn = _mm(act, w_down, "nn", "ffn_down", **WHOLE_K)
    dh2, dh2b, lossp, g_final = _final(h1, ffn, loss_target, p["final_norm_w"])

    g_down = _mm(act, dh2b, "tn", "g_w_down", out_dtype=bf16, **WHOLE_K_T)
    dg, du = _ffn_dact(dh2b, w_down, gate, up)
    g_gate = _mm(n2, dg, "tn", "g_w_gate", tm_cap=512, tn_cap=1408, tk_cap=2752, out_dtype=bf16, col_slabs=True)
    g_up = _mm(n2, du, "tn", "g_w_up", tm_cap=512, tn_cap=1408, tk_cap=2752, out_dtype=bf16, col_slabs=True)
    dn2 = _mm(dg, w_gate, "nt", "d_n2_gate", **WHOLE_K)
    dn2 = _mm(du, w_up, "nt", "d_n2_up", acc_in=dn2, **WHOLE_K)
    dh1, dh1b, g_ffn_norm = _norm_bwd(dn2, h1, dh2, p["ffn_norm_w"])
    dmix = _mm(dh1b, w_out, "nt", "d_mixed")
    g_out = _mm(mixed, dh1b, "tn", "g_w_out", out_dtype=bf16, **WHOLE_K_T)
    dog, dol, dproj, g_gdn_norm, g_gla_norm = _gated_norm_bwd(og, ol, proj, gdn_norm_w, gla_norm_w, dmix)
    dproj, dla, (r_down,) = _gla_bwd(proj, la, stall, dol, dproj, side=_Exchange([_row_slabs(g_down)]))
    dqkvc, dgb, (r_gate, r_up, r_out) = _gdn_bwd(qkvc, gb, sall, pall, dog,
                                                 side=_Exchange([g_gate, g_up, _row_slabs(g_out)]))
    dproj, g_conv, (h_out, h_down) = _conv_bwd(
        proj, conv_w, dqkvc, dproj, side=_Siblings([_sum_slots(r_out, "sum_w_out"), _sum_slots(r_down, "sum_w_down")]))
    dpsm, g_w2p, g_gate_b, g_alog, g_dtb = _gates_bwd(psm, w2p, gate_b, alog_p, dtb_p, dgb, dla)
    g_wp, (h_gate, h_up) = _mm(
        n1, dproj, "tn", "g_w_in", out_dtype=bf16,
        side=_Siblings([_sum_slots(r_gate, "sum_w_gate"), _sum_slots(r_up, "sum_w_up")]), **WHOLE_K_T)
    g_wsm = _mm(n1, dpsm, "tn", "g_w_in_small", out_dtype=bf16, **WHOLE_K_T)
    dn1, r_in = _mm(dproj, wp, "nt", "d_n1", side=_Exchange([_row_slabs(g_wp), _row_slabs(g_wsm)]), **WHOLE_K)
    dn1 = _mm(dpsm, wsm, "nt", "d_n1_small", acc_in=dn1)
    s_in = _w_in_columns(_sum_slots(r_in[0], "sum_w_in"), _sum_slots(r_in[1], "sum_w_in_small"))
    in_by_chip = s_in.reshape(s_in.shape[0], N_CHIP, -1).transpose(1, 0, 2)
    grad_x, g_meta, g_attn_norm, (h_in,) = _embed_norm_bwd(dn1, h0, dh1, p["attn_norm_w"], S,
                                                           side=_Exchange([], by_chip=[in_by_chip]))

    received = dict(w_in=h_in, w_gate=h_gate, w_up=h_up, w_out=h_out, w_down=h_down)
    small = dict(
        meta_tokens=g_meta, attn_norm_w=g_attn_norm, gdn_conv_w=g_conv, gdn_a_log=g_alog[:, :GDN_H],
        gdn_dt_bias=g_dtb[:, :GDN_H], gdn_norm_w=g_gdn_norm, gla_gate_w2=g_w2p[SM_LR:SM_LR + GATE_RANK],
        gla_gate_b=g_gate_b, gla_norm_w=g_gla_norm, ffn_norm_w=g_ffn_norm, final_norm_w=g_final)
    return lossp[0, 0], grad_x, received, small


_WEIGHTS = ("meta_tokens", "attn_norm_w", "w_in", "gdn_conv_w", "gdn_a_log", "gdn_dt_bias", "gdn_norm_w",
            "gla_gate_w2", "gla_gate_b", "gla_norm_w", "w_out", "ffn_norm_w", "w_gate", "w_up", "w_down",
            "final_norm_w")
_BIG_COLS = ("w_in", "w_gate", "w_up")
_BIG_ROWS = ("w_out", "w_down")
_SMALL_SHARDED = ("meta_tokens", "gdn_conv_w", "gla_gate_w2")


def kernel(x, meta_tokens, attn_norm_w, w_in, gdn_conv_w, gdn_a_log, gdn_dt_bias, gdn_norm_w, gla_gate_w2, gla_gate_b, gla_norm_w, w_out, ffn_norm_w, w_gate, w_up, w_down, final_norm_w, loss_target, m_meta_tokens, m_attn_norm_w, m_w_in, m_gdn_conv_w, m_gdn_a_log, m_gdn_dt_bias, m_gdn_norm_w, m_gla_gate_w2, m_gla_gate_b, m_gla_norm_w, m_w_out, m_ffn_norm_w, m_w_gate, m_w_up, m_w_down, m_final_norm_w, v_meta_tokens, v_attn_norm_w, v_w_in, v_gdn_conv_w, v_gdn_a_log, v_gdn_dt_bias, v_gdn_norm_w, v_gla_gate_w2, v_gla_gate_b, v_gla_norm_w, v_w_out, v_ffn_norm_w, v_w_gate, v_w_up, v_w_down, v_final_norm_w):
    w = dict(meta_tokens=meta_tokens, attn_norm_w=attn_norm_w, w_in=w_in, gdn_conv_w=gdn_conv_w, gdn_a_log=gdn_a_log,
             gdn_dt_bias=gdn_dt_bias, gdn_norm_w=gdn_norm_w, gla_gate_w2=gla_gate_w2, gla_gate_b=gla_gate_b,
             gla_norm_w=gla_norm_w, w_out=w_out, ffn_norm_w=ffn_norm_w, w_gate=w_gate, w_up=w_up, w_down=w_down,
             final_norm_w=final_norm_w)
    m = dict(meta_tokens=m_meta_tokens, attn_norm_w=m_attn_norm_w, w_in=m_w_in, gdn_conv_w=m_gdn_conv_w,
             gdn_a_log=m_gdn_a_log, gdn_dt_bias=m_gdn_dt_bias, gdn_norm_w=m_gdn_norm_w, gla_gate_w2=m_gla_gate_w2,
             gla_gate_b=m_gla_gate_b, gla_norm_w=m_gla_norm_w, w_out=m_w_out, ffn_norm_w=m_ffn_norm_w,
             w_gate=m_w_gate, w_up=m_w_up, w_down=m_w_down, final_norm_w=m_final_norm_w)
    v = dict(meta_tokens=v_meta_tokens, attn_norm_w=v_attn_norm_w, w_in=v_w_in, gdn_conv_w=v_gdn_conv_w,
             gdn_a_log=v_gdn_a_log, gdn_dt_bias=v_gdn_dt_bias, gdn_norm_w=v_gdn_norm_w, gla_gate_w2=v_gla_gate_w2,
             gla_gate_b=v_gla_gate_b, gla_norm_w=v_gla_norm_w, w_out=v_w_out, ffn_norm_w=v_ffn_norm_w,
             w_gate=v_w_gate, w_up=v_w_up, w_down=v_w_down, final_norm_w=v_final_norm_w)
    chip = 2 * lax.axis_index("x") + lax.axis_index("y")

    def two_d(a):
        return a.reshape(1, -1) if a.ndim == 1 else a.reshape(-1, a.shape[-1])

    w2d = {k: two_d(a) for k, a in w.items()}
    big = _BIG_COLS + _BIG_ROWS
    small = tuple(k for k in _WEIGHTS if k not in big)

    (meta4,), = _comm_now("gather_meta", [_Gather([w2d["meta_tokens"]])])
    shard = {k: w2d[k].astype(bf16) for k in big}
    shard.update({k: w2d[k] for k in ("gdn_conv_w", "gla_gate_w2")})
    lossp, grad_x, received, g = _step(x, loss_target, {k: w2d[k] for k in small}, _cat_cols(meta4), shard)
    loss = lax.psum(lossp, ("x", "y", "c"))

    sizes = [g[k].size for k in small]
    total = sum(sizes)
    rows = -(-total // 1024)
    rows += (-rows) % 8
    packed = jnp.concatenate([g[k].reshape(-1) for k in small] + [jnp.zeros((rows * 1024 - total,), f32)])
    (packed8,), = _comm_now("exchange_small", [_Exchange([], [packed.reshape(rows, 1024)])])
    red = {k: h.reshape(w2d[k].shape) for k, h in received.items()}
    psum_small = _sum_slots(packed8, "sum_small").reshape(-1)
    off = 0
    for k, n in zip(small, sizes):
        a = psum_small[off:off + n].reshape(g[k].shape)
        off += n
        if k in _SMALL_SHARDED:
            c = w2d[k].shape[1]
            a = lax.dynamic_slice_in_dim(a, chip * c, c, axis=1)
        red[k] = a

    grads, deltas, new_m, new_v = [], [], [], []
    for k in _WEIGHTS:
        d, m2, v2 = _adamw(red[k], w2d[k], two_d(m[k]), two_d(v[k]), "adamw_" + k)
        shape = w[k].shape
        grads.append(red[k].reshape(shape))
        deltas.append(d.reshape(shape))
        new_m.append(m2.reshape(shape))
        new_v.append(v2.reshape(shape))
    return (loss, grad_x, *grads, *deltas, *new_m, *new_v)
```

```python
import functools

import jax
import jax.numpy as jnp
from jax import lax
from jax.experimental import pallas as pl
from jax.experimental.pallas import tpu as pltpu

f32 = jnp.float32
bf16 = jnp.bfloat16
HIGH = lax.Precision.HIGH
MESH = pl.DeviceIdType.MESH

N_META = 16
CONV_K = 4
GDN_H, GDN_DK, GDN_DV, GDN_C = 8, 128, 128, 64
GLA_H, GLA_DK, GLA_DV, GLA_C = 4, 128, 256, 16
GATE_RANK = 16
GATE_NORMALIZER = 16.0
EPS = 1e-6
GDN_QK = GDN_H * GDN_DK
GDN_V = GDN_H * GDN_DV
GLA_QK = GLA_H * GLA_DK
GLA_V = GLA_H * GLA_DV
PAD = (-N_META) % GDN_C
OFF = PAD + N_META
ROWS = 64

R_QKV, R_Z, R_AB, R_G, R_GR, R_LR, R_END = 0, 3072, 4096, 4112, 6160, 7184, 7200
C_ZR, C_G, C_QKV, C_END = 0, 2048, 4096, 7168
ZR_W = GDN_V + GLA_V
G_W = 2 * GLA_QK + GLA_V
QKV_W = 2 * GDN_QK + GDN_V
Q0, K0, V0 = 0, GDN_QK, 2 * GDN_QK
SM_W = 128
SM_LR = 2 * GDN_H

ADAM_LR, ADAM_B1, ADAM_B2, ADAM_EPS, ADAM_WD, ADAM_STEP = 0.001, 0.9, 0.999, 1e-08, 0.01, 10

VMEM_LIMIT_V7X = 56 * 1024 * 1024
N_DEV = 8
N_CHIP = 4


def _params(*sem):
    return pltpu.CompilerParams(dimension_semantics=sem, vmem_limit_bytes=VMEM_LIMIT_V7X)


def _tile(n, cap, mult=16):
    best = None
    for d in range(mult, min(n, cap) + 1, mult):
        if n % d == 0:
            best = d
    assert best is not None, (n, cap, mult)
    return best


NN = ((1,), (0,))
NT = ((1,), (1,))
TN = ((0,), (0,))


def _dot(a, b, dims, prec=None):
    return lax.dot_general(a, b, (dims, ((), ())), precision=prec, preferred_element_type=f32)


def _mmb(a, b, dims):
    return _dot(a.astype(bf16), b.astype(bf16), dims)


def _sigmoid(x):
    return jax.nn.sigmoid(x)


def _silu(x):
    return x * _sigmoid(x)


def _dsilu(x):
    s = _sigmoid(x)
    return s * (1.0 + x * (1.0 - s))


def _log1p_exp_neg_abs(x):
    t = jnp.exp(-jnp.abs(x))
    u = 1.0 + t
    d = u - 1.0
    return jnp.where(d == 0.0, t, jnp.log(u) * (t / jnp.where(d == 0.0, 1.0, d)))


def _softplus(x):
    return jnp.maximum(x, 0.0) + _log1p_exp_neg_abs(x)


def _log_sigmoid(x):
    return jnp.minimum(x, 0.0) - _log1p_exp_neg_abs(x)


def _rms(x):
    r = lax.rsqrt(jnp.mean(x * x, axis=-1, keepdims=True) + EPS)
    return x * r, r


def _rms_bwd(dy, xh, r, w):
    t = dy * w
    return r * (t - xh * jnp.mean(t * xh, axis=-1, keepdims=True))


def _l2n(x):
    return x * lax.rsqrt(jnp.sum(x * x, axis=-1, keepdims=True) + EPS)


INV_LEAF = 8


def _same_block(C, b):
    sh = b.bit_length() - 1
    row = lax.broadcasted_iota(jnp.int32, (C, C), 0)
    col = lax.broadcasted_iota(jnp.int32, (C, C), 1)
    return lax.shift_right_logical(row, sh) == lax.shift_right_logical(col, sh)


def _tri_inv_impl(As):
    C = As[0].shape[0]
    R = range(len(As))
    row = lax.broadcasted_iota(jnp.int32, (C, C), 0)
    col = lax.broadcasted_iota(jnp.int32, (C, C), 1)
    eye = (row == col).astype(f32)
    b = INV_LEAF
    inner = _same_block(C, b)
    leaf = [jnp.where(inner, As[h], 0.0) for h in R]
    d = [eye - leaf[h] for h in R]
    pw = leaf
    n = 2
    while n < b:
        pw = [_dot(pw[h], pw[h], NN, HIGH) for h in R]
        d = [_dot(d[h], eye + pw[h], NN, HIGH) for h in R]
        n *= 2
    while b < C:
        outer = _same_block(C, 2 * b)
        level = jnp.logical_and(outer, jnp.logical_not(inner))
        ed = [_dot(jnp.where(level, As[h], 0.0), d[h], NN, HIGH) for h in R]
        d = [d[h] - _dot(d[h], ed[h], NN, HIGH) for h in R]
        inner = outer
        b *= 2
    return d


@jax.custom_vjp
def _tri_inv(As):
    return _tri_inv_impl(As)


def _tri_inv_fwd(As):
    d = _tri_inv_impl(As)
    return d, d


def _tri_inv_bwd(d, g):
    R = range(len(d))
    t = [_dot(d[h], g[h], TN, HIGH) for h in R]
    return ([-_dot(t[h], d[h], NT, HIGH) for h in R],)


_tri_inv.defvjp(_tri_inv_fwd, _tri_inv_bwd)


@jax.custom_vjp
def _tri_inv_known(As, Ps):
    del As
    return Ps


def _tri_inv_known_fwd(As, Ps):
    del As
    return Ps, Ps


def _tri_inv_known_bwd(d, g):
    return _tri_inv_bwd(d, g)[0], [jnp.zeros_like(x) for x in d]


_tri_inv_known.defvjp(_tri_inv_known_fwd, _tri_inv_known_bwd)


def _gdn_chunk(Ss, qrs, krs, vs, betas, gs, Ps=None):
    H = len(Ss)
    C, dk = qrs[0].shape
    R = range(len(qrs))
    row = lax.broadcasted_iota(jnp.int32, (C, C), 0)
    col = lax.broadcasted_iota(jnp.int32, (C, C), 1)
    causal = row >= col
    strict = row > col
    cf = causal.astype(f32)
    q = [_l2n(qrs[h]) * (dk ** -0.5) for h in R]
    k = [_l2n(krs[h]) for h in R]
    mc = [_rows_exact(cf, jnp.broadcast_to(gs[h], (C, C))) for h in R]
    gc = [mc[h][:, 0:1] for h in R]
    decay = [jnp.where(causal, jnp.exp(jnp.where(causal, mc[h] - mc[h].T, 0.0)), 0.0) for h in R]
    kb = [k[h] * betas[h] for h in R]
    a = [jnp.where(strict, _mmb(kb[h], k[h], NT) * decay[h], 0.0) for h in R]
    p = _tri_inv(a) if Ps is None else _tri_inv_known(a, Ps)
    egc = [jnp.exp(gc[h]) for h in R]
    u = [_mmb(p[h], vs[h] * betas[h], NN) for h in R]
    w = [_mmb(p[h], kb[h] * egc[h], NN) for h in R]
    qk = [jnp.where(causal, _mmb(q[h], k[h], NT) * decay[h], 0.0) for h in R]
    qe = [q[h] * egc[h] for h in R]
    gl = [gc[h][C - 1:C, :] for h in R]
    kd = [k[h] * jnp.exp(gl[h] - gc[h]) for h in R]
    egl = [jnp.exp(gl[h]) for h in R]
    S, o, entering = list(Ss), [], []
    for chunk in range(len(qrs) // H):
        idx = [chunk * H + h for h in range(H)]
        entering += S
        v_new = [u[i] - _mmb(w[i], S[h], NN) for h, i in enumerate(idx)]
        o += [_mmb(qe[i], S[h], NN) + _mmb(qk[i], v_new[h], NN) for h, i in enumerate(idx)]
        S = [S[h] * egl[i] + _mmb(kd[i], v_new[h], TN) for h, i in enumerate(idx)]
    return S, o, p, entering


def _rows_exact_impl(m01, x, dims):
    m = m01.astype(bf16)
    x1 = x.astype(bf16)
    r1 = x - x1.astype(f32)
    x2 = r1.astype(bf16)
    x3 = (r1 - x2.astype(f32)).astype(bf16)
    d = lambda y: _dot(m, y, dims)
    return d(x1) + (d(x2) + d(x3))


@jax.custom_vjp
def _rows_exact(m01, x):
    return _rows_exact_impl(m01, x, NN)


def _rows_exact_fwd(m01, x):
    return _rows_exact_impl(m01, x, NN), m01


def _rows_exact_bwd(m01, g):
    return jnp.zeros_like(m01), _rows_exact_impl(m01, g, TN)


_rows_exact.defvjp(_rows_exact_fwd, _rows_exact_bwd)


def _gla_blocks(Sts, qrs, ks, vs, las):
    H = len(Sts)
    n = len(qrs)
    C, dk = qrs[0].shape
    R = range(n)
    row = lax.broadcasted_iota(jnp.int32, (C, C), 0)
    col = lax.broadcasted_iota(jnp.int32, (C, C), 1)
    ri = lax.broadcasted_iota(jnp.int32, (C, dk), 0)
    q = [qrs[h] * (dk ** -0.5) for h in R]
    running = (row >= col).astype(f32)
    b = [_rows_exact(running, las[h]) for h in R]
    sc = [jnp.where(row == col, jnp.sum(q[h] * ks[h], axis=-1, keepdims=True), 0.0) for h in R]
    s = C // 2
    while s >= 1:
        sh = s.bit_length() - 1
        ref = lax.shift_left(lax.shift_right_logical(row, sh + 1), sh + 1) + (s - 1)
        pick = (col == ref).astype(f32)
        bref = [_rows_exact(pick, b[h]) for h in R]
        upper = (lax.shift_right_logical(ri, sh) & 1) == 1
        qt = [jnp.where(upper, q[h] * jnp.exp(jnp.where(upper, b[h] - bref[h], 0.0)), 0.0) for h in R]
        kt = [jnp.where(upper, 0.0, ks[h] * jnp.exp(jnp.where(upper, 0.0, bref[h] - b[h]))) for h in R]
        same = lax.shift_right_logical(row, sh + 1) == lax.shift_right_logical(col, sh + 1)
        sc = [sc[h] + jnp.where(same, _mmb(qt[h], kt[h], NT), 0.0) for h in R]
        s //= 2
    o = [_mmb(sc[h], vs[h], NN) for h in R]
    qe = [q[h] * jnp.exp(b[h]) for h in R]
    bl = [b[h][C - 1:C, :] for h in R]
    upd = [_mmb(vs[h], ks[h] * jnp.exp(bl[h] - b[h]), TN) for h in R]
    ebl = [jnp.exp(bl[h]) for h in R]
    St = list(Sts)
    for blk in range(n // H):
        for h in range(H):
            i = blk * H + h
            o[i] = o[i] + _mmb(qe[i], St[h], NT)
        St = [St[h] * ebl[blk * H + h] + upd[blk * H + h] for h in range(H)]
    return St, o


_ANY = pl.BlockSpec(memory_space=pl.ANY)


class _Gather:
    def __init__(self, arrays):
        self.arrays = list(arrays)
        self.n = len(self.arrays)
        self.out_shape = [jax.ShapeDtypeStruct((N_CHIP,) + a.shape, a.dtype) for a in self.arrays]
        self.sems = [pltpu.SemaphoreType.DMA((self.n, 3)), pltpu.SemaphoreType.DMA((self.n, 3)),
                     pltpu.SemaphoreType.DMA((self.n,))]

    def hooks(self, ins, outs, send, recv, lsem):
        def copies():
            x, y, c = lax.axis_index("x"), lax.axis_index("y"), lax.axis_index("c")
            me = 2 * x + y
            out = []
            for a in range(self.n):
                out.append((pltpu.make_async_copy(ins[a], outs[a].at[me], lsem.at[a]), None))
                for j, (px, py) in enumerate([(1 - x, y), (x, 1 - y), (1 - x, 1 - y)]):
                    mk = lambda dst, a=a, j=j, px=px, py=py: pltpu.make_async_remote_copy(
                        src_ref=ins[a], dst_ref=dst, send_sem=send.at[a, j], recv_sem=recv.at[a, j],
                        device_id=(px, py, c), device_id_type=MESH)
                    out.append((mk(outs[a].at[me]), mk(outs[a].at[2 * px + py])))
            return out

        return _start_wait(copies)


class _Exchange:
    def __init__(self, slotted, shared=(), by_chip=()):
        self.arrays = list(slotted) + list(by_chip) + list(shared)
        self.ns, self.nc = len(slotted), len(by_chip)
        self.n = len(self.arrays)
        self.out_shape = [jax.ShapeDtypeStruct(a.shape, a.dtype) for a in slotted]
        self.out_shape += [jax.ShapeDtypeStruct((N_DEV,) + a.shape[1:], a.dtype) for a in by_chip]
        self.out_shape += [jax.ShapeDtypeStruct((N_DEV,) + b.shape, b.dtype) for b in shared]
        self.sems = [pltpu.SemaphoreType.DMA((self.n, N_DEV - 1)), pltpu.SemaphoreType.DMA((self.n, N_DEV - 1)),
                     pltpu.SemaphoreType.DMA((self.n,))]

    def hooks(self, ins, outs, send, recv, lsem):
        def copies():
            x, y, c = lax.axis_index("x"), lax.axis_index("y"), lax.axis_index("c")
            me = 4 * x + 2 * y + c

            def src(a, dev):
                tx, ty, tc = dev
                if a < self.ns:
                    return ins[a].at[4 * tx + 2 * ty + tc]
                return ins[a].at[2 * tx + ty] if a < self.ns + self.nc else ins[a]

            out = []
            for a in range(self.n):
                out.append((pltpu.make_async_copy(src(a, (x, y, c)), outs[a].at[me], lsem.at[a]), None))
                for o in range(1, N_DEV):
                    dev = (1 - x if o & 4 else x, 1 - y if o & 2 else y, 1 - c if o & 1 else c)
                    t = 4 * dev[0] + 2 * dev[1] + dev[2]
                    mk = lambda dst, a=a, o=o, dev=dev: pltpu.make_async_remote_copy(
                        src_ref=src(a, dev), dst_ref=dst, send_sem=send.at[a, o - 1], recv_sem=recv.at[a, o - 1],
                        device_id=dev, device_id_type=MESH)
                    out.append((mk(outs[a].at[me]), mk(outs[a].at[t])))
            return out

        return _start_wait(copies)


class _Sides:
    def __init__(self, *members):
        self.members = members
        self.arrays = [a for s in members for a in s.arrays]
        self.n = len(self.arrays)
        self.out_shape = [sh for s in members for sh in s.out_shape]
        self.sems = [sm for s in members for sm in s.sems]

    def hooks(self, ins, outs, *sems):
        hooks, o = [], 0
        for i, s in enumerate(self.members):
            hooks.append(s.hooks(ins[o:o + s.n], outs[o:o + s.n], *sems[3 * i:3 * i + 3]))
            o += s.n

        def start():
            for st, _ in hooks:
                st()

        def wait():
            for _, wt in hooks:
                wt()

        return start, wait


def _start_wait(copies):
    def start():
        for s, _ in copies():
            s.start()

    def wait():
        for s, w in copies():
            (s if w is None else w).wait()

    return start, wait


def _call(body, *, name, grid, in_specs, out_specs, out_shape, args, sem, scratch_shapes=(), aliases=None, side=None):
    in_specs, out_specs, out_shape, args = list(in_specs), list(out_specs), list(out_shape), list(args)
    scratch_shapes = list(scratch_shapes)
    aliases = aliases or {}
    if side is None:
        return pl.pallas_call(
            body, name=name, grid=grid, in_specs=in_specs, out_specs=out_specs, out_shape=out_shape,
            scratch_shapes=scratch_shapes, input_output_aliases=aliases, compiler_params=_params(*sem))(*args)
    n_in, n_out, n_scr, ns = len(in_specs), len(out_specs), len(scratch_shapes), side.n

    def full_body(*refs):
        ins, refs = refs[:n_in], refs[n_in:]
        s_in, refs = refs[:ns], refs[ns:]
        outs, refs = refs[:n_out], refs[n_out:]
        s_out, refs = refs[:ns], refs[ns:]
        scr, sems = refs[:n_scr], refs[n_scr:]
        start, wait = side.hooks(s_in, s_out, *sems)
        ids = [pl.program_id(d) for d in range(len(grid))]
        first = functools.reduce(jnp.logical_and, [i == 0 for i in ids])
        last = functools.reduce(jnp.logical_and, [i == g - 1 for i, g in zip(ids, grid)])
        pl.when(first)(start)
        body(*ins, *outs, *scr)
        pl.when(last)(wait)

    return pl.pallas_call(
        full_body, name=name, grid=grid, in_specs=in_specs + [_ANY] * ns, out_specs=out_specs + [_ANY] * ns,
        out_shape=out_shape + side.out_shape, scratch_shapes=scratch_shapes + side.sems,
        input_output_aliases=aliases, compiler_params=_params(*(["arbitrary"] * len(grid))))(*args, *side.arrays)


WHOLE_K = dict(tm_cap=688, tn_cap=512, tk_cap=1 << 20)
WHOLE_K_T = dict(tm_cap=512, tn_cap=512, tk_cap=1 << 20)

def _mm(a, b, mode, name, *, tm_cap=1408, tn_cap=1024, tk_cap=2048, out_dtype=f32, acc_in=None, side=None,
        col_slabs=False):
    if mode == "nn":
        (M, K), (K2, N) = a.shape, b.shape
    elif mode == "nt":
        (M, K), (N, K2) = a.shape, b.shape
    else:
        (K, M), (K2, N) = a.shape, b.shape
    assert K == K2, (name, a.shape, b.shape)
    tm = _tile(M // 2 if col_slabs else M, tm_cap)
    tn = _tile(N // N_CHIP if col_slabs else N, tn_cap, 128)
    tk = _tile(K, tk_cap, 128 if K % 128 == 0 else 16)
    nk = K // tk
    dims = {"nn": NN, "nt": NT, "tn": TN}[mode]
    use_scratch = nk > 1 and out_dtype != f32

    def body(*refs):
        if acc_in is not None:
            a_ref, b_ref, c_ref, o_ref, *scr = refs
        else:
            a_ref, b_ref, o_ref, *scr = refs
            c_ref = None
        p = _mmb(a_ref[...], b_ref[...], dims)
        if nk == 1:
            if c_ref is not None:
                p = p + c_ref[...]
            o_ref[...] = p.astype(out_dtype)
            return
        k = pl.program_id(2)
        acc = scr[0] if use_scratch else o_ref

        @pl.when(k == 0)
        def _():
            acc[...] = p if c_ref is None else p + c_ref[...]

        @pl.when(k > 0)
        def _():
            acc[...] += p

        if use_scratch:
            @pl.when(k == nk - 1)
            def _():
                o_ref[...] = acc[...].astype(out_dtype)

    if mode == "tn":
        a_spec = pl.BlockSpec((tk, tm), lambda i, j, k: (k, i))
    else:
        a_spec = pl.BlockSpec((tm, tk), lambda i, j, k: (i, k))
    if mode == "nt":
        b_spec = pl.BlockSpec((tn, tk), lambda i, j, k: (j, k))
    else:
        b_spec = pl.BlockSpec((tk, tn), lambda i, j, k: (k, j))
    if col_slabs:
        assert acc_in is None
        ni, nj = M // 2 // tm, N // N_CHIP // tn
        o_spec = pl.BlockSpec((None, tm, tn), lambda i, j, k: (2 * (j // nj) + i // ni, i % ni, j % nj))
        o_shape = jax.ShapeDtypeStruct((N_DEV, M // 2, N // N_CHIP), out_dtype)
    else:
        o_spec = pl.BlockSpec((tm, tn), lambda i, j, k: (i, j))
        o_shape = jax.ShapeDtypeStruct((M, N), out_dtype)
    in_specs = [a_spec, b_spec]
    args = [a, b]
    if acc_in is not None:
        in_specs.append(o_spec)
        args.append(acc_in)
    out = _call(body, name=name, grid=(M // tm, N // tn, nk), in_specs=in_specs, out_specs=[o_spec],
                out_shape=[o_shape], args=args,
                scratch_shapes=[pltpu.VMEM((tm, tn), f32)] if use_scratch else [],
                sem=("parallel", "parallel", "arbitrary"), side=side)
    return out[0] if side is None else (out[0], out[1:])


def _embed_norm(x3, m64, w, side=None):
    _, S, D = x3.shape
    Lp = OFF + S

    def body(x_ref, m_ref, w_ref, h_ref, n_ref):
        i = pl.program_id(0)
        h = jnp.where(i == 0, m_ref[...], x_ref[...])
        h_ref[...] = h
        xh, _ = _rms(h)
        n_ref[...] = (xh * w_ref[...]).astype(bf16)

    row = pl.BlockSpec((ROWS, D), lambda i: (i, 0))
    out = _call(
        body, name="embed_norm", grid=(Lp // ROWS,),
        in_specs=[pl.BlockSpec((None, ROWS, D), lambda i: (0, jnp.maximum(i - 1, 0), 0)),
                  pl.BlockSpec((ROWS, D), lambda i: (0, 0)),
                  pl.BlockSpec((1, D), lambda i: (0, 0))],
        out_specs=[row, row],
        out_shape=[jax.ShapeDtypeStruct((Lp, D), f32), jax.ShapeDtypeStruct((Lp, D), bf16)],
        args=[x3, m64, w], sem=("parallel",), side=side)
    return out[0], out[1], out[2:]


def _add_norm(h, d, w):
    Lp, D = h.shape
    tr = _tile(Lp, 256)

    def body(h_ref, d_ref, w_ref, o_ref, n_ref):
        h1 = h_ref[...] + d_ref[...]
        o_ref[...] = h1
        xh, _ = _rms(h1)
        n_ref[...] = (xh * w_ref[...]).astype(bf16)

    row = pl.BlockSpec((tr, D), lambda i: (i, 0))
    return pl.pallas_call(
        body, name="add_norm", grid=(Lp // tr,),
        in_specs=[row, row, pl.BlockSpec((1, D), lambda i: (0, 0))], out_specs=[row, row],
        out_shape=[jax.ShapeDtypeStruct((Lp, D), f32), jax.ShapeDtypeStruct((Lp, D), bf16)],
        compiler_params=_params("parallel"),
    )(h, d, w)


def _norm_bwd(dn, h, dh, w):
    Lp, D = h.shape
    tr = _tile(Lp, 256)

    def body(dn_ref, h_ref, dh_ref, w_ref, o_ref, ob_ref, gw_ref):
        i = pl.program_id(0)
        xh, r = _rms(h_ref[...])
        dn_ = dn_ref[...]
        o = dh_ref[...] + _rms_bwd(dn_, xh, r, w_ref[...])
        o_ref[...] = o
        ob_ref[...] = o.astype(bf16)
        gw = jnp.sum(dn_ * xh, axis=0, keepdims=True)

        @pl.when(i == 0)
        def _():
            gw_ref[...] = gw

        @pl.when(i > 0)
        def _():
            gw_ref[...] += gw

    row = pl.BlockSpec((tr, D), lambda i: (i, 0))
    vec = pl.BlockSpec((1, D), lambda i: (0, 0))
    return pl.pallas_call(
        body, name="norm_bwd", grid=(Lp // tr,), in_specs=[row, row, row, vec], out_specs=[row, row, vec],
        out_shape=[jax.ShapeDtypeStruct((Lp, D), f32), jax.ShapeDtypeStruct((Lp, D), bf16),
                   jax.ShapeDtypeStruct((1, D), f32)],
        compiler_params=_params("arbitrary"),
    )(dn, h, dh, w)


def _embed_norm_bwd(dn, h, dh, w, S, side=None):
    Lp, D = h.shape

    def body(dn_ref, h_ref, dh_ref, w_ref, gx_ref, gm_ref, gw_ref):
        i = pl.program_id(0)
        xh, r = _rms(h_ref[...])
        dn_ = dn_ref[...]
        d0 = dh_ref[...] + _rms_bwd(dn_, xh, r, w_ref[...])
        gx_ref[...] = d0
        gw = jnp.sum(dn_ * xh, axis=0, keepdims=True)

        @pl.when(i == 0)
        def _():
            gm_ref[...] = d0[PAD:OFF, :]
            gw_ref[...] = gw

        @pl.when(i > 0)
        def _():
            gw_ref[...] += gw

    row = pl.BlockSpec((ROWS, D), lambda i: (i, 0))
    vec = pl.BlockSpec((1, D), lambda i: (0, 0))
    out = _call(
        body, name="embed_norm_bwd", grid=(Lp // ROWS,), in_specs=[row, row, row, vec],
        out_specs=[pl.BlockSpec((None, ROWS, D), lambda i: (0, jnp.maximum(i - 1, 0), 0)),
                   pl.BlockSpec((N_META, D), lambda i: (0, 0)), vec],
        out_shape=[jax.ShapeDtypeStruct((1, S, D), f32), jax.ShapeDtypeStruct((N_META, D), f32),
                   jax.ShapeDtypeStruct((1, D), f32)],
        args=[dn, h, dh, w], sem=("arbitrary",), side=side)
    return out[0], out[1], out[2], out[3:]


def _final(h1, ffn, tgt3, w):
    Lp, D = h1.shape

    def body(h_ref, f_ref, t_ref, w_ref, d_ref, db_ref, l_ref, gw_ref):
        i = pl.program_id(0)
        h2 = h_ref[...] + f_ref[...]
        xh, r = _rms(h2)
        w_ = w_ref[...]
        e = xh * w_ - t_ref[...]
        valid = (i > 0).astype(f32)
        loss = 0.5 * jnp.sum(jnp.mean(e * e, axis=-1, keepdims=True), axis=0, keepdims=True) * valid
        dy = e * (valid / D)
        d = _rms_bwd(dy, xh, r, w_)
        d_ref[...] = d
        db_ref[...] = d.astype(bf16)
        gw = jnp.sum(dy * xh, axis=0, keepdims=True)

        @pl.when(i == 0)
        def _():
            l_ref[...] = jnp.zeros_like(l_ref)
            gw_ref[...] = jnp.zeros_like(gw_ref)

        l_ref[...] += jnp.broadcast_to(loss, l_ref.shape)
        gw_ref[...] += gw

    row = pl.BlockSpec((ROWS, D), lambda i: (i, 0))
    vec = pl.BlockSpec((1, D), lambda i: (0, 0))
    return pl.pallas_call(
        body, name="final_loss", grid=(Lp // ROWS,),
        in_specs=[row, row, pl.BlockSpec((None, ROWS, D), lambda i: (0, jnp.maximum(i - 1, 0), 0)), vec],
        out_specs=[row, row, pl.BlockSpec((8, 128), lambda i: (0, 0)), vec],
        out_shape=[jax.ShapeDtypeStruct((Lp, D), f32), jax.ShapeDtypeStruct((Lp, D), bf16),
                   jax.ShapeDtypeStruct((8, 128), f32), jax.ShapeDtypeStruct((1, D), f32)],
        compiler_params=_params("arbitrary"),
    )(h1, ffn, tgt3, w)


def _ffn_in(n, w_gate, w_up, side=None):
    M, K = n.shape
    F = w_gate.shape[1]
    tm = _tile(M, 1408)
    tn = _tile(F, 512, 128)

    def body(a_ref, bg_ref, bu_ref, act_ref, pg_ref, pu_ref):
        a = a_ref[...]
        g = _mmb(a, bg_ref[...], NN)
        u = _mmb(a, bu_ref[...], NN)
        s = _sigmoid(g)
        gs = g * s
        act_ref[...] = (gs * u).astype(bf16)
        pg_ref[...] = (u * (s + gs * (1.0 - s))).astype(bf16)
        pu_ref[...] = gs.astype(bf16)

    wsp = pl.BlockSpec((K, tn), lambda i, j: (0, j))
    osp = pl.BlockSpec((tm, tn), lambda i, j: (i, j))
    out = _call(body, name="ffn_in", grid=(M // tm, F // tn),
                in_specs=[pl.BlockSpec((tm, K), lambda i, j: (i, 0)), wsp, wsp], out_specs=[osp] * 3,
                out_shape=[jax.ShapeDtypeStruct((M, F), bf16)] * 3, args=[n, w_gate, w_up],
                sem=("parallel", "parallel"), side=side)
    return out[0], out[1], out[2], out[3:]


def _ffn_dact(d, w_down, pg, pu):
    M, K = d.shape
    F = w_down.shape[0]
    tm = _tile(M, 1408)
    tn = _tile(F, 512, 128)

    def body(d_ref, w_ref, pg_ref, pu_ref, dg_ref, du_ref):
        da = _mmb(d_ref[...], w_ref[...], NT)
        dg_ref[...] = (da * pg_ref[...].astype(f32)).astype(bf16)
        du_ref[...] = (da * pu_ref[...].astype(f32)).astype(bf16)

    osp = pl.BlockSpec((tm, tn), lambda i, j: (i, j))
    return pl.pallas_call(
        body, name="ffn_dact", grid=(M // tm, F // tn),
        in_specs=[pl.BlockSpec((tm, K), lambda i, j: (i, 0)), pl.BlockSpec((tn, K), lambda i, j: (j, 0)), osp, osp],
        out_specs=[osp, osp], out_shape=[jax.ShapeDtypeStruct((M, F), bf16)] * 2,
        compiler_params=_params("parallel", "parallel"),
    )(d, w_down, pg, pu)


def _gates(psm, w2p, gate_b, alog, dtb):
    Lp = psm.shape[0]
    tr = _tile(Lp, 256)

    def body(p_ref, w_ref, b_ref, a_ref, t_ref, gb_ref, la_ref):
        i = pl.program_id(0)
        psm_ = p_ref[...]
        lane = lax.broadcasted_iota(jnp.int32, psm_.shape, 1)
        rowi = lax.broadcasted_iota(jnp.int32, (tr, 1), 0) + i * tr
        g = -jnp.exp(a_ref[...]) * _softplus(psm_ + t_ref[...])
        beta = _sigmoid(psm_)
        gb = jnp.where(lane < GDN_H, g, jnp.where(lane < 2 * GDN_H, beta, 0.0))
        gb_ref[...] = gb * (rowi >= PAD).astype(f32)
        logit = _mmb(psm_, w_ref[...], NN) + b_ref[...]
        la_ref[...] = _log_sigmoid(logit) * (1.0 / GATE_NORMALIZER)

    row = pl.BlockSpec((tr, SM_W), lambda i: (i, 0))
    return pl.pallas_call(
        body, name="gates", grid=(Lp // tr,),
        in_specs=[row, pl.BlockSpec((SM_W, GLA_QK), lambda i: (0, 0)), pl.BlockSpec((1, GLA_QK), lambda i: (0, 0)),
                  pl.BlockSpec((1, SM_W), lambda i: (0, 0)), pl.BlockSpec((1, SM_W), lambda i: (0, 0))],
        out_specs=[row, pl.BlockSpec((tr, GLA_QK), lambda i: (i, 0))],
        out_shape=[jax.ShapeDtypeStruct((Lp, SM_W), f32), jax.ShapeDtypeStruct((Lp, GLA_QK), f32)],
        compiler_params=_params("parallel"),
    )(psm, w2p, gate_b, alog, dtb)


def _gates_bwd(psm, w2p, gate_b, alog, dtb, dgb, dla):
    Lp = psm.shape[0]
    tr = _tile(Lp, 256)

    def body(p_ref, w_ref, b_ref, a_ref, t_ref, dgb_ref, dla_ref, dp_ref, gw_ref, gb_ref, ga_ref, gt_ref):
        i = pl.program_id(0)
        psm_ = p_ref[...]
        lane = lax.broadcasted_iota(jnp.int32, psm_.shape, 1)
        rowi = lax.broadcasted_iota(jnp.int32, (tr, 1), 0) + i * tr
        d = dgb_ref[...] * (rowi >= PAD).astype(f32)
        ea = jnp.exp(a_ref[...])
        z = psm_ + t_ref[...]
        is_g = lane < GDN_H
        dz = jnp.where(is_g, -ea * _sigmoid(z) * d, 0.0)
        dalog = jnp.where(is_g, -ea * _softplus(z) * d, 0.0)
        beta = _sigmoid(psm_)
        dbeta = jnp.where(jnp.logical_and(lane >= GDN_H, lane < 2 * GDN_H), beta * (1.0 - beta) * d, 0.0)
        logit = _mmb(psm_, w_ref[...], NN) + b_ref[...]
        dlogit = dla_ref[...] * (_sigmoid(-logit) * (1.0 / GATE_NORMALIZER))
        dlr = _mmb(dlogit, w_ref[...], NT)
        dp_ref[...] = (dz + dbeta + dlr).astype(bf16)
        gw = _mmb(psm_, dlogit, TN)
        gb = jnp.sum(dlogit, axis=0, keepdims=True)
        ga = jnp.sum(dalog, axis=0, keepdims=True)
        gt = jnp.sum(dz, axis=0, keepdims=True)

        @pl.when(i == 0)
        def _():
            gw_ref[...] = gw
            gb_ref[...] = gb
            ga_ref[...] = ga
            gt_ref[...] = gt

        @pl.when(i > 0)
        def _():
            gw_ref[...] += gw
            gb_ref[...] += gb
            ga_ref[...] += ga
            gt_ref[...] += gt

    row = pl.BlockSpec((tr, SM_W), lambda i: (i, 0))
    wsp = pl.BlockSpec((SM_W, GLA_QK), lambda i: (0, 0))
    bsp = pl.BlockSpec((1, GLA_QK), lambda i: (0, 0))
    vsp = pl.BlockSpec((1, SM_W), lambda i: (0, 0))
    return pl.pallas_call(
        body, name="gates_bwd", grid=(Lp // tr,),
        in_specs=[row, wsp, bsp, vsp, vsp, row, pl.BlockSpec((tr, GLA_QK), lambda i: (i, 0))],
        out_specs=[row, wsp, bsp, vsp, vsp],
        out_shape=[jax.ShapeDtypeStruct((Lp, SM_W), bf16), jax.ShapeDtypeStruct((SM_W, GLA_QK), f32),
                   jax.ShapeDtypeStruct((1, GLA_QK), f32), jax.ShapeDtypeStruct((1, SM_W), f32),
                   jax.ShapeDtypeStruct((1, SM_W), f32)],
        compiler_params=_params("arbitrary"),
    )(psm, w2p, gate_b, alog, dtb, dgb, dla)


def _conv_pre(x_ext, w, n):
    rows = x_ext.shape[0]
    y = x_ext * w[CONV_K - 1:CONV_K, :]
    for s in range(1, CONV_K):
        y = y + pltpu.roll(x_ext, s, 0) * w[CONV_K - 1 - s:CONV_K - s, :]
    return y[rows - n:, :]


def _conv(proj, cw, side=None):
    Lp = proj.shape[0]
    W = cw.shape[1]
    tr = _tile(Lp, 256, 64)
    tc = _tile(W, 1024, 128)
    c0 = C_QKV // tc

    def body(h_ref, x_ref, w_ref, o_ref):
        i = pl.program_id(1)
        halo = jnp.where(i == 0, 0.0, h_ref[...])
        x_ext = jnp.concatenate([halo, x_ref[...]], axis=0)
        o_ref[...] = _silu(_conv_pre(x_ext, w_ref[...], tr))

    out = _call(
        body, name="conv", grid=(W // tc, Lp // tr),
        in_specs=[pl.BlockSpec((8, tc), lambda j, i: (jnp.maximum(i * (tr // 8) - 1, 0), j + c0)),
                  pl.BlockSpec((tr, tc), lambda j, i: (i, j + c0)),
                  pl.BlockSpec((CONV_K, tc), lambda j, i: (0, j))],
        out_specs=[pl.BlockSpec((tr, tc), lambda j, i: (i, j))],
        out_shape=[jax.ShapeDtypeStruct((Lp, W), f32)], args=[proj, proj, cw],
        sem=("parallel", "parallel"), side=side)
    return out[0] if side is None else (out[0], out[1:])


def _conv_bwd(proj, cw, dy, dproj, side=None):
    Lp = proj.shape[0]
    W = cw.shape[1]
    tr = _tile(Lp, 256, 64)
    tc = _tile(W, 1024, 128)
    c0 = C_QKV // tc
    nr = Lp // tr
    last8 = Lp // 8 - 1

    def body(xp_ref, x_ref, xn_ref, w_ref, d_ref, dn_ref, dproj_ref, o_ref, gw_ref):
        del dproj_ref
        i = pl.program_id(1)
        w = w_ref[...]
        xp = jnp.where(i == 0, 0.0, xp_ref[...])
        x_ext = jnp.concatenate([xp, x_ref[...], xn_ref[...]], axis=0)
        n = tr + 8
        pre = _conv_pre(x_ext, w, n)
        dn = jnp.where(i == nr - 1, 0.0, dn_ref[...])
        dpre = jnp.concatenate([d_ref[...], dn], axis=0) * _dsilu(pre)
        dx = dpre * w[CONV_K - 1:CONV_K, :]
        for s in range(1, CONV_K):
            dx = dx + pltpu.roll(dpre, n - s, 0) * w[CONV_K - 1 - s:CONV_K - s, :]
        o_ref[...] = dx[:tr, :].astype(bf16)
        dp = dpre[:tr, :]
        rows = []
        for k in range(CONV_K):
            xs = x_ext if k == CONV_K - 1 else pltpu.roll(x_ext, CONV_K - 1 - k, 0)
            rows.append(jnp.sum(dp * xs[8:8 + tr, :], axis=0, keepdims=True))
        gw = jnp.concatenate(rows, axis=0)

        @pl.when(i == 0)
        def _():
            gw_ref[...] = gw

        @pl.when(i > 0)
        def _():
            gw_ref[...] += gw

    cur = pl.BlockSpec((tr, tc), lambda j, i: (i, j))
    nxt = pl.BlockSpec((8, tc), lambda j, i: (jnp.minimum((i + 1) * (tr // 8), last8), j))
    pcur = pl.BlockSpec((tr, tc), lambda j, i: (i, j + c0))
    pprev = pl.BlockSpec((8, tc), lambda j, i: (jnp.maximum(i * (tr // 8) - 1, 0), j + c0))
    pnext = pl.BlockSpec((8, tc), lambda j, i: (jnp.minimum((i + 1) * (tr // 8), last8), j + c0))
    wsp = pl.BlockSpec((CONV_K, tc), lambda j, i: (0, j))
    out = _call(
        body, name="conv_bwd", grid=(W // tc, nr),
        in_specs=[pprev, pcur, pnext, wsp, cur, nxt, _ANY], out_specs=[pcur, wsp],
        out_shape=[jax.ShapeDtypeStruct(dproj.shape, dproj.dtype), jax.ShapeDtypeStruct((CONV_K, W), f32)],
        aliases={6: 0}, args=[proj, proj, proj, cw, dy, dy, dproj], sem=("parallel", "arbitrary"), side=side)
    return out[0], out[1], out[2:]


GDN_FWD_GROUP = 3


def _gdn_group(Lp, most):
    n = Lp // GDN_C
    return next(g for g in range(most, 0, -1) if n % g == 0)


def _gdn_heads(x_ref, gb_ref, group):
    qs, ks, vs, bs, gs = [], [], [], [], []
    for chunk in range(group):
        r = slice(chunk * GDN_C, (chunk + 1) * GDN_C)
        gbv = gb_ref[r, :]
        for h in range(GDN_H):
            qs.append(x_ref[r, Q0 + h * GDN_DK:Q0 + (h + 1) * GDN_DK])
            ks.append(x_ref[r, K0 + h * GDN_DK:K0 + (h + 1) * GDN_DK])
            vs.append(x_ref[r, V0 + h * GDN_DV:V0 + (h + 1) * GDN_DV])
            bs.append(gbv[:, GDN_H + h:GDN_H + h + 1])
            gs.append(gbv[:, h:h + 1])
    return qs, ks, vs, bs, gs


def _gdn_fwd(qkvc, gb, side=None):
    Lp = qkvc.shape[0]
    group = _gdn_group(Lp, GDN_FWD_GROUP)
    rows = group * GDN_C
    steps = Lp // rows
    R = range(GDN_H)

    def body(x_ref, gb_ref, o_ref, sall_ref, pall_ref, s_scr):
        @pl.when(pl.program_id(0) == 0)
        def _():
            s_scr[...] = jnp.zeros_like(s_scr)

        S2, o, p, entering = _gdn_chunk([s_scr[h] for h in R], *_gdn_heads(x_ref, gb_ref, group))
        for h in R:
            s_scr[h] = S2[h]
        for chunk in range(group):
            for h in R:
                i = chunk * GDN_H + h
                o_ref[chunk * GDN_C:(chunk + 1) * GDN_C, h * GDN_DV:(h + 1) * GDN_DV] = o[i]
                pall_ref[chunk, h] = p[i]
                sall_ref[chunk, h] = entering[i]

    out = _call(
        body, name="gdn_fwd", grid=(steps,),
        in_specs=[pl.BlockSpec((rows, QKV_W), lambda n: (n, 0)), pl.BlockSpec((rows, SM_W), lambda n: (n, 0))],
        out_specs=[pl.BlockSpec((rows, GDN_V), lambda n: (n, 0)),
                   pl.BlockSpec((group, GDN_H, GDN_DK, GDN_DV), lambda n: (n, 0, 0, 0)),
                   pl.BlockSpec((group, GDN_H, GDN_C, GDN_C), lambda n: (n, 0, 0, 0))],
        out_shape=[jax.ShapeDtypeStruct((Lp, GDN_V), f32),
                   jax.ShapeDtypeStruct((Lp // GDN_C, GDN_H, GDN_DK, GDN_DV), f32),
                   jax.ShapeDtypeStruct((Lp // GDN_C, GDN_H, GDN_C, GDN_C), f32)],
        scratch_shapes=[pltpu.VMEM((GDN_H, GDN_DK, GDN_DV), f32)], args=[qkvc, gb], sem=("arbitrary",), side=side)
    return out[0], out[1], out[2], out[3:]


def _gdn_bwd(qkvc, gb, sall, pall, do, side=None):
    Lp = qkvc.shape[0]
    group = 1
    rows = group * GDN_C
    steps = Lp // rows
    R = range(GDN_H)

    def body(x_ref, gb_ref, sall_ref, pall_ref, do_ref, dx_ref, dgb_ref, ds_scr):
        @pl.when(pl.program_id(0) == 0)
        def _():
            ds_scr[...] = jnp.zeros_like(ds_scr)

        lane = lax.broadcasted_iota(jnp.int32, (GDN_C, SM_W), 1)
        ps = [pall_ref[chunk, h] for chunk in range(group) for h in R]
        _, vjp = jax.vjp(lambda *a: _gdn_chunk(*a, Ps=ps)[:2],
                         [sall_ref[0, h] for h in R], *_gdn_heads(x_ref, gb_ref, group))
        do = [do_ref[chunk * GDN_C:(chunk + 1) * GDN_C, h * GDN_DV:(h + 1) * GDN_DV]
              for chunk in range(group) for h in R]
        dS, dq, dk, dv, dbeta, dg = vjp(([ds_scr[h] for h in R], do))
        for h in R:
            ds_scr[h] = dS[h]
        for chunk in range(group):
            r = slice(chunk * GDN_C, (chunk + 1) * GDN_C)
            acc = jnp.zeros((GDN_C, SM_W), f32)
            for h in R:
                i = chunk * GDN_H + h
                dx_ref[r, Q0 + h * GDN_DK:Q0 + (h + 1) * GDN_DK] = dq[i]
                dx_ref[r, K0 + h * GDN_DK:K0 + (h + 1) * GDN_DK] = dk[i]
                dx_ref[r, V0 + h * GDN_DV:V0 + (h + 1) * GDN_DV] = dv[i]
                acc = acc + jnp.where(lane == h, dg[i], 0.0) + jnp.where(lane == GDN_H + h, dbeta[i], 0.0)
            dgb_ref[r, :] = acc

    rev = lambda n: (steps - 1 - n, 0)
    out = _call(
        body, name="gdn_bwd", grid=(steps,),
        in_specs=[pl.BlockSpec((rows, QKV_W), rev), pl.BlockSpec((rows, SM_W), rev),
                  pl.BlockSpec((1, GDN_H, GDN_DK, GDN_DV), lambda n: (steps - 1 - n, 0, 0, 0)),
                  pl.BlockSpec((group, GDN_H, GDN_C, GDN_C), lambda n: (steps - 1 - n, 0, 0, 0)),
                  pl.BlockSpec((rows, GDN_V), rev)],
        out_specs=[pl.BlockSpec((rows, QKV_W), rev), pl.BlockSpec((rows, SM_W), rev)],
        out_shape=[jax.ShapeDtypeStruct((Lp, QKV_W), f32), jax.ShapeDtypeStruct((Lp, SM_W), f32)],
        scratch_shapes=[pltpu.VMEM((GDN_H, GDN_DK, GDN_DV), f32)], args=[qkvc, gb, sall, pall, do],
        sem=("arbitrary",), side=side)
    return out[0], out[1], out[2:]


GLA_BLOCK = 64


def _gla_group(Lp):
    nb = Lp // GLA_BLOCK
    return next(g for g in (3, 2, 1) if nb % g == 0)


def _gla_slices(h):
    sq = slice(h * GLA_DK, (h + 1) * GLA_DK)
    sk = slice(GLA_QK + h * GLA_DK, GLA_QK + (h + 1) * GLA_DK)
    sv = slice(2 * GLA_QK + h * GLA_DV, 2 * GLA_QK + (h + 1) * GLA_DV)
    return sq, sk, sv


def _gla_heads(x_ref, la_ref, group):
    qs, ks, vs, ls = [], [], [], []
    for blk in range(group):
        r = slice(blk * GLA_BLOCK, (blk + 1) * GLA_BLOCK)
        for h in range(GLA_H):
            sq, sk, sv = _gla_slices(h)
            qs.append(x_ref[r, sq])
            ks.append(x_ref[r, sk])
            vs.append(x_ref[r, sv])
            ls.append(la_ref[r, sq])
    return qs, ks, vs, ls


def _gla_fwd(proj, la):
    Lp = proj.shape[0]
    group = _gla_group(Lp)
    rows = group * GLA_BLOCK
    steps = Lp // rows
    R = range(GLA_H)

    def body(x_ref, la_ref, o_ref, sall_ref, s_scr):
        @pl.when(pl.program_id(0) == 0)
        def _():
            s_scr[...] = jnp.zeros_like(s_scr)

        Sts = [s_scr[h] for h in R]
        for h in R:
            sall_ref[0, h] = Sts[h]
        St2, o = _gla_blocks(Sts, *_gla_heads(x_ref, la_ref, group))
        for h in R:
            s_scr[h] = St2[h]
        for blk in range(group):
            for h in R:
                o_ref[blk * GLA_BLOCK:(blk + 1) * GLA_BLOCK, h * GLA_DV:(h + 1) * GLA_DV] = o[blk * GLA_H + h]

    return pl.pallas_call(
        body, name="gla_fwd", grid=(steps,),
        in_specs=[pl.BlockSpec((rows, G_W), lambda n: (n, C_G // G_W)),
                  pl.BlockSpec((rows, GLA_QK), lambda n: (n, 0))],
        out_specs=[pl.BlockSpec((rows, GLA_V), lambda n: (n, 0)),
                   pl.BlockSpec((1, GLA_H, GLA_DV, GLA_DK), lambda n: (n, 0, 0, 0))],
        out_shape=[jax.ShapeDtypeStruct((Lp, GLA_V), f32),
                   jax.ShapeDtypeStruct((steps, GLA_H, GLA_DV, GLA_DK), f32)],
        scratch_shapes=[pltpu.VMEM((GLA_H, GLA_DV, GLA_DK), f32)],
        compiler_params=_params("arbitrary"),
    )(proj, la)


def _gla_bwd(proj, la, sall, do, dproj, side=None):
    Lp = proj.shape[0]
    group = _gla_group(Lp)
    rows = group * GLA_BLOCK
    steps = Lp // rows
    R = range(GLA_H)

    def body(x_ref, la_ref, sall_ref, do_ref, dproj_ref, dx_ref, dla_ref, ds_scr):
        del dproj_ref

        @pl.when(pl.program_id(0) == 0)
        def _():
            ds_scr[...] = jnp.zeros_like(ds_scr)

        _, vjp = jax.vjp(_gla_blocks, [sall_ref[0, h] for h in R], *_gla_heads(x_ref, la_ref, group))
        do = [do_ref[blk * GLA_BLOCK:(blk + 1) * GLA_BLOCK, h * GLA_DV:(h + 1) * GLA_DV]
              for blk in range(group) for h in R]
        dS, dq, dk, dv, dl = vjp(([ds_scr[h] for h in R], do))
        for h in R:
            ds_scr[h] = dS[h]
        for blk in range(group):
            r = slice(blk * GLA_BLOCK, (blk + 1) * GLA_BLOCK)
            for h in R:
                sq, sk, sv = _gla_slices(h)
                i = blk * GLA_H + h
                dx_ref[r, sq] = dq[i].astype(bf16)
                dx_ref[r, sk] = dk[i].astype(bf16)
                dx_ref[r, sv] = dv[i].astype(bf16)
                dla_ref[r, sq] = dl[i]

    x_spec = pl.BlockSpec((rows, G_W), lambda n: (steps - 1 - n, C_G // G_W))
    rev = lambda n: (steps - 1 - n, 0)
    out = _call(
        body, name="gla_bwd", grid=(steps,),
        in_specs=[x_spec, pl.BlockSpec((rows, GLA_QK), rev),
                  pl.BlockSpec((1, GLA_H, GLA_DV, GLA_DK), lambda n: (steps - 1 - n, 0, 0, 0)),
                  pl.BlockSpec((rows, GLA_V), rev), _ANY],
        out_specs=[x_spec, pl.BlockSpec((rows, GLA_QK), rev)],
        out_shape=[jax.ShapeDtypeStruct(dproj.shape, dproj.dtype), jax.ShapeDtypeStruct((Lp, GLA_QK), f32)],
        aliases={4: 0}, scratch_shapes=[pltpu.VMEM((GLA_H, GLA_DV, GLA_DK), f32)],
        args=[proj, la, sall, do, dproj], sem=("arbitrary",), side=side)
    return out[0], out[1], out[2:]


def _gated_norm_fn(og, ol, zr, wg, wl):
    outs = []
    for h in range(GDN_H):
        s = slice(h * GDN_DV, (h + 1) * GDN_DV)
        outs.append(_rms(og[:, s])[0] * wg * _silu(zr[:, s]))
    for h in range(GLA_H):
        s = slice(h * GLA_DV, (h + 1) * GLA_DV)
        sr = slice(GDN_V + h * GLA_DV, GDN_V + (h + 1) * GLA_DV)
        outs.append(_rms(ol[:, s])[0] * wl * _silu(zr[:, sr]))
    return jnp.concatenate(outs, axis=-1)


def _gated_norm(og, ol, proj, wg, wl):
    Lp = og.shape[0]
    tr = _tile(Lp, 256)

    def body(og_ref, ol_ref, zr_ref, wg_ref, wl_ref, o_ref):
        o_ref[...] = _gated_norm_fn(og_ref[...], ol_ref[...], zr_ref[...], wg_ref[...], wl_ref[...]).astype(bf16)

    return pl.pallas_call(
        body, name="gated_norm", grid=(Lp // tr,),
        in_specs=[pl.BlockSpec((tr, GDN_V), lambda i: (i, 0)), pl.BlockSpec((tr, GLA_V), lambda i: (i, 0)),
                  pl.BlockSpec((tr, ZR_W), lambda i: (i, C_ZR // ZR_W)),
                  pl.BlockSpec((1, GDN_DV), lambda i: (0, 0)), pl.BlockSpec((1, GLA_DV), lambda i: (0, 0))],
        out_specs=pl.BlockSpec((tr, ZR_W), lambda i: (i, 0)),
        out_shape=jax.ShapeDtypeStruct((Lp, ZR_W), bf16),
        compiler_params=_params("parallel"),
    )(og, ol, proj, wg, wl)


def _gated_norm_bwd(og, ol, proj, wg, wl, dmix):
    Lp = og.shape[0]
    tr = _tile(Lp, 128)

    def body(og_ref, ol_ref, zr_ref, wg_ref, wl_ref, d_ref, dog_ref, dol_ref, dzr_ref, gwg_ref, gwl_ref):
        i = pl.program_id(0)
        _, vjp = jax.vjp(_gated_norm_fn, og_ref[...], ol_ref[...], zr_ref[...], wg_ref[...], wl_ref[...])
        dog, dol, dzr, gwg, gwl = vjp(d_ref[...])
        dog_ref[...] = dog
        dol_ref[...] = dol
        dzr_ref[...] = dzr.astype(bf16)

        @pl.when(i == 0)
        def _():
            gwg_ref[...] = gwg
            gwl_ref[...] = gwl

        @pl.when(i > 0)
        def _():
            gwg_ref[...] += gwg
            gwl_ref[...] += gwl

    og_spec = pl.BlockSpec((tr, GDN_V), lambda i: (i, 0))
    ol_spec = pl.BlockSpec((tr, GLA_V), lambda i: (i, 0))
    zr_spec = pl.BlockSpec((tr, ZR_W), lambda i: (i, C_ZR // ZR_W))
    vg = pl.BlockSpec((1, GDN_DV), lambda i: (0, 0))
    vl = pl.BlockSpec((1, GLA_DV), lambda i: (0, 0))
    return pl.pallas_call(
        body, name="gated_norm_bwd", grid=(Lp // tr,),
        in_specs=[og_spec, ol_spec, zr_spec, vg, vl, pl.BlockSpec((tr, ZR_W), lambda i: (i, 0))],
        out_specs=[og_spec, ol_spec, zr_spec, vg, vl],
        out_shape=[jax.ShapeDtypeStruct((Lp, GDN_V), f32), jax.ShapeDtypeStruct((Lp, GLA_V), f32),
                   jax.ShapeDtypeStruct((Lp, C_END), bf16),
                   jax.ShapeDtypeStruct((1, GDN_DV), f32), jax.ShapeDtypeStruct((1, GLA_DV), f32)],
        compiler_params=_params("arbitrary"),
    )(og, ol, proj, wg, wl, dmix)


def _adamw(g, w, m, v, name):
    R, C = g.shape
    tr = _tile(R, 256, 8) if R % 8 == 0 and R > 256 else R
    c1 = 1.0 - ADAM_B1 ** ADAM_STEP
    c2 = 1.0 - ADAM_B2 ** ADAM_STEP

    def body(g_ref, w_ref, m_ref, v_ref, d_ref, mo_ref, vo_ref):
        g_ = g_ref[...]
        m2 = ADAM_B1 * m_ref[...] + (1.0 - ADAM_B1) * g_
        v2 = ADAM_B2 * v_ref[...] + (1.0 - ADAM_B2) * (g_ * g_)
        mo_ref[...] = m2
        vo_ref[...] = v2
        d_ref[...] = -ADAM_LR * ((m2 / c1) / (jnp.sqrt(v2 / c2) + ADAM_EPS) + ADAM_WD * w_ref[...])

    blk = pl.BlockSpec((tr, C), lambda i: (i, 0))
    return pl.pallas_call(
        body, name=name, grid=(R // tr,), in_specs=[blk] * 4, out_specs=[blk] * 3,
        out_shape=[jax.ShapeDtypeStruct((R, C), f32)] * 3,
        compiler_params=_params("parallel"),
    )(g, w, m, v)


def _sum_slots(r, name):
    n, R, C = r.shape
    tr = _tile(R, 128, 16) if R % 16 == 0 and R > 128 else R

    def body(r_ref, o_ref):
        acc = r_ref[0].astype(f32)
        for s in range(1, n):
            acc = acc + r_ref[s].astype(f32)
        o_ref[...] = acc

    return pl.pallas_call(
        body, name=name, grid=(R // tr,),
        in_specs=[pl.BlockSpec((n, tr, C), lambda i: (0, i, 0))],
        out_specs=pl.BlockSpec((tr, C), lambda i: (i, 0)),
        out_shape=jax.ShapeDtypeStruct((R, C), f32),
        compiler_params=_params("parallel"),
    )(r)


SIBLING_PARTS = 8


class _Siblings:
    def __init__(self, arrays):
        self.arrays = list(arrays)
        self.n = len(self.arrays)
        self.parts = [next(p for p in range(SIBLING_PARTS, 0, -1) if a.shape[0] % (8 * p) == 0 or p == 1)
                      for a in self.arrays]
        total = sum(self.parts)
        self.out_shape = [jax.ShapeDtypeStruct((2,) + a.shape, a.dtype) for a in self.arrays]
        self.sems = [pltpu.SemaphoreType.DMA((total,)), pltpu.SemaphoreType.DMA((total,)),
                     pltpu.SemaphoreType.DMA((self.n,))]

    def hooks(self, ins, outs, send, recv, lsem):
        def copies():
            x, y, c = lax.axis_index("x"), lax.axis_index("y"), lax.axis_index("c")
            out, k = [], 0
            for a in range(self.n):
                out.append((pltpu.make_async_copy(ins[a], outs[a].at[c], lsem.at[a]), None))
                rows = self.arrays[a].shape[0] // self.parts[a]
                for part in range(self.parts[a]):
                    r = pl.ds(part * rows, rows)
                    mk = lambda dst, a=a, r=r, k=k: pltpu.make_async_remote_copy(
                        src_ref=ins[a].at[r], dst_ref=dst.at[r], send_sem=send.at[k], recv_sem=recv.at[k],
                        device_id=(x, y, 1 - c), device_id_type=MESH)
                    out.append((mk(outs[a].at[c]), mk(outs[a].at[1 - c])))
                    k += 1
            return out

        return _start_wait(copies)


def _comm_now(name, sides):
    total = sum(s.n for s in sides)

    def body(*refs):
        ins, outs, sems = refs[:total], refs[total:2 * total], refs[2 * total:]
        hooks, o = [], 0
        for i, s in enumerate(sides):
            hooks.append(s.hooks(ins[o:o + s.n], outs[o:o + s.n], *sems[3 * i:3 * i + 3]))
            o += s.n
        for start, _ in hooks:
            start()
        for _, wait in hooks:
            wait()

    out = pl.pallas_call(
        body, name=name, in_specs=[_ANY] * total, out_specs=[_ANY] * total,
        out_shape=[sh for s in sides for sh in s.out_shape], scratch_shapes=[sm for s in sides for sm in s.sems],
    )(*[a for s in sides for a in s.arrays])
    res, o = [], 0
    for s in sides:
        res.append(list(out[o:o + s.n]))
        o += s.n
    return res


def _cat_cols(g):
    return jnp.concatenate([g[i] for i in range(N_CHIP)], axis=-1)


def _row_slabs(a):
    return a.reshape(N_DEV, a.shape[0] // N_DEV, a.shape[1])


def _w_in_columns(g_wp, g_wsm):
    return jnp.concatenate([g_wp[:, C_QKV:C_END], g_wp[:, C_ZR:C_ZR + GDN_V], g_wsm[:, :SM_LR],
                            g_wp[:, C_G:C_G + G_W], g_wp[:, C_ZR + GDN_V:C_ZR + ZR_W],
                            g_wsm[:, SM_LR:SM_LR + GATE_RANK]], axis=1)


def _step(x, loss_target, p, meta, shard):
    _, S, D = x.shape
    alog_p = jnp.pad(p["gdn_a_log"], ((0, 0), (0, SM_W - GDN_H)))
    dtb_p = jnp.pad(p["gdn_dt_bias"], ((0, 0), (0, SM_W - GDN_H)))
    m64 = jnp.concatenate([jnp.zeros((PAD, D), f32), meta], axis=0)
    gate_b, gdn_norm_w, gla_norm_w = p["gla_gate_b"], p["gdn_norm_w"], p["gla_norm_w"]
    half = shard["w_up"].shape[0] // 2

    h0, n1, (w_in4, conv4, w24) = _embed_norm(
        x, m64, p["attn_norm_w"], side=_Gather([shard["w_in"], shard["gdn_conv_w"], shard["gla_gate_w2"]]))
    w_in, conv_w, w2 = _cat_cols(w_in4), _cat_cols(conv4), _cat_cols(w24)
    wp = jnp.concatenate([w_in[:, R_Z:R_AB], w_in[:, R_GR:R_LR], w_in[:, R_G:R_GR], w_in[:, R_QKV:R_Z]], axis=1)
    wsm = jnp.concatenate([w_in[:, R_AB:R_G], w_in[:, R_LR:R_END],
                           jnp.zeros((D, SM_W - SM_LR - GATE_RANK), w_in.dtype)], axis=1)
    w2p = jnp.pad(w2, ((SM_LR, SM_W - SM_LR - GATE_RANK), (0, 0)))
    proj, (w_out4, w_up4a) = _mm(n1, wp, "nn", "proj", side=_Gather([shard["w_out"], shard["w_up"][:half]]))
    w_out = w_out4.reshape(-1, D)
    psm = _mm(n1, wsm, "nn", "proj_small")
    gb, la = _gates(psm, w2p, gate_b, alog_p, dtb_p)
    qkvc, (w_up4b,) = _conv(proj, conv_w, side=_Gather([shard["w_up"][half:]]))
    w_up = jnp.concatenate([_cat_cols(w_up4a), _cat_cols(w_up4b)], axis=0)
    og, sall, pall, (w_gate4,) = _gdn_fwd(qkvc, gb, side=_Gather([shard["w_gate"]]))
    w_gate = _cat_cols(w_gate4)
    ol, stall = _gla_fwd(proj, la)
    mixed = _gated_norm(og, ol, proj, gdn_norm_w, gla_norm_w)
    attn = _mm(mixed, w_out, "nn", "out_proj")
    h1, n2 = _add_norm(h0, attn, p["ffn_norm_w"])
    act, act_dgate, act_dup, (w_down4,) = _ffn_in(n2, w_gate, w_up, side=_Gather([shard["w_down"]]))
    w_down = w_down4.reshape(-1, D)
    ffn = _mm(act, w_down, "nn", "ffn_down", **WHOLE_K)
    dh2, dh2b, lossp, g_final = _final(h1, ffn, loss_target, p["final_norm_w"])

    g_down = _mm(act, dh2b, "tn", "g_w_down", out_dtype=bf16, **WHOLE_K_T)
    dg, du = _ffn_dact(dh2b, w_down, act_dgate, act_dup)
    g_gate = _mm(n2, dg, "tn", "g_w_gate", tm_cap=512, tn_cap=1408, tk_cap=2752, out_dtype=bf16, col_slabs=True)
    g_up = _mm(n2, du, "tn", "g_w_up", tm_cap=512, tn_cap=1408, tk_cap=2752, out_dtype=bf16, col_slabs=True)
    dn2 = _mm(dg, w_gate, "nt", "d_n2_gate", **WHOLE_K)
    dn2 = _mm(du, w_up, "nt", "d_n2_up", acc_in=dn2, **WHOLE_K)
    dh1, dh1b, g_ffn_norm = _norm_bwd(dn2, h1, dh2, p["ffn_norm_w"])
    dmix = _mm(dh1b, w_out, "nt", "d_mixed")
    g_out = _mm(mixed, dh1b, "tn", "g_w_out", out_dtype=bf16, **WHOLE_K_T)
    dog, dol, dproj, g_gdn_norm, g_gla_norm = _gated_norm_bwd(og, ol, proj, gdn_norm_w, gla_norm_w, dmix)
    dproj, dla, (r_down,) = _gla_bwd(proj, la, stall, dol, dproj, side=_Exchange([_row_slabs(g_down)]))
    dqkvc, dgb, (r_gate, r_up, r_out, h_down) = _gdn_bwd(
        qkvc, gb, sall, pall, dog,
        side=_Sides(_Exchange([g_gate, g_up, _row_slabs(g_out)]), _Siblings([_sum_slots(r_down, "sum_w_down")])))
    dproj, g_conv, (h_gate,) = _conv_bwd(proj, conv_w, dqkvc, dproj,
                                         side=_Siblings([_sum_slots(r_gate, "sum_w_gate")]))
    dpsm, g_w2p, g_gate_b, g_alog, g_dtb = _gates_bwd(psm, w2p, gate_b, alog_p, dtb_p, dgb, dla)
    g_wp, (h_up, h_out) = _mm(
        n1, dproj, "tn", "g_w_in", out_dtype=bf16,
        side=_Siblings([_sum_slots(r_up, "sum_w_up"), _sum_slots(r_out, "sum_w_out")]), **WHOLE_K_T)
    g_wsm = _mm(n1, dpsm, "tn", "g_w_in_small", out_dtype=bf16, **WHOLE_K_T)
    dn1, r_in = _mm(dproj, wp, "nt", "d_n1", side=_Exchange([_row_slabs(g_wp), _row_slabs(g_wsm)]), **WHOLE_K)
    dn1 = _mm(dpsm, wsm, "nt", "d_n1_small", acc_in=dn1)
    s_in = _w_in_columns(_sum_slots(r_in[0], "sum_w_in"), _sum_slots(r_in[1], "sum_w_in_small"))
    in_by_chip = s_in.reshape(s_in.shape[0], N_CHIP, -1).transpose(1, 0, 2)
    grad_x, g_meta, g_attn_norm, (h_in,) = _embed_norm_bwd(dn1, h0, dh1, p["attn_norm_w"], S,
                                                           side=_Exchange([], by_chip=[in_by_chip]))

    received = dict(w_in=h_in, w_gate=h_gate, w_up=h_up, w_out=h_out, w_down=h_down)
    small = dict(
        meta_tokens=g_meta, attn_norm_w=g_attn_norm, gdn_conv_w=g_conv, gdn_a_log=g_alog[:, :GDN_H],
        gdn_dt_bias=g_dtb[:, :GDN_H], gdn_norm_w=g_gdn_norm, gla_gate_w2=g_w2p[SM_LR:SM_LR + GATE_RANK],
        gla_gate_b=g_gate_b, gla_norm_w=g_gla_norm, ffn_norm_w=g_ffn_norm, final_norm_w=g_final)
    return lossp[0, 0], grad_x, received, small


_WEIGHTS = ("meta_tokens", "attn_norm_w", "w_in", "gdn_conv_w", "gdn_a_log", "gdn_dt_bias", "gdn_norm_w",
            "gla_gate_w2", "gla_gate_b", "gla_norm_w", "w_out", "ffn_norm_w", "w_gate", "w_up", "w_down",
            "final_norm_w")
_BIG_COLS = ("w_in", "w_gate", "w_up")
_BIG_ROWS = ("w_out", "w_down")
_SMALL_SHARDED = ("meta_tokens", "gdn_conv_w", "gla_gate_w2")


def kernel(x, meta_tokens, attn_norm_w, w_in, gdn_conv_w, gdn_a_log, gdn_dt_bias, gdn_norm_w, gla_gate_w2, gla_gate_b, gla_norm_w, w_out, ffn_norm_w, w_gate, w_up, w_down, final_norm_w, loss_target, m_meta_tokens, m_attn_norm_w, m_w_in, m_gdn_conv_w, m_gdn_a_log, m_gdn_dt_bias, m_gdn_norm_w, m_gla_gate_w2, m_gla_gate_b, m_gla_norm_w, m_w_out, m_ffn_norm_w, m_w_gate, m_w_up, m_w_down, m_final_norm_w, v_meta_tokens, v_attn_norm_w, v_w_in, v_gdn_conv_w, v_gdn_a_log, v_gdn_dt_bias, v_gdn_norm_w, v_gla_gate_w2, v_gla_gate_b, v_gla_norm_w, v_w_out, v_ffn_norm_w, v_w_gate, v_w_up, v_w_down, v_final_norm_w):
    w = dict(meta_tokens=meta_tokens, attn_norm_w=attn_norm_w, w_in=w_in, gdn_conv_w=gdn_conv_w, gdn_a_log=gdn_a_log,
             gdn_dt_bias=gdn_dt_bias, gdn_norm_w=gdn_norm_w, gla_gate_w2=gla_gate_w2, gla_gate_b=gla_gate_b,
             gla_norm_w=gla_norm_w, w_out=w_out, ffn_norm_w=ffn_norm_w, w_gate=w_gate, w_up=w_up, w_down=w_down,
             final_norm_w=final_norm_w)
    m = dict(meta_tokens=m_meta_tokens, attn_norm_w=m_attn_norm_w, w_in=m_w_in, gdn_conv_w=m_gdn_conv_w,
             gdn_a_log=m_gdn_a_log, gdn_dt_bias=m_gdn_dt_bias, gdn_norm_w=m_gdn_norm_w, gla_gate_w2=m_gla_gate_w2,
             gla_gate_b=m_gla_gate_b, gla_norm_w=m_gla_norm_w, w_out=m_w_out, ffn_norm_w=m_ffn_norm_w,
             w_gate=m_w_gate, w_up=m_w_up, w_down=m_w_down, final_norm_w=m_final_norm_w)
    v = dict(meta_tokens=v_meta_tokens, attn_norm_w=v_attn_norm_w, w_in=v_w_in, gdn_conv_w=v_gdn_conv_w,
             gdn_a_log=v_gdn_a_log, gdn_dt_bias=v_gdn_dt_bias, gdn_norm_w=v_gdn_norm_w, gla_gate_w2=v_gla_gate_w2,
             gla_gate_b=v_gla_gate_b, gla_norm_w=v_gla_norm_w, w_out=v_w_out, ffn_norm_w=v_ffn_norm_w,
             w_gate=v_w_gate, w_up=v_w_up, w_down=v_w_down, final_norm_w=v_final_norm_w)
    chip = 2 * lax.axis_index("x") + lax.axis_index("y")

    def two_d(a):
        return a.reshape(1, -1) if a.ndim == 1 else a.reshape(-1, a.shape[-1])

    w2d = {k: two_d(a) for k, a in w.items()}
    big = _BIG_COLS + _BIG_ROWS
    small = tuple(k for k in _WEIGHTS if k not in big)

    (meta4,), = _comm_now("gather_meta", [_Gather([w2d["meta_tokens"]])])
    shard = {k: w2d[k].astype(bf16) for k in big}
    shard.update({k: w2d[k] for k in ("gdn_conv_w", "gla_gate_w2")})
    lossp, grad_x, received, g = _step(x, loss_target, {k: w2d[k] for k in small}, _cat_cols(meta4), shard)
    loss = lax.psum(lossp, ("x", "y", "c"))

    sizes = [g[k].size for k in small]
    total = sum(sizes)
    rows = -(-total // 1024)
    rows += (-rows) % 8
    packed = jnp.concatenate([g[k].reshape(-1) for k in small] + [jnp.zeros((rows * 1024 - total,), f32)])
    (packed8,), = _comm_now("exchange_small", [_Exchange([], [packed.reshape(rows, 1024)])])
    red = {k: h.reshape(w2d[k].shape) for k, h in received.items()}
    psum_small = _sum_slots(packed8, "sum_small").reshape(-1)
    off = 0
    for k, n in zip(small, sizes):
        a = psum_small[off:off + n].reshape(g[k].shape)
        off += n
        if k in _SMALL_SHARDED:
            c = w2d[k].shape[1]
            a = lax.dynamic_slice_in_dim(a, chip * c, c, axis=1)
        red[k] = a

    grads, deltas, new_m, new_v = [], [], [], []
    for k in _WEIGHTS:
        d, m2, v2 = _adamw(red[k], w2d[k], two_d(m[k]), two_d(v[k]), "adamw_" + k)
        shape = w[k].shape
        grads.append(red[k].reshape(shape))
        deltas.append(d.reshape(shape))
        new_m.append(m2.reshape(shape))
        new_v.append(v2.reshape(shape))
    return (loss, grad_x, *grads, *deltas, *new_m, *new_v)
```

```python
import functools

import jax
import jax.numpy as jnp
from jax import lax
from jax.experimental import pallas as pl
from jax.experimental.pallas import tpu as pltpu

f32 = jnp.float32
bf16 = jnp.bfloat16
HIGH = lax.Precision.HIGH
MESH = pl.DeviceIdType.MESH

N_META = 16
CONV_K = 4
GDN_H, GDN_DK, GDN_DV, GDN_C = 8, 128, 128, 64
GLA_H, GLA_DK, GLA_DV, GLA_C = 4, 128, 256, 16
GATE_RANK = 16
GATE_NORMALIZER = 16.0
EPS = 1e-6
GDN_QK = GDN_H * GDN_DK
GDN_V = GDN_H * GDN_DV
GLA_QK = GLA_H * GLA_DK
GLA_V = GLA_H * GLA_DV
PAD = (-N_META) % GDN_C
OFF = PAD + N_META
ROWS = 64

R_QKV, R_Z, R_AB, R_G, R_GR, R_LR, R_END = 0, 3072, 4096, 4112, 6160, 7184, 7200
C_ZR, C_G, C_QKV, C_END = 0, 2048, 4096, 7168
ZR_W = GDN_V + GLA_V
G_W = 2 * GLA_QK + GLA_V
QKV_W = 2 * GDN_QK + GDN_V
Q0, K0, V0 = 0, GDN_QK, 2 * GDN_QK
SM_W = 128
SM_LR = 2 * GDN_H

ADAM_LR, ADAM_B1, ADAM_B2, ADAM_EPS, ADAM_WD, ADAM_STEP = 0.001, 0.9, 0.999, 1e-08, 0.01, 10

VMEM_LIMIT_V7X = 56 * 1024 * 1024
N_DEV = 8
N_CHIP = 4


def _params(*sem):
    return pltpu.CompilerParams(dimension_semantics=sem, vmem_limit_bytes=VMEM_LIMIT_V7X)


def _tile(n, cap, mult=16):
    best = None
    for d in range(mult, min(n, cap) + 1, mult):
        if n % d == 0:
            best = d
    assert best is not None, (n, cap, mult)
    return best


NN = ((1,), (0,))
NT = ((1,), (1,))
TN = ((0,), (0,))


def _dot(a, b, dims, prec=None):
    return lax.dot_general(a, b, (dims, ((), ())), precision=prec, preferred_element_type=f32)


def _mmb(a, b, dims):
    return _dot(a.astype(bf16), b.astype(bf16), dims)


def _sigmoid(x):
    return jax.nn.sigmoid(x)


def _silu(x):
    return x * _sigmoid(x)


def _dsilu(x):
    s = _sigmoid(x)
    return s * (1.0 + x * (1.0 - s))


def _log1p_exp_neg_abs(x):
    t = jnp.exp(-jnp.abs(x))
    u = 1.0 + t
    d = u - 1.0
    return jnp.where(d == 0.0, t, jnp.log(u) * (t / jnp.where(d == 0.0, 1.0, d)))


def _softplus(x):
    return jnp.maximum(x, 0.0) + _log1p_exp_neg_abs(x)


def _log_sigmoid(x):
    return jnp.minimum(x, 0.0) - _log1p_exp_neg_abs(x)


def _rms(x):
    r = lax.rsqrt(jnp.mean(x * x, axis=-1, keepdims=True) + EPS)
    return x * r, r


def _rms_bwd(dy, xh, r, w):
    t = dy * w
    return r * (t - xh * jnp.mean(t * xh, axis=-1, keepdims=True))


def _l2n(x):
    return x * lax.rsqrt(jnp.sum(x * x, axis=-1, keepdims=True) + EPS)


INV_LEAF = 8


def _same_block(C, b):
    sh = b.bit_length() - 1
    row = lax.broadcasted_iota(jnp.int32, (C, C), 0)
    col = lax.broadcasted_iota(jnp.int32, (C, C), 1)
    return lax.shift_right_logical(row, sh) == lax.shift_right_logical(col, sh)


def _tri_inv_impl(As):
    C = As[0].shape[0]
    R = range(len(As))
    row = lax.broadcasted_iota(jnp.int32, (C, C), 0)
    col = lax.broadcasted_iota(jnp.int32, (C, C), 1)
    eye = (row == col).astype(f32)
    b = INV_LEAF
    inner = _same_block(C, b)
    leaf = [jnp.where(inner, As[h], 0.0) for h in R]
    d = [eye - leaf[h] for h in R]
    pw = leaf
    n = 2
    while n < b:
        pw = [_dot(pw[h], pw[h], NN, HIGH) for h in R]
        d = [_dot(d[h], eye + pw[h], NN, HIGH) for h in R]
        n *= 2
    while b < C:
        outer = _same_block(C, 2 * b)
        level = jnp.logical_and(outer, jnp.logical_not(inner))
        ed = [_dot(jnp.where(level, As[h], 0.0), d[h], NN, HIGH) for h in R]
        d = [d[h] - _dot(d[h], ed[h], NN, HIGH) for h in R]
        inner = outer
        b *= 2
    return d


@jax.custom_vjp
def _tri_inv(As):
    return _tri_inv_impl(As)


def _tri_inv_fwd(As):
    d = _tri_inv_impl(As)
    return d, d


def _tri_inv_bwd(d, g):
    R = range(len(d))
    t = [_dot(d[h], g[h], TN, HIGH) for h in R]
    return ([-_dot(t[h], d[h], NT, HIGH) for h in R],)


_tri_inv.defvjp(_tri_inv_fwd, _tri_inv_bwd)


@jax.custom_vjp
def _tri_inv_known(As, Ps):
    del As
    return Ps


def _tri_inv_known_fwd(As, Ps):
    del As
    return Ps, Ps


def _tri_inv_known_bwd(d, g):
    return _tri_inv_bwd(d, g)[0], [jnp.zeros_like(x) for x in d]


_tri_inv_known.defvjp(_tri_inv_known_fwd, _tri_inv_known_bwd)


def _gdn_chunk(Ss, qrs, krs, vs, betas, gs, Ps=None):
    H = len(Ss)
    C, dk = qrs[0].shape
    R = range(len(qrs))
    row = lax.broadcasted_iota(jnp.int32, (C, C), 0)
    col = lax.broadcasted_iota(jnp.int32, (C, C), 1)
    causal = row >= col
    strict = row > col
    cf = causal.astype(f32)
    q = [_l2n(qrs[h]) * (dk ** -0.5) for h in R]
    k = [_l2n(krs[h]) for h in R]
    mc = [_rows_exact(cf, jnp.broadcast_to(gs[h], (C, C))) for h in R]
    gc = [mc[h][:, 0:1] for h in R]
    decay = [jnp.where(causal, jnp.exp(jnp.where(causal, mc[h] - mc[h].T, 0.0)), 0.0) for h in R]
    kb = [k[h] * betas[h] for h in R]
    a = [jnp.where(strict, _mmb(kb[h], k[h], NT) * decay[h], 0.0) for h in R]
    p = _tri_inv(a) if Ps is None else _tri_inv_known(a, Ps)
    egc = [jnp.exp(gc[h]) for h in R]
    u = [_mmb(p[h], vs[h] * betas[h], NN) for h in R]
    w = [_mmb(p[h], kb[h] * egc[h], NN) for h in R]
    qk = [jnp.where(causal, _mmb(q[h], k[h], NT) * decay[h], 0.0) for h in R]
    qe = [q[h] * egc[h] for h in R]
    gl = [gc[h][C - 1:C, :] for h in R]
    kd = [k[h] * jnp.exp(gl[h] - gc[h]) for h in R]
    egl = [jnp.exp(gl[h]) for h in R]
    S, o, entering = list(Ss), [], []
    for chunk in range(len(qrs) // H):
        idx = [chunk * H + h for h in range(H)]
        entering += S
        v_new = [u[i] - _mmb(w[i], S[h], NN) for h, i in enumerate(idx)]
        o += [_mmb(qe[i], S[h], NN) + _mmb(qk[i], v_new[h], NN) for h, i in enumerate(idx)]
        S = [S[h] * egl[i] + _mmb(kd[i], v_new[h], TN) for h, i in enumerate(idx)]
    return S, o, p, entering


def _rows_exact_impl(m01, x, dims):
    m = m01.astype(bf16)
    x1 = x.astype(bf16)
    r1 = x - x1.astype(f32)
    x2 = r1.astype(bf16)
    x3 = (r1 - x2.astype(f32)).astype(bf16)
    d = lambda y: _dot(m, y, dims)
    return d(x1) + (d(x2) + d(x3))


@jax.custom_vjp
def _rows_exact(m01, x):
    return _rows_exact_impl(m01, x, NN)


def _rows_exact_fwd(m01, x):
    return _rows_exact_impl(m01, x, NN), m01


def _rows_exact_bwd(m01, g):
    return jnp.zeros_like(m01), _rows_exact_impl(m01, g, TN)


_rows_exact.defvjp(_rows_exact_fwd, _rows_exact_bwd)


def _gla_blocks(Sts, qrs, ks, vs, las):
    H = len(Sts)
    n = len(qrs)
    C, dk = qrs[0].shape
    R = range(n)
    row = lax.broadcasted_iota(jnp.int32, (C, C), 0)
    col = lax.broadcasted_iota(jnp.int32, (C, C), 1)
    ri = lax.broadcasted_iota(jnp.int32, (C, dk), 0)
    q = [qrs[h] * (dk ** -0.5) for h in R]
    running = (row >= col).astype(f32)
    b = [_rows_exact(running, las[h]) for h in R]
    sc = [jnp.where(row == col, jnp.sum(q[h] * ks[h], axis=-1, keepdims=True), 0.0) for h in R]
    s = C // 2
    while s >= 1:
        sh = s.bit_length() - 1
        ref = lax.shift_left(lax.shift_right_logical(row, sh + 1), sh + 1) + (s - 1)
        pick = (col == ref).astype(f32)
        bref = [_rows_exact(pick, b[h]) for h in R]
        upper = (lax.shift_right_logical(ri, sh) & 1) == 1
        qt = [jnp.where(upper, q[h] * jnp.exp(jnp.where(upper, b[h] - bref[h], 0.0)), 0.0) for h in R]
        kt = [jnp.where(upper, 0.0, ks[h] * jnp.exp(jnp.where(upper, 0.0, bref[h] - b[h]))) for h in R]
        same = lax.shift_right_logical(row, sh + 1) == lax.shift_right_logical(col, sh + 1)
        sc = [sc[h] + jnp.where(same, _mmb(qt[h], kt[h], NT), 0.0) for h in R]
        s //= 2
    o = [_mmb(sc[h], vs[h], NN) for h in R]
    qe = [q[h] * jnp.exp(b[h]) for h in R]
    bl = [b[h][C - 1:C, :] for h in R]
    upd = [_mmb(vs[h], ks[h] * jnp.exp(bl[h] - b[h]), TN) for h in R]
    ebl = [jnp.exp(bl[h]) for h in R]
    St = list(Sts)
    for blk in range(n // H):
        for h in range(H):
            i = blk * H + h
            o[i] = o[i] + _mmb(qe[i], St[h], NT)
        St = [St[h] * ebl[blk * H + h] + upd[blk * H + h] for h in range(H)]
    return St, o


_ANY = pl.BlockSpec(memory_space=pl.ANY)


class _Gather:
    def __init__(self, arrays):
        self.arrays = list(arrays)
        self.n = len(self.arrays)
        self.out_shape = [jax.ShapeDtypeStruct((N_CHIP,) + a.shape, a.dtype) for a in self.arrays]
        self.sems = [pltpu.SemaphoreType.DMA((self.n, 3)), pltpu.SemaphoreType.DMA((self.n, 3)),
                     pltpu.SemaphoreType.DMA((self.n,))]

    def hooks(self, ins, outs, send, recv, lsem):
        def copies():
            x, y, c = lax.axis_index("x"), lax.axis_index("y"), lax.axis_index("c")
            me = 2 * x + y
            out = []
            for a in range(self.n):
                out.append((pltpu.make_async_copy(ins[a], outs[a].at[me], lsem.at[a]), None))
                for j, (px, py) in enumerate([(1 - x, y), (x, 1 - y), (1 - x, 1 - y)]):
                    mk = lambda dst, a=a, j=j, px=px, py=py: pltpu.make_async_remote_copy(
                        src_ref=ins[a], dst_ref=dst, send_sem=send.at[a, j], recv_sem=recv.at[a, j],
                        device_id=(px, py, c), device_id_type=MESH)
                    out.append((mk(outs[a].at[me]), mk(outs[a].at[2 * px + py])))
            return out

        return _start_wait(copies)


class _Exchange:
    def __init__(self, slotted, shared=(), by_chip=()):
        self.arrays = list(slotted) + list(by_chip) + list(shared)
        self.ns, self.nc = len(slotted), len(by_chip)
        self.n = len(self.arrays)
        self.out_shape = [jax.ShapeDtypeStruct(a.shape, a.dtype) for a in slotted]
        self.out_shape += [jax.ShapeDtypeStruct((N_DEV,) + a.shape[1:], a.dtype) for a in by_chip]
        self.out_shape += [jax.ShapeDtypeStruct((N_DEV,) + b.shape, b.dtype) for b in shared]
        self.sems = [pltpu.SemaphoreType.DMA((self.n, N_DEV - 1)), pltpu.SemaphoreType.DMA((self.n, N_DEV - 1)),
                     pltpu.SemaphoreType.DMA((self.n,))]

    def hooks(self, ins, outs, send, recv, lsem):
        def copies():
            x, y, c = lax.axis_index("x"), lax.axis_index("y"), lax.axis_index("c")
            me = 4 * x + 2 * y + c

            def src(a, dev):
                tx, ty, tc = dev
                if a < self.ns:
                    return ins[a].at[4 * tx + 2 * ty + tc]
                return ins[a].at[2 * tx + ty] if a < self.ns + self.nc else ins[a]

            out = []
            for a in range(self.n):
                out.append((pltpu.make_async_copy(src(a, (x, y, c)), outs[a].at[me], lsem.at[a]), None))
                for o in range(1, N_DEV):
                    dev = (1 - x if o & 4 else x, 1 - y if o & 2 else y, 1 - c if o & 1 else c)
                    t = 4 * dev[0] + 2 * dev[1] + dev[2]
                    mk = lambda dst, a=a, o=o, dev=dev: pltpu.make_async_remote_copy(
                        src_ref=src(a, dev), dst_ref=dst, send_sem=send.at[a, o - 1], recv_sem=recv.at[a, o - 1],
                        device_id=dev, device_id_type=MESH)
                    out.append((mk(outs[a].at[me]), mk(outs[a].at[t])))
            return out

        return _start_wait(copies)


class _Sides:
    def __init__(self, *members):
        self.members = members
        self.arrays = [a for s in members for a in s.arrays]
        self.n = len(self.arrays)
        self.out_shape = [sh for s in members for sh in s.out_shape]
        self.sems = [sm for s in members for sm in s.sems]

    def hooks(self, ins, outs, *sems):
        hooks, o = [], 0
        for i, s in enumerate(self.members):
            hooks.append(s.hooks(ins[o:o + s.n], outs[o:o + s.n], *sems[3 * i:3 * i + 3]))
            o += s.n

        def start():
            for st, _ in hooks:
                st()

        def wait():
            for _, wt in hooks:
                wt()

        return start, wait


def _start_wait(copies):
    def start():
        for s, _ in copies():
            s.start()

    def wait():
        for s, w in copies():
            (s if w is None else w).wait()

    return start, wait


def _call(body, *, name, grid, in_specs, out_specs, out_shape, args, sem, scratch_shapes=(), aliases=None, side=None):
    in_specs, out_specs, out_shape, args = list(in_specs), list(out_specs), list(out_shape), list(args)
    scratch_shapes = list(scratch_shapes)
    aliases = aliases or {}
    if side is None:
        return pl.pallas_call(
            body, name=name, grid=grid, in_specs=in_specs, out_specs=out_specs, out_shape=out_shape,
            scratch_shapes=scratch_shapes, input_output_aliases=aliases, compiler_params=_params(*sem))(*args)
    n_in, n_out, n_scr, ns = len(in_specs), len(out_specs), len(scratch_shapes), side.n

    def full_body(*refs):
        ins, refs = refs[:n_in], refs[n_in:]
        s_in, refs = refs[:ns], refs[ns:]
        outs, refs = refs[:n_out], refs[n_out:]
        s_out, refs = refs[:ns], refs[ns:]
        scr, sems = refs[:n_scr], refs[n_scr:]
        start, wait = side.hooks(s_in, s_out, *sems)
        ids = [pl.program_id(d) for d in range(len(grid))]
        first = functools.reduce(jnp.logical_and, [i == 0 for i in ids])
        last = functools.reduce(jnp.logical_and, [i == g - 1 for i, g in zip(ids, grid)])
        pl.when(first)(start)
        body(*ins, *outs, *scr)
        pl.when(last)(wait)

    return pl.pallas_call(
        full_body, name=name, grid=grid, in_specs=in_specs + [_ANY] * ns, out_specs=out_specs + [_ANY] * ns,
        out_shape=out_shape + side.out_shape, scratch_shapes=scratch_shapes + side.sems,
        input_output_aliases=aliases, compiler_params=_params(*(["arbitrary"] * len(grid))))(*args, *side.arrays)


WHOLE_K = dict(tm_cap=688, tn_cap=512, tk_cap=1 << 20)
WHOLE_K_T = dict(tm_cap=512, tn_cap=512, tk_cap=1 << 20)

def _mm(a, b, mode, name, *, tm_cap=1408, tn_cap=1024, tk_cap=2048, out_dtype=f32, acc_in=None, side=None,
        col_slabs=False):
    if mode == "nn":
        (M, K), (K2, N) = a.shape, b.shape
    elif mode == "nt":
        (M, K), (N, K2) = a.shape, b.shape
    else:
        (K, M), (K2, N) = a.shape, b.shape
    assert K == K2, (name, a.shape, b.shape)
    tm = _tile(M // 2 if col_slabs else M, tm_cap)
    tn = _tile(N // N_CHIP if col_slabs else N, tn_cap, 128)
    tk = _tile(K, tk_cap, 128 if K % 128 == 0 else 16)
    nk = K // tk
    dims = {"nn": NN, "nt": NT, "tn": TN}[mode]
    use_scratch = nk > 1 and out_dtype != f32

    def body(*refs):
        if acc_in is not None:
            a_ref, b_ref, c_ref, o_ref, *scr = refs
        else:
            a_ref, b_ref, o_ref, *scr = refs
            c_ref = None
        p = _mmb(a_ref[...], b_ref[...], dims)
        if nk == 1:
            if c_ref is not None:
                p = p + c_ref[...]
            o_ref[...] = p.astype(out_dtype)
            return
        k = pl.program_id(2)
        acc = scr[0] if use_scratch else o_ref

        @pl.when(k == 0)
        def _():
            acc[...] = p if c_ref is None else p + c_ref[...]

        @pl.when(k > 0)
        def _():
            acc[...] += p

        if use_scratch:
            @pl.when(k == nk - 1)
            def _():
                o_ref[...] = acc[...].astype(out_dtype)

    if mode == "tn":
        a_spec = pl.BlockSpec((tk, tm), lambda i, j, k: (k, i))
    else:
        a_spec = pl.BlockSpec((tm, tk), lambda i, j, k: (i, k))
    if mode == "nt":
        b_spec = pl.BlockSpec((tn, tk), lambda i, j, k: (j, k))
    else:
        b_spec = pl.BlockSpec((tk, tn), lambda i, j, k: (k, j))
    if col_slabs:
        assert acc_in is None
        ni, nj = M // 2 // tm, N // N_CHIP // tn
        o_spec = pl.BlockSpec((None, tm, tn), lambda i, j, k: (2 * (j // nj) + i // ni, i % ni, j % nj))
        o_shape = jax.ShapeDtypeStruct((N_DEV, M // 2, N // N_CHIP), out_dtype)
    else:
        o_spec = pl.BlockSpec((tm, tn), lambda i, j, k: (i, j))
        o_shape = jax.ShapeDtypeStruct((M, N), out_dtype)
    in_specs = [a_spec, b_spec]
    args = [a, b]
    if acc_in is not None:
        in_specs.append(o_spec)
        args.append(acc_in)
    out = _call(body, name=name, grid=(M // tm, N // tn, nk), in_specs=in_specs, out_specs=[o_spec],
                out_shape=[o_shape], args=args,
                scratch_shapes=[pltpu.VMEM((tm, tn), f32)] if use_scratch else [],
                sem=("parallel", "parallel", "arbitrary"), side=side)
    return out[0] if side is None else (out[0], out[1:])


def _embed_norm(x3, m64, w, side=None):
    _, S, D = x3.shape
    Lp = OFF + S

    def body(x_ref, m_ref, w_ref, h_ref, n_ref):
        i = pl.program_id(0)
        h = jnp.where(i == 0, m_ref[...], x_ref[...])
        h_ref[...] = h
        xh, _ = _rms(h)
        n_ref[...] = (xh * w_ref[...]).astype(bf16)

    row = pl.BlockSpec((ROWS, D), lambda i: (i, 0))
    out = _call(
        body, name="embed_norm", grid=(Lp // ROWS,),
        in_specs=[pl.BlockSpec((None, ROWS, D), lambda i: (0, jnp.maximum(i - 1, 0), 0)),
                  pl.BlockSpec((ROWS, D), lambda i: (0, 0)),
                  pl.BlockSpec((1, D), lambda i: (0, 0))],
        out_specs=[row, row],
        out_shape=[jax.ShapeDtypeStruct((Lp, D), f32), jax.ShapeDtypeStruct((Lp, D), bf16)],
        args=[x3, m64, w], sem=("parallel",), side=side)
    return out[0], out[1], out[2:]


def _add_norm(h, d, w):
    Lp, D = h.shape
    tr = _tile(Lp, 256)

    def body(h_ref, d_ref, w_ref, o_ref, n_ref):
        h1 = h_ref[...] + d_ref[...]
        o_ref[...] = h1
        xh, _ = _rms(h1)
        n_ref[...] = (xh * w_ref[...]).astype(bf16)

    row = pl.BlockSpec((tr, D), lambda i: (i, 0))
    return pl.pallas_call(
        body, name="add_norm", grid=(Lp // tr,),
        in_specs=[row, row, pl.BlockSpec((1, D), lambda i: (0, 0))], out_specs=[row, row],
        out_shape=[jax.ShapeDtypeStruct((Lp, D), f32), jax.ShapeDtypeStruct((Lp, D), bf16)],
        compiler_params=_params("parallel"),
    )(h, d, w)


def _norm_bwd(dn, h, dh, w):
    Lp, D = h.shape
    tr = _tile(Lp, 256)

    def body(dn_ref, h_ref, dh_ref, w_ref, o_ref, ob_ref, gw_ref):
        i = pl.program_id(0)
        xh, r = _rms(h_ref[...])
        dn_ = dn_ref[...]
        o = dh_ref[...] + _rms_bwd(dn_, xh, r, w_ref[...])
        o_ref[...] = o
        ob_ref[...] = o.astype(bf16)
        gw = jnp.sum(dn_ * xh, axis=0, keepdims=True)

        @pl.when(i == 0)
        def _():
            gw_ref[...] = gw

        @pl.when(i > 0)
        def _():
            gw_ref[...] += gw

    row = pl.BlockSpec((tr, D), lambda i: (i, 0))
    vec = pl.BlockSpec((1, D), lambda i: (0, 0))
    return pl.pallas_call(
        body, name="norm_bwd", grid=(Lp // tr,), in_specs=[row, row, row, vec], out_specs=[row, row, vec],
        out_shape=[jax.ShapeDtypeStruct((Lp, D), f32), jax.ShapeDtypeStruct((Lp, D), bf16),
                   jax.ShapeDtypeStruct((1, D), f32)],
        compiler_params=_params("arbitrary"),
    )(dn, h, dh, w)


def _embed_norm_bwd(dn, h, dh, w, S, side=None):
    Lp, D = h.shape
    tr = _tile(S, 256, OFF)

    def body(dn_ref, h_ref, dh_ref, dn0_ref, h0_ref, dh0_ref, w_ref, gx_ref, gm_ref, gw_ref):
        i = pl.program_id(0)

        def rows(dn_, h_, dh_):
            xh, r = _rms(h_)
            return dh_ + _rms_bwd(dn_, xh, r, w_ref[...]), jnp.sum(dn_ * xh, axis=0, keepdims=True)

        d, gw = rows(dn_ref[...], h_ref[...], dh_ref[...])
        gx_ref[...] = d

        @pl.when(i == 0)
        def _():
            d0, gw0 = rows(dn0_ref[...], h0_ref[...], dh0_ref[...])
            gm_ref[...] = d0[PAD:OFF, :]
            gw_ref[...] = gw0 + gw

        @pl.when(i > 0)
        def _():
            gw_ref[...] += gw

    win = pl.BlockSpec((pl.Element(tr), pl.Element(D)), lambda i: (pl.multiple_of(OFF + i * tr, OFF), 0))
    head = pl.BlockSpec((OFF, D), lambda i: (0, 0))
    vec = pl.BlockSpec((1, D), lambda i: (0, 0))
    out = _call(
        body, name="embed_norm_bwd", grid=(S // tr,), in_specs=[win, win, win, head, head, head, vec],
        out_specs=[pl.BlockSpec((None, tr, D), lambda i: (0, i, 0)),
                   pl.BlockSpec((N_META, D), lambda i: (0, 0)), vec],
        out_shape=[jax.ShapeDtypeStruct((1, S, D), f32), jax.ShapeDtypeStruct((N_META, D), f32),
                   jax.ShapeDtypeStruct((1, D), f32)],
        args=[dn, h, dh, dn, h, dh, w], sem=("arbitrary",), side=side)
    return out[0], out[1], out[2], out[3:]


def _final(h1, ffn, tgt3, w):
    Lp, D = h1.shape
    _, S, _ = tgt3.shape
    tr = _tile(Lp, min(256, S), OFF)

    def body(h_ref, f_ref, t_ref, w_ref, d_ref, db_ref, l_ref, gw_ref):
        i = pl.program_id(0)
        h2 = h_ref[...] + f_ref[...]
        xh, r = _rms(h2)
        w_ = w_ref[...]
        t = t_ref[...]
        t = jnp.where(i == 0, pltpu.roll(t, OFF, 0), t)
        valid = (lax.broadcasted_iota(jnp.int32, (tr, 1), 0) + i * tr >= OFF).astype(f32)
        e = xh * w_ - t
        loss = 0.5 * jnp.sum(jnp.mean(e * e, axis=-1, keepdims=True) * valid, axis=0, keepdims=True)
        dy = e * (valid / D)
        d = _rms_bwd(dy, xh, r, w_)
        d_ref[...] = d
        db_ref[...] = d.astype(bf16)
        gw = jnp.sum(dy * xh, axis=0, keepdims=True)

        @pl.when(i == 0)
        def _():
            l_ref[...] = jnp.zeros_like(l_ref)
            gw_ref[...] = jnp.zeros_like(gw_ref)

        l_ref[...] += jnp.broadcast_to(loss, l_ref.shape)
        gw_ref[...] += gw

    row = pl.BlockSpec((tr, D), lambda i: (i, 0))
    vec = pl.BlockSpec((1, D), lambda i: (0, 0))
    tgt = pl.BlockSpec((pl.Element(tr), pl.Element(D)),
                       lambda i: (pl.multiple_of(jnp.maximum(i * tr - OFF, 0), OFF), 0))
    return pl.pallas_call(
        body, name="final_loss", grid=(Lp // tr,), in_specs=[row, row, tgt, vec],
        out_specs=[row, row, pl.BlockSpec((8, 128), lambda i: (0, 0)), vec],
        out_shape=[jax.ShapeDtypeStruct((Lp, D), f32), jax.ShapeDtypeStruct((Lp, D), bf16),
                   jax.ShapeDtypeStruct((8, 128), f32), jax.ShapeDtypeStruct((1, D), f32)],
        compiler_params=_params("arbitrary"),
    )(h1, ffn, tgt3.reshape(S, D), w)


def _ffn_in(n, w_gate, w_up, side=None):
    M, K = n.shape
    F = w_gate.shape[1]
    tm = _tile(M, 1408)
    tn = _tile(F, 512, 128)

    def body(a_ref, bg_ref, bu_ref, act_ref, pg_ref, pu_ref):
        a = a_ref[...]
        g = _mmb(a, bg_ref[...], NN)
        u = _mmb(a, bu_ref[...], NN)
        s = _sigmoid(g)
        gs = g * s
        act_ref[...] = (gs * u).astype(bf16)
        pg_ref[...] = (u * (s + gs * (1.0 - s))).astype(bf16)
        pu_ref[...] = gs.astype(bf16)

    wsp = pl.BlockSpec((K, tn), lambda i, j: (0, j))
    osp = pl.BlockSpec((tm, tn), lambda i, j: (i, j))
    out = _call(body, name="ffn_in", grid=(M // tm, F // tn),
                in_specs=[pl.BlockSpec((tm, K), lambda i, j: (i, 0)), wsp, wsp], out_specs=[osp] * 3,
                out_shape=[jax.ShapeDtypeStruct((M, F), bf16)] * 3, args=[n, w_gate, w_up],
                sem=("parallel", "parallel"), side=side)
    return out[0], out[1], out[2], out[3:]


def _ffn_dact(d, w_down, pg, pu):
    M, K = d.shape
    F = w_down.shape[0]
    tm = _tile(M, 1408)
    tn = _tile(F, 512, 128)

    def body(d_ref, w_ref, pg_ref, pu_ref, dg_ref, du_ref):
        da = _mmb(d_ref[...], w_ref[...], NT)
        dg_ref[...] = (da * pg_ref[...].astype(f32)).astype(bf16)
        du_ref[...] = (da * pu_ref[...].astype(f32)).astype(bf16)

    osp = pl.BlockSpec((tm, tn), lambda i, j: (i, j))
    return pl.pallas_call(
        body, name="ffn_dact", grid=(M // tm, F // tn),
        in_specs=[pl.BlockSpec((tm, K), lambda i, j: (i, 0)), pl.BlockSpec((tn, K), lambda i, j: (j, 0)), osp, osp],
        out_specs=[osp, osp], out_shape=[jax.ShapeDtypeStruct((M, F), bf16)] * 2,
        compiler_params=_params("parallel", "parallel"),
    )(d, w_down, pg, pu)


def _ffn_dn(dg, du, w_gate, w_up):
    M, F = dg.shape
    D = w_gate.shape[0]
    tm = _tile(M, 688)
    tn = _tile(D, 256, 128)

    def body(dg_ref, du_ref, wg_ref, wu_ref, o_ref):
        o_ref[...] = _mmb(dg_ref[...], wg_ref[...], NT) + _mmb(du_ref[...], wu_ref[...], NT)

    asp = pl.BlockSpec((tm, F), lambda i, j: (i, 0))
    wsp = pl.BlockSpec((tn, F), lambda i, j: (j, 0))
    return pl.pallas_call(
        body, name="d_n2", grid=(M // tm, D // tn), in_specs=[asp, asp, wsp, wsp],
        out_specs=pl.BlockSpec((tm, tn), lambda i, j: (i, j)), out_shape=jax.ShapeDtypeStruct((M, D), f32),
        compiler_params=_params("parallel", "parallel"),
    )(dg, du, w_gate, w_up)


def _gates(psm, w2p, gate_b, alog, dtb):
    Lp = psm.shape[0]
    tr = _tile(Lp, 256)

    def body(p_ref, w_ref, b_ref, a_ref, t_ref, gb_ref, la_ref):
        i = pl.program_id(0)
        psm_ = p_ref[...]
        lane = lax.broadcasted_iota(jnp.int32, psm_.shape, 1)
        rowi = lax.broadcasted_iota(jnp.int32, (tr, 1), 0) + i * tr
        g = -jnp.exp(a_ref[...]) * _softplus(psm_ + t_ref[...])
        beta = _sigmoid(psm_)
        gb = jnp.where(lane < GDN_H, g, jnp.where(lane < 2 * GDN_H, beta, 0.0))
        gb_ref[...] = gb * (rowi >= PAD).astype(f32)
        logit = _mmb(psm_, w_ref[...], NN) + b_ref[...]
        la_ref[...] = _log_sigmoid(logit) * (1.0 / GATE_NORMALIZER)

    row = pl.BlockSpec((tr, SM_W), lambda i: (i, 0))
    return pl.pallas_call(
        body, name="gates", grid=(Lp // tr,),
        in_specs=[row, pl.BlockSpec((SM_W, GLA_QK), lambda i: (0, 0)), pl.BlockSpec((1, GLA_QK), lambda i: (0, 0)),
                  pl.BlockSpec((1, SM_W), lambda i: (0, 0)), pl.BlockSpec((1, SM_W), lambda i: (0, 0))],
        out_specs=[row, pl.BlockSpec((tr, GLA_QK), lambda i: (i, 0))],
        out_shape=[jax.ShapeDtypeStruct((Lp, SM_W), f32), jax.ShapeDtypeStruct((Lp, GLA_QK), f32)],
        compiler_params=_params("parallel"),
    )(psm, w2p, gate_b, alog, dtb)


def _gates_bwd(psm, w2p, gate_b, alog, dtb, dgb, dla):
    Lp = psm.shape[0]
    tr = _tile(Lp, 256)

    def body(p_ref, w_ref, b_ref, a_ref, t_ref, dgb_ref, dla_ref, dp_ref, gw_ref, gb_ref, ga_ref, gt_ref):
        i = pl.program_id(0)
        psm_ = p_ref[...]
        lane = lax.broadcasted_iota(jnp.int32, psm_.shape, 1)
        rowi = lax.broadcasted_iota(jnp.int32, (tr, 1), 0) + i * tr
        d = dgb_ref[...] * (rowi >= PAD).astype(f32)
        ea = jnp.exp(a_ref[...])
        z = psm_ + t_ref[...]
        is_g = lane < GDN_H
        dz = jnp.where(is_g, -ea * _sigmoid(z) * d, 0.0)
        dalog = jnp.where(is_g, -ea * _softplus(z) * d, 0.0)
        beta = _sigmoid(psm_)
        dbeta = jnp.where(jnp.logical_and(lane >= GDN_H, lane < 2 * GDN_H), beta * (1.0 - beta) * d, 0.0)
        logit = _mmb(psm_, w_ref[...], NN) + b_ref[...]
        dlogit = dla_ref[...] * (_sigmoid(-logit) * (1.0 / GATE_NORMALIZER))
        dlr = _mmb(dlogit, w_ref[...], NT)
        dp_ref[...] = (dz + dbeta + dlr).astype(bf16)
        gw = _mmb(psm_, dlogit, TN)
        gb = jnp.sum(dlogit, axis=0, keepdims=True)
        ga = jnp.sum(dalog, axis=0, keepdims=True)
        gt = jnp.sum(dz, axis=0, keepdims=True)

        @pl.when(i == 0)
        def _():
            gw_ref[...] = gw
            gb_ref[...] = gb
            ga_ref[...] = ga
            gt_ref[...] = gt

        @pl.when(i > 0)
        def _():
            gw_ref[...] += gw
            gb_ref[...] += gb
            ga_ref[...] += ga
            gt_ref[...] += gt

    row = pl.BlockSpec((tr, SM_W), lambda i: (i, 0))
    wsp = pl.BlockSpec((SM_W, GLA_QK), lambda i: (0, 0))
    bsp = pl.BlockSpec((1, GLA_QK), lambda i: (0, 0))
    vsp = pl.BlockSpec((1, SM_W), lambda i: (0, 0))
    return pl.pallas_call(
        body, name="gates_bwd", grid=(Lp // tr,),
        in_specs=[row, wsp, bsp, vsp, vsp, row, pl.BlockSpec((tr, GLA_QK), lambda i: (i, 0))],
        out_specs=[row, wsp, bsp, vsp, vsp],
        out_shape=[jax.ShapeDtypeStruct((Lp, SM_W), bf16), jax.ShapeDtypeStruct((SM_W, GLA_QK), f32),
                   jax.ShapeDtypeStruct((1, GLA_QK), f32), jax.ShapeDtypeStruct((1, SM_W), f32),
                   jax.ShapeDtypeStruct((1, SM_W), f32)],
        compiler_params=_params("arbitrary"),
    )(psm, w2p, gate_b, alog, dtb, dgb, dla)


def _conv_pre(x_ext, w, n):
    rows = x_ext.shape[0]
    y = x_ext * w[CONV_K - 1:CONV_K, :]
    for s in range(1, CONV_K):
        y = y + pltpu.roll(x_ext, s, 0) * w[CONV_K - 1 - s:CONV_K - s, :]
    return y[rows - n:, :]


def _conv(proj, cw, side=None):
    Lp = proj.shape[0]
    W = cw.shape[1]
    tr = _tile(Lp, 256, 64)
    tc = _tile(W, 1024, 128)
    c0 = C_QKV // tc

    def body(h_ref, x_ref, w_ref, o_ref):
        i = pl.program_id(1)
        halo = jnp.where(i == 0, 0.0, h_ref[...])
        x_ext = jnp.concatenate([halo, x_ref[...]], axis=0)
        o_ref[...] = _silu(_conv_pre(x_ext, w_ref[...], tr))

    out = _call(
        body, name="conv", grid=(W // tc, Lp // tr),
        in_specs=[pl.BlockSpec((8, tc), lambda j, i: (jnp.maximum(i * (tr // 8) - 1, 0), j + c0)),
                  pl.BlockSpec((tr, tc), lambda j, i: (i, j + c0)),
                  pl.BlockSpec((CONV_K, tc), lambda j, i: (0, j))],
        out_specs=[pl.BlockSpec((tr, tc), lambda j, i: (i, j))],
        out_shape=[jax.ShapeDtypeStruct((Lp, W), f32)], args=[proj, proj, cw],
        sem=("parallel", "parallel"), side=side)
    return out[0] if side is None else (out[0], out[1:])


def _conv_bwd(proj, cw, dy, dproj, side=None):
    Lp = proj.shape[0]
    W = cw.shape[1]
    tr = _tile(Lp, 256, 64)
    tc = _tile(W, 1024, 128)
    c0 = C_QKV // tc
    nr = Lp // tr
    last8 = Lp // 8 - 1

    def body(xp_ref, x_ref, xn_ref, w_ref, d_ref, dn_ref, dproj_ref, o_ref, gw_ref):
        del dproj_ref
        i = pl.program_id(1)
        w = w_ref[...]
        xp = jnp.where(i == 0, 0.0, xp_ref[...])
        x_ext = jnp.concatenate([xp, x_ref[...], xn_ref[...]], axis=0)
        n = tr + 8
        pre = _conv_pre(x_ext, w, n)
        dn = jnp.where(i == nr - 1, 0.0, dn_ref[...])
        dpre = jnp.concatenate([d_ref[...], dn], axis=0) * _dsilu(pre)
        dx = dpre * w[CONV_K - 1:CONV_K, :]
        for s in range(1, CONV_K):
            dx = dx + pltpu.roll(dpre, n - s, 0) * w[CONV_K - 1 - s:CONV_K - s, :]
        o_ref[...] = dx[:tr, :].astype(bf16)
        dp = dpre[:tr, :]
        rows = []
        for k in range(CONV_K):
            xs = x_ext if k == CONV_K - 1 else pltpu.roll(x_ext, CONV_K - 1 - k, 0)
            rows.append(jnp.sum(dp * xs[8:8 + tr, :], axis=0, keepdims=True))
        gw = jnp.concatenate(rows, axis=0)

        @pl.when(i == 0)
        def _():
            gw_ref[...] = gw

        @pl.when(i > 0)
        def _():
            gw_ref[...] += gw

    cur = pl.BlockSpec((tr, tc), lambda j, i: (i, j))
    nxt = pl.BlockSpec((8, tc), lambda j, i: (jnp.minimum((i + 1) * (tr // 8), last8), j))
    pcur = pl.BlockSpec((tr, tc), lambda j, i: (i, j + c0))
    pprev = pl.BlockSpec((8, tc), lambda j, i: (jnp.maximum(i * (tr // 8) - 1, 0), j + c0))
    pnext = pl.BlockSpec((8, tc), lambda j, i: (jnp.minimum((i + 1) * (tr // 8), last8), j + c0))
    wsp = pl.BlockSpec((CONV_K, tc), lambda j, i: (0, j))
    out = _call(
        body, name="conv_bwd", grid=(W // tc, nr),
        in_specs=[pprev, pcur, pnext, wsp, cur, nxt, _ANY], out_specs=[pcur, wsp],
        out_shape=[jax.ShapeDtypeStruct(dproj.shape, dproj.dtype), jax.ShapeDtypeStruct((CONV_K, W), f32)],
        aliases={6: 0}, args=[proj, proj, proj, cw, dy, dy, dproj], sem=("parallel", "arbitrary"), side=side)
    return out[0], out[1], out[2:]


GDN_FWD_GROUP = 3


def _gdn_group(Lp, most):
    n = Lp // GDN_C
    return next(g for g in range(most, 0, -1) if n % g == 0)


def _gdn_heads(x_ref, gb_ref, group):
    qs, ks, vs, bs, gs = [], [], [], [], []
    for chunk in range(group):
        r = slice(chunk * GDN_C, (chunk + 1) * GDN_C)
        gbv = gb_ref[r, :]
        for h in range(GDN_H):
            qs.append(x_ref[r, Q0 + h * GDN_DK:Q0 + (h + 1) * GDN_DK])
            ks.append(x_ref[r, K0 + h * GDN_DK:K0 + (h + 1) * GDN_DK])
            vs.append(x_ref[r, V0 + h * GDN_DV:V0 + (h + 1) * GDN_DV])
            bs.append(gbv[:, GDN_H + h:GDN_H + h + 1])
            gs.append(gbv[:, h:h + 1])
    return qs, ks, vs, bs, gs


def _gdn_fwd(qkvc, gb, side=None):
    Lp = qkvc.shape[0]
    group = _gdn_group(Lp, GDN_FWD_GROUP)
    rows = group * GDN_C
    steps = Lp // rows
    R = range(GDN_H)

    def body(x_ref, gb_ref, o_ref, sall_ref, pall_ref, s_scr):
        @pl.when(pl.program_id(0) == 0)
        def _():
            s_scr[...] = jnp.zeros_like(s_scr)

        S2, o, p, entering = _gdn_chunk([s_scr[h] for h in R], *_gdn_heads(x_ref, gb_ref, group))
        for h in R:
            s_scr[h] = S2[h]
        for chunk in range(group):
            for h in R:
                i = chunk * GDN_H + h
                o_ref[chunk * GDN_C:(chunk + 1) * GDN_C, h * GDN_DV:(h + 1) * GDN_DV] = o[i]
                pall_ref[chunk, h] = p[i]
                sall_ref[chunk, h] = entering[i]

    out = _call(
        body, name="gdn_fwd", grid=(steps,),
        in_specs=[pl.BlockSpec((rows, QKV_W), lambda n: (n, 0)), pl.BlockSpec((rows, SM_W), lambda n: (n, 0))],
        out_specs=[pl.BlockSpec((rows, GDN_V), lambda n: (n, 0)),
                   pl.BlockSpec((group, GDN_H, GDN_DK, GDN_DV), lambda n: (n, 0, 0, 0)),
                   pl.BlockSpec((group, GDN_H, GDN_C, GDN_C), lambda n: (n, 0, 0, 0))],
        out_shape=[jax.ShapeDtypeStruct((Lp, GDN_V), f32),
                   jax.ShapeDtypeStruct((Lp // GDN_C, GDN_H, GDN_DK, GDN_DV), f32),
                   jax.ShapeDtypeStruct((Lp // GDN_C, GDN_H, GDN_C, GDN_C), f32)],
        scratch_shapes=[pltpu.VMEM((GDN_H, GDN_DK, GDN_DV), f32)], args=[qkvc, gb], sem=("arbitrary",), side=side)
    return out[0], out[1], out[2], out[3:]


def _gdn_bwd(qkvc, gb, sall, pall, do, side=None):
    Lp = qkvc.shape[0]
    group = 1
    rows = group * GDN_C
    steps = Lp // rows
    R = range(GDN_H)

    def body(x_ref, gb_ref, sall_ref, pall_ref, do_ref, dx_ref, dgb_ref, ds_scr):
        @pl.when(pl.program_id(0) == 0)
        def _():
            ds_scr[...] = jnp.zeros_like(ds_scr)

        lane = lax.broadcasted_iota(jnp.int32, (GDN_C, SM_W), 1)
        ps = [pall_ref[chunk, h] for chunk in range(group) for h in R]
        _, vjp = jax.vjp(lambda *a: _gdn_chunk(*a, Ps=ps)[:2],
                         [sall_ref[0, h] for h in R], *_gdn_heads(x_ref, gb_ref, group))
        do = [do_ref[chunk * GDN_C:(chunk + 1) * GDN_C, h * GDN_DV:(h + 1) * GDN_DV]
              for chunk in range(group) for h in R]
        dS, dq, dk, dv, dbeta, dg = vjp(([ds_scr[h] for h in R], do))
        for h in R:
            ds_scr[h] = dS[h]
        for chunk in range(group):
            r = slice(chunk * GDN_C, (chunk + 1) * GDN_C)
            acc = jnp.zeros((GDN_C, SM_W), f32)
            for h in R:
                i = chunk * GDN_H + h
                dx_ref[r, Q0 + h * GDN_DK:Q0 + (h + 1) * GDN_DK] = dq[i]
                dx_ref[r, K0 + h * GDN_DK:K0 + (h + 1) * GDN_DK] = dk[i]
                dx_ref[r, V0 + h * GDN_DV:V0 + (h + 1) * GDN_DV] = dv[i]
                acc = acc + jnp.where(lane == h, dg[i], 0.0) + jnp.where(lane == GDN_H + h, dbeta[i], 0.0)
            dgb_ref[r, :] = acc

    rev = lambda n: (steps - 1 - n, 0)
    out = _call(
        body, name="gdn_bwd", grid=(steps,),
        in_specs=[pl.BlockSpec((rows, QKV_W), rev), pl.BlockSpec((rows, SM_W), rev),
                  pl.BlockSpec((1, GDN_H, GDN_DK, GDN_DV), lambda n: (steps - 1 - n, 0, 0, 0)),
                  pl.BlockSpec((group, GDN_H, GDN_C, GDN_C), lambda n: (steps - 1 - n, 0, 0, 0)),
                  pl.BlockSpec((rows, GDN_V), rev)],
        out_specs=[pl.BlockSpec((rows, QKV_W), rev), pl.BlockSpec((rows, SM_W), rev)],
        out_shape=[jax.ShapeDtypeStruct((Lp, QKV_W), f32), jax.ShapeDtypeStruct((Lp, SM_W), f32)],
        scratch_shapes=[pltpu.VMEM((GDN_H, GDN_DK, GDN_DV), f32)], args=[qkvc, gb, sall, pall, do],
        sem=("arbitrary",), side=side)
    return out[0], out[1], out[2:]


GLA_BLOCK = 64


def _gla_group(Lp):
    nb = Lp // GLA_BLOCK
    return next(g for g in (3, 2, 1) if nb % g == 0)


def _gla_slices(h):
    sq = slice(h * GLA_DK, (h + 1) * GLA_DK)
    sk = slice(GLA_QK + h * GLA_DK, GLA_QK + (h + 1) * GLA_DK)
    sv = slice(2 * GLA_QK + h * GLA_DV, 2 * GLA_QK + (h + 1) * GLA_DV)
    return sq, sk, sv


def _gla_heads(x_ref, la_ref, group):
    qs, ks, vs, ls = [], [], [], []
    for blk in range(group):
        r = slice(blk * GLA_BLOCK, (blk + 1) * GLA_BLOCK)
        for h in range(GLA_H):
            sq, sk, sv = _gla_slices(h)
            qs.append(x_ref[r, sq])
            ks.append(x_ref[r, sk])
            vs.append(x_ref[r, sv])
            ls.append(la_ref[r, sq])
    return qs, ks, vs, ls


def _gla_fwd(proj, la):
    Lp = proj.shape[0]
    group = _gla_group(Lp)
    rows = group * GLA_BLOCK
    steps = Lp // rows
    R = range(GLA_H)

    def body(x_ref, la_ref, o_ref, sall_ref, s_scr):
        @pl.when(pl.program_id(0) == 0)
        def _():
            s_scr[...] = jnp.zeros_like(s_scr)

        Sts = [s_scr[h] for h in R]
        for h in R:
            sall_ref[0, h] = Sts[h]
        St2, o = _gla_blocks(Sts, *_gla_heads(x_ref, la_ref, group))
        for h in R:
            s_scr[h] = St2[h]
        for blk in range(group):
            for h in R:
                o_ref[blk * GLA_BLOCK:(blk + 1) * GLA_BLOCK, h * GLA_DV:(h + 1) * GLA_DV] = o[blk * GLA_H + h]

    return pl.pallas_call(
        body, name="gla_fwd", grid=(steps,),
        in_specs=[pl.BlockSpec((rows, G_W), lambda n: (n, C_G // G_W)),
                  pl.BlockSpec((rows, GLA_QK), lambda n: (n, 0))],
        out_specs=[pl.BlockSpec((rows, GLA_V), lambda n: (n, 0)),
                   pl.BlockSpec((1, GLA_H, GLA_DV, GLA_DK), lambda n: (n, 0, 0, 0))],
        out_shape=[jax.ShapeDtypeStruct((Lp, GLA_V), f32),
                   jax.ShapeDtypeStruct((steps, GLA_H, GLA_DV, GLA_DK), f32)],
        scratch_shapes=[pltpu.VMEM((GLA_H, GLA_DV, GLA_DK), f32)],
        compiler_params=_params("arbitrary"),
    )(proj, la)


def _gla_bwd(proj, la, sall, do, dproj, side=None):
    Lp = proj.shape[0]
    group = _gla_group(Lp)
    rows = group * GLA_BLOCK
    steps = Lp // rows
    R = range(GLA_H)

    def body(x_ref, la_ref, sall_ref, do_ref, dproj_ref, dx_ref, dla_ref, ds_scr):
        del dproj_ref

        @pl.when(pl.program_id(0) == 0)
        def _():
            ds_scr[...] = jnp.zeros_like(ds_scr)

        _, vjp = jax.vjp(_gla_blocks, [sall_ref[0, h] for h in R], *_gla_heads(x_ref, la_ref, group))
        do = [do_ref[blk * GLA_BLOCK:(blk + 1) * GLA_BLOCK, h * GLA_DV:(h + 1) * GLA_DV]
              for blk in range(group) for h in R]
        dS, dq, dk, dv, dl = vjp(([ds_scr[h] for h in R], do))
        for h in R:
            ds_scr[h] = dS[h]
        for blk in range(group):
            r = slice(blk * GLA_BLOCK, (blk + 1) * GLA_BLOCK)
            for h in R:
                sq, sk, sv = _gla_slices(h)
                i = blk * GLA_H + h
                dx_ref[r, sq] = dq[i].astype(bf16)
                dx_ref[r, sk] = dk[i].astype(bf16)
                dx_ref[r, sv] = dv[i].astype(bf16)
                dla_ref[r, sq] = dl[i]

    x_spec = pl.BlockSpec((rows, G_W), lambda n: (steps - 1 - n, C_G // G_W))
    rev = lambda n: (steps - 1 - n, 0)
    out = _call(
        body, name="gla_bwd", grid=(steps,),
        in_specs=[x_spec, pl.BlockSpec((rows, GLA_QK), rev),
                  pl.BlockSpec((1, GLA_H, GLA_DV, GLA_DK), lambda n: (steps - 1 - n, 0, 0, 0)),
                  pl.BlockSpec((rows, GLA_V), rev), _ANY],
        out_specs=[x_spec, pl.BlockSpec((rows, GLA_QK), rev)],
        out_shape=[jax.ShapeDtypeStruct(dproj.shape, dproj.dtype), jax.ShapeDtypeStruct((Lp, GLA_QK), f32)],
        aliases={4: 0}, scratch_shapes=[pltpu.VMEM((GLA_H, GLA_DV, GLA_DK), f32)],
        args=[proj, la, sall, do, dproj], sem=("arbitrary",), side=side)
    return out[0], out[1], out[2:]


def _gated_norm_fn(og, ol, zr, wg, wl):
    outs = []
    for h in range(GDN_H):
        s = slice(h * GDN_DV, (h + 1) * GDN_DV)
        outs.append(_rms(og[:, s])[0] * wg * _silu(zr[:, s]))
    for h in range(GLA_H):
        s = slice(h * GLA_DV, (h + 1) * GLA_DV)
        sr = slice(GDN_V + h * GLA_DV, GDN_V + (h + 1) * GLA_DV)
        outs.append(_rms(ol[:, s])[0] * wl * _silu(zr[:, sr]))
    return jnp.concatenate(outs, axis=-1)


def _gated_norm(og, ol, proj, wg, wl):
    Lp = og.shape[0]
    tr = _tile(Lp, 256)

    def body(og_ref, ol_ref, zr_ref, wg_ref, wl_ref, o_ref):
        o_ref[...] = _gated_norm_fn(og_ref[...], ol_ref[...], zr_ref[...], wg_ref[...], wl_ref[...]).astype(bf16)

    return pl.pallas_call(
        body, name="gated_norm", grid=(Lp // tr,),
        in_specs=[pl.BlockSpec((tr, GDN_V), lambda i: (i, 0)), pl.BlockSpec((tr, GLA_V), lambda i: (i, 0)),
                  pl.BlockSpec((tr, ZR_W), lambda i: (i, C_ZR // ZR_W)),
                  pl.BlockSpec((1, GDN_DV), lambda i: (0, 0)), pl.BlockSpec((1, GLA_DV), lambda i: (0, 0))],
        out_specs=pl.BlockSpec((tr, ZR_W), lambda i: (i, 0)),
        out_shape=jax.ShapeDtypeStruct((Lp, ZR_W), bf16),
        compiler_params=_params("parallel"),
    )(og, ol, proj, wg, wl)


def _gated_norm_bwd(og, ol, proj, wg, wl, dmix):
    Lp = og.shape[0]
    tr = _tile(Lp, 128)

    def body(og_ref, ol_ref, zr_ref, wg_ref, wl_ref, d_ref, dog_ref, dol_ref, dzr_ref, gwg_ref, gwl_ref):
        i = pl.program_id(0)
        _, vjp = jax.vjp(_gated_norm_fn, og_ref[...], ol_ref[...], zr_ref[...], wg_ref[...], wl_ref[...])
        dog, dol, dzr, gwg, gwl = vjp(d_ref[...])
        dog_ref[...] = dog
        dol_ref[...] = dol
        dzr_ref[...] = dzr.astype(bf16)

        @pl.when(i == 0)
        def _():
            gwg_ref[...] = gwg
            gwl_ref[...] = gwl

        @pl.when(i > 0)
        def _():
            gwg_ref[...] += gwg
            gwl_ref[...] += gwl

    og_spec = pl.BlockSpec((tr, GDN_V), lambda i: (i, 0))
    ol_spec = pl.BlockSpec((tr, GLA_V), lambda i: (i, 0))
    zr_spec = pl.BlockSpec((tr, ZR_W), lambda i: (i, C_ZR // ZR_W))
    vg = pl.BlockSpec((1, GDN_DV), lambda i: (0, 0))
    vl = pl.BlockSpec((1, GLA_DV), lambda i: (0, 0))
    return pl.pallas_call(
        body, name="gated_norm_bwd", grid=(Lp // tr,),
        in_specs=[og_spec, ol_spec, zr_spec, vg, vl, pl.BlockSpec((tr, ZR_W), lambda i: (i, 0))],
        out_specs=[og_spec, ol_spec, zr_spec, vg, vl],
        out_shape=[jax.ShapeDtypeStruct((Lp, GDN_V), f32), jax.ShapeDtypeStruct((Lp, GLA_V), f32),
                   jax.ShapeDtypeStruct((Lp, C_END), bf16),
                   jax.ShapeDtypeStruct((1, GDN_DV), f32), jax.ShapeDtypeStruct((1, GLA_DV), f32)],
        compiler_params=_params("arbitrary"),
    )(og, ol, proj, wg, wl, dmix)


def _adamw(g, w, m, v, name):
    R, C = g.shape
    tr = _tile(R, 256, 8) if R % 8 == 0 and R > 256 else R
    c1 = 1.0 - ADAM_B1 ** ADAM_STEP
    c2 = 1.0 - ADAM_B2 ** ADAM_STEP

    def body(g_ref, w_ref, m_ref, v_ref, d_ref, mo_ref, vo_ref):
        g_ = g_ref[...]
        m2 = ADAM_B1 * m_ref[...] + (1.0 - ADAM_B1) * g_
        v2 = ADAM_B2 * v_ref[...] + (1.0 - ADAM_B2) * (g_ * g_)
        mo_ref[...] = m2
        vo_ref[...] = v2
        d_ref[...] = -ADAM_LR * ((m2 / c1) / (jnp.sqrt(v2 / c2) + ADAM_EPS) + ADAM_WD * w_ref[...])

    blk = pl.BlockSpec((tr, C), lambda i: (i, 0))
    return pl.pallas_call(
        body, name=name, grid=(R // tr,), in_specs=[blk] * 4, out_specs=[blk] * 3,
        out_shape=[jax.ShapeDtypeStruct((R, C), f32)] * 3,
        compiler_params=_params("parallel"),
    )(g, w, m, v)


def _sum_slots(r, name):
    n, R, C = r.shape
    tr = _tile(R, 128, 16) if R % 16 == 0 and R > 128 else R

    def body(r_ref, o_ref):
        acc = r_ref[0].astype(f32)
        for s in range(1, n):
            acc = acc + r_ref[s].astype(f32)
        o_ref[...] = acc

    return pl.pallas_call(
        body, name=name, grid=(R // tr,),
        in_specs=[pl.BlockSpec((n, tr, C), lambda i: (0, i, 0))],
        out_specs=pl.BlockSpec((tr, C), lambda i: (i, 0)),
        out_shape=jax.ShapeDtypeStruct((R, C), f32),
        compiler_params=_params("parallel"),
    )(r)


SIBLING_PARTS = 8


class _Siblings:
    def __init__(self, arrays):
        self.arrays = list(arrays)
        self.n = len(self.arrays)
        self.parts = [next(p for p in range(SIBLING_PARTS, 0, -1) if a.shape[0] % (8 * p) == 0 or p == 1)
                      for a in self.arrays]
        total = sum(self.parts)
        self.out_shape = [jax.ShapeDtypeStruct((2,) + a.shape, a.dtype) for a in self.arrays]
        self.sems = [pltpu.SemaphoreType.DMA((total,)), pltpu.SemaphoreType.DMA((total,)),
                     pltpu.SemaphoreType.DMA((self.n,))]

    def hooks(self, ins, outs, send, recv, lsem):
        def copies():
            x, y, c = lax.axis_index("x"), lax.axis_index("y"), lax.axis_index("c")
            out, k = [], 0
            for a in range(self.n):
                out.append((pltpu.make_async_copy(ins[a], outs[a].at[c], lsem.at[a]), None))
                rows = self.arrays[a].shape[0] // self.parts[a]
                for part in range(self.parts[a]):
                    r = pl.ds(part * rows, rows)
                    mk = lambda dst, a=a, r=r, k=k: pltpu.make_async_remote_copy(
                        src_ref=ins[a].at[r], dst_ref=dst.at[r], send_sem=send.at[k], recv_sem=recv.at[k],
                        device_id=(x, y, 1 - c), device_id_type=MESH)
                    out.append((mk(outs[a].at[c]), mk(outs[a].at[1 - c])))
                    k += 1
            return out

        return _start_wait(copies)


def _comm_now(name, sides):
    total = sum(s.n for s in sides)

    def body(*refs):
        ins, outs, sems = refs[:total], refs[total:2 * total], refs[2 * total:]
        hooks, o = [], 0
        for i, s in enumerate(sides):
            hooks.append(s.hooks(ins[o:o + s.n], outs[o:o + s.n], *sems[3 * i:3 * i + 3]))
            o += s.n
        for start, _ in hooks:
            start()
        for _, wait in hooks:
            wait()

    out = pl.pallas_call(
        body, name=name, in_specs=[_ANY] * total, out_specs=[_ANY] * total,
        out_shape=[sh for s in sides for sh in s.out_shape], scratch_shapes=[sm for s in sides for sm in s.sems],
    )(*[a for s in sides for a in s.arrays])
    res, o = [], 0
    for s in sides:
        res.append(list(out[o:o + s.n]))
        o += s.n
    return res


def _cat_cols(g):
    return jnp.concatenate([g[i] for i in range(N_CHIP)], axis=-1)


def _row_slabs(a):
    return a.reshape(N_DEV, a.shape[0] // N_DEV, a.shape[1])


def _w_in_columns(g_wp, g_wsm):
    return jnp.concatenate([g_wp[:, C_QKV:C_END], g_wp[:, C_ZR:C_ZR + GDN_V], g_wsm[:, :SM_LR],
                            g_wp[:, C_G:C_G + G_W], g_wp[:, C_ZR + GDN_V:C_ZR + ZR_W],
                            g_wsm[:, SM_LR:SM_LR + GATE_RANK]], axis=1)


def _step(x, loss_target, p, meta, shard):
    _, S, D = x.shape
    alog_p = jnp.pad(p["gdn_a_log"], ((0, 0), (0, SM_W - GDN_H)))
    dtb_p = jnp.pad(p["gdn_dt_bias"], ((0, 0), (0, SM_W - GDN_H)))
    m64 = jnp.concatenate([jnp.zeros((PAD, D), f32), meta], axis=0)
    gate_b, gdn_norm_w, gla_norm_w = p["gla_gate_b"], p["gdn_norm_w"], p["gla_norm_w"]
    half = shard["w_up"].shape[0] // 2

    h0, n1, (w_in4, conv4, w24) = _embed_norm(
        x, m64, p["attn_norm_w"], side=_Gather([shard["w_in"], shard["gdn_conv_w"], shard["gla_gate_w2"]]))
    w_in, conv_w, w2 = _cat_cols(w_in4), _cat_cols(conv4), _cat_cols(w24)
    wp = jnp.concatenate([w_in[:, R_Z:R_AB], w_in[:, R_GR:R_LR], w_in[:, R_G:R_GR], w_in[:, R_QKV:R_Z]], axis=1)
    wsm = jnp.concatenate([w_in[:, R_AB:R_G], w_in[:, R_LR:R_END],
                           jnp.zeros((D, SM_W - SM_LR - GATE_RANK), w_in.dtype)], axis=1)
    w2p = jnp.pad(w2, ((SM_LR, SM_W - SM_LR - GATE_RANK), (0, 0)))
    proj, (w_out4, w_up4a) = _mm(n1, wp, "nn", "proj", side=_Gather([shard["w_out"], shard["w_up"][:half]]))
    w_out = w_out4.reshape(-1, D)
    psm = _mm(n1, wsm, "nn", "proj_small")
    gb, la = _gates(psm, w2p, gate_b, alog_p, dtb_p)
    qkvc, (w_up4b,) = _conv(proj, conv_w, side=_Gather([shard["w_up"][half:]]))
    w_up = jnp.concatenate([_cat_cols(w_up4a), _cat_cols(w_up4b)], axis=0)
    og, sall, pall, (w_gate4,) = _gdn_fwd(qkvc, gb, side=_Gather([shard["w_gate"]]))
    w_gate = _cat_cols(w_gate4)
    ol, stall = _gla_fwd(proj, la)
    mixed = _gated_norm(og, ol, proj, gdn_norm_w, gla_norm_w)
    attn = _mm(mixed, w_out, "nn", "out_proj")
    h1, n2 = _add_norm(h0, attn, p["ffn_norm_w"])
    act, act_dgate, act_dup, (w_down4,) = _ffn_in(n2, w_gate, w_up, side=_Gather([shard["w_down"]]))
    w_down = w_down4.reshape(-1, D)
    ffn = _mm(act, w_down, "nn", "ffn_down", **WHOLE_K)
    dh2, dh2b, lossp, g_final = _final(h1, ffn, loss_target, p["final_norm_w"])

    g_down = _mm(act, dh2b, "tn", "g_w_down", out_dtype=bf16, **WHOLE_K_T)
    dg, du = _ffn_dact(dh2b, w_down, act_dgate, act_dup)
    g_gate = _mm(n2, dg, "tn", "g_w_gate", tm_cap=512, tn_cap=1408, tk_cap=2752, out_dtype=bf16, col_slabs=True)
    g_up = _mm(n2, du, "tn", "g_w_up", tm_cap=512, tn_cap=1408, tk_cap=2752, out_dtype=bf16, col_slabs=True)
    dn2 = _ffn_dn(dg, du, w_gate, w_up)
    dh1, dh1b, g_ffn_norm = _norm_bwd(dn2, h1, dh2, p["ffn_norm_w"])
    dmix = _mm(dh1b, w_out, "nt", "d_mixed")
    g_out = _mm(mixed, dh1b, "tn", "g_w_out", out_dtype=bf16, **WHOLE_K_T)
    dog, dol, dproj, g_gdn_norm, g_gla_norm = _gated_norm_bwd(og, ol, proj, gdn_norm_w, gla_norm_w, dmix)
    dproj, dla, (r_down,) = _gla_bwd(proj, la, stall, dol, dproj, side=_Exchange([_row_slabs(g_down)]))
    dqkvc, dgb, (r_gate, r_up, r_out, h_down) = _gdn_bwd(
        qkvc, gb, sall, pall, dog,
        side=_Sides(_Exchange([g_gate, g_up, _row_slabs(g_out)]), _Siblings([_sum_slots(r_down, "sum_w_down")])))
    dproj, g_conv, (h_gate,) = _conv_bwd(proj, conv_w, dqkvc, dproj,
                                         side=_Siblings([_sum_slots(r_gate, "sum_w_gate")]))
    dpsm, g_w2p, g_gate_b, g_alog, g_dtb = _gates_bwd(psm, w2p, gate_b, alog_p, dtb_p, dgb, dla)
    g_wp, (h_up, h_out) = _mm(
        n1, dproj, "tn", "g_w_in", out_dtype=bf16,
        side=_Siblings([_sum_slots(r_up, "sum_w_up"), _sum_slots(r_out, "sum_w_out")]), **WHOLE_K_T)
    g_wsm = _mm(n1, dpsm, "tn", "g_w_in_small", out_dtype=bf16, **WHOLE_K_T)
    dn1, r_in = _mm(dproj, wp, "nt", "d_n1", side=_Exchange([_row_slabs(g_wp), _row_slabs(g_wsm)]), **WHOLE_K)
    dn1 = _mm(dpsm, wsm, "nt", "d_n1_small", acc_in=dn1)
    s_in = _w_in_columns(_sum_slots(r_in[0], "sum_w_in"), _sum_slots(r_in[1], "sum_w_in_small"))
    in_by_chip = s_in.reshape(s_in.shape[0], N_CHIP, -1).transpose(1, 0, 2)
    grad_x, g_meta, g_attn_norm, (h_in,) = _embed_norm_bwd(dn1, h0, dh1, p["attn_norm_w"], S,
                                                           side=_Exchange([], by_chip=[in_by_chip]))

    received = dict(w_in=h_in, w_gate=h_gate, w_up=h_up, w_out=h_out, w_down=h_down)
    small = dict(
        meta_tokens=g_meta, attn_norm_w=g_attn_norm, gdn_conv_w=g_conv, gdn_a_log=g_alog[:, :GDN_H],
        gdn_dt_bias=g_dtb[:, :GDN_H], gdn_norm_w=g_gdn_norm, gla_gate_w2=g_w2p[SM_LR:SM_LR + GATE_RANK],
        gla_gate_b=g_gate_b, gla_norm_w=g_gla_norm, ffn_norm_w=g_ffn_norm, final_norm_w=g_final)
    return lossp[0, 0], grad_x, received, small


_WEIGHTS = ("meta_tokens", "attn_norm_w", "w_in", "gdn_conv_w", "gdn_a_log", "gdn_dt_bias", "gdn_norm_w",
            "gla_gate_w2", "gla_gate_b", "gla_norm_w", "w_out", "ffn_norm_w", "w_gate", "w_up", "w_down",
            "final_norm_w")
_BIG_COLS = ("w_in", "w_gate", "w_up")
_BIG_ROWS = ("w_out", "w_down")
_SMALL_SHARDED = ("meta_tokens", "gdn_conv_w", "gla_gate_w2")


def kernel(x, meta_tokens, attn_norm_w, w_in, gdn_conv_w, gdn_a_log, gdn_dt_bias, gdn_norm_w, gla_gate_w2, gla_gate_b, gla_norm_w, w_out, ffn_norm_w, w_gate, w_up, w_down, final_norm_w, loss_target, m_meta_tokens, m_attn_norm_w, m_w_in, m_gdn_conv_w, m_gdn_a_log, m_gdn_dt_bias, m_gdn_norm_w, m_gla_gate_w2, m_gla_gate_b, m_gla_norm_w, m_w_out, m_ffn_norm_w, m_w_gate, m_w_up, m_w_down, m_final_norm_w, v_meta_tokens, v_attn_norm_w, v_w_in, v_gdn_conv_w, v_gdn_a_log, v_gdn_dt_bias, v_gdn_norm_w, v_gla_gate_w2, v_gla_gate_b, v_gla_norm_w, v_w_out, v_ffn_norm_w, v_w_gate, v_w_up, v_w_down, v_final_norm_w):
    w = dict(meta_tokens=meta_tokens, attn_norm_w=attn_norm_w, w_in=w_in, gdn_conv_w=gdn_conv_w, gdn_a_log=gdn_a_log,
             gdn_dt_bias=gdn_dt_bias, gdn_norm_w=gdn_norm_w, gla_gate_w2=gla_gate_w2, gla_gate_b=gla_gate_b,
             gla_norm_w=gla_norm_w, w_out=w_out, ffn_norm_w=ffn_norm_w, w_gate=w_gate, w_up=w_up, w_down=w_down,
             final_norm_w=final_norm_w)
    m = dict(meta_tokens=m_meta_tokens, attn_norm_w=m_attn_norm_w, w_in=m_w_in, gdn_conv_w=m_gdn_conv_w,
             gdn_a_log=m_gdn_a_log, gdn_dt_bias=m_gdn_dt_bias, gdn_norm_w=m_gdn_norm_w, gla_gate_w2=m_gla_gate_w2,
             gla_gate_b=m_gla_gate_b, gla_norm_w=m_gla_norm_w, w_out=m_w_out, ffn_norm_w=m_ffn_norm_w,
             w_gate=m_w_gate, w_up=m_w_up, w_down=m_w_down, final_norm_w=m_final_norm_w)
    v = dict(meta_tokens=v_meta_tokens, attn_norm_w=v_attn_norm_w, w_in=v_w_in, gdn_conv_w=v_gdn_conv_w,
             gdn_a_log=v_gdn_a_log, gdn_dt_bias=v_gdn_dt_bias, gdn_norm_w=v_gdn_norm_w, gla_gate_w2=v_gla_gate_w2,
             gla_gate_b=v_gla_gate_b, gla_norm_w=v_gla_norm_w, w_out=v_w_out, ffn_norm_w=v_ffn_norm_w,
             w_gate=v_w_gate, w_up=v_w_up, w_down=v_w_down, final_norm_w=v_final_norm_w)
    chip = 2 * lax.axis_index("x") + lax.axis_index("y")

    def two_d(a):
        return a.reshape(1, -1) if a.ndim == 1 else a.reshape(-1, a.shape[-1])

    w2d = {k: two_d(a) for k, a in w.items()}
    big = _BIG_COLS + _BIG_ROWS
    small = tuple(k for k in _WEIGHTS if k not in big)

    (meta4,), = _comm_now("gather_meta", [_Gather([w2d["meta_tokens"]])])
    shard = {k: w2d[k].astype(bf16) for k in big}
    shard.update({k: w2d[k] for k in ("gdn_conv_w", "gla_gate_w2")})
    lossp, grad_x, received, g = _step(x, loss_target, {k: w2d[k] for k in small}, _cat_cols(meta4), shard)
    loss = lax.psum(lossp, ("x", "y", "c"))

    sizes = [g[k].size for k in small]
    total = sum(sizes)
    rows = -(-total // 1024)
    rows += (-rows) % 8
    packed = jnp.concatenate([g[k].reshape(-1) for k in small] + [jnp.zeros((rows * 1024 - total,), f32)])
    (packed8,), = _comm_now("exchange_small", [_Exchange([], [packed.reshape(rows, 1024)])])
    red = {k: h.reshape(w2d[k].shape) for k, h in received.items()}
    psum_small = _sum_slots(packed8, "sum_small").reshape(-1)
    off = 0
    for k, n in zip(small, sizes):
        a = psum_small[off:off + n].reshape(g[k].shape)
        off += n
        if k in _SMALL_SHARDED:
            c = w2d[k].shape[1]
            a = lax.dynamic_slice_in_dim(a, chip * c, c, axis=1)
        red[k] = a

    grads, deltas, new_m, new_v = [], [], [], []
    for k in _WEIGHTS:
        d, m2, v2 = _adamw(red[k], w2d[k], two_d(m[k]), two_d(v[k]), "adamw_" + k)
        shape = w[k].shape
        grads.append(red[k].reshape(shape))
        deltas.append(d.reshape(shape))
        new_m.append(m2.reshape(shape))
        new_v.append(v2.reshape(shape))
    return (loss, grad_x, *grads, *deltas, *new_m, *new_v)
```

```python
import functools

import jax
import jax.numpy as jnp
from jax import lax
from jax.experimental import pallas as pl
from jax.experimental.pallas import tpu as pltpu

f32 = jnp.float32
bf16 = jnp.bfloat16
HIGH = lax.Precision.HIGH
MESH = pl.DeviceIdType.MESH

N_META = 16
CONV_K = 4
GDN_H, GDN_DK, GDN_DV, GDN_C = 8, 128, 128, 64
GLA_H, GLA_DK, GLA_DV, GLA_C = 4, 128, 256, 16
GATE_RANK = 16
GATE_NORMALIZER = 16.0
EPS = 1e-6
GDN_QK = GDN_H * GDN_DK
GDN_V = GDN_H * GDN_DV
GLA_QK = GLA_H * GLA_DK
GLA_V = GLA_H * GLA_DV
PAD = (-N_META) % GDN_C
OFF = PAD + N_META
ROWS = 64

R_QKV, R_Z, R_AB, R_G, R_GR, R_LR, R_END = 0, 3072, 4096, 4112, 6160, 7184, 7200
C_ZR, C_G, C_QKV, C_END = 0, 2048, 4096, 7168
ZR_W = GDN_V + GLA_V
G_W = 2 * GLA_QK + GLA_V
QKV_W = 2 * GDN_QK + GDN_V
Q0, K0, V0 = 0, GDN_QK, 2 * GDN_QK
SM_W = 128
SM_LR = 2 * GDN_H

ADAM_LR, ADAM_B1, ADAM_B2, ADAM_EPS, ADAM_WD, ADAM_STEP = 0.001, 0.9, 0.999, 1e-08, 0.01, 10

VMEM_LIMIT_V7X = 56 * 1024 * 1024
N_DEV = 8
N_CHIP = 4


def _params(*sem):
    return pltpu.CompilerParams(dimension_semantics=sem, vmem_limit_bytes=VMEM_LIMIT_V7X)


def _tile(n, cap, mult=16):
    best = None
    for d in range(mult, min(n, cap) + 1, mult):
        if n % d == 0:
            best = d
    assert best is not None, (n, cap, mult)
    return best


NN = ((1,), (0,))
NT = ((1,), (1,))
TN = ((0,), (0,))


def _dot(a, b, dims, prec=None):
    return lax.dot_general(a, b, (dims, ((), ())), precision=prec, preferred_element_type=f32)


def _mmb(a, b, dims):
    return _dot(a.astype(bf16), b.astype(bf16), dims)


def _sigmoid(x):
    return jax.nn.sigmoid(x)


def _silu(x):
    return x * _sigmoid(x)


def _dsilu(x):
    s = _sigmoid(x)
    return s * (1.0 + x * (1.0 - s))


def _log1p_exp_neg_abs(x):
    t = jnp.exp(-jnp.abs(x))
    u = 1.0 + t
    d = u - 1.0
    return jnp.where(d == 0.0, t, jnp.log(u) * (t / jnp.where(d == 0.0, 1.0, d)))


def _softplus(x):
    return jnp.maximum(x, 0.0) + _log1p_exp_neg_abs(x)


def _log_sigmoid(x):
    return jnp.minimum(x, 0.0) - _log1p_exp_neg_abs(x)


def _rms(x):
    r = lax.rsqrt(jnp.mean(x * x, axis=-1, keepdims=True) + EPS)
    return x * r, r


def _rms_bwd(dy, xh, r, w):
    t = dy * w
    return r * (t - xh * jnp.mean(t * xh, axis=-1, keepdims=True))


def _l2n(x):
    return x * lax.rsqrt(jnp.sum(x * x, axis=-1, keepdims=True) + EPS)


INV_LEAF = 8


def _same_block(C, b):
    sh = b.bit_length() - 1
    row = lax.broadcasted_iota(jnp.int32, (C, C), 0)
    col = lax.broadcasted_iota(jnp.int32, (C, C), 1)
    return lax.shift_right_logical(row, sh) == lax.shift_right_logical(col, sh)


def _tri_inv_impl(As):
    C = As[0].shape[0]
    R = range(len(As))
    row = lax.broadcasted_iota(jnp.int32, (C, C), 0)
    col = lax.broadcasted_iota(jnp.int32, (C, C), 1)
    eye = (row == col).astype(f32)
    b = INV_LEAF
    inner = _same_block(C, b)
    leaf = [jnp.where(inner, As[h], 0.0) for h in R]
    d = [eye - leaf[h] for h in R]
    pw = leaf
    n = 2
    while n < b:
        pw = [_dot(pw[h], pw[h], NN, HIGH) for h in R]
        d = [_dot(d[h], eye + pw[h], NN, HIGH) for h in R]
        n *= 2
    while b < C:
        outer = _same_block(C, 2 * b)
        level = jnp.logical_and(outer, jnp.logical_not(inner))
        ed = [_dot(jnp.where(level, As[h], 0.0), d[h], NN, HIGH) for h in R]
        d = [d[h] - _dot(d[h], ed[h], NN, HIGH) for h in R]
        inner = outer
        b *= 2
    return d


@jax.custom_vjp
def _tri_inv(As):
    return _tri_inv_impl(As)


def _tri_inv_fwd(As):
    d = _tri_inv_impl(As)
    return d, d


def _tri_inv_bwd(d, g):
    R = range(len(d))
    t = [_dot(d[h], g[h], TN, HIGH) for h in R]
    return ([-_dot(t[h], d[h], NT, HIGH) for h in R],)


_tri_inv.defvjp(_tri_inv_fwd, _tri_inv_bwd)


@jax.custom_vjp
def _tri_inv_known(As, Ps):
    del As
    return Ps


def _tri_inv_known_fwd(As, Ps):
    del As
    return Ps, Ps


def _tri_inv_known_bwd(d, g):
    return _tri_inv_bwd(d, g)[0], [jnp.zeros_like(x) for x in d]


_tri_inv_known.defvjp(_tri_inv_known_fwd, _tri_inv_known_bwd)


def _gdn_chunk(Ss, qrs, krs, vs, betas, gs, Ps=None):
    H = len(Ss)
    C, dk = qrs[0].shape
    R = range(len(qrs))
    row = lax.broadcasted_iota(jnp.int32, (C, C), 0)
    col = lax.broadcasted_iota(jnp.int32, (C, C), 1)
    causal = row >= col
    strict = row > col
    cf = causal.astype(f32)
    q = [_l2n(qrs[h]) * (dk ** -0.5) for h in R]
    k = [_l2n(krs[h]) for h in R]
    mc = [_rows_exact(cf, jnp.broadcast_to(gs[h], (C, C))) for h in R]
    gc = [mc[h][:, 0:1] for h in R]
    decay = [jnp.where(causal, jnp.exp(jnp.where(causal, mc[h] - mc[h].T, 0.0)), 0.0) for h in R]
    kb = [k[h] * betas[h] for h in R]
    a = [jnp.where(strict, _mmb(kb[h], k[h], NT) * decay[h], 0.0) for h in R]
    p = _tri_inv(a) if Ps is None else _tri_inv_known(a, Ps)
    egc = [jnp.exp(gc[h]) for h in R]
    u = [_mmb(p[h], vs[h] * betas[h], NN) for h in R]
    w = [_mmb(p[h], kb[h] * egc[h], NN) for h in R]
    qk = [jnp.where(causal, _mmb(q[h], k[h], NT) * decay[h], 0.0) for h in R]
    qe = [q[h] * egc[h] for h in R]
    gl = [gc[h][C - 1:C, :] for h in R]
    kd = [k[h] * jnp.exp(gl[h] - gc[h]) for h in R]
    egl = [jnp.exp(gl[h]) for h in R]
    S, o, entering = list(Ss), [], []
    for chunk in range(len(qrs) // H):
        idx = [chunk * H + h for h in range(H)]
        entering += S
        v_new = [u[i] - _mmb(w[i], S[h], NN) for h, i in enumerate(idx)]
        o += [_mmb(qe[i], S[h], NN) + _mmb(qk[i], v_new[h], NN) for h, i in enumerate(idx)]
        S = [S[h] * egl[i] + _mmb(kd[i], v_new[h], TN) for h, i in enumerate(idx)]
    return S, o, p, entering


def _rows_exact_impl(m01, x, dims):
    m = m01.astype(bf16)
    x1 = x.astype(bf16)
    r1 = x - x1.astype(f32)
    x2 = r1.astype(bf16)
    x3 = (r1 - x2.astype(f32)).astype(bf16)
    d = lambda y: _dot(m, y, dims)
    return d(x1) + (d(x2) + d(x3))


@jax.custom_vjp
def _rows_exact(m01, x):
    return _rows_exact_impl(m01, x, NN)


def _rows_exact_fwd(m01, x):
    return _rows_exact_impl(m01, x, NN), m01


def _rows_exact_bwd(m01, g):
    return jnp.zeros_like(m01), _rows_exact_impl(m01, g, TN)


_rows_exact.defvjp(_rows_exact_fwd, _rows_exact_bwd)


def _gla_blocks(Sts, qrs, ks, vs, las):
    H = len(Sts)
    n = len(qrs)
    C, dk = qrs[0].shape
    R = range(n)
    row = lax.broadcasted_iota(jnp.int32, (C, C), 0)
    col = lax.broadcasted_iota(jnp.int32, (C, C), 1)
    ri = lax.broadcasted_iota(jnp.int32, (C, dk), 0)
    q = [qrs[h] * (dk ** -0.5) for h in R]
    running = (row >= col).astype(f32)
    b = [_rows_exact(running, las[h]) for h in R]
    sc = [jnp.where(row == col, jnp.sum(q[h] * ks[h], axis=-1, keepdims=True), 0.0) for h in R]
    s = C // 2
    while s >= 1:
        sh = s.bit_length() - 1
        ref = lax.shift_left(lax.shift_right_logical(row, sh + 1), sh + 1) + (s - 1)
        pick = (col == ref).astype(f32)
        bref = [_rows_exact(pick, b[h]) for h in R]
        upper = (lax.shift_right_logical(ri, sh) & 1) == 1
        qt = [jnp.where(upper, q[h] * jnp.exp(jnp.where(upper, b[h] - bref[h], 0.0)), 0.0) for h in R]
        kt = [jnp.where(upper, 0.0, ks[h] * jnp.exp(jnp.where(upper, 0.0, bref[h] - b[h]))) for h in R]
        same = lax.shift_right_logical(row, sh + 1) == lax.shift_right_logical(col, sh + 1)
        sc = [sc[h] + jnp.where(same, _mmb(qt[h], kt[h], NT), 0.0) for h in R]
        s //= 2
    o = [_mmb(sc[h], vs[h], NN) for h in R]
    qe = [q[h] * jnp.exp(b[h]) for h in R]
    bl = [b[h][C - 1:C, :] for h in R]
    upd = [_mmb(vs[h], ks[h] * jnp.exp(bl[h] - b[h]), TN) for h in R]
    ebl = [jnp.exp(bl[h]) for h in R]
    St = list(Sts)
    for blk in range(n // H):
        for h in range(H):
            i = blk * H + h
            o[i] = o[i] + _mmb(qe[i], St[h], NT)
        St = [St[h] * ebl[blk * H + h] + upd[blk * H + h] for h in range(H)]
    return St, o


_ANY = pl.BlockSpec(memory_space=pl.ANY)


class _Gather:
    def __init__(self, arrays):
        self.arrays = list(arrays)
        self.n = len(self.arrays)
        self.out_shape = [jax.ShapeDtypeStruct((N_CHIP,) + a.shape, a.dtype) for a in self.arrays]
        self.sems = [pltpu.SemaphoreType.DMA((self.n, 3)), pltpu.SemaphoreType.DMA((self.n, 3)),
                     pltpu.SemaphoreType.DMA((self.n,))]

    def hooks(self, ins, outs, send, recv, lsem):
        def copies():
            x, y, c = lax.axis_index("x"), lax.axis_index("y"), lax.axis_index("c")
            me = 2 * x + y
            out = []
            for a in range(self.n):
                out.append((pltpu.make_async_copy(ins[a], outs[a].at[me], lsem.at[a]), None))
                for j, (px, py) in enumerate([(1 - x, y), (x, 1 - y), (1 - x, 1 - y)]):
                    mk = lambda dst, a=a, j=j, px=px, py=py: pltpu.make_async_remote_copy(
                        src_ref=ins[a], dst_ref=dst, send_sem=send.at[a, j], recv_sem=recv.at[a, j],
                        device_id=(px, py, c), device_id_type=MESH)
                    out.append((mk(outs[a].at[me]), mk(outs[a].at[2 * px + py])))
            return out

        return _start_wait(copies)


class _Exchange:
    def __init__(self, slotted, shared=(), by_chip=()):
        self.arrays = list(slotted) + list(by_chip) + list(shared)
        self.ns, self.nc = len(slotted), len(by_chip)
        self.n = len(self.arrays)
        self.out_shape = [jax.ShapeDtypeStruct(a.shape, a.dtype) for a in slotted]
        self.out_shape += [jax.ShapeDtypeStruct((N_DEV,) + a.shape[1:], a.dtype) for a in by_chip]
        self.out_shape += [jax.ShapeDtypeStruct((N_DEV,) + b.shape, b.dtype) for b in shared]
        self.sems = [pltpu.SemaphoreType.DMA((self.n, N_DEV - 1)), pltpu.SemaphoreType.DMA((self.n, N_DEV - 1)),
                     pltpu.SemaphoreType.DMA((self.n,))]

    def hooks(self, ins, outs, send, recv, lsem):
        def copies():
            x, y, c = lax.axis_index("x"), lax.axis_index("y"), lax.axis_index("c")
            me = 4 * x + 2 * y + c

            def src(a, dev):
                tx, ty, tc = dev
                if a < self.ns:
                    return ins[a].at[4 * tx + 2 * ty + tc]
                return ins[a].at[2 * tx + ty] if a < self.ns + self.nc else ins[a]

            out = []
            for a in range(self.n):
                out.append((pltpu.make_async_copy(src(a, (x, y, c)), outs[a].at[me], lsem.at[a]), None))
                for o in range(1, N_DEV):
                    dev = (1 - x if o & 4 else x, 1 - y if o & 2 else y, 1 - c if o & 1 else c)
                    t = 4 * dev[0] + 2 * dev[1] + dev[2]
                    mk = lambda dst, a=a, o=o, dev=dev: pltpu.make_async_remote_copy(
                        src_ref=src(a, dev), dst_ref=dst, send_sem=send.at[a, o - 1], recv_sem=recv.at[a, o - 1],
                        device_id=dev, device_id_type=MESH)
                    out.append((mk(outs[a].at[me]), mk(outs[a].at[t])))
            return out

        return _start_wait(copies)


class _Sides:
    def __init__(self, *members):
        self.members = members
        self.arrays = [a for s in members for a in s.arrays]
        self.n = len(self.arrays)
        self.out_shape = [sh for s in members for sh in s.out_shape]
        self.sems = [sm for s in members for sm in s.sems]

    def hooks(self, ins, outs, *sems):
        hooks, o = [], 0
        for i, s in enumerate(self.members):
            hooks.append(s.hooks(ins[o:o + s.n], outs[o:o + s.n], *sems[3 * i:3 * i + 3]))
            o += s.n

        def start():
            for st, _ in hooks:
                st()

        def wait():
            for _, wt in hooks:
                wt()

        return start, wait


def _start_wait(copies):
    def start():
        for s, _ in copies():
            s.start()

    def wait():
        for s, w in copies():
            (s if w is None else w).wait()

    return start, wait


def _call(body, *, name, grid, in_specs, out_specs, out_shape, args, sem, scratch_shapes=(), aliases=None, side=None):
    in_specs, out_specs, out_shape, args = list(in_specs), list(out_specs), list(out_shape), list(args)
    scratch_shapes = list(scratch_shapes)
    aliases = aliases or {}
    if side is None:
        return pl.pallas_call(
            body, name=name, grid=grid, in_specs=in_specs, out_specs=out_specs, out_shape=out_shape,
            scratch_shapes=scratch_shapes, input_output_aliases=aliases, compiler_params=_params(*sem))(*args)
    n_in, n_out, n_scr, ns = len(in_specs), len(out_specs), len(scratch_shapes), side.n

    def full_body(*refs):
        ins, refs = refs[:n_in], refs[n_in:]
        s_in, refs = refs[:ns], refs[ns:]
        outs, refs = refs[:n_out], refs[n_out:]
        s_out, refs = refs[:ns], refs[ns:]
        scr, sems = refs[:n_scr], refs[n_scr:]
        start, wait = side.hooks(s_in, s_out, *sems)
        ids = [pl.program_id(d) for d in range(len(grid))]
        first = functools.reduce(jnp.logical_and, [i == 0 for i in ids])
        last = functools.reduce(jnp.logical_and, [i == g - 1 for i, g in zip(ids, grid)])
        pl.when(first)(start)
        body(*ins, *outs, *scr)
        pl.when(last)(wait)

    return pl.pallas_call(
        full_body, name=name, grid=grid, in_specs=in_specs + [_ANY] * ns, out_specs=out_specs + [_ANY] * ns,
        out_shape=out_shape + side.out_shape, scratch_shapes=scratch_shapes + side.sems,
        input_output_aliases=aliases, compiler_params=_params(*(["arbitrary"] * len(grid))))(*args, *side.arrays)


WHOLE_K = dict(tm_cap=688, tn_cap=512, tk_cap=1 << 20)
WHOLE_K_T = dict(tm_cap=512, tn_cap=512, tk_cap=1 << 20)

def _mm(a, b, mode, name, *, tm_cap=1408, tn_cap=1024, tk_cap=2048, out_dtype=f32, acc_in=None, side=None,
        col_slabs=False):
    if mode == "nn":
        (M, K), (K2, N) = a.shape, b.shape
    elif mode == "nt":
        (M, K), (N, K2) = a.shape, b.shape
    else:
        (K, M), (K2, N) = a.shape, b.shape
    assert K == K2, (name, a.shape, b.shape)
    tm = _tile(M // 2 if col_slabs else M, tm_cap)
    tn = _tile(N // N_CHIP if col_slabs else N, tn_cap, 128)
    tk = _tile(K, tk_cap, 128 if K % 128 == 0 else 16)
    nk = K // tk
    dims = {"nn": NN, "nt": NT, "tn": TN}[mode]
    use_scratch = nk > 1 and out_dtype != f32

    def body(*refs):
        if acc_in is not None:
            a_ref, b_ref, c_ref, o_ref, *scr = refs
        else:
            a_ref, b_ref, o_ref, *scr = refs
            c_ref = None
        p = _mmb(a_ref[...], b_ref[...], dims)
        if nk == 1:
            if c_ref is not None:
                p = p + c_ref[...]
            o_ref[...] = p.astype(out_dtype)
            return
        k = pl.program_id(2)
        acc = scr[0] if use_scratch else o_ref

        @pl.when(k == 0)
        def _():
            acc[...] = p if c_ref is None else p + c_ref[...]

        @pl.when(k > 0)
        def _():
            acc[...] += p

        if use_scratch:
            @pl.when(k == nk - 1)
            def _():
                o_ref[...] = acc[...].astype(out_dtype)

    if mode == "tn":
        a_spec = pl.BlockSpec((tk, tm), lambda i, j, k: (k, i))
    else:
        a_spec = pl.BlockSpec((tm, tk), lambda i, j, k: (i, k))
    if mode == "nt":
        b_spec = pl.BlockSpec((tn, tk), lambda i, j, k: (j, k))
    else:
        b_spec = pl.BlockSpec((tk, tn), lambda i, j, k: (k, j))
    if col_slabs:
        assert acc_in is None
        ni, nj = M // 2 // tm, N // N_CHIP // tn
        o_spec = pl.BlockSpec((None, tm, tn), lambda i, j, k: (2 * (j // nj) + i // ni, i % ni, j % nj))
        o_shape = jax.ShapeDtypeStruct((N_DEV, M // 2, N // N_CHIP), out_dtype)
    else:
        o_spec = pl.BlockSpec((tm, tn), lambda i, j, k: (i, j))
        o_shape = jax.ShapeDtypeStruct((M, N), out_dtype)
    in_specs = [a_spec, b_spec]
    args = [a, b]
    if acc_in is not None:
        in_specs.append(o_spec)
        args.append(acc_in)
    out = _call(body, name=name, grid=(M // tm, N // tn, nk), in_specs=in_specs, out_specs=[o_spec],
                out_shape=[o_shape], args=args,
                scratch_shapes=[pltpu.VMEM((tm, tn), f32)] if use_scratch else [],
                sem=("parallel", "parallel", "arbitrary"), side=side)
    return out[0] if side is None else (out[0], out[1:])


def _embed_norm(x3, m64, w, side=None):
    _, S, D = x3.shape
    Lp = OFF + S

    def body(x_ref, m_ref, w_ref, h_ref, n_ref):
        i = pl.program_id(0)
        h = jnp.where(i == 0, m_ref[...], x_ref[...])
        h_ref[...] = h
        xh, _ = _rms(h)
        n_ref[...] = (xh * w_ref[...]).astype(bf16)

    row = pl.BlockSpec((ROWS, D), lambda i: (i, 0))
    out = _call(
        body, name="embed_norm", grid=(Lp // ROWS,),
        in_specs=[pl.BlockSpec((None, ROWS, D), lambda i: (0, jnp.maximum(i - 1, 0), 0)),
                  pl.BlockSpec((ROWS, D), lambda i: (0, 0)),
                  pl.BlockSpec((1, D), lambda i: (0, 0))],
        out_specs=[row, row],
        out_shape=[jax.ShapeDtypeStruct((Lp, D), f32), jax.ShapeDtypeStruct((Lp, D), bf16)],
        args=[x3, m64, w], sem=("parallel",), side=side)
    return out[0], out[1], out[2:]


def _add_norm(h, d, w):
    Lp, D = h.shape
    tr = _tile(Lp, 256)

    def body(h_ref, d_ref, w_ref, o_ref, n_ref):
        h1 = h_ref[...] + d_ref[...]
        o_ref[...] = h1
        xh, _ = _rms(h1)
        n_ref[...] = (xh * w_ref[...]).astype(bf16)

    row = pl.BlockSpec((tr, D), lambda i: (i, 0))
    return pl.pallas_call(
        body, name="add_norm", grid=(Lp // tr,),
        in_specs=[row, row, pl.BlockSpec((1, D), lambda i: (0, 0))], out_specs=[row, row],
        out_shape=[jax.ShapeDtypeStruct((Lp, D), f32), jax.ShapeDtypeStruct((Lp, D), bf16)],
        compiler_params=_params("parallel"),
    )(h, d, w)


def _norm_bwd(dn, h, dh, w):
    Lp, D = h.shape
    tr = _tile(Lp, 256)

    def body(dn_ref, h_ref, dh_ref, w_ref, o_ref, ob_ref, gw_ref):
        i = pl.program_id(0)
        xh, r = _rms(h_ref[...])
        dn_ = dn_ref[...]
        o = dh_ref[...] + _rms_bwd(dn_, xh, r, w_ref[...])
        o_ref[...] = o
        ob_ref[...] = o.astype(bf16)
        gw = jnp.sum(dn_ * xh, axis=0, keepdims=True)

        @pl.when(i == 0)
        def _():
            gw_ref[...] = gw

        @pl.when(i > 0)
        def _():
            gw_ref[...] += gw

    row = pl.BlockSpec((tr, D), lambda i: (i, 0))
    vec = pl.BlockSpec((1, D), lambda i: (0, 0))
    return pl.pallas_call(
        body, name="norm_bwd", grid=(Lp // tr,), in_specs=[row, row, row, vec], out_specs=[row, row, vec],
        out_shape=[jax.ShapeDtypeStruct((Lp, D), f32), jax.ShapeDtypeStruct((Lp, D), bf16),
                   jax.ShapeDtypeStruct((1, D), f32)],
        compiler_params=_params("arbitrary"),
    )(dn, h, dh, w)


def _embed_norm_bwd(dn, h, dh, w, S, side=None):
    Lp, D = h.shape
    tr = _tile(S, 256, OFF)

    def body(dn_ref, h_ref, dh_ref, dn0_ref, h0_ref, dh0_ref, w_ref, gx_ref, gm_ref, gw_ref):
        i = pl.program_id(0)

        def rows(dn_, h_, dh_):
            xh, r = _rms(h_)
            return dh_ + _rms_bwd(dn_, xh, r, w_ref[...]), jnp.sum(dn_ * xh, axis=0, keepdims=True)

        d, gw = rows(dn_ref[...], h_ref[...], dh_ref[...])
        gx_ref[...] = d

        @pl.when(i == 0)
        def _():
            d0, gw0 = rows(dn0_ref[...], h0_ref[...], dh0_ref[...])
            gm_ref[...] = d0[PAD:OFF, :]
            gw_ref[...] = gw0 + gw

        @pl.when(i > 0)
        def _():
            gw_ref[...] += gw

    win = pl.BlockSpec((pl.Element(tr), pl.Element(D)), lambda i: (pl.multiple_of(OFF + i * tr, OFF), 0))
    head = pl.BlockSpec((OFF, D), lambda i: (0, 0))
    vec = pl.BlockSpec((1, D), lambda i: (0, 0))
    out = _call(
        body, name="embed_norm_bwd", grid=(S // tr,), in_specs=[win, win, win, head, head, head, vec],
        out_specs=[pl.BlockSpec((None, tr, D), lambda i: (0, i, 0)),
                   pl.BlockSpec((N_META, D), lambda i: (0, 0)), vec],
        out_shape=[jax.ShapeDtypeStruct((1, S, D), f32), jax.ShapeDtypeStruct((N_META, D), f32),
                   jax.ShapeDtypeStruct((1, D), f32)],
        args=[dn, h, dh, dn, h, dh, w], sem=("arbitrary",), side=side)
    return out[0], out[1], out[2], out[3:]


def _final(h1, ffn, tgt3, w):
    Lp, D = h1.shape
    _, S, _ = tgt3.shape
    tr = _tile(Lp, min(256, S), OFF)

    def body(h_ref, f_ref, t_ref, w_ref, d_ref, db_ref, l_ref, gw_ref):
        i = pl.program_id(0)
        h2 = h_ref[...] + f_ref[...]
        xh, r = _rms(h2)
        w_ = w_ref[...]
        t = t_ref[...]
        t = jnp.where(i == 0, pltpu.roll(t, OFF, 0), t)
        valid = (lax.broadcasted_iota(jnp.int32, (tr, 1), 0) + i * tr >= OFF).astype(f32)
        e = xh * w_ - t
        loss = 0.5 * jnp.sum(jnp.mean(e * e, axis=-1, keepdims=True) * valid, axis=0, keepdims=True)
        dy = e * (valid / D)
        d = _rms_bwd(dy, xh, r, w_)
        d_ref[...] = d
        db_ref[...] = d.astype(bf16)
        gw = jnp.sum(dy * xh, axis=0, keepdims=True)

        @pl.when(i == 0)
        def _():
            l_ref[...] = jnp.zeros_like(l_ref)
            gw_ref[...] = jnp.zeros_like(gw_ref)

        l_ref[...] += jnp.broadcast_to(loss, l_ref.shape)
        gw_ref[...] += gw

    row = pl.BlockSpec((tr, D), lambda i: (i, 0))
    vec = pl.BlockSpec((1, D), lambda i: (0, 0))
    tgt = pl.BlockSpec((pl.Element(tr), pl.Element(D)),
                       lambda i: (pl.multiple_of(jnp.maximum(i * tr - OFF, 0), OFF), 0))
    return pl.pallas_call(
        body, name="final_loss", grid=(Lp // tr,), in_specs=[row, row, tgt, vec],
        out_specs=[row, row, pl.BlockSpec((8, 128), lambda i: (0, 0)), vec],
        out_shape=[jax.ShapeDtypeStruct((Lp, D), f32), jax.ShapeDtypeStruct((Lp, D), bf16),
                   jax.ShapeDtypeStruct((8, 128), f32), jax.ShapeDtypeStruct((1, D), f32)],
        compiler_params=_params("arbitrary"),
    )(h1, ffn, tgt3.reshape(S, D), w)


def _ffn_in(n, w_gate, w_up, side=None):
    M, K = n.shape
    F = w_gate.shape[1]
    tm = _tile(M, 1408)
    tn = _tile(F, 512, 128)

    def body(a_ref, bg_ref, bu_ref, act_ref, pg_ref, pu_ref):
        a = a_ref[...]
        g = _mmb(a, bg_ref[...], NN)
        u = _mmb(a, bu_ref[...], NN)
        s = _sigmoid(g)
        gs = g * s
        act_ref[...] = (gs * u).astype(bf16)
        pg_ref[...] = (u * (s + gs * (1.0 - s))).astype(bf16)
        pu_ref[...] = gs.astype(bf16)

    wsp = pl.BlockSpec((K, tn), lambda i, j: (0, j))
    osp = pl.BlockSpec((tm, tn), lambda i, j: (i, j))
    out = _call(body, name="ffn_in", grid=(M // tm, F // tn),
                in_specs=[pl.BlockSpec((tm, K), lambda i, j: (i, 0)), wsp, wsp], out_specs=[osp] * 3,
                out_shape=[jax.ShapeDtypeStruct((M, F), bf16)] * 3, args=[n, w_gate, w_up],
                sem=("parallel", "parallel"), side=side)
    return out[0], out[1], out[2], out[3:]


def _ffn_dact(d, w_down, pg, pu):
    M, K = d.shape
    F = w_down.shape[0]
    tm = _tile(M, 1408)
    tn = _tile(F, 512, 128)

    def body(d_ref, w_ref, pg_ref, pu_ref, dg_ref, du_ref):
        da = _mmb(d_ref[...], w_ref[...], NT)
        dg_ref[...] = (da * pg_ref[...].astype(f32)).astype(bf16)
        du_ref[...] = (da * pu_ref[...].astype(f32)).astype(bf16)

    osp = pl.BlockSpec((tm, tn), lambda i, j: (i, j))
    return pl.pallas_call(
        body, name="ffn_dact", grid=(M // tm, F // tn),
        in_specs=[pl.BlockSpec((tm, K), lambda i, j: (i, 0)), pl.BlockSpec((tn, K), lambda i, j: (j, 0)), osp, osp],
        out_specs=[osp, osp], out_shape=[jax.ShapeDtypeStruct((M, F), bf16)] * 2,
        compiler_params=_params("parallel", "parallel"),
    )(d, w_down, pg, pu)


def _ffn_dn(dg, du, w_gate, w_up):
    M, F = dg.shape
    D = w_gate.shape[0]
    tm = _tile(M, 688)
    tn = _tile(D, 256, 128)

    def body(dg_ref, du_ref, wg_ref, wu_ref, o_ref):
        o_ref[...] = _mmb(dg_ref[...], wg_ref[...], NT) + _mmb(du_ref[...], wu_ref[...], NT)

    asp = pl.BlockSpec((tm, F), lambda i, j: (i, 0))
    wsp = pl.BlockSpec((tn, F), lambda i, j: (j, 0))
    return pl.pallas_call(
        body, name="d_n2", grid=(M // tm, D // tn), in_specs=[asp, asp, wsp, wsp],
        out_specs=pl.BlockSpec((tm, tn), lambda i, j: (i, j)), out_shape=jax.ShapeDtypeStruct((M, D), f32),
        compiler_params=_params("parallel", "parallel"),
    )(dg, du, w_gate, w_up)


def _gates(psm, w2p, gate_b, alog, dtb):
    Lp = psm.shape[0]
    tr = _tile(Lp, 256)

    def body(p_ref, w_ref, b_ref, a_ref, t_ref, gb_ref, la_ref):
        i = pl.program_id(0)
        psm_ = p_ref[...]
        lane = lax.broadcasted_iota(jnp.int32, psm_.shape, 1)
        rowi = lax.broadcasted_iota(jnp.int32, (tr, 1), 0) + i * tr
        g = -jnp.exp(a_ref[...]) * _softplus(psm_ + t_ref[...])
        beta = _sigmoid(psm_)
        gb = jnp.where(lane < GDN_H, g, jnp.where(lane < 2 * GDN_H, beta, 0.0))
        gb_ref[...] = gb * (rowi >= PAD).astype(f32)
        logit = _mmb(psm_, w_ref[...], NN) + b_ref[...]
        la_ref[...] = _log_sigmoid(logit) * (1.0 / GATE_NORMALIZER)

    row = pl.BlockSpec((tr, SM_W), lambda i: (i, 0))
    return pl.pallas_call(
        body, name="gates", grid=(Lp // tr,),
        in_specs=[row, pl.BlockSpec((SM_W, GLA_QK), lambda i: (0, 0)), pl.BlockSpec((1, GLA_QK), lambda i: (0, 0)),
                  pl.BlockSpec((1, SM_W), lambda i: (0, 0)), pl.BlockSpec((1, SM_W), lambda i: (0, 0))],
        out_specs=[row, pl.BlockSpec((tr, GLA_QK), lambda i: (i, 0))],
        out_shape=[jax.ShapeDtypeStruct((Lp, SM_W), f32), jax.ShapeDtypeStruct((Lp, GLA_QK), f32)],
        compiler_params=_params("parallel"),
    )(psm, w2p, gate_b, alog, dtb)


def _gates_bwd(psm, w2p, gate_b, alog, dtb, dgb, dla):
    Lp = psm.shape[0]
    tr = _tile(Lp, 256)

    def body(p_ref, w_ref, b_ref, a_ref, t_ref, dgb_ref, dla_ref, dp_ref, gw_ref, gb_ref, ga_ref, gt_ref):
        i = pl.program_id(0)
        psm_ = p_ref[...]
        lane = lax.broadcasted_iota(jnp.int32, psm_.shape, 1)
        rowi = lax.broadcasted_iota(jnp.int32, (tr, 1), 0) + i * tr
        d = dgb_ref[...] * (rowi >= PAD).astype(f32)
        ea = jnp.exp(a_ref[...])
        z = psm_ + t_ref[...]
        is_g = lane < GDN_H
        dz = jnp.where(is_g, -ea * _sigmoid(z) * d, 0.0)
        dalog = jnp.where(is_g, -ea * _softplus(z) * d, 0.0)
        beta = _sigmoid(psm_)
        dbeta = jnp.where(jnp.logical_and(lane >= GDN_H, lane < 2 * GDN_H), beta * (1.0 - beta) * d, 0.0)
        logit = _mmb(psm_, w_ref[...], NN) + b_ref[...]
        dlogit = dla_ref[...] * (_sigmoid(-logit) * (1.0 / GATE_NORMALIZER))
        dlr = _mmb(dlogit, w_ref[...], NT)
        dp_ref[...] = (dz + dbeta + dlr).astype(bf16)
        gw = _mmb(psm_, dlogit, TN)
        gb = jnp.sum(dlogit, axis=0, keepdims=True)
        ga = jnp.sum(dalog, axis=0, keepdims=True)
        gt = jnp.sum(dz, axis=0, keepdims=True)

        @pl.when(i == 0)
        def _():
            gw_ref[...] = gw
            gb_ref[...] = gb
            ga_ref[...] = ga
            gt_ref[...] = gt

        @pl.when(i > 0)
        def _():
            gw_ref[...] += gw
            gb_ref[...] += gb
            ga_ref[...] += ga
            gt_ref[...] += gt

    row = pl.BlockSpec((tr, SM_W), lambda i: (i, 0))
    wsp = pl.BlockSpec((SM_W, GLA_QK), lambda i: (0, 0))
    bsp = pl.BlockSpec((1, GLA_QK), lambda i: (0, 0))
    vsp = pl.BlockSpec((1, SM_W), lambda i: (0, 0))
    return pl.pallas_call(
        body, name="gates_bwd", grid=(Lp // tr,),
        in_specs=[row, wsp, bsp, vsp, vsp, row, pl.BlockSpec((tr, GLA_QK), lambda i: (i, 0))],
        out_specs=[row, wsp, bsp, vsp, vsp],
        out_shape=[jax.ShapeDtypeStruct((Lp, SM_W), bf16), jax.ShapeDtypeStruct((SM_W, GLA_QK), f32),
                   jax.ShapeDtypeStruct((1, GLA_QK), f32), jax.ShapeDtypeStruct((1, SM_W), f32),
                   jax.ShapeDtypeStruct((1, SM_W), f32)],
        compiler_params=_params("arbitrary"),
    )(psm, w2p, gate_b, alog, dtb, dgb, dla)


def _conv_pre(x_ext, w, n):
    rows = x_ext.shape[0]
    y = x_ext * w[CONV_K - 1:CONV_K, :]
    for s in range(1, CONV_K):
        y = y + pltpu.roll(x_ext, s, 0) * w[CONV_K - 1 - s:CONV_K - s, :]
    return y[rows - n:, :]


def _conv(proj, cw, side=None):
    Lp = proj.shape[0]
    W = cw.shape[1]
    tr = _tile(Lp, 256, 64)
    tc = _tile(W, 1024, 128)
    c0 = C_QKV // tc

    def body(h_ref, x_ref, w_ref, o_ref):
        i = pl.program_id(1)
        halo = jnp.where(i == 0, 0.0, h_ref[...])
        x_ext = jnp.concatenate([halo, x_ref[...]], axis=0)
        o_ref[...] = _silu(_conv_pre(x_ext, w_ref[...], tr))

    out = _call(
        body, name="conv", grid=(W // tc, Lp // tr),
        in_specs=[pl.BlockSpec((8, tc), lambda j, i: (jnp.maximum(i * (tr // 8) - 1, 0), j + c0)),
                  pl.BlockSpec((tr, tc), lambda j, i: (i, j + c0)),
                  pl.BlockSpec((CONV_K, tc), lambda j, i: (0, j))],
        out_specs=[pl.BlockSpec((tr, tc), lambda j, i: (i, j))],
        out_shape=[jax.ShapeDtypeStruct((Lp, W), f32)], args=[proj, proj, cw],
        sem=("parallel", "parallel"), side=side)
    return out[0] if side is None else (out[0], out[1:])


def _conv_bwd(proj, cw, dy, dproj, side=None):
    Lp = proj.shape[0]
    W = cw.shape[1]
    tr = _tile(Lp, 256, 64)
    tc = _tile(W, 1024, 128)
    c0 = C_QKV // tc
    nr = Lp // tr
    last8 = Lp // 8 - 1

    def body(xp_ref, x_ref, xn_ref, w_ref, d_ref, dn_ref, dproj_ref, o_ref, gw_ref):
        del dproj_ref
        i = pl.program_id(1)
        w = w_ref[...]
        xp = jnp.where(i == 0, 0.0, xp_ref[...])
        x_ext = jnp.concatenate([xp, x_ref[...], xn_ref[...]], axis=0)
        n = tr + 8
        pre = _conv_pre(x_ext, w, n)
        dn = jnp.where(i == nr - 1, 0.0, dn_ref[...])
        dpre = jnp.concatenate([d_ref[...], dn], axis=0) * _dsilu(pre)
        dx = dpre * w[CONV_K - 1:CONV_K, :]
        for s in range(1, CONV_K):
            dx = dx + pltpu.roll(dpre, n - s, 0) * w[CONV_K - 1 - s:CONV_K - s, :]
        o_ref[...] = dx[:tr, :].astype(bf16)
        dp = dpre[:tr, :]
        rows = []
        for k in range(CONV_K):
            xs = x_ext if k == CONV_K - 1 else pltpu.roll(x_ext, CONV_K - 1 - k, 0)
            rows.append(jnp.sum(dp * xs[8:8 + tr, :], axis=0, keepdims=True))
        gw = jnp.concatenate(rows, axis=0)

        @pl.when(i == 0)
        def _():
            gw_ref[...] = gw

        @pl.when(i > 0)
        def _():
            gw_ref[...] += gw

    cur = pl.BlockSpec((tr, tc), lambda j, i: (i, j))
    nxt = pl.BlockSpec((8, tc), lambda j, i: (jnp.minimum((i + 1) * (tr // 8), last8), j))
    pcur = pl.BlockSpec((tr, tc), lambda j, i: (i, j + c0))
    pprev = pl.BlockSpec((8, tc), lambda j, i: (jnp.maximum(i * (tr // 8) - 1, 0), j + c0))
    pnext = pl.BlockSpec((8, tc), lambda j, i: (jnp.minimum((i + 1) * (tr // 8), last8), j + c0))
    wsp = pl.BlockSpec((CONV_K, tc), lambda j, i: (0, j))
    out = _call(
        body, name="conv_bwd", grid=(W // tc, nr),
        in_specs=[pprev, pcur, pnext, wsp, cur, nxt, _ANY], out_specs=[pcur, wsp],
        out_shape=[jax.ShapeDtypeStruct(dproj.shape, dproj.dtype), jax.ShapeDtypeStruct((CONV_K, W), f32)],
        aliases={6: 0}, args=[proj, proj, proj, cw, dy, dy, dproj], sem=("parallel", "arbitrary"), side=side)
    return out[0], out[1], out[2:]


GDN_FWD_GROUP = 3


def _gdn_group(Lp, most):
    n = Lp // GDN_C
    return next(g for g in range(most, 0, -1) if n % g == 0)


def _gdn_heads(x_ref, gb_ref, group):
    qs, ks, vs, bs, gs = [], [], [], [], []
    for chunk in range(group):
        r = slice(chunk * GDN_C, (chunk + 1) * GDN_C)
        gbv = gb_ref[r, :]
        for h in range(GDN_H):
            qs.append(x_ref[r, Q0 + h * GDN_DK:Q0 + (h + 1) * GDN_DK])
            ks.append(x_ref[r, K0 + h * GDN_DK:K0 + (h + 1) * GDN_DK])
            vs.append(x_ref[r, V0 + h * GDN_DV:V0 + (h + 1) * GDN_DV])
            bs.append(gbv[:, GDN_H + h:GDN_H + h + 1])
            gs.append(gbv[:, h:h + 1])
    return qs, ks, vs, bs, gs


def _gdn_fwd(qkvc, gb, side=None):
    Lp = qkvc.shape[0]
    group = _gdn_group(Lp, GDN_FWD_GROUP)
    rows = group * GDN_C
    steps = Lp // rows
    R = range(GDN_H)

    def body(x_ref, gb_ref, o_ref, sall_ref, pall_ref, s_scr):
        @pl.when(pl.program_id(0) == 0)
        def _():
            s_scr[...] = jnp.zeros_like(s_scr)

        S2, o, p, entering = _gdn_chunk([s_scr[h] for h in R], *_gdn_heads(x_ref, gb_ref, group))
        for h in R:
            s_scr[h] = S2[h]
        for chunk in range(group):
            for h in R:
                i = chunk * GDN_H + h
                o_ref[chunk * GDN_C:(chunk + 1) * GDN_C, h * GDN_DV:(h + 1) * GDN_DV] = o[i]
                pall_ref[chunk, h] = p[i]
                sall_ref[chunk, h] = entering[i]

    out = _call(
        body, name="gdn_fwd", grid=(steps,),
        in_specs=[pl.BlockSpec((rows, QKV_W), lambda n: (n, 0)), pl.BlockSpec((rows, SM_W), lambda n: (n, 0))],
        out_specs=[pl.BlockSpec((rows, GDN_V), lambda n: (n, 0)),
                   pl.BlockSpec((group, GDN_H, GDN_DK, GDN_DV), lambda n: (n, 0, 0, 0)),
                   pl.BlockSpec((group, GDN_H, GDN_C, GDN_C), lambda n: (n, 0, 0, 0))],
        out_shape=[jax.ShapeDtypeStruct((Lp, GDN_V), f32),
                   jax.ShapeDtypeStruct((Lp // GDN_C, GDN_H, GDN_DK, GDN_DV), f32),
                   jax.ShapeDtypeStruct((Lp // GDN_C, GDN_H, GDN_C, GDN_C), f32)],
        scratch_shapes=[pltpu.VMEM((GDN_H, GDN_DK, GDN_DV), f32)], args=[qkvc, gb], sem=("arbitrary",), side=side)
    return out[0], out[1], out[2], out[3:]


def _gdn_bwd(qkvc, gb, sall, pall, do, side=None):
    Lp = qkvc.shape[0]
    group = 1
    rows = group * GDN_C
    steps = Lp // rows
    R = range(GDN_H)

    def body(x_ref, gb_ref, sall_ref, pall_ref, do_ref, dx_ref, dgb_ref, ds_scr):
        @pl.when(pl.program_id(0) == 0)
        def _():
            ds_scr[...] = jnp.zeros_like(ds_scr)

        lane = lax.broadcasted_iota(jnp.int32, (GDN_C, SM_W), 1)
        ps = [pall_ref[chunk, h] for chunk in range(group) for h in R]
        _, vjp = jax.vjp(lambda *a: _gdn_chunk(*a, Ps=ps)[:2],
                         [sall_ref[0, h] for h in R], *_gdn_heads(x_ref, gb_ref, group))
        do = [do_ref[chunk * GDN_C:(chunk + 1) * GDN_C, h * GDN_DV:(h + 1) * GDN_DV]
              for chunk in range(group) for h in R]
        dS, dq, dk, dv, dbeta, dg = vjp(([ds_scr[h] for h in R], do))
        for h in R:
            ds_scr[h] = dS[h]
        for chunk in range(group):
            r = slice(chunk * GDN_C, (chunk + 1) * GDN_C)
            acc = jnp.zeros((GDN_C, SM_W), f32)
            for h in R:
                i = chunk * GDN_H + h
                dx_ref[r, Q0 + h * GDN_DK:Q0 + (h + 1) * GDN_DK] = dq[i]
                dx_ref[r, K0 + h * GDN_DK:K0 + (h + 1) * GDN_DK] = dk[i]
                dx_ref[r, V0 + h * GDN_DV:V0 + (h + 1) * GDN_DV] = dv[i]
                acc = acc + jnp.where(lane == h, dg[i], 0.0) + jnp.where(lane == GDN_H + h, dbeta[i], 0.0)
            dgb_ref[r, :] = acc

    rev = lambda n: (steps - 1 - n, 0)
    out = _call(
        body, name="gdn_bwd", grid=(steps,),
        in_specs=[pl.BlockSpec((rows, QKV_W), rev), pl.BlockSpec((rows, SM_W), rev),
                  pl.BlockSpec((1, GDN_H, GDN_DK, GDN_DV), lambda n: (steps - 1 - n, 0, 0, 0)),
                  pl.BlockSpec((group, GDN_H, GDN_C, GDN_C), lambda n: (steps - 1 - n, 0, 0, 0)),
                  pl.BlockSpec((rows, GDN_V), rev)],
        out_specs=[pl.BlockSpec((rows, QKV_W), rev), pl.BlockSpec((rows, SM_W), rev)],
        out_shape=[jax.ShapeDtypeStruct((Lp, QKV_W), f32), jax.ShapeDtypeStruct((Lp, SM_W), f32)],
        scratch_shapes=[pltpu.VMEM((GDN_H, GDN_DK, GDN_DV), f32)], args=[qkvc, gb, sall, pall, do],
        sem=("arbitrary",), side=side)
    return out[0], out[1], out[2:]


GLA_BLOCK = 64


def _gla_group(Lp):
    nb = Lp // GLA_BLOCK
    return next(g for g in (3, 2, 1) if nb % g == 0)


def _gla_slices(h):
    sq = slice(h * GLA_DK, (h + 1) * GLA_DK)
    sk = slice(GLA_QK + h * GLA_DK, GLA_QK + (h + 1) * GLA_DK)
    sv = slice(2 * GLA_QK + h * GLA_DV, 2 * GLA_QK + (h + 1) * GLA_DV)
    return sq, sk, sv


def _gla_heads(x_ref, la_ref, group):
    qs, ks, vs, ls = [], [], [], []
    for blk in range(group):
        r = slice(blk * GLA_BLOCK, (blk + 1) * GLA_BLOCK)
        for h in range(GLA_H):
            sq, sk, sv = _gla_slices(h)
            qs.append(x_ref[r, sq])
            ks.append(x_ref[r, sk])
            vs.append(x_ref[r, sv])
            ls.append(la_ref[r, sq])
    return qs, ks, vs, ls


def _gla_fwd(proj, la):
    Lp = proj.shape[0]
    group = _gla_group(Lp)
    rows = group * GLA_BLOCK
    steps = Lp // rows
    R = range(GLA_H)

    def body(x_ref, la_ref, o_ref, sall_ref, s_scr):
        @pl.when(pl.program_id(0) == 0)
        def _():
            s_scr[...] = jnp.zeros_like(s_scr)

        Sts = [s_scr[h] for h in R]
        for h in R:
            sall_ref[0, h] = Sts[h]
        St2, o = _gla_blocks(Sts, *_gla_heads(x_ref, la_ref, group))
        for h in R:
            s_scr[h] = St2[h]
        for blk in range(group):
            for h in R:
                o_ref[blk * GLA_BLOCK:(blk + 1) * GLA_BLOCK, h * GLA_DV:(h + 1) * GLA_DV] = o[blk * GLA_H + h]

    return pl.pallas_call(
        body, name="gla_fwd", grid=(steps,),
        in_specs=[pl.BlockSpec((rows, G_W), lambda n: (n, C_G // G_W)),
                  pl.BlockSpec((rows, GLA_QK), lambda n: (n, 0))],
        out_specs=[pl.BlockSpec((rows, GLA_V), lambda n: (n, 0)),
                   pl.BlockSpec((1, GLA_H, GLA_DV, GLA_DK), lambda n: (n, 0, 0, 0))],
        out_shape=[jax.ShapeDtypeStruct((Lp, GLA_V), f32),
                   jax.ShapeDtypeStruct((steps, GLA_H, GLA_DV, GLA_DK), f32)],
        scratch_shapes=[pltpu.VMEM((GLA_H, GLA_DV, GLA_DK), f32)],
        compiler_params=_params("arbitrary"),
    )(proj, la)


def _gla_bwd(proj, la, sall, do, dproj, side=None):
    Lp = proj.shape[0]
    group = _gla_group(Lp)
    rows = group * GLA_BLOCK
    steps = Lp // rows
    R = range(GLA_H)

    def body(x_ref, la_ref, sall_ref, do_ref, dproj_ref, dx_ref, dla_ref, ds_scr):
        del dproj_ref

        @pl.when(pl.program_id(0) == 0)
        def _():
            ds_scr[...] = jnp.zeros_like(ds_scr)

        _, vjp = jax.vjp(_gla_blocks, [sall_ref[0, h] for h in R], *_gla_heads(x_ref, la_ref, group))
        do = [do_ref[blk * GLA_BLOCK:(blk + 1) * GLA_BLOCK, h * GLA_DV:(h + 1) * GLA_DV]
              for blk in range(group) for h in R]
        dS, dq, dk, dv, dl = vjp(([ds_scr[h] for h in R], do))
        for h in R:
            ds_scr[h] = dS[h]
        for blk in range(group):
            r = slice(blk * GLA_BLOCK, (blk + 1) * GLA_BLOCK)
            for h in R:
                sq, sk, sv = _gla_slices(h)
                i = blk * GLA_H + h
                dx_ref[r, sq] = dq[i].astype(bf16)
                dx_ref[r, sk] = dk[i].astype(bf16)
                dx_ref[r, sv] = dv[i].astype(bf16)
                dla_ref[r, sq] = dl[i]

    x_spec = pl.BlockSpec((rows, G_W), lambda n: (steps - 1 - n, C_G // G_W))
    rev = lambda n: (steps - 1 - n, 0)
    out = _call(
        body, name="gla_bwd", grid=(steps,),
        in_specs=[x_spec, pl.BlockSpec((rows, GLA_QK), rev),
                  pl.BlockSpec((1, GLA_H, GLA_DV, GLA_DK), lambda n: (steps - 1 - n, 0, 0, 0)),
                  pl.BlockSpec((rows, GLA_V), rev), _ANY],
        out_specs=[x_spec, pl.BlockSpec((rows, GLA_QK), rev)],
        out_shape=[jax.ShapeDtypeStruct(dproj.shape, dproj.dtype), jax.ShapeDtypeStruct((Lp, GLA_QK), f32)],
        aliases={4: 0}, scratch_shapes=[pltpu.VMEM((GLA_H, GLA_DV, GLA_DK), f32)],
        args=[proj, la, sall, do, dproj], sem=("arbitrary",), side=side)
    return out[0], out[1], out[2:]


def _gated_norm_fn(og, ol, zr, wg, wl):
    outs = []
    for h in range(GDN_H):
        s = slice(h * GDN_DV, (h + 1) * GDN_DV)
        outs.append(_rms(og[:, s])[0] * wg * _silu(zr[:, s]))
    for h in range(GLA_H):
        s = slice(h * GLA_DV, (h + 1) * GLA_DV)
        sr = slice(GDN_V + h * GLA_DV, GDN_V + (h + 1) * GLA_DV)
        outs.append(_rms(ol[:, s])[0] * wl * _silu(zr[:, sr]))
    return jnp.concatenate(outs, axis=-1)


def _gated_norm(og, ol, proj, wg, wl):
    Lp = og.shape[0]
    tr = _tile(Lp, 256)

    def body(og_ref, ol_ref, zr_ref, wg_ref, wl_ref, o_ref):
        o_ref[...] = _gated_norm_fn(og_ref[...], ol_ref[...], zr_ref[...], wg_ref[...], wl_ref[...]).astype(bf16)

    return pl.pallas_call(
        body, name="gated_norm", grid=(Lp // tr,),
        in_specs=[pl.BlockSpec((tr, GDN_V), lambda i: (i, 0)), pl.BlockSpec((tr, GLA_V), lambda i: (i, 0)),
                  pl.BlockSpec((tr, ZR_W), lambda i: (i, C_ZR // ZR_W)),
                  pl.BlockSpec((1, GDN_DV), lambda i: (0, 0)), pl.BlockSpec((1, GLA_DV), lambda i: (0, 0))],
        out_specs=pl.BlockSpec((tr, ZR_W), lambda i: (i, 0)),
        out_shape=jax.ShapeDtypeStruct((Lp, ZR_W), bf16),
        compiler_params=_params("parallel"),
    )(og, ol, proj, wg, wl)


def _gated_norm_bwd(og, ol, proj, wg, wl, dmix):
    Lp = og.shape[0]
    tr = _tile(Lp, 128)

    def body(og_ref, ol_ref, zr_ref, wg_ref, wl_ref, d_ref, dog_ref, dol_ref, dzr_ref, gwg_ref, gwl_ref):
        i = pl.program_id(0)
        _, vjp = jax.vjp(_gated_norm_fn, og_ref[...], ol_ref[...], zr_ref[...], wg_ref[...], wl_ref[...])
        dog, dol, dzr, gwg, gwl = vjp(d_ref[...])
        dog_ref[...] = dog
        dol_ref[...] = dol
        dzr_ref[...] = dzr.astype(bf16)

        @pl.when(i == 0)
        def _():
            gwg_ref[...] = gwg
            gwl_ref[...] = gwl

        @pl.when(i > 0)
        def _():
            gwg_ref[...] += gwg
            gwl_ref[...] += gwl

    og_spec = pl.BlockSpec((tr, GDN_V), lambda i: (i, 0))
    ol_spec = pl.BlockSpec((tr, GLA_V), lambda i: (i, 0))
    zr_spec = pl.BlockSpec((tr, ZR_W), lambda i: (i, C_ZR // ZR_W))
    vg = pl.BlockSpec((1, GDN_DV), lambda i: (0, 0))
    vl = pl.BlockSpec((1, GLA_DV), lambda i: (0, 0))
    return pl.pallas_call(
        body, name="gated_norm_bwd", grid=(Lp // tr,),
        in_specs=[og_spec, ol_spec, zr_spec, vg, vl, pl.BlockSpec((tr, ZR_W), lambda i: (i, 0))],
        out_specs=[og_spec, ol_spec, zr_spec, vg, vl],
        out_shape=[jax.ShapeDtypeStruct((Lp, GDN_V), f32), jax.ShapeDtypeStruct((Lp, GLA_V), f32),
                   jax.ShapeDtypeStruct((Lp, C_END), bf16),
                   jax.ShapeDtypeStruct((1, GDN_DV), f32), jax.ShapeDtypeStruct((1, GLA_DV), f32)],
        compiler_params=_params("arbitrary"),
    )(og, ol, proj, wg, wl, dmix)


def _adamw(g, w, m, v, name):
    R, C = g.shape
    tr = _tile(R, 256, 8) if R % 8 == 0 and R > 256 else R
    c1 = 1.0 - ADAM_B1 ** ADAM_STEP
    c2 = 1.0 - ADAM_B2 ** ADAM_STEP

    def body(g_ref, w_ref, m_ref, v_ref, d_ref, mo_ref, vo_ref):
        g_ = g_ref[...]
        m2 = ADAM_B1 * m_ref[...] + (1.0 - ADAM_B1) * g_
        v2 = ADAM_B2 * v_ref[...] + (1.0 - ADAM_B2) * (g_ * g_)
        mo_ref[...] = m2
        vo_ref[...] = v2
        d_ref[...] = -ADAM_LR * ((m2 / c1) / (jnp.sqrt(v2 / c2) + ADAM_EPS) + ADAM_WD * w_ref[...])

    blk = pl.BlockSpec((tr, C), lambda i: (i, 0))
    return pl.pallas_call(
        body, name=name, grid=(R // tr,), in_specs=[blk] * 4, out_specs=[blk] * 3,
        out_shape=[jax.ShapeDtypeStruct((R, C), f32)] * 3,
        compiler_params=_params("parallel"),
    )(g, w, m, v)


def _sum_slots(r, name):
    n, R, C = r.shape
    tr = _tile(R, 128, 16) if R % 16 == 0 and R > 128 else R

    def body(r_ref, o_ref):
        acc = r_ref[0].astype(f32)
        for s in range(1, n):
            acc = acc + r_ref[s].astype(f32)
        o_ref[...] = acc

    return pl.pallas_call(
        body, name=name, grid=(R // tr,),
        in_specs=[pl.BlockSpec((n, tr, C), lambda i: (0, i, 0))],
        out_specs=pl.BlockSpec((tr, C), lambda i: (i, 0)),
        out_shape=jax.ShapeDtypeStruct((R, C), f32),
        compiler_params=_params("parallel"),
    )(r)


SIBLING_PARTS = 8


class _Siblings:
    def __init__(self, arrays):
        self.arrays = list(arrays)
        self.n = len(self.arrays)
        self.parts = [next(p for p in range(SIBLING_PARTS, 0, -1) if a.shape[0] % (8 * p) == 0 or p == 1)
                      for a in self.arrays]
        total = sum(self.parts)
        self.out_shape = [jax.ShapeDtypeStruct((2,) + a.shape, a.dtype) for a in self.arrays]
        self.sems = [pltpu.SemaphoreType.DMA((total,)), pltpu.SemaphoreType.DMA((total,)),
                     pltpu.SemaphoreType.DMA((self.n,))]

    def hooks(self, ins, outs, send, recv, lsem):
        def copies():
            x, y, c = lax.axis_index("x"), lax.axis_index("y"), lax.axis_index("c")
            out, k = [], 0
            for a in range(self.n):
                out.append((pltpu.make_async_copy(ins[a], outs[a].at[c], lsem.at[a]), None))
                rows = self.arrays[a].shape[0] // self.parts[a]
                for part in range(self.parts[a]):
                    r = pl.ds(part * rows, rows)
                    mk = lambda dst, a=a, r=r, k=k: pltpu.make_async_remote_copy(
                        src_ref=ins[a].at[r], dst_ref=dst.at[r], send_sem=send.at[k], recv_sem=recv.at[k],
                        device_id=(x, y, 1 - c), device_id_type=MESH)
                    out.append((mk(outs[a].at[c]), mk(outs[a].at[1 - c])))
                    k += 1
            return out

        return _start_wait(copies)


def _comm_now(name, sides):
    total = sum(s.n for s in sides)

    def body(*refs):
        ins, outs, sems = refs[:total], refs[total:2 * total], refs[2 * total:]
        hooks, o = [], 0
        for i, s in enumerate(sides):
            hooks.append(s.hooks(ins[o:o + s.n], outs[o:o + s.n], *sems[3 * i:3 * i + 3]))
            o += s.n
        for start, _ in hooks:
            start()
        for _, wait in hooks:
            wait()

    out = pl.pallas_call(
        body, name=name, in_specs=[_ANY] * total, out_specs=[_ANY] * total,
        out_shape=[sh for s in sides for sh in s.out_shape], scratch_shapes=[sm for s in sides for sm in s.sems],
    )(*[a for s in sides for a in s.arrays])
    res, o = [], 0
    for s in sides:
        res.append(list(out[o:o + s.n]))
        o += s.n
    return res


def _cat_cols(g):
    return jnp.concatenate([g[i] for i in range(N_CHIP)], axis=-1)


def _row_slabs(a):
    return a.reshape(N_DEV, a.shape[0] // N_DEV, a.shape[1])


def _w_in_columns(g_wp, g_wsm):
    return jnp.concatenate([g_wp[:, C_QKV:C_END], g_wp[:, C_ZR:C_ZR + GDN_V], g_wsm[:, :SM_LR],
                            g_wp[:, C_G:C_G + G_W], g_wp[:, C_ZR + GDN_V:C_ZR + ZR_W],
                            g_wsm[:, SM_LR:SM_LR + GATE_RANK]], axis=1)


def _step(x, loss_target, p, meta, shard):
    _, S, D = x.shape
    alog_p = jnp.pad(p["gdn_a_log"], ((0, 0), (0, SM_W - GDN_H)))
    dtb_p = jnp.pad(p["gdn_dt_bias"], ((0, 0), (0, SM_W - GDN_H)))
    m64 = jnp.concatenate([jnp.zeros((PAD, D), f32), meta], axis=0)
    gate_b, gdn_norm_w, gla_norm_w = p["gla_gate_b"], p["gdn_norm_w"], p["gla_norm_w"]
    half = shard["w_up"].shape[0] // 2

    h0, n1, (w_in4, conv4, w24) = _embed_norm(
        x, m64, p["attn_norm_w"], side=_Gather([shard["w_in"], shard["gdn_conv_w"], shard["gla_gate_w2"]]))
    w_in, conv_w, w2 = _cat_cols(w_in4), _cat_cols(conv4), _cat_cols(w24)
    wp = jnp.concatenate([w_in[:, R_Z:R_AB], w_in[:, R_GR:R_LR], w_in[:, R_G:R_GR], w_in[:, R_QKV:R_Z]], axis=1)
    wsm = jnp.concatenate([w_in[:, R_AB:R_G], w_in[:, R_LR:R_END],
                           jnp.zeros((D, SM_W - SM_LR - GATE_RANK), w_in.dtype)], axis=1)
    w2p = jnp.pad(w2, ((SM_LR, SM_W - SM_LR - GATE_RANK), (0, 0)))
    proj, (w_out4, w_up4a) = _mm(n1, wp, "nn", "proj", side=_Gather([shard["w_out"], shard["w_up"][:half]]))
    w_out = w_out4.reshape(-1, D)
    psm = _mm(n1, wsm, "nn", "proj_small")
    gb, la = _gates(psm, w2p, gate_b, alog_p, dtb_p)
    qkvc, (w_up4b,) = _conv(proj, conv_w, side=_Gather([shard["w_up"][half:]]))
    w_up = jnp.concatenate([_cat_cols(w_up4a), _cat_cols(w_up4b)], axis=0)
    og, sall, pall, (w_gate4,) = _gdn_fwd(qkvc, gb, side=_Gather([shard["w_gate"]]))
    w_gate = _cat_cols(w_gate4)
    ol, stall = _gla_fwd(proj, la)
    mixed = _gated_norm(og, ol, proj, gdn_norm_w, gla_norm_w)
    attn = _mm(mixed, w_out, "nn", "out_proj")
    h1, n2 = _add_norm(h0, attn, p["ffn_norm_w"])
    act, act_dgate, act_dup, (w_down4,) = _ffn_in(n2, w_gate, w_up, side=_Gather([shard["w_down"]]))
    w_down = w_down4.reshape(-1, D)
    ffn = _mm(act, w_down, "nn", "ffn_down", tm_cap=1376, tn_cap=256, tk_cap=1 << 20)
    dh2, dh2b, lossp, g_final = _final(h1, ffn, loss_target, p["final_norm_w"])

    g_down = _mm(act, dh2b, "tn", "g_w_down", out_dtype=bf16, **WHOLE_K_T)
    dg, du = _ffn_dact(dh2b, w_down, act_dgate, act_dup)
    g_gate = _mm(n2, dg, "tn", "g_w_gate", tm_cap=512, tn_cap=1408, tk_cap=2752, out_dtype=bf16, col_slabs=True)
    g_up = _mm(n2, du, "tn", "g_w_up", tm_cap=512, tn_cap=1408, tk_cap=4128, out_dtype=bf16, col_slabs=True)
    dn2 = _ffn_dn(dg, du, w_gate, w_up)
    dh1, dh1b, g_ffn_norm = _norm_bwd(dn2, h1, dh2, p["ffn_norm_w"])
    dmix = _mm(dh1b, w_out, "nt", "d_mixed")
    g_out = _mm(mixed, dh1b, "tn", "g_w_out", out_dtype=bf16, **WHOLE_K_T)
    dog, dol, dproj, g_gdn_norm, g_gla_norm = _gated_norm_bwd(og, ol, proj, gdn_norm_w, gla_norm_w, dmix)
    dproj, dla, (r_down,) = _gla_bwd(proj, la, stall, dol, dproj, side=_Exchange([_row_slabs(g_down)]))
    dqkvc, dgb, (r_gate, r_up, r_out, h_down) = _gdn_bwd(
        qkvc, gb, sall, pall, dog,
        side=_Sides(_Exchange([g_gate, g_up, _row_slabs(g_out)]), _Siblings([_sum_slots(r_down, "sum_w_down")])))
    dproj, g_conv, (h_gate,) = _conv_bwd(proj, conv_w, dqkvc, dproj,
                                         side=_Siblings([_sum_slots(r_gate, "sum_w_gate")]))
    dpsm, g_w2p, g_gate_b, g_alog, g_dtb = _gates_bwd(psm, w2p, gate_b, alog_p, dtb_p, dgb, dla)
    g_wp, (h_up, h_out) = _mm(
        n1, dproj, "tn", "g_w_in", out_dtype=bf16,
        side=_Siblings([_sum_slots(r_up, "sum_w_up"), _sum_slots(r_out, "sum_w_out")]), **WHOLE_K_T)
    g_wsm = _mm(n1, dpsm, "tn", "g_w_in_small", out_dtype=bf16, **WHOLE_K_T)
    dn1, r_in = _mm(dproj, wp, "nt", "d_n1", side=_Exchange([_row_slabs(g_wp), _row_slabs(g_wsm)]), **WHOLE_K)
    dn1 = _mm(dpsm, wsm, "nt", "d_n1_small", acc_in=dn1)
    s_in = _w_in_columns(_sum_slots(r_in[0], "sum_w_in"), _sum_slots(r_in[1], "sum_w_in_small"))
    in_by_chip = s_in.reshape(s_in.shape[0], N_CHIP, -1).transpose(1, 0, 2).astype(bf16)
    grad_x, g_meta, g_attn_norm, (h_in,) = _embed_norm_bwd(dn1, h0, dh1, p["attn_norm_w"], S,
                                                           side=_Exchange([], by_chip=[in_by_chip]))

    received = dict(w_in=h_in, w_gate=h_gate, w_up=h_up, w_out=h_out, w_down=h_down)
    small = dict(
        meta_tokens=g_meta, attn_norm_w=g_attn_norm, gdn_conv_w=g_conv, gdn_a_log=g_alog[:, :GDN_H],
        gdn_dt_bias=g_dtb[:, :GDN_H], gdn_norm_w=g_gdn_norm, gla_gate_w2=g_w2p[SM_LR:SM_LR + GATE_RANK],
        gla_gate_b=g_gate_b, gla_norm_w=g_gla_norm, ffn_norm_w=g_ffn_norm, final_norm_w=g_final)
    return lossp[0, 0], grad_x, received, small


_WEIGHTS = ("meta_tokens", "attn_norm_w", "w_in", "gdn_conv_w", "gdn_a_log", "gdn_dt_bias", "gdn_norm_w",
            "gla_gate_w2", "gla_gate_b", "gla_norm_w", "w_out", "ffn_norm_w", "w_gate", "w_up", "w_down",
            "final_norm_w")
_BIG_COLS = ("w_in", "w_gate", "w_up")
_BIG_ROWS = ("w_out", "w_down")
_SMALL_SHARDED = ("meta_tokens", "gdn_conv_w", "gla_gate_w2")


def kernel(x, meta_tokens, attn_norm_w, w_in, gdn_conv_w, gdn_a_log, gdn_dt_bias, gdn_norm_w, gla_gate_w2, gla_gate_b, gla_norm_w, w_out, ffn_norm_w, w_gate, w_up, w_down, final_norm_w, loss_target, m_meta_tokens, m_attn_norm_w, m_w_in, m_gdn_conv_w, m_gdn_a_log, m_gdn_dt_bias, m_gdn_norm_w, m_gla_gate_w2, m_gla_gate_b, m_gla_norm_w, m_w_out, m_ffn_norm_w, m_w_gate, m_w_up, m_w_down, m_final_norm_w, v_meta_tokens, v_attn_norm_w, v_w_in, v_gdn_conv_w, v_gdn_a_log, v_gdn_dt_bias, v_gdn_norm_w, v_gla_gate_w2, v_gla_gate_b, v_gla_norm_w, v_w_out, v_ffn_norm_w, v_w_gate, v_w_up, v_w_down, v_final_norm_w):
    w = dict(meta_tokens=meta_tokens, attn_norm_w=attn_norm_w, w_in=w_in, gdn_conv_w=gdn_conv_w, gdn_a_log=gdn_a_log,
             gdn_dt_bias=gdn_dt_bias, gdn_norm_w=gdn_norm_w, gla_gate_w2=gla_gate_w2, gla_gate_b=gla_gate_b,
             gla_norm_w=gla_norm_w, w_out=w_out, ffn_norm_w=ffn_norm_w, w_gate=w_gate, w_up=w_up, w_down=w_down,
             final_norm_w=final_norm_w)
    m = dict(meta_tokens=m_meta_tokens, attn_norm_w=m_attn_norm_w, w_in=m_w_in, gdn_conv_w=m_gdn_conv_w,
             gdn_a_log=m_gdn_a_log, gdn_dt_bias=m_gdn_dt_bias, gdn_norm_w=m_gdn_norm_w, gla_gate_w2=m_gla_gate_w2,
             gla_gate_b=m_gla_gate_b, gla_norm_w=m_gla_norm_w, w_out=m_w_out, ffn_norm_w=m_ffn_norm_w,
             w_gate=m_w_gate, w_up=m_w_up, w_down=m_w_down, final_norm_w=m_final_norm_w)
    v = dict(meta_tokens=v_meta_tokens, attn_norm_w=v_attn_norm_w, w_in=v_w_in, gdn_conv_w=v_gdn_conv_w,
             gdn_a_log=v_gdn_a_log, gdn_dt_bias=v_gdn_dt_bias, gdn_norm_w=v_gdn_norm_w, gla_gate_w2=v_gla_gate_w2,
             gla_gate_b=v_gla_gate_b, gla_norm_w=v_gla_norm_w, w_out=v_w_out, ffn_norm_w=v_ffn_norm_w,
             w_gate=v_w_gate, w_up=v_w_up, w_down=v_w_down, final_norm_w=v_final_norm_w)
    chip = 2 * lax.axis_index("x") + lax.axis_index("y")

    def two_d(a):
        return a.reshape(1, -1) if a.ndim == 1 else a.reshape(-1, a.shape[-1])

    w2d = {k: two_d(a) for k, a in w.items()}
    big = _BIG_COLS + _BIG_ROWS
    small = tuple(k for k in _WEIGHTS if k not in big)

    (meta4,), = _comm_now("gather_meta", [_Gather([w2d["meta_tokens"]])])
    shard = {k: w2d[k].astype(bf16) for k in big}
    shard.update({k: w2d[k] for k in ("gdn_conv_w", "gla_gate_w2")})
    lossp, grad_x, received, g = _step(x, loss_target, {k: w2d[k] for k in small}, _cat_cols(meta4), shard)
    loss = lax.psum(lossp, ("x", "y", "c"))

    sizes = [g[k].size for k in small]
    total = sum(sizes)
    rows = -(-total // 1024)
    rows += (-rows) % 8
    packed = jnp.concatenate([g[k].reshape(-1) for k in small] + [jnp.zeros((rows * 1024 - total,), f32)])
    (packed8,), = _comm_now("exchange_small", [_Exchange([], [packed.reshape(rows, 1024)])])
    red = {k: h.reshape(w2d[k].shape).astype(f32) for k, h in received.items()}
    psum_small = _sum_slots(packed8, "sum_small").reshape(-1)
    off = 0
    for k, n in zip(small, sizes):
        a = psum_small[off:off + n].reshape(g[k].shape)
        off += n
        if k in _SMALL_SHARDED:
            c = w2d[k].shape[1]
            a = lax.dynamic_slice_in_dim(a, chip * c, c, axis=1)
        red[k] = a

    grads, deltas, new_m, new_v = [], [], [], []
    for k in _WEIGHTS:
        d, m2, v2 = _adamw(red[k], w2d[k], two_d(m[k]), two_d(v[k]), "adamw_" + k)
        shape = w[k].shape
        grads.append(red[k].reshape(shape))
        deltas.append(d.reshape(shape))
        new_m.append(m2.reshape(shape))
        new_v.append(v2.reshape(shape))
    return (loss, grad_x, *grads, *deltas, *new_m, *new_v)
```

```python
import functools

import jax
import jax.numpy as jnp
from jax import lax
from jax.experimental import pallas as pl
from jax.experimental.pallas import tpu as pltpu

f32 = jnp.float32
bf16 = jnp.bfloat16
HIGH = lax.Precision.HIGH
MESH = pl.DeviceIdType.MESH

N_META = 16
CONV_K = 4
GDN_H, GDN_DK, GDN_DV, GDN_C = 8, 128, 128, 64
GLA_H, GLA_DK, GLA_DV, GLA_C = 4, 128, 256, 16
GATE_RANK = 16
GATE_NORMALIZER = 16.0
EPS = 1e-6
GDN_QK = GDN_H * GDN_DK
GDN_V = GDN_H * GDN_DV
GLA_QK = GLA_H * GLA_DK
GLA_V = GLA_H * GLA_DV
PAD = (-N_META) % GDN_C
OFF = PAD + N_META
ROWS = 64

R_QKV, R_Z, R_AB, R_G, R_GR, R_LR, R_END = 0, 3072, 4096, 4112, 6160, 7184, 7200
C_ZR, C_G, C_QKV, C_END = 0, 2048, 4096, 7168
W_IN_SPLIT = 3072
ZR_W = GDN_V + GLA_V
G_W = 2 * GLA_QK + GLA_V
QKV_W = 2 * GDN_QK + GDN_V
Q0, K0, V0 = 0, GDN_QK, 2 * GDN_QK
SM_W = 128
SM_LR = 2 * GDN_H

ADAM_LR, ADAM_B1, ADAM_B2, ADAM_EPS, ADAM_WD, ADAM_STEP = 0.001, 0.9, 0.999, 1e-08, 0.01, 10

VMEM_LIMIT_V7X = 56 * 1024 * 1024
N_DEV = 8
N_CHIP = 4


def _params(*sem):
    return pltpu.CompilerParams(dimension_semantics=sem, vmem_limit_bytes=VMEM_LIMIT_V7X)


def _tile(n, cap, mult=16):
    best = None
    for d in range(mult, min(n, cap) + 1, mult):
        if n % d == 0:
            best = d
    assert best is not None, (n, cap, mult)
    return best


NN = ((1,), (0,))
NT = ((1,), (1,))
TN = ((0,), (0,))


def _dot(a, b, dims, prec=None):
    return lax.dot_general(a, b, (dims, ((), ())), precision=prec, preferred_element_type=f32)


def _mmb(a, b, dims):
    return _dot(a.astype(bf16), b.astype(bf16), dims)


def _sigmoid(x):
    return jax.nn.sigmoid(x)


def _silu(x):
    return x * _sigmoid(x)


def _dsilu(x):
    s = _sigmoid(x)
    return s * (1.0 + x * (1.0 - s))


def _log1p_exp_neg_abs(x):
    t = jnp.exp(-jnp.abs(x))
    u = 1.0 + t
    d = u - 1.0
    return jnp.where(d == 0.0, t, jnp.log(u) * (t / jnp.where(d == 0.0, 1.0, d)))


def _softplus(x):
    return jnp.maximum(x, 0.0) + _log1p_exp_neg_abs(x)


def _log_sigmoid(x):
    return jnp.minimum(x, 0.0) - _log1p_exp_neg_abs(x)


def _rms(x):
    r = lax.rsqrt(jnp.mean(x * x, axis=-1, keepdims=True) + EPS)
    return x * r, r


def _rms_bwd(dy, xh, r, w):
    t = dy * w
    return r * (t - xh * jnp.mean(t * xh, axis=-1, keepdims=True))


def _l2n(x):
    return x * lax.rsqrt(jnp.sum(x * x, axis=-1, keepdims=True) + EPS)


INV_LEAF = 8


def _same_block(C, b):
    sh = b.bit_length() - 1
    row = lax.broadcasted_iota(jnp.int32, (C, C), 0)
    col = lax.broadcasted_iota(jnp.int32, (C, C), 1)
    return lax.shift_right_logical(row, sh) == lax.shift_right_logical(col, sh)


def _tri_inv_impl(As):
    C = As[0].shape[0]
    R = range(len(As))
    row = lax.broadcasted_iota(jnp.int32, (C, C), 0)
    col = lax.broadcasted_iota(jnp.int32, (C, C), 1)
    eye = (row == col).astype(f32)
    b = INV_LEAF
    inner = _same_block(C, b)
    leaf = [jnp.where(inner, As[h], 0.0) for h in R]
    d = [eye - leaf[h] for h in R]
    pw = leaf
    n = 2
    while n < b:
        pw = [_dot(pw[h], pw[h], NN, HIGH) for h in R]
        d = [_dot(d[h], eye + pw[h], NN, HIGH) for h in R]
        n *= 2
    while b < C:
        outer = _same_block(C, 2 * b)
        level = jnp.logical_and(outer, jnp.logical_not(inner))
        ed = [_dot(jnp.where(level, As[h], 0.0), d[h], NN, HIGH) for h in R]
        d = [d[h] - _dot(d[h], ed[h], NN, HIGH) for h in R]
        inner = outer
        b *= 2
    return d


@jax.custom_vjp
def _tri_inv(As):
    return _tri_inv_impl(As)


def _tri_inv_fwd(As):
    d = _tri_inv_impl(As)
    return d, d


def _tri_inv_bwd(d, g):
    R = range(len(d))
    t = [_dot(d[h], g[h], TN, HIGH) for h in R]
    return ([-_dot(t[h], d[h], NT, HIGH) for h in R],)


_tri_inv.defvjp(_tri_inv_fwd, _tri_inv_bwd)


@jax.custom_vjp
def _tri_inv_known(As, Ps):
    del As
    return Ps


def _tri_inv_known_fwd(As, Ps):
    del As
    return Ps, Ps


def _tri_inv_known_bwd(d, g):
    return _tri_inv_bwd(d, g)[0], [jnp.zeros_like(x) for x in d]


_tri_inv_known.defvjp(_tri_inv_known_fwd, _tri_inv_known_bwd)


def _gdn_chunk(Ss, qrs, krs, vs, betas, gs, Ps=None):
    H = len(Ss)
    C, dk = qrs[0].shape
    R = range(len(qrs))
    row = lax.broadcasted_iota(jnp.int32, (C, C), 0)
    col = lax.broadcasted_iota(jnp.int32, (C, C), 1)
    causal = row >= col
    strict = row > col
    cf = causal.astype(f32)
    q = [_l2n(qrs[h]) * (dk ** -0.5) for h in R]
    k = [_l2n(krs[h]) for h in R]
    mc = [_rows_exact(cf, jnp.broadcast_to(gs[h], (C, C))) for h in R]
    gc = [mc[h][:, 0:1] for h in R]
    decay = [jnp.where(causal, jnp.exp(jnp.where(causal, mc[h] - mc[h].T, 0.0)), 0.0) for h in R]
    kb = [k[h] * betas[h] for h in R]
    a = [jnp.where(strict, _mmb(kb[h], k[h], NT) * decay[h], 0.0) for h in R]
    p = _tri_inv(a) if Ps is None else _tri_inv_known(a, Ps)
    egc = [jnp.exp(gc[h]) for h in R]
    u = [_mmb(p[h], vs[h] * betas[h], NN) for h in R]
    w = [_mmb(p[h], kb[h] * egc[h], NN) for h in R]
    qk = [jnp.where(causal, _mmb(q[h], k[h], NT) * decay[h], 0.0) for h in R]
    qe = [q[h] * egc[h] for h in R]
    gl = [gc[h][C - 1:C, :] for h in R]
    kd = [k[h] * jnp.exp(gl[h] - gc[h]) for h in R]
    egl = [jnp.exp(gl[h]) for h in R]
    S, o, entering = list(Ss), [], []
    for chunk in range(len(qrs) // H):
        idx = [chunk * H + h for h in range(H)]
        entering += S
        v_new = [u[i] - _mmb(w[i], S[h], NN) for h, i in enumerate(idx)]
        o += [_mmb(qe[i], S[h], NN) + _mmb(qk[i], v_new[h], NN) for h, i in enumerate(idx)]
        S = [S[h] * egl[i] + _mmb(kd[i], v_new[h], TN) for h, i in enumerate(idx)]
    return S, o, p, entering


def _rows_exact_impl(m01, x, dims):
    m = m01.astype(bf16)
    x1 = x.astype(bf16)
    r1 = x - x1.astype(f32)
    x2 = r1.astype(bf16)
    x3 = (r1 - x2.astype(f32)).astype(bf16)
    d = lambda y: _dot(m, y, dims)
    return d(x1) + (d(x2) + d(x3))


@jax.custom_vjp
def _rows_exact(m01, x):
    return _rows_exact_impl(m01, x, NN)


def _rows_exact_fwd(m01, x):
    return _rows_exact_impl(m01, x, NN), m01


def _rows_exact_bwd(m01, g):
    return jnp.zeros_like(m01), _rows_exact_impl(m01, g, TN)


_rows_exact.defvjp(_rows_exact_fwd, _rows_exact_bwd)


def _gla_blocks(Sts, qrs, ks, vs, las):
    H = len(Sts)
    n = len(qrs)
    C, dk = qrs[0].shape
    R = range(n)
    row = lax.broadcasted_iota(jnp.int32, (C, C), 0)
    col = lax.broadcasted_iota(jnp.int32, (C, C), 1)
    ri = lax.broadcasted_iota(jnp.int32, (C, dk), 0)
    q = [qrs[h] * (dk ** -0.5) for h in R]
    running = (row >= col).astype(f32)
    b = [_rows_exact(running, las[h]) for h in R]
    sc = [jnp.where(row == col, jnp.sum(q[h] * ks[h], axis=-1, keepdims=True), 0.0) for h in R]
    s = C // 2
    while s >= 1:
        sh = s.bit_length() - 1
        ref = lax.shift_left(lax.shift_right_logical(row, sh + 1), sh + 1) + (s - 1)
        pick = (col == ref).astype(f32)
        bref = [_rows_exact(pick, b[h]) for h in R]
        upper = (lax.shift_right_logical(ri, sh) & 1) == 1
        qt = [jnp.where(upper, q[h] * jnp.exp(jnp.where(upper, b[h] - bref[h], 0.0)), 0.0) for h in R]
        kt = [jnp.where(upper, 0.0, ks[h] * jnp.exp(jnp.where(upper, 0.0, bref[h] - b[h]))) for h in R]
        same = lax.shift_right_logical(row, sh + 1) == lax.shift_right_logical(col, sh + 1)
        sc = [sc[h] + jnp.where(same, _mmb(qt[h], kt[h], NT), 0.0) for h in R]
        s //= 2
    o = [_mmb(sc[h], vs[h], NN) for h in R]
    qe = [q[h] * jnp.exp(b[h]) for h in R]
    bl = [b[h][C - 1:C, :] for h in R]
    upd = [_mmb(vs[h], ks[h] * jnp.exp(bl[h] - b[h]), TN) for h in R]
    ebl = [jnp.exp(bl[h]) for h in R]
    St = list(Sts)
    for blk in range(n // H):
        for h in range(H):
            i = blk * H + h
            o[i] = o[i] + _mmb(qe[i], St[h], NT)
        St = [St[h] * ebl[blk * H + h] + upd[blk * H + h] for h in range(H)]
    return St, o


_ANY = pl.BlockSpec(memory_space=pl.ANY)


class _Gather:
    def __init__(self, arrays):
        self.arrays = list(arrays)
        self.n = len(self.arrays)
        self.out_shape = [jax.ShapeDtypeStruct((N_CHIP,) + a.shape, a.dtype) for a in self.arrays]
        self.sems = [pltpu.SemaphoreType.DMA((self.n, 3)), pltpu.SemaphoreType.DMA((self.n, 3)),
                     pltpu.SemaphoreType.DMA((self.n,))]

    def hooks(self, ins, outs, send, recv, lsem):
        def copies():
            x, y, c = lax.axis_index("x"), lax.axis_index("y"), lax.axis_index("c")
            me = 2 * x + y
            out = []
            for a in range(self.n):
                out.append((pltpu.make_async_copy(ins[a], outs[a].at[me], lsem.at[a]), None))
                for j, (px, py) in enumerate([(1 - x, y), (x, 1 - y), (1 - x, 1 - y)]):
                    mk = lambda dst, a=a, j=j, px=px, py=py: pltpu.make_async_remote_copy(
                        src_ref=ins[a], dst_ref=dst, send_sem=send.at[a, j], recv_sem=recv.at[a, j],
                        device_id=(px, py, c), device_id_type=MESH)
                    out.append((mk(outs[a].at[me]), mk(outs[a].at[2 * px + py])))
            return out

        return _start_wait(copies)


class _Exchange:
    def __init__(self, slotted, shared=(), by_chip=()):
        self.arrays = list(slotted) + list(by_chip) + list(shared)
        self.ns, self.nc = len(slotted), len(by_chip)
        self.n = len(self.arrays)
        self.out_shape = [jax.ShapeDtypeStruct(a.shape, a.dtype) for a in slotted]
        self.out_shape += [jax.ShapeDtypeStruct((N_DEV,) + a.shape[1:], a.dtype) for a in by_chip]
        self.out_shape += [jax.ShapeDtypeStruct((N_DEV,) + b.shape, b.dtype) for b in shared]
        self.sems = [pltpu.SemaphoreType.DMA((self.n, N_DEV - 1)), pltpu.SemaphoreType.DMA((self.n, N_DEV - 1)),
                     pltpu.SemaphoreType.DMA((self.n,))]

    def hooks(self, ins, outs, send, recv, lsem):
        def copies():
            x, y, c = lax.axis_index("x"), lax.axis_index("y"), lax.axis_index("c")
            me = 4 * x + 2 * y + c

            def src(a, dev):
                tx, ty, tc = dev
                if a < self.ns:
                    return ins[a].at[4 * tx + 2 * ty + tc]
                return ins[a].at[2 * tx + ty] if a < self.ns + self.nc else ins[a]

            out = []
            for a in range(self.n):
                out.append((pltpu.make_async_copy(src(a, (x, y, c)), outs[a].at[me], lsem.at[a]), None))
                for o in range(1, N_DEV):
                    dev = (1 - x if o & 4 else x, 1 - y if o & 2 else y, 1 - c if o & 1 else c)
                    t = 4 * dev[0] + 2 * dev[1] + dev[2]
                    mk = lambda dst, a=a, o=o, dev=dev: pltpu.make_async_remote_copy(
                        src_ref=src(a, dev), dst_ref=dst, send_sem=send.at[a, o - 1], recv_sem=recv.at[a, o - 1],
                        device_id=dev, device_id_type=MESH)
                    out.append((mk(outs[a].at[me]), mk(outs[a].at[t])))
            return out

        return _start_wait(copies)


class _Sides:
    def __init__(self, *members):
        self.members = members
        self.arrays = [a for s in members for a in s.arrays]
        self.n = len(self.arrays)
        self.out_shape = [sh for s in members for sh in s.out_shape]
        self.sems = [sm for s in members for sm in s.sems]

    def hooks(self, ins, outs, *sems):
        hooks, o = [], 0
        for i, s in enumerate(self.members):
            hooks.append(s.hooks(ins[o:o + s.n], outs[o:o + s.n], *sems[3 * i:3 * i + 3]))
            o += s.n

        def start():
            for st, _ in hooks:
                st()

        def wait():
            for _, wt in hooks:
                wt()

        return start, wait


def _start_wait(copies):
    def start():
        for s, _ in copies():
            s.start()

    def wait():
        for s, w in copies():
            (s if w is None else w).wait()

    return start, wait


def _call(body, *, name, grid, in_specs, out_specs, out_shape, args, sem, scratch_shapes=(), aliases=None, side=None):
    in_specs, out_specs, out_shape, args = list(in_specs), list(out_specs), list(out_shape), list(args)
    scratch_shapes = list(scratch_shapes)
    aliases = aliases or {}
    if side is None:
        return pl.pallas_call(
            body, name=name, grid=grid, in_specs=in_specs, out_specs=out_specs, out_shape=out_shape,
            scratch_shapes=scratch_shapes, input_output_aliases=aliases, compiler_params=_params(*sem))(*args)
    n_in, n_out, n_scr, ns = len(in_specs), len(out_specs), len(scratch_shapes), side.n

    def full_body(*refs):
        ins, refs = refs[:n_in], refs[n_in:]
        s_in, refs = refs[:ns], refs[ns:]
        outs, refs = refs[:n_out], refs[n_out:]
        s_out, refs = refs[:ns], refs[ns:]
        scr, sems = refs[:n_scr], refs[n_scr:]
        start, wait = side.hooks(s_in, s_out, *sems)
        ids = [pl.program_id(d) for d in range(len(grid))]
        first = functools.reduce(jnp.logical_and, [i == 0 for i in ids])
        last = functools.reduce(jnp.logical_and, [i == g - 1 for i, g in zip(ids, grid)])
        pl.when(first)(start)
        body(*ins, *outs, *scr)
        pl.when(last)(wait)

    return pl.pallas_call(
        full_body, name=name, grid=grid, in_specs=in_specs + [_ANY] * ns, out_specs=out_specs + [_ANY] * ns,
        out_shape=out_shape + side.out_shape, scratch_shapes=scratch_shapes + side.sems,
        input_output_aliases=aliases, compiler_params=_params(*(["arbitrary"] * len(grid))))(*args, *side.arrays)


WHOLE_K = dict(tm_cap=688, tn_cap=512, tk_cap=1 << 20)
WHOLE_K_T = dict(tm_cap=512, tn_cap=512, tk_cap=1 << 20)

def _mm(a, b, mode, name, *, tm_cap=1408, tn_cap=1024, tk_cap=2048, out_dtype=f32, acc_in=None, side=None,
        col_slabs=False, b_cols=None):
    if mode == "nn":
        (M, K), (K2, N) = a.shape, b.shape
    elif mode == "nt":
        (M, K), (N, K2) = a.shape, b.shape
    else:
        (K, M), (K2, N) = a.shape, b.shape
    assert K == K2, (name, a.shape, b.shape)
    b_first = 0
    if b_cols is not None:
        assert mode != "nt"
        b_first, N = b_cols
    tm = _tile(M // 2 if col_slabs else M, tm_cap)
    tn = _tile(N // N_CHIP if col_slabs else N, tn_cap, 128)
    tk = _tile(K, tk_cap, 128 if K % 128 == 0 else 16)
    nk = K // tk
    dims = {"nn": NN, "nt": NT, "tn": TN}[mode]
    use_scratch = nk > 1 and out_dtype != f32

    def body(*refs):
        if acc_in is not None:
            a_ref, b_ref, c_ref, o_ref, *scr = refs
        else:
            a_ref, b_ref, o_ref, *scr = refs
            c_ref = None
        p = _mmb(a_ref[...], b_ref[...], dims)
        if nk == 1:
            if c_ref is not None:
                p = p + c_ref[...]
            o_ref[...] = p.astype(out_dtype)
            return
        k = pl.program_id(2)
        acc = scr[0] if use_scratch else o_ref

        @pl.when(k == 0)
        def _():
            acc[...] = p if c_ref is None else p + c_ref[...]

        @pl.when(k > 0)
        def _():
            acc[...] += p

        if use_scratch:
            @pl.when(k == nk - 1)
            def _():
                o_ref[...] = acc[...].astype(out_dtype)

    if mode == "tn":
        a_spec = pl.BlockSpec((tk, tm), lambda i, j, k: (k, i))
    else:
        a_spec = pl.BlockSpec((tm, tk), lambda i, j, k: (i, k))
    if mode == "nt":
        b_spec = pl.BlockSpec((tn, tk), lambda i, j, k: (j, k))
    else:
        assert b_first % tn == 0, (name, b_first, tn)
        b_spec = pl.BlockSpec((tk, tn), lambda i, j, k: (k, j + b_first // tn))
    if col_slabs:
        assert acc_in is None
        ni, nj = M // 2 // tm, N // N_CHIP // tn
        o_spec = pl.BlockSpec((None, tm, tn), lambda i, j, k: (2 * (j // nj) + i // ni, i % ni, j % nj))
        o_shape = jax.ShapeDtypeStruct((N_DEV, M // 2, N // N_CHIP), out_dtype)
    else:
        o_spec = pl.BlockSpec((tm, tn), lambda i, j, k: (i, j))
        o_shape = jax.ShapeDtypeStruct((M, N), out_dtype)
    in_specs = [a_spec, b_spec]
    args = [a, b]
    if acc_in is not None:
        in_specs.append(o_spec)
        args.append(acc_in)
    out = _call(body, name=name, grid=(M // tm, N // tn, nk), in_specs=in_specs, out_specs=[o_spec],
                out_shape=[o_shape], args=args,
                scratch_shapes=[pltpu.VMEM((tm, tn), f32)] if use_scratch else [],
                sem=("parallel", "parallel", "arbitrary"), side=side)
    return out[0] if side is None else (out[0], out[1:])


def _embed_norm(x3, m64, w, side=None):
    _, S, D = x3.shape
    Lp = OFF + S

    def body(x_ref, m_ref, w_ref, h_ref, n_ref):
        i = pl.program_id(0)
        h = jnp.where(i == 0, m_ref[...], x_ref[...])
        h_ref[...] = h
        xh, _ = _rms(h)
        n_ref[...] = (xh * w_ref[...]).astype(bf16)

    row = pl.BlockSpec((ROWS, D), lambda i: (i, 0))
    out = _call(
        body, name="embed_norm", grid=(Lp // ROWS,),
        in_specs=[pl.BlockSpec((None, ROWS, D), lambda i: (0, jnp.maximum(i - 1, 0), 0)),
                  pl.BlockSpec((ROWS, D), lambda i: (0, 0)),
                  pl.BlockSpec((1, D), lambda i: (0, 0))],
        out_specs=[row, row],
        out_shape=[jax.ShapeDtypeStruct((Lp, D), f32), jax.ShapeDtypeStruct((Lp, D), bf16)],
        args=[x3, m64, w], sem=("parallel",), side=side)
    return out[0], out[1], out[2:]


def _add_norm(h, d, w):
    Lp, D = h.shape
    tr = _tile(Lp, 256)

    def body(h_ref, d_ref, w_ref, o_ref, n_ref):
        h1 = h_ref[...] + d_ref[...]
        o_ref[...] = h1
        xh, _ = _rms(h1)
        n_ref[...] = (xh * w_ref[...]).astype(bf16)

    row = pl.BlockSpec((tr, D), lambda i: (i, 0))
    return pl.pallas_call(
        body, name="add_norm", grid=(Lp // tr,),
        in_specs=[row, row, pl.BlockSpec((1, D), lambda i: (0, 0))], out_specs=[row, row],
        out_shape=[jax.ShapeDtypeStruct((Lp, D), f32), jax.ShapeDtypeStruct((Lp, D), bf16)],
        compiler_params=_params("parallel"),
    )(h, d, w)


def _norm_bwd(dn, h, dh, w):
    Lp, D = h.shape
    tr = _tile(Lp, 256)

    def body(dn_ref, h_ref, dh_ref, w_ref, o_ref, ob_ref, gw_ref):
        i = pl.program_id(0)
        xh, r = _rms(h_ref[...])
        dn_ = dn_ref[...]
        o = dh_ref[...] + _rms_bwd(dn_, xh, r, w_ref[...])
        o_ref[...] = o
        ob_ref[...] = o.astype(bf16)
        gw = jnp.sum(dn_ * xh, axis=0, keepdims=True)

        @pl.when(i == 0)
        def _():
            gw_ref[...] = gw

        @pl.when(i > 0)
        def _():
            gw_ref[...] += gw

    row = pl.BlockSpec((tr, D), lambda i: (i, 0))
    vec = pl.BlockSpec((1, D), lambda i: (0, 0))
    return pl.pallas_call(
        body, name="norm_bwd", grid=(Lp // tr,), in_specs=[row, row, row, vec], out_specs=[row, row, vec],
        out_shape=[jax.ShapeDtypeStruct((Lp, D), f32), jax.ShapeDtypeStruct((Lp, D), bf16),
                   jax.ShapeDtypeStruct((1, D), f32)],
        compiler_params=_params("arbitrary"),
    )(dn, h, dh, w)


def _embed_norm_bwd(dn, h, dh, w, S, side=None):
    Lp, D = h.shape
    tr = _tile(S, 256, OFF)

    def body(dn_ref, h_ref, dh_ref, dn0_ref, h0_ref, dh0_ref, w_ref, gx_ref, gm_ref, gw_ref):
        i = pl.program_id(0)

        def rows(dn_, h_, dh_):
            xh, r = _rms(h_)
            return dh_ + _rms_bwd(dn_, xh, r, w_ref[...]), jnp.sum(dn_ * xh, axis=0, keepdims=True)

        d, gw = rows(dn_ref[...], h_ref[...], dh_ref[...])
        gx_ref[...] = d

        @pl.when(i == 0)
        def _():
            d0, gw0 = rows(dn0_ref[...], h0_ref[...], dh0_ref[...])
            gm_ref[...] = d0[PAD:OFF, :]
            gw_ref[...] = gw0 + gw

        @pl.when(i > 0)
        def _():
            gw_ref[...] += gw

    win = pl.BlockSpec((pl.Element(tr), pl.Element(D)), lambda i: (pl.multiple_of(OFF + i * tr, OFF), 0))
    head = pl.BlockSpec((OFF, D), lambda i: (0, 0))
    vec = pl.BlockSpec((1, D), lambda i: (0, 0))
    out = _call(
        body, name="embed_norm_bwd", grid=(S // tr,), in_specs=[win, win, win, head, head, head, vec],
        out_specs=[pl.BlockSpec((None, tr, D), lambda i: (0, i, 0)),
                   pl.BlockSpec((N_META, D), lambda i: (0, 0)), vec],
        out_shape=[jax.ShapeDtypeStruct((1, S, D), f32), jax.ShapeDtypeStruct((N_META, D), f32),
                   jax.ShapeDtypeStruct((1, D), f32)],
        args=[dn, h, dh, dn, h, dh, w], sem=("arbitrary",), side=side)
    return out[0], out[1], out[2], out[3:]


def _final(h1, ffn, tgt3, w):
    Lp, D = h1.shape
    _, S, _ = tgt3.shape
    tr = _tile(Lp, min(256, S), OFF)

    def body(h_ref, f_ref, t_ref, w_ref, d_ref, db_ref, l_ref, gw_ref):
        i = pl.program_id(0)
        h2 = h_ref[...] + f_ref[...]
        xh, r = _rms(h2)
        w_ = w_ref[...]
        t = t_ref[...]
        t = jnp.where(i == 0, pltpu.roll(t, OFF, 0), t)
        valid = (lax.broadcasted_iota(jnp.int32, (tr, 1), 0) + i * tr >= OFF).astype(f32)
        e = xh * w_ - t
        loss = 0.5 * jnp.sum(jnp.mean(e * e, axis=-1, keepdims=True) * valid, axis=0, keepdims=True)
        dy = e * (valid / D)
        d = _rms_bwd(dy, xh, r, w_)
        d_ref[...] = d
        db_ref[...] = d.astype(bf16)
        gw = jnp.sum(dy * xh, axis=0, keepdims=True)

        @pl.when(i == 0)
        def _():
            l_ref[...] = jnp.zeros_like(l_ref)
            gw_ref[...] = jnp.zeros_like(gw_ref)

        l_ref[...] += jnp.broadcast_to(loss, l_ref.shape)
        gw_ref[...] += gw

    row = pl.BlockSpec((tr, D), lambda i: (i, 0))
    vec = pl.BlockSpec((1, D), lambda i: (0, 0))
    tgt = pl.BlockSpec((pl.Element(tr), pl.Element(D)),
                       lambda i: (pl.multiple_of(jnp.maximum(i * tr - OFF, 0), OFF), 0))
    return pl.pallas_call(
        body, name="final_loss", grid=(Lp // tr,), in_specs=[row, row, tgt, vec],
        out_specs=[row, row, pl.BlockSpec((8, 128), lambda i: (0, 0)), vec],
        out_shape=[jax.ShapeDtypeStruct((Lp, D), f32), jax.ShapeDtypeStruct((Lp, D), bf16),
                   jax.ShapeDtypeStruct((8, 128), f32), jax.ShapeDtypeStruct((1, D), f32)],
        compiler_params=_params("arbitrary"),
    )(h1, ffn, tgt3.reshape(S, D), w)


def _ffn_in(n, w_gate, w_up, side=None):
    M, K = n.shape
    F = w_gate.shape[1]
    tm = _tile(M, 1408)
    tn = _tile(F, 512, 128)

    def body(a_ref, bg_ref, bu_ref, act_ref, pg_ref, pu_ref):
        a = a_ref[...]
        g = _mmb(a, bg_ref[...], NN)
        u = _mmb(a, bu_ref[...], NN)
        s = _sigmoid(g)
        gs = g * s
        act_ref[...] = (gs * u).astype(bf16)
        pg_ref[...] = (u * (s + gs * (1.0 - s))).astype(bf16)
        pu_ref[...] = gs.astype(bf16)

    wsp = pl.BlockSpec((K, tn), lambda i, j: (0, j))
    osp = pl.BlockSpec((tm, tn), lambda i, j: (i, j))
    out = _call(body, name="ffn_in", grid=(M // tm, F // tn),
                in_specs=[pl.BlockSpec((tm, K), lambda i, j: (i, 0)), wsp, wsp], out_specs=[osp] * 3,
                out_shape=[jax.ShapeDtypeStruct((M, F), bf16)] * 3, args=[n, w_gate, w_up],
                sem=("parallel", "parallel"), side=side)
    return out[0], out[1], out[2], out[3:]


def _ffn_dact(d, w_down, pg, pu):
    M, K = d.shape
    F = w_down.shape[0]
    tm = _tile(M, 1408)
    tn = _tile(F, 512, 128)

    def body(d_ref, w_ref, pg_ref, pu_ref, dg_ref, du_ref):
        da = _mmb(d_ref[...], w_ref[...], NT)
        dg_ref[...] = (da * pg_ref[...].astype(f32)).astype(bf16)
        du_ref[...] = (da * pu_ref[...].astype(f32)).astype(bf16)

    osp = pl.BlockSpec((tm, tn), lambda i, j: (i, j))
    return pl.pallas_call(
        body, name="ffn_dact", grid=(M // tm, F // tn),
        in_specs=[pl.BlockSpec((tm, K), lambda i, j: (i, 0)), pl.BlockSpec((tn, K), lambda i, j: (j, 0)), osp, osp],
        out_specs=[osp, osp], out_shape=[jax.ShapeDtypeStruct((M, F), bf16)] * 2,
        compiler_params=_params("parallel", "parallel"),
    )(d, w_down, pg, pu)


def _ffn_dn(dg, du, w_gate, w_up):
    M, F = dg.shape
    D = w_gate.shape[0]
    tm = _tile(M, 688)
    tn = _tile(D, 256, 128)

    def body(dg_ref, du_ref, wg_ref, wu_ref, o_ref):
        o_ref[...] = _mmb(dg_ref[...], wg_ref[...], NT) + _mmb(du_ref[...], wu_ref[...], NT)

    asp = pl.BlockSpec((tm, F), lambda i, j: (i, 0))
    wsp = pl.BlockSpec((tn, F), lambda i, j: (j, 0))
    return pl.pallas_call(
        body, name="d_n2", grid=(M // tm, D // tn), in_specs=[asp, asp, wsp, wsp],
        out_specs=pl.BlockSpec((tm, tn), lambda i, j: (i, j)), out_shape=jax.ShapeDtypeStruct((M, D), f32),
        compiler_params=_params("parallel", "parallel"),
    )(dg, du, w_gate, w_up)


def _gates(psm, w2p, gate_b, alog, dtb):
    Lp = psm.shape[0]
    tr = _tile(Lp, 256)

    def body(p_ref, w_ref, b_ref, a_ref, t_ref, gb_ref, la_ref):
        i = pl.program_id(0)
        psm_ = p_ref[...]
        lane = lax.broadcasted_iota(jnp.int32, psm_.shape, 1)
        rowi = lax.broadcasted_iota(jnp.int32, (tr, 1), 0) + i * tr
        g = -jnp.exp(a_ref[...]) * _softplus(psm_ + t_ref[...])
        beta = _sigmoid(psm_)
        gb = jnp.where(lane < GDN_H, g, jnp.where(lane < 2 * GDN_H, beta, 0.0))
        gb_ref[...] = gb * (rowi >= PAD).astype(f32)
        logit = _mmb(psm_, w_ref[...], NN) + b_ref[...]
        la_ref[...] = _log_sigmoid(logit) * (1.0 / GATE_NORMALIZER)

    row = pl.BlockSpec((tr, SM_W), lambda i: (i, 0))
    return pl.pallas_call(
        body, name="gates", grid=(Lp // tr,),
        in_specs=[row, pl.BlockSpec((SM_W, GLA_QK), lambda i: (0, 0)), pl.BlockSpec((1, GLA_QK), lambda i: (0, 0)),
                  pl.BlockSpec((1, SM_W), lambda i: (0, 0)), pl.BlockSpec((1, SM_W), lambda i: (0, 0))],
        out_specs=[row, pl.BlockSpec((tr, GLA_QK), lambda i: (i, 0))],
        out_shape=[jax.ShapeDtypeStruct((Lp, SM_W), f32), jax.ShapeDtypeStruct((Lp, GLA_QK), f32)],
        compiler_params=_params("parallel"),
    )(psm, w2p, gate_b, alog, dtb)


def _gates_bwd(psm, w2p, gate_b, alog, dtb, dgb, dla):
    Lp = psm.shape[0]
    tr = _tile(Lp, 256)

    def body(p_ref, w_ref, b_ref, a_ref, t_ref, dgb_ref, dla_ref, dp_ref, gw_ref, gb_ref, ga_ref, gt_ref):
        i = pl.program_id(0)
        psm_ = p_ref[...]
        lane = lax.broadcasted_iota(jnp.int32, psm_.shape, 1)
        rowi = lax.broadcasted_iota(jnp.int32, (tr, 1), 0) + i * tr
        d = dgb_ref[...] * (rowi >= PAD).astype(f32)
        ea = jnp.exp(a_ref[...])
        z = psm_ + t_ref[...]
        is_g = lane < GDN_H
        dz = jnp.where(is_g, -ea * _sigmoid(z) * d, 0.0)
        dalog = jnp.where(is_g, -ea * _softplus(z) * d, 0.0)
        beta = _sigmoid(psm_)
        dbeta = jnp.where(jnp.logical_and(lane >= GDN_H, lane < 2 * GDN_H), beta * (1.0 - beta) * d, 0.0)
        logit = _mmb(psm_, w_ref[...], NN) + b_ref[...]
        dlogit = dla_ref[...] * (_sigmoid(-logit) * (1.0 / GATE_NORMALIZER))
        dlr = _mmb(dlogit, w_ref[...], NT)
        dp_ref[...] = (dz + dbeta + dlr).astype(bf16)
        gw = _mmb(psm_, dlogit, TN)
        gb = jnp.sum(dlogit, axis=0, keepdims=True)
        ga = jnp.sum(dalog, axis=0, keepdims=True)
        gt = jnp.sum(dz, axis=0, keepdims=True)

        @pl.when(i == 0)
        def _():
            gw_ref[...] = gw
            gb_ref[...] = gb
            ga_ref[...] = ga
            gt_ref[...] = gt

        @pl.when(i > 0)
        def _():
            gw_ref[...] += gw
            gb_ref[...] += gb
            ga_ref[...] += ga
            gt_ref[...] += gt

    row = pl.BlockSpec((tr, SM_W), lambda i: (i, 0))
    wsp = pl.BlockSpec((SM_W, GLA_QK), lambda i: (0, 0))
    bsp = pl.BlockSpec((1, GLA_QK), lambda i: (0, 0))
    vsp = pl.BlockSpec((1, SM_W), lambda i: (0, 0))
    return pl.pallas_call(
        body, name="gates_bwd", grid=(Lp // tr,),
        in_specs=[row, wsp, bsp, vsp, vsp, row, pl.BlockSpec((tr, GLA_QK), lambda i: (i, 0))],
        out_specs=[row, wsp, bsp, vsp, vsp],
        out_shape=[jax.ShapeDtypeStruct((Lp, SM_W), bf16), jax.ShapeDtypeStruct((SM_W, GLA_QK), f32),
                   jax.ShapeDtypeStruct((1, GLA_QK), f32), jax.ShapeDtypeStruct((1, SM_W), f32),
                   jax.ShapeDtypeStruct((1, SM_W), f32)],
        compiler_params=_params("arbitrary"),
    )(psm, w2p, gate_b, alog, dtb, dgb, dla)


def _conv_pre(x_ext, w, n):
    rows = x_ext.shape[0]
    y = x_ext * w[CONV_K - 1:CONV_K, :]
    for s in range(1, CONV_K):
        y = y + pltpu.roll(x_ext, s, 0) * w[CONV_K - 1 - s:CONV_K - s, :]
    return y[rows - n:, :]


def _conv(proj, cw, side=None):
    Lp = proj.shape[0]
    W = cw.shape[1]
    tr = _tile(Lp, 256, 64)
    tc = _tile(W, 1024, 128)
    c0 = C_QKV // tc

    def body(h_ref, x_ref, w_ref, o_ref):
        i = pl.program_id(1)
        halo = jnp.where(i == 0, 0.0, h_ref[...])
        x_ext = jnp.concatenate([halo, x_ref[...]], axis=0)
        o_ref[...] = _silu(_conv_pre(x_ext, w_ref[...], tr))

    out = _call(
        body, name="conv", grid=(W // tc, Lp // tr),
        in_specs=[pl.BlockSpec((8, tc), lambda j, i: (jnp.maximum(i * (tr // 8) - 1, 0), j + c0)),
                  pl.BlockSpec((tr, tc), lambda j, i: (i, j + c0)),
                  pl.BlockSpec((CONV_K, tc), lambda j, i: (0, j))],
        out_specs=[pl.BlockSpec((tr, tc), lambda j, i: (i, j))],
        out_shape=[jax.ShapeDtypeStruct((Lp, W), f32)], args=[proj, proj, cw],
        sem=("parallel", "parallel"), side=side)
    return out[0] if side is None else (out[0], out[1:])


def _conv_bwd(proj, cw, dy, dproj, side=None):
    Lp = proj.shape[0]
    W = cw.shape[1]
    tr = _tile(Lp, 256, 64)
    tc = _tile(W, 1024, 128)
    c0 = C_QKV // tc
    nr = Lp // tr
    last8 = Lp // 8 - 1

    def body(xp_ref, x_ref, xn_ref, w_ref, d_ref, dn_ref, dproj_ref, o_ref, gw_ref):
        del dproj_ref
        i = pl.program_id(1)
        w = w_ref[...]
        xp = jnp.where(i == 0, 0.0, xp_ref[...])
        x_ext = jnp.concatenate([xp, x_ref[...], xn_ref[...]], axis=0)
        n = tr + 8
        pre = _conv_pre(x_ext, w, n)
        dn = jnp.where(i == nr - 1, 0.0, dn_ref[...])
        dpre = jnp.concatenate([d_ref[...], dn], axis=0) * _dsilu(pre)
        dx = dpre * w[CONV_K - 1:CONV_K, :]
        for s in range(1, CONV_K):
            dx = dx + pltpu.roll(dpre, n - s, 0) * w[CONV_K - 1 - s:CONV_K - s, :]
        o_ref[...] = dx[:tr, :].astype(bf16)
        dp = dpre[:tr, :]
        rows = []
        for k in range(CONV_K):
            xs = x_ext if k == CONV_K - 1 else pltpu.roll(x_ext, CONV_K - 1 - k, 0)
            rows.append(jnp.sum(dp * xs[8:8 + tr, :], axis=0, keepdims=True))
        gw = jnp.concatenate(rows, axis=0)

        @pl.when(i == 0)
        def _():
            gw_ref[...] = gw

        @pl.when(i > 0)
        def _():
            gw_ref[...] += gw

    cur = pl.BlockSpec((tr, tc), lambda j, i: (i, j))
    nxt = pl.BlockSpec((8, tc), lambda j, i: (jnp.minimum((i + 1) * (tr // 8), last8), j))
    pcur = pl.BlockSpec((tr, tc), lambda j, i: (i, j + c0))
    pprev = pl.BlockSpec((8, tc), lambda j, i: (jnp.maximum(i * (tr // 8) - 1, 0), j + c0))
    pnext = pl.BlockSpec((8, tc), lambda j, i: (jnp.minimum((i + 1) * (tr // 8), last8), j + c0))
    wsp = pl.BlockSpec((CONV_K, tc), lambda j, i: (0, j))
    out = _call(
        body, name="conv_bwd", grid=(W // tc, nr),
        in_specs=[pprev, pcur, pnext, wsp, cur, nxt, _ANY], out_specs=[pcur, wsp],
        out_shape=[jax.ShapeDtypeStruct(dproj.shape, dproj.dtype), jax.ShapeDtypeStruct((CONV_K, W), f32)],
        aliases={6: 0}, args=[proj, proj, proj, cw, dy, dy, dproj], sem=("parallel", "arbitrary"), side=side)
    return out[0], out[1], out[2:]


GDN_FWD_GROUP = 3


def _gdn_group(Lp, most):
    n = Lp // GDN_C
    return next(g for g in range(most, 0, -1) if n % g == 0)


def _gdn_heads(x_ref, gb_ref, group):
    qs, ks, vs, bs, gs = [], [], [], [], []
    for chunk in range(group):
        r = slice(chunk * GDN_C, (chunk + 1) * GDN_C)
        gbv = gb_ref[r, :]
        for h in range(GDN_H):
            qs.append(x_ref[r, Q0 + h * GDN_DK:Q0 + (h + 1) * GDN_DK])
            ks.append(x_ref[r, K0 + h * GDN_DK:K0 + (h + 1) * GDN_DK])
            vs.append(x_ref[r, V0 + h * GDN_DV:V0 + (h + 1) * GDN_DV])
            bs.append(gbv[:, GDN_H + h:GDN_H + h + 1])
            gs.append(gbv[:, h:h + 1])
    return qs, ks, vs, bs, gs


def _gdn_fwd(qkvc, gb, side=None):
    Lp = qkvc.shape[0]
    group = _gdn_group(Lp, GDN_FWD_GROUP)
    rows = group * GDN_C
    steps = Lp // rows
    R = range(GDN_H)

    def body(x_ref, gb_ref, o_ref, sall_ref, pall_ref, s_scr):
        @pl.when(pl.program_id(0) == 0)
        def _():
            s_scr[...] = jnp.zeros_like(s_scr)

        S2, o, p, entering = _gdn_chunk([s_scr[h] for h in R], *_gdn_heads(x_ref, gb_ref, group))
        for h in R:
            s_scr[h] = S2[h]
        for chunk in range(group):
            for h in R:
                i = chunk * GDN_H + h
                o_ref[chunk * GDN_C:(chunk + 1) * GDN_C, h * GDN_DV:(h + 1) * GDN_DV] = o[i]
                pall_ref[chunk, h] = p[i]
                sall_ref[chunk, h] = entering[i]

    out = _call(
        body, name="gdn_fwd", grid=(steps,),
        in_specs=[pl.BlockSpec((rows, QKV_W), lambda n: (n, 0)), pl.BlockSpec((rows, SM_W), lambda n: (n, 0))],
        out_specs=[pl.BlockSpec((rows, GDN_V), lambda n: (n, 0)),
                   pl.BlockSpec((group, GDN_H, GDN_DK, GDN_DV), lambda n: (n, 0, 0, 0)),
                   pl.BlockSpec((group, GDN_H, GDN_C, GDN_C), lambda n: (n, 0, 0, 0))],
        out_shape=[jax.ShapeDtypeStruct((Lp, GDN_V), f32),
                   jax.ShapeDtypeStruct((Lp // GDN_C, GDN_H, GDN_DK, GDN_DV), f32),
                   jax.ShapeDtypeStruct((Lp // GDN_C, GDN_H, GDN_C, GDN_C), f32)],
        scratch_shapes=[pltpu.VMEM((GDN_H, GDN_DK, GDN_DV), f32)], args=[qkvc, gb], sem=("arbitrary",), side=side)
    return out[0], out[1], out[2], out[3:]


def _gdn_bwd(qkvc, gb, sall, pall, do, side=None):
    Lp = qkvc.shape[0]
    group = 1
    rows = group * GDN_C
    steps = Lp // rows
    R = range(GDN_H)

    def body(x_ref, gb_ref, sall_ref, pall_ref, do_ref, dx_ref, dgb_ref, ds_scr):
        @pl.when(pl.program_id(0) == 0)
        def _():
            ds_scr[...] = jnp.zeros_like(ds_scr)

        lane = lax.broadcasted_iota(jnp.int32, (GDN_C, SM_W), 1)
        ps = [pall_ref[chunk, h] for chunk in range(group) for h in R]
        _, vjp = jax.vjp(lambda *a: _gdn_chunk(*a, Ps=ps)[:2],
                         [sall_ref[0, h] for h in R], *_gdn_heads(x_ref, gb_ref, group))
        do = [do_ref[chunk * GDN_C:(chunk + 1) * GDN_C, h * GDN_DV:(h + 1) * GDN_DV]
              for chunk in range(group) for h in R]
        dS, dq, dk, dv, dbeta, dg = vjp(([ds_scr[h] for h in R], do))
        for h in R:
            ds_scr[h] = dS[h]
        for chunk in range(group):
            r = slice(chunk * GDN_C, (chunk + 1) * GDN_C)
            acc = jnp.zeros((GDN_C, SM_W), f32)
            for h in R:
                i = chunk * GDN_H + h
                dx_ref[r, Q0 + h * GDN_DK:Q0 + (h + 1) * GDN_DK] = dq[i]
                dx_ref[r, K0 + h * GDN_DK:K0 + (h + 1) * GDN_DK] = dk[i]
                dx_ref[r, V0 + h * GDN_DV:V0 + (h + 1) * GDN_DV] = dv[i]
                acc = acc + jnp.where(lane == h, dg[i], 0.0) + jnp.where(lane == GDN_H + h, dbeta[i], 0.0)
            dgb_ref[r, :] = acc

    rev = lambda n: (steps - 1 - n, 0)
    out = _call(
        body, name="gdn_bwd", grid=(steps,),
        in_specs=[pl.BlockSpec((rows, QKV_W), rev), pl.BlockSpec((rows, SM_W), rev),
                  pl.BlockSpec((1, GDN_H, GDN_DK, GDN_DV), lambda n: (steps - 1 - n, 0, 0, 0)),
                  pl.BlockSpec((group, GDN_H, GDN_C, GDN_C), lambda n: (steps - 1 - n, 0, 0, 0)),
                  pl.BlockSpec((rows, GDN_V), rev)],
        out_specs=[pl.BlockSpec((rows, QKV_W), rev), pl.BlockSpec((rows, SM_W), rev)],
        out_shape=[jax.ShapeDtypeStruct((Lp, QKV_W), f32), jax.ShapeDtypeStruct((Lp, SM_W), f32)],
        scratch_shapes=[pltpu.VMEM((GDN_H, GDN_DK, GDN_DV), f32)], args=[qkvc, gb, sall, pall, do],
        sem=("arbitrary",), side=side)
    return out[0], out[1], out[2:]


GLA_BLOCK = 64


def _gla_group(Lp):
    nb = Lp // GLA_BLOCK
    return next(g for g in (3, 2, 1) if nb % g == 0)


def _gla_slices(h):
    sq = slice(h * GLA_DK, (h + 1) * GLA_DK)
    sk = slice(GLA_QK + h * GLA_DK, GLA_QK + (h + 1) * GLA_DK)
    sv = slice(2 * GLA_QK + h * GLA_DV, 2 * GLA_QK + (h + 1) * GLA_DV)
    return sq, sk, sv


def _gla_heads(x_ref, la_ref, group):
    qs, ks, vs, ls = [], [], [], []
    for blk in range(group):
        r = slice(blk * GLA_BLOCK, (blk + 1) * GLA_BLOCK)
        for h in range(GLA_H):
            sq, sk, sv = _gla_slices(h)
            qs.append(x_ref[r, sq])
            ks.append(x_ref[r, sk])
            vs.append(x_ref[r, sv])
            ls.append(la_ref[r, sq])
    return qs, ks, vs, ls


def _gla_fwd(proj, la):
    Lp = proj.shape[0]
    group = _gla_group(Lp)
    rows = group * GLA_BLOCK
    steps = Lp // rows
    R = range(GLA_H)

    def body(x_ref, la_ref, o_ref, sall_ref, s_scr):
        @pl.when(pl.program_id(0) == 0)
        def _():
            s_scr[...] = jnp.zeros_like(s_scr)

        Sts = [s_scr[h] for h in R]
        for h in R:
            sall_ref[0, h] = Sts[h]
        St2, o = _gla_blocks(Sts, *_gla_heads(x_ref, la_ref, group))
        for h in R:
            s_scr[h] = St2[h]
        for blk in range(group):
            for h in R:
                o_ref[blk * GLA_BLOCK:(blk + 1) * GLA_BLOCK, h * GLA_DV:(h + 1) * GLA_DV] = o[blk * GLA_H + h]

    return pl.pallas_call(
        body, name="gla_fwd", grid=(steps,),
        in_specs=[pl.BlockSpec((rows, G_W), lambda n: (n, C_G // G_W)),
                  pl.BlockSpec((rows, GLA_QK), lambda n: (n, 0))],
        out_specs=[pl.BlockSpec((rows, GLA_V), lambda n: (n, 0)),
                   pl.BlockSpec((1, GLA_H, GLA_DV, GLA_DK), lambda n: (n, 0, 0, 0))],
        out_shape=[jax.ShapeDtypeStruct((Lp, GLA_V), f32),
                   jax.ShapeDtypeStruct((steps, GLA_H, GLA_DV, GLA_DK), f32)],
        scratch_shapes=[pltpu.VMEM((GLA_H, GLA_DV, GLA_DK), f32)],
        compiler_params=_params("arbitrary"),
    )(proj, la)


def _gla_bwd(proj, la, sall, do, dproj, side=None):
    Lp = proj.shape[0]
    group = _gla_group(Lp)
    rows = group * GLA_BLOCK
    steps = Lp // rows
    R = range(GLA_H)

    def body(x_ref, la_ref, sall_ref, do_ref, dproj_ref, dx_ref, dla_ref, ds_scr):
        del dproj_ref

        @pl.when(pl.program_id(0) == 0)
        def _():
            ds_scr[...] = jnp.zeros_like(ds_scr)

        _, vjp = jax.vjp(_gla_blocks, [sall_ref[0, h] for h in R], *_gla_heads(x_ref, la_ref, group))
        do = [do_ref[blk * GLA_BLOCK:(blk + 1) * GLA_BLOCK, h * GLA_DV:(h + 1) * GLA_DV]
              for blk in range(group) for h in R]
        dS, dq, dk, dv, dl = vjp(([ds_scr[h] for h in R], do))
        for h in R:
            ds_scr[h] = dS[h]
        for blk in range(group):
            r = slice(blk * GLA_BLOCK, (blk + 1) * GLA_BLOCK)
            for h in R:
                sq, sk, sv = _gla_slices(h)
                i = blk * GLA_H + h
                dx_ref[r, sq] = dq[i].astype(bf16)
                dx_ref[r, sk] = dk[i].astype(bf16)
                dx_ref[r, sv] = dv[i].astype(bf16)
                dla_ref[r, sq] = dl[i]

    x_spec = pl.BlockSpec((rows, G_W), lambda n: (steps - 1 - n, C_G // G_W))
    rev = lambda n: (steps - 1 - n, 0)
    out = _call(
        body, name="gla_bwd", grid=(steps,),
        in_specs=[x_spec, pl.BlockSpec((rows, GLA_QK), rev),
                  pl.BlockSpec((1, GLA_H, GLA_DV, GLA_DK), lambda n: (steps - 1 - n, 0, 0, 0)),
                  pl.BlockSpec((rows, GLA_V), rev), _ANY],
        out_specs=[x_spec, pl.BlockSpec((rows, GLA_QK), rev)],
        out_shape=[jax.ShapeDtypeStruct(dproj.shape, dproj.dtype), jax.ShapeDtypeStruct((Lp, GLA_QK), f32)],
        aliases={4: 0}, scratch_shapes=[pltpu.VMEM((GLA_H, GLA_DV, GLA_DK), f32)],
        args=[proj, la, sall, do, dproj], sem=("arbitrary",), side=side)
    return out[0], out[1], out[2:]


def _gated_norm_fn(og, ol, zr, wg, wl):
    outs = []
    for h in range(GDN_H):
        s = slice(h * GDN_DV, (h + 1) * GDN_DV)
        outs.append(_rms(og[:, s])[0] * wg * _silu(zr[:, s]))
    for h in range(GLA_H):
        s = slice(h * GLA_DV, (h + 1) * GLA_DV)
        sr = slice(GDN_V + h * GLA_DV, GDN_V + (h + 1) * GLA_DV)
        outs.append(_rms(ol[:, s])[0] * wl * _silu(zr[:, sr]))
    return jnp.concatenate(outs, axis=-1)


def _gated_norm(og, ol, proj, wg, wl):
    Lp = og.shape[0]
    tr = _tile(Lp, 256)

    def body(og_ref, ol_ref, zr_ref, wg_ref, wl_ref, o_ref):
        o_ref[...] = _gated_norm_fn(og_ref[...], ol_ref[...], zr_ref[...], wg_ref[...], wl_ref[...]).astype(bf16)

    return pl.pallas_call(
        body, name="gated_norm", grid=(Lp // tr,),
        in_specs=[pl.BlockSpec((tr, GDN_V), lambda i: (i, 0)), pl.BlockSpec((tr, GLA_V), lambda i: (i, 0)),
                  pl.BlockSpec((tr, ZR_W), lambda i: (i, C_ZR // ZR_W)),
                  pl.BlockSpec((1, GDN_DV), lambda i: (0, 0)), pl.BlockSpec((1, GLA_DV), lambda i: (0, 0))],
        out_specs=pl.BlockSpec((tr, ZR_W), lambda i: (i, 0)),
        out_shape=jax.ShapeDtypeStruct((Lp, ZR_W), bf16),
        compiler_params=_params("parallel"),
    )(og, ol, proj, wg, wl)


def _gated_norm_bwd(og, ol, proj, wg, wl, dmix):
    Lp = og.shape[0]
    tr = _tile(Lp, 128)

    def body(og_ref, ol_ref, zr_ref, wg_ref, wl_ref, d_ref, dog_ref, dol_ref, dzr_ref, gwg_ref, gwl_ref):
        i = pl.program_id(0)
        _, vjp = jax.vjp(_gated_norm_fn, og_ref[...], ol_ref[...], zr_ref[...], wg_ref[...], wl_ref[...])
        dog, dol, dzr, gwg, gwl = vjp(d_ref[...])
        dog_ref[...] = dog
        dol_ref[...] = dol
        dzr_ref[...] = dzr.astype(bf16)

        @pl.when(i == 0)
        def _():
            gwg_ref[...] = gwg
            gwl_ref[...] = gwl

        @pl.when(i > 0)
        def _():
            gwg_ref[...] += gwg
            gwl_ref[...] += gwl

    og_spec = pl.BlockSpec((tr, GDN_V), lambda i: (i, 0))
    ol_spec = pl.BlockSpec((tr, GLA_V), lambda i: (i, 0))
    zr_spec = pl.BlockSpec((tr, ZR_W), lambda i: (i, C_ZR // ZR_W))
    vg = pl.BlockSpec((1, GDN_DV), lambda i: (0, 0))
    vl = pl.BlockSpec((1, GLA_DV), lambda i: (0, 0))
    return pl.pallas_call(
        body, name="gated_norm_bwd", grid=(Lp // tr,),
        in_specs=[og_spec, ol_spec, zr_spec, vg, vl, pl.BlockSpec((tr, ZR_W), lambda i: (i, 0))],
        out_specs=[og_spec, ol_spec, zr_spec, vg, vl],
        out_shape=[jax.ShapeDtypeStruct((Lp, GDN_V), f32), jax.ShapeDtypeStruct((Lp, GLA_V), f32),
                   jax.ShapeDtypeStruct((Lp, C_END), bf16),
                   jax.ShapeDtypeStruct((1, GDN_DV), f32), jax.ShapeDtypeStruct((1, GLA_DV), f32)],
        compiler_params=_params("arbitrary"),
    )(og, ol, proj, wg, wl, dmix)


def _adamw_rule(g_, w_, m_, v_):
    c1 = 1.0 - ADAM_B1 ** ADAM_STEP
    c2 = 1.0 - ADAM_B2 ** ADAM_STEP
    m2 = ADAM_B1 * m_ + (1.0 - ADAM_B1) * g_
    v2 = ADAM_B2 * v_ + (1.0 - ADAM_B2) * (g_ * g_)
    return -ADAM_LR * ((m2 / c1) / (jnp.sqrt(v2 / c2) + ADAM_EPS) + ADAM_WD * w_), m2, v2


def _adamw(g, w, m, v, name):
    R, C = g.shape
    tr = _tile(R, 256, 8) if R % 8 == 0 and R > 256 else R

    def body(g_ref, w_ref, m_ref, v_ref, d_ref, mo_ref, vo_ref):
        d_ref[...], mo_ref[...], vo_ref[...] = _adamw_rule(g_ref[...], w_ref[...], m_ref[...], v_ref[...])

    blk = pl.BlockSpec((tr, C), lambda i: (i, 0))
    return pl.pallas_call(
        body, name=name, grid=(R // tr,), in_specs=[blk] * 4, out_specs=[blk] * 3,
        out_shape=[jax.ShapeDtypeStruct((R, C), f32)] * 3,
        compiler_params=_params("parallel"),
    )(g, w, m, v)


def _adamw_transposed(gt, w, m, v, name):
    n, C, rb = gt.shape
    assert w.shape == (C, n * rb), (gt.shape, w.shape)

    def body(g_ref, w_ref, m_ref, v_ref, go_ref, d_ref, mo_ref, vo_ref):
        g_ = g_ref[...].astype(f32)
        go_ref[...] = g_
        d_ref[...], mo_ref[...], vo_ref[...] = _adamw_rule(g_, w_ref[...], m_ref[...], v_ref[...])

    blk = pl.BlockSpec((C, rb), lambda j: (0, j))
    return pl.pallas_call(
        body, name=name, grid=(n,), in_specs=[pl.BlockSpec((None, C, rb), lambda j: (j, 0, 0))] + [blk] * 3,
        out_specs=[blk] * 4, out_shape=[jax.ShapeDtypeStruct((C, n * rb), f32)] * 4,
        compiler_params=_params("parallel"),
    )(gt, w, m, v)


def _sum_slots(r, name):
    n, R, C = r.shape
    tr = _tile(R, 128, 16) if R % 16 == 0 and R > 128 else R

    def body(r_ref, o_ref):
        acc = r_ref[0].astype(f32)
        for s in range(1, n):
            acc = acc + r_ref[s].astype(f32)
        o_ref[...] = acc

    return pl.pallas_call(
        body, name=name, grid=(R // tr,),
        in_specs=[pl.BlockSpec((n, tr, C), lambda i: (0, i, 0))],
        out_specs=pl.BlockSpec((tr, C), lambda i: (i, 0)),
        out_shape=jax.ShapeDtypeStruct((R, C), f32),
        compiler_params=_params("parallel"),
    )(r)


SIBLING_PARTS = 8


class _Siblings:
    def __init__(self, arrays):
        self.arrays = list(arrays)
        self.n = len(self.arrays)
        self.parts = [next(p for p in range(SIBLING_PARTS, 0, -1) if a.shape[0] % (8 * p) == 0 or p == 1)
                      for a in self.arrays]
        total = sum(self.parts)
        self.out_shape = [jax.ShapeDtypeStruct((2,) + a.shape, a.dtype) for a in self.arrays]
        self.sems = [pltpu.SemaphoreType.DMA((total,)), pltpu.SemaphoreType.DMA((total,)),
                     pltpu.SemaphoreType.DMA((self.n,))]

    def hooks(self, ins, outs, send, recv, lsem):
        def copies():
            x, y, c = lax.axis_index("x"), lax.axis_index("y"), lax.axis_index("c")
            out, k = [], 0
            for a in range(self.n):
                out.append((pltpu.make_async_copy(ins[a], outs[a].at[c], lsem.at[a]), None))
                rows = self.arrays[a].shape[0] // self.parts[a]
                for part in range(self.parts[a]):
                    r = pl.ds(part * rows, rows)
                    mk = lambda dst, a=a, r=r, k=k: pltpu.make_async_remote_copy(
                        src_ref=ins[a].at[r], dst_ref=dst.at[r], send_sem=send.at[k], recv_sem=recv.at[k],
                        device_id=(x, y, 1 - c), device_id_type=MESH)
                    out.append((mk(outs[a].at[c]), mk(outs[a].at[1 - c])))
                    k += 1
            return out

        return _start_wait(copies)


def _comm_now(name, sides):
    total = sum(s.n for s in sides)

    def body(*refs):
        ins, outs, sems = refs[:total], refs[total:2 * total], refs[2 * total:]
        hooks, o = [], 0
        for i, s in enumerate(sides):
            hooks.append(s.hooks(ins[o:o + s.n], outs[o:o + s.n], *sems[3 * i:3 * i + 3]))
            o += s.n
        for start, _ in hooks:
            start()
        for _, wait in hooks:
            wait()

    out = pl.pallas_call(
        body, name=name, in_specs=[_ANY] * total, out_specs=[_ANY] * total,
        out_shape=[sh for s in sides for sh in s.out_shape], scratch_shapes=[sm for s in sides for sm in s.sems],
    )(*[a for s in sides for a in s.arrays])
    res, o = [], 0
    for s in sides:
        res.append(list(out[o:o + s.n]))
        o += s.n
    return res


def _cat_cols(g):
    return jnp.concatenate([g[i] for i in range(N_CHIP)], axis=-1)


def _row_slabs(a):
    return a.reshape(N_DEV, a.shape[0] // N_DEV, a.shape[1])


def _w_in_columns(g_wp, g_wsm):
    return jnp.concatenate([g_wp[:, C_QKV:C_END], g_wp[:, C_ZR:C_ZR + GDN_V], g_wsm[:, :SM_LR],
                            g_wp[:, C_G:C_G + G_W], g_wp[:, C_ZR + GDN_V:C_ZR + ZR_W],
                            g_wsm[:, SM_LR:SM_LR + GATE_RANK]], axis=1)


def _step(x, loss_target, p, meta, shard):
    _, S, D = x.shape
    alog_p = jnp.pad(p["gdn_a_log"], ((0, 0), (0, SM_W - GDN_H)))
    dtb_p = jnp.pad(p["gdn_dt_bias"], ((0, 0), (0, SM_W - GDN_H)))
    m64 = jnp.concatenate([jnp.zeros((PAD, D), f32), meta], axis=0)
    gate_b, gdn_norm_w, gla_norm_w = p["gla_gate_b"], p["gdn_norm_w"], p["gla_norm_w"]
    half = shard["w_up"].shape[0] // 2

    h0, n1, (w_in4, conv4, w24) = _embed_norm(
        x, m64, p["attn_norm_w"], side=_Gather([shard["w_in"], shard["gdn_conv_w"], shard["gla_gate_w2"]]))
    w_in, conv_w, w2 = _cat_cols(w_in4), _cat_cols(conv4), _cat_cols(w24)
    wp = jnp.concatenate([w_in[:, R_Z:R_AB], w_in[:, R_GR:R_LR], w_in[:, R_G:R_GR], w_in[:, R_QKV:R_Z]], axis=1)
    wsm = jnp.concatenate([w_in[:, R_AB:R_G], w_in[:, R_LR:R_END],
                           jnp.zeros((D, SM_W - SM_LR - GATE_RANK), w_in.dtype)], axis=1)
    w2p = jnp.pad(w2, ((SM_LR, SM_W - SM_LR - GATE_RANK), (0, 0)))
    proj, (w_out4, w_up4a) = _mm(n1, wp, "nn", "proj", side=_Gather([shard["w_out"], shard["w_up"][:half]]))
    w_out = w_out4.reshape(-1, D)
    psm = _mm(n1, wsm, "nn", "proj_small")
    gb, la = _gates(psm, w2p, gate_b, alog_p, dtb_p)
    qkvc, (w_up4b,) = _conv(proj, conv_w, side=_Gather([shard["w_up"][half:]]))
    w_up = jnp.concatenate([_cat_cols(w_up4a), _cat_cols(w_up4b)], axis=0)
    og, sall, pall, (w_gate4,) = _gdn_fwd(qkvc, gb, side=_Gather([shard["w_gate"]]))
    w_gate = _cat_cols(w_gate4)
    ol, stall = _gla_fwd(proj, la)
    mixed = _gated_norm(og, ol, proj, gdn_norm_w, gla_norm_w)
    attn = _mm(mixed, w_out, "nn", "out_proj")
    h1, n2 = _add_norm(h0, attn, p["ffn_norm_w"])
    act, act_dgate, act_dup, (w_down4,) = _ffn_in(n2, w_gate, w_up, side=_Gather([shard["w_down"]]))
    w_down = w_down4.reshape(-1, D)
    ffn = _mm(act, w_down, "nn", "ffn_down", **WHOLE_K)
    dh2, dh2b, lossp, g_final = _final(h1, ffn, loss_target, p["final_norm_w"])

    g_down = _mm(act, dh2b, "tn", "g_w_down", out_dtype=bf16, **WHOLE_K_T)
    dg, du = _ffn_dact(dh2b, w_down, act_dgate, act_dup)
    g_gate = _mm(n2, dg, "tn", "g_w_gate", tm_cap=512, tn_cap=1408, tk_cap=2752, out_dtype=bf16, col_slabs=True)
    g_up = _mm(n2, du, "tn", "g_w_up", tm_cap=512, tn_cap=1408, tk_cap=2752, out_dtype=bf16, col_slabs=True)
    dn2 = _ffn_dn(dg, du, w_gate, w_up)
    dh1, dh1b, g_ffn_norm = _norm_bwd(dn2, h1, dh2, p["ffn_norm_w"])
    dmix = _mm(dh1b, w_out, "nt", "d_mixed")
    g_out = _mm(mixed, dh1b, "tn", "g_w_out", out_dtype=bf16, **WHOLE_K_T)
    dog, dol, dproj, g_gdn_norm, g_gla_norm = _gated_norm_bwd(og, ol, proj, gdn_norm_w, gla_norm_w, dmix)
    dproj, dla, (r_down,) = _gla_bwd(proj, la, stall, dol, dproj, side=_Exchange([_row_slabs(g_down)]))
    dqkvc, dgb, (r_gate, r_up, r_out, h_down) = _gdn_bwd(
        qkvc, gb, sall, pall, dog,
        side=_Sides(_Exchange([g_gate, g_up, _row_slabs(g_out)]), _Siblings([_sum_slots(r_down, "sum_w_down")])))
    dproj, g_conv, (h_gate,) = _conv_bwd(proj, conv_w, dqkvc, dproj,
                                         side=_Siblings([_sum_slots(r_gate, "sum_w_gate")]))
    dpsm, g_w2p, g_gate_b, g_alog, g_dtb = _gates_bwd(psm, w2p, gate_b, alog_p, dtb_p, dgb, dla)
    g_wsm = _mm(n1, dpsm, "tn", "g_w_in_small", out_dtype=bf16, **WHOLE_K_T)
    g_wp_a, (h_out,) = _mm(n1, dproj, "tn", "g_w_in_a", out_dtype=bf16, b_cols=(0, W_IN_SPLIT),
                           side=_Siblings([_sum_slots(r_out, "sum_w_out")]), **WHOLE_K_T)
    g_wp_b, r_in_a = _mm(n1, dproj, "tn", "g_w_in_b", out_dtype=bf16, b_cols=(W_IN_SPLIT, C_END - W_IN_SPLIT),
                         side=_Exchange([_row_slabs(g_wp_a), _row_slabs(g_wsm)]), **WHOLE_K_T)
    dn1, (r_in_b, h_up) = _mm(
        dproj, wp, "nt", "d_n1",
        side=_Sides(_Exchange([_row_slabs(g_wp_b)]), _Siblings([_sum_slots(r_up, "sum_w_up")])), **WHOLE_K)
    dn1 = _mm(dpsm, wsm, "nt", "d_n1_small", acc_in=dn1)
    s_wp = jnp.concatenate([_sum_slots(r_in_a[0], "sum_w_in_a"), _sum_slots(r_in_b, "sum_w_in_b")], axis=1)
    s_in = _w_in_columns(s_wp, _sum_slots(r_in_a[1], "sum_w_in_small"))
    in_by_chip = s_in.reshape(s_in.shape[0], N_CHIP, -1).transpose(1, 2, 0).astype(bf16)
    grad_x, g_meta, g_attn_norm, (h_in,) = _embed_norm_bwd(dn1, h0, dh1, p["attn_norm_w"], S,
                                                           side=_Exchange([], by_chip=[in_by_chip]))

    received = dict(w_in=h_in, w_gate=h_gate, w_up=h_up, w_out=h_out, w_down=h_down)
    small = dict(
        meta_tokens=g_meta, attn_norm_w=g_attn_norm, gdn_conv_w=g_conv, gdn_a_log=g_alog[:, :GDN_H],
        gdn_dt_bias=g_dtb[:, :GDN_H], gdn_norm_w=g_gdn_norm, gla_gate_w2=g_w2p[SM_LR:SM_LR + GATE_RANK],
        gla_gate_b=g_gate_b, gla_norm_w=g_gla_norm, ffn_norm_w=g_ffn_norm, final_norm_w=g_final)
    return lossp[0, 0], grad_x, received, small


_WEIGHTS = ("meta_tokens", "attn_norm_w", "w_in", "gdn_conv_w", "gdn_a_log", "gdn_dt_bias", "gdn_norm_w",
            "gla_gate_w2", "gla_gate_b", "gla_norm_w", "w_out", "ffn_norm_w", "w_gate", "w_up", "w_down",
            "final_norm_w")
_BIG_COLS = ("w_in", "w_gate", "w_up")
_BIG_ROWS = ("w_out", "w_down")
_SMALL_SHARDED = ("meta_tokens", "gdn_conv_w", "gla_gate_w2")


def kernel(x, meta_tokens, attn_norm_w, w_in, gdn_conv_w, gdn_a_log, gdn_dt_bias, gdn_norm_w, gla_gate_w2, gla_gate_b, gla_norm_w, w_out, ffn_norm_w, w_gate, w_up, w_down, final_norm_w, loss_target, m_meta_tokens, m_attn_norm_w, m_w_in, m_gdn_conv_w, m_gdn_a_log, m_gdn_dt_bias, m_gdn_norm_w, m_gla_gate_w2, m_gla_gate_b, m_gla_norm_w, m_w_out, m_ffn_norm_w, m_w_gate, m_w_up, m_w_down, m_final_norm_w, v_meta_tokens, v_attn_norm_w, v_w_in, v_gdn_conv_w, v_gdn_a_log, v_gdn_dt_bias, v_gdn_norm_w, v_gla_gate_w2, v_gla_gate_b, v_gla_norm_w, v_w_out, v_ffn_norm_w, v_w_gate, v_w_up, v_w_down, v_final_norm_w):
    w = dict(meta_tokens=meta_tokens, attn_norm_w=attn_norm_w, w_in=w_in, gdn_conv_w=gdn_conv_w, gdn_a_log=gdn_a_log,
             gdn_dt_bias=gdn_dt_bias, gdn_norm_w=gdn_norm_w, gla_gate_w2=gla_gate_w2, gla_gate_b=gla_gate_b,
             gla_norm_w=gla_norm_w, w_out=w_out, ffn_norm_w=ffn_norm_w, w_gate=w_gate, w_up=w_up, w_down=w_down,
             final_norm_w=final_norm_w)
    m = dict(meta_tokens=m_meta_tokens, attn_norm_w=m_attn_norm_w, w_in=m_w_in, gdn_conv_w=m_gdn_conv_w,
             gdn_a_log=m_gdn_a_log, gdn_dt_bias=m_gdn_dt_bias, gdn_norm_w=m_gdn_norm_w, gla_gate_w2=m_gla_gate_w2,
             gla_gate_b=m_gla_gate_b, gla_norm_w=m_gla_norm_w, w_out=m_w_out, ffn_norm_w=m_ffn_norm_w,
             w_gate=m_w_gate, w_up=m_w_up, w_down=m_w_down, final_norm_w=m_final_norm_w)
    v = dict(meta_tokens=v_meta_tokens, attn_norm_w=v_attn_norm_w, w_in=v_w_in, gdn_conv_w=v_gdn_conv_w,
             gdn_a_log=v_gdn_a_log, gdn_dt_bias=v_gdn_dt_bias, gdn_norm_w=v_gdn_norm_w, gla_gate_w2=v_gla_gate_w2,
             gla_gate_b=v_gla_gate_b, gla_norm_w=v_gla_norm_w, w_out=v_w_out, ffn_norm_w=v_ffn_norm_w,
             w_gate=v_w_gate, w_up=v_w_up, w_down=v_w_down, final_norm_w=v_final_norm_w)
    chip = 2 * lax.axis_index("x") + lax.axis_index("y")

    def two_d(a):
        return a.reshape(1, -1) if a.ndim == 1 else a.reshape(-1, a.shape[-1])

    w2d = {k: two_d(a) for k, a in w.items()}
    big = _BIG_COLS + _BIG_ROWS
    small = tuple(k for k in _WEIGHTS if k not in big)

    (meta4,), = _comm_now("gather_meta", [_Gather([w2d["meta_tokens"]])])
    shard = {k: w2d[k].astype(bf16) for k in big}
    shard.update({k: w2d[k] for k in ("gdn_conv_w", "gla_gate_w2")})
    lossp, grad_x, received, g = _step(x, loss_target, {k: w2d[k] for k in small}, _cat_cols(meta4), shard)
    loss = lax.psum(lossp, ("x", "y", "c"))

    sizes = [g[k].size for k in small]
    total = sum(sizes)
    rows = -(-total // 1024)
    rows += (-rows) % 8
    packed = jnp.concatenate([g[k].reshape(-1) for k in small] + [jnp.zeros((rows * 1024 - total,), f32)])
    (packed8,), = _comm_now("exchange_small", [_Exchange([], [packed.reshape(rows, 1024)])])
    red = {k: h.reshape(w2d[k].shape) for k, h in received.items() if k != "w_in"}
    psum_small = _sum_slots(packed8, "sum_small").reshape(-1)
    off = 0
    for k, n in zip(small, sizes):
        a = psum_small[off:off + n].reshape(g[k].shape)
        off += n
        if k in _SMALL_SHARDED:
            c = w2d[k].shape[1]
            a = lax.dynamic_slice_in_dim(a, chip * c, c, axis=1)
        red[k] = a

    grads, deltas, new_m, new_v = [], [], [], []
    for k in _WEIGHTS:
        shape = w[k].shape
        if k == "w_in":
            flip = lambda a: jnp.swapaxes(a, 1, 2).reshape(shape[2], shape[1])
            unflip = lambda a: jnp.swapaxes(a.reshape(shape[0], shape[2], shape[1]), 1, 2)
            out = _adamw_transposed(received[k], flip(w[k]), flip(m[k]), flip(v[k]), "adamw_" + k)
            gk, d, m2, v2 = [unflip(a) for a in out]
        else:
            gk = red[k].reshape(shape)
            d, m2, v2 = [a.reshape(shape) for a in _adamw(red[k], w2d[k], two_d(m[k]), two_d(v[k]), "adamw_" + k)]
        grads.append(gk)
        deltas.append(d)
        new_m.append(m2)
        new_v.append(v2)
    return (loss, grad_x, *grads, *deltas, *new_m, *new_v)
```

```python
import functools

import jax
import jax.numpy as jnp
from jax import lax
from jax.experimental import pallas as pl
from jax.experimental.pallas import tpu as pltpu

f32 = jnp.float32
bf16 = jnp.bfloat16
HIGH = lax.Precision.HIGH
MESH = pl.DeviceIdType.MESH

N_META = 16
CONV_K = 4
GDN_H, GDN_DK, GDN_DV, GDN_C = 8, 128, 128, 64
GLA_H, GLA_DK, GLA_DV, GLA_C = 4, 128, 256, 16
GATE_RANK = 16
GATE_NORMALIZER = 16.0
EPS = 1e-6
GDN_QK = GDN_H * GDN_DK
GDN_V = GDN_H * GDN_DV
GLA_QK = GLA_H * GLA_DK
GLA_V = GLA_H * GLA_DV
PAD = (-N_META) % GDN_C
OFF = PAD + N_META
ROWS = 64

R_QKV, R_Z, R_AB, R_G, R_GR, R_LR, R_END = 0, 3072, 4096, 4112, 6160, 7184, 7200
C_ZR, C_G, C_QKV, C_END = 0, 2048, 4096, 7168
W_IN_SPLIT = 3072
ZR_W = GDN_V + GLA_V
G_W = 2 * GLA_QK + GLA_V
QKV_W = 2 * GDN_QK + GDN_V
Q0, K0, V0 = 0, GDN_QK, 2 * GDN_QK
SM_W = 128
SM_LR = 2 * GDN_H

ADAM_LR, ADAM_B1, ADAM_B2, ADAM_EPS, ADAM_WD, ADAM_STEP = 0.001, 0.9, 0.999, 1e-08, 0.01, 10

VMEM_LIMIT_V7X = 56 * 1024 * 1024
LANES = 128
N_DEV = 8
N_CHIP = 4


def _params(*sem):
    return pltpu.CompilerParams(dimension_semantics=sem, vmem_limit_bytes=VMEM_LIMIT_V7X)


def _tile(n, cap, mult=16):
    best = None
    for d in range(mult, min(n, cap) + 1, mult):
        if n % d == 0:
            best = d
    assert best is not None, (n, cap, mult)
    return best


NN = ((1,), (0,))
NT = ((1,), (1,))
TN = ((0,), (0,))


def _dot(a, b, dims, prec=None):
    return lax.dot_general(a, b, (dims, ((), ())), precision=prec, preferred_element_type=f32)


def _mmb(a, b, dims):
    return _dot(a.astype(bf16), b.astype(bf16), dims)


def _sigmoid(x):
    return jax.nn.sigmoid(x)


def _silu(x):
    return x * _sigmoid(x)


def _dsilu(x):
    s = _sigmoid(x)
    return s * (1.0 + x * (1.0 - s))


def _log1p_exp_neg_abs(x):
    t = jnp.exp(-jnp.abs(x))
    u = 1.0 + t
    d = u - 1.0
    return jnp.where(d == 0.0, t, jnp.log(u) * (t / jnp.where(d == 0.0, 1.0, d)))


def _softplus(x):
    return jnp.maximum(x, 0.0) + _log1p_exp_neg_abs(x)


def _log_sigmoid(x):
    return jnp.minimum(x, 0.0) - _log1p_exp_neg_abs(x)


def _rms(x):
    r = lax.rsqrt(jnp.mean(x * x, axis=-1, keepdims=True) + EPS)
    return x * r, r


def _rms_bwd(dy, xh, r, w):
    t = dy * w
    return r * (t - xh * jnp.mean(t * xh, axis=-1, keepdims=True))


def _l2n(x):
    return x * lax.rsqrt(jnp.sum(x * x, axis=-1, keepdims=True) + EPS)


INV_LEAF = 8


def _same_block(C, b):
    sh = b.bit_length() - 1
    row = lax.broadcasted_iota(jnp.int32, (C, C), 0)
    col = lax.broadcasted_iota(jnp.int32, (C, C), 1)
    return lax.shift_right_logical(row, sh) == lax.shift_right_logical(col, sh)


def _tri_inv_impl(As):
    C = As[0].shape[0]
    R = range(len(As))
    row = lax.broadcasted_iota(jnp.int32, (C, C), 0)
    col = lax.broadcasted_iota(jnp.int32, (C, C), 1)
    eye = (row == col).astype(f32)
    b = INV_LEAF
    inner = _same_block(C, b)
    leaf = [jnp.where(inner, As[h], 0.0) for h in R]
    d = [eye - leaf[h] for h in R]
    pw = leaf
    n = 2
    while n < b:
        pw = [_dot(pw[h], pw[h], NN, HIGH) for h in R]
        d = [_dot(d[h], eye + pw[h], NN, HIGH) for h in R]
        n *= 2
    while b < C:
        outer = _same_block(C, 2 * b)
        level = jnp.logical_and(outer, jnp.logical_not(inner))
        ed = [_dot(jnp.where(level, As[h], 0.0), d[h], NN, HIGH) for h in R]
        d = [d[h] - _dot(d[h], ed[h], NN, HIGH) for h in R]
        inner = outer
        b *= 2
    return d


@jax.custom_vjp
def _tri_inv(As):
    return _tri_inv_impl(As)


def _tri_inv_fwd(As):
    d = _tri_inv_impl(As)
    return d, d


def _tri_inv_bwd(d, g):
    R = range(len(d))
    t = [_dot(d[h], g[h], TN, HIGH) for h in R]
    return ([-_dot(t[h], d[h], NT, HIGH) for h in R],)


_tri_inv.defvjp(_tri_inv_fwd, _tri_inv_bwd)


@jax.custom_vjp
def _tri_inv_known(As, Ps):
    del As
    return Ps


def _tri_inv_known_fwd(As, Ps):
    del As
    return Ps, Ps


def _tri_inv_known_bwd(d, g):
    return _tri_inv_bwd(d, g)[0], [jnp.zeros_like(x) for x in d]


_tri_inv_known.defvjp(_tri_inv_known_fwd, _tri_inv_known_bwd)


def _gdn_chunk(Ss, qrs, krs, vs, betas, gs, Ps=None):
    H = len(Ss)
    C, dk = qrs[0].shape
    R = range(len(qrs))
    row = lax.broadcasted_iota(jnp.int32, (C, C), 0)
    col = lax.broadcasted_iota(jnp.int32, (C, C), 1)
    causal = row >= col
    strict = row > col
    cf = causal.astype(f32)
    q = [_l2n(qrs[h]) * (dk ** -0.5) for h in R]
    k = [_l2n(krs[h]) for h in R]
    mc = [_rows_exact(cf, jnp.broadcast_to(gs[h], (C, C))) for h in R]
    gc = [mc[h][:, 0:1] for h in R]
    decay = [jnp.where(causal, jnp.exp(jnp.where(causal, mc[h] - mc[h].T, 0.0)), 0.0) for h in R]
    kb = [k[h] * betas[h] for h in R]
    a = [jnp.where(strict, _mmb(kb[h], k[h], NT) * decay[h], 0.0) for h in R]
    p = _tri_inv(a) if Ps is None else _tri_inv_known(a, Ps)
    egc = [jnp.exp(gc[h]) for h in R]
    u = [_mmb(p[h], vs[h] * betas[h], NN) for h in R]
    w = [_mmb(p[h], kb[h] * egc[h], NN) for h in R]
    qk = [jnp.where(causal, _mmb(q[h], k[h], NT) * decay[h], 0.0) for h in R]
    qe = [q[h] * egc[h] for h in R]
    gl = [gc[h][C - 1:C, :] for h in R]
    kd = [k[h] * jnp.exp(gl[h] - gc[h]) for h in R]
    egl = [jnp.exp(gl[h]) for h in R]
    S, o, entering = list(Ss), [], []
    for chunk in range(len(qrs) // H):
        idx = [chunk * H + h for h in range(H)]
        entering += S
        v_new = [u[i] - _mmb(w[i], S[h], NN) for h, i in enumerate(idx)]
        o += [_mmb(qe[i], S[h], NN) + _mmb(qk[i], v_new[h], NN) for h, i in enumerate(idx)]
        S = [S[h] * egl[i] + _mmb(kd[i], v_new[h], TN) for h, i in enumerate(idx)]
    return S, o, p, entering


def _rows_exact_impl(m01, x, dims):
    m = m01.astype(bf16)
    x1 = x.astype(bf16)
    r1 = x - x1.astype(f32)
    x2 = r1.astype(bf16)
    x3 = (r1 - x2.astype(f32)).astype(bf16)
    d = lambda y: _dot(m, y, dims)
    return d(x1) + (d(x2) + d(x3))


@jax.custom_vjp
def _rows_exact(m01, x):
    return _rows_exact_impl(m01, x, NN)


def _rows_exact_fwd(m01, x):
    return _rows_exact_impl(m01, x, NN), m01


def _rows_exact_bwd(m01, g):
    return jnp.zeros_like(m01), _rows_exact_impl(m01, g, TN)


_rows_exact.defvjp(_rows_exact_fwd, _rows_exact_bwd)


def _gla_blocks(Sts, qrs, ks, vs, las):
    H = len(Sts)
    n = len(qrs)
    C, dk = qrs[0].shape
    R = range(n)
    row = lax.broadcasted_iota(jnp.int32, (C, C), 0)
    col = lax.broadcasted_iota(jnp.int32, (C, C), 1)
    ri = lax.broadcasted_iota(jnp.int32, (C, dk), 0)
    q = [qrs[h] * (dk ** -0.5) for h in R]
    running = (row >= col).astype(f32)
    b = [_rows_exact(running, las[h]) for h in R]
    sc = [jnp.where(row == col, jnp.sum(q[h] * ks[h], axis=-1, keepdims=True), 0.0) for h in R]
    s = C // 2
    while s >= 1:
        sh = s.bit_length() - 1
        ref = lax.shift_left(lax.shift_right_logical(row, sh + 1), sh + 1) + (s - 1)
        pick = (col == ref).astype(f32)
        bref = [_rows_exact(pick, b[h]) for h in R]
        upper = (lax.shift_right_logical(ri, sh) & 1) == 1
        qt = [jnp.where(upper, q[h] * jnp.exp(jnp.where(upper, b[h] - bref[h], 0.0)), 0.0) for h in R]
        kt = [jnp.where(upper, 0.0, ks[h] * jnp.exp(jnp.where(upper, 0.0, bref[h] - b[h]))) for h in R]
        same = lax.shift_right_logical(row, sh + 1) == lax.shift_right_logical(col, sh + 1)
        sc = [sc[h] + jnp.where(same, _mmb(qt[h], kt[h], NT), 0.0) for h in R]
        s //= 2
    o = [_mmb(sc[h], vs[h], NN) for h in R]
    qe = [q[h] * jnp.exp(b[h]) for h in R]
    bl = [b[h][C - 1:C, :] for h in R]
    upd = [_mmb(vs[h], ks[h] * jnp.exp(bl[h] - b[h]), TN) for h in R]
    ebl = [jnp.exp(bl[h]) for h in R]
    St = list(Sts)
    for blk in range(n // H):
        for h in range(H):
            i = blk * H + h
            o[i] = o[i] + _mmb(qe[i], St[h], NT)
        St = [St[h] * ebl[blk * H + h] + upd[blk * H + h] for h in range(H)]
    return St, o


_ANY = pl.BlockSpec(memory_space=pl.ANY)


class _Gather:
    def __init__(self, arrays, by_columns=False):
        self.arrays = list(arrays)
        self.n = len(self.arrays)
        self.by_columns = by_columns
        if by_columns:
            assert all(a.ndim == 2 and a.shape[1] % LANES == 0 for a in self.arrays)
            self.out_shape = [jax.ShapeDtypeStruct((a.shape[0], N_CHIP * a.shape[1]), a.dtype) for a in self.arrays]
        else:
            self.out_shape = [jax.ShapeDtypeStruct((N_CHIP,) + a.shape, a.dtype) for a in self.arrays]
        self.sems = [pltpu.SemaphoreType.DMA((self.n, 3)), pltpu.SemaphoreType.DMA((self.n, 3)),
                     pltpu.SemaphoreType.DMA((self.n,))]

    def hooks(self, ins, outs, send, recv, lsem):
        def place(a, chip):
            if not self.by_columns:
                return outs[a].at[chip]
            cols = self.arrays[a].shape[1]
            return outs[a].at[:, pl.ds(pl.multiple_of(chip * cols, LANES), cols)]

        def copies():
            x, y, c = lax.axis_index("x"), lax.axis_index("y"), lax.axis_index("c")
            me = 2 * x + y
            out = []
            for a in range(self.n):
                out.append((pltpu.make_async_copy(ins[a], place(a, me), lsem.at[a]), None))
                for j, (px, py) in enumerate([(1 - x, y), (x, 1 - y), (1 - x, 1 - y)]):
                    mk = lambda dst, a=a, j=j, px=px, py=py: pltpu.make_async_remote_copy(
                        src_ref=ins[a], dst_ref=dst, send_sem=send.at[a, j], recv_sem=recv.at[a, j],
                        device_id=(px, py, c), device_id_type=MESH)
                    out.append((mk(place(a, me)), mk(place(a, 2 * px + py))))
            return out

        return _start_wait(copies)


class _Exchange:
    def __init__(self, slotted, shared=(), by_chip=()):
        self.arrays = list(slotted) + list(by_chip) + list(shared)
        self.ns, self.nc = len(slotted), len(by_chip)
        self.n = len(self.arrays)
        self.out_shape = [jax.ShapeDtypeStruct(a.shape, a.dtype) for a in slotted]
        self.out_shape += [jax.ShapeDtypeStruct((N_DEV,) + a.shape[1:], a.dtype) for a in by_chip]
        self.out_shape += [jax.ShapeDtypeStruct((N_DEV,) + b.shape, b.dtype) for b in shared]
        self.sems = [pltpu.SemaphoreType.DMA((self.n, N_DEV - 1)), pltpu.SemaphoreType.DMA((self.n, N_DEV - 1)),
                     pltpu.SemaphoreType.DMA((self.n,))]

    def hooks(self, ins, outs, send, recv, lsem):
        def copies():
            x, y, c = lax.axis_index("x"), lax.axis_index("y"), lax.axis_index("c")
            me = 4 * x + 2 * y + c

            def src(a, dev):
                tx, ty, tc = dev
                if a < self.ns:
                    return ins[a].at[4 * tx + 2 * ty + tc]
                return ins[a].at[2 * tx + ty] if a < self.ns + self.nc else ins[a]

            out = []
            for a in range(self.n):
                out.append((pltpu.make_async_copy(src(a, (x, y, c)), outs[a].at[me], lsem.at[a]), None))
                for o in range(1, N_DEV):
                    dev = (1 - x if o & 4 else x, 1 - y if o & 2 else y, 1 - c if o & 1 else c)
                    t = 4 * dev[0] + 2 * dev[1] + dev[2]
                    mk = lambda dst, a=a, o=o, dev=dev: pltpu.make_async_remote_copy(
                        src_ref=src(a, dev), dst_ref=dst, send_sem=send.at[a, o - 1], recv_sem=recv.at[a, o - 1],
                        device_id=dev, device_id_type=MESH)
                    out.append((mk(outs[a].at[me]), mk(outs[a].at[t])))
            return out

        return _start_wait(copies)


class _Sides:
    def __init__(self, *members):
        self.members = members
        self.arrays = [a for s in members for a in s.arrays]
        self.n = len(self.arrays)
        self.out_shape = [sh for s in members for sh in s.out_shape]
        self.sems = [sm for s in members for sm in s.sems]

    def hooks(self, ins, outs, *sems):
        hooks, o = [], 0
        for i, s in enumerate(self.members):
            hooks.append(s.hooks(ins[o:o + s.n], outs[o:o + s.n], *sems[3 * i:3 * i + 3]))
            o += s.n

        def start():
            for st, _ in hooks:
                st()

        def wait():
            for _, wt in hooks:
                wt()

        return start, wait


def _start_wait(copies):
    def start():
        for s, _ in copies():
            s.start()

    def wait():
        for s, w in copies():
            (s if w is None else w).wait()

    return start, wait


def _call(body, *, name, grid, in_specs, out_specs, out_shape, args, sem, scratch_shapes=(), aliases=None, side=None):
    in_specs, out_specs, out_shape, args = list(in_specs), list(out_specs), list(out_shape), list(args)
    scratch_shapes = list(scratch_shapes)
    aliases = aliases or {}
    if side is None:
        return pl.pallas_call(
            body, name=name, grid=grid, in_specs=in_specs, out_specs=out_specs, out_shape=out_shape,
            scratch_shapes=scratch_shapes, input_output_aliases=aliases, compiler_params=_params(*sem))(*args)
    n_in, n_out, n_scr, ns = len(in_specs), len(out_specs), len(scratch_shapes), side.n

    def full_body(*refs):
        ins, refs = refs[:n_in], refs[n_in:]
        s_in, refs = refs[:ns], refs[ns:]
        outs, refs = refs[:n_out], refs[n_out:]
        s_out, refs = refs[:ns], refs[ns:]
        scr, sems = refs[:n_scr], refs[n_scr:]
        start, wait = side.hooks(s_in, s_out, *sems)
        ids = [pl.program_id(d) for d in range(len(grid))]
        first = functools.reduce(jnp.logical_and, [i == 0 for i in ids])
        last = functools.reduce(jnp.logical_and, [i == g - 1 for i, g in zip(ids, grid)])
        pl.when(first)(start)
        body(*ins, *outs, *scr)
        pl.when(last)(wait)

    return pl.pallas_call(
        full_body, name=name, grid=grid, in_specs=in_specs + [_ANY] * ns, out_specs=out_specs + [_ANY] * ns,
        out_shape=out_shape + side.out_shape, scratch_shapes=scratch_shapes + side.sems,
        input_output_aliases=aliases, compiler_params=_params(*(["arbitrary"] * len(grid))))(*args, *side.arrays)


WHOLE_K = dict(tm_cap=688, tn_cap=512, tk_cap=1 << 20)
WHOLE_K_T = dict(tm_cap=512, tn_cap=512, tk_cap=1 << 20)

def _mm(a, b, mode, name, *, tm_cap=1408, tn_cap=1024, tk_cap=2048, out_dtype=f32, acc_in=None, side=None,
        col_slabs=False, b_cols=None):
    if mode == "nn":
        (M, K), (K2, N) = a.shape, b.shape
    elif mode == "nt":
        (M, K), (N, K2) = a.shape, b.shape
    else:
        (K, M), (K2, N) = a.shape, b.shape
    assert K == K2, (name, a.shape, b.shape)
    b_first = 0
    if b_cols is not None:
        assert mode != "nt"
        b_first, N = b_cols
    tm = _tile(M // 2 if col_slabs else M, tm_cap)
    tn = _tile(N // N_CHIP if col_slabs else N, tn_cap, 128)
    tk = _tile(K, tk_cap, 128 if K % 128 == 0 else 16)
    nk = K // tk
    dims = {"nn": NN, "nt": NT, "tn": TN}[mode]
    use_scratch = nk > 1 and out_dtype != f32

    def body(*refs):
        if acc_in is not None:
            a_ref, b_ref, c_ref, o_ref, *scr = refs
        else:
            a_ref, b_ref, o_ref, *scr = refs
            c_ref = None
        p = _mmb(a_ref[...], b_ref[...], dims)
        if nk == 1:
            if c_ref is not None:
                p = p + c_ref[...]
            o_ref[...] = p.astype(out_dtype)
            return
        k = pl.program_id(2)
        acc = scr[0] if use_scratch else o_ref

        @pl.when(k == 0)
        def _():
            acc[...] = p if c_ref is None else p + c_ref[...]

        @pl.when(k > 0)
        def _():
            acc[...] += p

        if use_scratch:
            @pl.when(k == nk - 1)
            def _():
                o_ref[...] = acc[...].astype(out_dtype)

    if mode == "tn":
        a_spec = pl.BlockSpec((tk, tm), lambda i, j, k: (k, i))
    else:
        a_spec = pl.BlockSpec((tm, tk), lambda i, j, k: (i, k))
    if mode == "nt":
        b_spec = pl.BlockSpec((tn, tk), lambda i, j, k: (j, k))
    else:
        assert b_first % tn == 0, (name, b_first, tn)
        b_spec = pl.BlockSpec((tk, tn), lambda i, j, k: (k, j + b_first // tn))
    if col_slabs:
        assert acc_in is None
        ni, nj = M // 2 // tm, N // N_CHIP // tn
        o_spec = pl.BlockSpec((None, tm, tn), lambda i, j, k: (2 * (j // nj) + i // ni, i % ni, j % nj))
        o_shape = jax.ShapeDtypeStruct((N_DEV, M // 2, N // N_CHIP), out_dtype)
    else:
        o_spec = pl.BlockSpec((tm, tn), lambda i, j, k: (i, j))
        o_shape = jax.ShapeDtypeStruct((M, N), out_dtype)
    in_specs = [a_spec, b_spec]
    args = [a, b]
    if acc_in is not None:
        in_specs.append(o_spec)
        args.append(acc_in)
    out = _call(body, name=name, grid=(M // tm, N // tn, nk), in_specs=in_specs, out_specs=[o_spec],
                out_shape=[o_shape], args=args,
                scratch_shapes=[pltpu.VMEM((tm, tn), f32)] if use_scratch else [],
                sem=("parallel", "parallel", "arbitrary"), side=side)
    return out[0] if side is None else (out[0], out[1:])


def _embed_norm(x3, m64, w, side=None):
    _, S, D = x3.shape
    Lp = OFF + S

    def body(x_ref, m_ref, w_ref, h_ref, n_ref):
        i = pl.program_id(0)
        h = jnp.where(i == 0, m_ref[...], x_ref[...])
        h_ref[...] = h
        xh, _ = _rms(h)
        n_ref[...] = (xh * w_ref[...]).astype(bf16)

    row = pl.BlockSpec((ROWS, D), lambda i: (i, 0))
    out = _call(
        body, name="embed_norm", grid=(Lp // ROWS,),
        in_specs=[pl.BlockSpec((None, ROWS, D), lambda i: (0, jnp.maximum(i - 1, 0), 0)),
                  pl.BlockSpec((ROWS, D), lambda i: (0, 0)),
                  pl.BlockSpec((1, D), lambda i: (0, 0))],
        out_specs=[row, row],
        out_shape=[jax.ShapeDtypeStruct((Lp, D), f32), jax.ShapeDtypeStruct((Lp, D), bf16)],
        args=[x3, m64, w], sem=("parallel",), side=side)
    return out[0], out[1], out[2:]


def _add_norm(h, d, w):
    Lp, D = h.shape
    tr = _tile(Lp, 256)

    def body(h_ref, d_ref, w_ref, o_ref, n_ref):
        h1 = h_ref[...] + d_ref[...]
        o_ref[...] = h1
        xh, _ = _rms(h1)
        n_ref[...] = (xh * w_ref[...]).astype(bf16)

    row = pl.BlockSpec((tr, D), lambda i: (i, 0))
    return pl.pallas_call(
        body, name="add_norm", grid=(Lp // tr,),
        in_specs=[row, row, pl.BlockSpec((1, D), lambda i: (0, 0))], out_specs=[row, row],
        out_shape=[jax.ShapeDtypeStruct((Lp, D), f32), jax.ShapeDtypeStruct((Lp, D), bf16)],
        compiler_params=_params("parallel"),
    )(h, d, w)


def _norm_bwd(dn, h, dh, w):
    Lp, D = h.shape
    tr = _tile(Lp, 256)

    def body(dn_ref, h_ref, dh_ref, w_ref, o_ref, ob_ref, gw_ref):
        i = pl.program_id(0)
        xh, r = _rms(h_ref[...])
        dn_ = dn_ref[...]
        o = dh_ref[...] + _rms_bwd(dn_, xh, r, w_ref[...])
        o_ref[...] = o
        ob_ref[...] = o.astype(bf16)
        gw = jnp.sum(dn_ * xh, axis=0, keepdims=True)

        @pl.when(i == 0)
        def _():
            gw_ref[...] = gw

        @pl.when(i > 0)
        def _():
            gw_ref[...] += gw

    row = pl.BlockSpec((tr, D), lambda i: (i, 0))
    vec = pl.BlockSpec((1, D), lambda i: (0, 0))
    return pl.pallas_call(
        body, name="norm_bwd", grid=(Lp // tr,), in_specs=[row, row, row, vec], out_specs=[row, row, vec],
        out_shape=[jax.ShapeDtypeStruct((Lp, D), f32), jax.ShapeDtypeStruct((Lp, D), bf16),
                   jax.ShapeDtypeStruct((1, D), f32)],
        compiler_params=_params("arbitrary"),
    )(dn, h, dh, w)


def _embed_norm_bwd(dn, h, dh, w, S, side=None):
    Lp, D = h.shape
    tr = _tile(S, 256, OFF)

    def body(dn_ref, h_ref, dh_ref, dn0_ref, h0_ref, dh0_ref, w_ref, gx_ref, gm_ref, gw_ref):
        i = pl.program_id(0)

        def rows(dn_, h_, dh_):
            xh, r = _rms(h_)
            return dh_ + _rms_bwd(dn_, xh, r, w_ref[...]), jnp.sum(dn_ * xh, axis=0, keepdims=True)

        d, gw = rows(dn_ref[...], h_ref[...], dh_ref[...])
        gx_ref[...] = d

        @pl.when(i == 0)
        def _():
            d0, gw0 = rows(dn0_ref[...], h0_ref[...], dh0_ref[...])
            gm_ref[...] = d0[PAD:OFF, :]
            gw_ref[...] = gw0 + gw

        @pl.when(i > 0)
        def _():
            gw_ref[...] += gw

    win = pl.BlockSpec((pl.Element(tr), pl.Element(D)), lambda i: (pl.multiple_of(OFF + i * tr, OFF), 0))
    head = pl.BlockSpec((OFF, D), lambda i: (0, 0))
    vec = pl.BlockSpec((1, D), lambda i: (0, 0))
    out = _call(
        body, name="embed_norm_bwd", grid=(S // tr,), in_specs=[win, win, win, head, head, head, vec],
        out_specs=[pl.BlockSpec((None, tr, D), lambda i: (0, i, 0)),
                   pl.BlockSpec((N_META, D), lambda i: (0, 0)), vec],
        out_shape=[jax.ShapeDtypeStruct((1, S, D), f32), jax.ShapeDtypeStruct((N_META, D), f32),
                   jax.ShapeDtypeStruct((1, D), f32)],
        args=[dn, h, dh, dn, h, dh, w], sem=("arbitrary",), side=side)
    return out[0], out[1], out[2], out[3:]


def _final(h1, ffn, tgt3, w):
    Lp, D = h1.shape
    _, S, _ = tgt3.shape
    tr = _tile(Lp, min(256, S), OFF)

    def body(h_ref, f_ref, t_ref, w_ref, d_ref, db_ref, l_ref, gw_ref):
        i = pl.program_id(0)
        h2 = h_ref[...] + f_ref[...]
        xh, r = _rms(h2)
        w_ = w_ref[...]
        t = t_ref[...]
        t = jnp.where(i == 0, pltpu.roll(t, OFF, 0), t)
        valid = (lax.broadcasted_iota(jnp.int32, (tr, 1), 0) + i * tr >= OFF).astype(f32)
        e = xh * w_ - t
        loss = 0.5 * jnp.sum(jnp.mean(e * e, axis=-1, keepdims=True) * valid, axis=0, keepdims=True)
        dy = e * (valid / D)
        d = _rms_bwd(dy, xh, r, w_)
        d_ref[...] = d
        db_ref[...] = d.astype(bf16)
        gw = jnp.sum(dy * xh, axis=0, keepdims=True)

        @pl.when(i == 0)
        def _():
            l_ref[...] = jnp.zeros_like(l_ref)
            gw_ref[...] = jnp.zeros_like(gw_ref)

        l_ref[...] += jnp.broadcast_to(loss, l_ref.shape)
        gw_ref[...] += gw

    row = pl.BlockSpec((tr, D), lambda i: (i, 0))
    vec = pl.BlockSpec((1, D), lambda i: (0, 0))
    tgt = pl.BlockSpec((pl.Element(tr), pl.Element(D)),
                       lambda i: (pl.multiple_of(jnp.maximum(i * tr - OFF, 0), OFF), 0))
    return pl.pallas_call(
        body, name="final_loss", grid=(Lp // tr,), in_specs=[row, row, tgt, vec],
        out_specs=[row, row, pl.BlockSpec((8, 128), lambda i: (0, 0)), vec],
        out_shape=[jax.ShapeDtypeStruct((Lp, D), f32), jax.ShapeDtypeStruct((Lp, D), bf16),
                   jax.ShapeDtypeStruct((8, 128), f32), jax.ShapeDtypeStruct((1, D), f32)],
        compiler_params=_params("arbitrary"),
    )(h1, ffn, tgt3.reshape(S, D), w)


def _ffn_in(n, w_gate, w_up, side=None):
    M, K = n.shape
    F = w_gate.shape[1]
    tm = _tile(M, 1408)
    tn = _tile(F, 512, 128)

    def body(a_ref, bg_ref, bu_ref, act_ref, pg_ref, pu_ref):
        a = a_ref[...]
        g = _mmb(a, bg_ref[...], NN)
        u = _mmb(a, bu_ref[...], NN)
        s = _sigmoid(g)
        gs = g * s
        act_ref[...] = (gs * u).astype(bf16)
        pg_ref[...] = (u * (s + gs * (1.0 - s))).astype(bf16)
        pu_ref[...] = gs.astype(bf16)

    wsp = pl.BlockSpec((K, tn), lambda i, j: (0, j))
    osp = pl.BlockSpec((tm, tn), lambda i, j: (i, j))
    out = _call(body, name="ffn_in", grid=(M // tm, F // tn),
                in_specs=[pl.BlockSpec((tm, K), lambda i, j: (i, 0)), wsp, wsp], out_specs=[osp] * 3,
                out_shape=[jax.ShapeDtypeStruct((M, F), bf16)] * 3, args=[n, w_gate, w_up],
                sem=("parallel", "parallel"), side=side)
    return out[0], out[1], out[2], out[3:]


def _ffn_dact(d, w_down, pg, pu):
    M, K = d.shape
    F = w_down.shape[0]
    tm = _tile(M, 1408)
    tn = _tile(F, 512, 128)

    def body(d_ref, w_ref, pg_ref, pu_ref, dg_ref, du_ref):
        da = _mmb(d_ref[...], w_ref[...], NT)
        dg_ref[...] = (da * pg_ref[...].astype(f32)).astype(bf16)
        du_ref[...] = (da * pu_ref[...].astype(f32)).astype(bf16)

    osp = pl.BlockSpec((tm, tn), lambda i, j: (i, j))
    return pl.pallas_call(
        body, name="ffn_dact", grid=(M // tm, F // tn),
        in_specs=[pl.BlockSpec((tm, K), lambda i, j: (i, 0)), pl.BlockSpec((tn, K), lambda i, j: (j, 0)), osp, osp],
        out_specs=[osp, osp], out_shape=[jax.ShapeDtypeStruct((M, F), bf16)] * 2,
        compiler_params=_params("parallel", "parallel"),
    )(d, w_down, pg, pu)


def _ffn_dn(dg, du, w_gate, w_up):
    M, F = dg.shape
    D = w_gate.shape[0]
    tm = _tile(M, 688)
    tn = _tile(D, 256, 128)

    def body(dg_ref, du_ref, wg_ref, wu_ref, o_ref):
        o_ref[...] = _mmb(dg_ref[...], wg_ref[...], NT) + _mmb(du_ref[...], wu_ref[...], NT)

    asp = pl.BlockSpec((tm, F), lambda i, j: (i, 0))
    wsp = pl.BlockSpec((tn, F), lambda i, j: (j, 0))
    return pl.pallas_call(
        body, name="d_n2", grid=(M // tm, D // tn), in_specs=[asp, asp, wsp, wsp],
        out_specs=pl.BlockSpec((tm, tn), lambda i, j: (i, j)), out_shape=jax.ShapeDtypeStruct((M, D), f32),
        compiler_params=_params("parallel", "parallel"),
    )(dg, du, w_gate, w_up)


def _gates(psm, w2p, gate_b, alog, dtb):
    Lp = psm.shape[0]
    tr = _tile(Lp, 256)

    def body(p_ref, w_ref, b_ref, a_ref, t_ref, gb_ref, la_ref):
        i = pl.program_id(0)
        psm_ = p_ref[...]
        lane = lax.broadcasted_iota(jnp.int32, psm_.shape, 1)
        rowi = lax.broadcasted_iota(jnp.int32, (tr, 1), 0) + i * tr
        g = -jnp.exp(a_ref[...]) * _softplus(psm_ + t_ref[...])
        beta = _sigmoid(psm_)
        gb = jnp.where(lane < GDN_H, g, jnp.where(lane < 2 * GDN_H, beta, 0.0))
        gb_ref[...] = gb * (rowi >= PAD).astype(f32)
        logit = _mmb(psm_, w_ref[...], NN) + b_ref[...]
        la_ref[...] = _log_sigmoid(logit) * (1.0 / GATE_NORMALIZER)

    row = pl.BlockSpec((tr, SM_W), lambda i: (i, 0))
    return pl.pallas_call(
        body, name="gates", grid=(Lp // tr,),
        in_specs=[row, pl.BlockSpec((SM_W, GLA_QK), lambda i: (0, 0)), pl.BlockSpec((1, GLA_QK), lambda i: (0, 0)),
                  pl.BlockSpec((1, SM_W), lambda i: (0, 0)), pl.BlockSpec((1, SM_W), lambda i: (0, 0))],
        out_specs=[row, pl.BlockSpec((tr, GLA_QK), lambda i: (i, 0))],
        out_shape=[jax.ShapeDtypeStruct((Lp, SM_W), f32), jax.ShapeDtypeStruct((Lp, GLA_QK), f32)],
        compiler_params=_params("parallel"),
    )(psm, w2p, gate_b, alog, dtb)


def _gates_bwd(psm, w2p, gate_b, alog, dtb, dgb, dla):
    Lp = psm.shape[0]
    tr = _tile(Lp, 256)

    def body(p_ref, w_ref, b_ref, a_ref, t_ref, dgb_ref, dla_ref, dp_ref, gw_ref, gb_ref, ga_ref, gt_ref):
        i = pl.program_id(0)
        psm_ = p_ref[...]
        lane = lax.broadcasted_iota(jnp.int32, psm_.shape, 1)
        rowi = lax.broadcasted_iota(jnp.int32, (tr, 1), 0) + i * tr
        d = dgb_ref[...] * (rowi >= PAD).astype(f32)
        ea = jnp.exp(a_ref[...])
        z = psm_ + t_ref[...]
        is_g = lane < GDN_H
        dz = jnp.where(is_g, -ea * _sigmoid(z) * d, 0.0)
        dalog = jnp.where(is_g, -ea * _softplus(z) * d, 0.0)
        beta = _sigmoid(psm_)
        dbeta = jnp.where(jnp.logical_and(lane >= GDN_H, lane < 2 * GDN_H), beta * (1.0 - beta) * d, 0.0)
        logit = _mmb(psm_, w_ref[...], NN) + b_ref[...]
        dlogit = dla_ref[...] * (_sigmoid(-logit) * (1.0 / GATE_NORMALIZER))
        dlr = _mmb(dlogit, w_ref[...], NT)
        dp_ref[...] = (dz + dbeta + dlr).astype(bf16)
        gw = _mmb(psm_, dlogit, TN)
        gb = jnp.sum(dlogit, axis=0, keepdims=True)
        ga = jnp.sum(dalog, axis=0, keepdims=True)
        gt = jnp.sum(dz, axis=0, keepdims=True)

        @pl.when(i == 0)
        def _():
            gw_ref[...] = gw
            gb_ref[...] = gb
            ga_ref[...] = ga
            gt_ref[...] = gt

        @pl.when(i > 0)
        def _():
            gw_ref[...] += gw
            gb_ref[...] += gb
            ga_ref[...] += ga
            gt_ref[...] += gt

    row = pl.BlockSpec((tr, SM_W), lambda i: (i, 0))
    wsp = pl.BlockSpec((SM_W, GLA_QK), lambda i: (0, 0))
    bsp = pl.BlockSpec((1, GLA_QK), lambda i: (0, 0))
    vsp = pl.BlockSpec((1, SM_W), lambda i: (0, 0))
    return pl.pallas_call(
        body, name="gates_bwd", grid=(Lp // tr,),
        in_specs=[row, wsp, bsp, vsp, vsp, row, pl.BlockSpec((tr, GLA_QK), lambda i: (i, 0))],
        out_specs=[row, wsp, bsp, vsp, vsp],
        out_shape=[jax.ShapeDtypeStruct((Lp, SM_W), bf16), jax.ShapeDtypeStruct((SM_W, GLA_QK), f32),
                   jax.ShapeDtypeStruct((1, GLA_QK), f32), jax.ShapeDtypeStruct((1, SM_W), f32),
                   jax.ShapeDtypeStruct((1, SM_W), f32)],
        compiler_params=_params("arbitrary"),
    )(psm, w2p, gate_b, alog, dtb, dgb, dla)


def _conv_pre(x_ext, w, n):
    rows = x_ext.shape[0]
    y = x_ext * w[CONV_K - 1:CONV_K, :]
    for s in range(1, CONV_K):
        y = y + pltpu.roll(x_ext, s, 0) * w[CONV_K - 1 - s:CONV_K - s, :]
    return y[rows - n:, :]


def _conv(proj, cw, side=None):
    Lp = proj.shape[0]
    W = cw.shape[1]
    tr = _tile(Lp, 256, 64)
    tc = _tile(W, 1024, 128)
    c0 = C_QKV // tc

    def body(h_ref, x_ref, w_ref, o_ref):
        i = pl.program_id(1)
        halo = jnp.where(i == 0, 0.0, h_ref[...])
        x_ext = jnp.concatenate([halo, x_ref[...]], axis=0)
        o_ref[...] = _silu(_conv_pre(x_ext, w_ref[...], tr))

    out = _call(
        body, name="conv", grid=(W // tc, Lp // tr),
        in_specs=[pl.BlockSpec((8, tc), lambda j, i: (jnp.maximum(i * (tr // 8) - 1, 0), j + c0)),
                  pl.BlockSpec((tr, tc), lambda j, i: (i, j + c0)),
                  pl.BlockSpec((CONV_K, tc), lambda j, i: (0, j))],
        out_specs=[pl.BlockSpec((tr, tc), lambda j, i: (i, j))],
        out_shape=[jax.ShapeDtypeStruct((Lp, W), f32)], args=[proj, proj, cw],
        sem=("parallel", "parallel"), side=side)
    return out[0] if side is None else (out[0], out[1:])


def _conv_bwd(proj, cw, dy, dproj, side=None):
    Lp = proj.shape[0]
    W = cw.shape[1]
    tr = _tile(Lp, 256, 64)
    tc = _tile(W, 1024, 128)
    c0 = C_QKV // tc
    nr = Lp // tr
    last8 = Lp // 8 - 1

    def body(xp_ref, x_ref, xn_ref, w_ref, d_ref, dn_ref, dproj_ref, o_ref, gw_ref):
        del dproj_ref
        i = pl.program_id(1)
        w = w_ref[...]
        xp = jnp.where(i == 0, 0.0, xp_ref[...])
        x_ext = jnp.concatenate([xp, x_ref[...], xn_ref[...]], axis=0)
        n = tr + 8
        pre = _conv_pre(x_ext, w, n)
        dn = jnp.where(i == nr - 1, 0.0, dn_ref[...])
        dpre = jnp.concatenate([d_ref[...], dn], axis=0) * _dsilu(pre)
        dx = dpre * w[CONV_K - 1:CONV_K, :]
        for s in range(1, CONV_K):
            dx = dx + pltpu.roll(dpre, n - s, 0) * w[CONV_K - 1 - s:CONV_K - s, :]
        o_ref[...] = dx[:tr, :].astype(bf16)
        dp = dpre[:tr, :]
        rows = []
        for k in range(CONV_K):
            xs = x_ext if k == CONV_K - 1 else pltpu.roll(x_ext, CONV_K - 1 - k, 0)
            rows.append(jnp.sum(dp * xs[8:8 + tr, :], axis=0, keepdims=True))
        gw = jnp.concatenate(rows, axis=0)

        @pl.when(i == 0)
        def _():
            gw_ref[...] = gw

        @pl.when(i > 0)
        def _():
            gw_ref[...] += gw

    cur = pl.BlockSpec((tr, tc), lambda j, i: (i, j))
    nxt = pl.BlockSpec((8, tc), lambda j, i: (jnp.minimum((i + 1) * (tr // 8), last8), j))
    pcur = pl.BlockSpec((tr, tc), lambda j, i: (i, j + c0))
    pprev = pl.BlockSpec((8, tc), lambda j, i: (jnp.maximum(i * (tr // 8) - 1, 0), j + c0))
    pnext = pl.BlockSpec((8, tc), lambda j, i: (jnp.minimum((i + 1) * (tr // 8), last8), j + c0))
    wsp = pl.BlockSpec((CONV_K, tc), lambda j, i: (0, j))
    out = _call(
        body, name="conv_bwd", grid=(W // tc, nr),
        in_specs=[pprev, pcur, pnext, wsp, cur, nxt, _ANY], out_specs=[pcur, wsp],
        out_shape=[jax.ShapeDtypeStruct(dproj.shape, dproj.dtype), jax.ShapeDtypeStruct((CONV_K, W), f32)],
        aliases={6: 0}, args=[proj, proj, proj, cw, dy, dy, dproj], sem=("parallel", "arbitrary"), side=side)
    return out[0], out[1], out[2:]


GDN_FWD_GROUP = 3


def _gdn_group(Lp, most):
    n = Lp // GDN_C
    return next(g for g in range(most, 0, -1) if n % g == 0)


def _gdn_heads(x_ref, gb_ref, group):
    qs, ks, vs, bs, gs = [], [], [], [], []
    for chunk in range(group):
        r = slice(chunk * GDN_C, (chunk + 1) * GDN_C)
        gbv = gb_ref[r, :]
        for h in range(GDN_H):
            qs.append(x_ref[r, Q0 + h * GDN_DK:Q0 + (h + 1) * GDN_DK])
            ks.append(x_ref[r, K0 + h * GDN_DK:K0 + (h + 1) * GDN_DK])
            vs.append(x_ref[r, V0 + h * GDN_DV:V0 + (h + 1) * GDN_DV])
            bs.append(gbv[:, GDN_H + h:GDN_H + h + 1])
            gs.append(gbv[:, h:h + 1])
    return qs, ks, vs, bs, gs


def _gdn_fwd(qkvc, gb, side=None):
    Lp = qkvc.shape[0]
    group = _gdn_group(Lp, GDN_FWD_GROUP)
    rows = group * GDN_C
    steps = Lp // rows
    R = range(GDN_H)

    def body(x_ref, gb_ref, o_ref, sall_ref, pall_ref, s_scr):
        @pl.when(pl.program_id(0) == 0)
        def _():
            s_scr[...] = jnp.zeros_like(s_scr)

        S2, o, p, entering = _gdn_chunk([s_scr[h] for h in R], *_gdn_heads(x_ref, gb_ref, group))
        for h in R:
            s_scr[h] = S2[h]
        for chunk in range(group):
            for h in R:
                i = chunk * GDN_H + h
                o_ref[chunk * GDN_C:(chunk + 1) * GDN_C, h * GDN_DV:(h + 1) * GDN_DV] = o[i]
                pall_ref[chunk, h] = p[i]
                sall_ref[chunk, h] = entering[i]

    out = _call(
        body, name="gdn_fwd", grid=(steps,),
        in_specs=[pl.BlockSpec((rows, QKV_W), lambda n: (n, 0)), pl.BlockSpec((rows, SM_W), lambda n: (n, 0))],
        out_specs=[pl.BlockSpec((rows, GDN_V), lambda n: (n, 0)),
                   pl.BlockSpec((group, GDN_H, GDN_DK, GDN_DV), lambda n: (n, 0, 0, 0)),
                   pl.BlockSpec((group, GDN_H, GDN_C, GDN_C), lambda n: (n, 0, 0, 0))],
        out_shape=[jax.ShapeDtypeStruct((Lp, GDN_V), f32),
                   jax.ShapeDtypeStruct((Lp // GDN_C, GDN_H, GDN_DK, GDN_DV), f32),
                   jax.ShapeDtypeStruct((Lp // GDN_C, GDN_H, GDN_C, GDN_C), f32)],
        scratch_shapes=[pltpu.VMEM((GDN_H, GDN_DK, GDN_DV), f32)], args=[qkvc, gb], sem=("arbitrary",), side=side)
    return out[0], out[1], out[2], out[3:]


def _gdn_bwd(qkvc, gb, sall, pall, do, side=None):
    Lp = qkvc.shape[0]
    group = 1
    rows = group * GDN_C
    steps = Lp // rows
    R = range(GDN_H)

    def body(x_ref, gb_ref, sall_ref, pall_ref, do_ref, dx_ref, dgb_ref, ds_scr):
        @pl.when(pl.program_id(0) == 0)
        def _():
            ds_scr[...] = jnp.zeros_like(ds_scr)

        lane = lax.broadcasted_iota(jnp.int32, (GDN_C, SM_W), 1)
        ps = [pall_ref[chunk, h] for chunk in range(group) for h in R]
        _, vjp = jax.vjp(lambda *a: _gdn_chunk(*a, Ps=ps)[:2],
                         [sall_ref[0, h] for h in R], *_gdn_heads(x_ref, gb_ref, group))
        do = [do_ref[chunk * GDN_C:(chunk + 1) * GDN_C, h * GDN_DV:(h + 1) * GDN_DV]
              for chunk in range(group) for h in R]
        dS, dq, dk, dv, dbeta, dg = vjp(([ds_scr[h] for h in R], do))
        for h in R:
            ds_scr[h] = dS[h]
        for chunk in range(group):
            r = slice(chunk * GDN_C, (chunk + 1) * GDN_C)
            acc = jnp.zeros((GDN_C, SM_W), f32)
            for h in R:
                i = chunk * GDN_H + h
                dx_ref[r, Q0 + h * GDN_DK:Q0 + (h + 1) * GDN_DK] = dq[i]
                dx_ref[r, K0 + h * GDN_DK:K0 + (h + 1) * GDN_DK] = dk[i]
                dx_ref[r, V0 + h * GDN_DV:V0 + (h + 1) * GDN_DV] = dv[i]
                acc = acc + jnp.where(lane == h, dg[i], 0.0) + jnp.where(lane == GDN_H + h, dbeta[i], 0.0)
            dgb_ref[r, :] = acc

    rev = lambda n: (steps - 1 - n, 0)
    out = _call(
        body, name="gdn_bwd", grid=(steps,),
        in_specs=[pl.BlockSpec((rows, QKV_W), rev), pl.BlockSpec((rows, SM_W), rev),
                  pl.BlockSpec((1, GDN_H, GDN_DK, GDN_DV), lambda n: (steps - 1 - n, 0, 0, 0)),
                  pl.BlockSpec((group, GDN_H, GDN_C, GDN_C), lambda n: (steps - 1 - n, 0, 0, 0)),
                  pl.BlockSpec((rows, GDN_V), rev)],
        out_specs=[pl.BlockSpec((rows, QKV_W), rev), pl.BlockSpec((rows, SM_W), rev)],
        out_shape=[jax.ShapeDtypeStruct((Lp, QKV_W), f32), jax.ShapeDtypeStruct((Lp, SM_W), f32)],
        scratch_shapes=[pltpu.VMEM((GDN_H, GDN_DK, GDN_DV), f32)], args=[qkvc, gb, sall, pall, do],
        sem=("arbitrary",), side=side)
    return out[0], out[1], out[2:]


GLA_BLOCK = 64


def _gla_group(Lp):
    nb = Lp // GLA_BLOCK
    return next(g for g in (3, 2, 1) if nb % g == 0)


def _gla_slices(h):
    sq = slice(h * GLA_DK, (h + 1) * GLA_DK)
    sk = slice(GLA_QK + h * GLA_DK, GLA_QK + (h + 1) * GLA_DK)
    sv = slice(2 * GLA_QK + h * GLA_DV, 2 * GLA_QK + (h + 1) * GLA_DV)
    return sq, sk, sv


def _gla_heads(x_ref, la_ref, group):
    qs, ks, vs, ls = [], [], [], []
    for blk in range(group):
        r = slice(blk * GLA_BLOCK, (blk + 1) * GLA_BLOCK)
        for h in range(GLA_H):
            sq, sk, sv = _gla_slices(h)
            qs.append(x_ref[r, sq])
            ks.append(x_ref[r, sk])
            vs.append(x_ref[r, sv])
            ls.append(la_ref[r, sq])
    return qs, ks, vs, ls


def _gla_fwd(proj, la):
    Lp = proj.shape[0]
    group = _gla_group(Lp)
    rows = group * GLA_BLOCK
    steps = Lp // rows
    R = range(GLA_H)

    def body(x_ref, la_ref, o_ref, sall_ref, s_scr):
        @pl.when(pl.program_id(0) == 0)
        def _():
            s_scr[...] = jnp.zeros_like(s_scr)

        Sts = [s_scr[h] for h in R]
        for h in R:
            sall_ref[0, h] = Sts[h]
        St2, o = _gla_blocks(Sts, *_gla_heads(x_ref, la_ref, group))
        for h in R:
            s_scr[h] = St2[h]
        for blk in range(group):
            for h in R:
                o_ref[blk * GLA_BLOCK:(blk + 1) * GLA_BLOCK, h * GLA_DV:(h + 1) * GLA_DV] = o[blk * GLA_H + h]

    return pl.pallas_call(
        body, name="gla_fwd", grid=(steps,),
        in_specs=[pl.BlockSpec((rows, G_W), lambda n: (n, C_G // G_W)),
                  pl.BlockSpec((rows, GLA_QK), lambda n: (n, 0))],
        out_specs=[pl.BlockSpec((rows, GLA_V), lambda n: (n, 0)),
                   pl.BlockSpec((1, GLA_H, GLA_DV, GLA_DK), lambda n: (n, 0, 0, 0))],
        out_shape=[jax.ShapeDtypeStruct((Lp, GLA_V), f32),
                   jax.ShapeDtypeStruct((steps, GLA_H, GLA_DV, GLA_DK), f32)],
        scratch_shapes=[pltpu.VMEM((GLA_H, GLA_DV, GLA_DK), f32)],
        compiler_params=_params("arbitrary"),
    )(proj, la)


def _gla_bwd(proj, la, sall, do, dproj, side=None):
    Lp = proj.shape[0]
    group = _gla_group(Lp)
    rows = group * GLA_BLOCK
    steps = Lp // rows
    R = range(GLA_H)

    def body(x_ref, la_ref, sall_ref, do_ref, dproj_ref, dx_ref, dla_ref, ds_scr):
        del dproj_ref

        @pl.when(pl.program_id(0) == 0)
        def _():
            ds_scr[...] = jnp.zeros_like(ds_scr)

        _, vjp = jax.vjp(_gla_blocks, [sall_ref[0, h] for h in R], *_gla_heads(x_ref, la_ref, group))
        do = [do_ref[blk * GLA_BLOCK:(blk + 1) * GLA_BLOCK, h * GLA_DV:(h + 1) * GLA_DV]
              for blk in range(group) for h in R]
        dS, dq, dk, dv, dl = vjp(([ds_scr[h] for h in R], do))
        for h in R:
            ds_scr[h] = dS[h]
        for blk in range(group):
            r = slice(blk * GLA_BLOCK, (blk + 1) * GLA_BLOCK)
            for h in R:
                sq, sk, sv = _gla_slices(h)
                i = blk * GLA_H + h
                dx_ref[r, sq] = dq[i].astype(bf16)
                dx_ref[r, sk] = dk[i].astype(bf16)
                dx_ref[r, sv] = dv[i].astype(bf16)
                dla_ref[r, sq] = dl[i]

    x_spec = pl.BlockSpec((rows, G_W), lambda n: (steps - 1 - n, C_G // G_W))
    rev = lambda n: (steps - 1 - n, 0)
    out = _call(
        body, name="gla_bwd", grid=(steps,),
        in_specs=[x_spec, pl.BlockSpec((rows, GLA_QK), rev),
                  pl.BlockSpec((1, GLA_H, GLA_DV, GLA_DK), lambda n: (steps - 1 - n, 0, 0, 0)),
                  pl.BlockSpec((rows, GLA_V), rev), _ANY],
        out_specs=[x_spec, pl.BlockSpec((rows, GLA_QK), rev)],
        out_shape=[jax.ShapeDtypeStruct(dproj.shape, dproj.dtype), jax.ShapeDtypeStruct((Lp, GLA_QK), f32)],
        aliases={4: 0}, scratch_shapes=[pltpu.VMEM((GLA_H, GLA_DV, GLA_DK), f32)],
        args=[proj, la, sall, do, dproj], sem=("arbitrary",), side=side)
    return out[0], out[1], out[2:]


def _gated_norm_fn(og, ol, zr, wg, wl):
    outs = []
    for h in range(GDN_H):
        s = slice(h * GDN_DV, (h + 1) * GDN_DV)
        outs.append(_rms(og[:, s])[0] * wg * _silu(zr[:, s]))
    for h in range(GLA_H):
        s = slice(h * GLA_DV, (h + 1) * GLA_DV)
        sr = slice(GDN_V + h * GLA_DV, GDN_V + (h + 1) * GLA_DV)
        outs.append(_rms(ol[:, s])[0] * wl * _silu(zr[:, sr]))
    return jnp.concatenate(outs, axis=-1)


def _gated_norm(og, ol, proj, wg, wl):
    Lp = og.shape[0]
    tr = _tile(Lp, 256)

    def body(og_ref, ol_ref, zr_ref, wg_ref, wl_ref, o_ref):
        o_ref[...] = _gated_norm_fn(og_ref[...], ol_ref[...], zr_ref[...], wg_ref[...], wl_ref[...]).astype(bf16)

    return pl.pallas_call(
        body, name="gated_norm", grid=(Lp // tr,),
        in_specs=[pl.BlockSpec((tr, GDN_V), lambda i: (i, 0)), pl.BlockSpec((tr, GLA_V), lambda i: (i, 0)),
                  pl.BlockSpec((tr, ZR_W), lambda i: (i, C_ZR // ZR_W)),
                  pl.BlockSpec((1, GDN_DV), lambda i: (0, 0)), pl.BlockSpec((1, GLA_DV), lambda i: (0, 0))],
        out_specs=pl.BlockSpec((tr, ZR_W), lambda i: (i, 0)),
        out_shape=jax.ShapeDtypeStruct((Lp, ZR_W), bf16),
        compiler_params=_params("parallel"),
    )(og, ol, proj, wg, wl)


def _gated_norm_bwd(og, ol, proj, wg, wl, dmix):
    Lp = og.shape[0]
    tr = _tile(Lp, 128)

    def body(og_ref, ol_ref, zr_ref, wg_ref, wl_ref, d_ref, dog_ref, dol_ref, dzr_ref, gwg_ref, gwl_ref):
        i = pl.program_id(0)
        _, vjp = jax.vjp(_gated_norm_fn, og_ref[...], ol_ref[...], zr_ref[...], wg_ref[...], wl_ref[...])
        dog, dol, dzr, gwg, gwl = vjp(d_ref[...])
        dog_ref[...] = dog
        dol_ref[...] = dol
        dzr_ref[...] = dzr.astype(bf16)

        @pl.when(i == 0)
        def _():
            gwg_ref[...] = gwg
            gwl_ref[...] = gwl

        @pl.when(i > 0)
        def _():
            gwg_ref[...] += gwg
            gwl_ref[...] += gwl

    og_spec = pl.BlockSpec((tr, GDN_V), lambda i: (i, 0))
    ol_spec = pl.BlockSpec((tr, GLA_V), lambda i: (i, 0))
    zr_spec = pl.BlockSpec((tr, ZR_W), lambda i: (i, C_ZR // ZR_W))
    vg = pl.BlockSpec((1, GDN_DV), lambda i: (0, 0))
    vl = pl.BlockSpec((1, GLA_DV), lambda i: (0, 0))
    return pl.pallas_call(
        body, name="gated_norm_bwd", grid=(Lp // tr,),
        in_specs=[og_spec, ol_spec, zr_spec, vg, vl, pl.BlockSpec((tr, ZR_W), lambda i: (i, 0))],
        out_specs=[og_spec, ol_spec, zr_spec, vg, vl],
        out_shape=[jax.ShapeDtypeStruct((Lp, GDN_V), f32), jax.ShapeDtypeStruct((Lp, GLA_V), f32),
                   jax.ShapeDtypeStruct((Lp, C_END), bf16),
                   jax.ShapeDtypeStruct((1, GDN_DV), f32), jax.ShapeDtypeStruct((1, GLA_DV), f32)],
        compiler_params=_params("arbitrary"),
    )(og, ol, proj, wg, wl, dmix)


def _adamw_rule(g_, w_, m_, v_):
    c1 = 1.0 - ADAM_B1 ** ADAM_STEP
    c2 = 1.0 - ADAM_B2 ** ADAM_STEP
    m2 = ADAM_B1 * m_ + (1.0 - ADAM_B1) * g_
    v2 = ADAM_B2 * v_ + (1.0 - ADAM_B2) * (g_ * g_)
    return -ADAM_LR * ((m2 / c1) / (jnp.sqrt(v2 / c2) + ADAM_EPS) + ADAM_WD * w_), m2, v2


def _adamw(g, w, m, v, name):
    R, C = g.shape
    tr = _tile(R, 256, 8) if R % 8 == 0 and R > 256 else R

    def body(g_ref, w_ref, m_ref, v_ref, d_ref, mo_ref, vo_ref):
        d_ref[...], mo_ref[...], vo_ref[...] = _adamw_rule(g_ref[...], w_ref[...], m_ref[...], v_ref[...])

    blk = pl.BlockSpec((tr, C), lambda i: (i, 0))
    return pl.pallas_call(
        body, name=name, grid=(R // tr,), in_specs=[blk] * 4, out_specs=[blk] * 3,
        out_shape=[jax.ShapeDtypeStruct((R, C), f32)] * 3,
        compiler_params=_params("parallel"),
    )(g, w, m, v)


def _adamw_transposed(gt, w, m, v, name):
    n, C, rb = gt.shape
    assert w.shape == (C, n * rb), (gt.shape, w.shape)

    def body(g_ref, w_ref, m_ref, v_ref, go_ref, d_ref, mo_ref, vo_ref):
        g_ = g_ref[...].astype(f32)
        go_ref[...] = g_
        d_ref[...], mo_ref[...], vo_ref[...] = _adamw_rule(g_, w_ref[...], m_ref[...], v_ref[...])

    blk = pl.BlockSpec((C, rb), lambda j: (0, j))
    return pl.pallas_call(
        body, name=name, grid=(n,), in_specs=[pl.BlockSpec((None, C, rb), lambda j: (j, 0, 0))] + [blk] * 3,
        out_specs=[blk] * 4, out_shape=[jax.ShapeDtypeStruct((C, n * rb), f32)] * 4,
        compiler_params=_params("parallel"),
    )(gt, w, m, v)


def _sum_slots(r, name):
    n, R, C = r.shape
    tr = _tile(R, 128, 16) if R % 16 == 0 and R > 128 else R

    def body(r_ref, o_ref):
        acc = r_ref[0].astype(f32)
        for s in range(1, n):
            acc = acc + r_ref[s].astype(f32)
        o_ref[...] = acc

    return pl.pallas_call(
        body, name=name, grid=(R // tr,),
        in_specs=[pl.BlockSpec((n, tr, C), lambda i: (0, i, 0))],
        out_specs=pl.BlockSpec((tr, C), lambda i: (i, 0)),
        out_shape=jax.ShapeDtypeStruct((R, C), f32),
        compiler_params=_params("parallel"),
    )(r)


SIBLING_PARTS = 8


class _Siblings:
    def __init__(self, arrays):
        self.arrays = list(arrays)
        self.n = len(self.arrays)
        self.parts = [next(p for p in range(SIBLING_PARTS, 0, -1) if a.shape[0] % (8 * p) == 0 or p == 1)
                      for a in self.arrays]
        total = sum(self.parts)
        self.out_shape = [jax.ShapeDtypeStruct((2,) + a.shape, a.dtype) for a in self.arrays]
        self.sems = [pltpu.SemaphoreType.DMA((total,)), pltpu.SemaphoreType.DMA((total,)),
                     pltpu.SemaphoreType.DMA((self.n,))]

    def hooks(self, ins, outs, send, recv, lsem):
        def copies():
            x, y, c = lax.axis_index("x"), lax.axis_index("y"), lax.axis_index("c")
            out, k = [], 0
            for a in range(self.n):
                out.append((pltpu.make_async_copy(ins[a], outs[a].at[c], lsem.at[a]), None))
                rows = self.arrays[a].shape[0] // self.parts[a]
                for part in range(self.parts[a]):
                    r = pl.ds(part * rows, rows)
                    mk = lambda dst, a=a, r=r, k=k: pltpu.make_async_remote_copy(
                        src_ref=ins[a].at[r], dst_ref=dst.at[r], send_sem=send.at[k], recv_sem=recv.at[k],
                        device_id=(x, y, 1 - c), device_id_type=MESH)
                    out.append((mk(outs[a].at[c]), mk(outs[a].at[1 - c])))
                    k += 1
            return out

        return _start_wait(copies)


def _comm_now(name, sides):
    total = sum(s.n for s in sides)

    def body(*refs):
        ins, outs, sems = refs[:total], refs[total:2 * total], refs[2 * total:]
        hooks, o = [], 0
        for i, s in enumerate(sides):
            hooks.append(s.hooks(ins[o:o + s.n], outs[o:o + s.n], *sems[3 * i:3 * i + 3]))
            o += s.n
        for start, _ in hooks:
            start()
        for _, wait in hooks:
            wait()

    out = pl.pallas_call(
        body, name=name, in_specs=[_ANY] * total, out_specs=[_ANY] * total,
        out_shape=[sh for s in sides for sh in s.out_shape], scratch_shapes=[sm for s in sides for sm in s.sems],
    )(*[a for s in sides for a in s.arrays])
    res, o = [], 0
    for s in sides:
        res.append(list(out[o:o + s.n]))
        o += s.n
    return res


def _cat_cols(g):
    return jnp.concatenate([g[i] for i in range(N_CHIP)], axis=-1)


def _row_slabs(a):
    return a.reshape(N_DEV, a.shape[0] // N_DEV, a.shape[1])


def _w_in_columns(g_wp, g_wsm):
    return jnp.concatenate([g_wp[:, C_QKV:C_END], g_wp[:, C_ZR:C_ZR + GDN_V], g_wsm[:, :SM_LR],
                            g_wp[:, C_G:C_G + G_W], g_wp[:, C_ZR + GDN_V:C_ZR + ZR_W],
                            g_wsm[:, SM_LR:SM_LR + GATE_RANK]], axis=1)


def _step(x, loss_target, p, meta, shard):
    _, S, D = x.shape
    alog_p = jnp.pad(p["gdn_a_log"], ((0, 0), (0, SM_W - GDN_H)))
    dtb_p = jnp.pad(p["gdn_dt_bias"], ((0, 0), (0, SM_W - GDN_H)))
    m64 = jnp.concatenate([jnp.zeros((PAD, D), f32), meta], axis=0)
    gate_b, gdn_norm_w, gla_norm_w = p["gla_gate_b"], p["gdn_norm_w"], p["gla_norm_w"]

    h0, n1, (w_in4, conv4, w24) = _embed_norm(
        x, m64, p["attn_norm_w"], side=_Gather([shard["w_in"], shard["gdn_conv_w"], shard["gla_gate_w2"]]))
    w_in, conv_w, w2 = _cat_cols(w_in4), _cat_cols(conv4), _cat_cols(w24)
    wp = jnp.concatenate([w_in[:, R_Z:R_AB], w_in[:, R_GR:R_LR], w_in[:, R_G:R_GR], w_in[:, R_QKV:R_Z]], axis=1)
    wsm = jnp.concatenate([w_in[:, R_AB:R_G], w_in[:, R_LR:R_END],
                           jnp.zeros((D, SM_W - SM_LR - GATE_RANK), w_in.dtype)], axis=1)
    w2p = jnp.pad(w2, ((SM_LR, SM_W - SM_LR - GATE_RANK), (0, 0)))
    proj, (w_up,) = _mm(n1, wp, "nn", "proj", side=_Gather([shard["w_up"]], by_columns=True))
    psm = _mm(n1, wsm, "nn", "proj_small")
    gb, la = _gates(psm, w2p, gate_b, alog_p, dtb_p)
    qkvc, (w_out4,) = _conv(proj, conv_w, side=_Gather([shard["w_out"]]))
    w_out = w_out4.reshape(-1, D)
    og, sall, pall, (w_gate,) = _gdn_fwd(qkvc, gb, side=_Gather([shard["w_gate"]], by_columns=True))
    ol, stall = _gla_fwd(proj, la)
    mixed = _gated_norm(og, ol, proj, gdn_norm_w, gla_norm_w)
    attn = _mm(mixed, w_out, "nn", "out_proj")
    h1, n2 = _add_norm(h0, attn, p["ffn_norm_w"])
    act, act_dgate, act_dup, (w_down4,) = _ffn_in(n2, w_gate, w_up, side=_Gather([shard["w_down"]]))
    w_down = w_down4.reshape(-1, D)
    ffn = _mm(act, w_down, "nn", "ffn_down", **WHOLE_K)
    dh2, dh2b, lossp, g_final = _final(h1, ffn, loss_target, p["final_norm_w"])

    g_down = _mm(act, dh2b, "tn", "g_w_down", out_dtype=bf16, **WHOLE_K_T)
    dg, du = _ffn_dact(dh2b, w_down, act_dgate, act_dup)
    g_gate = _mm(n2, dg, "tn", "g_w_gate", tm_cap=512, tn_cap=1408, tk_cap=2752, out_dtype=bf16, col_slabs=True)
    g_up = _mm(n2, du, "tn", "g_w_up", tm_cap=512, tn_cap=1408, tk_cap=2752, out_dtype=bf16, col_slabs=True)
    dn2 = _ffn_dn(dg, du, w_gate, w_up)
    dh1, dh1b, g_ffn_norm = _norm_bwd(dn2, h1, dh2, p["ffn_norm_w"])
    dmix = _mm(dh1b, w_out, "nt", "d_mixed")
    g_out = _mm(mixed, dh1b, "tn", "g_w_out", out_dtype=bf16, **WHOLE_K_T)
    dog, dol, dproj, g_gdn_norm, g_gla_norm = _gated_norm_bwd(og, ol, proj, gdn_norm_w, gla_norm_w, dmix)
    dproj, dla, (r_down,) = _gla_bwd(proj, la, stall, dol, dproj, side=_Exchange([_row_slabs(g_down)]))
    dqkvc, dgb, (r_gate, r_up, r_out, h_down) = _gdn_bwd(
        qkvc, gb, sall, pall, dog,
        side=_Sides(_Exchange([g_gate, g_up, _row_slabs(g_out)]), _Siblings([_sum_slots(r_down, "sum_w_down")])))
    dproj, g_conv, (h_gate,) = _conv_bwd(proj, conv_w, dqkvc, dproj,
                                         side=_Siblings([_sum_slots(r_gate, "sum_w_gate")]))
    dpsm, g_w2p, g_gate_b, g_alog, g_dtb = _gates_bwd(psm, w2p, gate_b, alog_p, dtb_p, dgb, dla)
    g_wsm = _mm(n1, dpsm, "tn", "g_w_in_small", out_dtype=bf16, **WHOLE_K_T)
    g_wp_a, (h_out,) = _mm(n1, dproj, "tn", "g_w_in_a", out_dtype=bf16, b_cols=(0, W_IN_SPLIT),
                           side=_Siblings([_sum_slots(r_out, "sum_w_out")]), **WHOLE_K_T)
    g_wp_b, r_in_a = _mm(n1, dproj, "tn", "g_w_in_b", out_dtype=bf16, b_cols=(W_IN_SPLIT, C_END - W_IN_SPLIT),
                         side=_Exchange([_row_slabs(g_wp_a), _row_slabs(g_wsm)]), **WHOLE_K_T)
    dn1, (r_in_b, h_up) = _mm(
        dproj, wp, "nt", "d_n1",
        side=_Sides(_Exchange([_row_slabs(g_wp_b)]), _Siblings([_sum_slots(r_up, "sum_w_up")])), **WHOLE_K)
    dn1 = _mm(dpsm, wsm, "nt", "d_n1_small", acc_in=dn1)
    s_wp = jnp.concatenate([_sum_slots(r_in_a[0], "sum_w_in_a"), _sum_slots(r_in_b, "sum_w_in_b")], axis=1)
    s_in = _w_in_columns(s_wp, _sum_slots(r_in_a[1], "sum_w_in_small"))
    in_by_chip = s_in.reshape(s_in.shape[0], N_CHIP, -1).transpose(1, 2, 0).astype(bf16)
    grad_x, g_meta, g_attn_norm, (h_in,) = _embed_norm_bwd(dn1, h0, dh1, p["attn_norm_w"], S,
                                                           side=_Exchange([], by_chip=[in_by_chip]))

    received = dict(w_in=h_in, w_gate=h_gate, w_up=h_up, w_out=h_out, w_down=h_down)
    small = dict(
        meta_tokens=g_meta, attn_norm_w=g_attn_norm, gdn_conv_w=g_conv, gdn_a_log=g_alog[:, :GDN_H],
        gdn_dt_bias=g_dtb[:, :GDN_H], gdn_norm_w=g_gdn_norm, gla_gate_w2=g_w2p[SM_LR:SM_LR + GATE_RANK],
        gla_gate_b=g_gate_b, gla_norm_w=g_gla_norm, ffn_norm_w=g_ffn_norm, final_norm_w=g_final)
    return lossp[0, 0], grad_x, received, small


_WEIGHTS = ("meta_tokens", "attn_norm_w", "w_in", "gdn_conv_w", "gdn_a_log", "gdn_dt_bias", "gdn_norm_w",
            "gla_gate_w2", "gla_gate_b", "gla_norm_w", "w_out", "ffn_norm_w", "w_gate", "w_up", "w_down",
            "final_norm_w")
_BIG_COLS = ("w_in", "w_gate", "w_up")
_BIG_ROWS = ("w_out", "w_down")
_SMALL_SHARDED = ("meta_tokens", "gdn_conv_w", "gla_gate_w2")


def kernel(x, meta_tokens, attn_norm_w, w_in, gdn_conv_w, gdn_a_log, gdn_dt_bias, gdn_norm_w, gla_gate_w2, gla_gate_b, gla_norm_w, w_out, ffn_norm_w, w_gate, w_up, w_down, final_norm_w, loss_target, m_meta_tokens, m_attn_norm_w, m_w_in, m_gdn_conv_w, m_gdn_a_log, m_gdn_dt_bias, m_gdn_norm_w, m_gla_gate_w2, m_gla_gate_b, m_gla_norm_w, m_w_out, m_ffn_norm_w, m_w_gate, m_w_up, m_w_down, m_final_norm_w, v_meta_tokens, v_attn_norm_w, v_w_in, v_gdn_conv_w, v_gdn_a_log, v_gdn_dt_bias, v_gdn_norm_w, v_gla_gate_w2, v_gla_gate_b, v_gla_norm_w, v_w_out, v_ffn_norm_w, v_w_gate, v_w_up, v_w_down, v_final_norm_w):
    w = dict(meta_tokens=meta_tokens, attn_norm_w=attn_norm_w, w_in=w_in, gdn_conv_w=gdn_conv_w, gdn_a_log=gdn_a_log,
             gdn_dt_bias=gdn_dt_bias, gdn_norm_w=gdn_norm_w, gla_gate_w2=gla_gate_w2, gla_gate_b=gla_gate_b,
             gla_norm_w=gla_norm_w, w_out=w_out, ffn_norm_w=ffn_norm_w, w_gate=w_gate, w_up=w_up, w_down=w_down,
             final_norm_w=final_norm_w)
    m = dict(meta_tokens=m_meta_tokens, attn_norm_w=m_attn_norm_w, w_in=m_w_in, gdn_conv_w=m_gdn_conv_w,
             gdn_a_log=m_gdn_a_log, gdn_dt_bias=m_gdn_dt_bias, gdn_norm_w=m_gdn_norm_w, gla_gate_w2=m_gla_gate_w2,
             gla_gate_b=m_gla_gate_b, gla_norm_w=m_gla_norm_w, w_out=m_w_out, ffn_norm_w=m_ffn_norm_w,
             w_gate=m_w_gate, w_up=m_w_up, w_down=m_w_down, final_norm_w=m_final_norm_w)
    v = dict(meta_tokens=v_meta_tokens, attn_norm_w=v_attn_norm_w, w_in=v_w_in, gdn_conv_w=v_gdn_conv_w,
             gdn_a_log=v_gdn_a_log, gdn_dt_bias=v_gdn_dt_bias, gdn_norm_w=v_gdn_norm_w, gla_gate_w2=v_gla_gate_w2,
             gla_gate_b=v_gla_gate_b, gla_norm_w=v_gla_norm_w, w_out=v_w_out, ffn_norm_w=v_ffn_norm_w,
             w_gate=v_w_gate, w_up=v_w_up, w_down=v_w_down, final_norm_w=v_final_norm_w)
    chip = 2 * lax.axis_index("x") + lax.axis_index("y")

    def two_d(a):
        return a.reshape(1, -1) if a.ndim == 1 else a.reshape(-1, a.shape[-1])

    w2d = {k: two_d(a) for k, a in w.items()}
    big = _BIG_COLS + _BIG_ROWS
    small = tuple(k for k in _WEIGHTS if k not in big)

    (meta4,), = _comm_now("gather_meta", [_Gather([w2d["meta_tokens"]])])
    shard = {k: w2d[k].astype(bf16) for k in big}
    shard.update({k: w2d[k] for k in ("gdn_conv_w", "gla_gate_w2")})
    lossp, grad_x, received, g = _step(x, loss_target, {k: w2d[k] for k in small}, _cat_cols(meta4), shard)
    loss = lax.psum(lossp, ("x", "y", "c"))

    sizes = [g[k].size for k in small]
    total = sum(sizes)
    rows = -(-total // 1024)
    rows += (-rows) % 8
    packed = jnp.concatenate([g[k].reshape(-1) for k in small] + [jnp.zeros((rows * 1024 - total,), f32)])
    (packed8,), = _comm_now("exchange_small", [_Exchange([], [packed.reshape(rows, 1024)])])
    red = {k: h.reshape(w2d[k].shape) for k, h in received.items() if k != "w_in"}
    psum_small = _sum_slots(packed8, "sum_small").reshape(-1)
    off = 0
    for k, n in zip(small, sizes):
        a = psum_small[off:off + n].reshape(g[k].shape)
        off += n
        if k in _SMALL_SHARDED:
            c = w2d[k].shape[1]
            a = lax.dynamic_slice_in_dim(a, chip * c, c, axis=1)
        red[k] = a

    grads, deltas, new_m, new_v = [], [], [], []
    for k in _WEIGHTS:
        shape = w[k].shape
        if k == "w_in":
            flip = lambda a: jnp.swapaxes(a, 1, 2).reshape(shape[2], shape[1])
            unflip = lambda a: jnp.swapaxes(a.reshape(shape[0], shape[2], shape[1]), 1, 2)
            out = _adamw_transposed(received[k], flip(w[k]), flip(m[k]), flip(v[k]), "adamw_" + k)
            gk, d, m2, v2 = [unflip(a) for a in out]
        else:
            gk = red[k].reshape(shape)
            d, m2, v2 = [a.reshape(shape) for a in _adamw(red[k], w2d[k], two_d(m[k]), two_d(v[k]), "adamw_" + k)]
        grads.append(gk)
        deltas.append(d)
        new_m.append(m2)
        new_v.append(v2)
    return (loss, grad_x, *grads, *deltas, *new_m, *new_v)
```

```python
import functools

import jax
import jax.numpy as jnp
from jax import lax
from jax.experimental import pallas as pl
from jax.experimental.pallas import tpu as pltpu

f32 = jnp.float32
bf16 = jnp.bfloat16
HIGH = lax.Precision.HIGH
MESH = pl.DeviceIdType.MESH

N_META = 16
CONV_K = 4
GDN_H, GDN_DK, GDN_DV, GDN_C = 8, 128, 128, 64
GLA_H, GLA_DK, GLA_DV, GLA_C = 4, 128, 256, 16
GATE_RANK = 16
GATE_NORMALIZER = 16.0
EPS = 1e-6
GDN_QK = GDN_H * GDN_DK
GDN_V = GDN_H * GDN_DV
GLA_QK = GLA_H * GLA_DK
GLA_V = GLA_H * GLA_DV
PAD = (-N_META) % GDN_C
OFF = PAD + N_META
ROWS = 64

R_QKV, R_Z, R_AB, R_G, R_GR, R_LR, R_END = 0, 3072, 4096, 4112, 6160, 7184, 7200
C_ZR, C_G, C_QKV, C_END = 0, 2048, 4096, 7168
W_IN_SPLIT = 3072
ZR_W = GDN_V + GLA_V
G_W = 2 * GLA_QK + GLA_V
QKV_W = 2 * GDN_QK + GDN_V
Q0, K0, V0 = 0, GDN_QK, 2 * GDN_QK
SM_W = 128
SM_LR = 2 * GDN_H

ADAM_LR, ADAM_B1, ADAM_B2, ADAM_EPS, ADAM_WD, ADAM_STEP = 0.001, 0.9, 0.999, 1e-08, 0.01, 10

VMEM_LIMIT_V7X = 56 * 1024 * 1024
LANES = 128
N_DEV = 8
N_CHIP = 4


def _params(*sem):
    return pltpu.CompilerParams(dimension_semantics=sem, vmem_limit_bytes=VMEM_LIMIT_V7X)


def _tile(n, cap, mult=16):
    best = None
    for d in range(mult, min(n, cap) + 1, mult):
        if n % d == 0:
            best = d
    assert best is not None, (n, cap, mult)
    return best


NN = ((1,), (0,))
NT = ((1,), (1,))
TN = ((0,), (0,))


def _dot(a, b, dims, prec=None):
    return lax.dot_general(a, b, (dims, ((), ())), precision=prec, preferred_element_type=f32)


def _mmb(a, b, dims):
    return _dot(a.astype(bf16), b.astype(bf16), dims)


def _sigmoid(x):
    return jax.nn.sigmoid(x)


def _silu(x):
    return x * _sigmoid(x)


def _dsilu(x):
    s = _sigmoid(x)
    return s * (1.0 + x * (1.0 - s))


def _log1p_exp_neg_abs(x):
    t = jnp.exp(-jnp.abs(x))
    u = 1.0 + t
    d = u - 1.0
    return jnp.where(d == 0.0, t, jnp.log(u) * (t / jnp.where(d == 0.0, 1.0, d)))


def _softplus(x):
    return jnp.maximum(x, 0.0) + _log1p_exp_neg_abs(x)


def _log_sigmoid(x):
    return jnp.minimum(x, 0.0) - _log1p_exp_neg_abs(x)


def _rms(x):
    r = lax.rsqrt(jnp.mean(x * x, axis=-1, keepdims=True) + EPS)
    return x * r, r


def _rms_bwd(dy, xh, r, w):
    t = dy * w
    return r * (t - xh * jnp.mean(t * xh, axis=-1, keepdims=True))


def _l2n(x):
    return x * lax.rsqrt(jnp.sum(x * x, axis=-1, keepdims=True) + EPS)


INV_LEAF = 8


def _same_block(C, b):
    sh = b.bit_length() - 1
    row = lax.broadcasted_iota(jnp.int32, (C, C), 0)
    col = lax.broadcasted_iota(jnp.int32, (C, C), 1)
    return lax.shift_right_logical(row, sh) == lax.shift_right_logical(col, sh)


def _tri_inv_impl(As):
    C = As[0].shape[0]
    R = range(len(As))
    row = lax.broadcasted_iota(jnp.int32, (C, C), 0)
    col = lax.broadcasted_iota(jnp.int32, (C, C), 1)
    eye = (row == col).astype(f32)
    b = INV_LEAF
    inner = _same_block(C, b)
    leaf = [jnp.where(inner, As[h], 0.0) for h in R]
    d = [eye - leaf[h] for h in R]
    pw = leaf
    n = 2
    while n < b:
        pw = [_dot(pw[h], pw[h], NN, HIGH) for h in R]
        d = [_dot(d[h], eye + pw[h], NN, HIGH) for h in R]
        n *= 2
    while b < C:
        outer = _same_block(C, 2 * b)
        level = jnp.logical_and(outer, jnp.logical_not(inner))
        ed = [_dot(jnp.where(level, As[h], 0.0), d[h], NN, HIGH) for h in R]
        d = [d[h] - _dot(d[h], ed[h], NN, HIGH) for h in R]
        inner = outer
        b *= 2
    return d


@jax.custom_vjp
def _tri_inv(As):
    return _tri_inv_impl(As)


def _tri_inv_fwd(As):
    d = _tri_inv_impl(As)
    return d, d


def _tri_inv_bwd(d, g):
    R = range(len(d))
    t = [_dot(d[h], g[h], TN, HIGH) for h in R]
    return ([-_dot(t[h], d[h], NT, HIGH) for h in R],)


_tri_inv.defvjp(_tri_inv_fwd, _tri_inv_bwd)


@jax.custom_vjp
def _tri_inv_known(As, Ps):
    del As
    return Ps


def _tri_inv_known_fwd(As, Ps):
    del As
    return Ps, Ps


def _tri_inv_known_bwd(d, g):
    return _tri_inv_bwd(d, g)[0], [jnp.zeros_like(x) for x in d]


_tri_inv_known.defvjp(_tri_inv_known_fwd, _tri_inv_known_bwd)


def _gdn_chunk(Ss, qrs, krs, vs, betas, gs, Ps=None):
    H = len(Ss)
    C, dk = qrs[0].shape
    R = range(len(qrs))
    row = lax.broadcasted_iota(jnp.int32, (C, C), 0)
    col = lax.broadcasted_iota(jnp.int32, (C, C), 1)
    causal = row >= col
    strict = row > col
    cf = causal.astype(f32)
    q = [_l2n(qrs[h]) * (dk ** -0.5) for h in R]
    k = [_l2n(krs[h]) for h in R]
    mc = [_rows_exact(cf, jnp.broadcast_to(gs[h], (C, C))) for h in R]
    gc = [mc[h][:, 0:1] for h in R]
    decay = [jnp.where(causal, jnp.exp(jnp.where(causal, mc[h] - mc[h].T, 0.0)), 0.0) for h in R]
    kb = [k[h] * betas[h] for h in R]
    a = [jnp.where(strict, _mmb(kb[h], k[h], NT) * decay[h], 0.0) for h in R]
    p = _tri_inv(a) if Ps is None else _tri_inv_known(a, Ps)
    egc = [jnp.exp(gc[h]) for h in R]
    u = [_mmb(p[h], vs[h] * betas[h], NN) for h in R]
    w = [_mmb(p[h], kb[h] * egc[h], NN) for h in R]
    qk = [jnp.where(causal, _mmb(q[h], k[h], NT) * decay[h], 0.0) for h in R]
    qe = [q[h] * egc[h] for h in R]
    gl = [gc[h][C - 1:C, :] for h in R]
    kd = [k[h] * jnp.exp(gl[h] - gc[h]) for h in R]
    egl = [jnp.exp(gl[h]) for h in R]
    S, o, entering = list(Ss), [], []
    for chunk in range(len(qrs) // H):
        idx = [chunk * H + h for h in range(H)]
        entering += S
        v_new = [u[i] - _mmb(w[i], S[h], NN) for h, i in enumerate(idx)]
        o += [_mmb(qe[i], S[h], NN) + _mmb(qk[i], v_new[h], NN) for h, i in enumerate(idx)]
        S = [S[h] * egl[i] + _mmb(kd[i], v_new[h], TN) for h, i in enumerate(idx)]
    return S, o, p, entering


def _rows_exact_impl(m01, x, dims):
    m = m01.astype(bf16)
    x1 = x.astype(bf16)
    r1 = x - x1.astype(f32)
    x2 = r1.astype(bf16)
    x3 = (r1 - x2.astype(f32)).astype(bf16)
    d = lambda y: _dot(m, y, dims)
    return d(x1) + (d(x2) + d(x3))


@jax.custom_vjp
def _rows_exact(m01, x):
    return _rows_exact_impl(m01, x, NN)


def _rows_exact_fwd(m01, x):
    return _rows_exact_impl(m01, x, NN), m01


def _rows_exact_bwd(m01, g):
    return jnp.zeros_like(m01), _rows_exact_impl(m01, g, TN)


_rows_exact.defvjp(_rows_exact_fwd, _rows_exact_bwd)


def _gla_blocks(Sts, qrs, ks, vs, las):
    H = len(Sts)
    n = len(qrs)
    C, dk = qrs[0].shape
    R = range(n)
    row = lax.broadcasted_iota(jnp.int32, (C, C), 0)
    col = lax.broadcasted_iota(jnp.int32, (C, C), 1)
    ri = lax.broadcasted_iota(jnp.int32, (C, dk), 0)
    q = [qrs[h] * (dk ** -0.5) for h in R]
    running = (row >= col).astype(f32)
    b = [_rows_exact(running, las[h]) for h in R]
    sc = [jnp.where(row == col, jnp.sum(q[h] * ks[h], axis=-1, keepdims=True), 0.0) for h in R]
    s = C // 2
    while s >= 1:
        sh = s.bit_length() - 1
        ref = lax.shift_left(lax.shift_right_logical(row, sh + 1), sh + 1) + (s - 1)
        pick = (col == ref).astype(f32)
        bref = [_rows_exact(pick, b[h]) for h in R]
        upper = (lax.shift_right_logical(ri, sh) & 1) == 1
        qt = [jnp.where(upper, q[h] * jnp.exp(jnp.where(upper, b[h] - bref[h], 0.0)), 0.0) for h in R]
        kt = [jnp.where(upper, 0.0, ks[h] * jnp.exp(jnp.where(upper, 0.0, bref[h] - b[h]))) for h in R]
        same = lax.shift_right_logical(row, sh + 1) == lax.shift_right_logical(col, sh + 1)
        sc = [sc[h] + jnp.where(same, _mmb(qt[h], kt[h], NT), 0.0) for h in R]
        s //= 2
    o = [_mmb(sc[h], vs[h], NN) for h in R]
    qe = [q[h] * jnp.exp(b[h]) for h in R]
    bl = [b[h][C - 1:C, :] for h in R]
    upd = [_mmb(vs[h], ks[h] * jnp.exp(bl[h] - b[h]), TN) for h in R]
    ebl = [jnp.exp(bl[h]) for h in R]
    St = list(Sts)
    for blk in range(n // H):
        for h in range(H):
            i = blk * H + h
            o[i] = o[i] + _mmb(qe[i], St[h], NT)
        St = [St[h] * ebl[blk * H + h] + upd[blk * H + h] for h in range(H)]
    return St, o


_ANY = pl.BlockSpec(memory_space=pl.ANY)


class _Gather:
    def __init__(self, arrays, by_columns=False):
        self.arrays = list(arrays)
        self.n = len(self.arrays)
        self.by_columns = by_columns
        if by_columns:
            assert all(a.ndim == 2 and a.shape[1] % LANES == 0 for a in self.arrays)
            self.out_shape = [jax.ShapeDtypeStruct((a.shape[0], N_CHIP * a.shape[1]), a.dtype) for a in self.arrays]
        else:
            self.out_shape = [jax.ShapeDtypeStruct((N_CHIP,) + a.shape, a.dtype) for a in self.arrays]
        self.sems = [pltpu.SemaphoreType.DMA((self.n, 3)), pltpu.SemaphoreType.DMA((self.n, 3)),
                     pltpu.SemaphoreType.DMA((self.n,))]

    def hooks(self, ins, outs, send, recv, lsem):
        def place(a, chip):
            if not self.by_columns:
                return outs[a].at[chip]
            cols = self.arrays[a].shape[1]
            return outs[a].at[:, pl.ds(pl.multiple_of(chip * cols, LANES), cols)]

        def copies():
            x, y, c = lax.axis_index("x"), lax.axis_index("y"), lax.axis_index("c")
            me = 2 * x + y
            out = []
            for a in range(self.n):
                out.append((pltpu.make_async_copy(ins[a], place(a, me), lsem.at[a]), None))
                for j, (px, py) in enumerate([(1 - x, y), (x, 1 - y), (1 - x, 1 - y)]):
                    mk = lambda dst, a=a, j=j, px=px, py=py: pltpu.make_async_remote_copy(
                        src_ref=ins[a], dst_ref=dst, send_sem=send.at[a, j], recv_sem=recv.at[a, j],
                        device_id=(px, py, c), device_id_type=MESH)
                    out.append((mk(place(a, me)), mk(place(a, 2 * px + py))))
            return out

        return _start_wait(copies)


class _Exchange:
    def __init__(self, slotted, shared=(), by_chip=()):
        self.arrays = list(slotted) + list(by_chip) + list(shared)
        self.ns, self.nc = len(slotted), len(by_chip)
        self.n = len(self.arrays)
        self.out_shape = [jax.ShapeDtypeStruct(a.shape, a.dtype) for a in slotted]
        self.out_shape += [jax.ShapeDtypeStruct((N_DEV,) + a.shape[1:], a.dtype) for a in by_chip]
        self.out_shape += [jax.ShapeDtypeStruct((N_DEV,) + b.shape, b.dtype) for b in shared]
        self.sems = [pltpu.SemaphoreType.DMA((self.n, N_DEV - 1)), pltpu.SemaphoreType.DMA((self.n, N_DEV - 1)),
                     pltpu.SemaphoreType.DMA((self.n,))]

    def hooks(self, ins, outs, send, recv, lsem):
        def copies():
            x, y, c = lax.axis_index("x"), lax.axis_index("y"), lax.axis_index("c")
            me = 4 * x + 2 * y + c

            def src(a, dev):
                tx, ty, tc = dev
                if a < self.ns:
                    return ins[a].at[4 * tx + 2 * ty + tc]
                return ins[a].at[2 * tx + ty] if a < self.ns + self.nc else ins[a]

            out = []
            for a in range(self.n):
                out.append((pltpu.make_async_copy(src(a, (x, y, c)), outs[a].at[me], lsem.at[a]), None))
                for o in range(1, N_DEV):
                    dev = (1 - x if o & 4 else x, 1 - y if o & 2 else y, 1 - c if o & 1 else c)
                    t = 4 * dev[0] + 2 * dev[1] + dev[2]
                    mk = lambda dst, a=a, o=o, dev=dev: pltpu.make_async_remote_copy(
                        src_ref=src(a, dev), dst_ref=dst, send_sem=send.at[a, o - 1], recv_sem=recv.at[a, o - 1],
                        device_id=dev, device_id_type=MESH)
                    out.append((mk(outs[a].at[me]), mk(outs[a].at[t])))
            return out

        return _start_wait(copies)


class _Sides:
    def __init__(self, *members):
        self.members = members
        self.arrays = [a for s in members for a in s.arrays]
        self.n = len(self.arrays)
        self.out_shape = [sh for s in members for sh in s.out_shape]
        self.sems = [sm for s in members for sm in s.sems]

    def hooks(self, ins, outs, *sems):
        hooks, o = [], 0
        for i, s in enumerate(self.members):
            hooks.append(s.hooks(ins[o:o + s.n], outs[o:o + s.n], *sems[3 * i:3 * i + 3]))
            o += s.n

        def start():
            for st, _ in hooks:
                st()

        def wait():
            for _, wt in hooks:
                wt()

        return start, wait


def _start_wait(copies):
    def start():
        for s, _ in copies():
            s.start()

    def wait():
        for s, w in copies():
            (s if w is None else w).wait()

    return start, wait


def _call(body, *, name, grid, in_specs, out_specs, out_shape, args, sem, scratch_shapes=(), aliases=None, side=None):
    in_specs, out_specs, out_shape, args = list(in_specs), list(out_specs), list(out_shape), list(args)
    scratch_shapes = list(scratch_shapes)
    aliases = aliases or {}
    if side is None:
        return pl.pallas_call(
            body, name=name, grid=grid, in_specs=in_specs, out_specs=out_specs, out_shape=out_shape,
            scratch_shapes=scratch_shapes, input_output_aliases=aliases, compiler_params=_params(*sem))(*args)
    n_in, n_out, n_scr, ns = len(in_specs), len(out_specs), len(scratch_shapes), side.n

    def full_body(*refs):
        ins, refs = refs[:n_in], refs[n_in:]
        s_in, refs = refs[:ns], refs[ns:]
        outs, refs = refs[:n_out], refs[n_out:]
        s_out, refs = refs[:ns], refs[ns:]
        scr, sems = refs[:n_scr], refs[n_scr:]
        start, wait = side.hooks(s_in, s_out, *sems)
        ids = [pl.program_id(d) for d in range(len(grid))]
        first = functools.reduce(jnp.logical_and, [i == 0 for i in ids])
        last = functools.reduce(jnp.logical_and, [i == g - 1 for i, g in zip(ids, grid)])
        pl.when(first)(start)
        body(*ins, *outs, *scr)
        pl.when(last)(wait)

    return pl.pallas_call(
        full_body, name=name, grid=grid, in_specs=in_specs + [_ANY] * ns, out_specs=out_specs + [_ANY] * ns,
        out_shape=out_shape + side.out_shape, scratch_shapes=scratch_shapes + side.sems,
        input_output_aliases=aliases, compiler_params=_params(*(["arbitrary"] * len(grid))))(*args, *side.arrays)


WHOLE_K = dict(tm_cap=688, tn_cap=512, tk_cap=1 << 20)
WHOLE_K_T = dict(tm_cap=512, tn_cap=512, tk_cap=1 << 20)

def _mm(a, b, mode, name, *, tm_cap=1408, tn_cap=1024, tk_cap=2048, out_dtype=f32, acc_in=None, side=None,
        col_slabs=False, b_cols=None, plus=None):
    if mode == "nn":
        (M, K), (K2, N) = a.shape, b.shape
    elif mode == "nt":
        (M, K), (N, K2) = a.shape, b.shape
    else:
        (K, M), (K2, N) = a.shape, b.shape
    assert K == K2, (name, a.shape, b.shape)
    b_first = 0
    if b_cols is not None:
        assert mode != "nt"
        b_first, N = b_cols
    tm = _tile(M // 2 if col_slabs else M, tm_cap)
    tn = _tile(N // N_CHIP if col_slabs else N, tn_cap, 128)
    tk = _tile(K, tk_cap, 128 if K % 128 == 0 else 16)
    nk = K // tk
    dims = {"nn": NN, "nt": NT, "tn": TN}[mode]
    use_scratch = nk > 1 and out_dtype != f32

    def body(*refs):
        a_ref, b_ref, *rest = refs
        c_ref = rest.pop(0) if acc_in is not None else None
        p = _mmb(a_ref[...], b_ref[...], dims)
        if plus is not None:
            p = p + _mmb(rest.pop(0)[...], rest.pop(0)[...], dims)
        o_ref, *scr = rest
        if nk == 1:
            if c_ref is not None:
                p = p + c_ref[...]
            o_ref[...] = p.astype(out_dtype)
            return
        k = pl.program_id(2)
        acc = scr[0] if use_scratch else o_ref

        @pl.when(k == 0)
        def _():
            acc[...] = p if c_ref is None else p + c_ref[...]

        @pl.when(k > 0)
        def _():
            acc[...] += p

        if use_scratch:
            @pl.when(k == nk - 1)
            def _():
                o_ref[...] = acc[...].astype(out_dtype)

    if mode == "tn":
        a_spec = pl.BlockSpec((tk, tm), lambda i, j, k: (k, i))
    else:
        a_spec = pl.BlockSpec((tm, tk), lambda i, j, k: (i, k))
    if mode == "nt":
        b_spec = pl.BlockSpec((tn, tk), lambda i, j, k: (j, k))
    else:
        assert b_first % tn == 0, (name, b_first, tn)
        b_spec = pl.BlockSpec((tk, tn), lambda i, j, k: (k, j + b_first // tn))
    if col_slabs:
        assert acc_in is None
        ni, nj = M // 2 // tm, N // N_CHIP // tn
        o_spec = pl.BlockSpec((None, tm, tn), lambda i, j, k: (2 * (j // nj) + i // ni, i % ni, j % nj))
        o_shape = jax.ShapeDtypeStruct((N_DEV, M // 2, N // N_CHIP), out_dtype)
    else:
        o_spec = pl.BlockSpec((tm, tn), lambda i, j, k: (i, j))
        o_shape = jax.ShapeDtypeStruct((M, N), out_dtype)
    in_specs = [a_spec, b_spec]
    args = [a, b]
    if acc_in is not None:
        in_specs.append(o_spec)
        args.append(acc_in)
    if plus is not None:
        assert mode == "nt" and nk == 1, (name, mode, nk)
        k2 = plus[0].shape[1]
        assert plus[0].shape == (M, k2) and plus[1].shape == (N, k2), (name, plus[0].shape, plus[1].shape)
        in_specs += [pl.BlockSpec((tm, k2), lambda i, j, k: (i, 0)), pl.BlockSpec((tn, k2), lambda i, j, k: (j, 0))]
        args += list(plus)
    out = _call(body, name=name, grid=(M // tm, N // tn, nk), in_specs=in_specs, out_specs=[o_spec],
                out_shape=[o_shape], args=args,
                scratch_shapes=[pltpu.VMEM((tm, tn), f32)] if use_scratch else [],
                sem=("parallel", "parallel", "arbitrary"), side=side)
    return out[0] if side is None else (out[0], out[1:])


def _embed_norm(x3, m64, w, side=None):
    _, S, D = x3.shape
    Lp = OFF + S

    def body(x_ref, m_ref, w_ref, h_ref, n_ref):
        i = pl.program_id(0)
        h = jnp.where(i == 0, m_ref[...], x_ref[...])
        h_ref[...] = h
        xh, _ = _rms(h)
        n_ref[...] = (xh * w_ref[...]).astype(bf16)

    row = pl.BlockSpec((ROWS, D), lambda i: (i, 0))
    out = _call(
        body, name="embed_norm", grid=(Lp // ROWS,),
        in_specs=[pl.BlockSpec((None, ROWS, D), lambda i: (0, jnp.maximum(i - 1, 0), 0)),
                  pl.BlockSpec((ROWS, D), lambda i: (0, 0)),
                  pl.BlockSpec((1, D), lambda i: (0, 0))],
        out_specs=[row, row],
        out_shape=[jax.ShapeDtypeStruct((Lp, D), f32), jax.ShapeDtypeStruct((Lp, D), bf16)],
        args=[x3, m64, w], sem=("parallel",), side=side)
    return out[0], out[1], out[2:]


def _add_norm(h, d, w):
    Lp, D = h.shape
    tr = _tile(Lp, 256)

    def body(h_ref, d_ref, w_ref, o_ref, n_ref):
        h1 = h_ref[...] + d_ref[...]
        o_ref[...] = h1
        xh, _ = _rms(h1)
        n_ref[...] = (xh * w_ref[...]).astype(bf16)

    row = pl.BlockSpec((tr, D), lambda i: (i, 0))
    return pl.pallas_call(
        body, name="add_norm", grid=(Lp // tr,),
        in_specs=[row, row, pl.BlockSpec((1, D), lambda i: (0, 0))], out_specs=[row, row],
        out_shape=[jax.ShapeDtypeStruct((Lp, D), f32), jax.ShapeDtypeStruct((Lp, D), bf16)],
        compiler_params=_params("parallel"),
    )(h, d, w)


def _norm_bwd(dn, h, dh, w):
    Lp, D = h.shape
    tr = _tile(Lp, 256)

    def body(dn_ref, h_ref, dh_ref, w_ref, o_ref, ob_ref, gw_ref):
        i = pl.program_id(0)
        xh, r = _rms(h_ref[...])
        dn_ = dn_ref[...]
        o = dh_ref[...] + _rms_bwd(dn_, xh, r, w_ref[...])
        o_ref[...] = o
        ob_ref[...] = o.astype(bf16)
        gw = jnp.sum(dn_ * xh, axis=0, keepdims=True)

        @pl.when(i == 0)
        def _():
            gw_ref[...] = gw

        @pl.when(i > 0)
        def _():
            gw_ref[...] += gw

    row = pl.BlockSpec((tr, D), lambda i: (i, 0))
    vec = pl.BlockSpec((1, D), lambda i: (0, 0))
    return pl.pallas_call(
        body, name="norm_bwd", grid=(Lp // tr,), in_specs=[row, row, row, vec], out_specs=[row, row, vec],
        out_shape=[jax.ShapeDtypeStruct((Lp, D), f32), jax.ShapeDtypeStruct((Lp, D), bf16),
                   jax.ShapeDtypeStruct((1, D), f32)],
        compiler_params=_params("arbitrary"),
    )(dn, h, dh, w)


def _embed_norm_bwd(dn, h, dh, w, S, side=None):
    Lp, D = h.shape
    tr = _tile(S, 256, OFF)

    def body(dn_ref, h_ref, dh_ref, dn0_ref, h0_ref, dh0_ref, w_ref, gx_ref, gm_ref, gw_ref):
        i = pl.program_id(0)

        def rows(dn_, h_, dh_):
            xh, r = _rms(h_)
            return dh_ + _rms_bwd(dn_, xh, r, w_ref[...]), jnp.sum(dn_ * xh, axis=0, keepdims=True)

        d, gw = rows(dn_ref[...], h_ref[...], dh_ref[...])
        gx_ref[...] = d

        @pl.when(i == 0)
        def _():
            d0, gw0 = rows(dn0_ref[...], h0_ref[...], dh0_ref[...])
            gm_ref[...] = d0[PAD:OFF, :]
            gw_ref[...] = gw0 + gw

        @pl.when(i > 0)
        def _():
            gw_ref[...] += gw

    win = pl.BlockSpec((pl.Element(tr), pl.Element(D)), lambda i: (pl.multiple_of(OFF + i * tr, OFF), 0))
    head = pl.BlockSpec((OFF, D), lambda i: (0, 0))
    vec = pl.BlockSpec((1, D), lambda i: (0, 0))
    out = _call(
        body, name="embed_norm_bwd", grid=(S // tr,), in_specs=[win, win, win, head, head, head, vec],
        out_specs=[pl.BlockSpec((None, tr, D), lambda i: (0, i, 0)),
                   pl.BlockSpec((N_META, D), lambda i: (0, 0)), vec],
        out_shape=[jax.ShapeDtypeStruct((1, S, D), f32), jax.ShapeDtypeStruct((N_META, D), f32),
                   jax.ShapeDtypeStruct((1, D), f32)],
        args=[dn, h, dh, dn, h, dh, w], sem=("arbitrary",), side=side)
    return out[0], out[1], out[2], out[3:]


def _final(h1, ffn, tgt3, w):
    Lp, D = h1.shape
    _, S, _ = tgt3.shape
    tr = _tile(Lp, min(256, S), OFF)

    def body(h_ref, f_ref, t_ref, w_ref, d_ref, db_ref, l_ref, gw_ref):
        i = pl.program_id(0)
        h2 = h_ref[...] + f_ref[...]
        xh, r = _rms(h2)
        w_ = w_ref[...]
        t = t_ref[...]
        t = jnp.where(i == 0, pltpu.roll(t, OFF, 0), t)
        valid = (lax.broadcasted_iota(jnp.int32, (tr, 1), 0) + i * tr >= OFF).astype(f32)
        e = xh * w_ - t
        loss = 0.5 * jnp.sum(jnp.mean(e * e, axis=-1, keepdims=True) * valid, axis=0, keepdims=True)
        dy = e * (valid / D)
        d = _rms_bwd(dy, xh, r, w_)
        d_ref[...] = d
        db_ref[...] = d.astype(bf16)
        gw = jnp.sum(dy * xh, axis=0, keepdims=True)

        @pl.when(i == 0)
        def _():
            l_ref[...] = jnp.zeros_like(l_ref)
            gw_ref[...] = jnp.zeros_like(gw_ref)

        l_ref[...] += jnp.broadcast_to(loss, l_ref.shape)
        gw_ref[...] += gw

    row = pl.BlockSpec((tr, D), lambda i: (i, 0))
    vec = pl.BlockSpec((1, D), lambda i: (0, 0))
    tgt = pl.BlockSpec((pl.Element(tr), pl.Element(D)),
                       lambda i: (pl.multiple_of(jnp.maximum(i * tr - OFF, 0), OFF), 0))
    return pl.pallas_call(
        body, name="final_loss", grid=(Lp // tr,), in_specs=[row, row, tgt, vec],
        out_specs=[row, row, pl.BlockSpec((8, 128), lambda i: (0, 0)), vec],
        out_shape=[jax.ShapeDtypeStruct((Lp, D), f32), jax.ShapeDtypeStruct((Lp, D), bf16),
                   jax.ShapeDtypeStruct((8, 128), f32), jax.ShapeDtypeStruct((1, D), f32)],
        compiler_params=_params("arbitrary"),
    )(h1, ffn, tgt3.reshape(S, D), w)


def _ffn_in(n, w_gate, w_up, side=None):
    M, K = n.shape
    F = w_gate.shape[1]
    tm = _tile(M, 1408)
    tn = _tile(F, 512, 128)

    def body(a_ref, bg_ref, bu_ref, act_ref, pg_ref, pu_ref):
        a = a_ref[...]
        g = _mmb(a, bg_ref[...], NN)
        u = _mmb(a, bu_ref[...], NN)
        s = _sigmoid(g)
        gs = g * s
        act_ref[...] = (gs * u).astype(bf16)
        pg_ref[...] = (u * (s + gs * (1.0 - s))).astype(bf16)
        pu_ref[...] = gs.astype(bf16)

    wsp = pl.BlockSpec((K, tn), lambda i, j: (0, j))
    osp = pl.BlockSpec((tm, tn), lambda i, j: (i, j))
    out = _call(body, name="ffn_in", grid=(M // tm, F // tn),
                in_specs=[pl.BlockSpec((tm, K), lambda i, j: (i, 0)), wsp, wsp], out_specs=[osp] * 3,
                out_shape=[jax.ShapeDtypeStruct((M, F), bf16)] * 3, args=[n, w_gate, w_up],
                sem=("parallel", "parallel"), side=side)
    return out[0], out[1], out[2], out[3:]


def _ffn_dact(d, w_down, pg, pu):
    M, K = d.shape
    F = w_down.shape[0]
    tm = _tile(M, 1408)
    tn = _tile(F, 512, 128)

    def body(d_ref, w_ref, pg_ref, pu_ref, dg_ref, du_ref):
        da = _mmb(d_ref[...], w_ref[...], NT)
        dg_ref[...] = (da * pg_ref[...].astype(f32)).astype(bf16)
        du_ref[...] = (da * pu_ref[...].astype(f32)).astype(bf16)

    osp = pl.BlockSpec((tm, tn), lambda i, j: (i, j))
    return pl.pallas_call(
        body, name="ffn_dact", grid=(M // tm, F // tn),
        in_specs=[pl.BlockSpec((tm, K), lambda i, j: (i, 0)), pl.BlockSpec((tn, K), lambda i, j: (j, 0)), osp, osp],
        out_specs=[osp, osp], out_shape=[jax.ShapeDtypeStruct((M, F), bf16)] * 2,
        compiler_params=_params("parallel", "parallel"),
    )(d, w_down, pg, pu)


def _ffn_dn(dg, du, w_gate, w_up):
    M, F = dg.shape
    D = w_gate.shape[0]
    tm = _tile(M, 688)
    tn = _tile(D, 256, 128)

    def body(dg_ref, du_ref, wg_ref, wu_ref, o_ref):
        o_ref[...] = _mmb(dg_ref[...], wg_ref[...], NT) + _mmb(du_ref[...], wu_ref[...], NT)

    asp = pl.BlockSpec((tm, F), lambda i, j: (i, 0))
    wsp = pl.BlockSpec((tn, F), lambda i, j: (j, 0))
    return pl.pallas_call(
        body, name="d_n2", grid=(M // tm, D // tn), in_specs=[asp, asp, wsp, wsp],
        out_specs=pl.BlockSpec((tm, tn), lambda i, j: (i, j)), out_shape=jax.ShapeDtypeStruct((M, D), f32),
        compiler_params=_params("parallel", "parallel"),
    )(dg, du, w_gate, w_up)


def _gates(psm, w2p, gate_b, alog, dtb):
    Lp = psm.shape[0]
    tr = _tile(Lp, 256)

    def body(p_ref, w_ref, b_ref, a_ref, t_ref, gb_ref, la_ref):
        i = pl.program_id(0)
        psm_ = p_ref[...]
        lane = lax.broadcasted_iota(jnp.int32, psm_.shape, 1)
        rowi = lax.broadcasted_iota(jnp.int32, (tr, 1), 0) + i * tr
        g = -jnp.exp(a_ref[...]) * _softplus(psm_ + t_ref[...])
        beta = _sigmoid(psm_)
        gb = jnp.where(lane < GDN_H, g, jnp.where(lane < 2 * GDN_H, beta, 0.0))
        gb_ref[...] = gb * (rowi >= PAD).astype(f32)
        logit = _mmb(psm_, w_ref[...], NN) + b_ref[...]
        la_ref[...] = _log_sigmoid(logit) * (1.0 / GATE_NORMALIZER)

    row = pl.BlockSpec((tr, SM_W), lambda i: (i, 0))
    return pl.pallas_call(
        body, name="gates", grid=(Lp // tr,),
        in_specs=[row, pl.BlockSpec((SM_W, GLA_QK), lambda i: (0, 0)), pl.BlockSpec((1, GLA_QK), lambda i: (0, 0)),
                  pl.BlockSpec((1, SM_W), lambda i: (0, 0)), pl.BlockSpec((1, SM_W), lambda i: (0, 0))],
        out_specs=[row, pl.BlockSpec((tr, GLA_QK), lambda i: (i, 0))],
        out_shape=[jax.ShapeDtypeStruct((Lp, SM_W), f32), jax.ShapeDtypeStruct((Lp, GLA_QK), f32)],
        compiler_params=_params("parallel"),
    )(psm, w2p, gate_b, alog, dtb)


def _gates_bwd(psm, w2p, gate_b, alog, dtb, dgb, dla):
    Lp = psm.shape[0]
    tr = _tile(Lp, 256)

    def body(p_ref, w_ref, b_ref, a_ref, t_ref, dgb_ref, dla_ref, dp_ref, gw_ref, gb_ref, ga_ref, gt_ref):
        i = pl.program_id(0)
        psm_ = p_ref[...]
        lane = lax.broadcasted_iota(jnp.int32, psm_.shape, 1)
        rowi = lax.broadcasted_iota(jnp.int32, (tr, 1), 0) + i * tr
        d = dgb_ref[...] * (rowi >= PAD).astype(f32)
        ea = jnp.exp(a_ref[...])
        z = psm_ + t_ref[...]
        is_g = lane < GDN_H
        dz = jnp.where(is_g, -ea * _sigmoid(z) * d, 0.0)
        dalog = jnp.where(is_g, -ea * _softplus(z) * d, 0.0)
        beta = _sigmoid(psm_)
        dbeta = jnp.where(jnp.logical_and(lane >= GDN_H, lane < 2 * GDN_H), beta * (1.0 - beta) * d, 0.0)
        logit = _mmb(psm_, w_ref[...], NN) + b_ref[...]
        dlogit = dla_ref[...] * (_sigmoid(-logit) * (1.0 / GATE_NORMALIZER))
        dlr = _mmb(dlogit, w_ref[...], NT)
        dp_ref[...] = (dz + dbeta + dlr).astype(bf16)
        gw = _mmb(psm_, dlogit, TN)
        gb = jnp.sum(dlogit, axis=0, keepdims=True)
        ga = jnp.sum(dalog, axis=0, keepdims=True)
        gt = jnp.sum(dz, axis=0, keepdims=True)

        @pl.when(i == 0)
        def _():
            gw_ref[...] = gw
            gb_ref[...] = gb
            ga_ref[...] = ga
            gt_ref[...] = gt

        @pl.when(i > 0)
        def _():
            gw_ref[...] += gw
            gb_ref[...] += gb
            ga_ref[...] += ga
            gt_ref[...] += gt

    row = pl.BlockSpec((tr, SM_W), lambda i: (i, 0))
    wsp = pl.BlockSpec((SM_W, GLA_QK), lambda i: (0, 0))
    bsp = pl.BlockSpec((1, GLA_QK), lambda i: (0, 0))
    vsp = pl.BlockSpec((1, SM_W), lambda i: (0, 0))
    return pl.pallas_call(
        body, name="gates_bwd", grid=(Lp // tr,),
        in_specs=[row, wsp, bsp, vsp, vsp, row, pl.BlockSpec((tr, GLA_QK), lambda i: (i, 0))],
        out_specs=[row, wsp, bsp, vsp, vsp],
        out_shape=[jax.ShapeDtypeStruct((Lp, SM_W), bf16), jax.ShapeDtypeStruct((SM_W, GLA_QK), f32),
                   jax.ShapeDtypeStruct((1, GLA_QK), f32), jax.ShapeDtypeStruct((1, SM_W), f32),
                   jax.ShapeDtypeStruct((1, SM_W), f32)],
        compiler_params=_params("arbitrary"),
    )(psm, w2p, gate_b, alog, dtb, dgb, dla)


def _conv_pre(x_ext, w, n):
    rows = x_ext.shape[0]
    y = x_ext * w[CONV_K - 1:CONV_K, :]
    for s in range(1, CONV_K):
        y = y + pltpu.roll(x_ext, s, 0) * w[CONV_K - 1 - s:CONV_K - s, :]
    return y[rows - n:, :]


def _conv(proj, cw, side=None):
    Lp = proj.shape[0]
    W = cw.shape[1]
    tr = _tile(Lp, 256, 64)
    tc = _tile(W, 1024, 128)
    c0 = C_QKV // tc

    def body(h_ref, x_ref, w_ref, o_ref):
        i = pl.program_id(1)
        halo = jnp.where(i == 0, 0.0, h_ref[...])
        x_ext = jnp.concatenate([halo, x_ref[...]], axis=0)
        o_ref[...] = _silu(_conv_pre(x_ext, w_ref[...], tr))

    out = _call(
        body, name="conv", grid=(W // tc, Lp // tr),
        in_specs=[pl.BlockSpec((8, tc), lambda j, i: (jnp.maximum(i * (tr // 8) - 1, 0), j + c0)),
                  pl.BlockSpec((tr, tc), lambda j, i: (i, j + c0)),
                  pl.BlockSpec((CONV_K, tc), lambda j, i: (0, j))],
        out_specs=[pl.BlockSpec((tr, tc), lambda j, i: (i, j))],
        out_shape=[jax.ShapeDtypeStruct((Lp, W), f32)], args=[proj, proj, cw],
        sem=("parallel", "parallel"), side=side)
    return out[0] if side is None else (out[0], out[1:])


def _conv_bwd(proj, cw, dy, dproj, side=None):
    Lp = proj.shape[0]
    W = cw.shape[1]
    tr = _tile(Lp, 256, 64)
    tc = _tile(W, 1024, 128)
    c0 = C_QKV // tc
    nr = Lp // tr
    last8 = Lp // 8 - 1

    def body(xp_ref, x_ref, xn_ref, w_ref, d_ref, dn_ref, dproj_ref, o_ref, gw_ref):
        del dproj_ref
        i = pl.program_id(1)
        w = w_ref[...]
        xp = jnp.where(i == 0, 0.0, xp_ref[...])
        x_ext = jnp.concatenate([xp, x_ref[...], xn_ref[...]], axis=0)
        n = tr + 8
        pre = _conv_pre(x_ext, w, n)
        dn = jnp.where(i == nr - 1, 0.0, dn_ref[...])
        dpre = jnp.concatenate([d_ref[...], dn], axis=0) * _dsilu(pre)
        dx = dpre * w[CONV_K - 1:CONV_K, :]
        for s in range(1, CONV_K):
            dx = dx + pltpu.roll(dpre, n - s, 0) * w[CONV_K - 1 - s:CONV_K - s, :]
        o_ref[...] = dx[:tr, :].astype(bf16)
        dp = dpre[:tr, :]
        rows = []
        for k in range(CONV_K):
            xs = x_ext if k == CONV_K - 1 else pltpu.roll(x_ext, CONV_K - 1 - k, 0)
            rows.append(jnp.sum(dp * xs[8:8 + tr, :], axis=0, keepdims=True))
        gw = jnp.concatenate(rows, axis=0)

        @pl.when(i == 0)
        def _():
            gw_ref[...] = gw

        @pl.when(i > 0)
        def _():
            gw_ref[...] += gw

    cur = pl.BlockSpec((tr, tc), lambda j, i: (i, j))
    nxt = pl.BlockSpec((8, tc), lambda j, i: (jnp.minimum((i + 1) * (tr // 8), last8), j))
    pcur = pl.BlockSpec((tr, tc), lambda j, i: (i, j + c0))
    pprev = pl.BlockSpec((8, tc), lambda j, i: (jnp.maximum(i * (tr // 8) - 1, 0), j + c0))
    pnext = pl.BlockSpec((8, tc), lambda j, i: (jnp.minimum((i + 1) * (tr // 8), last8), j + c0))
    wsp = pl.BlockSpec((CONV_K, tc), lambda j, i: (0, j))
    out = _call(
        body, name="conv_bwd", grid=(W // tc, nr),
        in_specs=[pprev, pcur, pnext, wsp, cur, nxt, _ANY], out_specs=[pcur, wsp],
        out_shape=[jax.ShapeDtypeStruct(dproj.shape, dproj.dtype), jax.ShapeDtypeStruct((CONV_K, W), f32)],
        aliases={6: 0}, args=[proj, proj, proj, cw, dy, dy, dproj], sem=("parallel", "arbitrary"), side=side)
    return out[0], out[1], out[2:]


GDN_FWD_GROUP = 3


def _gdn_group(Lp, most):
    n = Lp // GDN_C
    return next(g for g in range(most, 0, -1) if n % g == 0)


def _gdn_heads(x_ref, gb_ref, group):
    qs, ks, vs, bs, gs = [], [], [], [], []
    for chunk in range(group):
        r = slice(chunk * GDN_C, (chunk + 1) * GDN_C)
        gbv = gb_ref[r, :]
        for h in range(GDN_H):
            qs.append(x_ref[r, Q0 + h * GDN_DK:Q0 + (h + 1) * GDN_DK])
            ks.append(x_ref[r, K0 + h * GDN_DK:K0 + (h + 1) * GDN_DK])
            vs.append(x_ref[r, V0 + h * GDN_DV:V0 + (h + 1) * GDN_DV])
            bs.append(gbv[:, GDN_H + h:GDN_H + h + 1])
            gs.append(gbv[:, h:h + 1])
    return qs, ks, vs, bs, gs


def _gdn_fwd(qkvc, gb, side=None):
    Lp = qkvc.shape[0]
    group = _gdn_group(Lp, GDN_FWD_GROUP)
    rows = group * GDN_C
    steps = Lp // rows
    R = range(GDN_H)

    def body(x_ref, gb_ref, o_ref, sall_ref, pall_ref, s_scr):
        @pl.when(pl.program_id(0) == 0)
        def _():
            s_scr[...] = jnp.zeros_like(s_scr)

        S2, o, p, entering = _gdn_chunk([s_scr[h] for h in R], *_gdn_heads(x_ref, gb_ref, group))
        for h in R:
            s_scr[h] = S2[h]
        for chunk in range(group):
            for h in R:
                i = chunk * GDN_H + h
                o_ref[chunk * GDN_C:(chunk + 1) * GDN_C, h * GDN_DV:(h + 1) * GDN_DV] = o[i]
                pall_ref[chunk, h] = p[i]
                sall_ref[chunk, h] = entering[i]

    out = _call(
        body, name="gdn_fwd", grid=(steps,),
        in_specs=[pl.BlockSpec((rows, QKV_W), lambda n: (n, 0)), pl.BlockSpec((rows, SM_W), lambda n: (n, 0))],
        out_specs=[pl.BlockSpec((rows, GDN_V), lambda n: (n, 0)),
                   pl.BlockSpec((group, GDN_H, GDN_DK, GDN_DV), lambda n: (n, 0, 0, 0)),
                   pl.BlockSpec((group, GDN_H, GDN_C, GDN_C), lambda n: (n, 0, 0, 0))],
        out_shape=[jax.ShapeDtypeStruct((Lp, GDN_V), f32),
                   jax.ShapeDtypeStruct((Lp // GDN_C, GDN_H, GDN_DK, GDN_DV), f32),
                   jax.ShapeDtypeStruct((Lp // GDN_C, GDN_H, GDN_C, GDN_C), f32)],
        scratch_shapes=[pltpu.VMEM((GDN_H, GDN_DK, GDN_DV), f32)], args=[qkvc, gb], sem=("arbitrary",), side=side)
    return out[0], out[1], out[2], out[3:]


def _gdn_bwd(qkvc, gb, sall, pall, do, side=None):
    Lp = qkvc.shape[0]
    group = 1
    rows = group * GDN_C
    steps = Lp // rows
    R = range(GDN_H)

    def body(x_ref, gb_ref, sall_ref, pall_ref, do_ref, dx_ref, dgb_ref, ds_scr):
        @pl.when(pl.program_id(0) == 0)
        def _():
            ds_scr[...] = jnp.zeros_like(ds_scr)

        lane = lax.broadcasted_iota(jnp.int32, (GDN_C, SM_W), 1)
        ps = [pall_ref[chunk, h] for chunk in range(group) for h in R]
        _, vjp = jax.vjp(lambda *a: _gdn_chunk(*a, Ps=ps)[:2],
                         [sall_ref[0, h] for h in R], *_gdn_heads(x_ref, gb_ref, group))
        do = [do_ref[chunk * GDN_C:(chunk + 1) * GDN_C, h * GDN_DV:(h + 1) * GDN_DV]
              for chunk in range(group) for h in R]
        dS, dq, dk, dv, dbeta, dg = vjp(([ds_scr[h] for h in R], do))
        for h in R:
            ds_scr[h] = dS[h]
        for chunk in range(group):
            r = slice(chunk * GDN_C, (chunk + 1) * GDN_C)
            acc = jnp.zeros((GDN_C, SM_W), f32)
            for h in R:
                i = chunk * GDN_H + h
                dx_ref[r, Q0 + h * GDN_DK:Q0 + (h + 1) * GDN_DK] = dq[i]
                dx_ref[r, K0 + h * GDN_DK:K0 + (h + 1) * GDN_DK] = dk[i]
                dx_ref[r, V0 + h * GDN_DV:V0 + (h + 1) * GDN_DV] = dv[i]
                acc = acc + jnp.where(lane == h, dg[i], 0.0) + jnp.where(lane == GDN_H + h, dbeta[i], 0.0)
            dgb_ref[r, :] = acc

    rev = lambda n: (steps - 1 - n, 0)
    out = _call(
        body, name="gdn_bwd", grid=(steps,),
        in_specs=[pl.BlockSpec((rows, QKV_W), rev), pl.BlockSpec((rows, SM_W), rev),
                  pl.BlockSpec((1, GDN_H, GDN_DK, GDN_DV), lambda n: (steps - 1 - n, 0, 0, 0)),
                  pl.BlockSpec((group, GDN_H, GDN_C, GDN_C), lambda n: (steps - 1 - n, 0, 0, 0)),
                  pl.BlockSpec((rows, GDN_V), rev)],
        out_specs=[pl.BlockSpec((rows, QKV_W), rev), pl.BlockSpec((rows, SM_W), rev)],
        out_shape=[jax.ShapeDtypeStruct((Lp, QKV_W), f32), jax.ShapeDtypeStruct((Lp, SM_W), f32)],
        scratch_shapes=[pltpu.VMEM((GDN_H, GDN_DK, GDN_DV), f32)], args=[qkvc, gb, sall, pall, do],
        sem=("arbitrary",), side=side)
    return out[0], out[1], out[2:]


GLA_BLOCK = 64


def _gla_group(Lp):
    nb = Lp // GLA_BLOCK
    return next(g for g in (3, 2, 1) if nb % g == 0)


def _gla_slices(h):
    sq = slice(h * GLA_DK, (h + 1) * GLA_DK)
    sk = slice(GLA_QK + h * GLA_DK, GLA_QK + (h + 1) * GLA_DK)
    sv = slice(2 * GLA_QK + h * GLA_DV, 2 * GLA_QK + (h + 1) * GLA_DV)
    return sq, sk, sv


def _gla_heads(x_ref, la_ref, group):
    qs, ks, vs, ls = [], [], [], []
    for blk in range(group):
        r = slice(blk * GLA_BLOCK, (blk + 1) * GLA_BLOCK)
        for h in range(GLA_H):
            sq, sk, sv = _gla_slices(h)
            qs.append(x_ref[r, sq])
            ks.append(x_ref[r, sk])
            vs.append(x_ref[r, sv])
            ls.append(la_ref[r, sq])
    return qs, ks, vs, ls


def _gla_fwd(proj, la):
    Lp = proj.shape[0]
    group = _gla_group(Lp)
    rows = group * GLA_BLOCK
    steps = Lp // rows
    R = range(GLA_H)

    def body(x_ref, la_ref, o_ref, sall_ref, s_scr):
        @pl.when(pl.program_id(0) == 0)
        def _():
            s_scr[...] = jnp.zeros_like(s_scr)

        Sts = [s_scr[h] for h in R]
        for h in R:
            sall_ref[0, h] = Sts[h]
        St2, o = _gla_blocks(Sts, *_gla_heads(x_ref, la_ref, group))
        for h in R:
            s_scr[h] = St2[h]
        for blk in range(group):
            for h in R:
                o_ref[blk * GLA_BLOCK:(blk + 1) * GLA_BLOCK, h * GLA_DV:(h + 1) * GLA_DV] = o[blk * GLA_H + h]

    return pl.pallas_call(
        body, name="gla_fwd", grid=(steps,),
        in_specs=[pl.BlockSpec((rows, G_W), lambda n: (n, C_G // G_W)),
                  pl.BlockSpec((rows, GLA_QK), lambda n: (n, 0))],
        out_specs=[pl.BlockSpec((rows, GLA_V), lambda n: (n, 0)),
                   pl.BlockSpec((1, GLA_H, GLA_DV, GLA_DK), lambda n: (n, 0, 0, 0))],
        out_shape=[jax.ShapeDtypeStruct((Lp, GLA_V), f32),
                   jax.ShapeDtypeStruct((steps, GLA_H, GLA_DV, GLA_DK), f32)],
        scratch_shapes=[pltpu.VMEM((GLA_H, GLA_DV, GLA_DK), f32)],
        compiler_params=_params("arbitrary"),
    )(proj, la)


def _gla_bwd(proj, la, sall, do, dproj, side=None):
    Lp = proj.shape[0]
    group = _gla_group(Lp)
    rows = group * GLA_BLOCK
    steps = Lp // rows
    R = range(GLA_H)

    def body(x_ref, la_ref, sall_ref, do_ref, dproj_ref, dx_ref, dla_ref, ds_scr):
        del dproj_ref

        @pl.when(pl.program_id(0) == 0)
        def _():
            ds_scr[...] = jnp.zeros_like(ds_scr)

        _, vjp = jax.vjp(_gla_blocks, [sall_ref[0, h] for h in R], *_gla_heads(x_ref, la_ref, group))
        do = [do_ref[blk * GLA_BLOCK:(blk + 1) * GLA_BLOCK, h * GLA_DV:(h + 1) * GLA_DV]
              for blk in range(group) for h in R]
        dS, dq, dk, dv, dl = vjp(([ds_scr[h] for h in R], do))
        for h in R:
            ds_scr[h] = dS[h]
        for blk in range(group):
            r = slice(blk * GLA_BLOCK, (blk + 1) * GLA_BLOCK)
            for h in R:
                sq, sk, sv = _gla_slices(h)
                i = blk * GLA_H + h
                dx_ref[r, sq] = dq[i].astype(bf16)
                dx_ref[r, sk] = dk[i].astype(bf16)
                dx_ref[r, sv] = dv[i].astype(bf16)
                dla_ref[r, sq] = dl[i]

    x_spec = pl.BlockSpec((rows, G_W), lambda n: (steps - 1 - n, C_G // G_W))
    rev = lambda n: (steps - 1 - n, 0)
    out = _call(
        body, name="gla_bwd", grid=(steps,),
        in_specs=[x_spec, pl.BlockSpec((rows, GLA_QK), rev),
                  pl.BlockSpec((1, GLA_H, GLA_DV, GLA_DK), lambda n: (steps - 1 - n, 0, 0, 0)),
                  pl.BlockSpec((rows, GLA_V), rev), _ANY],
        out_specs=[x_spec, pl.BlockSpec((rows, GLA_QK), rev)],
        out_shape=[jax.ShapeDtypeStruct(dproj.shape, dproj.dtype), jax.ShapeDtypeStruct((Lp, GLA_QK), f32)],
        aliases={4: 0}, scratch_shapes=[pltpu.VMEM((GLA_H, GLA_DV, GLA_DK), f32)],
        args=[proj, la, sall, do, dproj], sem=("arbitrary",), side=side)
    return out[0], out[1], out[2:]


def _gated_norm_fn(og, ol, zr, wg, wl):
    outs = []
    for h in range(GDN_H):
        s = slice(h * GDN_DV, (h + 1) * GDN_DV)
        outs.append(_rms(og[:, s])[0] * wg * _silu(zr[:, s]))
    for h in range(GLA_H):
        s = slice(h * GLA_DV, (h + 1) * GLA_DV)
        sr = slice(GDN_V + h * GLA_DV, GDN_V + (h + 1) * GLA_DV)
        outs.append(_rms(ol[:, s])[0] * wl * _silu(zr[:, sr]))
    return jnp.concatenate(outs, axis=-1)


def _gated_norm(og, ol, proj, wg, wl):
    Lp = og.shape[0]
    tr = _tile(Lp, 256)

    def body(og_ref, ol_ref, zr_ref, wg_ref, wl_ref, o_ref):
        o_ref[...] = _gated_norm_fn(og_ref[...], ol_ref[...], zr_ref[...], wg_ref[...], wl_ref[...]).astype(bf16)

    return pl.pallas_call(
        body, name="gated_norm", grid=(Lp // tr,),
        in_specs=[pl.BlockSpec((tr, GDN_V), lambda i: (i, 0)), pl.BlockSpec((tr, GLA_V), lambda i: (i, 0)),
                  pl.BlockSpec((tr, ZR_W), lambda i: (i, C_ZR // ZR_W)),
                  pl.BlockSpec((1, GDN_DV), lambda i: (0, 0)), pl.BlockSpec((1, GLA_DV), lambda i: (0, 0))],
        out_specs=pl.BlockSpec((tr, ZR_W), lambda i: (i, 0)),
        out_shape=jax.ShapeDtypeStruct((Lp, ZR_W), bf16),
        compiler_params=_params("parallel"),
    )(og, ol, proj, wg, wl)


def _gated_norm_bwd(og, ol, proj, wg, wl, dmix):
    Lp = og.shape[0]
    tr = _tile(Lp, 256)

    def body(og_ref, ol_ref, zr_ref, wg_ref, wl_ref, d_ref, dog_ref, dol_ref, dzr_ref, gwg_ref, gwl_ref):
        i = pl.program_id(0)
        _, vjp = jax.vjp(_gated_norm_fn, og_ref[...], ol_ref[...], zr_ref[...], wg_ref[...], wl_ref[...])
        dog, dol, dzr, gwg, gwl = vjp(d_ref[...])
        dog_ref[...] = dog
        dol_ref[...] = dol
        dzr_ref[...] = dzr.astype(bf16)

        @pl.when(i == 0)
        def _():
            gwg_ref[...] = gwg
            gwl_ref[...] = gwl

        @pl.when(i > 0)
        def _():
            gwg_ref[...] += gwg
            gwl_ref[...] += gwl

    og_spec = pl.BlockSpec((tr, GDN_V), lambda i: (i, 0))
    ol_spec = pl.BlockSpec((tr, GLA_V), lambda i: (i, 0))
    zr_spec = pl.BlockSpec((tr, ZR_W), lambda i: (i, C_ZR // ZR_W))
    vg = pl.BlockSpec((1, GDN_DV), lambda i: (0, 0))
    vl = pl.BlockSpec((1, GLA_DV), lambda i: (0, 0))
    return pl.pallas_call(
        body, name="gated_norm_bwd", grid=(Lp // tr,),
        in_specs=[og_spec, ol_spec, zr_spec, vg, vl, pl.BlockSpec((tr, ZR_W), lambda i: (i, 0))],
        out_specs=[og_spec, ol_spec, zr_spec, vg, vl],
        out_shape=[jax.ShapeDtypeStruct((Lp, GDN_V), f32), jax.ShapeDtypeStruct((Lp, GLA_V), f32),
                   jax.ShapeDtypeStruct((Lp, C_END), bf16),
                   jax.ShapeDtypeStruct((1, GDN_DV), f32), jax.ShapeDtypeStruct((1, GLA_DV), f32)],
        compiler_params=_params("arbitrary"),
    )(og, ol, proj, wg, wl, dmix)


def _adamw_rule(g_, w_, m_, v_):
    c1 = 1.0 - ADAM_B1 ** ADAM_STEP
    c2 = 1.0 - ADAM_B2 ** ADAM_STEP
    m2 = ADAM_B1 * m_ + (1.0 - ADAM_B1) * g_
    v2 = ADAM_B2 * v_ + (1.0 - ADAM_B2) * (g_ * g_)
    return -ADAM_LR * ((m2 / c1) / (jnp.sqrt(v2 / c2) + ADAM_EPS) + ADAM_WD * w_), m2, v2


def _adamw(g, w, m, v, name, copy_g=False):
    R, C = g.shape
    tr = _tile(R, 256, 8) if R % 8 == 0 and R > 256 else R

    def body(g_ref, w_ref, m_ref, v_ref, *o_refs):
        g_ = g_ref[...]
        if copy_g:
            o_refs[0][...] = g_
        d_ref, mo_ref, vo_ref = o_refs[-3:]
        d_ref[...], mo_ref[...], vo_ref[...] = _adamw_rule(g_, w_ref[...], m_ref[...], v_ref[...])

    blk = pl.BlockSpec((tr, C), lambda i: (i, 0))
    n_out = 4 if copy_g else 3
    return pl.pallas_call(
        body, name=name, grid=(R // tr,), in_specs=[blk] * 4, out_specs=[blk] * n_out,
        out_shape=[jax.ShapeDtypeStruct((R, C), f32)] * n_out,
        compiler_params=_params("parallel"),
    )(g, w, m, v)


def _adamw_transposed(gt, w, m, v, name):
    n, C, rb = gt.shape
    assert w.shape == (C, n * rb), (gt.shape, w.shape)

    def body(g_ref, w_ref, m_ref, v_ref, go_ref, d_ref, mo_ref, vo_ref):
        g_ = g_ref[...].astype(f32)
        go_ref[...] = g_
        d_ref[...], mo_ref[...], vo_ref[...] = _adamw_rule(g_, w_ref[...], m_ref[...], v_ref[...])

    blk = pl.BlockSpec((C, rb), lambda j: (0, j))
    return pl.pallas_call(
        body, name=name, grid=(n,), in_specs=[pl.BlockSpec((None, C, rb), lambda j: (j, 0, 0))] + [blk] * 3,
        out_specs=[blk] * 4, out_shape=[jax.ShapeDtypeStruct((C, n * rb), f32)] * 4,
        compiler_params=_params("parallel"),
    )(gt, w, m, v)


def _sum_slots(r, name):
    n, R, C = r.shape
    tr = _tile(R, 128, 16) if R % 16 == 0 and R > 128 else R

    def body(r_ref, o_ref):
        acc = r_ref[0].astype(f32)
        for s in range(1, n):
            acc = acc + r_ref[s].astype(f32)
        o_ref[...] = acc

    return pl.pallas_call(
        body, name=name, grid=(R // tr,),
        in_specs=[pl.BlockSpec((n, tr, C), lambda i: (0, i, 0))],
        out_specs=pl.BlockSpec((tr, C), lambda i: (i, 0)),
        out_shape=jax.ShapeDtypeStruct((R, C), f32),
        compiler_params=_params("parallel"),
    )(r)


SIBLING_PARTS = 8


class _Siblings:
    def __init__(self, arrays):
        self.arrays = list(arrays)
        self.n = len(self.arrays)
        self.parts = [next(p for p in range(SIBLING_PARTS, 0, -1) if a.shape[0] % (8 * p) == 0 or p == 1)
                      for a in self.arrays]
        total = sum(self.parts)
        self.out_shape = [jax.ShapeDtypeStruct((2,) + a.shape, a.dtype) for a in self.arrays]
        self.sems = [pltpu.SemaphoreType.DMA((total,)), pltpu.SemaphoreType.DMA((total,)),
                     pltpu.SemaphoreType.DMA((self.n,))]

    def hooks(self, ins, outs, send, recv, lsem):
        def copies():
            x, y, c = lax.axis_index("x"), lax.axis_index("y"), lax.axis_index("c")
            out, k = [], 0
            for a in range(self.n):
                out.append((pltpu.make_async_copy(ins[a], outs[a].at[c], lsem.at[a]), None))
                rows = self.arrays[a].shape[0] // self.parts[a]
                for part in range(self.parts[a]):
                    r = pl.ds(part * rows, rows)
                    mk = lambda dst, a=a, r=r, k=k: pltpu.make_async_remote_copy(
                        src_ref=ins[a].at[r], dst_ref=dst.at[r], send_sem=send.at[k], recv_sem=recv.at[k],
                        device_id=(x, y, 1 - c), device_id_type=MESH)
                    out.append((mk(outs[a].at[c]), mk(outs[a].at[1 - c])))
                    k += 1
            return out

        return _start_wait(copies)


def _comm_now(name, sides):
    total = sum(s.n for s in sides)

    def body(*refs):
        ins, outs, sems = refs[:total], refs[total:2 * total], refs[2 * total:]
        hooks, o = [], 0
        for i, s in enumerate(sides):
            hooks.append(s.hooks(ins[o:o + s.n], outs[o:o + s.n], *sems[3 * i:3 * i + 3]))
            o += s.n
        for start, _ in hooks:
            start()
        for _, wait in hooks:
            wait()

    out = pl.pallas_call(
        body, name=name, in_specs=[_ANY] * total, out_specs=[_ANY] * total,
        out_shape=[sh for s in sides for sh in s.out_shape], scratch_shapes=[sm for s in sides for sm in s.sems],
    )(*[a for s in sides for a in s.arrays])
    res, o = [], 0
    for s in sides:
        res.append(list(out[o:o + s.n]))
        o += s.n
    return res


def _cat_cols(g):
    return jnp.concatenate([g[i] for i in range(N_CHIP)], axis=-1)


def _row_slabs(a):
    return a.reshape(N_DEV, a.shape[0] // N_DEV, a.shape[1])


def _w_in_columns(g_wp, g_wsm):
    return jnp.concatenate([g_wp[:, C_QKV:C_END], g_wp[:, C_ZR:C_ZR + GDN_V], g_wsm[:, :SM_LR],
                            g_wp[:, C_G:C_G + G_W], g_wp[:, C_ZR + GDN_V:C_ZR + ZR_W],
                            g_wsm[:, SM_LR:SM_LR + GATE_RANK]], axis=1)


def _step(x, loss_target, p, meta, shard):
    _, S, D = x.shape
    alog_p = jnp.pad(p["gdn_a_log"], ((0, 0), (0, SM_W - GDN_H)))
    dtb_p = jnp.pad(p["gdn_dt_bias"], ((0, 0), (0, SM_W - GDN_H)))
    m64 = jnp.concatenate([jnp.zeros((PAD, D), f32), meta], axis=0)
    gate_b, gdn_norm_w, gla_norm_w = p["gla_gate_b"], p["gdn_norm_w"], p["gla_norm_w"]

    h0, n1, (w_in4, conv4, w24) = _embed_norm(
        x, m64, p["attn_norm_w"], side=_Gather([shard["w_in"], shard["gdn_conv_w"], shard["gla_gate_w2"]]))
    w_in, conv_w, w2 = _cat_cols(w_in4), _cat_cols(conv4), _cat_cols(w24)
    wp = jnp.concatenate([w_in[:, R_Z:R_AB], w_in[:, R_GR:R_LR], w_in[:, R_G:R_GR], w_in[:, R_QKV:R_Z]], axis=1)
    wsm = jnp.concatenate([w_in[:, R_AB:R_G], w_in[:, R_LR:R_END],
                           jnp.zeros((D, SM_W - SM_LR - GATE_RANK), w_in.dtype)], axis=1)
    w2p = jnp.pad(w2, ((SM_LR, SM_W - SM_LR - GATE_RANK), (0, 0)))
    proj, (w_up,) = _mm(n1, wp, "nn", "proj", side=_Gather([shard["w_up"]], by_columns=True))
    psm = _mm(n1, wsm, "nn", "proj_small")
    gb, la = _gates(psm, w2p, gate_b, alog_p, dtb_p)
    qkvc, (w_out4,) = _conv(proj, conv_w, side=_Gather([shard["w_out"]]))
    w_out = w_out4.reshape(-1, D)
    og, sall, pall, (w_gate,) = _gdn_fwd(qkvc, gb, side=_Gather([shard["w_gate"]], by_columns=True))
    ol, stall = _gla_fwd(proj, la)
    mixed = _gated_norm(og, ol, proj, gdn_norm_w, gla_norm_w)
    attn = _mm(mixed, w_out, "nn", "out_proj")
    h1, n2 = _add_norm(h0, attn, p["ffn_norm_w"])
    act, act_dgate, act_dup, (w_down4,) = _ffn_in(n2, w_gate, w_up, side=_Gather([shard["w_down"]]))
    w_down = w_down4.reshape(-1, D)
    ffn = _mm(act, w_down, "nn", "ffn_down", **WHOLE_K)
    dh2, dh2b, lossp, g_final = _final(h1, ffn, loss_target, p["final_norm_w"])

    g_down = _mm(act, dh2b, "tn", "g_w_down", out_dtype=bf16, **WHOLE_K_T)
    dg, du = _ffn_dact(dh2b, w_down, act_dgate, act_dup)
    g_gate = _mm(n2, dg, "tn", "g_w_gate", tm_cap=512, tn_cap=1408, tk_cap=2752, out_dtype=bf16, col_slabs=True)
    g_up = _mm(n2, du, "tn", "g_w_up", tm_cap=512, tn_cap=1408, tk_cap=2752, out_dtype=bf16, col_slabs=True)
    dn2 = _ffn_dn(dg, du, w_gate, w_up)
    dh1, dh1b, g_ffn_norm = _norm_bwd(dn2, h1, dh2, p["ffn_norm_w"])
    dmix = _mm(dh1b, w_out, "nt", "d_mixed")
    g_out = _mm(mixed, dh1b, "tn", "g_w_out", out_dtype=bf16, **WHOLE_K_T)
    dog, dol, dproj, g_gdn_norm, g_gla_norm = _gated_norm_bwd(og, ol, proj, gdn_norm_w, gla_norm_w, dmix)
    dproj, dla, (r_down,) = _gla_bwd(proj, la, stall, dol, dproj, side=_Exchange([_row_slabs(g_down)]))
    dqkvc, dgb, (r_gate, r_up, r_out, h_down) = _gdn_bwd(
        qkvc, gb, sall, pall, dog,
        side=_Sides(_Exchange([g_gate, g_up, _row_slabs(g_out)]), _Siblings([_sum_slots(r_down, "sum_w_down")])))
    dproj, g_conv, (h_gate,) = _conv_bwd(proj, conv_w, dqkvc, dproj,
                                         side=_Siblings([_sum_slots(r_gate, "sum_w_gate")]))
    dpsm, g_w2p, g_gate_b, g_alog, g_dtb = _gates_bwd(psm, w2p, gate_b, alog_p, dtb_p, dgb, dla)
    g_wsm = _mm(n1, dpsm, "tn", "g_w_in_small", out_dtype=bf16, **WHOLE_K_T)
    g_wp_a, (h_out,) = _mm(n1, dproj, "tn", "g_w_in_a", out_dtype=bf16, b_cols=(0, W_IN_SPLIT),
                           side=_Siblings([_sum_slots(r_out, "sum_w_out")]), **WHOLE_K_T)
    g_wp_b, r_in_a = _mm(n1, dproj, "tn", "g_w_in_b", out_dtype=bf16, b_cols=(W_IN_SPLIT, C_END - W_IN_SPLIT),
                         side=_Exchange([_row_slabs(g_wp_a), _row_slabs(g_wsm)]), **WHOLE_K_T)
    dn1, (r_in_b, h_up) = _mm(
        dproj, wp, "nt", "d_n1", plus=(dpsm, wsm),
        side=_Sides(_Exchange([_row_slabs(g_wp_b)]), _Siblings([_sum_slots(r_up, "sum_w_up")])), **WHOLE_K)
    s_wp = jnp.concatenate([_sum_slots(r_in_a[0], "sum_w_in_a"), _sum_slots(r_in_b, "sum_w_in_b")], axis=1)
    s_in = _w_in_columns(s_wp, _sum_slots(r_in_a[1], "sum_w_in_small"))
    in_by_chip = s_in.reshape(s_in.shape[0], N_CHIP, -1).transpose(1, 2, 0).astype(bf16)
    grad_x, g_meta, g_attn_norm, (h_in,) = _embed_norm_bwd(dn1, h0, dh1, p["attn_norm_w"], S,
                                                           side=_Exchange([], by_chip=[in_by_chip]))

    received = dict(w_in=h_in, w_gate=h_gate, w_up=h_up, w_out=h_out, w_down=h_down)
    small = dict(
        meta_tokens=g_meta, attn_norm_w=g_attn_norm, gdn_conv_w=g_conv, gdn_a_log=g_alog[:, :GDN_H],
        gdn_dt_bias=g_dtb[:, :GDN_H], gdn_norm_w=g_gdn_norm, gla_gate_w2=g_w2p[SM_LR:SM_LR + GATE_RANK],
        gla_gate_b=g_gate_b, gla_norm_w=g_gla_norm, ffn_norm_w=g_ffn_norm, final_norm_w=g_final)
    return lossp[0, 0], grad_x, received, small


_WEIGHTS = ("meta_tokens", "attn_norm_w", "w_in", "gdn_conv_w", "gdn_a_log", "gdn_dt_bias", "gdn_norm_w",
            "gla_gate_w2", "gla_gate_b", "gla_norm_w", "w_out", "ffn_norm_w", "w_gate", "w_up", "w_down",
            "final_norm_w")
_BIG_COLS = ("w_in", "w_gate", "w_up")
_BIG_ROWS = ("w_out", "w_down")
_SMALL_SHARDED = ("meta_tokens", "gdn_conv_w", "gla_gate_w2")


def kernel(x, meta_tokens, attn_norm_w, w_in, gdn_conv_w, gdn_a_log, gdn_dt_bias, gdn_norm_w, gla_gate_w2, gla_gate_b, gla_norm_w, w_out, ffn_norm_w, w_gate, w_up, w_down, final_norm_w, loss_target, m_meta_tokens, m_attn_norm_w, m_w_in, m_gdn_conv_w, m_gdn_a_log, m_gdn_dt_bias, m_gdn_norm_w, m_gla_gate_w2, m_gla_gate_b, m_gla_norm_w, m_w_out, m_ffn_norm_w, m_w_gate, m_w_up, m_w_down, m_final_norm_w, v_meta_tokens, v_attn_norm_w, v_w_in, v_gdn_conv_w, v_gdn_a_log, v_gdn_dt_bias, v_gdn_norm_w, v_gla_gate_w2, v_gla_gate_b, v_gla_norm_w, v_w_out, v_ffn_norm_w, v_w_gate, v_w_up, v_w_down, v_final_norm_w):
    w = dict(meta_tokens=meta_tokens, attn_norm_w=attn_norm_w, w_in=w_in, gdn_conv_w=gdn_conv_w, gdn_a_log=gdn_a_log,
             gdn_dt_bias=gdn_dt_bias, gdn_norm_w=gdn_norm_w, gla_gate_w2=gla_gate_w2, gla_gate_b=gla_gate_b,
             gla_norm_w=gla_norm_w, w_out=w_out, ffn_norm_w=ffn_norm_w, w_gate=w_gate, w_up=w_up, w_down=w_down,
             final_norm_w=final_norm_w)
    m = dict(meta_tokens=m_meta_tokens, attn_norm_w=m_attn_norm_w, w_in=m_w_in, gdn_conv_w=m_gdn_conv_w,
             gdn_a_log=m_gdn_a_log, gdn_dt_bias=m_gdn_dt_bias, gdn_norm_w=m_gdn_norm_w, gla_gate_w2=m_gla_gate_w2,
             gla_gate_b=m_gla_gate_b, gla_norm_w=m_gla_norm_w, w_out=m_w_out, ffn_norm_w=m_ffn_norm_w,
             w_gate=m_w_gate, w_up=m_w_up, w_down=m_w_down, final_norm_w=m_final_norm_w)
    v = dict(meta_tokens=v_meta_tokens, attn_norm_w=v_attn_norm_w, w_in=v_w_in, gdn_conv_w=v_gdn_conv_w,
             gdn_a_log=v_gdn_a_log, gdn_dt_bias=v_gdn_dt_bias, gdn_norm_w=v_gdn_norm_w, gla_gate_w2=v_gla_gate_w2,
             gla_gate_b=v_gla_gate_b, gla_norm_w=v_gla_norm_w, w_out=v_w_out, ffn_norm_w=v_ffn_norm_w,
             w_gate=v_w_gate, w_up=v_w_up, w_down=v_w_down, final_norm_w=v_final_norm_w)
    chip = 2 * lax.axis_index("x") + lax.axis_index("y")

    def two_d(a):
        return a.reshape(1, -1) if a.ndim == 1 else a.reshape(-1, a.shape[-1])

    w2d = {k: two_d(a) for k, a in w.items()}
    big = _BIG_COLS + _BIG_ROWS
    small = tuple(k for k in _WEIGHTS if k not in big)

    (meta4,), = _comm_now("gather_meta", [_Gather([w2d["meta_tokens"]])])
    shard = {k: w2d[k].astype(bf16) for k in big}
    shard.update({k: w2d[k] for k in ("gdn_conv_w", "gla_gate_w2")})
    lossp, grad_x, received, g = _step(x, loss_target, {k: w2d[k] for k in small}, _cat_cols(meta4), shard)
    loss = lax.psum(lossp, ("x", "y", "c"))

    sizes = [g[k].size for k in small]
    total = sum(sizes)
    rows = -(-total // 1024)
    rows += (-rows) % 8
    packed = jnp.concatenate([g[k].reshape(-1) for k in small] + [jnp.zeros((rows * 1024 - total,), f32)])
    (packed8,), = _comm_now("exchange_small", [_Exchange([], [packed.reshape(rows, 1024)])])
    red = {k: h.reshape(w2d[k].shape) for k, h in received.items() if k != "w_in"}
    psum_small = _sum_slots(packed8, "sum_small").reshape(-1)
    off = 0
    for k, n in zip(small, sizes):
        a = psum_small[off:off + n].reshape(g[k].shape)
        off += n
        if k in _SMALL_SHARDED:
            c = w2d[k].shape[1]
            a = lax.dynamic_slice_in_dim(a, chip * c, c, axis=1)
        red[k] = a

    grads, deltas, new_m, new_v = [], [], [], []
    for k in _WEIGHTS:
        shape = w[k].shape
        if k == "w_in":
            flip = lambda a: jnp.swapaxes(a, 1, 2).reshape(shape[2], shape[1])
            unflip = lambda a: jnp.swapaxes(a.reshape(shape[0], shape[2], shape[1]), 1, 2)
            out = _adamw_transposed(received[k], flip(w[k]), flip(m[k]), flip(v[k]), "adamw_" + k)
            gk, d, m2, v2 = [unflip(a) for a in out]
        elif k in big:
            gk, d, m2, v2 = [a.reshape(shape) for a in
                             _adamw(red[k], w2d[k], two_d(m[k]), two_d(v[k]), "adamw_" + k, copy_g=True)]
        else:
            gk = red[k].reshape(shape)
            d, m2, v2 = [a.reshape(shape) for a in _adamw(red[k], w2d[k], two_d(m[k]), two_d(v[k]), "adamw_" + k)]
        grads.append(gk)
        deltas.append(d)
        new_m.append(m2)
        new_v.append(v2)
    return (loss, grad_x, *grads, *deltas, *new_m, *new_v)
```

```python
import functools

import jax
import jax.numpy as jnp
from jax import lax
from jax.experimental import pallas as pl
from jax.experimental.pallas import tpu as pltpu

f32 = jnp.float32
bf16 = jnp.bfloat16
HIGH = lax.Precision.HIGH
MESH = pl.DeviceIdType.MESH

N_META = 16
CONV_K = 4
GDN_H, GDN_DK, GDN_DV, GDN_C = 8, 128, 128, 64
GLA_H, GLA_DK, GLA_DV, GLA_C = 4, 128, 256, 16
GATE_RANK = 16
GATE_NORMALIZER = 16.0
EPS = 1e-6
GDN_QK = GDN_H * GDN_DK
GDN_V = GDN_H * GDN_DV
GLA_QK = GLA_H * GLA_DK
GLA_V = GLA_H * GLA_DV
PAD = (-N_META) % GDN_C
OFF = PAD + N_META
ROWS = 64

R_QKV, R_Z, R_AB, R_G, R_GR, R_LR, R_END = 0, 3072, 4096, 4112, 6160, 7184, 7200
C_ZR, C_G, C_QKV, C_END = 0, 2048, 4096, 7168
W_IN_SPLIT = 3072
ZR_W = GDN_V + GLA_V
G_W = 2 * GLA_QK + GLA_V
QKV_W = 2 * GDN_QK + GDN_V
Q0, K0, V0 = 0, GDN_QK, 2 * GDN_QK
SM_W = 128
SM_LR = 2 * GDN_H

ADAM_LR, ADAM_B1, ADAM_B2, ADAM_EPS, ADAM_WD, ADAM_STEP = 0.001, 0.9, 0.999, 1e-08, 0.01, 10

VMEM_LIMIT_V7X = 56 * 1024 * 1024
LANES = 128
N_DEV = 8
N_CHIP = 4


def _params(*sem):
    return pltpu.CompilerParams(dimension_semantics=sem, vmem_limit_bytes=VMEM_LIMIT_V7X)


def _tile(n, cap, mult=16):
    best = None
    for d in range(mult, min(n, cap) + 1, mult):
        if n % d == 0:
            best = d
    assert best is not None, (n, cap, mult)
    return best


NN = ((1,), (0,))
NT = ((1,), (1,))
TN = ((0,), (0,))


def _dot(a, b, dims, prec=None):
    return lax.dot_general(a, b, (dims, ((), ())), precision=prec, preferred_element_type=f32)


def _mmb(a, b, dims):
    return _dot(a.astype(bf16), b.astype(bf16), dims)


def _sigmoid(x):
    return jax.nn.sigmoid(x)


def _silu(x):
    return x * _sigmoid(x)


def _dsilu(x):
    s = _sigmoid(x)
    return s * (1.0 + x * (1.0 - s))


def _log1p_exp_neg_abs(x):
    t = jnp.exp(-jnp.abs(x))
    u = 1.0 + t
    d = u - 1.0
    return jnp.where(d == 0.0, t, jnp.log(u) * (t / jnp.where(d == 0.0, 1.0, d)))


def _softplus(x):
    return jnp.maximum(x, 0.0) + _log1p_exp_neg_abs(x)


def _log_sigmoid(x):
    return jnp.minimum(x, 0.0) - _log1p_exp_neg_abs(x)


def _rms(x):
    r = lax.rsqrt(jnp.mean(x * x, axis=-1, keepdims=True) + EPS)
    return x * r, r


def _rms_bwd(dy, xh, r, w):
    t = dy * w
    return r * (t - xh * jnp.mean(t * xh, axis=-1, keepdims=True))


def _l2n(x):
    return x * lax.rsqrt(jnp.sum(x * x, axis=-1, keepdims=True) + EPS)


INV_LEAF = 8


def _same_block(C, b):
    sh = b.bit_length() - 1
    row = lax.broadcasted_iota(jnp.int32, (C, C), 0)
    col = lax.broadcasted_iota(jnp.int32, (C, C), 1)
    return lax.shift_right_logical(row, sh) == lax.shift_right_logical(col, sh)


def _tri_inv_impl(As):
    C = As[0].shape[0]
    R = range(len(As))
    row = lax.broadcasted_iota(jnp.int32, (C, C), 0)
    col = lax.broadcasted_iota(jnp.int32, (C, C), 1)
    eye = (row == col).astype(f32)
    b = INV_LEAF
    inner = _same_block(C, b)
    leaf = [jnp.where(inner, As[h], 0.0) for h in R]
    d = [eye - leaf[h] for h in R]
    pw = leaf
    n = 2
    while n < b:
        pw = [_dot(pw[h], pw[h], NN, HIGH) for h in R]
        d = [_dot(d[h], eye + pw[h], NN, HIGH) for h in R]
        n *= 2
    while b < C:
        outer = _same_block(C, 2 * b)
        level = jnp.logical_and(outer, jnp.logical_not(inner))
        ed = [_dot(jnp.where(level, As[h], 0.0), d[h], NN, HIGH) for h in R]
        d = [d[h] - _dot(d[h], ed[h], NN, HIGH) for h in R]
        inner = outer
        b *= 2
    return d


@jax.custom_vjp
def _tri_inv(As):
    return _tri_inv_impl(As)


def _tri_inv_fwd(As):
    d = _tri_inv_impl(As)
    return d, d


def _tri_inv_bwd(d, g):
    R = range(len(d))
    t = [_dot(d[h], g[h], TN, HIGH) for h in R]
    return ([-_dot(t[h], d[h], NT, HIGH) for h in R],)


_tri_inv.defvjp(_tri_inv_fwd, _tri_inv_bwd)


@jax.custom_vjp
def _tri_inv_known(As, Ps):
    del As
    return Ps


def _tri_inv_known_fwd(As, Ps):
    del As
    return Ps, Ps


def _tri_inv_known_bwd(d, g):
    return _tri_inv_bwd(d, g)[0], [jnp.zeros_like(x) for x in d]


_tri_inv_known.defvjp(_tri_inv_known_fwd, _tri_inv_known_bwd)


def _gdn_chunk(Ss, qrs, krs, vs, betas, gs, Ps=None):
    H = len(Ss)
    C, dk = qrs[0].shape
    R = range(len(qrs))
    row = lax.broadcasted_iota(jnp.int32, (C, C), 0)
    col = lax.broadcasted_iota(jnp.int32, (C, C), 1)
    causal = row >= col
    strict = row > col
    cf = causal.astype(f32)
    q = [_l2n(qrs[h]) * (dk ** -0.5) for h in R]
    k = [_l2n(krs[h]) for h in R]
    mc = [_rows_exact(cf, jnp.broadcast_to(gs[h], (C, C))) for h in R]
    gc = [mc[h][:, 0:1] for h in R]
    decay = [jnp.where(causal, jnp.exp(jnp.where(causal, mc[h] - mc[h].T, 0.0)), 0.0) for h in R]
    kb = [k[h] * betas[h] for h in R]
    a = [jnp.where(strict, _mmb(kb[h], k[h], NT) * decay[h], 0.0) for h in R]
    p = _tri_inv(a) if Ps is None else _tri_inv_known(a, Ps)
    egc = [jnp.exp(gc[h]) for h in R]
    u = [_mmb(p[h], vs[h] * betas[h], NN) for h in R]
    w = [_mmb(p[h], kb[h] * egc[h], NN) for h in R]
    qk = [jnp.where(causal, _mmb(q[h], k[h], NT) * decay[h], 0.0) for h in R]
    qe = [q[h] * egc[h] for h in R]
    gl = [gc[h][C - 1:C, :] for h in R]
    kd = [k[h] * jnp.exp(gl[h] - gc[h]) for h in R]
    egl = [jnp.exp(gl[h]) for h in R]
    S, o, entering = list(Ss), [], []
    for chunk in range(len(qrs) // H):
        idx = [chunk * H + h for h in range(H)]
        entering += S
        v_new = [u[i] - _mmb(w[i], S[h], NN) for h, i in enumerate(idx)]
        o += [_mmb(qe[i], S[h], NN) + _mmb(qk[i], v_new[h], NN) for h, i in enumerate(idx)]
        S = [S[h] * egl[i] + _mmb(kd[i], v_new[h], TN) for h, i in enumerate(idx)]
    return S, o, p, entering


def _rows_exact_impl(m01, x, dims):
    m = m01.astype(bf16)
    x1 = x.astype(bf16)
    r1 = x - x1.astype(f32)
    x2 = r1.astype(bf16)
    x3 = (r1 - x2.astype(f32)).astype(bf16)
    d = lambda y: _dot(m, y, dims)
    return d(x1) + (d(x2) + d(x3))


@jax.custom_vjp
def _rows_exact(m01, x):
    return _rows_exact_impl(m01, x, NN)


def _rows_exact_fwd(m01, x):
    return _rows_exact_impl(m01, x, NN), m01


def _rows_exact_bwd(m01, g):
    return jnp.zeros_like(m01), _rows_exact_impl(m01, g, TN)


_rows_exact.defvjp(_rows_exact_fwd, _rows_exact_bwd)


def _gla_blocks(Sts, qrs, ks, vs, las):
    H = len(Sts)
    n = len(qrs)
    C, dk = qrs[0].shape
    R = range(n)
    row = lax.broadcasted_iota(jnp.int32, (C, C), 0)
    col = lax.broadcasted_iota(jnp.int32, (C, C), 1)
    ri = lax.broadcasted_iota(jnp.int32, (C, dk), 0)
    q = [qrs[h] * (dk ** -0.5) for h in R]
    running = (row >= col).astype(f32)
    b = [_rows_exact(running, las[h]) for h in R]
    sc = [jnp.where(row == col, jnp.sum(q[h] * ks[h], axis=-1, keepdims=True), 0.0) for h in R]
    s = C // 2
    while s >= 1:
        sh = s.bit_length() - 1
        ref = lax.shift_left(lax.shift_right_logical(row, sh + 1), sh + 1) + (s - 1)
        pick = (col == ref).astype(f32)
        bref = [_rows_exact(pick, b[h]) for h in R]
        upper = (lax.shift_right_logical(ri, sh) & 1) == 1
        qt = [jnp.where(upper, q[h] * jnp.exp(jnp.where(upper, b[h] - bref[h], 0.0)), 0.0) for h in R]
        kt = [jnp.where(upper, 0.0, ks[h] * jnp.exp(jnp.where(upper, 0.0, bref[h] - b[h]))) for h in R]
        same = lax.shift_right_logical(row, sh + 1) == lax.shift_right_logical(col, sh + 1)
        sc = [sc[h] + jnp.where(same, _mmb(qt[h], kt[h], NT), 0.0) for h in R]
        s //= 2
    o = [_mmb(sc[h], vs[h], NN) for h in R]
    qe = [q[h] * jnp.exp(b[h]) for h in R]
    bl = [b[h][C - 1:C, :] for h in R]
    upd = [_mmb(vs[h], ks[h] * jnp.exp(bl[h] - b[h]), TN) for h in R]
    ebl = [jnp.exp(bl[h]) for h in R]
    St = list(Sts)
    for blk in range(n // H):
        for h in range(H):
            i = blk * H + h
            o[i] = o[i] + _mmb(qe[i], St[h], NT)
        St = [St[h] * ebl[blk * H + h] + upd[blk * H + h] for h in range(H)]
    return St, o


_ANY = pl.BlockSpec(memory_space=pl.ANY)


class _Gather:
    def __init__(self, arrays, by_columns=False):
        self.arrays = list(arrays)
        self.n = len(self.arrays)
        self.by_columns = by_columns
        if by_columns:
            assert all(a.ndim == 2 and a.shape[1] % LANES == 0 for a in self.arrays)
            self.out_shape = [jax.ShapeDtypeStruct((a.shape[0], N_CHIP * a.shape[1]), a.dtype) for a in self.arrays]
        else:
            self.out_shape = [jax.ShapeDtypeStruct((N_CHIP,) + a.shape, a.dtype) for a in self.arrays]
        self.sems = [pltpu.SemaphoreType.DMA((self.n, 3)), pltpu.SemaphoreType.DMA((self.n, 3)),
                     pltpu.SemaphoreType.DMA((self.n,))]

    def hooks(self, ins, outs, send, recv, lsem):
        def place(a, chip):
            if not self.by_columns:
                return outs[a].at[chip]
            cols = self.arrays[a].shape[1]
            return outs[a].at[:, pl.ds(pl.multiple_of(chip * cols, LANES), cols)]

        def copies():
            x, y, c = lax.axis_index("x"), lax.axis_index("y"), lax.axis_index("c")
            me = 2 * x + y
            out = []
            for a in range(self.n):
                out.append((pltpu.make_async_copy(ins[a], place(a, me), lsem.at[a]), None))
                for j, (px, py) in enumerate([(1 - x, y), (x, 1 - y), (1 - x, 1 - y)]):
                    mk = lambda dst, a=a, j=j, px=px, py=py: pltpu.make_async_remote_copy(
                        src_ref=ins[a], dst_ref=dst, send_sem=send.at[a, j], recv_sem=recv.at[a, j],
                        device_id=(px, py, c), device_id_type=MESH)
                    out.append((mk(place(a, me)), mk(place(a, 2 * px + py))))
            return out

        return _start_wait(copies)


class _Exchange:
    def __init__(self, slotted, shared=(), by_chip=()):
        self.arrays = list(slotted) + list(by_chip) + list(shared)
        self.ns, self.nc = len(slotted), len(by_chip)
        self.n = len(self.arrays)
        self.out_shape = [jax.ShapeDtypeStruct(a.shape, a.dtype) for a in slotted]
        self.out_shape += [jax.ShapeDtypeStruct((N_DEV,) + a.shape[1:], a.dtype) for a in by_chip]
        self.out_shape += [jax.ShapeDtypeStruct((N_DEV,) + b.shape, b.dtype) for b in shared]
        self.sems = [pltpu.SemaphoreType.DMA((self.n, N_DEV - 1)), pltpu.SemaphoreType.DMA((self.n, N_DEV - 1)),
                     pltpu.SemaphoreType.DMA((self.n,))]

    def hooks(self, ins, outs, send, recv, lsem):
        def copies():
            x, y, c = lax.axis_index("x"), lax.axis_index("y"), lax.axis_index("c")
            me = 4 * x + 2 * y + c

            def src(a, dev):
                tx, ty, tc = dev
                if a < self.ns:
                    return ins[a].at[4 * tx + 2 * ty + tc]
                return ins[a].at[2 * tx + ty] if a < self.ns + self.nc else ins[a]

            out = []
            for a in range(self.n):
                out.append((pltpu.make_async_copy(src(a, (x, y, c)), outs[a].at[me], lsem.at[a]), None))
                for o in range(1, N_DEV):
                    dev = (1 - x if o & 4 else x, 1 - y if o & 2 else y, 1 - c if o & 1 else c)
                    t = 4 * dev[0] + 2 * dev[1] + dev[2]
                    mk = lambda dst, a=a, o=o, dev=dev: pltpu.make_async_remote_copy(
                        src_ref=src(a, dev), dst_ref=dst, send_sem=send.at[a, o - 1], recv_sem=recv.at[a, o - 1],
                        device_id=dev, device_id_type=MESH)
                    out.append((mk(outs[a].at[me]), mk(outs[a].at[t])))
            return out

        return _start_wait(copies)


class _Sides:
    def __init__(self, *members):
        self.members = members
        self.arrays = [a for s in members for a in s.arrays]
        self.n = len(self.arrays)
        self.out_shape = [sh for s in members for sh in s.out_shape]
        self.sems = [sm for s in members for sm in s.sems]

    def hooks(self, ins, outs, *sems):
        hooks, o = [], 0
        for i, s in enumerate(self.members):
            hooks.append(s.hooks(ins[o:o + s.n], outs[o:o + s.n], *sems[3 * i:3 * i + 3]))
            o += s.n

        def start():
            for st, _ in hooks:
                st()

        def wait():
            for _, wt in hooks:
                wt()

        return start, wait


def _start_wait(copies):
    def start():
        for s, _ in copies():
            s.start()

    def wait():
        for s, w in copies():
            (s if w is None else w).wait()

    return start, wait


def _call(body, *, name, grid, in_specs, out_specs, out_shape, args, sem, scratch_shapes=(), aliases=None, side=None):
    in_specs, out_specs, out_shape, args = list(in_specs), list(out_specs), list(out_shape), list(args)
    scratch_shapes = list(scratch_shapes)
    aliases = aliases or {}
    if side is None:
        return pl.pallas_call(
            body, name=name, grid=grid, in_specs=in_specs, out_specs=out_specs, out_shape=out_shape,
            scratch_shapes=scratch_shapes, input_output_aliases=aliases, compiler_params=_params(*sem))(*args)
    n_in, n_out, n_scr, ns = len(in_specs), len(out_specs), len(scratch_shapes), side.n

    def full_body(*refs):
        ins, refs = refs[:n_in], refs[n_in:]
        s_in, refs = refs[:ns], refs[ns:]
        outs, refs = refs[:n_out], refs[n_out:]
        s_out, refs = refs[:ns], refs[ns:]
        scr, sems = refs[:n_scr], refs[n_scr:]
        start, wait = side.hooks(s_in, s_out, *sems)
        ids = [pl.program_id(d) for d in range(len(grid))]
        first = functools.reduce(jnp.logical_and, [i == 0 for i in ids])
        last = functools.reduce(jnp.logical_and, [i == g - 1 for i, g in zip(ids, grid)])
        pl.when(first)(start)
        body(*ins, *outs, *scr)
        pl.when(last)(wait)

    return pl.pallas_call(
        full_body, name=name, grid=grid, in_specs=in_specs + [_ANY] * ns, out_specs=out_specs + [_ANY] * ns,
        out_shape=out_shape + side.out_shape, scratch_shapes=scratch_shapes + side.sems,
        input_output_aliases=aliases, compiler_params=_params(*(["arbitrary"] * len(grid))))(*args, *side.arrays)


WHOLE_K = dict(tm_cap=688, tn_cap=512, tk_cap=1 << 20)
WHOLE_K_T = dict(tm_cap=512, tn_cap=512, tk_cap=1 << 20)

def _mm(a, b, mode, name, *, tm_cap=1408, tn_cap=1024, tk_cap=2048, out_dtype=f32, acc_in=None, side=None,
        col_slabs=False, b_cols=None, plus=None):
    if mode == "nn":
        (M, K), (K2, N) = a.shape, b.shape
    elif mode == "nt":
        (M, K), (N, K2) = a.shape, b.shape
    else:
        (K, M), (K2, N) = a.shape, b.shape
    assert K == K2, (name, a.shape, b.shape)
    b_first = 0
    if b_cols is not None:
        assert mode != "nt"
        b_first, N = b_cols
    tm = _tile(M // 2 if col_slabs else M, tm_cap)
    tn = _tile(N // N_CHIP if col_slabs else N, tn_cap, 128)
    tk = _tile(K, tk_cap, 128 if K % 128 == 0 else 16)
    nk = K // tk
    dims = {"nn": NN, "nt": NT, "tn": TN}[mode]
    use_scratch = nk > 1 and out_dtype != f32

    def body(*refs):
        a_ref, b_ref, *rest = refs
        c_ref = rest.pop(0) if acc_in is not None else None
        p = _mmb(a_ref[...], b_ref[...], dims)
        if plus is not None:
            p = p + _mmb(rest.pop(0)[...], rest.pop(0)[...], dims)
        o_ref, *scr = rest
        if nk == 1:
            if c_ref is not None:
                p = p + c_ref[...]
            o_ref[...] = p.astype(out_dtype)
            return
        k = pl.program_id(2)
        acc = scr[0] if use_scratch else o_ref

        @pl.when(k == 0)
        def _():
            acc[...] = p if c_ref is None else p + c_ref[...]

        @pl.when(k > 0)
        def _():
            acc[...] += p

        if use_scratch:
            @pl.when(k == nk - 1)
            def _():
                o_ref[...] = acc[...].astype(out_dtype)

    if mode == "tn":
        a_spec = pl.BlockSpec((tk, tm), lambda i, j, k: (k, i))
    else:
        a_spec = pl.BlockSpec((tm, tk), lambda i, j, k: (i, k))
    if mode == "nt":
        b_spec = pl.BlockSpec((tn, tk), lambda i, j, k: (j, k))
    else:
        assert b_first % tn == 0, (name, b_first, tn)
        b_spec = pl.BlockSpec((tk, tn), lambda i, j, k: (k, j + b_first // tn))
    if col_slabs:
        assert acc_in is None
        ni, nj = M // 2 // tm, N // N_CHIP // tn
        o_spec = pl.BlockSpec((None, tm, tn), lambda i, j, k: (2 * (j // nj) + i // ni, i % ni, j % nj))
        o_shape = jax.ShapeDtypeStruct((N_DEV, M // 2, N // N_CHIP), out_dtype)
    else:
        o_spec = pl.BlockSpec((tm, tn), lambda i, j, k: (i, j))
        o_shape = jax.ShapeDtypeStruct((M, N), out_dtype)
    in_specs = [a_spec, b_spec]
    args = [a, b]
    if acc_in is not None:
        in_specs.append(o_spec)
        args.append(acc_in)
    if plus is not None:
        assert mode == "nt" and nk == 1, (name, mode, nk)
        k2 = plus[0].shape[1]
        assert plus[0].shape == (M, k2) and plus[1].shape == (N, k2), (name, plus[0].shape, plus[1].shape)
        in_specs += [pl.BlockSpec((tm, k2), lambda i, j, k: (i, 0)), pl.BlockSpec((tn, k2), lambda i, j, k: (j, 0))]
        args += list(plus)
    out = _call(body, name=name, grid=(M // tm, N // tn, nk), in_specs=in_specs, out_specs=[o_spec],
                out_shape=[o_shape], args=args,
                scratch_shapes=[pltpu.VMEM((tm, tn), f32)] if use_scratch else [],
                sem=("parallel", "parallel", "arbitrary"), side=side)
    return out[0] if side is None else (out[0], out[1:])


def _embed_norm(x3, m64, w, to_bf16=(), side=None):
    _, S, D = x3.shape
    Lp = OFF + S
    steps = Lp // ROWS
    nc = len(to_bf16)

    def body(x_ref, m_ref, w_ref, *rest):
        c_in, (h_ref, n_ref), c_out = rest[:nc], rest[nc:nc + 2], rest[nc + 2:]
        i = pl.program_id(0)
        h = jnp.where(i == 0, m_ref[...], x_ref[...])
        h_ref[...] = h
        xh, _ = _rms(h)
        n_ref[...] = (xh * w_ref[...]).astype(bf16)
        for ci, co in zip(c_in, c_out):
            co[...] = ci[...].astype(bf16)

    def piece(a):
        R, C = a.shape
        rb = next(d for d in range(16, R + 1, 16) if R % d == 0 and d * steps >= R)
        return pl.BlockSpec((rb, C), lambda i: (jnp.minimum(i, R // rb - 1), 0))

    row = pl.BlockSpec((ROWS, D), lambda i: (i, 0))
    pieces = [piece(a) for a in to_bf16]
    out = _call(
        body, name="embed_norm", grid=(steps,),
        in_specs=[pl.BlockSpec((None, ROWS, D), lambda i: (0, jnp.maximum(i - 1, 0), 0)),
                  pl.BlockSpec((ROWS, D), lambda i: (0, 0)),
                  pl.BlockSpec((1, D), lambda i: (0, 0))] + pieces,
        out_specs=[row, row] + pieces,
        out_shape=[jax.ShapeDtypeStruct((Lp, D), f32), jax.ShapeDtypeStruct((Lp, D), bf16)]
        + [jax.ShapeDtypeStruct(a.shape, bf16) for a in to_bf16],
        args=[x3, m64, w, *to_bf16], sem=("arbitrary",), side=side)
    return out[0], out[1], out[2:2 + nc], out[2 + nc:]


def _add_norm(h, d, w):
    Lp, D = h.shape
    tr = _tile(Lp, 256)

    def body(h_ref, d_ref, w_ref, o_ref, n_ref):
        h1 = h_ref[...] + d_ref[...]
        o_ref[...] = h1
        xh, _ = _rms(h1)
        n_ref[...] = (xh * w_ref[...]).astype(bf16)

    row = pl.BlockSpec((tr, D), lambda i: (i, 0))
    return pl.pallas_call(
        body, name="add_norm", grid=(Lp // tr,),
        in_specs=[row, row, pl.BlockSpec((1, D), lambda i: (0, 0))], out_specs=[row, row],
        out_shape=[jax.ShapeDtypeStruct((Lp, D), f32), jax.ShapeDtypeStruct((Lp, D), bf16)],
        compiler_params=_params("parallel"),
    )(h, d, w)


def _norm_bwd(dn, h, dh, w):
    Lp, D = h.shape
    tr = _tile(Lp, 256)

    def body(dn_ref, h_ref, dh_ref, w_ref, o_ref, ob_ref, gw_ref):
        i = pl.program_id(0)
        xh, r = _rms(h_ref[...])
        dn_ = dn_ref[...]
        o = dh_ref[...] + _rms_bwd(dn_, xh, r, w_ref[...])
        o_ref[...] = o
        ob_ref[...] = o.astype(bf16)
        gw = jnp.sum(dn_ * xh, axis=0, keepdims=True)

        @pl.when(i == 0)
        def _():
            gw_ref[...] = gw

        @pl.when(i > 0)
        def _():
            gw_ref[...] += gw

    row = pl.BlockSpec((tr, D), lambda i: (i, 0))
    vec = pl.BlockSpec((1, D), lambda i: (0, 0))
    return pl.pallas_call(
        body, name="norm_bwd", grid=(Lp // tr,), in_specs=[row, row, row, vec], out_specs=[row, row, vec],
        out_shape=[jax.ShapeDtypeStruct((Lp, D), f32), jax.ShapeDtypeStruct((Lp, D), bf16),
                   jax.ShapeDtypeStruct((1, D), f32)],
        compiler_params=_params("arbitrary"),
    )(dn, h, dh, w)


def _embed_norm_bwd(dn, h, dh, w, S, side=None):
    Lp, D = h.shape
    tr = _tile(S, 256, OFF)

    def body(dn_ref, h_ref, dh_ref, dn0_ref, h0_ref, dh0_ref, w_ref, gx_ref, gm_ref, gw_ref):
        i = pl.program_id(0)

        def rows(dn_, h_, dh_):
            xh, r = _rms(h_)
            return dh_ + _rms_bwd(dn_, xh, r, w_ref[...]), jnp.sum(dn_ * xh, axis=0, keepdims=True)

        d, gw = rows(dn_ref[...], h_ref[...], dh_ref[...])
        gx_ref[...] = d

        @pl.when(i == 0)
        def _():
            d0, gw0 = rows(dn0_ref[...], h0_ref[...], dh0_ref[...])
            gm_ref[...] = d0[PAD:OFF, :]
            gw_ref[...] = gw0 + gw

        @pl.when(i > 0)
        def _():
            gw_ref[...] += gw

    win = pl.BlockSpec((pl.Element(tr), pl.Element(D)), lambda i: (pl.multiple_of(OFF + i * tr, OFF), 0))
    head = pl.BlockSpec((OFF, D), lambda i: (0, 0))
    vec = pl.BlockSpec((1, D), lambda i: (0, 0))
    out = _call(
        body, name="embed_norm_bwd", grid=(S // tr,), in_specs=[win, win, win, head, head, head, vec],
        out_specs=[pl.BlockSpec((None, tr, D), lambda i: (0, i, 0)),
                   pl.BlockSpec((N_META, D), lambda i: (0, 0)), vec],
        out_shape=[jax.ShapeDtypeStruct((1, S, D), f32), jax.ShapeDtypeStruct((N_META, D), f32),
                   jax.ShapeDtypeStruct((1, D), f32)],
        args=[dn, h, dh, dn, h, dh, w], sem=("arbitrary",), side=side)
    return out[0], out[1], out[2], out[3:]


def _final(h1, ffn, tgt3, w):
    Lp, D = h1.shape
    _, S, _ = tgt3.shape
    tr = _tile(Lp, min(256, S), OFF)

    def body(h_ref, f_ref, t_ref, w_ref, d_ref, db_ref, l_ref, gw_ref):
        i = pl.program_id(0)
        h2 = h_ref[...] + f_ref[...]
        xh, r = _rms(h2)
        w_ = w_ref[...]
        t = t_ref[...]
        t = jnp.where(i == 0, pltpu.roll(t, OFF, 0), t)
        valid = (lax.broadcasted_iota(jnp.int32, (tr, 1), 0) + i * tr >= OFF).astype(f32)
        e = xh * w_ - t
        loss = 0.5 * jnp.sum(jnp.mean(e * e, axis=-1, keepdims=True) * valid, axis=0, keepdims=True)
        dy = e * (valid / D)
        d = _rms_bwd(dy, xh, r, w_)
        d_ref[...] = d
        db_ref[...] = d.astype(bf16)
        gw = jnp.sum(dy * xh, axis=0, keepdims=True)

        @pl.when(i == 0)
        def _():
            l_ref[...] = jnp.zeros_like(l_ref)
            gw_ref[...] = jnp.zeros_like(gw_ref)

        l_ref[...] += jnp.broadcast_to(loss, l_ref.shape)
        gw_ref[...] += gw

    row = pl.BlockSpec((tr, D), lambda i: (i, 0))
    vec = pl.BlockSpec((1, D), lambda i: (0, 0))
    tgt = pl.BlockSpec((pl.Element(tr), pl.Element(D)),
                       lambda i: (pl.multiple_of(jnp.maximum(i * tr - OFF, 0), OFF), 0))
    return pl.pallas_call(
        body, name="final_loss", grid=(Lp // tr,), in_specs=[row, row, tgt, vec],
        out_specs=[row, row, pl.BlockSpec((8, 128), lambda i: (0, 0)), vec],
        out_shape=[jax.ShapeDtypeStruct((Lp, D), f32), jax.ShapeDtypeStruct((Lp, D), bf16),
                   jax.ShapeDtypeStruct((8, 128), f32), jax.ShapeDtypeStruct((1, D), f32)],
        compiler_params=_params("arbitrary"),
    )(h1, ffn, tgt3.reshape(S, D), w)


def _ffn_in(n, w_gate, w_up, side=None):
    M, K = n.shape
    F = w_gate.shape[1]
    tm = _tile(M, 1408)
    tn = _tile(F, 512, 128)

    def body(a_ref, bg_ref, bu_ref, act_ref, pg_ref, pu_ref):
        a = a_ref[...]
        g = _mmb(a, bg_ref[...], NN)
        u = _mmb(a, bu_ref[...], NN)
        s = _sigmoid(g)
        gs = g * s
        act_ref[...] = (gs * u).astype(bf16)
        pg_ref[...] = (u * (s + gs * (1.0 - s))).astype(bf16)
        pu_ref[...] = gs.astype(bf16)

    wsp = pl.BlockSpec((K, tn), lambda i, j: (0, j))
    osp = pl.BlockSpec((tm, tn), lambda i, j: (i, j))
    out = _call(body, name="ffn_in", grid=(M // tm, F // tn),
                in_specs=[pl.BlockSpec((tm, K), lambda i, j: (i, 0)), wsp, wsp], out_specs=[osp] * 3,
                out_shape=[jax.ShapeDtypeStruct((M, F), bf16)] * 3, args=[n, w_gate, w_up],
                sem=("parallel", "parallel"), side=side)
    return out[0], out[1], out[2], out[3:]


def _ffn_dact(d, w_down, pg, pu):
    M, K = d.shape
    F = w_down.shape[0]
    tm = _tile(M, 1408)
    tn = _tile(F, 512, 128)

    def body(d_ref, w_ref, pg_ref, pu_ref, dg_ref, du_ref):
        da = _mmb(d_ref[...], w_ref[...], NT)
        dg_ref[...] = (da * pg_ref[...].astype(f32)).astype(bf16)
        du_ref[...] = (da * pu_ref[...].astype(f32)).astype(bf16)

    osp = pl.BlockSpec((tm, tn), lambda i, j: (i, j))
    return pl.pallas_call(
        body, name="ffn_dact", grid=(M // tm, F // tn),
        in_specs=[pl.BlockSpec((tm, K), lambda i, j: (i, 0)), pl.BlockSpec((tn, K), lambda i, j: (j, 0)), osp, osp],
        out_specs=[osp, osp], out_shape=[jax.ShapeDtypeStruct((M, F), bf16)] * 2,
        compiler_params=_params("parallel", "parallel"),
    )(d, w_down, pg, pu)


def _ffn_dn(dg, du, w_gate, w_up):
    M, F = dg.shape
    D = w_gate.shape[0]
    tm = _tile(M, 688)
    tn = _tile(D, 256, 128)

    def body(dg_ref, du_ref, wg_ref, wu_ref, o_ref):
        o_ref[...] = _mmb(dg_ref[...], wg_ref[...], NT) + _mmb(du_ref[...], wu_ref[...], NT)

    asp = pl.BlockSpec((tm, F), lambda i, j: (i, 0))
    wsp = pl.BlockSpec((tn, F), lambda i, j: (j, 0))
    return pl.pallas_call(
        body, name="d_n2", grid=(M // tm, D // tn), in_specs=[asp, asp, wsp, wsp],
        out_specs=pl.BlockSpec((tm, tn), lambda i, j: (i, j)), out_shape=jax.ShapeDtypeStruct((M, D), f32),
        compiler_params=_params("parallel", "parallel"),
    )(dg, du, w_gate, w_up)


def _gates(psm, w2p, gate_b, alog, dtb):
    Lp = psm.shape[0]
    tr = _tile(Lp, 256)

    def body(p_ref, w_ref, b_ref, a_ref, t_ref, gb_ref, la_ref):
        i = pl.program_id(0)
        psm_ = p_ref[...]
        lane = lax.broadcasted_iota(jnp.int32, psm_.shape, 1)
        rowi = lax.broadcasted_iota(jnp.int32, (tr, 1), 0) + i * tr
        g = -jnp.exp(a_ref[...]) * _softplus(psm_ + t_ref[...])
        beta = _sigmoid(psm_)
        gb = jnp.where(lane < GDN_H, g, jnp.where(lane < 2 * GDN_H, beta, 0.0))
        gb_ref[...] = gb * (rowi >= PAD).astype(f32)
        logit = _mmb(psm_, w_ref[...], NN) + b_ref[...]
        la_ref[...] = _log_sigmoid(logit) * (1.0 / GATE_NORMALIZER)

    row = pl.BlockSpec((tr, SM_W), lambda i: (i, 0))
    return pl.pallas_call(
        body, name="gates", grid=(Lp // tr,),
        in_specs=[row, pl.BlockSpec((SM_W, GLA_QK), lambda i: (0, 0)), pl.BlockSpec((1, GLA_QK), lambda i: (0, 0)),
                  pl.BlockSpec((1, SM_W), lambda i: (0, 0)), pl.BlockSpec((1, SM_W), lambda i: (0, 0))],
        out_specs=[row, pl.BlockSpec((tr, GLA_QK), lambda i: (i, 0))],
        out_shape=[jax.ShapeDtypeStruct((Lp, SM_W), f32), jax.ShapeDtypeStruct((Lp, GLA_QK), f32)],
        compiler_params=_params("parallel"),
    )(psm, w2p, gate_b, alog, dtb)


def _gates_bwd(psm, w2p, gate_b, alog, dtb, dgb, dla):
    Lp = psm.shape[0]
    tr = _tile(Lp, 256)

    def body(p_ref, w_ref, b_ref, a_ref, t_ref, dgb_ref, dla_ref, dp_ref, gw_ref, gb_ref, ga_ref, gt_ref):
        i = pl.program_id(0)
        psm_ = p_ref[...]
        lane = lax.broadcasted_iota(jnp.int32, psm_.shape, 1)
        rowi = lax.broadcasted_iota(jnp.int32, (tr, 1), 0) + i * tr
        d = dgb_ref[...] * (rowi >= PAD).astype(f32)
        ea = jnp.exp(a_ref[...])
        z = psm_ + t_ref[...]
        is_g = lane < GDN_H
        dz = jnp.where(is_g, -ea * _sigmoid(z) * d, 0.0)
        dalog = jnp.where(is_g, -ea * _softplus(z) * d, 0.0)
        beta = _sigmoid(psm_)
        dbeta = jnp.where(jnp.logical_and(lane >= GDN_H, lane < 2 * GDN_H), beta * (1.0 - beta) * d, 0.0)
        logit = _mmb(psm_, w_ref[...], NN) + b_ref[...]
        dlogit = dla_ref[...] * (_sigmoid(-logit) * (1.0 / GATE_NORMALIZER))
        dlr = _mmb(dlogit, w_ref[...], NT)
        dp_ref[...] = (dz + dbeta + dlr).astype(bf16)
        gw = _mmb(psm_, dlogit, TN)
        gb = jnp.sum(dlogit, axis=0, keepdims=True)
        ga = jnp.sum(dalog, axis=0, keepdims=True)
        gt = jnp.sum(dz, axis=0, keepdims=True)

        @pl.when(i == 0)
        def _():
            gw_ref[...] = gw
            gb_ref[...] = gb
            ga_ref[...] = ga
            gt_ref[...] = gt

        @pl.when(i > 0)
        def _():
            gw_ref[...] += gw
            gb_ref[...] += gb
            ga_ref[...] += ga
            gt_ref[...] += gt

    row = pl.BlockSpec((tr, SM_W), lambda i: (i, 0))
    wsp = pl.BlockSpec((SM_W, GLA_QK), lambda i: (0, 0))
    bsp = pl.BlockSpec((1, GLA_QK), lambda i: (0, 0))
    vsp = pl.BlockSpec((1, SM_W), lambda i: (0, 0))
    return pl.pallas_call(
        body, name="gates_bwd", grid=(Lp // tr,),
        in_specs=[row, wsp, bsp, vsp, vsp, row, pl.BlockSpec((tr, GLA_QK), lambda i: (i, 0))],
        out_specs=[row, wsp, bsp, vsp, vsp],
        out_shape=[jax.ShapeDtypeStruct((Lp, SM_W), bf16), jax.ShapeDtypeStruct((SM_W, GLA_QK), f32),
                   jax.ShapeDtypeStruct((1, GLA_QK), f32), jax.ShapeDtypeStruct((1, SM_W), f32),
                   jax.ShapeDtypeStruct((1, SM_W), f32)],
        compiler_params=_params("arbitrary"),
    )(psm, w2p, gate_b, alog, dtb, dgb, dla)


def _conv_pre(x_ext, w, n):
    rows = x_ext.shape[0]
    y = x_ext * w[CONV_K - 1:CONV_K, :]
    for s in range(1, CONV_K):
        y = y + pltpu.roll(x_ext, s, 0) * w[CONV_K - 1 - s:CONV_K - s, :]
    return y[rows - n:, :]


def _conv(proj, cw, side=None):
    Lp = proj.shape[0]
    W = cw.shape[1]
    tr = _tile(Lp, 256, 64)
    tc = _tile(W, 1024, 128)
    c0 = C_QKV // tc

    def body(h_ref, x_ref, w_ref, o_ref):
        i = pl.program_id(1)
        halo = jnp.where(i == 0, 0.0, h_ref[...])
        x_ext = jnp.concatenate([halo, x_ref[...]], axis=0)
        o_ref[...] = _silu(_conv_pre(x_ext, w_ref[...], tr))

    out = _call(
        body, name="conv", grid=(W // tc, Lp // tr),
        in_specs=[pl.BlockSpec((8, tc), lambda j, i: (jnp.maximum(i * (tr // 8) - 1, 0), j + c0)),
                  pl.BlockSpec((tr, tc), lambda j, i: (i, j + c0)),
                  pl.BlockSpec((CONV_K, tc), lambda j, i: (0, j))],
        out_specs=[pl.BlockSpec((tr, tc), lambda j, i: (i, j))],
        out_shape=[jax.ShapeDtypeStruct((Lp, W), f32)], args=[proj, proj, cw],
        sem=("parallel", "parallel"), side=side)
    return out[0] if side is None else (out[0], out[1:])


def _conv_bwd(proj, cw, dy, dproj, side=None):
    Lp = proj.shape[0]
    W = cw.shape[1]
    tr = _tile(Lp, 256, 64)
    tc = _tile(W, 1024, 128)
    c0 = C_QKV // tc
    nr = Lp // tr
    last8 = Lp // 8 - 1

    def body(xp_ref, x_ref, xn_ref, w_ref, d_ref, dn_ref, dproj_ref, o_ref, gw_ref):
        del dproj_ref
        i = pl.program_id(1)
        w = w_ref[...]
        xp = jnp.where(i == 0, 0.0, xp_ref[...])
        x_ext = jnp.concatenate([xp, x_ref[...], xn_ref[...]], axis=0)
        n = tr + 8
        pre = _conv_pre(x_ext, w, n)
        dn = jnp.where(i == nr - 1, 0.0, dn_ref[...])
        dpre = jnp.concatenate([d_ref[...], dn], axis=0) * _dsilu(pre)
        dx = dpre * w[CONV_K - 1:CONV_K, :]
        for s in range(1, CONV_K):
            dx = dx + pltpu.roll(dpre, n - s, 0) * w[CONV_K - 1 - s:CONV_K - s, :]
        o_ref[...] = dx[:tr, :].astype(bf16)
        dp = dpre[:tr, :]
        rows = []
        for k in range(CONV_K):
            xs = x_ext if k == CONV_K - 1 else pltpu.roll(x_ext, CONV_K - 1 - k, 0)
            rows.append(jnp.sum(dp * xs[8:8 + tr, :], axis=0, keepdims=True))
        gw = jnp.concatenate(rows, axis=0)

        @pl.when(i == 0)
        def _():
            gw_ref[...] = gw

        @pl.when(i > 0)
        def _():
            gw_ref[...] += gw

    cur = pl.BlockSpec((tr, tc), lambda j, i: (i, j))
    nxt = pl.BlockSpec((8, tc), lambda j, i: (jnp.minimum((i + 1) * (tr // 8), last8), j))
    pcur = pl.BlockSpec((tr, tc), lambda j, i: (i, j + c0))
    pprev = pl.BlockSpec((8, tc), lambda j, i: (jnp.maximum(i * (tr // 8) - 1, 0), j + c0))
    pnext = pl.BlockSpec((8, tc), lambda j, i: (jnp.minimum((i + 1) * (tr // 8), last8), j + c0))
    wsp = pl.BlockSpec((CONV_K, tc), lambda j, i: (0, j))
    out = _call(
        body, name="conv_bwd", grid=(W // tc, nr),
        in_specs=[pprev, pcur, pnext, wsp, cur, nxt, _ANY], out_specs=[pcur, wsp],
        out_shape=[jax.ShapeDtypeStruct(dproj.shape, dproj.dtype), jax.ShapeDtypeStruct((CONV_K, W), f32)],
        aliases={6: 0}, args=[proj, proj, proj, cw, dy, dy, dproj], sem=("parallel", "arbitrary"), side=side)
    return out[0], out[1], out[2:]


GDN_FWD_GROUP = 3


def _gdn_group(Lp, most):
    n = Lp // GDN_C
    return next(g for g in range(most, 0, -1) if n % g == 0)


def _gdn_heads(x_ref, gb_ref, group):
    qs, ks, vs, bs, gs = [], [], [], [], []
    for chunk in range(group):
        r = slice(chunk * GDN_C, (chunk + 1) * GDN_C)
        gbv = gb_ref[r, :]
        for h in range(GDN_H):
            qs.append(x_ref[r, Q0 + h * GDN_DK:Q0 + (h + 1) * GDN_DK])
            ks.append(x_ref[r, K0 + h * GDN_DK:K0 + (h + 1) * GDN_DK])
            vs.append(x_ref[r, V0 + h * GDN_DV:V0 + (h + 1) * GDN_DV])
            bs.append(gbv[:, GDN_H + h:GDN_H + h + 1])
            gs.append(gbv[:, h:h + 1])
    return qs, ks, vs, bs, gs


def _gdn_fwd(qkvc, gb, side=None):
    Lp = qkvc.shape[0]
    group = _gdn_group(Lp, GDN_FWD_GROUP)
    rows = group * GDN_C
    steps = Lp // rows
    R = range(GDN_H)

    def body(x_ref, gb_ref, o_ref, sall_ref, pall_ref, s_scr):
        @pl.when(pl.program_id(0) == 0)
        def _():
            s_scr[...] = jnp.zeros_like(s_scr)

        S2, o, p, entering = _gdn_chunk([s_scr[h] for h in R], *_gdn_heads(x_ref, gb_ref, group))
        for h in R:
            s_scr[h] = S2[h]
        for chunk in range(group):
            for h in R:
                i = chunk * GDN_H + h
                o_ref[chunk * GDN_C:(chunk + 1) * GDN_C, h * GDN_DV:(h + 1) * GDN_DV] = o[i]
                pall_ref[chunk, h] = p[i]
                sall_ref[chunk, h] = entering[i]

    out = _call(
        body, name="gdn_fwd", grid=(steps,),
        in_specs=[pl.BlockSpec((rows, QKV_W), lambda n: (n, 0)), pl.BlockSpec((rows, SM_W), lambda n: (n, 0))],
        out_specs=[pl.BlockSpec((rows, GDN_V), lambda n: (n, 0)),
                   pl.BlockSpec((group, GDN_H, GDN_DK, GDN_DV), lambda n: (n, 0, 0, 0)),
                   pl.BlockSpec((group, GDN_H, GDN_C, GDN_C), lambda n: (n, 0, 0, 0))],
        out_shape=[jax.ShapeDtypeStruct((Lp, GDN_V), f32),
                   jax.ShapeDtypeStruct((Lp // GDN_C, GDN_H, GDN_DK, GDN_DV), f32),
                   jax.ShapeDtypeStruct((Lp // GDN_C, GDN_H, GDN_C, GDN_C), f32)],
        scratch_shapes=[pltpu.VMEM((GDN_H, GDN_DK, GDN_DV), f32)], args=[qkvc, gb], sem=("arbitrary",), side=side)
    return out[0], out[1], out[2], out[3:]


def _gdn_bwd(qkvc, gb, sall, pall, do, side=None):
    Lp = qkvc.shape[0]
    group = 1
    rows = group * GDN_C
    steps = Lp // rows
    R = range(GDN_H)

    def body(x_ref, gb_ref, sall_ref, pall_ref, do_ref, dx_ref, dgb_ref, ds_scr):
        @pl.when(pl.program_id(0) == 0)
        def _():
            ds_scr[...] = jnp.zeros_like(ds_scr)

        lane = lax.broadcasted_iota(jnp.int32, (GDN_C, SM_W), 1)
        ps = [pall_ref[chunk, h] for chunk in range(group) for h in R]
        _, vjp = jax.vjp(lambda *a: _gdn_chunk(*a, Ps=ps)[:2],
                         [sall_ref[0, h] for h in R], *_gdn_heads(x_ref, gb_ref, group))
        do = [do_ref[chunk * GDN_C:(chunk + 1) * GDN_C, h * GDN_DV:(h + 1) * GDN_DV]
              for chunk in range(group) for h in R]
        dS, dq, dk, dv, dbeta, dg = vjp(([ds_scr[h] for h in R], do))
        for h in R:
            ds_scr[h] = dS[h]
        for chunk in range(group):
            r = slice(chunk * GDN_C, (chunk + 1) * GDN_C)
            acc = jnp.zeros((GDN_C, SM_W), f32)
            for h in R:
                i = chunk * GDN_H + h
                dx_ref[r, Q0 + h * GDN_DK:Q0 + (h + 1) * GDN_DK] = dq[i]
                dx_ref[r, K0 + h * GDN_DK:K0 + (h + 1) * GDN_DK] = dk[i]
                dx_ref[r, V0 + h * GDN_DV:V0 + (h + 1) * GDN_DV] = dv[i]
                acc = acc + jnp.where(lane == h, dg[i], 0.0) + jnp.where(lane == GDN_H + h, dbeta[i], 0.0)
            dgb_ref[r, :] = acc

    rev = lambda n: (steps - 1 - n, 0)
    out = _call(
        body, name="gdn_bwd", grid=(steps,),
        in_specs=[pl.BlockSpec((rows, QKV_W), rev), pl.BlockSpec((rows, SM_W), rev),
                  pl.BlockSpec((1, GDN_H, GDN_DK, GDN_DV), lambda n: (steps - 1 - n, 0, 0, 0)),
                  pl.BlockSpec((group, GDN_H, GDN_C, GDN_C), lambda n: (steps - 1 - n, 0, 0, 0)),
                  pl.BlockSpec((rows, GDN_V), rev)],
        out_specs=[pl.BlockSpec((rows, QKV_W), rev), pl.BlockSpec((rows, SM_W), rev)],
        out_shape=[jax.ShapeDtypeStruct((Lp, QKV_W), f32), jax.ShapeDtypeStruct((Lp, SM_W), f32)],
        scratch_shapes=[pltpu.VMEM((GDN_H, GDN_DK, GDN_DV), f32)], args=[qkvc, gb, sall, pall, do],
        sem=("arbitrary",), side=side)
    return out[0], out[1], out[2:]


GLA_BLOCK = 64


def _gla_group(Lp):
    nb = Lp // GLA_BLOCK
    return next(g for g in (3, 2, 1) if nb % g == 0)


def _gla_slices(h):
    sq = slice(h * GLA_DK, (h + 1) * GLA_DK)
    sk = slice(GLA_QK + h * GLA_DK, GLA_QK + (h + 1) * GLA_DK)
    sv = slice(2 * GLA_QK + h * GLA_DV, 2 * GLA_QK + (h + 1) * GLA_DV)
    return sq, sk, sv


def _gla_heads(x_ref, la_ref, group):
    qs, ks, vs, ls = [], [], [], []
    for blk in range(group):
        r = slice(blk * GLA_BLOCK, (blk + 1) * GLA_BLOCK)
        for h in range(GLA_H):
            sq, sk, sv = _gla_slices(h)
            qs.append(x_ref[r, sq])
            ks.append(x_ref[r, sk])
            vs.append(x_ref[r, sv])
            ls.append(la_ref[r, sq])
    return qs, ks, vs, ls


def _gla_fwd(proj, la):
    Lp = proj.shape[0]
    group = _gla_group(Lp)
    rows = group * GLA_BLOCK
    steps = Lp // rows
    R = range(GLA_H)

    def body(x_ref, la_ref, o_ref, sall_ref, s_scr):
        @pl.when(pl.program_id(0) == 0)
        def _():
            s_scr[...] = jnp.zeros_like(s_scr)

        Sts = [s_scr[h] for h in R]
        for h in R:
            sall_ref[0, h] = Sts[h]
        St2, o = _gla_blocks(Sts, *_gla_heads(x_ref, la_ref, group))
        for h in R:
            s_scr[h] = St2[h]
        for blk in range(group):
            for h in R:
                o_ref[blk * GLA_BLOCK:(blk + 1) * GLA_BLOCK, h * GLA_DV:(h + 1) * GLA_DV] = o[blk * GLA_H + h]

    return pl.pallas_call(
        body, name="gla_fwd", grid=(steps,),
        in_specs=[pl.BlockSpec((rows, G_W), lambda n: (n, C_G // G_W)),
                  pl.BlockSpec((rows, GLA_QK), lambda n: (n, 0))],
        out_specs=[pl.BlockSpec((rows, GLA_V), lambda n: (n, 0)),
                   pl.BlockSpec((1, GLA_H, GLA_DV, GLA_DK), lambda n: (n, 0, 0, 0))],
        out_shape=[jax.ShapeDtypeStruct((Lp, GLA_V), f32),
                   jax.ShapeDtypeStruct((steps, GLA_H, GLA_DV, GLA_DK), f32)],
        scratch_shapes=[pltpu.VMEM((GLA_H, GLA_DV, GLA_DK), f32)],
        compiler_params=_params("arbitrary"),
    )(proj, la)


def _gla_bwd(proj, la, sall, do, dproj, side=None):
    Lp = proj.shape[0]
    group = _gla_group(Lp)
    rows = group * GLA_BLOCK
    steps = Lp // rows
    R = range(GLA_H)

    def body(x_ref, la_ref, sall_ref, do_ref, dproj_ref, dx_ref, dla_ref, ds_scr):
        del dproj_ref

        @pl.when(pl.program_id(0) == 0)
        def _():
            ds_scr[...] = jnp.zeros_like(ds_scr)

        _, vjp = jax.vjp(_gla_blocks, [sall_ref[0, h] for h in R], *_gla_heads(x_ref, la_ref, group))
        do = [do_ref[blk * GLA_BLOCK:(blk + 1) * GLA_BLOCK, h * GLA_DV:(h + 1) * GLA_DV]
              for blk in range(group) for h in R]
        dS, dq, dk, dv, dl = vjp(([ds_scr[h] for h in R], do))
        for h in R:
            ds_scr[h] = dS[h]
        for blk in range(group):
            r = slice(blk * GLA_BLOCK, (blk + 1) * GLA_BLOCK)
            for h in R:
                sq, sk, sv = _gla_slices(h)
                i = blk * GLA_H + h
                dx_ref[r, sq] = dq[i].astype(bf16)
                dx_ref[r, sk] = dk[i].astype(bf16)
                dx_ref[r, sv] = dv[i].astype(bf16)
                dla_ref[r, sq] = dl[i]

    x_spec = pl.BlockSpec((rows, G_W), lambda n: (steps - 1 - n, C_G // G_W))
    rev = lambda n: (steps - 1 - n, 0)
    out = _call(
        body, name="gla_bwd", grid=(steps,),
        in_specs=[x_spec, pl.BlockSpec((rows, GLA_QK), rev),
                  pl.BlockSpec((1, GLA_H, GLA_DV, GLA_DK), lambda n: (steps - 1 - n, 0, 0, 0)),
                  pl.BlockSpec((rows, GLA_V), rev), _ANY],
        out_specs=[x_spec, pl.BlockSpec((rows, GLA_QK), rev)],
        out_shape=[jax.ShapeDtypeStruct(dproj.shape, dproj.dtype), jax.ShapeDtypeStruct((Lp, GLA_QK), f32)],
        aliases={4: 0}, scratch_shapes=[pltpu.VMEM((GLA_H, GLA_DV, GLA_DK), f32)],
        args=[proj, la, sall, do, dproj], sem=("arbitrary",), side=side)
    return out[0], out[1], out[2:]


def _gated_norm_fn(og, ol, zr, wg, wl):
    outs = []
    for h in range(GDN_H):
        s = slice(h * GDN_DV, (h + 1) * GDN_DV)
        outs.append(_rms(og[:, s])[0] * wg * _silu(zr[:, s]))
    for h in range(GLA_H):
        s = slice(h * GLA_DV, (h + 1) * GLA_DV)
        sr = slice(GDN_V + h * GLA_DV, GDN_V + (h + 1) * GLA_DV)
        outs.append(_rms(ol[:, s])[0] * wl * _silu(zr[:, sr]))
    return jnp.concatenate(outs, axis=-1)


def _gated_norm(og, ol, proj, wg, wl):
    Lp = og.shape[0]
    tr = _tile(Lp, 256)

    def body(og_ref, ol_ref, zr_ref, wg_ref, wl_ref, o_ref):
        o_ref[...] = _gated_norm_fn(og_ref[...], ol_ref[...], zr_ref[...], wg_ref[...], wl_ref[...]).astype(bf16)

    return pl.pallas_call(
        body, name="gated_norm", grid=(Lp // tr,),
        in_specs=[pl.BlockSpec((tr, GDN_V), lambda i: (i, 0)), pl.BlockSpec((tr, GLA_V), lambda i: (i, 0)),
                  pl.BlockSpec((tr, ZR_W), lambda i: (i, C_ZR // ZR_W)),
                  pl.BlockSpec((1, GDN_DV), lambda i: (0, 0)), pl.BlockSpec((1, GLA_DV), lambda i: (0, 0))],
        out_specs=pl.BlockSpec((tr, ZR_W), lambda i: (i, 0)),
        out_shape=jax.ShapeDtypeStruct((Lp, ZR_W), bf16),
        compiler_params=_params("parallel"),
    )(og, ol, proj, wg, wl)


def _gated_norm_bwd(og, ol, proj, wg, wl, dmix):
    Lp = og.shape[0]
    tr = _tile(Lp, 256)

    def body(og_ref, ol_ref, zr_ref, wg_ref, wl_ref, d_ref, dog_ref, dol_ref, dzr_ref, gwg_ref, gwl_ref):
        i = pl.program_id(0)
        _, vjp = jax.vjp(_gated_norm_fn, og_ref[...], ol_ref[...], zr_ref[...], wg_ref[...], wl_ref[...])
        dog, dol, dzr, gwg, gwl = vjp(d_ref[...])
        dog_ref[...] = dog
        dol_ref[...] = dol
        dzr_ref[...] = dzr.astype(bf16)

        @pl.when(i == 0)
        def _():
            gwg_ref[...] = gwg
            gwl_ref[...] = gwl

        @pl.when(i > 0)
        def _():
            gwg_ref[...] += gwg
            gwl_ref[...] += gwl

    og_spec = pl.BlockSpec((tr, GDN_V), lambda i: (i, 0))
    ol_spec = pl.BlockSpec((tr, GLA_V), lambda i: (i, 0))
    zr_spec = pl.BlockSpec((tr, ZR_W), lambda i: (i, C_ZR // ZR_W))
    vg = pl.BlockSpec((1, GDN_DV), lambda i: (0, 0))
    vl = pl.BlockSpec((1, GLA_DV), lambda i: (0, 0))
    return pl.pallas_call(
        body, name="gated_norm_bwd", grid=(Lp // tr,),
        in_specs=[og_spec, ol_spec, zr_spec, vg, vl, pl.BlockSpec((tr, ZR_W), lambda i: (i, 0))],
        out_specs=[og_spec, ol_spec, zr_spec, vg, vl],
        out_shape=[jax.ShapeDtypeStruct((Lp, GDN_V), f32), jax.ShapeDtypeStruct((Lp, GLA_V), f32),
                   jax.ShapeDtypeStruct((Lp, C_END), bf16),
                   jax.ShapeDtypeStruct((1, GDN_DV), f32), jax.ShapeDtypeStruct((1, GLA_DV), f32)],
        compiler_params=_params("arbitrary"),
    )(og, ol, proj, wg, wl, dmix)


def _adamw_rule(g_, w_, m_, v_):
    c1 = 1.0 - ADAM_B1 ** ADAM_STEP
    c2 = 1.0 - ADAM_B2 ** ADAM_STEP
    m2 = ADAM_B1 * m_ + (1.0 - ADAM_B1) * g_
    v2 = ADAM_B2 * v_ + (1.0 - ADAM_B2) * (g_ * g_)
    return -ADAM_LR * ((m2 / c1) / (jnp.sqrt(v2 / c2) + ADAM_EPS) + ADAM_WD * w_), m2, v2


def _adamw(g, w, m, v, name, copy_g=False):
    R, C = g.shape
    tr = _tile(R, 256, 8) if R % 8 == 0 and R > 256 else R

    def body(g_ref, w_ref, m_ref, v_ref, *o_refs):
        g_ = g_ref[...]
        if copy_g:
            o_refs[0][...] = g_
        d_ref, mo_ref, vo_ref = o_refs[-3:]
        d_ref[...], mo_ref[...], vo_ref[...] = _adamw_rule(g_, w_ref[...], m_ref[...], v_ref[...])

    blk = pl.BlockSpec((tr, C), lambda i: (i, 0))
    n_out = 4 if copy_g else 3
    return pl.pallas_call(
        body, name=name, grid=(R // tr,), in_specs=[blk] * 4, out_specs=[blk] * n_out,
        out_shape=[jax.ShapeDtypeStruct((R, C), f32)] * n_out,
        compiler_params=_params("parallel"),
    )(g, w, m, v)


def _adamw_transposed(gt, w, m, v, name):
    n, C, rb = gt.shape
    assert w.shape == (C, n * rb), (gt.shape, w.shape)

    def body(g_ref, w_ref, m_ref, v_ref, go_ref, d_ref, mo_ref, vo_ref):
        g_ = g_ref[...].astype(f32)
        go_ref[...] = g_
        d_ref[...], mo_ref[...], vo_ref[...] = _adamw_rule(g_, w_ref[...], m_ref[...], v_ref[...])

    blk = pl.BlockSpec((C, rb), lambda j: (0, j))
    return pl.pallas_call(
        body, name=name, grid=(n,), in_specs=[pl.BlockSpec((None, C, rb), lambda j: (j, 0, 0))] + [blk] * 3,
        out_specs=[blk] * 4, out_shape=[jax.ShapeDtypeStruct((C, n * rb), f32)] * 4,
        compiler_params=_params("parallel"),
    )(gt, w, m, v)


def _sum_slots(r, name):
    n, R, C = r.shape
    tr = _tile(R, 128, 16) if R % 16 == 0 and R > 128 else R

    def body(r_ref, o_ref):
        acc = r_ref[0].astype(f32)
        for s in range(1, n):
            acc = acc + r_ref[s].astype(f32)
        o_ref[...] = acc

    return pl.pallas_call(
        body, name=name, grid=(R // tr,),
        in_specs=[pl.BlockSpec((n, tr, C), lambda i: (0, i, 0))],
        out_specs=pl.BlockSpec((tr, C), lambda i: (i, 0)),
        out_shape=jax.ShapeDtypeStruct((R, C), f32),
        compiler_params=_params("parallel"),
    )(r)


SIBLING_PARTS = 8


class _Siblings:
    def __init__(self, arrays):
        self.arrays = list(arrays)
        self.n = len(self.arrays)
        self.parts = [next(p for p in range(SIBLING_PARTS, 0, -1) if a.shape[0] % (8 * p) == 0 or p == 1)
                      for a in self.arrays]
        total = sum(self.parts)
        self.out_shape = [jax.ShapeDtypeStruct((2,) + a.shape, a.dtype) for a in self.arrays]
        self.sems = [pltpu.SemaphoreType.DMA((total,)), pltpu.SemaphoreType.DMA((total,)),
                     pltpu.SemaphoreType.DMA((self.n,))]

    def hooks(self, ins, outs, send, recv, lsem):
        def copies():
            x, y, c = lax.axis_index("x"), lax.axis_index("y"), lax.axis_index("c")
            out, k = [], 0
            for a in range(self.n):
                out.append((pltpu.make_async_copy(ins[a], outs[a].at[c], lsem.at[a]), None))
                rows = self.arrays[a].shape[0] // self.parts[a]
                for part in range(self.parts[a]):
                    r = pl.ds(part * rows, rows)
                    mk = lambda dst, a=a, r=r, k=k: pltpu.make_async_remote_copy(
                        src_ref=ins[a].at[r], dst_ref=dst.at[r], send_sem=send.at[k], recv_sem=recv.at[k],
                        device_id=(x, y, 1 - c), device_id_type=MESH)
                    out.append((mk(outs[a].at[c]), mk(outs[a].at[1 - c])))
                    k += 1
            return out

        return _start_wait(copies)


def _comm_now(name, sides):
    total = sum(s.n for s in sides)

    def body(*refs):
        ins, outs, sems = refs[:total], refs[total:2 * total], refs[2 * total:]
        hooks, o = [], 0
        for i, s in enumerate(sides):
            hooks.append(s.hooks(ins[o:o + s.n], outs[o:o + s.n], *sems[3 * i:3 * i + 3]))
            o += s.n
        for start, _ in hooks:
            start()
        for _, wait in hooks:
            wait()

    out = pl.pallas_call(
        body, name=name, in_specs=[_ANY] * total, out_specs=[_ANY] * total,
        out_shape=[sh for s in sides for sh in s.out_shape], scratch_shapes=[sm for s in sides for sm in s.sems],
    )(*[a for s in sides for a in s.arrays])
    res, o = [], 0
    for s in sides:
        res.append(list(out[o:o + s.n]))
        o += s.n
    return res


def _cat_cols(g):
    return jnp.concatenate([g[i] for i in range(N_CHIP)], axis=-1)


def _row_slabs(a):
    return a.reshape(N_DEV, a.shape[0] // N_DEV, a.shape[1])


def _w_in_columns(g_wp, g_wsm):
    return jnp.concatenate([g_wp[:, C_QKV:C_END], g_wp[:, C_ZR:C_ZR + GDN_V], g_wsm[:, :SM_LR],
                            g_wp[:, C_G:C_G + G_W], g_wp[:, C_ZR + GDN_V:C_ZR + ZR_W],
                            g_wsm[:, SM_LR:SM_LR + GATE_RANK]], axis=1)


def _step(x, loss_target, p, meta, shard):
    _, S, D = x.shape
    alog_p = jnp.pad(p["gdn_a_log"], ((0, 0), (0, SM_W - GDN_H)))
    dtb_p = jnp.pad(p["gdn_dt_bias"], ((0, 0), (0, SM_W - GDN_H)))
    m64 = jnp.concatenate([jnp.zeros((PAD, D), f32), meta], axis=0)
    gate_b, gdn_norm_w, gla_norm_w = p["gla_gate_b"], p["gdn_norm_w"], p["gla_norm_w"]

    later = ("w_up", "w_out", "w_gate", "w_down")
    h0, n1, in_bf16, (w_in4, conv4, w24) = _embed_norm(
        x, m64, p["attn_norm_w"], to_bf16=[shard[k] for k in later],
        side=_Gather([shard["w_in"], shard["gdn_conv_w"], shard["gla_gate_w2"]]))
    shard = dict(shard, **dict(zip(later, in_bf16)))
    w_in, conv_w, w2 = _cat_cols(w_in4), _cat_cols(conv4), _cat_cols(w24)
    wp = jnp.concatenate([w_in[:, R_Z:R_AB], w_in[:, R_GR:R_LR], w_in[:, R_G:R_GR], w_in[:, R_QKV:R_Z]], axis=1)
    wsm = jnp.concatenate([w_in[:, R_AB:R_G], w_in[:, R_LR:R_END],
                           jnp.zeros((D, SM_W - SM_LR - GATE_RANK), w_in.dtype)], axis=1)
    w2p = jnp.pad(w2, ((SM_LR, SM_W - SM_LR - GATE_RANK), (0, 0)))
    proj, (w_up,) = _mm(n1, wp, "nn", "proj", side=_Gather([shard["w_up"]], by_columns=True))
    psm = _mm(n1, wsm, "nn", "proj_small")
    gb, la = _gates(psm, w2p, gate_b, alog_p, dtb_p)
    qkvc, (w_out4,) = _conv(proj, conv_w, side=_Gather([shard["w_out"]]))
    w_out = w_out4.reshape(-1, D)
    og, sall, pall, (w_gate,) = _gdn_fwd(qkvc, gb, side=_Gather([shard["w_gate"]], by_columns=True))
    ol, stall = _gla_fwd(proj, la)
    mixed = _gated_norm(og, ol, proj, gdn_norm_w, gla_norm_w)
    attn = _mm(mixed, w_out, "nn", "out_proj")
    h1, n2 = _add_norm(h0, attn, p["ffn_norm_w"])
    act, act_dgate, act_dup, (w_down4,) = _ffn_in(n2, w_gate, w_up, side=_Gather([shard["w_down"]]))
    w_down = w_down4.reshape(-1, D)
    ffn = _mm(act, w_down, "nn", "ffn_down", **WHOLE_K)
    dh2, dh2b, lossp, g_final = _final(h1, ffn, loss_target, p["final_norm_w"])

    g_down = _mm(act, dh2b, "tn", "g_w_down", out_dtype=bf16, **WHOLE_K_T)
    dg, du = _ffn_dact(dh2b, w_down, act_dgate, act_dup)
    g_gate = _mm(n2, dg, "tn", "g_w_gate", tm_cap=512, tn_cap=1408, tk_cap=2752, out_dtype=bf16, col_slabs=True)
    g_up = _mm(n2, du, "tn", "g_w_up", tm_cap=512, tn_cap=1408, tk_cap=2752, out_dtype=bf16, col_slabs=True)
    dn2 = _ffn_dn(dg, du, w_gate, w_up)
    dh1, dh1b, g_ffn_norm = _norm_bwd(dn2, h1, dh2, p["ffn_norm_w"])
    dmix = _mm(dh1b, w_out, "nt", "d_mixed")
    g_out = _mm(mixed, dh1b, "tn", "g_w_out", out_dtype=bf16, **WHOLE_K_T)
    dog, dol, dproj, g_gdn_norm, g_gla_norm = _gated_norm_bwd(og, ol, proj, gdn_norm_w, gla_norm_w, dmix)
    dproj, dla, (r_down,) = _gla_bwd(proj, la, stall, dol, dproj, side=_Exchange([_row_slabs(g_down)]))
    dqkvc, dgb, (r_gate, r_up, r_out, h_down) = _gdn_bwd(
        qkvc, gb, sall, pall, dog,
        side=_Sides(_Exchange([g_gate, g_up, _row_slabs(g_out)]), _Siblings([_sum_slots(r_down, "sum_w_down")])))
    dproj, g_conv, (h_gate,) = _conv_bwd(proj, conv_w, dqkvc, dproj,
                                         side=_Siblings([_sum_slots(r_gate, "sum_w_gate")]))
    dpsm, g_w2p, g_gate_b, g_alog, g_dtb = _gates_bwd(psm, w2p, gate_b, alog_p, dtb_p, dgb, dla)
    g_wsm = _mm(n1, dpsm, "tn", "g_w_in_small", out_dtype=bf16, **WHOLE_K_T)
    g_wp_a, (h_out,) = _mm(n1, dproj, "tn", "g_w_in_a", out_dtype=bf16, b_cols=(0, W_IN_SPLIT),
                           side=_Siblings([_sum_slots(r_out, "sum_w_out")]), **WHOLE_K_T)
    g_wp_b, r_in_a = _mm(n1, dproj, "tn", "g_w_in_b", out_dtype=bf16, b_cols=(W_IN_SPLIT, C_END - W_IN_SPLIT),
                         side=_Exchange([_row_slabs(g_wp_a), _row_slabs(g_wsm)]), **WHOLE_K_T)
    dn1, (r_in_b, h_up) = _mm(
        dproj, wp, "nt", "d_n1", plus=(dpsm, wsm),
        side=_Sides(_Exchange([_row_slabs(g_wp_b)]), _Siblings([_sum_slots(r_up, "sum_w_up")])), **WHOLE_K)
    s_wp = jnp.concatenate([_sum_slots(r_in_a[0], "sum_w_in_a"), _sum_slots(r_in_b, "sum_w_in_b")], axis=1)
    s_in = _w_in_columns(s_wp, _sum_slots(r_in_a[1], "sum_w_in_small"))
    in_by_chip = s_in.reshape(s_in.shape[0], N_CHIP, -1).transpose(1, 2, 0).astype(bf16)
    grad_x, g_meta, g_attn_norm, (h_in,) = _embed_norm_bwd(dn1, h0, dh1, p["attn_norm_w"], S,
                                                           side=_Exchange([], by_chip=[in_by_chip]))

    received = dict(w_in=h_in, w_gate=h_gate, w_up=h_up, w_out=h_out, w_down=h_down)
    small = dict(
        meta_tokens=g_meta, attn_norm_w=g_attn_norm, gdn_conv_w=g_conv, gdn_a_log=g_alog[:, :GDN_H],
        gdn_dt_bias=g_dtb[:, :GDN_H], gdn_norm_w=g_gdn_norm, gla_gate_w2=g_w2p[SM_LR:SM_LR + GATE_RANK],
        gla_gate_b=g_gate_b, gla_norm_w=g_gla_norm, ffn_norm_w=g_ffn_norm, final_norm_w=g_final)
    return lossp[0, 0], grad_x, received, small


_WEIGHTS = ("meta_tokens", "attn_norm_w", "w_in", "gdn_conv_w", "gdn_a_log", "gdn_dt_bias", "gdn_norm_w",
            "gla_gate_w2", "gla_gate_b", "gla_norm_w", "w_out", "ffn_norm_w", "w_gate", "w_up", "w_down",
            "final_norm_w")
_BIG_COLS = ("w_in", "w_gate", "w_up")
_BIG_ROWS = ("w_out", "w_down")
_SMALL_SHARDED = ("meta_tokens", "gdn_conv_w", "gla_gate_w2")


def kernel(x, meta_tokens, attn_norm_w, w_in, gdn_conv_w, gdn_a_log, gdn_dt_bias, gdn_norm_w, gla_gate_w2, gla_gate_b, gla_norm_w, w_out, ffn_norm_w, w_gate, w_up, w_down, final_norm_w, loss_target, m_meta_tokens, m_attn_norm_w, m_w_in, m_gdn_conv_w, m_gdn_a_log, m_gdn_dt_bias, m_gdn_norm_w, m_gla_gate_w2, m_gla_gate_b, m_gla_norm_w, m_w_out, m_ffn_norm_w, m_w_gate, m_w_up, m_w_down, m_final_norm_w, v_meta_tokens, v_attn_norm_w, v_w_in, v_gdn_conv_w, v_gdn_a_log, v_gdn_dt_bias, v_gdn_norm_w, v_gla_gate_w2, v_gla_gate_b, v_gla_norm_w, v_w_out, v_ffn_norm_w, v_w_gate, v_w_up, v_w_down, v_final_norm_w):
    w = dict(meta_tokens=meta_tokens, attn_norm_w=attn_norm_w, w_in=w_in, gdn_conv_w=gdn_conv_w, gdn_a_log=gdn_a_log,
             gdn_dt_bias=gdn_dt_bias, gdn_norm_w=gdn_norm_w, gla_gate_w2=gla_gate_w2, gla_gate_b=gla_gate_b,
             gla_norm_w=gla_norm_w, w_out=w_out, ffn_norm_w=ffn_norm_w, w_gate=w_gate, w_up=w_up, w_down=w_down,
             final_norm_w=final_norm_w)
    m = dict(meta_tokens=m_meta_tokens, attn_norm_w=m_attn_norm_w, w_in=m_w_in, gdn_conv_w=m_gdn_conv_w,
             gdn_a_log=m_gdn_a_log, gdn_dt_bias=m_gdn_dt_bias, gdn_norm_w=m_gdn_norm_w, gla_gate_w2=m_gla_gate_w2,
             gla_gate_b=m_gla_gate_b, gla_norm_w=m_gla_norm_w, w_out=m_w_out, ffn_norm_w=m_ffn_norm_w,
             w_gate=m_w_gate, w_up=m_w_up, w_down=m_w_down, final_norm_w=m_final_norm_w)
    v = dict(meta_tokens=v_meta_tokens, attn_norm_w=v_attn_norm_w, w_in=v_w_in, gdn_conv_w=v_gdn_conv_w,
             gdn_a_log=v_gdn_a_log, gdn_dt_bias=v_gdn_dt_bias, gdn_norm_w=v_gdn_norm_w, gla_gate_w2=v_gla_gate_w2,
             gla_gate_b=v_gla_gate_b, gla_norm_w=v_gla_norm_w, w_out=v_w_out, ffn_norm_w=v_ffn_norm_w,
             w_gate=v_w_gate, w_up=v_w_up, w_down=v_w_down, final_norm_w=v_final_norm_w)
    chip = 2 * lax.axis_index("x") + lax.axis_index("y")

    def two_d(a):
        return a.reshape(1, -1) if a.ndim == 1 else a.reshape(-1, a.shape[-1])

    w2d = {k: two_d(a) for k, a in w.items()}
    big = _BIG_COLS + _BIG_ROWS
    small = tuple(k for k in _WEIGHTS if k not in big)

    (meta4,), = _comm_now("gather_meta", [_Gather([w2d["meta_tokens"]])])
    shard = {k: w2d[k] for k in big + ("gdn_conv_w", "gla_gate_w2")}
    shard["w_in"] = w2d["w_in"].astype(bf16)
    lossp, grad_x, received, g = _step(x, loss_target, {k: w2d[k] for k in small}, _cat_cols(meta4), shard)
    loss = lax.psum(lossp, ("x", "y", "c"))

    sizes = [g[k].size for k in small]
    total = sum(sizes)
    rows = -(-total // 1024)
    rows += (-rows) % 8
    packed = jnp.concatenate([g[k].reshape(-1) for k in small] + [jnp.zeros((rows * 1024 - total,), f32)])
    (packed8,), = _comm_now("exchange_small", [_Exchange([], [packed.reshape(rows, 1024)])])
    red = {k: h.reshape(w2d[k].shape) for k, h in received.items() if k != "w_in"}
    psum_small = _sum_slots(packed8, "sum_small").reshape(-1)
    off = 0
    for k, n in zip(small, sizes):
        a = psum_small[off:off + n].reshape(g[k].shape)
        off += n
        if k in _SMALL_SHARDED:
            c = w2d[k].shape[1]
            a = lax.dynamic_slice_in_dim(a, chip * c, c, axis=1)
        red[k] = a

    grads, deltas, new_m, new_v = [], [], [], []
    for k in _WEIGHTS:
        shape = w[k].shape
        if k == "w_in":
            flip = lambda a: jnp.swapaxes(a, 1, 2).reshape(shape[2], shape[1])
            unflip = lambda a: jnp.swapaxes(a.reshape(shape[0], shape[2], shape[1]), 1, 2)
            out = _adamw_transposed(received[k], flip(w[k]), flip(m[k]), flip(v[k]), "adamw_" + k)
            gk, d, m2, v2 = [unflip(a) for a in out]
        elif k in big:
            gk, d, m2, v2 = [a.reshape(shape) for a in
                             _adamw(red[k], w2d[k], two_d(m[k]), two_d(v[k]), "adamw_" + k, copy_g=True)]
        else:
            gk = red[k].reshape(shape)
            d, m2, v2 = [a.reshape(shape) for a in _adamw(red[k], w2d[k], two_d(m[k]), two_d(v[k]), "adamw_" + k)]
        grads.append(gk)
        deltas.append(d)
        new_m.append(m2)
        new_v.append(v2)
    return (loss, grad_x, *grads, *deltas, *new_m, *new_v)
```

```python
import functools

import jax
import jax.numpy as jnp
from jax import lax
from jax.experimental import pallas as pl
from jax.experimental.pallas import tpu as pltpu

f32 = jnp.float32
bf16 = jnp.bfloat16
HIGH = lax.Precision.HIGH
MESH = pl.DeviceIdType.MESH

N_META = 16
CONV_K = 4
GDN_H, GDN_DK, GDN_DV, GDN_C = 8, 128, 128, 64
GLA_H, GLA_DK, GLA_DV, GLA_C = 4, 128, 256, 16
GATE_RANK = 16
GATE_NORMALIZER = 16.0
EPS = 1e-6
GDN_QK = GDN_H * GDN_DK
GDN_V = GDN_H * GDN_DV
GLA_QK = GLA_H * GLA_DK
GLA_V = GLA_H * GLA_DV
PAD = (-N_META) % GDN_C
OFF = PAD + N_META
ROWS = 64

R_QKV, R_Z, R_AB, R_G, R_GR, R_LR, R_END = 0, 3072, 4096, 4112, 6160, 7184, 7200
C_ZR, C_G, C_QKV, C_END = 0, 2048, 4096, 7168
W_IN_SPLIT = 3072
ZR_W = GDN_V + GLA_V
G_W = 2 * GLA_QK + GLA_V
QKV_W = 2 * GDN_QK + GDN_V
Q0, K0, V0 = 0, GDN_QK, 2 * GDN_QK
SM_W = 128
SM_LR = 2 * GDN_H

ADAM_LR, ADAM_B1, ADAM_B2, ADAM_EPS, ADAM_WD, ADAM_STEP = 0.001, 0.9, 0.999, 1e-08, 0.01, 10

VMEM_LIMIT_V7X = 56 * 1024 * 1024
LANES = 128
N_DEV = 8
N_CHIP = 4


def _params(*sem):
    return pltpu.CompilerParams(dimension_semantics=sem, vmem_limit_bytes=VMEM_LIMIT_V7X)


def _tile(n, cap, mult=16):
    best = None
    for d in range(mult, min(n, cap) + 1, mult):
        if n % d == 0:
            best = d
    assert best is not None, (n, cap, mult)
    return best


NN = ((1,), (0,))
NT = ((1,), (1,))
TN = ((0,), (0,))


def _dot(a, b, dims, prec=None):
    return lax.dot_general(a, b, (dims, ((), ())), precision=prec, preferred_element_type=f32)


def _mmb(a, b, dims):
    return _dot(a.astype(bf16), b.astype(bf16), dims)


def _sigmoid(x):
    return jax.nn.sigmoid(x)


def _silu(x):
    return x * _sigmoid(x)


def _dsilu(x):
    s = _sigmoid(x)
    return s * (1.0 + x * (1.0 - s))


def _log1p_exp_neg_abs(x):
    t = jnp.exp(-jnp.abs(x))
    u = 1.0 + t
    d = u - 1.0
    return jnp.where(d == 0.0, t, jnp.log(u) * (t / jnp.where(d == 0.0, 1.0, d)))


def _softplus(x):
    return jnp.maximum(x, 0.0) + _log1p_exp_neg_abs(x)


def _log_sigmoid(x):
    return jnp.minimum(x, 0.0) - _log1p_exp_neg_abs(x)


def _rms(x):
    r = lax.rsqrt(jnp.mean(x * x, axis=-1, keepdims=True) + EPS)
    return x * r, r


def _rms_bwd(dy, xh, r, w):
    t = dy * w
    return r * (t - xh * jnp.mean(t * xh, axis=-1, keepdims=True))


def _l2n(x):
    return x * lax.rsqrt(jnp.sum(x * x, axis=-1, keepdims=True) + EPS)


INV_LEAF = 8


def _same_block(C, b):
    sh = b.bit_length() - 1
    row = lax.broadcasted_iota(jnp.int32, (C, C), 0)
    col = lax.broadcasted_iota(jnp.int32, (C, C), 1)
    return lax.shift_right_logical(row, sh) == lax.shift_right_logical(col, sh)


def _tri_inv_impl(As):
    C = As[0].shape[0]
    R = range(len(As))
    row = lax.broadcasted_iota(jnp.int32, (C, C), 0)
    col = lax.broadcasted_iota(jnp.int32, (C, C), 1)
    eye = (row == col).astype(f32)
    b = INV_LEAF
    inner = _same_block(C, b)
    leaf = [jnp.where(inner, As[h], 0.0) for h in R]
    d = [eye - leaf[h] for h in R]
    pw = leaf
    n = 2
    while n < b:
        pw = [_dot(pw[h], pw[h], NN, HIGH) for h in R]
        d = [_dot(d[h], eye + pw[h], NN, HIGH) for h in R]
        n *= 2
    while b < C:
        outer = _same_block(C, 2 * b)
        level = jnp.logical_and(outer, jnp.logical_not(inner))
        ed = [_dot(jnp.where(level, As[h], 0.0), d[h], NN, HIGH) for h in R]
        d = [d[h] - _dot(d[h], ed[h], NN, HIGH) for h in R]
        inner = outer
        b *= 2
    return d


@jax.custom_vjp
def _tri_inv(As):
    return _tri_inv_impl(As)


def _tri_inv_fwd(As):
    d = _tri_inv_impl(As)
    return d, d


def _tri_inv_bwd(d, g):
    R = range(len(d))
    t = [_dot(d[h], g[h], TN, HIGH) for h in R]
    return ([-_dot(t[h], d[h], NT, HIGH) for h in R],)


_tri_inv.defvjp(_tri_inv_fwd, _tri_inv_bwd)


@jax.custom_vjp
def _tri_inv_known(As, Ps):
    del As
    return Ps


def _tri_inv_known_fwd(As, Ps):
    del As
    return Ps, Ps


def _tri_inv_known_bwd(d, g):
    return _tri_inv_bwd(d, g)[0], [jnp.zeros_like(x) for x in d]


_tri_inv_known.defvjp(_tri_inv_known_fwd, _tri_inv_known_bwd)


def _gdn_chunk(Ss, qrs, krs, vs, betas, gs, Ps=None):
    H = len(Ss)
    C, dk = qrs[0].shape
    R = range(len(qrs))
    row = lax.broadcasted_iota(jnp.int32, (C, C), 0)
    col = lax.broadcasted_iota(jnp.int32, (C, C), 1)
    causal = row >= col
    strict = row > col
    cf = causal.astype(f32)
    q = [_l2n(qrs[h]) * (dk ** -0.5) for h in R]
    k = [_l2n(krs[h]) for h in R]
    mc = [_rows_exact(cf, jnp.broadcast_to(gs[h], (C, C))) for h in R]
    gc = [mc[h][:, 0:1] for h in R]
    decay = [jnp.where(causal, jnp.exp(jnp.where(causal, mc[h] - mc[h].T, 0.0)), 0.0) for h in R]
    kb = [k[h] * betas[h] for h in R]
    a = [jnp.where(strict, _mmb(kb[h], k[h], NT) * decay[h], 0.0) for h in R]
    p = _tri_inv(a) if Ps is None else _tri_inv_known(a, Ps)
    egc = [jnp.exp(gc[h]) for h in R]
    u = [_mmb(p[h], vs[h] * betas[h], NN) for h in R]
    w = [_mmb(p[h], kb[h] * egc[h], NN) for h in R]
    qk = [jnp.where(causal, _mmb(q[h], k[h], NT) * decay[h], 0.0) for h in R]
    qe = [q[h] * egc[h] for h in R]
    gl = [gc[h][C - 1:C, :] for h in R]
    kd = [k[h] * jnp.exp(gl[h] - gc[h]) for h in R]
    egl = [jnp.exp(gl[h]) for h in R]
    S, o, entering = list(Ss), [], []
    for chunk in range(len(qrs) // H):
        idx = [chunk * H + h for h in range(H)]
        entering += S
        v_new = [u[i] - _mmb(w[i], S[h], NN) for h, i in enumerate(idx)]
        o += [_mmb(qe[i], S[h], NN) + _mmb(qk[i], v_new[h], NN) for h, i in enumerate(idx)]
        S = [S[h] * egl[i] + _mmb(kd[i], v_new[h], TN) for h, i in enumerate(idx)]
    return S, o, p, entering


def _rows_exact_impl(m01, x, dims):
    m = m01.astype(bf16)
    x1 = x.astype(bf16)
    r1 = x - x1.astype(f32)
    x2 = r1.astype(bf16)
    x3 = (r1 - x2.astype(f32)).astype(bf16)
    d = lambda y: _dot(m, y, dims)
    return d(x1) + (d(x2) + d(x3))


@jax.custom_vjp
def _rows_exact(m01, x):
    return _rows_exact_impl(m01, x, NN)


def _rows_exact_fwd(m01, x):
    return _rows_exact_impl(m01, x, NN), m01


def _rows_exact_bwd(m01, g):
    return jnp.zeros_like(m01), _rows_exact_impl(m01, g, TN)


_rows_exact.defvjp(_rows_exact_fwd, _rows_exact_bwd)


def _gla_blocks(Sts, qrs, ks, vs, las):
    H = len(Sts)
    n = len(qrs)
    C, dk = qrs[0].shape
    R = range(n)
    row = lax.broadcasted_iota(jnp.int32, (C, C), 0)
    col = lax.broadcasted_iota(jnp.int32, (C, C), 1)
    ri = lax.broadcasted_iota(jnp.int32, (C, dk), 0)
    q = [qrs[h] * (dk ** -0.5) for h in R]
    running = (row >= col).astype(f32)
    b = [_rows_exact(running, las[h]) for h in R]
    sc = [jnp.where(row == col, jnp.sum(q[h] * ks[h], axis=-1, keepdims=True), 0.0) for h in R]
    s = C // 2
    while s >= 1:
        sh = s.bit_length() - 1
        ref = lax.shift_left(lax.shift_right_logical(row, sh + 1), sh + 1) + (s - 1)
        pick = (col == ref).astype(f32)
        bref = [_rows_exact(pick, b[h]) for h in R]
        upper = (lax.shift_right_logical(ri, sh) & 1) == 1
        qt = [jnp.where(upper, q[h] * jnp.exp(jnp.where(upper, b[h] - bref[h], 0.0)), 0.0) for h in R]
        kt = [jnp.where(upper, 0.0, ks[h] * jnp.exp(jnp.where(upper, 0.0, bref[h] - b[h]))) for h in R]
        same = lax.shift_right_logical(row, sh + 1) == lax.shift_right_logical(col, sh + 1)
        sc = [sc[h] + jnp.where(same, _mmb(qt[h], kt[h], NT), 0.0) for h in R]
        s //= 2
    o = [_mmb(sc[h], vs[h], NN) for h in R]
    qe = [q[h] * jnp.exp(b[h]) for h in R]
    bl = [b[h][C - 1:C, :] for h in R]
    upd = [_mmb(vs[h], ks[h] * jnp.exp(bl[h] - b[h]), TN) for h in R]
    ebl = [jnp.exp(bl[h]) for h in R]
    St = list(Sts)
    for blk in range(n // H):
        for h in range(H):
            i = blk * H + h
            o[i] = o[i] + _mmb(qe[i], St[h], NT)
        St = [St[h] * ebl[blk * H + h] + upd[blk * H + h] for h in range(H)]
    return St, o


_ANY = pl.BlockSpec(memory_space=pl.ANY)


class _Gather:
    def __init__(self, arrays, by_columns=False):
        self.arrays = list(arrays)
        self.n = len(self.arrays)
        self.by_columns = by_columns
        if by_columns:
            assert all(a.ndim == 2 and a.shape[1] % LANES == 0 for a in self.arrays)
            self.out_shape = [jax.ShapeDtypeStruct((a.shape[0], N_CHIP * a.shape[1]), a.dtype) for a in self.arrays]
        else:
            self.out_shape = [jax.ShapeDtypeStruct((N_CHIP,) + a.shape, a.dtype) for a in self.arrays]
        self.sems = [pltpu.SemaphoreType.DMA((self.n, 3)), pltpu.SemaphoreType.DMA((self.n, 3)),
                     pltpu.SemaphoreType.DMA((self.n,))]

    def hooks(self, ins, outs, send, recv, lsem):
        def place(a, chip):
            if not self.by_columns:
                return outs[a].at[chip]
            cols = self.arrays[a].shape[1]
            return outs[a].at[:, pl.ds(pl.multiple_of(chip * cols, LANES), cols)]

        def copies():
            x, y, c = lax.axis_index("x"), lax.axis_index("y"), lax.axis_index("c")
            me = 2 * x + y
            out = []
            for a in range(self.n):
                out.append((pltpu.make_async_copy(ins[a], place(a, me), lsem.at[a]), None))
                for j, (px, py) in enumerate([(1 - x, y), (x, 1 - y), (1 - x, 1 - y)]):
                    mk = lambda dst, a=a, j=j, px=px, py=py: pltpu.make_async_remote_copy(
                        src_ref=ins[a], dst_ref=dst, send_sem=send.at[a, j], recv_sem=recv.at[a, j],
                        device_id=(px, py, c), device_id_type=MESH)
                    out.append((mk(place(a, me)), mk(place(a, 2 * px + py))))
            return out

        return _start_wait(copies)


class _Exchange:
    def __init__(self, slotted, shared=(), by_chip=()):
        self.arrays = list(slotted) + list(by_chip) + list(shared)
        self.ns, self.nc = len(slotted), len(by_chip)
        self.n = len(self.arrays)
        self.out_shape = [jax.ShapeDtypeStruct(a.shape, a.dtype) for a in slotted]
        self.out_shape += [jax.ShapeDtypeStruct((N_DEV,) + a.shape[1:], a.dtype) for a in by_chip]
        self.out_shape += [jax.ShapeDtypeStruct((N_DEV,) + b.shape, b.dtype) for b in shared]
        self.sems = [pltpu.SemaphoreType.DMA((self.n, N_DEV - 1)), pltpu.SemaphoreType.DMA((self.n, N_DEV - 1)),
                     pltpu.SemaphoreType.DMA((self.n,))]

    def hooks(self, ins, outs, send, recv, lsem):
        def copies():
            x, y, c = lax.axis_index("x"), lax.axis_index("y"), lax.axis_index("c")
            me = 4 * x + 2 * y + c

            def src(a, dev):
                tx, ty, tc = dev
                if a < self.ns:
                    return ins[a].at[4 * tx + 2 * ty + tc]
                return ins[a].at[2 * tx + ty] if a < self.ns + self.nc else ins[a]

            out = []
            for a in range(self.n):
                out.append((pltpu.make_async_copy(src(a, (x, y, c)), outs[a].at[me], lsem.at[a]), None))
                for o in range(1, N_DEV):
                    dev = (1 - x if o & 4 else x, 1 - y if o & 2 else y, 1 - c if o & 1 else c)
                    t = 4 * dev[0] + 2 * dev[1] + dev[2]
                    mk = lambda dst, a=a, o=o, dev=dev: pltpu.make_async_remote_copy(
                        src_ref=src(a, dev), dst_ref=dst, send_sem=send.at[a, o - 1], recv_sem=recv.at[a, o - 1],
                        device_id=dev, device_id_type=MESH)
                    out.append((mk(outs[a].at[me]), mk(outs[a].at[t])))
            return out

        return _start_wait(copies)


class _Sides:
    def __init__(self, *members):
        self.members = members
        self.arrays = [a for s in members for a in s.arrays]
        self.n = len(self.arrays)
        self.out_shape = [sh for s in members for sh in s.out_shape]
        self.sems = [sm for s in members for sm in s.sems]

    def hooks(self, ins, outs, *sems):
        hooks, o = [], 0
        for i, s in enumerate(self.members):
            hooks.append(s.hooks(ins[o:o + s.n], outs[o:o + s.n], *sems[3 * i:3 * i + 3]))
            o += s.n

        def start():
            for st, _ in hooks:
                st()

        def wait():
            for _, wt in hooks:
                wt()

        return start, wait


def _start_wait(copies):
    def start():
        for s, _ in copies():
            s.start()

    def wait():
        for s, w in copies():
            (s if w is None else w).wait()

    return start, wait


def _call(body, *, name, grid, in_specs, out_specs, out_shape, args, sem, scratch_shapes=(), aliases=None, side=None):
    in_specs, out_specs, out_shape, args = list(in_specs), list(out_specs), list(out_shape), list(args)
    scratch_shapes = list(scratch_shapes)
    aliases = aliases or {}
    if side is None:
        return pl.pallas_call(
            body, name=name, grid=grid, in_specs=in_specs, out_specs=out_specs, out_shape=out_shape,
            scratch_shapes=scratch_shapes, input_output_aliases=aliases, compiler_params=_params(*sem))(*args)
    n_in, n_out, n_scr, ns = len(in_specs), len(out_specs), len(scratch_shapes), side.n

    def full_body(*refs):
        ins, refs = refs[:n_in], refs[n_in:]
        s_in, refs = refs[:ns], refs[ns:]
        outs, refs = refs[:n_out], refs[n_out:]
        s_out, refs = refs[:ns], refs[ns:]
        scr, sems = refs[:n_scr], refs[n_scr:]
        start, wait = side.hooks(s_in, s_out, *sems)
        ids = [pl.program_id(d) for d in range(len(grid))]
        first = functools.reduce(jnp.logical_and, [i == 0 for i in ids])
        last = functools.reduce(jnp.logical_and, [i == g - 1 for i, g in zip(ids, grid)])
        pl.when(first)(start)
        body(*ins, *outs, *scr)
        pl.when(last)(wait)

    return pl.pallas_call(
        full_body, name=name, grid=grid, in_specs=in_specs + [_ANY] * ns, out_specs=out_specs + [_ANY] * ns,
        out_shape=out_shape + side.out_shape, scratch_shapes=scratch_shapes + side.sems,
        input_output_aliases=aliases, compiler_params=_params(*(["arbitrary"] * len(grid))))(*args, *side.arrays)


WHOLE_K = dict(tm_cap=688, tn_cap=512, tk_cap=1 << 20)
WHOLE_K_T = dict(tm_cap=512, tn_cap=512, tk_cap=1 << 20)

def _mm(a, b, mode, name, *, tm_cap=1408, tn_cap=1024, tk_cap=2048, out_dtype=f32, acc_in=None, side=None,
        col_slabs=False, b_cols=None, plus=None):
    if mode == "nn":
        (M, K), (K2, N) = a.shape, b.shape
    elif mode == "nt":
        (M, K), (N, K2) = a.shape, b.shape
    else:
        (K, M), (K2, N) = a.shape, b.shape
    assert K == K2, (name, a.shape, b.shape)
    b_first = 0
    if b_cols is not None:
        assert mode != "nt"
        b_first, N = b_cols
    tm = _tile(M // 2 if col_slabs else M, tm_cap)
    tn = _tile(N // N_CHIP if col_slabs else N, tn_cap, 128)
    tk = _tile(K, tk_cap, 128 if K % 128 == 0 else 16)
    nk = K // tk
    dims = {"nn": NN, "nt": NT, "tn": TN}[mode]
    use_scratch = nk > 1 and out_dtype != f32

    def body(*refs):
        a_ref, b_ref, *rest = refs
        c_ref = rest.pop(0) if acc_in is not None else None
        p = _mmb(a_ref[...], b_ref[...], dims)
        if plus is not None:
            p = p + _mmb(rest.pop(0)[...], rest.pop(0)[...], dims)
        o_ref, *scr = rest
        if nk == 1:
            if c_ref is not None:
                p = p + c_ref[...]
            o_ref[...] = p.astype(out_dtype)
            return
        k = pl.program_id(2)
        acc = scr[0] if use_scratch else o_ref

        @pl.when(k == 0)
        def _():
            acc[...] = p if c_ref is None else p + c_ref[...]

        @pl.when(k > 0)
        def _():
            acc[...] += p

        if use_scratch:
            @pl.when(k == nk - 1)
            def _():
                o_ref[...] = acc[...].astype(out_dtype)

    if mode == "tn":
        a_spec = pl.BlockSpec((tk, tm), lambda i, j, k: (k, i))
    else:
        a_spec = pl.BlockSpec((tm, tk), lambda i, j, k: (i, k))
    if mode == "nt":
        b_spec = pl.BlockSpec((tn, tk), lambda i, j, k: (j, k))
    else:
        assert b_first % tn == 0, (name, b_first, tn)
        b_spec = pl.BlockSpec((tk, tn), lambda i, j, k: (k, j + b_first // tn))
    if col_slabs:
        assert acc_in is None
        ni, nj = M // 2 // tm, N // N_CHIP // tn
        o_spec = pl.BlockSpec((None, tm, tn), lambda i, j, k: (2 * (j // nj) + i // ni, i % ni, j % nj))
        o_shape = jax.ShapeDtypeStruct((N_DEV, M // 2, N // N_CHIP), out_dtype)
    else:
        o_spec = pl.BlockSpec((tm, tn), lambda i, j, k: (i, j))
        o_shape = jax.ShapeDtypeStruct((M, N), out_dtype)
    in_specs = [a_spec, b_spec]
    args = [a, b]
    if acc_in is not None:
        in_specs.append(o_spec)
        args.append(acc_in)
    if plus is not None:
        assert mode == "nt" and nk == 1, (name, mode, nk)
        k2 = plus[0].shape[1]
        assert plus[0].shape == (M, k2) and plus[1].shape == (N, k2), (name, plus[0].shape, plus[1].shape)
        in_specs += [pl.BlockSpec((tm, k2), lambda i, j, k: (i, 0)), pl.BlockSpec((tn, k2), lambda i, j, k: (j, 0))]
        args += list(plus)
    out = _call(body, name=name, grid=(M // tm, N // tn, nk), in_specs=in_specs, out_specs=[o_spec],
                out_shape=[o_shape], args=args,
                scratch_shapes=[pltpu.VMEM((tm, tn), f32)] if use_scratch else [],
                sem=("parallel", "parallel", "arbitrary"), side=side)
    return out[0] if side is None else (out[0], out[1:])


def _embed_norm(x3, m64, w, to_bf16=(), side=None):
    _, S, D = x3.shape
    Lp = OFF + S
    steps = Lp // ROWS
    nc = len(to_bf16)

    def body(x_ref, m_ref, w_ref, *rest):
        c_in, (h_ref, n_ref), c_out = rest[:nc], rest[nc:nc + 2], rest[nc + 2:]
        i = pl.program_id(0)
        h = jnp.where(i == 0, m_ref[...], x_ref[...])
        h_ref[...] = h
        xh, _ = _rms(h)
        n_ref[...] = (xh * w_ref[...]).astype(bf16)
        for ci, co in zip(c_in, c_out):
            co[...] = ci[...].astype(bf16)

    def piece(a):
        R, C = a.shape
        rb = next(d for d in range(16, R + 1, 16) if R % d == 0 and d * steps >= R)
        return pl.BlockSpec((rb, C), lambda i: (jnp.minimum(i, R // rb - 1), 0))

    row = pl.BlockSpec((ROWS, D), lambda i: (i, 0))
    pieces = [piece(a) for a in to_bf16]
    out = _call(
        body, name="embed_norm", grid=(steps,),
        in_specs=[pl.BlockSpec((None, ROWS, D), lambda i: (0, jnp.maximum(i - 1, 0), 0)),
                  pl.BlockSpec((ROWS, D), lambda i: (0, 0)),
                  pl.BlockSpec((1, D), lambda i: (0, 0))] + pieces,
        out_specs=[row, row] + pieces,
        out_shape=[jax.ShapeDtypeStruct((Lp, D), f32), jax.ShapeDtypeStruct((Lp, D), bf16)]
        + [jax.ShapeDtypeStruct(a.shape, bf16) for a in to_bf16],
        args=[x3, m64, w, *to_bf16], sem=("arbitrary",), side=side)
    return out[0], out[1], out[2:2 + nc], out[2 + nc:]


def _add_norm(h, d, w):
    Lp, D = h.shape
    tr = _tile(Lp, 256)

    def body(h_ref, d_ref, w_ref, o_ref, n_ref):
        h1 = h_ref[...] + d_ref[...]
        o_ref[...] = h1
        xh, _ = _rms(h1)
        n_ref[...] = (xh * w_ref[...]).astype(bf16)

    row = pl.BlockSpec((tr, D), lambda i: (i, 0))
    return pl.pallas_call(
        body, name="add_norm", grid=(Lp // tr,),
        in_specs=[row, row, pl.BlockSpec((1, D), lambda i: (0, 0))], out_specs=[row, row],
        out_shape=[jax.ShapeDtypeStruct((Lp, D), f32), jax.ShapeDtypeStruct((Lp, D), bf16)],
        compiler_params=_params("parallel"),
    )(h, d, w)


def _norm_bwd(dn, h, dh, w):
    Lp, D = h.shape
    tr = _tile(Lp, 256)

    def body(dn_ref, h_ref, dh_ref, w_ref, o_ref, ob_ref, gw_ref):
        i = pl.program_id(0)
        xh, r = _rms(h_ref[...])
        dn_ = dn_ref[...]
        o = dh_ref[...] + _rms_bwd(dn_, xh, r, w_ref[...])
        o_ref[...] = o
        ob_ref[...] = o.astype(bf16)
        gw = jnp.sum(dn_ * xh, axis=0, keepdims=True)

        @pl.when(i == 0)
        def _():
            gw_ref[...] = gw

        @pl.when(i > 0)
        def _():
            gw_ref[...] += gw

    row = pl.BlockSpec((tr, D), lambda i: (i, 0))
    vec = pl.BlockSpec((1, D), lambda i: (0, 0))
    return pl.pallas_call(
        body, name="norm_bwd", grid=(Lp // tr,), in_specs=[row, row, row, vec], out_specs=[row, row, vec],
        out_shape=[jax.ShapeDtypeStruct((Lp, D), f32), jax.ShapeDtypeStruct((Lp, D), bf16),
                   jax.ShapeDtypeStruct((1, D), f32)],
        compiler_params=_params("arbitrary"),
    )(dn, h, dh, w)


def _embed_norm_bwd(dn, h, dh, w, S, side=None):
    Lp, D = h.shape
    tr = _tile(S, 256, OFF)

    def body(dn_ref, h_ref, dh_ref, dn0_ref, h0_ref, dh0_ref, w_ref, gx_ref, gm_ref, gw_ref):
        i = pl.program_id(0)

        def rows(dn_, h_, dh_):
            xh, r = _rms(h_)
            return dh_ + _rms_bwd(dn_, xh, r, w_ref[...]), jnp.sum(dn_ * xh, axis=0, keepdims=True)

        d, gw = rows(dn_ref[...], h_ref[...], dh_ref[...])
        gx_ref[...] = d

        @pl.when(i == 0)
        def _():
            d0, gw0 = rows(dn0_ref[...], h0_ref[...], dh0_ref[...])
            gm_ref[...] = d0[PAD:OFF, :]
            gw_ref[...] = gw0 + gw

        @pl.when(i > 0)
        def _():
            gw_ref[...] += gw

    win = pl.BlockSpec((pl.Element(tr), pl.Element(D)), lambda i: (pl.multiple_of(OFF + i * tr, OFF), 0))
    head = pl.BlockSpec((OFF, D), lambda i: (0, 0))
    vec = pl.BlockSpec((1, D), lambda i: (0, 0))
    n = S // tr
    split = side is not None and n > 1
    out = _call(
        body, name="embed_norm_bwd", grid=(n - 1 if split else n,), in_specs=[win, win, win, head, head, head, vec],
        out_specs=[pl.BlockSpec((None, tr, D), lambda i: (0, i, 0)),
                   pl.BlockSpec((N_META, D), lambda i: (0, 0)), vec],
        out_shape=[jax.ShapeDtypeStruct((1, S, D), f32), jax.ShapeDtypeStruct((N_META, D), f32),
                   jax.ShapeDtypeStruct((1, D), f32)],
        args=[dn, h, dh, dn, h, dh, w], sem=("arbitrary",), side=side)
    gx, gm, gw = out[:3]
    if split:
        def last(dn_ref, h_ref, dh_ref, w_ref, gw_ref, gx_in, gx_ref, gwo_ref):
            xh, r = _rms(h_ref[...])
            dn_ = dn_ref[...]
            gx_ref[...] = dh_ref[...] + _rms_bwd(dn_, xh, r, w_ref[...])
            gwo_ref[...] = gw_ref[...] + jnp.sum(dn_ * xh, axis=0, keepdims=True)

        tail = pl.BlockSpec((pl.Element(tr), pl.Element(D)), lambda i: (OFF + (n - 1) * tr, 0))
        gx, gw = pl.pallas_call(
            last, name="embed_norm_bwd_last", grid=(1,), in_specs=[tail, tail, tail, vec, vec, _ANY],
            out_specs=[pl.BlockSpec((None, tr, D), lambda i: (0, n - 1, 0)), vec],
            out_shape=[jax.ShapeDtypeStruct((1, S, D), f32), jax.ShapeDtypeStruct((1, D), f32)],
            input_output_aliases={5: 0}, compiler_params=_params("arbitrary"),
        )(dn, h, dh, w, gw, gx)
    return gx, gm, gw, out[3:]


def _final(h1, ffn, tgt3, w):
    Lp, D = h1.shape
    _, S, _ = tgt3.shape
    tr = _tile(Lp, min(256, S), OFF)

    def body(h_ref, f_ref, t_ref, w_ref, d_ref, db_ref, l_ref, gw_ref):
        i = pl.program_id(0)
        h2 = h_ref[...] + f_ref[...]
        xh, r = _rms(h2)
        w_ = w_ref[...]
        t = t_ref[...]
        t = jnp.where(i == 0, pltpu.roll(t, OFF, 0), t)
        valid = (lax.broadcasted_iota(jnp.int32, (tr, 1), 0) + i * tr >= OFF).astype(f32)
        e = xh * w_ - t
        loss = 0.5 * jnp.sum(jnp.mean(e * e, axis=-1, keepdims=True) * valid, axis=0, keepdims=True)
        dy = e * (valid / D)
        d = _rms_bwd(dy, xh, r, w_)
        d_ref[...] = d
        db_ref[...] = d.astype(bf16)
        gw = jnp.sum(dy * xh, axis=0, keepdims=True)

        @pl.when(i == 0)
        def _():
            l_ref[...] = jnp.zeros_like(l_ref)
            gw_ref[...] = jnp.zeros_like(gw_ref)

        l_ref[...] += jnp.broadcast_to(loss, l_ref.shape)
        gw_ref[...] += gw

    row = pl.BlockSpec((tr, D), lambda i: (i, 0))
    vec = pl.BlockSpec((1, D), lambda i: (0, 0))
    tgt = pl.BlockSpec((pl.Element(tr), pl.Element(D)),
                       lambda i: (pl.multiple_of(jnp.maximum(i * tr - OFF, 0), OFF), 0))
    return pl.pallas_call(
        body, name="final_loss", grid=(Lp // tr,), in_specs=[row, row, tgt, vec],
        out_specs=[row, row, pl.BlockSpec((8, 128), lambda i: (0, 0)), vec],
        out_shape=[jax.ShapeDtypeStruct((Lp, D), f32), jax.ShapeDtypeStruct((Lp, D), bf16),
                   jax.ShapeDtypeStruct((8, 128), f32), jax.ShapeDtypeStruct((1, D), f32)],
        compiler_params=_params("arbitrary"),
    )(h1, ffn, tgt3.reshape(S, D), w)


def _ffn_in(n, w_gate, w_up, side=None):
    M, K = n.shape
    F = w_gate.shape[1]
    tm = _tile(M, 1408)
    tn = _tile(F, 512, 128)

    def body(a_ref, bg_ref, bu_ref, act_ref, pg_ref, pu_ref):
        a = a_ref[...]
        g = _mmb(a, bg_ref[...], NN)
        u = _mmb(a, bu_ref[...], NN)
        s = _sigmoid(g)
        gs = g * s
        act_ref[...] = (gs * u).astype(bf16)
        pg_ref[...] = (u * (s + gs * (1.0 - s))).astype(bf16)
        pu_ref[...] = gs.astype(bf16)

    wsp = pl.BlockSpec((K, tn), lambda i, j: (0, j))
    osp = pl.BlockSpec((tm, tn), lambda i, j: (i, j))
    out = _call(body, name="ffn_in", grid=(M // tm, F // tn),
                in_specs=[pl.BlockSpec((tm, K), lambda i, j: (i, 0)), wsp, wsp], out_specs=[osp] * 3,
                out_shape=[jax.ShapeDtypeStruct((M, F), bf16)] * 3, args=[n, w_gate, w_up],
                sem=("parallel", "parallel"), side=side)
    return out[0], out[1], out[2], out[3:]


def _ffn_dact(d, w_down, pg, pu):
    M, K = d.shape
    F = w_down.shape[0]
    tm = _tile(M, 1408)
    tn = _tile(F, 512, 128)

    def body(d_ref, w_ref, pg_ref, pu_ref, dg_ref, du_ref):
        da = _mmb(d_ref[...], w_ref[...], NT)
        dg_ref[...] = (da * pg_ref[...].astype(f32)).astype(bf16)
        du_ref[...] = (da * pu_ref[...].astype(f32)).astype(bf16)

    osp = pl.BlockSpec((tm, tn), lambda i, j: (i, j))
    return pl.pallas_call(
        body, name="ffn_dact", grid=(M // tm, F // tn),
        in_specs=[pl.BlockSpec((tm, K), lambda i, j: (i, 0)), pl.BlockSpec((tn, K), lambda i, j: (j, 0)), osp, osp],
        out_specs=[osp, osp], out_shape=[jax.ShapeDtypeStruct((M, F), bf16)] * 2,
        compiler_params=_params("parallel", "parallel"),
    )(d, w_down, pg, pu)


def _ffn_dn(dg, du, w_gate, w_up):
    M, F = dg.shape
    D = w_gate.shape[0]
    tm = _tile(M, 688)
    tn = _tile(D, 256, 128)

    def body(dg_ref, du_ref, wg_ref, wu_ref, o_ref):
        o_ref[...] = _mmb(dg_ref[...], wg_ref[...], NT) + _mmb(du_ref[...], wu_ref[...], NT)

    asp = pl.BlockSpec((tm, F), lambda i, j: (i, 0))
    wsp = pl.BlockSpec((tn, F), lambda i, j: (j, 0))
    return pl.pallas_call(
        body, name="d_n2", grid=(M // tm, D // tn), in_specs=[asp, asp, wsp, wsp],
        out_specs=pl.BlockSpec((tm, tn), lambda i, j: (i, j)), out_shape=jax.ShapeDtypeStruct((M, D), f32),
        compiler_params=_params("parallel", "parallel"),
    )(dg, du, w_gate, w_up)


def _gates(psm, w2p, gate_b, alog, dtb):
    Lp = psm.shape[0]
    tr = _tile(Lp, 256)

    def body(p_ref, w_ref, b_ref, a_ref, t_ref, gb_ref, la_ref):
        i = pl.program_id(0)
        psm_ = p_ref[...]
        lane = lax.broadcasted_iota(jnp.int32, psm_.shape, 1)
        rowi = lax.broadcasted_iota(jnp.int32, (tr, 1), 0) + i * tr
        g = -jnp.exp(a_ref[...]) * _softplus(psm_ + t_ref[...])
        beta = _sigmoid(psm_)
        gb = jnp.where(lane < GDN_H, g, jnp.where(lane < 2 * GDN_H, beta, 0.0))
        gb_ref[...] = gb * (rowi >= PAD).astype(f32)
        logit = _mmb(psm_, w_ref[...], NN) + b_ref[...]
        la_ref[...] = _log_sigmoid(logit) * (1.0 / GATE_NORMALIZER)

    row = pl.BlockSpec((tr, SM_W), lambda i: (i, 0))
    return pl.pallas_call(
        body, name="gates", grid=(Lp // tr,),
        in_specs=[row, pl.BlockSpec((SM_W, GLA_QK), lambda i: (0, 0)), pl.BlockSpec((1, GLA_QK), lambda i: (0, 0)),
                  pl.BlockSpec((1, SM_W), lambda i: (0, 0)), pl.BlockSpec((1, SM_W), lambda i: (0, 0))],
        out_specs=[row, pl.BlockSpec((tr, GLA_QK), lambda i: (i, 0))],
        out_shape=[jax.ShapeDtypeStruct((Lp, SM_W), f32), jax.ShapeDtypeStruct((Lp, GLA_QK), f32)],
        compiler_params=_params("parallel"),
    )(psm, w2p, gate_b, alog, dtb)


def _gates_bwd(psm, w2p, gate_b, alog, dtb, dgb, dla):
    Lp = psm.shape[0]
    tr = _tile(Lp, 256)

    def body(p_ref, w_ref, b_ref, a_ref, t_ref, dgb_ref, dla_ref, dp_ref, gw_ref, gb_ref, ga_ref, gt_ref):
        i = pl.program_id(0)
        psm_ = p_ref[...]
        lane = lax.broadcasted_iota(jnp.int32, psm_.shape, 1)
        rowi = lax.broadcasted_iota(jnp.int32, (tr, 1), 0) + i * tr
        d = dgb_ref[...] * (rowi >= PAD).astype(f32)
        ea = jnp.exp(a_ref[...])
        z = psm_ + t_ref[...]
        is_g = lane < GDN_H
        dz = jnp.where(is_g, -ea * _sigmoid(z) * d, 0.0)
        dalog = jnp.where(is_g, -ea * _softplus(z) * d, 0.0)
        beta = _sigmoid(psm_)
        dbeta = jnp.where(jnp.logical_and(lane >= GDN_H, lane < 2 * GDN_H), beta * (1.0 - beta) * d, 0.0)
        logit = _mmb(psm_, w_ref[...], NN) + b_ref[...]
        dlogit = dla_ref[...] * (_sigmoid(-logit) * (1.0 / GATE_NORMALIZER))
        dlr = _mmb(dlogit, w_ref[...], NT)
        dp_ref[...] = (dz + dbeta + dlr).astype(bf16)
        gw = _mmb(psm_, dlogit, TN)
        gb = jnp.sum(dlogit, axis=0, keepdims=True)
        ga = jnp.sum(dalog, axis=0, keepdims=True)
        gt = jnp.sum(dz, axis=0, keepdims=True)

        @pl.when(i == 0)
        def _():
            gw_ref[...] = gw
            gb_ref[...] = gb
            ga_ref[...] = ga
            gt_ref[...] = gt

        @pl.when(i > 0)
        def _():
            gw_ref[...] += gw
            gb_ref[...] += gb
            ga_ref[...] += ga
            gt_ref[...] += gt

    row = pl.BlockSpec((tr, SM_W), lambda i: (i, 0))
    wsp = pl.BlockSpec((SM_W, GLA_QK), lambda i: (0, 0))
    bsp = pl.BlockSpec((1, GLA_QK), lambda i: (0, 0))
    vsp = pl.BlockSpec((1, SM_W), lambda i: (0, 0))
    return pl.pallas_call(
        body, name="gates_bwd", grid=(Lp // tr,),
        in_specs=[row, wsp, bsp, vsp, vsp, row, pl.BlockSpec((tr, GLA_QK), lambda i: (i, 0))],
        out_specs=[row, wsp, bsp, vsp, vsp],
        out_shape=[jax.ShapeDtypeStruct((Lp, SM_W), bf16), jax.ShapeDtypeStruct((SM_W, GLA_QK), f32),
                   jax.ShapeDtypeStruct((1, GLA_QK), f32), jax.ShapeDtypeStruct((1, SM_W), f32),
                   jax.ShapeDtypeStruct((1, SM_W), f32)],
        compiler_params=_params("arbitrary"),
    )(psm, w2p, gate_b, alog, dtb, dgb, dla)


def _conv_pre(x_ext, w, n):
    rows = x_ext.shape[0]
    y = x_ext * w[CONV_K - 1:CONV_K, :]
    for s in range(1, CONV_K):
        y = y + pltpu.roll(x_ext, s, 0) * w[CONV_K - 1 - s:CONV_K - s, :]
    return y[rows - n:, :]


def _conv(proj, cw, side=None):
    Lp = proj.shape[0]
    W = cw.shape[1]
    tr = _tile(Lp, 256, 64)
    tc = _tile(W, 1024, 128)
    c0 = C_QKV // tc

    def body(h_ref, x_ref, w_ref, o_ref):
        i = pl.program_id(1)
        halo = jnp.where(i == 0, 0.0, h_ref[...])
        x_ext = jnp.concatenate([halo, x_ref[...]], axis=0)
        o_ref[...] = _silu(_conv_pre(x_ext, w_ref[...], tr))

    out = _call(
        body, name="conv", grid=(W // tc, Lp // tr),
        in_specs=[pl.BlockSpec((8, tc), lambda j, i: (jnp.maximum(i * (tr // 8) - 1, 0), j + c0)),
                  pl.BlockSpec((tr, tc), lambda j, i: (i, j + c0)),
                  pl.BlockSpec((CONV_K, tc), lambda j, i: (0, j))],
        out_specs=[pl.BlockSpec((tr, tc), lambda j, i: (i, j))],
        out_shape=[jax.ShapeDtypeStruct((Lp, W), f32)], args=[proj, proj, cw],
        sem=("parallel", "parallel"), side=side)
    return out[0] if side is None else (out[0], out[1:])


def _conv_bwd(proj, cw, dy, dproj, side=None):
    Lp = proj.shape[0]
    W = cw.shape[1]
    tr = _tile(Lp, 256, 64)
    tc = _tile(W, 1024, 128)
    c0 = C_QKV // tc
    nr = Lp // tr
    last8 = Lp // 8 - 1

    def body(xp_ref, x_ref, xn_ref, w_ref, d_ref, dn_ref, dproj_ref, o_ref, gw_ref):
        del dproj_ref
        i = pl.program_id(1)
        w = w_ref[...]
        xp = jnp.where(i == 0, 0.0, xp_ref[...])
        x_ext = jnp.concatenate([xp, x_ref[...], xn_ref[...]], axis=0)
        n = tr + 8
        pre = _conv_pre(x_ext, w, n)
        dn = jnp.where(i == nr - 1, 0.0, dn_ref[...])
        dpre = jnp.concatenate([d_ref[...], dn], axis=0) * _dsilu(pre)
        dx = dpre * w[CONV_K - 1:CONV_K, :]
        for s in range(1, CONV_K):
            dx = dx + pltpu.roll(dpre, n - s, 0) * w[CONV_K - 1 - s:CONV_K - s, :]
        o_ref[...] = dx[:tr, :].astype(bf16)
        dp = dpre[:tr, :]
        rows = []
        for k in range(CONV_K):
            xs = x_ext if k == CONV_K - 1 else pltpu.roll(x_ext, CONV_K - 1 - k, 0)
            rows.append(jnp.sum(dp * xs[8:8 + tr, :], axis=0, keepdims=True))
        gw = jnp.concatenate(rows, axis=0)

        @pl.when(i == 0)
        def _():
            gw_ref[...] = gw

        @pl.when(i > 0)
        def _():
            gw_ref[...] += gw

    cur = pl.BlockSpec((tr, tc), lambda j, i: (i, j))
    nxt = pl.BlockSpec((8, tc), lambda j, i: (jnp.minimum((i + 1) * (tr // 8), last8), j))
    pcur = pl.BlockSpec((tr, tc), lambda j, i: (i, j + c0))
    pprev = pl.BlockSpec((8, tc), lambda j, i: (jnp.maximum(i * (tr // 8) - 1, 0), j + c0))
    pnext = pl.BlockSpec((8, tc), lambda j, i: (jnp.minimum((i + 1) * (tr // 8), last8), j + c0))
    wsp = pl.BlockSpec((CONV_K, tc), lambda j, i: (0, j))
    out = _call(
        body, name="conv_bwd", grid=(W // tc, nr),
        in_specs=[pprev, pcur, pnext, wsp, cur, nxt, _ANY], out_specs=[pcur, wsp],
        out_shape=[jax.ShapeDtypeStruct(dproj.shape, dproj.dtype), jax.ShapeDtypeStruct((CONV_K, W), f32)],
        aliases={6: 0}, args=[proj, proj, proj, cw, dy, dy, dproj], sem=("parallel", "arbitrary"), side=side)
    return out[0], out[1], out[2:]


GDN_FWD_GROUP = 3


def _gdn_group(Lp, most):
    n = Lp // GDN_C
    return next(g for g in range(most, 0, -1) if n % g == 0)


def _gdn_heads(x_ref, gb_ref, group):
    qs, ks, vs, bs, gs = [], [], [], [], []
    for chunk in range(group):
        r = slice(chunk * GDN_C, (chunk + 1) * GDN_C)
        gbv = gb_ref[r, :]
        for h in range(GDN_H):
            qs.append(x_ref[r, Q0 + h * GDN_DK:Q0 + (h + 1) * GDN_DK])
            ks.append(x_ref[r, K0 + h * GDN_DK:K0 + (h + 1) * GDN_DK])
            vs.append(x_ref[r, V0 + h * GDN_DV:V0 + (h + 1) * GDN_DV])
            bs.append(gbv[:, GDN_H + h:GDN_H + h + 1])
            gs.append(gbv[:, h:h + 1])
    return qs, ks, vs, bs, gs


def _gdn_fwd(qkvc, gb, side=None):
    Lp = qkvc.shape[0]
    group = _gdn_group(Lp, GDN_FWD_GROUP)
    rows = group * GDN_C
    steps = Lp // rows
    R = range(GDN_H)

    def body(x_ref, gb_ref, o_ref, sall_ref, pall_ref, s_scr):
        @pl.when(pl.program_id(0) == 0)
        def _():
            s_scr[...] = jnp.zeros_like(s_scr)

        S2, o, p, entering = _gdn_chunk([s_scr[h] for h in R], *_gdn_heads(x_ref, gb_ref, group))
        for h in R:
            s_scr[h] = S2[h]
        for chunk in range(group):
            for h in R:
                i = chunk * GDN_H + h
                o_ref[chunk * GDN_C:(chunk + 1) * GDN_C, h * GDN_DV:(h + 1) * GDN_DV] = o[i]
                pall_ref[chunk, h] = p[i]
                sall_ref[chunk, h] = entering[i]

    out = _call(
        body, name="gdn_fwd", grid=(steps,),
        in_specs=[pl.BlockSpec((rows, QKV_W), lambda n: (n, 0)), pl.BlockSpec((rows, SM_W), lambda n: (n, 0))],
        out_specs=[pl.BlockSpec((rows, GDN_V), lambda n: (n, 0)),
                   pl.BlockSpec((group, GDN_H, GDN_DK, GDN_DV), lambda n: (n, 0, 0, 0)),
                   pl.BlockSpec((group, GDN_H, GDN_C, GDN_C), lambda n: (n, 0, 0, 0))],
        out_shape=[jax.ShapeDtypeStruct((Lp, GDN_V), f32),
                   jax.ShapeDtypeStruct((Lp // GDN_C, GDN_H, GDN_DK, GDN_DV), f32),
                   jax.ShapeDtypeStruct((Lp // GDN_C, GDN_H, GDN_C, GDN_C), f32)],
        scratch_shapes=[pltpu.VMEM((GDN_H, GDN_DK, GDN_DV), f32)], args=[qkvc, gb], sem=("arbitrary",), side=side)
    return out[0], out[1], out[2], out[3:]


def _gdn_bwd(qkvc, gb, sall, pall, do, side=None):
    Lp = qkvc.shape[0]
    group = 1
    rows = group * GDN_C
    steps = Lp // rows
    R = range(GDN_H)

    def body(x_ref, gb_ref, sall_ref, pall_ref, do_ref, dx_ref, dgb_ref, ds_scr):
        @pl.when(pl.program_id(0) == 0)
        def _():
            ds_scr[...] = jnp.zeros_like(ds_scr)

        lane = lax.broadcasted_iota(jnp.int32, (GDN_C, SM_W), 1)
        ps = [pall_ref[chunk, h] for chunk in range(group) for h in R]
        _, vjp = jax.vjp(lambda *a: _gdn_chunk(*a, Ps=ps)[:2],
                         [sall_ref[0, h] for h in R], *_gdn_heads(x_ref, gb_ref, group))
        do = [do_ref[chunk * GDN_C:(chunk + 1) * GDN_C, h * GDN_DV:(h + 1) * GDN_DV]
              for chunk in range(group) for h in R]
        dS, dq, dk, dv, dbeta, dg = vjp(([ds_scr[h] for h in R], do))
        for h in R:
            ds_scr[h] = dS[h]
        for chunk in range(group):
            r = slice(chunk * GDN_C, (chunk + 1) * GDN_C)
            acc = jnp.zeros((GDN_C, SM_W), f32)
            for h in R:
                i = chunk * GDN_H + h
                dx_ref[r, Q0 + h * GDN_DK:Q0 + (h + 1) * GDN_DK] = dq[i]
                dx_ref[r, K0 + h * GDN_DK:K0 + (h + 1) * GDN_DK] = dk[i]
                dx_ref[r, V0 + h * GDN_DV:V0 + (h + 1) * GDN_DV] = dv[i]
                acc = acc + jnp.where(lane == h, dg[i], 0.0) + jnp.where(lane == GDN_H + h, dbeta[i], 0.0)
            dgb_ref[r, :] = acc

    rev = lambda n: (steps - 1 - n, 0)
    out = _call(
        body, name="gdn_bwd", grid=(steps,),
        in_specs=[pl.BlockSpec((rows, QKV_W), rev), pl.BlockSpec((rows, SM_W), rev),
                  pl.BlockSpec((1, GDN_H, GDN_DK, GDN_DV), lambda n: (steps - 1 - n, 0, 0, 0)),
                  pl.BlockSpec((group, GDN_H, GDN_C, GDN_C), lambda n: (steps - 1 - n, 0, 0, 0)),
                  pl.BlockSpec((rows, GDN_V), rev)],
        out_specs=[pl.BlockSpec((rows, QKV_W), rev), pl.BlockSpec((rows, SM_W), rev)],
        out_shape=[jax.ShapeDtypeStruct((Lp, QKV_W), f32), jax.ShapeDtypeStruct((Lp, SM_W), f32)],
        scratch_shapes=[pltpu.VMEM((GDN_H, GDN_DK, GDN_DV), f32)], args=[qkvc, gb, sall, pall, do],
        sem=("arbitrary",), side=side)
    return out[0], out[1], out[2:]


GLA_BLOCK = 64


def _gla_group(Lp):
    nb = Lp // GLA_BLOCK
    return next(g for g in (3, 2, 1) if nb % g == 0)


def _gla_slices(h):
    sq = slice(h * GLA_DK, (h + 1) * GLA_DK)
    sk = slice(GLA_QK + h * GLA_DK, GLA_QK + (h + 1) * GLA_DK)
    sv = slice(2 * GLA_QK + h * GLA_DV, 2 * GLA_QK + (h + 1) * GLA_DV)
    return sq, sk, sv


def _gla_heads(x_ref, la_ref, group):
    qs, ks, vs, ls = [], [], [], []
    for blk in range(group):
        r = slice(blk * GLA_BLOCK, (blk + 1) * GLA_BLOCK)
        for h in range(GLA_H):
            sq, sk, sv = _gla_slices(h)
            qs.append(x_ref[r, sq])
            ks.append(x_ref[r, sk])
            vs.append(x_ref[r, sv])
            ls.append(la_ref[r, sq])
    return qs, ks, vs, ls


def _gla_fwd(proj, la):
    Lp = proj.shape[0]
    group = _gla_group(Lp)
    rows = group * GLA_BLOCK
    steps = Lp // rows
    R = range(GLA_H)

    def body(x_ref, la_ref, o_ref, sall_ref, s_scr):
        @pl.when(pl.program_id(0) == 0)
        def _():
            s_scr[...] = jnp.zeros_like(s_scr)

        Sts = [s_scr[h] for h in R]
        for h in R:
            sall_ref[0, h] = Sts[h]
        St2, o = _gla_blocks(Sts, *_gla_heads(x_ref, la_ref, group))
        for h in R:
            s_scr[h] = St2[h]
        for blk in range(group):
            for h in R:
                o_ref[blk * GLA_BLOCK:(blk + 1) * GLA_BLOCK, h * GLA_DV:(h + 1) * GLA_DV] = o[blk * GLA_H + h]

    return pl.pallas_call(
        body, name="gla_fwd", grid=(steps,),
        in_specs=[pl.BlockSpec((rows, G_W), lambda n: (n, C_G // G_W)),
                  pl.BlockSpec((rows, GLA_QK), lambda n: (n, 0))],
        out_specs=[pl.BlockSpec((rows, GLA_V), lambda n: (n, 0)),
                   pl.BlockSpec((1, GLA_H, GLA_DV, GLA_DK), lambda n: (n, 0, 0, 0))],
        out_shape=[jax.ShapeDtypeStruct((Lp, GLA_V), f32),
                   jax.ShapeDtypeStruct((steps, GLA_H, GLA_DV, GLA_DK), f32)],
        scratch_shapes=[pltpu.VMEM((GLA_H, GLA_DV, GLA_DK), f32)],
        compiler_params=_params("arbitrary"),
    )(proj, la)


def _gla_bwd(proj, la, sall, do, dproj, side=None):
    Lp = proj.shape[0]
    group = _gla_group(Lp)
    rows = group * GLA_BLOCK
    steps = Lp // rows
    R = range(GLA_H)

    def body(x_ref, la_ref, sall_ref, do_ref, dproj_ref, dx_ref, dla_ref, ds_scr):
        del dproj_ref

        @pl.when(pl.program_id(0) == 0)
        def _():
            ds_scr[...] = jnp.zeros_like(ds_scr)

        _, vjp = jax.vjp(_gla_blocks, [sall_ref[0, h] for h in R], *_gla_heads(x_ref, la_ref, group))
        do = [do_ref[blk * GLA_BLOCK:(blk + 1) * GLA_BLOCK, h * GLA_DV:(h + 1) * GLA_DV]
              for blk in range(group) for h in R]
        dS, dq, dk, dv, dl = vjp(([ds_scr[h] for h in R], do))
        for h in R:
            ds_scr[h] = dS[h]
        for blk in range(group):
            r = slice(blk * GLA_BLOCK, (blk + 1) * GLA_BLOCK)
            for h in R:
                sq, sk, sv = _gla_slices(h)
                i = blk * GLA_H + h
                dx_ref[r, sq] = dq[i].astype(bf16)
                dx_ref[r, sk] = dk[i].astype(bf16)
                dx_ref[r, sv] = dv[i].astype(bf16)
                dla_ref[r, sq] = dl[i]

    x_spec = pl.BlockSpec((rows, G_W), lambda n: (steps - 1 - n, C_G // G_W))
    rev = lambda n: (steps - 1 - n, 0)
    out = _call(
        body, name="gla_bwd", grid=(steps,),
        in_specs=[x_spec, pl.BlockSpec((rows, GLA_QK), rev),
                  pl.BlockSpec((1, GLA_H, GLA_DV, GLA_DK), lambda n: (steps - 1 - n, 0, 0, 0)),
                  pl.BlockSpec((rows, GLA_V), rev), _ANY],
        out_specs=[x_spec, pl.BlockSpec((rows, GLA_QK), rev)],
        out_shape=[jax.ShapeDtypeStruct(dproj.shape, dproj.dtype), jax.ShapeDtypeStruct((Lp, GLA_QK), f32)],
        aliases={4: 0}, scratch_shapes=[pltpu.VMEM((GLA_H, GLA_DV, GLA_DK), f32)],
        args=[proj, la, sall, do, dproj], sem=("arbitrary",), side=side)
    return out[0], out[1], out[2:]


def _gated_norm_fn(og, ol, zr, wg, wl):
    outs = []
    for h in range(GDN_H):
        s = slice(h * GDN_DV, (h + 1) * GDN_DV)
        outs.append(_rms(og[:, s])[0] * wg * _silu(zr[:, s]))
    for h in range(GLA_H):
        s = slice(h * GLA_DV, (h + 1) * GLA_DV)
        sr = slice(GDN_V + h * GLA_DV, GDN_V + (h + 1) * GLA_DV)
        outs.append(_rms(ol[:, s])[0] * wl * _silu(zr[:, sr]))
    return jnp.concatenate(outs, axis=-1)


def _gated_norm(og, ol, proj, wg, wl):
    Lp = og.shape[0]
    tr = _tile(Lp, 256)

    def body(og_ref, ol_ref, zr_ref, wg_ref, wl_ref, o_ref):
        o_ref[...] = _gated_norm_fn(og_ref[...], ol_ref[...], zr_ref[...], wg_ref[...], wl_ref[...]).astype(bf16)

    return pl.pallas_call(
        body, name="gated_norm", grid=(Lp // tr,),
        in_specs=[pl.BlockSpec((tr, GDN_V), lambda i: (i, 0)), pl.BlockSpec((tr, GLA_V), lambda i: (i, 0)),
                  pl.BlockSpec((tr, ZR_W), lambda i: (i, C_ZR // ZR_W)),
                  pl.BlockSpec((1, GDN_DV), lambda i: (0, 0)), pl.BlockSpec((1, GLA_DV), lambda i: (0, 0))],
        out_specs=pl.BlockSpec((tr, ZR_W), lambda i: (i, 0)),
        out_shape=jax.ShapeDtypeStruct((Lp, ZR_W), bf16),
        compiler_params=_params("parallel"),
    )(og, ol, proj, wg, wl)


def _gated_norm_bwd(og, ol, proj, wg, wl, dmix):
    Lp = og.shape[0]
    tr = _tile(Lp, 256)

    def body(og_ref, ol_ref, zr_ref, wg_ref, wl_ref, d_ref, dog_ref, dol_ref, dzr_ref, gwg_ref, gwl_ref):
        i = pl.program_id(0)
        _, vjp = jax.vjp(_gated_norm_fn, og_ref[...], ol_ref[...], zr_ref[...], wg_ref[...], wl_ref[...])
        dog, dol, dzr, gwg, gwl = vjp(d_ref[...])
        dog_ref[...] = dog
        dol_ref[...] = dol
        dzr_ref[...] = dzr.astype(bf16)

        @pl.when(i == 0)
        def _():
            gwg_ref[...] = gwg
            gwl_ref[...] = gwl

        @pl.when(i > 0)
        def _():
            gwg_ref[...] += gwg
            gwl_ref[...] += gwl

    og_spec = pl.BlockSpec((tr, GDN_V), lambda i: (i, 0))
    ol_spec = pl.BlockSpec((tr, GLA_V), lambda i: (i, 0))
    zr_spec = pl.BlockSpec((tr, ZR_W), lambda i: (i, C_ZR // ZR_W))
    vg = pl.BlockSpec((1, GDN_DV), lambda i: (0, 0))
    vl = pl.BlockSpec((1, GLA_DV), lambda i: (0, 0))
    return pl.pallas_call(
        body, name="gated_norm_bwd", grid=(Lp // tr,),
        in_specs=[og_spec, ol_spec, zr_spec, vg, vl, pl.BlockSpec((tr, ZR_W), lambda i: (i, 0))],
        out_specs=[og_spec, ol_spec, zr_spec, vg, vl],
        out_shape=[jax.ShapeDtypeStruct((Lp, GDN_V), f32), jax.ShapeDtypeStruct((Lp, GLA_V), f32),
                   jax.ShapeDtypeStruct((Lp, C_END), bf16),
                   jax.ShapeDtypeStruct((1, GDN_DV), f32), jax.ShapeDtypeStruct((1, GLA_DV), f32)],
        compiler_params=_params("arbitrary"),
    )(og, ol, proj, wg, wl, dmix)


def _adamw_rule(g_, w_, m_, v_):
    c1 = 1.0 - ADAM_B1 ** ADAM_STEP
    c2 = 1.0 - ADAM_B2 ** ADAM_STEP
    m2 = ADAM_B1 * m_ + (1.0 - ADAM_B1) * g_
    v2 = ADAM_B2 * v_ + (1.0 - ADAM_B2) * (g_ * g_)
    return -ADAM_LR * ((m2 / c1) / (jnp.sqrt(v2 / c2) + ADAM_EPS) + ADAM_WD * w_), m2, v2


def _adamw(g, w, m, v, name, copy_g=False):
    R, C = g.shape
    tr = _tile(R, 256, 8) if R % 8 == 0 and R > 256 else R

    def body(g_ref, w_ref, m_ref, v_ref, *o_refs):
        g_ = g_ref[...]
        if copy_g:
            o_refs[0][...] = g_
        d_ref, mo_ref, vo_ref = o_refs[-3:]
        d_ref[...], mo_ref[...], vo_ref[...] = _adamw_rule(g_, w_ref[...], m_ref[...], v_ref[...])

    blk = pl.BlockSpec((tr, C), lambda i: (i, 0))
    n_out = 4 if copy_g else 3
    return pl.pallas_call(
        body, name=name, grid=(R // tr,), in_specs=[blk] * 4, out_specs=[blk] * n_out,
        out_shape=[jax.ShapeDtypeStruct((R, C), f32)] * n_out,
        compiler_params=_params("parallel"),
    )(g, w, m, v)


def _adamw_transposed(gt, w, m, v, name):
    n, C, rb = gt.shape
    assert w.shape == (C, n * rb), (gt.shape, w.shape)

    def body(g_ref, w_ref, m_ref, v_ref, go_ref, d_ref, mo_ref, vo_ref):
        g_ = g_ref[...].astype(f32)
        go_ref[...] = g_
        d_ref[...], mo_ref[...], vo_ref[...] = _adamw_rule(g_, w_ref[...], m_ref[...], v_ref[...])

    blk = pl.BlockSpec((C, rb), lambda j: (0, j))
    return pl.pallas_call(
        body, name=name, grid=(n,), in_specs=[pl.BlockSpec((None, C, rb), lambda j: (j, 0, 0))] + [blk] * 3,
        out_specs=[blk] * 4, out_shape=[jax.ShapeDtypeStruct((C, n * rb), f32)] * 4,
        compiler_params=_params("parallel"),
    )(gt, w, m, v)


def _sum_slots(r, name):
    n, R, C = r.shape
    tr = _tile(R, 128, 16) if R % 16 == 0 and R > 128 else R

    def body(r_ref, o_ref):
        acc = r_ref[0].astype(f32)
        for s in range(1, n):
            acc = acc + r_ref[s].astype(f32)
        o_ref[...] = acc

    return pl.pallas_call(
        body, name=name, grid=(R // tr,),
        in_specs=[pl.BlockSpec((n, tr, C), lambda i: (0, i, 0))],
        out_specs=pl.BlockSpec((tr, C), lambda i: (i, 0)),
        out_shape=jax.ShapeDtypeStruct((R, C), f32),
        compiler_params=_params("parallel"),
    )(r)


SIBLING_PARTS = 8


class _Siblings:
    def __init__(self, arrays):
        self.arrays = list(arrays)
        self.n = len(self.arrays)
        self.parts = [next(p for p in range(SIBLING_PARTS, 0, -1) if a.shape[0] % (8 * p) == 0 or p == 1)
                      for a in self.arrays]
        total = sum(self.parts)
        self.out_shape = [jax.ShapeDtypeStruct((2,) + a.shape, a.dtype) for a in self.arrays]
        self.sems = [pltpu.SemaphoreType.DMA((total,)), pltpu.SemaphoreType.DMA((total,)),
                     pltpu.SemaphoreType.DMA((self.n,))]

    def hooks(self, ins, outs, send, recv, lsem):
        def copies():
            x, y, c = lax.axis_index("x"), lax.axis_index("y"), lax.axis_index("c")
            out, k = [], 0
            for a in range(self.n):
                out.append((pltpu.make_async_copy(ins[a], outs[a].at[c], lsem.at[a]), None))
                rows = self.arrays[a].shape[0] // self.parts[a]
                for part in range(self.parts[a]):
                    r = pl.ds(part * rows, rows)
                    mk = lambda dst, a=a, r=r, k=k: pltpu.make_async_remote_copy(
                        src_ref=ins[a].at[r], dst_ref=dst.at[r], send_sem=send.at[k], recv_sem=recv.at[k],
                        device_id=(x, y, 1 - c), device_id_type=MESH)
                    out.append((mk(outs[a].at[c]), mk(outs[a].at[1 - c])))
                    k += 1
            return out

        return _start_wait(copies)


def _comm_now(name, sides):
    total = sum(s.n for s in sides)

    def body(*refs):
        ins, outs, sems = refs[:total], refs[total:2 * total], refs[2 * total:]
        hooks, o = [], 0
        for i, s in enumerate(sides):
            hooks.append(s.hooks(ins[o:o + s.n], outs[o:o + s.n], *sems[3 * i:3 * i + 3]))
            o += s.n
        for start, _ in hooks:
            start()
        for _, wait in hooks:
            wait()

    out = pl.pallas_call(
        body, name=name, in_specs=[_ANY] * total, out_specs=[_ANY] * total,
        out_shape=[sh for s in sides for sh in s.out_shape], scratch_shapes=[sm for s in sides for sm in s.sems],
    )(*[a for s in sides for a in s.arrays])
    res, o = [], 0
    for s in sides:
        res.append(list(out[o:o + s.n]))
        o += s.n
    return res


def _cat_cols(g):
    return jnp.concatenate([g[i] for i in range(N_CHIP)], axis=-1)


def _row_slabs(a):
    return a.reshape(N_DEV, a.shape[0] // N_DEV, a.shape[1])


def _w_in_columns(g_wp, g_wsm):
    return jnp.concatenate([g_wp[:, C_QKV:C_END], g_wp[:, C_ZR:C_ZR + GDN_V], g_wsm[:, :SM_LR],
                            g_wp[:, C_G:C_G + G_W], g_wp[:, C_ZR + GDN_V:C_ZR + ZR_W],
                            g_wsm[:, SM_LR:SM_LR + GATE_RANK]], axis=1)


def _step(x, loss_target, p, meta, shard):
    _, S, D = x.shape
    alog_p = jnp.pad(p["gdn_a_log"], ((0, 0), (0, SM_W - GDN_H)))
    dtb_p = jnp.pad(p["gdn_dt_bias"], ((0, 0), (0, SM_W - GDN_H)))
    m64 = jnp.concatenate([jnp.zeros((PAD, D), f32), meta], axis=0)
    gate_b, gdn_norm_w, gla_norm_w = p["gla_gate_b"], p["gdn_norm_w"], p["gla_norm_w"]

    later = ("w_up", "w_out", "w_gate", "w_down")
    h0, n1, in_bf16, (w_in4, conv4, w24) = _embed_norm(
        x, m64, p["attn_norm_w"], to_bf16=[shard[k] for k in later],
        side=_Gather([shard["w_in"], shard["gdn_conv_w"], shard["gla_gate_w2"]]))
    shard = dict(shard, **dict(zip(later, in_bf16)))
    w_in, conv_w, w2 = _cat_cols(w_in4), _cat_cols(conv4), _cat_cols(w24)
    wp = jnp.concatenate([w_in[:, R_Z:R_AB], w_in[:, R_GR:R_LR], w_in[:, R_G:R_GR], w_in[:, R_QKV:R_Z]], axis=1)
    wsm = jnp.concatenate([w_in[:, R_AB:R_G], w_in[:, R_LR:R_END],
                           jnp.zeros((D, SM_W - SM_LR - GATE_RANK), w_in.dtype)], axis=1)
    w2p = jnp.pad(w2, ((SM_LR, SM_W - SM_LR - GATE_RANK), (0, 0)))
    proj, (w_up,) = _mm(n1, wp, "nn", "proj", side=_Gather([shard["w_up"]], by_columns=True))
    psm = _mm(n1, wsm, "nn", "proj_small")
    gb, la = _gates(psm, w2p, gate_b, alog_p, dtb_p)
    qkvc, (w_out4,) = _conv(proj, conv_w, side=_Gather([shard["w_out"]]))
    w_out = w_out4.reshape(-1, D)
    og, sall, pall, (w_gate,) = _gdn_fwd(qkvc, gb, side=_Gather([shard["w_gate"]], by_columns=True))
    ol, stall = _gla_fwd(proj, la)
    mixed = _gated_norm(og, ol, proj, gdn_norm_w, gla_norm_w)
    attn = _mm(mixed, w_out, "nn", "out_proj")
    h1, n2 = _add_norm(h0, attn, p["ffn_norm_w"])
    act, act_dgate, act_dup, (w_down4,) = _ffn_in(n2, w_gate, w_up, side=_Gather([shard["w_down"]]))
    w_down = w_down4.reshape(-1, D)
    ffn = _mm(act, w_down, "nn", "ffn_down", **WHOLE_K)
    dh2, dh2b, lossp, g_final = _final(h1, ffn, loss_target, p["final_norm_w"])

    g_down = _mm(act, dh2b, "tn", "g_w_down", out_dtype=bf16, **WHOLE_K_T)
    dg, du = _ffn_dact(dh2b, w_down, act_dgate, act_dup)
    g_gate = _mm(n2, dg, "tn", "g_w_gate", tm_cap=512, tn_cap=1408, tk_cap=2752, out_dtype=bf16, col_slabs=True)
    g_up = _mm(n2, du, "tn", "g_w_up", tm_cap=512, tn_cap=1408, tk_cap=2752, out_dtype=bf16, col_slabs=True)
    dn2 = _ffn_dn(dg, du, w_gate, w_up)
    dh1, dh1b, g_ffn_norm = _norm_bwd(dn2, h1, dh2, p["ffn_norm_w"])
    dmix = _mm(dh1b, w_out, "nt", "d_mixed")
    g_out = _mm(mixed, dh1b, "tn", "g_w_out", out_dtype=bf16, **WHOLE_K_T)
    dog, dol, dproj, g_gdn_norm, g_gla_norm = _gated_norm_bwd(og, ol, proj, gdn_norm_w, gla_norm_w, dmix)
    dproj, dla, (r_down,) = _gla_bwd(proj, la, stall, dol, dproj, side=_Exchange([_row_slabs(g_down)]))
    dqkvc, dgb, (r_gate, r_up, r_out, h_down) = _gdn_bwd(
        qkvc, gb, sall, pall, dog,
        side=_Sides(_Exchange([g_gate, g_up, _row_slabs(g_out)]), _Siblings([_sum_slots(r_down, "sum_w_down")])))
    dproj, g_conv, (h_gate,) = _conv_bwd(proj, conv_w, dqkvc, dproj,
                                         side=_Siblings([_sum_slots(r_gate, "sum_w_gate")]))
    dpsm, g_w2p, g_gate_b, g_alog, g_dtb = _gates_bwd(psm, w2p, gate_b, alog_p, dtb_p, dgb, dla)
    g_wsm = _mm(n1, dpsm, "tn", "g_w_in_small", out_dtype=bf16, **WHOLE_K_T)
    g_wp_a, (h_out,) = _mm(n1, dproj, "tn", "g_w_in_a", out_dtype=bf16, b_cols=(0, W_IN_SPLIT),
                           side=_Siblings([_sum_slots(r_out, "sum_w_out")]), **WHOLE_K_T)
    g_wp_b, r_in_a = _mm(n1, dproj, "tn", "g_w_in_b", out_dtype=bf16, b_cols=(W_IN_SPLIT, C_END - W_IN_SPLIT),
                         side=_Exchange([_row_slabs(g_wp_a), _row_slabs(g_wsm)]), **WHOLE_K_T)
    dn1, (r_in_b, h_up) = _mm(
        dproj, wp, "nt", "d_n1", plus=(dpsm, wsm),
        side=_Sides(_Exchange([_row_slabs(g_wp_b)]), _Siblings([_sum_slots(r_up, "sum_w_up")])), **WHOLE_K)
    s_wp = jnp.concatenate([_sum_slots(r_in_a[0], "sum_w_in_a"), _sum_slots(r_in_b, "sum_w_in_b")], axis=1)
    s_in = _w_in_columns(s_wp, _sum_slots(r_in_a[1], "sum_w_in_small"))
    in_by_chip = s_in.reshape(s_in.shape[0], N_CHIP, -1).transpose(1, 2, 0).astype(bf16)
    grad_x, g_meta, g_attn_norm, (h_in,) = _embed_norm_bwd(dn1, h0, dh1, p["attn_norm_w"], S,
                                                           side=_Exchange([], by_chip=[in_by_chip]))

    received = dict(w_in=h_in, w_gate=h_gate, w_up=h_up, w_out=h_out, w_down=h_down)
    small = dict(
        meta_tokens=g_meta, attn_norm_w=g_attn_norm, gdn_conv_w=g_conv, gdn_a_log=g_alog[:, :GDN_H],
        gdn_dt_bias=g_dtb[:, :GDN_H], gdn_norm_w=g_gdn_norm, gla_gate_w2=g_w2p[SM_LR:SM_LR + GATE_RANK],
        gla_gate_b=g_gate_b, gla_norm_w=g_gla_norm, ffn_norm_w=g_ffn_norm, final_norm_w=g_final)
    return lossp[0, 0], grad_x, received, small


_WEIGHTS = ("meta_tokens", "attn_norm_w", "w_in", "gdn_conv_w", "gdn_a_log", "gdn_dt_bias", "gdn_norm_w",
            "gla_gate_w2", "gla_gate_b", "gla_norm_w", "w_out", "ffn_norm_w", "w_gate", "w_up", "w_down",
            "final_norm_w")
_BIG_COLS = ("w_in", "w_gate", "w_up")
_BIG_ROWS = ("w_out", "w_down")
_SMALL_SHARDED = ("meta_tokens", "gdn_conv_w", "gla_gate_w2")


def kernel(x, meta_tokens, attn_norm_w, w_in, gdn_conv_w, gdn_a_log, gdn_dt_bias, gdn_norm_w, gla_gate_w2, gla_gate_b, gla_norm_w, w_out, ffn_norm_w, w_gate, w_up, w_down, final_norm_w, loss_target, m_meta_tokens, m_attn_norm_w, m_w_in, m_gdn_conv_w, m_gdn_a_log, m_gdn_dt_bias, m_gdn_norm_w, m_gla_gate_w2, m_gla_gate_b, m_gla_norm_w, m_w_out, m_ffn_norm_w, m_w_gate, m_w_up, m_w_down, m_final_norm_w, v_meta_tokens, v_attn_norm_w, v_w_in, v_gdn_conv_w, v_gdn_a_log, v_gdn_dt_bias, v_gdn_norm_w, v_gla_gate_w2, v_gla_gate_b, v_gla_norm_w, v_w_out, v_ffn_norm_w, v_w_gate, v_w_up, v_w_down, v_final_norm_w):
    w = dict(meta_tokens=meta_tokens, attn_norm_w=attn_norm_w, w_in=w_in, gdn_conv_w=gdn_conv_w, gdn_a_log=gdn_a_log,
             gdn_dt_bias=gdn_dt_bias, gdn_norm_w=gdn_norm_w, gla_gate_w2=gla_gate_w2, gla_gate_b=gla_gate_b,
             gla_norm_w=gla_norm_w, w_out=w_out, ffn_norm_w=ffn_norm_w, w_gate=w_gate, w_up=w_up, w_down=w_down,
             final_norm_w=final_norm_w)
    m = dict(meta_tokens=m_meta_tokens, attn_norm_w=m_attn_norm_w, w_in=m_w_in, gdn_conv_w=m_gdn_conv_w,
             gdn_a_log=m_gdn_a_log, gdn_dt_bias=m_gdn_dt_bias, gdn_norm_w=m_gdn_norm_w, gla_gate_w2=m_gla_gate_w2,
             gla_gate_b=m_gla_gate_b, gla_norm_w=m_gla_norm_w, w_out=m_w_out, ffn_norm_w=m_ffn_norm_w,
             w_gate=m_w_gate, w_up=m_w_up, w_down=m_w_down, final_norm_w=m_final_norm_w)
    v = dict(meta_tokens=v_meta_tokens, attn_norm_w=v_attn_norm_w, w_in=v_w_in, gdn_conv_w=v_gdn_conv_w,
             gdn_a_log=v_gdn_a_log, gdn_dt_bias=v_gdn_dt_bias, gdn_norm_w=v_gdn_norm_w, gla_gate_w2=v_gla_gate_w2,
             gla_gate_b=v_gla_gate_b, gla_norm_w=v_gla_norm_w, w_out=v_w_out, ffn_norm_w=v_ffn_norm_w,
             w_gate=v_w_gate, w_up=v_w_up, w_down=v_w_down, final_norm_w=v_final_norm_w)
    chip = 2 * lax.axis_index("x") + lax.axis_index("y")

    def two_d(a):
        return a.reshape(1, -1) if a.ndim == 1 else a.reshape(-1, a.shape[-1])

    w2d = {k: two_d(a) for k, a in w.items()}
    big = _BIG_COLS + _BIG_ROWS
    small = tuple(k for k in _WEIGHTS if k not in big)

    (meta4,), = _comm_now("gather_meta", [_Gather([w2d["meta_tokens"]])])
    shard = {k: w2d[k] for k in big + ("gdn_conv_w", "gla_gate_w2")}
    shard["w_in"] = w2d["w_in"].astype(bf16)
    lossp, grad_x, received, g = _step(x, loss_target, {k: w2d[k] for k in small}, _cat_cols(meta4), shard)
    loss = lax.psum(lossp, ("x", "y", "c"))

    sizes = [g[k].size for k in small]
    total = sum(sizes)
    rows = -(-total // 1024)
    rows += (-rows) % 8
    packed = jnp.concatenate([g[k].reshape(-1) for k in small] + [jnp.zeros((rows * 1024 - total,), f32)])
    (packed8,), = _comm_now("exchange_small", [_Exchange([], [packed.reshape(rows, 1024)])])
    red = {k: h.reshape(w2d[k].shape) for k, h in received.items() if k != "w_in"}
    psum_small = _sum_slots(packed8, "sum_small").reshape(-1)
    off = 0
    for k, n in zip(small, sizes):
        a = psum_small[off:off + n].reshape(g[k].shape)
        off += n
        if k in _SMALL_SHARDED:
            c = w2d[k].shape[1]
            a = lax.dynamic_slice_in_dim(a, chip * c, c, axis=1)
        red[k] = a

    grads, deltas, new_m, new_v = [], [], [], []
    for k in _WEIGHTS:
        shape = w[k].shape
        if k == "w_in":
            flip = lambda a: jnp.swapaxes(a, 1, 2).reshape(shape[2], shape[1])
            unflip = lambda a: jnp.swapaxes(a.reshape(shape[0], shape[2], shape[1]), 1, 2)
            out = _adamw_transposed(received[k], flip(w[k]), flip(m[k]), flip(v[k]), "adamw_" + k)
            gk, d, m2, v2 = [unflip(a) for a in out]
        elif k in big:
            gk, d, m2, v2 = [a.reshape(shape) for a in
                             _adamw(red[k], w2d[k], two_d(m[k]), two_d(v[k]), "adamw_" + k, copy_g=True)]
        else:
            gk = red[k].reshape(shape)
            d, m2, v2 = [a.reshape(shape) for a in _adamw(red[k], w2d[k], two_d(m[k]), two_d(v[k]), "adamw_" + k)]
        grads.append(gk)
        deltas.append(d)
        new_m.append(m2)
        new_v.append(v2)
    return (loss, grad_x, *grads, *deltas, *new_m, *new_v)
```

```python
import functools

import jax
import jax.numpy as jnp
from jax import lax
from jax.experimental import pallas as pl
from jax.experimental.pallas import tpu as pltpu

f32 = jnp.float32
bf16 = jnp.bfloat16
HIGH = lax.Precision.HIGH
MESH = pl.DeviceIdType.MESH

N_META = 16
CONV_K = 4
GDN_H, GDN_DK, GDN_DV, GDN_C = 8, 128, 128, 64
GLA_H, GLA_DK, GLA_DV, GLA_C = 4, 128, 256, 16
GATE_RANK = 16
GATE_NORMALIZER = 16.0
EPS = 1e-6
GDN_QK = GDN_H * GDN_DK
GDN_V = GDN_H * GDN_DV
GLA_QK = GLA_H * GLA_DK
GLA_V = GLA_H * GLA_DV
PAD = (-N_META) % GDN_C
OFF = PAD + N_META
ROWS = 64

R_QKV, R_Z, R_AB, R_G, R_GR, R_LR, R_END = 0, 3072, 4096, 4112, 6160, 7184, 7200
C_ZR, C_G, C_QKV, C_END = 0, 2048, 4096, 7168
W_IN_SPLIT = 3072
ZR_W = GDN_V + GLA_V
G_W = 2 * GLA_QK + GLA_V
QKV_W = 2 * GDN_QK + GDN_V
Q0, K0, V0 = 0, GDN_QK, 2 * GDN_QK
SM_W = 128
SM_LR = 2 * GDN_H

ADAM_LR, ADAM_B1, ADAM_B2, ADAM_EPS, ADAM_WD, ADAM_STEP = 0.001, 0.9, 0.999, 1e-08, 0.01, 10

VMEM_LIMIT_V7X = 56 * 1024 * 1024
LANES = 128
N_DEV = 8
N_CHIP = 4


def _params(*sem):
    return pltpu.CompilerParams(dimension_semantics=sem, vmem_limit_bytes=VMEM_LIMIT_V7X)


def _tile(n, cap, mult=16):
    best = None
    for d in range(mult, min(n, cap) + 1, mult):
        if n % d == 0:
            best = d
    assert best is not None, (n, cap, mult)
    return best


NN = ((1,), (0,))
NT = ((1,), (1,))
TN = ((0,), (0,))


def _dot(a, b, dims, prec=None):
    return lax.dot_general(a, b, (dims, ((), ())), precision=prec, preferred_element_type=f32)


def _mmb(a, b, dims):
    return _dot(a.astype(bf16), b.astype(bf16), dims)


def _sigmoid(x):
    return jax.nn.sigmoid(x)


def _silu(x):
    return x * _sigmoid(x)


def _dsilu(x):
    s = _sigmoid(x)
    return s * (1.0 + x * (1.0 - s))


def _log1p_exp_neg_abs(x):
    t = jnp.exp(-jnp.abs(x))
    u = 1.0 + t
    d = u - 1.0
    return jnp.where(d == 0.0, t, jnp.log(u) * (t / jnp.where(d == 0.0, 1.0, d)))


def _softplus(x):
    return jnp.maximum(x, 0.0) + _log1p_exp_neg_abs(x)


def _log_sigmoid(x):
    return jnp.minimum(x, 0.0) - _log1p_exp_neg_abs(x)


def _rms(x):
    r = lax.rsqrt(jnp.mean(x * x, axis=-1, keepdims=True) + EPS)
    return x * r, r


def _rms_bwd(dy, xh, r, w):
    t = dy * w
    return r * (t - xh * jnp.mean(t * xh, axis=-1, keepdims=True))


def _l2n(x):
    return x * lax.rsqrt(jnp.sum(x * x, axis=-1, keepdims=True) + EPS)


INV_LEAF = 8


def _same_block(C, b):
    sh = b.bit_length() - 1
    row = lax.broadcasted_iota(jnp.int32, (C, C), 0)
    col = lax.broadcasted_iota(jnp.int32, (C, C), 1)
    return lax.shift_right_logical(row, sh) == lax.shift_right_logical(col, sh)


def _tri_inv_impl(As):
    C = As[0].shape[0]
    R = range(len(As))
    row = lax.broadcasted_iota(jnp.int32, (C, C), 0)
    col = lax.broadcasted_iota(jnp.int32, (C, C), 1)
    eye = (row == col).astype(f32)
    b = INV_LEAF
    inner = _same_block(C, b)
    leaf = [jnp.where(inner, As[h], 0.0) for h in R]
    d = [eye - leaf[h] for h in R]
    pw = leaf
    n = 2
    while n < b:
        pw = [_dot(pw[h], pw[h], NN, HIGH) for h in R]
        d = [_dot(d[h], eye + pw[h], NN, HIGH) for h in R]
        n *= 2
    while b < C:
        outer = _same_block(C, 2 * b)
        level = jnp.logical_and(outer, jnp.logical_not(inner))
        ed = [_dot(jnp.where(level, As[h], 0.0), d[h], NN, HIGH) for h in R]
        d = [d[h] - _dot(d[h], ed[h], NN, HIGH) for h in R]
        inner = outer
        b *= 2
    return d


@jax.custom_vjp
def _tri_inv(As):
    return _tri_inv_impl(As)


def _tri_inv_fwd(As):
    d = _tri_inv_impl(As)
    return d, d


def _tri_inv_bwd(d, g):
    R = range(len(d))
    t = [_dot(d[h], g[h], TN, HIGH) for h in R]
    return ([-_dot(t[h], d[h], NT, HIGH) for h in R],)


_tri_inv.defvjp(_tri_inv_fwd, _tri_inv_bwd)


@jax.custom_vjp
def _tri_inv_known(As, Ps):
    del As
    return Ps


def _tri_inv_known_fwd(As, Ps):
    del As
    return Ps, Ps


def _tri_inv_known_bwd(d, g):
    return _tri_inv_bwd(d, g)[0], [jnp.zeros_like(x) for x in d]


_tri_inv_known.defvjp(_tri_inv_known_fwd, _tri_inv_known_bwd)


def _gdn_chunk(Ss, qrs, krs, vs, betas, gs, Ps=None):
    H = len(Ss)
    C, dk = qrs[0].shape
    R = range(len(qrs))
    row = lax.broadcasted_iota(jnp.int32, (C, C), 0)
    col = lax.broadcasted_iota(jnp.int32, (C, C), 1)
    causal = row >= col
    strict = row > col
    cf = causal.astype(f32)
    q = [_l2n(qrs[h]) * (dk ** -0.5) for h in R]
    k = [_l2n(krs[h]) for h in R]
    mc = [_rows_exact(cf, jnp.broadcast_to(gs[h], (C, C))) for h in R]
    gc = [mc[h][:, 0:1] for h in R]
    decay = [jnp.where(causal, jnp.exp(jnp.where(causal, mc[h] - mc[h].T, 0.0)), 0.0) for h in R]
    kb = [k[h] * betas[h] for h in R]
    a = [jnp.where(strict, _mmb(kb[h], k[h], NT) * decay[h], 0.0) for h in R]
    p = _tri_inv(a) if Ps is None else _tri_inv_known(a, Ps)
    egc = [jnp.exp(gc[h]) for h in R]
    u = [_mmb(p[h], vs[h] * betas[h], NN) for h in R]
    w = [_mmb(p[h], kb[h] * egc[h], NN) for h in R]
    qk = [jnp.where(causal, _mmb(q[h], k[h], NT) * decay[h], 0.0) for h in R]
    qe = [q[h] * egc[h] for h in R]
    gl = [gc[h][C - 1:C, :] for h in R]
    kd = [k[h] * jnp.exp(gl[h] - gc[h]) for h in R]
    egl = [jnp.exp(gl[h]) for h in R]
    S, o, entering = list(Ss), [], []
    for chunk in range(len(qrs) // H):
        idx = [chunk * H + h for h in range(H)]
        entering += S
        v_new = [u[i] - _mmb(w[i], S[h], NN) for h, i in enumerate(idx)]
        o += [_mmb(qe[i], S[h], NN) + _mmb(qk[i], v_new[h], NN) for h, i in enumerate(idx)]
        S = [S[h] * egl[i] + _mmb(kd[i], v_new[h], TN) for h, i in enumerate(idx)]
    return S, o, p, entering


def _rows_exact_impl(m01, x, dims):
    m = m01.astype(bf16)
    x1 = x.astype(bf16)
    r1 = x - x1.astype(f32)
    x2 = r1.astype(bf16)
    x3 = (r1 - x2.astype(f32)).astype(bf16)
    d = lambda y: _dot(m, y, dims)
    return d(x1) + (d(x2) + d(x3))


@jax.custom_vjp
def _rows_exact(m01, x):
    return _rows_exact_impl(m01, x, NN)


def _rows_exact_fwd(m01, x):
    return _rows_exact_impl(m01, x, NN), m01


def _rows_exact_bwd(m01, g):
    return jnp.zeros_like(m01), _rows_exact_impl(m01, g, TN)


_rows_exact.defvjp(_rows_exact_fwd, _rows_exact_bwd)


def _gla_blocks(Sts, qrs, ks, vs, las):
    H = len(Sts)
    n = len(qrs)
    C, dk = qrs[0].shape
    R = range(n)
    row = lax.broadcasted_iota(jnp.int32, (C, C), 0)
    col = lax.broadcasted_iota(jnp.int32, (C, C), 1)
    ri = lax.broadcasted_iota(jnp.int32, (C, dk), 0)
    q = [qrs[h] * (dk ** -0.5) for h in R]
    running = (row >= col).astype(f32)
    b = [_rows_exact(running, las[h]) for h in R]
    sc = [jnp.where(row == col, jnp.sum(q[h] * ks[h], axis=-1, keepdims=True), 0.0) for h in R]
    s = C // 2
    while s >= 1:
        sh = s.bit_length() - 1
        ref = lax.shift_left(lax.shift_right_logical(row, sh + 1), sh + 1) + (s - 1)
        pick = (col == ref).astype(f32)
        bref = [_rows_exact(pick, b[h]) for h in R]
        upper = (lax.shift_right_logical(ri, sh) & 1) == 1
        qt = [jnp.where(upper, q[h] * jnp.exp(jnp.where(upper, b[h] - bref[h], 0.0)), 0.0) for h in R]
        kt = [jnp.where(upper, 0.0, ks[h] * jnp.exp(jnp.where(upper, 0.0, bref[h] - b[h]))) for h in R]
        same = lax.shift_right_logical(row, sh + 1) == lax.shift_right_logical(col, sh + 1)
        sc = [sc[h] + jnp.where(same, _mmb(qt[h], kt[h], NT), 0.0) for h in R]
        s //= 2
    o = [_mmb(sc[h], vs[h], NN) for h in R]
    qe = [q[h] * jnp.exp(b[h]) for h in R]
    bl = [b[h][C - 1:C, :] for h in R]
    upd = [_mmb(vs[h], ks[h] * jnp.exp(bl[h] - b[h]), TN) for h in R]
    ebl = [jnp.exp(bl[h]) for h in R]
    St = list(Sts)
    for blk in range(n // H):
        for h in range(H):
            i = blk * H + h
            o[i] = o[i] + _mmb(qe[i], St[h], NT)
        St = [St[h] * ebl[blk * H + h] + upd[blk * H + h] for h in range(H)]
    return St, o


_ANY = pl.BlockSpec(memory_space=pl.ANY)


class _Gather:
    def __init__(self, arrays, by_columns=False):
        self.arrays = list(arrays)
        self.n = len(self.arrays)
        self.by_columns = by_columns
        if by_columns:
            assert all(a.ndim == 2 and a.shape[1] % LANES == 0 for a in self.arrays)
            self.out_shape = [jax.ShapeDtypeStruct((a.shape[0], N_CHIP * a.shape[1]), a.dtype) for a in self.arrays]
        else:
            self.out_shape = [jax.ShapeDtypeStruct((N_CHIP,) + a.shape, a.dtype) for a in self.arrays]
        self.sems = [pltpu.SemaphoreType.DMA((self.n, 3)), pltpu.SemaphoreType.DMA((self.n, 3)),
                     pltpu.SemaphoreType.DMA((self.n,))]

    def hooks(self, ins, outs, send, recv, lsem):
        def place(a, chip):
            if not self.by_columns:
                return outs[a].at[chip]
            cols = self.arrays[a].shape[1]
            return outs[a].at[:, pl.ds(pl.multiple_of(chip * cols, LANES), cols)]

        def copies():
            x, y, c = lax.axis_index("x"), lax.axis_index("y"), lax.axis_index("c")
            me = 2 * x + y
            out = []
            for a in range(self.n):
                out.append((pltpu.make_async_copy(ins[a], place(a, me), lsem.at[a]), None))
                for j, (px, py) in enumerate([(1 - x, y), (x, 1 - y), (1 - x, 1 - y)]):
                    mk = lambda dst, a=a, j=j, px=px, py=py: pltpu.make_async_remote_copy(
                        src_ref=ins[a], dst_ref=dst, send_sem=send.at[a, j], recv_sem=recv.at[a, j],
                        device_id=(px, py, c), device_id_type=MESH)
                    out.append((mk(place(a, me)), mk(place(a, 2 * px + py))))
            return out

        return _start_wait(copies)


class _Exchange:
    def __init__(self, slotted, shared=(), by_chip=()):
        self.arrays = list(slotted) + list(by_chip) + list(shared)
        self.ns, self.nc = len(slotted), len(by_chip)
        self.n = len(self.arrays)
        self.out_shape = [jax.ShapeDtypeStruct(a.shape, a.dtype) for a in slotted]
        self.out_shape += [jax.ShapeDtypeStruct((N_DEV,) + a.shape[1:], a.dtype) for a in by_chip]
        self.out_shape += [jax.ShapeDtypeStruct((N_DEV,) + b.shape, b.dtype) for b in shared]
        self.sems = [pltpu.SemaphoreType.DMA((self.n, N_DEV - 1)), pltpu.SemaphoreType.DMA((self.n, N_DEV - 1)),
                     pltpu.SemaphoreType.DMA((self.n,))]

    def hooks(self, ins, outs, send, recv, lsem):
        def copies():
            x, y, c = lax.axis_index("x"), lax.axis_index("y"), lax.axis_index("c")
            me = 4 * x + 2 * y + c

            def src(a, dev):
                tx, ty, tc = dev
                if a < self.ns:
                    return ins[a].at[4 * tx + 2 * ty + tc]
                return ins[a].at[2 * tx + ty] if a < self.ns + self.nc else ins[a]

            out = []
            for a in range(self.n):
                out.append((pltpu.make_async_copy(src(a, (x, y, c)), outs[a].at[me], lsem.at[a]), None))
                for o in range(1, N_DEV):
                    dev = (1 - x if o & 4 else x, 1 - y if o & 2 else y, 1 - c if o & 1 else c)
                    t = 4 * dev[0] + 2 * dev[1] + dev[2]
                    mk = lambda dst, a=a, o=o, dev=dev: pltpu.make_async_remote_copy(
                        src_ref=src(a, dev), dst_ref=dst, send_sem=send.at[a, o - 1], recv_sem=recv.at[a, o - 1],
                        device_id=dev, device_id_type=MESH)
                    out.append((mk(outs[a].at[me]), mk(outs[a].at[t])))
            return out

        return _start_wait(copies)


class _Sides:
    def __init__(self, *members):
        self.members = members
        self.arrays = [a for s in members for a in s.arrays]
        self.n = len(self.arrays)
        self.out_shape = [sh for s in members for sh in s.out_shape]
        self.sems = [sm for s in members for sm in s.sems]

    def hooks(self, ins, outs, *sems):
        hooks, o = [], 0
        for i, s in enumerate(self.members):
            hooks.append(s.hooks(ins[o:o + s.n], outs[o:o + s.n], *sems[3 * i:3 * i + 3]))
            o += s.n

        def start():
            for st, _ in hooks:
                st()

        def wait():
            for _, wt in hooks:
                wt()

        return start, wait


def _start_wait(copies):
    def start():
        for s, _ in copies():
            s.start()

    def wait():
        for s, w in copies():
            (s if w is None else w).wait()

    return start, wait


def _call(body, *, name, grid, in_specs, out_specs, out_shape, args, sem, scratch_shapes=(), aliases=None, side=None):
    in_specs, out_specs, out_shape, args = list(in_specs), list(out_specs), list(out_shape), list(args)
    scratch_shapes = list(scratch_shapes)
    aliases = aliases or {}
    if side is None:
        return pl.pallas_call(
            body, name=name, grid=grid, in_specs=in_specs, out_specs=out_specs, out_shape=out_shape,
            scratch_shapes=scratch_shapes, input_output_aliases=aliases, compiler_params=_params(*sem))(*args)
    n_in, n_out, n_scr, ns = len(in_specs), len(out_specs), len(scratch_shapes), side.n

    def full_body(*refs):
        ins, refs = refs[:n_in], refs[n_in:]
        s_in, refs = refs[:ns], refs[ns:]
        outs, refs = refs[:n_out], refs[n_out:]
        s_out, refs = refs[:ns], refs[ns:]
        scr, sems = refs[:n_scr], refs[n_scr:]
        start, wait = side.hooks(s_in, s_out, *sems)
        ids = [pl.program_id(d) for d in range(len(grid))]
        first = functools.reduce(jnp.logical_and, [i == 0 for i in ids])
        last = functools.reduce(jnp.logical_and, [i == g - 1 for i, g in zip(ids, grid)])
        pl.when(first)(start)
        body(*ins, *outs, *scr)
        pl.when(last)(wait)

    return pl.pallas_call(
        full_body, name=name, grid=grid, in_specs=in_specs + [_ANY] * ns, out_specs=out_specs + [_ANY] * ns,
        out_shape=out_shape + side.out_shape, scratch_shapes=scratch_shapes + side.sems,
        input_output_aliases=aliases, compiler_params=_params(*(["arbitrary"] * len(grid))))(*args, *side.arrays)


WHOLE_K = dict(tm_cap=688, tn_cap=512, tk_cap=1 << 20)
WHOLE_K_T = dict(tm_cap=512, tn_cap=512, tk_cap=1 << 20)

def _mm(a, b, mode, name, *, tm_cap=1408, tn_cap=1024, tk_cap=2048, out_dtype=f32, acc_in=None, side=None,
        col_slabs=False, b_cols=None, plus=None):
    if mode == "nn":
        (M, K), (K2, N) = a.shape, b.shape
    elif mode == "nt":
        (M, K), (N, K2) = a.shape, b.shape
    else:
        (K, M), (K2, N) = a.shape, b.shape
    assert K == K2, (name, a.shape, b.shape)
    b_first = 0
    if b_cols is not None:
        assert mode != "nt"
        b_first, N = b_cols
    tm = _tile(M // 2 if col_slabs else M, tm_cap)
    tn = _tile(N // N_CHIP if col_slabs else N, tn_cap, 128)
    tk = _tile(K, tk_cap, 128 if K % 128 == 0 else 16)
    nk = K // tk
    dims = {"nn": NN, "nt": NT, "tn": TN}[mode]
    use_scratch = nk > 1 and out_dtype != f32

    def body(*refs):
        a_ref, b_ref, *rest = refs
        c_ref = rest.pop(0) if acc_in is not None else None
        p = _mmb(a_ref[...], b_ref[...], dims)
        if plus is not None:
            p = p + _mmb(rest.pop(0)[...], rest.pop(0)[...], dims)
        o_ref, *scr = rest
        if nk == 1:
            if c_ref is not None:
                p = p + c_ref[...]
            o_ref[...] = p.astype(out_dtype)
            return
        k = pl.program_id(2)
        acc = scr[0] if use_scratch else o_ref

        @pl.when(k == 0)
        def _():
            acc[...] = p if c_ref is None else p + c_ref[...]

        @pl.when(k > 0)
        def _():
            acc[...] += p

        if use_scratch:
            @pl.when(k == nk - 1)
            def _():
                o_ref[...] = acc[...].astype(out_dtype)

    if mode == "tn":
        a_spec = pl.BlockSpec((tk, tm), lambda i, j, k: (k, i))
    else:
        a_spec = pl.BlockSpec((tm, tk), lambda i, j, k: (i, k))
    if mode == "nt":
        b_spec = pl.BlockSpec((tn, tk), lambda i, j, k: (j, k))
    else:
        assert b_first % tn == 0, (name, b_first, tn)
        b_spec = pl.BlockSpec((tk, tn), lambda i, j, k: (k, j + b_first // tn))
    if col_slabs:
        assert acc_in is None
        ni, nj = M // 2 // tm, N // N_CHIP // tn
        o_spec = pl.BlockSpec((None, tm, tn), lambda i, j, k: (2 * (j // nj) + i // ni, i % ni, j % nj))
        o_shape = jax.ShapeDtypeStruct((N_DEV, M // 2, N // N_CHIP), out_dtype)
    else:
        o_spec = pl.BlockSpec((tm, tn), lambda i, j, k: (i, j))
        o_shape = jax.ShapeDtypeStruct((M, N), out_dtype)
    in_specs = [a_spec, b_spec]
    args = [a, b]
    if acc_in is not None:
        in_specs.append(o_spec)
        args.append(acc_in)
    if plus is not None:
        assert mode == "nt" and nk == 1, (name, mode, nk)
        k2 = plus[0].shape[1]
        assert plus[0].shape == (M, k2) and plus[1].shape == (N, k2), (name, plus[0].shape, plus[1].shape)
        in_specs += [pl.BlockSpec((tm, k2), lambda i, j, k: (i, 0)), pl.BlockSpec((tn, k2), lambda i, j, k: (j, 0))]
        args += list(plus)
    out = _call(body, name=name, grid=(M // tm, N // tn, nk), in_specs=in_specs, out_specs=[o_spec],
                out_shape=[o_shape], args=args,
                scratch_shapes=[pltpu.VMEM((tm, tn), f32)] if use_scratch else [],
                sem=("parallel", "parallel", "arbitrary"), side=side)
    return out[0] if side is None else (out[0], out[1:])


def _embed_norm(x3, m64, w, to_bf16=(), side=None):
    _, S, D = x3.shape
    Lp = OFF + S
    steps = Lp // ROWS
    nc = len(to_bf16)

    def body(x_ref, m_ref, w_ref, *rest):
        c_in, (h_ref, n_ref), c_out = rest[:nc], rest[nc:nc + 2], rest[nc + 2:]
        i = pl.program_id(0)
        h = jnp.where(i == 0, m_ref[...], x_ref[...])
        h_ref[...] = h
        xh, _ = _rms(h)
        n_ref[...] = (xh * w_ref[...]).astype(bf16)
        for ci, co in zip(c_in, c_out):
            co[...] = ci[...].astype(bf16)

    def piece(a):
        R, C = a.shape
        rb = next(d for d in range(16, R + 1, 16) if R % d == 0 and d * steps >= R)
        return pl.BlockSpec((rb, C), lambda i: (jnp.minimum(i, R // rb - 1), 0))

    row = pl.BlockSpec((ROWS, D), lambda i: (i, 0))
    pieces = [piece(a) for a in to_bf16]
    out = _call(
        body, name="embed_norm", grid=(steps,),
        in_specs=[pl.BlockSpec((None, ROWS, D), lambda i: (0, jnp.maximum(i - 1, 0), 0)),
                  pl.BlockSpec((ROWS, D), lambda i: (0, 0)),
                  pl.BlockSpec((1, D), lambda i: (0, 0))] + pieces,
        out_specs=[row, row] + pieces,
        out_shape=[jax.ShapeDtypeStruct((Lp, D), f32), jax.ShapeDtypeStruct((Lp, D), bf16)]
        + [jax.ShapeDtypeStruct(a.shape, bf16) for a in to_bf16],
        args=[x3, m64, w, *to_bf16], sem=("arbitrary",), side=side)
    return out[0], out[1], out[2:2 + nc], out[2 + nc:]


def _add_norm(h, d, w):
    Lp, D = h.shape
    tr = _tile(Lp, 256)

    def body(h_ref, d_ref, w_ref, o_ref, n_ref):
        h1 = h_ref[...] + d_ref[...]
        o_ref[...] = h1
        xh, _ = _rms(h1)
        n_ref[...] = (xh * w_ref[...]).astype(bf16)

    row = pl.BlockSpec((tr, D), lambda i: (i, 0))
    return pl.pallas_call(
        body, name="add_norm", grid=(Lp // tr,),
        in_specs=[row, row, pl.BlockSpec((1, D), lambda i: (0, 0))], out_specs=[row, row],
        out_shape=[jax.ShapeDtypeStruct((Lp, D), f32), jax.ShapeDtypeStruct((Lp, D), bf16)],
        compiler_params=_params("parallel"),
    )(h, d, w)


def _norm_bwd(dn, h, dh, w):
    Lp, D = h.shape
    tr = _tile(Lp, 256)

    def body(dn_ref, h_ref, dh_ref, w_ref, o_ref, ob_ref, gw_ref):
        i = pl.program_id(0)
        xh, r = _rms(h_ref[...])
        dn_ = dn_ref[...]
        o = dh_ref[...] + _rms_bwd(dn_, xh, r, w_ref[...])
        o_ref[...] = o
        ob_ref[...] = o.astype(bf16)
        gw = jnp.sum(dn_ * xh, axis=0, keepdims=True)

        @pl.when(i == 0)
        def _():
            gw_ref[...] = gw

        @pl.when(i > 0)
        def _():
            gw_ref[...] += gw

    row = pl.BlockSpec((tr, D), lambda i: (i, 0))
    vec = pl.BlockSpec((1, D), lambda i: (0, 0))
    return pl.pallas_call(
        body, name="norm_bwd", grid=(Lp // tr,), in_specs=[row, row, row, vec], out_specs=[row, row, vec],
        out_shape=[jax.ShapeDtypeStruct((Lp, D), f32), jax.ShapeDtypeStruct((Lp, D), bf16),
                   jax.ShapeDtypeStruct((1, D), f32)],
        compiler_params=_params("arbitrary"),
    )(dn, h, dh, w)


def _embed_norm_bwd(dn, h, dh, w, S, side=None):
    Lp, D = h.shape
    tr = _tile(S, 256, OFF)

    def body(dn_ref, h_ref, dh_ref, dn0_ref, h0_ref, dh0_ref, w_ref, gx_ref, gm_ref, gw_ref):
        i = pl.program_id(0)

        def rows(dn_, h_, dh_):
            xh, r = _rms(h_)
            return dh_ + _rms_bwd(dn_, xh, r, w_ref[...]), jnp.sum(dn_ * xh, axis=0, keepdims=True)

        d, gw = rows(dn_ref[...], h_ref[...], dh_ref[...])
        gx_ref[...] = d

        @pl.when(i == 0)
        def _():
            d0, gw0 = rows(dn0_ref[...], h0_ref[...], dh0_ref[...])
            gm_ref[...] = d0[PAD:OFF, :]
            gw_ref[...] = gw0 + gw

        @pl.when(i > 0)
        def _():
            gw_ref[...] += gw

    win = pl.BlockSpec((pl.Element(tr), pl.Element(D)), lambda i: (pl.multiple_of(OFF + i * tr, OFF), 0))
    head = pl.BlockSpec((OFF, D), lambda i: (0, 0))
    vec = pl.BlockSpec((1, D), lambda i: (0, 0))
    out = _call(
        body, name="embed_norm_bwd", grid=(S // tr,), in_specs=[win, win, win, head, head, head, vec],
        out_specs=[pl.BlockSpec((None, tr, D), lambda i: (0, i, 0)),
                   pl.BlockSpec((N_META, D), lambda i: (0, 0)), vec],
        out_shape=[jax.ShapeDtypeStruct((1, S, D), f32), jax.ShapeDtypeStruct((N_META, D), f32),
                   jax.ShapeDtypeStruct((1, D), f32)],
        args=[dn, h, dh, dn, h, dh, w], sem=("arbitrary",), side=side)
    return out[0], out[1], out[2], out[3:]


def _final(h1, ffn, tgt3, w):
    Lp, D = h1.shape
    _, S, _ = tgt3.shape
    tr = _tile(Lp, min(256, S), OFF)

    def body(h_ref, f_ref, t_ref, w_ref, d_ref, db_ref, l_ref, gw_ref):
        i = pl.program_id(0)
        h2 = h_ref[...] + f_ref[...]
        xh, r = _rms(h2)
        w_ = w_ref[...]
        t = t_ref[...]
        t = jnp.where(i == 0, pltpu.roll(t, OFF, 0), t)
        valid = (lax.broadcasted_iota(jnp.int32, (tr, 1), 0) + i * tr >= OFF).astype(f32)
        e = xh * w_ - t
        loss = 0.5 * jnp.sum(jnp.mean(e * e, axis=-1, keepdims=True) * valid, axis=0, keepdims=True)
        dy = e * (valid / D)
        d = _rms_bwd(dy, xh, r, w_)
        d_ref[...] = d
        db_ref[...] = d.astype(bf16)
        gw = jnp.sum(dy * xh, axis=0, keepdims=True)

        @pl.when(i == 0)
        def _():
            l_ref[...] = jnp.zeros_like(l_ref)
            gw_ref[...] = jnp.zeros_like(gw_ref)

        l_ref[...] += jnp.broadcast_to(loss, l_ref.shape)
        gw_ref[...] += gw

    row = pl.BlockSpec((tr, D), lambda i: (i, 0))
    vec = pl.BlockSpec((1, D), lambda i: (0, 0))
    tgt = pl.BlockSpec((pl.Element(tr), pl.Element(D)),
                       lambda i: (pl.multiple_of(jnp.maximum(i * tr - OFF, 0), OFF), 0))
    return pl.pallas_call(
        body, name="final_loss", grid=(Lp // tr,), in_specs=[row, row, tgt, vec],
        out_specs=[row, row, pl.BlockSpec((8, 128), lambda i: (0, 0)), vec],
        out_shape=[jax.ShapeDtypeStruct((Lp, D), f32), jax.ShapeDtypeStruct((Lp, D), bf16),
                   jax.ShapeDtypeStruct((8, 128), f32), jax.ShapeDtypeStruct((1, D), f32)],
        compiler_params=_params("arbitrary"),
    )(h1, ffn, tgt3.reshape(S, D), w)


def _ffn_in(n, w_gate, w_up, side=None):
    M, K = n.shape
    F = w_gate.shape[1]
    tm = _tile(M, 1408)
    tn = _tile(F, 512, 128)

    def body(a_ref, bg_ref, bu_ref, act_ref, pg_ref, pu_ref):
        a = a_ref[...]
        g = _mmb(a, bg_ref[...], NN)
        u = _mmb(a, bu_ref[...], NN)
        s = _sigmoid(g)
        gs = g * s
        act_ref[...] = (gs * u).astype(bf16)
        pg_ref[...] = (u * (s + gs * (1.0 - s))).astype(bf16)
        pu_ref[...] = gs.astype(bf16)

    wsp = pl.BlockSpec((K, tn), lambda i, j: (0, j))
    osp = pl.BlockSpec((tm, tn), lambda i, j: (i, j))
    out = _call(body, name="ffn_in", grid=(M // tm, F // tn),
                in_specs=[pl.BlockSpec((tm, K), lambda i, j: (i, 0)), wsp, wsp], out_specs=[osp] * 3,
                out_shape=[jax.ShapeDtypeStruct((M, F), bf16)] * 3, args=[n, w_gate, w_up],
                sem=("parallel", "parallel"), side=side)
    return out[0], out[1], out[2], out[3:]


def _ffn_dact(d, w_down, pg, pu):
    M, K = d.shape
    F = w_down.shape[0]
    tm = _tile(M, 1408)
    tn = _tile(F, 512, 128)

    def body(d_ref, w_ref, pg_ref, pu_ref, dg_ref, du_ref):
        da = _mmb(d_ref[...], w_ref[...], NT)
        dg_ref[...] = (da * pg_ref[...].astype(f32)).astype(bf16)
        du_ref[...] = (da * pu_ref[...].astype(f32)).astype(bf16)

    osp = pl.BlockSpec((tm, tn), lambda i, j: (i, j))
    return pl.pallas_call(
        body, name="ffn_dact", grid=(M // tm, F // tn),
        in_specs=[pl.BlockSpec((tm, K), lambda i, j: (i, 0)), pl.BlockSpec((tn, K), lambda i, j: (j, 0)), osp, osp],
        out_specs=[osp, osp], out_shape=[jax.ShapeDtypeStruct((M, F), bf16)] * 2,
        compiler_params=_params("parallel", "parallel"),
    )(d, w_down, pg, pu)


def _ffn_dn(dg, du, w_gate, w_up):
    M, F = dg.shape
    D = w_gate.shape[0]
    tm = _tile(M, 688)
    tn = _tile(D, 256, 128)

    def body(dg_ref, du_ref, wg_ref, wu_ref, o_ref):
        o_ref[...] = _mmb(dg_ref[...], wg_ref[...], NT) + _mmb(du_ref[...], wu_ref[...], NT)

    asp = pl.BlockSpec((tm, F), lambda i, j: (i, 0))
    wsp = pl.BlockSpec((tn, F), lambda i, j: (j, 0))
    return pl.pallas_call(
        body, name="d_n2", grid=(M // tm, D // tn), in_specs=[asp, asp, wsp, wsp],
        out_specs=pl.BlockSpec((tm, tn), lambda i, j: (i, j)), out_shape=jax.ShapeDtypeStruct((M, D), f32),
        compiler_params=_params("parallel", "parallel"),
    )(dg, du, w_gate, w_up)


def _gates(psm, w2p, gate_b, alog, dtb):
    Lp = psm.shape[0]
    tr = _tile(Lp, 256)

    def body(p_ref, w_ref, b_ref, a_ref, t_ref, gb_ref, la_ref):
        i = pl.program_id(0)
        psm_ = p_ref[...]
        lane = lax.broadcasted_iota(jnp.int32, psm_.shape, 1)
        rowi = lax.broadcasted_iota(jnp.int32, (tr, 1), 0) + i * tr
        g = -jnp.exp(a_ref[...]) * _softplus(psm_ + t_ref[...])
        beta = _sigmoid(psm_)
        gb = jnp.where(lane < GDN_H, g, jnp.where(lane < 2 * GDN_H, beta, 0.0))
        gb_ref[...] = gb * (rowi >= PAD).astype(f32)
        logit = _mmb(psm_, w_ref[...], NN) + b_ref[...]
        la_ref[...] = _log_sigmoid(logit) * (1.0 / GATE_NORMALIZER)

    row = pl.BlockSpec((tr, SM_W), lambda i: (i, 0))
    return pl.pallas_call(
        body, name="gates", grid=(Lp // tr,),
        in_specs=[row, pl.BlockSpec((SM_W, GLA_QK), lambda i: (0, 0)), pl.BlockSpec((1, GLA_QK), lambda i: (0, 0)),
                  pl.BlockSpec((1, SM_W), lambda i: (0, 0)), pl.BlockSpec((1, SM_W), lambda i: (0, 0))],
        out_specs=[row, pl.BlockSpec((tr, GLA_QK), lambda i: (i, 0))],
        out_shape=[jax.ShapeDtypeStruct((Lp, SM_W), f32), jax.ShapeDtypeStruct((Lp, GLA_QK), f32)],
        compiler_params=_params("parallel"),
    )(psm, w2p, gate_b, alog, dtb)


def _gates_bwd(psm, w2p, gate_b, alog, dtb, dgb, dla):
    Lp = psm.shape[0]
    tr = _tile(Lp, 256)

    def body(p_ref, w_ref, b_ref, a_ref, t_ref, dgb_ref, dla_ref, dp_ref, gw_ref, gb_ref, ga_ref, gt_ref):
        i = pl.program_id(0)
        psm_ = p_ref[...]
        lane = lax.broadcasted_iota(jnp.int32, psm_.shape, 1)
        rowi = lax.broadcasted_iota(jnp.int32, (tr, 1), 0) + i * tr
        d = dgb_ref[...] * (rowi >= PAD).astype(f32)
        ea = jnp.exp(a_ref[...])
        z = psm_ + t_ref[...]
        is_g = lane < GDN_H
        dz = jnp.where(is_g, -ea * _sigmoid(z) * d, 0.0)
        dalog = jnp.where(is_g, -ea * _softplus(z) * d, 0.0)
        beta = _sigmoid(psm_)
        dbeta = jnp.where(jnp.logical_and(lane >= GDN_H, lane < 2 * GDN_H), beta * (1.0 - beta) * d, 0.0)
        logit = _mmb(psm_, w_ref[...], NN) + b_ref[...]
        dlogit = dla_ref[...] * (_sigmoid(-logit) * (1.0 / GATE_NORMALIZER))
        dlr = _mmb(dlogit, w_ref[...], NT)
        dp_ref[...] = (dz + dbeta + dlr).astype(bf16)
        gw = _mmb(psm_, dlogit, TN)
        gb = jnp.sum(dlogit, axis=0, keepdims=True)
        ga = jnp.sum(dalog, axis=0, keepdims=True)
        gt = jnp.sum(dz, axis=0, keepdims=True)

        @pl.when(i == 0)
        def _():
            gw_ref[...] = gw
            gb_ref[...] = gb
            ga_ref[...] = ga
            gt_ref[...] = gt

        @pl.when(i > 0)
        def _():
            gw_ref[...] += gw
            gb_ref[...] += gb
            ga_ref[...] += ga
            gt_ref[...] += gt

    row = pl.BlockSpec((tr, SM_W), lambda i: (i, 0))
    wsp = pl.BlockSpec((SM_W, GLA_QK), lambda i: (0, 0))
    bsp = pl.BlockSpec((1, GLA_QK), lambda i: (0, 0))
    vsp = pl.BlockSpec((1, SM_W), lambda i: (0, 0))
    return pl.pallas_call(
        body, name="gates_bwd", grid=(Lp // tr,),
        in_specs=[row, wsp, bsp, vsp, vsp, row, pl.BlockSpec((tr, GLA_QK), lambda i: (i, 0))],
        out_specs=[row, wsp, bsp, vsp, vsp],
        out_shape=[jax.ShapeDtypeStruct((Lp, SM_W), bf16), jax.ShapeDtypeStruct((SM_W, GLA_QK), f32),
                   jax.ShapeDtypeStruct((1, GLA_QK), f32), jax.ShapeDtypeStruct((1, SM_W), f32),
                   jax.ShapeDtypeStruct((1, SM_W), f32)],
        compiler_params=_params("arbitrary"),
    )(psm, w2p, gate_b, alog, dtb, dgb, dla)


def _conv_pre(x_ext, w, n):
    rows = x_ext.shape[0]
    y = x_ext * w[CONV_K - 1:CONV_K, :]
    for s in range(1, CONV_K):
        y = y + pltpu.roll(x_ext, s, 0) * w[CONV_K - 1 - s:CONV_K - s, :]
    return y[rows - n:, :]


def _conv(proj, cw, side=None):
    Lp = proj.shape[0]
    W = cw.shape[1]
    tr = _tile(Lp, 256, 64)
    tc = W

    def body(h_ref, x_ref, w_ref, o_ref):
        i = pl.program_id(1)
        halo = jnp.where(i == 0, 0.0, h_ref[...])
        x_ext = jnp.concatenate([halo, x_ref[...]], axis=0)
        o_ref[...] = _silu(_conv_pre(x_ext, w_ref[...], tr))

    def window(rows, first_row):
        return pl.BlockSpec((pl.Element(rows), pl.Element(tc)),
                            lambda j, i: (pl.multiple_of(first_row(i), 8), pl.multiple_of(C_QKV + j * tc, LANES)))

    out = _call(
        body, name="conv", grid=(W // tc, Lp // tr),
        in_specs=[window(8, lambda i: jnp.maximum(i * tr - 8, 0)), window(tr, lambda i: i * tr),
                  pl.BlockSpec((CONV_K, tc), lambda j, i: (0, j))],
        out_specs=[pl.BlockSpec((tr, tc), lambda j, i: (i, j))],
        out_shape=[jax.ShapeDtypeStruct((Lp, W), f32)], args=[proj, proj, cw],
        sem=("parallel", "parallel"), side=side)
    return out[0] if side is None else (out[0], out[1:])


def _conv_bwd(proj, cw, dy, dproj, side=None):
    Lp = proj.shape[0]
    W = cw.shape[1]
    tr = _tile(Lp, 256, 64)
    tc = _tile(W, 1024, 128)
    c0 = C_QKV // tc
    nr = Lp // tr
    last8 = Lp // 8 - 1

    def body(xp_ref, x_ref, xn_ref, w_ref, d_ref, dn_ref, dproj_ref, o_ref, gw_ref):
        del dproj_ref
        i = pl.program_id(1)
        w = w_ref[...]
        xp = jnp.where(i == 0, 0.0, xp_ref[...])
        x_ext = jnp.concatenate([xp, x_ref[...], xn_ref[...]], axis=0)
        n = tr + 8
        pre = _conv_pre(x_ext, w, n)
        dn = jnp.where(i == nr - 1, 0.0, dn_ref[...])
        dpre = jnp.concatenate([d_ref[...], dn], axis=0) * _dsilu(pre)
        dx = dpre * w[CONV_K - 1:CONV_K, :]
        for s in range(1, CONV_K):
            dx = dx + pltpu.roll(dpre, n - s, 0) * w[CONV_K - 1 - s:CONV_K - s, :]
        o_ref[...] = dx[:tr, :].astype(bf16)
        dp = dpre[:tr, :]
        rows = []
        for k in range(CONV_K):
            xs = x_ext if k == CONV_K - 1 else pltpu.roll(x_ext, CONV_K - 1 - k, 0)
            rows.append(jnp.sum(dp * xs[8:8 + tr, :], axis=0, keepdims=True))
        gw = jnp.concatenate(rows, axis=0)

        @pl.when(i == 0)
        def _():
            gw_ref[...] = gw

        @pl.when(i > 0)
        def _():
            gw_ref[...] += gw

    cur = pl.BlockSpec((tr, tc), lambda j, i: (i, j))
    nxt = pl.BlockSpec((8, tc), lambda j, i: (jnp.minimum((i + 1) * (tr // 8), last8), j))
    pcur = pl.BlockSpec((tr, tc), lambda j, i: (i, j + c0))
    pprev = pl.BlockSpec((8, tc), lambda j, i: (jnp.maximum(i * (tr // 8) - 1, 0), j + c0))
    pnext = pl.BlockSpec((8, tc), lambda j, i: (jnp.minimum((i + 1) * (tr // 8), last8), j + c0))
    wsp = pl.BlockSpec((CONV_K, tc), lambda j, i: (0, j))
    out = _call(
        body, name="conv_bwd", grid=(W // tc, nr),
        in_specs=[pprev, pcur, pnext, wsp, cur, nxt, _ANY], out_specs=[pcur, wsp],
        out_shape=[jax.ShapeDtypeStruct(dproj.shape, dproj.dtype), jax.ShapeDtypeStruct((CONV_K, W), f32)],
        aliases={6: 0}, args=[proj, proj, proj, cw, dy, dy, dproj], sem=("parallel", "arbitrary"), side=side)
    return out[0], out[1], out[2:]


GDN_FWD_GROUP = 3


def _gdn_group(Lp, most):
    n = Lp // GDN_C
    return next(g for g in range(most, 0, -1) if n % g == 0)


def _gdn_heads(x_ref, gb_ref, group):
    qs, ks, vs, bs, gs = [], [], [], [], []
    for chunk in range(group):
        r = slice(chunk * GDN_C, (chunk + 1) * GDN_C)
        gbv = gb_ref[r, :]
        for h in range(GDN_H):
            qs.append(x_ref[r, Q0 + h * GDN_DK:Q0 + (h + 1) * GDN_DK])
            ks.append(x_ref[r, K0 + h * GDN_DK:K0 + (h + 1) * GDN_DK])
            vs.append(x_ref[r, V0 + h * GDN_DV:V0 + (h + 1) * GDN_DV])
            bs.append(gbv[:, GDN_H + h:GDN_H + h + 1])
            gs.append(gbv[:, h:h + 1])
    return qs, ks, vs, bs, gs


def _gdn_fwd(qkvc, gb, side=None):
    Lp = qkvc.shape[0]
    group = _gdn_group(Lp, GDN_FWD_GROUP)
    rows = group * GDN_C
    steps = Lp // rows
    R = range(GDN_H)

    def body(x_ref, gb_ref, o_ref, sall_ref, pall_ref, s_scr):
        @pl.when(pl.program_id(0) == 0)
        def _():
            s_scr[...] = jnp.zeros_like(s_scr)

        S2, o, p, entering = _gdn_chunk([s_scr[h] for h in R], *_gdn_heads(x_ref, gb_ref, group))
        for h in R:
            s_scr[h] = S2[h]
        for chunk in range(group):
            for h in R:
                i = chunk * GDN_H + h
                o_ref[chunk * GDN_C:(chunk + 1) * GDN_C, h * GDN_DV:(h + 1) * GDN_DV] = o[i]
                pall_ref[chunk, h] = p[i]
                sall_ref[chunk, h] = entering[i]

    out = _call(
        body, name="gdn_fwd", grid=(steps,),
        in_specs=[pl.BlockSpec((rows, QKV_W), lambda n: (n, 0)), pl.BlockSpec((rows, SM_W), lambda n: (n, 0))],
        out_specs=[pl.BlockSpec((rows, GDN_V), lambda n: (n, 0)),
                   pl.BlockSpec((group, GDN_H, GDN_DK, GDN_DV), lambda n: (n, 0, 0, 0)),
                   pl.BlockSpec((group, GDN_H, GDN_C, GDN_C), lambda n: (n, 0, 0, 0))],
        out_shape=[jax.ShapeDtypeStruct((Lp, GDN_V), f32),
                   jax.ShapeDtypeStruct((Lp // GDN_C, GDN_H, GDN_DK, GDN_DV), f32),
                   jax.ShapeDtypeStruct((Lp // GDN_C, GDN_H, GDN_C, GDN_C), f32)],
        scratch_shapes=[pltpu.VMEM((GDN_H, GDN_DK, GDN_DV), f32)], args=[qkvc, gb], sem=("arbitrary",), side=side)
    return out[0], out[1], out[2], out[3:]


def _gdn_bwd(qkvc, gb, sall, pall, do, side=None):
    Lp = qkvc.shape[0]
    group = 1
    rows = group * GDN_C
    steps = Lp // rows
    R = range(GDN_H)

    def body(x_ref, gb_ref, sall_ref, pall_ref, do_ref, dx_ref, dgb_ref, ds_scr):
        @pl.when(pl.program_id(0) == 0)
        def _():
            ds_scr[...] = jnp.zeros_like(ds_scr)

        lane = lax.broadcasted_iota(jnp.int32, (GDN_C, SM_W), 1)
        ps = [pall_ref[chunk, h] for chunk in range(group) for h in R]
        _, vjp = jax.vjp(lambda *a: _gdn_chunk(*a, Ps=ps)[:2],
                         [sall_ref[0, h] for h in R], *_gdn_heads(x_ref, gb_ref, group))
        do = [do_ref[chunk * GDN_C:(chunk + 1) * GDN_C, h * GDN_DV:(h + 1) * GDN_DV]
              for chunk in range(group) for h in R]
        dS, dq, dk, dv, dbeta, dg = vjp(([ds_scr[h] for h in R], do))
        for h in R:
            ds_scr[h] = dS[h]
        for chunk in range(group):
            r = slice(chunk * GDN_C, (chunk + 1) * GDN_C)
            acc = jnp.zeros((GDN_C, SM_W), f32)
            for h in R:
                i = chunk * GDN_H + h
                dx_ref[r, Q0 + h * GDN_DK:Q0 + (h + 1) * GDN_DK] = dq[i]
                dx_ref[r, K0 + h * GDN_DK:K0 + (h + 1) * GDN_DK] = dk[i]
                dx_ref[r, V0 + h * GDN_DV:V0 + (h + 1) * GDN_DV] = dv[i]
                acc = acc + jnp.where(lane == h, dg[i], 0.0) + jnp.where(lane == GDN_H + h, dbeta[i], 0.0)
            dgb_ref[r, :] = acc

    rev = lambda n: (steps - 1 - n, 0)
    out = _call(
        body, name="gdn_bwd", grid=(steps,),
        in_specs=[pl.BlockSpec((rows, QKV_W), rev), pl.BlockSpec((rows, SM_W), rev),
                  pl.BlockSpec((1, GDN_H, GDN_DK, GDN_DV), lambda n: (steps - 1 - n, 0, 0, 0)),
                  pl.BlockSpec((group, GDN_H, GDN_C, GDN_C), lambda n: (steps - 1 - n, 0, 0, 0)),
                  pl.BlockSpec((rows, GDN_V), rev)],
        out_specs=[pl.BlockSpec((rows, QKV_W), rev), pl.BlockSpec((rows, SM_W), rev)],
        out_shape=[jax.ShapeDtypeStruct((Lp, QKV_W), f32), jax.ShapeDtypeStruct((Lp, SM_W), f32)],
        scratch_shapes=[pltpu.VMEM((GDN_H, GDN_DK, GDN_DV), f32)], args=[qkvc, gb, sall, pall, do],
        sem=("arbitrary",), side=side)
    return out[0], out[1], out[2:]


GLA_BLOCK = 64


def _gla_group(Lp):
    nb = Lp // GLA_BLOCK
    return next(g for g in (3, 2, 1) if nb % g == 0)


def _gla_slices(h):
    sq = slice(h * GLA_DK, (h + 1) * GLA_DK)
    sk = slice(GLA_QK + h * GLA_DK, GLA_QK + (h + 1) * GLA_DK)
    sv = slice(2 * GLA_QK + h * GLA_DV, 2 * GLA_QK + (h + 1) * GLA_DV)
    return sq, sk, sv


def _gla_heads(x_ref, la_ref, group):
    qs, ks, vs, ls = [], [], [], []
    for blk in range(group):
        r = slice(blk * GLA_BLOCK, (blk + 1) * GLA_BLOCK)
        for h in range(GLA_H):
            sq, sk, sv = _gla_slices(h)
            qs.append(x_ref[r, sq])
            ks.append(x_ref[r, sk])
            vs.append(x_ref[r, sv])
            ls.append(la_ref[r, sq])
    return qs, ks, vs, ls


def _gla_fwd(proj, la):
    Lp = proj.shape[0]
    group = _gla_group(Lp)
    rows = group * GLA_BLOCK
    steps = Lp // rows
    R = range(GLA_H)

    def body(x_ref, la_ref, o_ref, sall_ref, s_scr):
        @pl.when(pl.program_id(0) == 0)
        def _():
            s_scr[...] = jnp.zeros_like(s_scr)

        Sts = [s_scr[h] for h in R]
        for h in R:
            sall_ref[0, h] = Sts[h]
        St2, o = _gla_blocks(Sts, *_gla_heads(x_ref, la_ref, group))
        for h in R:
            s_scr[h] = St2[h]
        for blk in range(group):
            for h in R:
                o_ref[blk * GLA_BLOCK:(blk + 1) * GLA_BLOCK, h * GLA_DV:(h + 1) * GLA_DV] = o[blk * GLA_H + h]

    return pl.pallas_call(
        body, name="gla_fwd", grid=(steps,),
        in_specs=[pl.BlockSpec((rows, G_W), lambda n: (n, C_G // G_W)),
                  pl.BlockSpec((rows, GLA_QK), lambda n: (n, 0))],
        out_specs=[pl.BlockSpec((rows, GLA_V), lambda n: (n, 0)),
                   pl.BlockSpec((1, GLA_H, GLA_DV, GLA_DK), lambda n: (n, 0, 0, 0))],
        out_shape=[jax.ShapeDtypeStruct((Lp, GLA_V), f32),
                   jax.ShapeDtypeStruct((steps, GLA_H, GLA_DV, GLA_DK), f32)],
        scratch_shapes=[pltpu.VMEM((GLA_H, GLA_DV, GLA_DK), f32)],
        compiler_params=_params("arbitrary"),
    )(proj, la)


def _gla_bwd(proj, la, sall, do, dproj, side=None):
    Lp = proj.shape[0]
    group = _gla_group(Lp)
    rows = group * GLA_BLOCK
    steps = Lp // rows
    R = range(GLA_H)

    def body(x_ref, la_ref, sall_ref, do_ref, dproj_ref, dx_ref, dla_ref, ds_scr):
        del dproj_ref

        @pl.when(pl.program_id(0) == 0)
        def _():
            ds_scr[...] = jnp.zeros_like(ds_scr)

        _, vjp = jax.vjp(_gla_blocks, [sall_ref[0, h] for h in R], *_gla_heads(x_ref, la_ref, group))
        do = [do_ref[blk * GLA_BLOCK:(blk + 1) * GLA_BLOCK, h * GLA_DV:(h + 1) * GLA_DV]
              for blk in range(group) for h in R]
        dS, dq, dk, dv, dl = vjp(([ds_scr[h] for h in R], do))
        for h in R:
            ds_scr[h] = dS[h]
        for blk in range(group):
            r = slice(blk * GLA_BLOCK, (blk + 1) * GLA_BLOCK)
            for h in R:
                sq, sk, sv = _gla_slices(h)
                i = blk * GLA_H + h
                dx_ref[r, sq] = dq[i].astype(bf16)
                dx_ref[r, sk] = dk[i].astype(bf16)
                dx_ref[r, sv] = dv[i].astype(bf16)
                dla_ref[r, sq] = dl[i]

    x_spec = pl.BlockSpec((rows, G_W), lambda n: (steps - 1 - n, C_G // G_W))
    rev = lambda n: (steps - 1 - n, 0)
    out = _call(
        body, name="gla_bwd", grid=(steps,),
        in_specs=[x_spec, pl.BlockSpec((rows, GLA_QK), rev),
                  pl.BlockSpec((1, GLA_H, GLA_DV, GLA_DK), lambda n: (steps - 1 - n, 0, 0, 0)),
                  pl.BlockSpec((rows, GLA_V), rev), _ANY],
        out_specs=[x_spec, pl.BlockSpec((rows, GLA_QK), rev)],
        out_shape=[jax.ShapeDtypeStruct(dproj.shape, dproj.dtype), jax.ShapeDtypeStruct((Lp, GLA_QK), f32)],
        aliases={4: 0}, scratch_shapes=[pltpu.VMEM((GLA_H, GLA_DV, GLA_DK), f32)],
        args=[proj, la, sall, do, dproj], sem=("arbitrary",), side=side)
    return out[0], out[1], out[2:]


def _gated_norm_fn(og, ol, zr, wg, wl):
    outs = []
    for h in range(GDN_H):
        s = slice(h * GDN_DV, (h + 1) * GDN_DV)
        outs.append(_rms(og[:, s])[0] * wg * _silu(zr[:, s]))
    for h in range(GLA_H):
        s = slice(h * GLA_DV, (h + 1) * GLA_DV)
        sr = slice(GDN_V + h * GLA_DV, GDN_V + (h + 1) * GLA_DV)
        outs.append(_rms(ol[:, s])[0] * wl * _silu(zr[:, sr]))
    return jnp.concatenate(outs, axis=-1)


def _gated_norm(og, ol, proj, wg, wl):
    Lp = og.shape[0]
    tr = _tile(Lp, 256)

    def body(og_ref, ol_ref, zr_ref, wg_ref, wl_ref, o_ref):
        o_ref[...] = _gated_norm_fn(og_ref[...], ol_ref[...], zr_ref[...], wg_ref[...], wl_ref[...]).astype(bf16)

    return pl.pallas_call(
        body, name="gated_norm", grid=(Lp // tr,),
        in_specs=[pl.BlockSpec((tr, GDN_V), lambda i: (i, 0)), pl.BlockSpec((tr, GLA_V), lambda i: (i, 0)),
                  pl.BlockSpec((tr, ZR_W), lambda i: (i, C_ZR // ZR_W)),
                  pl.BlockSpec((1, GDN_DV), lambda i: (0, 0)), pl.BlockSpec((1, GLA_DV), lambda i: (0, 0))],
        out_specs=pl.BlockSpec((tr, ZR_W), lambda i: (i, 0)),
        out_shape=jax.ShapeDtypeStruct((Lp, ZR_W), bf16),
        compiler_params=_params("parallel"),
    )(og, ol, proj, wg, wl)


def _gated_norm_bwd(og, ol, proj, wg, wl, dmix):
    Lp = og.shape[0]
    tr = _tile(Lp, 256)

    def body(og_ref, ol_ref, zr_ref, wg_ref, wl_ref, d_ref, dog_ref, dol_ref, dzr_ref, gwg_ref, gwl_ref):
        i = pl.program_id(0)
        _, vjp = jax.vjp(_gated_norm_fn, og_ref[...], ol_ref[...], zr_ref[...], wg_ref[...], wl_ref[...])
        dog, dol, dzr, gwg, gwl = vjp(d_ref[...])
        dog_ref[...] = dog
        dol_ref[...] = dol
        dzr_ref[...] = dzr.astype(bf16)

        @pl.when(i == 0)
        def _():
            gwg_ref[...] = gwg
            gwl_ref[...] = gwl

        @pl.when(i > 0)
        def _():
            gwg_ref[...] += gwg
            gwl_ref[...] += gwl

    og_spec = pl.BlockSpec((tr, GDN_V), lambda i: (i, 0))
    ol_spec = pl.BlockSpec((tr, GLA_V), lambda i: (i, 0))
    zr_spec = pl.BlockSpec((tr, ZR_W), lambda i: (i, C_ZR // ZR_W))
    vg = pl.BlockSpec((1, GDN_DV), lambda i: (0, 0))
    vl = pl.BlockSpec((1, GLA_DV), lambda i: (0, 0))
    return pl.pallas_call(
        body, name="gated_norm_bwd", grid=(Lp // tr,),
        in_specs=[og_spec, ol_spec, zr_spec, vg, vl, pl.BlockSpec((tr, ZR_W), lambda i: (i, 0))],
        out_specs=[og_spec, ol_spec, zr_spec, vg, vl],
        out_shape=[jax.ShapeDtypeStruct((Lp, GDN_V), f32), jax.ShapeDtypeStruct((Lp, GLA_V), f32),
                   jax.ShapeDtypeStruct((Lp, C_END), bf16),
                   jax.ShapeDtypeStruct((1, GDN_DV), f32), jax.ShapeDtypeStruct((1, GLA_DV), f32)],
        compiler_params=_params("arbitrary"),
    )(og, ol, proj, wg, wl, dmix)


def _adamw_rule(g_, w_, m_, v_):
    c1 = 1.0 - ADAM_B1 ** ADAM_STEP
    c2 = 1.0 - ADAM_B2 ** ADAM_STEP
    m2 = ADAM_B1 * m_ + (1.0 - ADAM_B1) * g_
    v2 = ADAM_B2 * v_ + (1.0 - ADAM_B2) * (g_ * g_)
    return -ADAM_LR * ((m2 / c1) / (jnp.sqrt(v2 / c2) + ADAM_EPS) + ADAM_WD * w_), m2, v2


def _adamw(g, w, m, v, name, copy_g=False):
    R, C = g.shape
    tr = _tile(R, 256, 8) if R % 8 == 0 and R > 256 else R

    def body(g_ref, w_ref, m_ref, v_ref, *o_refs):
        g_ = g_ref[...]
        if copy_g:
            o_refs[0][...] = g_
        d_ref, mo_ref, vo_ref = o_refs[-3:]
        d_ref[...], mo_ref[...], vo_ref[...] = _adamw_rule(g_, w_ref[...], m_ref[...], v_ref[...])

    blk = pl.BlockSpec((tr, C), lambda i: (i, 0))
    n_out = 4 if copy_g else 3
    return pl.pallas_call(
        body, name=name, grid=(R // tr,), in_specs=[blk] * 4, out_specs=[blk] * n_out,
        out_shape=[jax.ShapeDtypeStruct((R, C), f32)] * n_out,
        compiler_params=_params("parallel"),
    )(g, w, m, v)


def _adamw_transposed(gt, w, m, v, name):
    n, C, rb = gt.shape
    assert w.shape == (C, n * rb), (gt.shape, w.shape)

    def body(g_ref, w_ref, m_ref, v_ref, go_ref, d_ref, mo_ref, vo_ref):
        g_ = g_ref[...].astype(f32)
        go_ref[...] = g_
        d_ref[...], mo_ref[...], vo_ref[...] = _adamw_rule(g_, w_ref[...], m_ref[...], v_ref[...])

    blk = pl.BlockSpec((C, rb), lambda j: (0, j))
    return pl.pallas_call(
        body, name=name, grid=(n,), in_specs=[pl.BlockSpec((None, C, rb), lambda j: (j, 0, 0))] + [blk] * 3,
        out_specs=[blk] * 4, out_shape=[jax.ShapeDtypeStruct((C, n * rb), f32)] * 4,
        compiler_params=_params("parallel"),
    )(gt, w, m, v)


def _sum_slots(r, name):
    n, R, C = r.shape
    tr = _tile(R, 128, 16) if R % 16 == 0 and R > 128 else R

    def body(r_ref, o_ref):
        acc = r_ref[0].astype(f32)
        for s in range(1, n):
            acc = acc + r_ref[s].astype(f32)
        o_ref[...] = acc

    return pl.pallas_call(
        body, name=name, grid=(R // tr,),
        in_specs=[pl.BlockSpec((n, tr, C), lambda i: (0, i, 0))],
        out_specs=pl.BlockSpec((tr, C), lambda i: (i, 0)),
        out_shape=jax.ShapeDtypeStruct((R, C), f32),
        compiler_params=_params("parallel"),
    )(r)


SIBLING_PARTS = 8


class _Siblings:
    def __init__(self, arrays):
        self.arrays = list(arrays)
        self.n = len(self.arrays)
        self.parts = [next(p for p in range(SIBLING_PARTS, 0, -1) if a.shape[0] % (8 * p) == 0 or p == 1)
                      for a in self.arrays]
        total = sum(self.parts)
        self.out_shape = [jax.ShapeDtypeStruct((2,) + a.shape, a.dtype) for a in self.arrays]
        self.sems = [pltpu.SemaphoreType.DMA((total,)), pltpu.SemaphoreType.DMA((total,)),
                     pltpu.SemaphoreType.DMA((self.n,))]

    def hooks(self, ins, outs, send, recv, lsem):
        def copies():
            x, y, c = lax.axis_index("x"), lax.axis_index("y"), lax.axis_index("c")
            out, k = [], 0
            for a in range(self.n):
                out.append((pltpu.make_async_copy(ins[a], outs[a].at[c], lsem.at[a]), None))
                rows = self.arrays[a].shape[0] // self.parts[a]
                for part in range(self.parts[a]):
                    r = pl.ds(part * rows, rows)
                    mk = lambda dst, a=a, r=r, k=k: pltpu.make_async_remote_copy(
                        src_ref=ins[a].at[r], dst_ref=dst.at[r], send_sem=send.at[k], recv_sem=recv.at[k],
                        device_id=(x, y, 1 - c), device_id_type=MESH)
                    out.append((mk(outs[a].at[c]), mk(outs[a].at[1 - c])))
                    k += 1
            return out

        return _start_wait(copies)


def _comm_now(name, sides):
    total = sum(s.n for s in sides)

    def body(*refs):
        ins, outs, sems = refs[:total], refs[total:2 * total], refs[2 * total:]
        hooks, o = [], 0
        for i, s in enumerate(sides):
            hooks.append(s.hooks(ins[o:o + s.n], outs[o:o + s.n], *sems[3 * i:3 * i + 3]))
            o += s.n
        for start, _ in hooks:
            start()
        for _, wait in hooks:
            wait()

    out = pl.pallas_call(
        body, name=name, in_specs=[_ANY] * total, out_specs=[_ANY] * total,
        out_shape=[sh for s in sides for sh in s.out_shape], scratch_shapes=[sm for s in sides for sm in s.sems],
    )(*[a for s in sides for a in s.arrays])
    res, o = [], 0
    for s in sides:
        res.append(list(out[o:o + s.n]))
        o += s.n
    return res


def _cat_cols(g):
    return jnp.concatenate([g[i] for i in range(N_CHIP)], axis=-1)


def _row_slabs(a):
    return a.reshape(N_DEV, a.shape[0] // N_DEV, a.shape[1])


def _w_in_columns(g_wp, g_wsm):
    return jnp.concatenate([g_wp[:, C_QKV:C_END], g_wp[:, C_ZR:C_ZR + GDN_V], g_wsm[:, :SM_LR],
                            g_wp[:, C_G:C_G + G_W], g_wp[:, C_ZR + GDN_V:C_ZR + ZR_W],
                            g_wsm[:, SM_LR:SM_LR + GATE_RANK]], axis=1)


def _step(x, loss_target, p, meta, shard):
    _, S, D = x.shape
    alog_p = jnp.pad(p["gdn_a_log"], ((0, 0), (0, SM_W - GDN_H)))
    dtb_p = jnp.pad(p["gdn_dt_bias"], ((0, 0), (0, SM_W - GDN_H)))
    m64 = jnp.concatenate([jnp.zeros((PAD, D), f32), meta], axis=0)
    gate_b, gdn_norm_w, gla_norm_w = p["gla_gate_b"], p["gdn_norm_w"], p["gla_norm_w"]

    later = ("w_up", "w_out", "w_gate", "w_down")
    h0, n1, in_bf16, (w_in4, conv4, w24) = _embed_norm(
        x, m64, p["attn_norm_w"], to_bf16=[shard[k] for k in later],
        side=_Gather([shard["w_in"], shard["gdn_conv_w"], shard["gla_gate_w2"]]))
    shard = dict(shard, **dict(zip(later, in_bf16)))
    w_in, conv_w, w2 = _cat_cols(w_in4), _cat_cols(conv4), _cat_cols(w24)
    wp = jnp.concatenate([w_in[:, R_Z:R_AB], w_in[:, R_GR:R_LR], w_in[:, R_G:R_GR], w_in[:, R_QKV:R_Z]], axis=1)
    wsm = jnp.concatenate([w_in[:, R_AB:R_G], w_in[:, R_LR:R_END],
                           jnp.zeros((D, SM_W - SM_LR - GATE_RANK), w_in.dtype)], axis=1)
    w2p = jnp.pad(w2, ((SM_LR, SM_W - SM_LR - GATE_RANK), (0, 0)))
    proj, (w_up,) = _mm(n1, wp, "nn", "proj", side=_Gather([shard["w_up"]], by_columns=True))
    psm = _mm(n1, wsm, "nn", "proj_small")
    gb, la = _gates(psm, w2p, gate_b, alog_p, dtb_p)
    qkvc, (w_out4,) = _conv(proj, conv_w, side=_Gather([shard["w_out"]]))
    w_out = w_out4.reshape(-1, D)
    og, sall, pall, (w_gate,) = _gdn_fwd(qkvc, gb, side=_Gather([shard["w_gate"]], by_columns=True))
    ol, stall = _gla_fwd(proj, la)
    mixed = _gated_norm(og, ol, proj, gdn_norm_w, gla_norm_w)
    attn = _mm(mixed, w_out, "nn", "out_proj")
    h1, n2 = _add_norm(h0, attn, p["ffn_norm_w"])
    act, act_dgate, act_dup, (w_down4,) = _ffn_in(n2, w_gate, w_up, side=_Gather([shard["w_down"]]))
    w_down = w_down4.reshape(-1, D)
    ffn = _mm(act, w_down, "nn", "ffn_down", **WHOLE_K)
    dh2, dh2b, lossp, g_final = _final(h1, ffn, loss_target, p["final_norm_w"])

    g_down = _mm(act, dh2b, "tn", "g_w_down", out_dtype=bf16, **WHOLE_K_T)
    dg, du = _ffn_dact(dh2b, w_down, act_dgate, act_dup)
    g_gate = _mm(n2, dg, "tn", "g_w_gate", tm_cap=512, tn_cap=1408, tk_cap=2752, out_dtype=bf16, col_slabs=True)
    g_up = _mm(n2, du, "tn", "g_w_up", tm_cap=512, tn_cap=1408, tk_cap=2752, out_dtype=bf16, col_slabs=True)
    dn2 = _ffn_dn(dg, du, w_gate, w_up)
    dh1, dh1b, g_ffn_norm = _norm_bwd(dn2, h1, dh2, p["ffn_norm_w"])
    dmix = _mm(dh1b, w_out, "nt", "d_mixed")
    g_out = _mm(mixed, dh1b, "tn", "g_w_out", out_dtype=bf16, **WHOLE_K_T)
    dog, dol, dproj, g_gdn_norm, g_gla_norm = _gated_norm_bwd(og, ol, proj, gdn_norm_w, gla_norm_w, dmix)
    dproj, dla, (r_down,) = _gla_bwd(proj, la, stall, dol, dproj, side=_Exchange([_row_slabs(g_down)]))
    dqkvc, dgb, (r_gate, r_up, r_out, h_down) = _gdn_bwd(
        qkvc, gb, sall, pall, dog,
        side=_Sides(_Exchange([g_gate, g_up, _row_slabs(g_out)]), _Siblings([_sum_slots(r_down, "sum_w_down")])))
    dproj, g_conv, (h_gate,) = _conv_bwd(proj, conv_w, dqkvc, dproj,
                                         side=_Siblings([_sum_slots(r_gate, "sum_w_gate")]))
    dpsm, g_w2p, g_gate_b, g_alog, g_dtb = _gates_bwd(psm, w2p, gate_b, alog_p, dtb_p, dgb, dla)
    g_wsm = _mm(n1, dpsm, "tn", "g_w_in_small", out_dtype=bf16, **WHOLE_K_T)
    g_wp_a, (h_out,) = _mm(n1, dproj, "tn", "g_w_in_a", out_dtype=bf16, b_cols=(0, W_IN_SPLIT),
                           side=_Siblings([_sum_slots(r_out, "sum_w_out")]), **WHOLE_K_T)
    g_wp_b, r_in_a = _mm(n1, dproj, "tn", "g_w_in_b", out_dtype=bf16, b_cols=(W_IN_SPLIT, C_END - W_IN_SPLIT),
                         side=_Exchange([_row_slabs(g_wp_a), _row_slabs(g_wsm)]), **WHOLE_K_T)
    dn1, (r_in_b, h_up) = _mm(
        dproj, wp, "nt", "d_n1", plus=(dpsm, wsm),
        side=_Sides(_Exchange([_row_slabs(g_wp_b)]), _Siblings([_sum_slots(r_up, "sum_w_up")])), **WHOLE_K)
    s_wp = jnp.concatenate([_sum_slots(r_in_a[0], "sum_w_in_a"), _sum_slots(r_in_b, "sum_w_in_b")], axis=1)
    s_in = _w_in_columns(s_wp, _sum_slots(r_in_a[1], "sum_w_in_small"))
    in_by_chip = s_in.reshape(s_in.shape[0], N_CHIP, -1).transpose(1, 2, 0).astype(bf16)
    grad_x, g_meta, g_attn_norm, (h_in,) = _embed_norm_bwd(dn1, h0, dh1, p["attn_norm_w"], S,
                                                           side=_Exchange([], by_chip=[in_by_chip]))

    received = dict(w_in=h_in, w_gate=h_gate, w_up=h_up, w_out=h_out, w_down=h_down)
    small = dict(
        meta_tokens=g_meta, attn_norm_w=g_attn_norm, gdn_conv_w=g_conv, gdn_a_log=g_alog[:, :GDN_H],
        gdn_dt_bias=g_dtb[:, :GDN_H], gdn_norm_w=g_gdn_norm, gla_gate_w2=g_w2p[SM_LR:SM_LR + GATE_RANK],
        gla_gate_b=g_gate_b, gla_norm_w=g_gla_norm, ffn_norm_w=g_ffn_norm, final_norm_w=g_final)
    return lossp[0, 0], grad_x, received, small


_WEIGHTS = ("meta_tokens", "attn_norm_w", "w_in", "gdn_conv_w", "gdn_a_log", "gdn_dt_bias", "gdn_norm_w",
            "gla_gate_w2", "gla_gate_b", "gla_norm_w", "w_out", "ffn_norm_w", "w_gate", "w_up", "w_down",
            "final_norm_w")
_BIG_COLS = ("w_in", "w_gate", "w_up")
_BIG_ROWS = ("w_out", "w_down")
_SMALL_SHARDED = ("meta_tokens", "gdn_conv_w", "gla_gate_w2")


def kernel(x, meta_tokens, attn_norm_w, w_in, gdn_conv_w, gdn_a_log, gdn_dt_bias, gdn_norm_w, gla_gate_w2, gla_gate_b, gla_norm_w, w_out, ffn_norm_w, w_gate, w_up, w_down, final_norm_w, loss_target, m_meta_tokens, m_attn_norm_w, m_w_in, m_gdn_conv_w, m_gdn_a_log, m_gdn_dt_bias, m_gdn_norm_w, m_gla_gate_w2, m_gla_gate_b, m_gla_norm_w, m_w_out, m_ffn_norm_w, m_w_gate, m_w_up, m_w_down, m_final_norm_w, v_meta_tokens, v_attn_norm_w, v_w_in, v_gdn_conv_w, v_gdn_a_log, v_gdn_dt_bias, v_gdn_norm_w, v_gla_gate_w2, v_gla_gate_b, v_gla_norm_w, v_w_out, v_ffn_norm_w, v_w_gate, v_w_up, v_w_down, v_final_norm_w):
    w = dict(meta_tokens=meta_tokens, attn_norm_w=attn_norm_w, w_in=w_in, gdn_conv_w=gdn_conv_w, gdn_a_log=gdn_a_log,
             gdn_dt_bias=gdn_dt_bias, gdn_norm_w=gdn_norm_w, gla_gate_w2=gla_gate_w2, gla_gate_b=gla_gate_b,
             gla_norm_w=gla_norm_w, w_out=w_out, ffn_norm_w=ffn_norm_w, w_gate=w_gate, w_up=w_up, w_down=w_down,
             final_norm_w=final_norm_w)
    m = dict(meta_tokens=m_meta_tokens, attn_norm_w=m_attn_norm_w, w_in=m_w_in, gdn_conv_w=m_gdn_conv_w,
             gdn_a_log=m_gdn_a_log, gdn_dt_bias=m_gdn_dt_bias, gdn_norm_w=m_gdn_norm_w, gla_gate_w2=m_gla_gate_w2,
             gla_gate_b=m_gla_gate_b, gla_norm_w=m_gla_norm_w, w_out=m_w_out, ffn_norm_w=m_ffn_norm_w,
             w_gate=m_w_gate, w_up=m_w_up, w_down=m_w_down, final_norm_w=m_final_norm_w)
    v = dict(meta_tokens=v_meta_tokens, attn_norm_w=v_attn_norm_w, w_in=v_w_in, gdn_conv_w=v_gdn_conv_w,
             gdn_a_log=v_gdn_a_log, gdn_dt_bias=v_gdn_dt_bias, gdn_norm_w=v_gdn_norm_w, gla_gate_w2=v_gla_gate_w2,
             gla_gate_b=v_gla_gate_b, gla_norm_w=v_gla_norm_w, w_out=v_w_out, ffn_norm_w=v_ffn_norm_w,
             w_gate=v_w_gate, w_up=v_w_up, w_down=v_w_down, final_norm_w=v_final_norm_w)
    chip = 2 * lax.axis_index("x") + lax.axis_index("y")

    def two_d(a):
        return a.reshape(1, -1) if a.ndim == 1 else a.reshape(-1, a.shape[-1])

    w2d = {k: two_d(a) for k, a in w.items()}
    big = _BIG_COLS + _BIG_ROWS
    small = tuple(k for k in _WEIGHTS if k not in big)

    (meta4,), = _comm_now("gather_meta", [_Gather([w2d["meta_tokens"]])])
    shard = {k: w2d[k] for k in big + ("gdn_conv_w", "gla_gate_w2")}
    shard["w_in"] = w2d["w_in"].astype(bf16)
    lossp, grad_x, received, g = _step(x, loss_target, {k: w2d[k] for k in small}, _cat_cols(meta4), shard)
    loss = lax.psum(lossp, ("x", "y", "c"))

    sizes = [g[k].size for k in small]
    total = sum(sizes)
    rows = -(-total // 1024)
    rows += (-rows) % 8
    packed = jnp.concatenate([g[k].reshape(-1) for k in small] + [jnp.zeros((rows * 1024 - total,), f32)])
    (packed8,), = _comm_now("exchange_small", [_Exchange([], [packed.reshape(rows, 1024)])])
    red = {k: h.reshape(w2d[k].shape) for k, h in received.items() if k != "w_in"}
    psum_small = _sum_slots(packed8, "sum_small").reshape(-1)
    off = 0
    for k, n in zip(small, sizes):
        a = psum_small[off:off + n].reshape(g[k].shape)
        off += n
        if k in _SMALL_SHARDED:
            c = w2d[k].shape[1]
            a = lax.dynamic_slice_in_dim(a, chip * c, c, axis=1)
        red[k] = a

    grads, deltas, new_m, new_v = [], [], [], []
    for k in _WEIGHTS:
        shape = w[k].shape
        if k == "w_in":
            flip = lambda a: jnp.swapaxes(a, 1, 2).reshape(shape[2], shape[1])
            unflip = lambda a: jnp.swapaxes(a.reshape(shape[0], shape[2], shape[1]), 1, 2)
            out = _adamw_transposed(received[k], flip(w[k]), flip(m[k]), flip(v[k]), "adamw_" + k)
            gk, d, m2, v2 = [unflip(a) for a in out]
        elif k in big:
            gk, d, m2, v2 = [a.reshape(shape) for a in
                             _adamw(red[k], w2d[k], two_d(m[k]), two_d(v[k]), "adamw_" + k, copy_g=True)]
        else:
            gk = red[k].reshape(shape)
            d, m2, v2 = [a.reshape(shape) for a in _adamw(red[k], w2d[k], two_d(m[k]), two_d(v[k]), "adamw_" + k)]
        grads.append(gk)
        deltas.append(d)
        new_m.append(m2)
        new_v.append(v2)
    return (loss, grad_x, *grads, *deltas, *new_m, *new_v)
```

```python
import functools

import jax
import jax.numpy as jnp
from jax import lax
from jax.experimental import pallas as pl
from jax.experimental.pallas import tpu as pltpu

f32 = jnp.float32
bf16 = jnp.bfloat16
HIGH = lax.Precision.HIGH
MESH = pl.DeviceIdType.MESH

N_META = 16
CONV_K = 4
GDN_H, GDN_DK, GDN_DV, GDN_C = 8, 128, 128, 64
GLA_H, GLA_DK, GLA_DV, GLA_C = 4, 128, 256, 16
GATE_RANK = 16
GATE_NORMALIZER = 16.0
EPS = 1e-6
GDN_QK = GDN_H * GDN_DK
GDN_V = GDN_H * GDN_DV
GLA_QK = GLA_H * GLA_DK
GLA_V = GLA_H * GLA_DV
PAD = (-N_META) % GDN_C
OFF = PAD + N_META
ROWS = 64

R_QKV, R_Z, R_AB, R_G, R_GR, R_LR, R_END = 0, 3072, 4096, 4112, 6160, 7184, 7200
C_ZR, C_G, C_QKV, C_END = 0, 2048, 4096, 7168
W_IN_SPLIT = 3072
ZR_W = GDN_V + GLA_V
G_W = 2 * GLA_QK + GLA_V
QKV_W = 2 * GDN_QK + GDN_V
Q0, K0, V0 = 0, GDN_QK, 2 * GDN_QK
SM_W = 128
SM_LR = 2 * GDN_H

ADAM_LR, ADAM_B1, ADAM_B2, ADAM_EPS, ADAM_WD, ADAM_STEP = 0.001, 0.9, 0.999, 1e-08, 0.01, 10

VMEM_LIMIT_V7X = 56 * 1024 * 1024
LANES = 128
N_DEV = 8
N_CHIP = 4


def _params(*sem):
    return pltpu.CompilerParams(dimension_semantics=sem, vmem_limit_bytes=VMEM_LIMIT_V7X)


def _tile(n, cap, mult=16):
    best = None
    for d in range(mult, min(n, cap) + 1, mult):
        if n % d == 0:
            best = d
    assert best is not None, (n, cap, mult)
    return best


NN = ((1,), (0,))
NT = ((1,), (1,))
TN = ((0,), (0,))


def _dot(a, b, dims, prec=None):
    return lax.dot_general(a, b, (dims, ((), ())), precision=prec, preferred_element_type=f32)


def _mmb(a, b, dims):
    return _dot(a.astype(bf16), b.astype(bf16), dims)


def _sigmoid(x):
    return jax.nn.sigmoid(x)


def _silu(x):
    return x * _sigmoid(x)


def _dsilu(x):
    s = _sigmoid(x)
    return s * (1.0 + x * (1.0 - s))


def _log1p_exp_neg_abs(x):
    t = jnp.exp(-jnp.abs(x))
    u = 1.0 + t
    d = u - 1.0
    return jnp.where(d == 0.0, t, jnp.log(u) * (t / jnp.where(d == 0.0, 1.0, d)))


def _softplus(x):
    return jnp.maximum(x, 0.0) + _log1p_exp_neg_abs(x)


def _log_sigmoid(x):
    return jnp.minimum(x, 0.0) - _log1p_exp_neg_abs(x)


def _rms(x):
    r = lax.rsqrt(jnp.mean(x * x, axis=-1, keepdims=True) + EPS)
    return x * r, r


def _rms_bwd(dy, xh, r, w):
    t = dy * w
    return r * (t - xh * jnp.mean(t * xh, axis=-1, keepdims=True))


def _l2n(x):
    return x * lax.rsqrt(jnp.sum(x * x, axis=-1, keepdims=True) + EPS)


INV_LEAF = 8


def _same_block(C, b):
    sh = b.bit_length() - 1
    row = lax.broadcasted_iota(jnp.int32, (C, C), 0)
    col = lax.broadcasted_iota(jnp.int32, (C, C), 1)
    return lax.shift_right_logical(row, sh) == lax.shift_right_logical(col, sh)


def _tri_inv_impl(As):
    C = As[0].shape[0]
    R = range(len(As))
    row = lax.broadcasted_iota(jnp.int32, (C, C), 0)
    col = lax.broadcasted_iota(jnp.int32, (C, C), 1)
    eye = (row == col).astype(f32)
    b = INV_LEAF
    inner = _same_block(C, b)
    leaf = [jnp.where(inner, As[h], 0.0) for h in R]
    d = [eye - leaf[h] for h in R]
    pw = leaf
    n = 2
    while n < b:
        pw = [_dot(pw[h], pw[h], NN, HIGH) for h in R]
        d = [_dot(d[h], eye + pw[h], NN, HIGH) for h in R]
        n *= 2
    while b < C:
        outer = _same_block(C, 2 * b)
        level = jnp.logical_and(outer, jnp.logical_not(inner))
        ed = [_dot(jnp.where(level, As[h], 0.0), d[h], NN, HIGH) for h in R]
        d = [d[h] - _dot(d[h], ed[h], NN, HIGH) for h in R]
        inner = outer
        b *= 2
    return d


@jax.custom_vjp
def _tri_inv(As):
    return _tri_inv_impl(As)


def _tri_inv_fwd(As):
    d = _tri_inv_impl(As)
    return d, d


def _tri_inv_bwd(d, g):
    R = range(len(d))
    t = [_dot(d[h], g[h], TN, HIGH) for h in R]
    return ([-_dot(t[h], d[h], NT, HIGH) for h in R],)


_tri_inv.defvjp(_tri_inv_fwd, _tri_inv_bwd)


@jax.custom_vjp
def _tri_inv_known(As, Ps):
    del As
    return Ps


def _tri_inv_known_fwd(As, Ps):
    del As
    return Ps, Ps


def _tri_inv_known_bwd(d, g):
    return _tri_inv_bwd(d, g)[0], [jnp.zeros_like(x) for x in d]


_tri_inv_known.defvjp(_tri_inv_known_fwd, _tri_inv_known_bwd)


def _gdn_chunk(Ss, qrs, krs, vs, betas, gs, Ps=None):
    H = len(Ss)
    C, dk = qrs[0].shape
    R = range(len(qrs))
    row = lax.broadcasted_iota(jnp.int32, (C, C), 0)
    col = lax.broadcasted_iota(jnp.int32, (C, C), 1)
    causal = row >= col
    strict = row > col
    cf = causal.astype(f32)
    q = [_l2n(qrs[h]) * (dk ** -0.5) for h in R]
    k = [_l2n(krs[h]) for h in R]
    mc = [_rows_exact(cf, jnp.broadcast_to(gs[h], (C, C))) for h in R]
    gc = [mc[h][:, 0:1] for h in R]
    decay = [jnp.where(causal, jnp.exp(jnp.where(causal, mc[h] - mc[h].T, 0.0)), 0.0) for h in R]
    kb = [k[h] * betas[h] for h in R]
    a = [jnp.where(strict, _mmb(kb[h], k[h], NT) * decay[h], 0.0) for h in R]
    p = _tri_inv(a) if Ps is None else _tri_inv_known(a, Ps)
    egc = [jnp.exp(gc[h]) for h in R]
    u = [_mmb(p[h], vs[h] * betas[h], NN) for h in R]
    w = [_mmb(p[h], kb[h] * egc[h], NN) for h in R]
    qk = [jnp.where(causal, _mmb(q[h], k[h], NT) * decay[h], 0.0) for h in R]
    qe = [q[h] * egc[h] for h in R]
    gl = [gc[h][C - 1:C, :] for h in R]
    kd = [k[h] * jnp.exp(gl[h] - gc[h]) for h in R]
    egl = [jnp.exp(gl[h]) for h in R]
    S, o, entering = list(Ss), [], []
    for chunk in range(len(qrs) // H):
        idx = [chunk * H + h for h in range(H)]
        entering += S
        v_new = [u[i] - _mmb(w[i], S[h], NN) for h, i in enumerate(idx)]
        o += [_mmb(qe[i], S[h], NN) + _mmb(qk[i], v_new[h], NN) for h, i in enumerate(idx)]
        S = [S[h] * egl[i] + _mmb(kd[i], v_new[h], TN) for h, i in enumerate(idx)]
    return S, o, p, entering


def _rows_exact_impl(m01, x, dims):
    m = m01.astype(bf16)
    x1 = x.astype(bf16)
    r1 = x - x1.astype(f32)
    x2 = r1.astype(bf16)
    x3 = (r1 - x2.astype(f32)).astype(bf16)
    d = lambda y: _dot(m, y, dims)
    return d(x1) + (d(x2) + d(x3))


@jax.custom_vjp
def _rows_exact(m01, x):
    return _rows_exact_impl(m01, x, NN)


def _rows_exact_fwd(m01, x):
    return _rows_exact_impl(m01, x, NN), m01


def _rows_exact_bwd(m01, g):
    return jnp.zeros_like(m01), _rows_exact_impl(m01, g, TN)


_rows_exact.defvjp(_rows_exact_fwd, _rows_exact_bwd)


def _gla_blocks(Sts, qrs, ks, vs, las):
    H = len(Sts)
    n = len(qrs)
    C, dk = qrs[0].shape
    R = range(n)
    row = lax.broadcasted_iota(jnp.int32, (C, C), 0)
    col = lax.broadcasted_iota(jnp.int32, (C, C), 1)
    ri = lax.broadcasted_iota(jnp.int32, (C, dk), 0)
    q = [qrs[h] * (dk ** -0.5) for h in R]
    running = (row >= col).astype(f32)
    b = [_rows_exact(running, las[h]) for h in R]
    sc = [jnp.where(row == col, jnp.sum(q[h] * ks[h], axis=-1, keepdims=True), 0.0) for h in R]
    s = C // 2
    while s >= 1:
        sh = s.bit_length() - 1
        ref = lax.shift_left(lax.shift_right_logical(row, sh + 1), sh + 1) + (s - 1)
        pick = (col == ref).astype(f32)
        bref = [_rows_exact(pick, b[h]) for h in R]
        upper = (lax.shift_right_logical(ri, sh) & 1) == 1
        qt = [jnp.where(upper, q[h] * jnp.exp(jnp.where(upper, b[h] - bref[h], 0.0)), 0.0) for h in R]
        kt = [jnp.where(upper, 0.0, ks[h] * jnp.exp(jnp.where(upper, 0.0, bref[h] - b[h]))) for h in R]
        same = lax.shift_right_logical(row, sh + 1) == lax.shift_right_logical(col, sh + 1)
        sc = [sc[h] + jnp.where(same, _mmb(qt[h], kt[h], NT), 0.0) for h in R]
        s //= 2
    o = [_mmb(sc[h], vs[h], NN) for h in R]
    qe = [q[h] * jnp.exp(b[h]) for h in R]
    bl = [b[h][C - 1:C, :] for h in R]
    upd = [_mmb(vs[h], ks[h] * jnp.exp(bl[h] - b[h]), TN) for h in R]
    ebl = [jnp.exp(bl[h]) for h in R]
    St = list(Sts)
    for blk in range(n // H):
        for h in range(H):
            i = blk * H + h
            o[i] = o[i] + _mmb(qe[i], St[h], NT)
        St = [St[h] * ebl[blk * H + h] + upd[blk * H + h] for h in range(H)]
    return St, o


_ANY = pl.BlockSpec(memory_space=pl.ANY)


class _Gather:
    def __init__(self, arrays, by_columns=False):
        self.arrays = list(arrays)
        self.n = len(self.arrays)
        self.by_columns = by_columns
        if by_columns:
            assert all(a.ndim == 2 and a.shape[1] % LANES == 0 for a in self.arrays)
            self.out_shape = [jax.ShapeDtypeStruct((a.shape[0], N_CHIP * a.shape[1]), a.dtype) for a in self.arrays]
        else:
            self.out_shape = [jax.ShapeDtypeStruct((N_CHIP,) + a.shape, a.dtype) for a in self.arrays]
        self.sems = [pltpu.SemaphoreType.DMA((self.n, 3)), pltpu.SemaphoreType.DMA((self.n, 3)),
                     pltpu.SemaphoreType.DMA((self.n,))]

    def hooks(self, ins, outs, send, recv, lsem):
        def place(a, chip):
            if not self.by_columns:
                return outs[a].at[chip]
            cols = self.arrays[a].shape[1]
            return outs[a].at[:, pl.ds(pl.multiple_of(chip * cols, LANES), cols)]

        def copies():
            x, y, c = lax.axis_index("x"), lax.axis_index("y"), lax.axis_index("c")
            me = 2 * x + y
            out = []
            for a in range(self.n):
                out.append((pltpu.make_async_copy(ins[a], place(a, me), lsem.at[a]), None))
                for j, (px, py) in enumerate([(1 - x, y), (x, 1 - y), (1 - x, 1 - y)]):
                    mk = lambda dst, a=a, j=j, px=px, py=py: pltpu.make_async_remote_copy(
                        src_ref=ins[a], dst_ref=dst, send_sem=send.at[a, j], recv_sem=recv.at[a, j],
                        device_id=(px, py, c), device_id_type=MESH)
                    out.append((mk(place(a, me)), mk(place(a, 2 * px + py))))
            return out

        return _start_wait(copies)


class _Exchange:
    def __init__(self, slotted, shared=(), by_chip=()):
        self.arrays = list(slotted) + list(by_chip) + list(shared)
        self.ns, self.nc = len(slotted), len(by_chip)
        self.n = len(self.arrays)
        self.out_shape = [jax.ShapeDtypeStruct(a.shape, a.dtype) for a in slotted]
        self.out_shape += [jax.ShapeDtypeStruct((N_DEV,) + a.shape[1:], a.dtype) for a in by_chip]
        self.out_shape += [jax.ShapeDtypeStruct((N_DEV,) + b.shape, b.dtype) for b in shared]
        self.sems = [pltpu.SemaphoreType.DMA((self.n, N_DEV - 1)), pltpu.SemaphoreType.DMA((self.n, N_DEV - 1)),
                     pltpu.SemaphoreType.DMA((self.n,))]

    def hooks(self, ins, outs, send, recv, lsem):
        def copies():
            x, y, c = lax.axis_index("x"), lax.axis_index("y"), lax.axis_index("c")
            me = 4 * x + 2 * y + c

            def src(a, dev):
                tx, ty, tc = dev
                if a < self.ns:
                    return ins[a].at[4 * tx + 2 * ty + tc]
                return ins[a].at[2 * tx + ty] if a < self.ns + self.nc else ins[a]

            out = []
            for a in range(self.n):
                out.append((pltpu.make_async_copy(src(a, (x, y, c)), outs[a].at[me], lsem.at[a]), None))
                for o in range(1, N_DEV):
                    dev = (1 - x if o & 4 else x, 1 - y if o & 2 else y, 1 - c if o & 1 else c)
                    t = 4 * dev[0] + 2 * dev[1] + dev[2]
                    mk = lambda dst, a=a, o=o, dev=dev: pltpu.make_async_remote_copy(
                        src_ref=src(a, dev), dst_ref=dst, send_sem=send.at[a, o - 1], recv_sem=recv.at[a, o - 1],
                        device_id=dev, device_id_type=MESH)
                    out.append((mk(outs[a].at[me]), mk(outs[a].at[t])))
            return out

        return _start_wait(copies)


class _Sides:
    def __init__(self, *members):
        self.members = members
        self.arrays = [a for s in members for a in s.arrays]
        self.n = len(self.arrays)
        self.out_shape = [sh for s in members for sh in s.out_shape]
        self.sems = [sm for s in members for sm in s.sems]

    def hooks(self, ins, outs, *sems):
        hooks, o = [], 0
        for i, s in enumerate(self.members):
            hooks.append(s.hooks(ins[o:o + s.n], outs[o:o + s.n], *sems[3 * i:3 * i + 3]))
            o += s.n

        def start():
            for st, _ in hooks:
                st()

        def wait():
            for _, wt in hooks:
                wt()

        return start, wait


def _start_wait(copies):
    def start():
        for s, _ in copies():
            s.start()

    def wait():
        for s, w in copies():
            (s if w is None else w).wait()

    return start, wait


def _call(body, *, name, grid, in_specs, out_specs, out_shape, args, sem, scratch_shapes=(), aliases=None, side=None):
    in_specs, out_specs, out_shape, args = list(in_specs), list(out_specs), list(out_shape), list(args)
    scratch_shapes = list(scratch_shapes)
    aliases = aliases or {}
    if side is None:
        return pl.pallas_call(
            body, name=name, grid=grid, in_specs=in_specs, out_specs=out_specs, out_shape=out_shape,
            scratch_shapes=scratch_shapes, input_output_aliases=aliases, compiler_params=_params(*sem))(*args)
    n_in, n_out, n_scr, ns = len(in_specs), len(out_specs), len(scratch_shapes), side.n

    def full_body(*refs):
        ins, refs = refs[:n_in], refs[n_in:]
        s_in, refs = refs[:ns], refs[ns:]
        outs, refs = refs[:n_out], refs[n_out:]
        s_out, refs = refs[:ns], refs[ns:]
        scr, sems = refs[:n_scr], refs[n_scr:]
        start, wait = side.hooks(s_in, s_out, *sems)
        ids = [pl.program_id(d) for d in range(len(grid))]
        first = functools.reduce(jnp.logical_and, [i == 0 for i in ids])
        last = functools.reduce(jnp.logical_and, [i == g - 1 for i, g in zip(ids, grid)])
        pl.when(first)(start)
        body(*ins, *outs, *scr)
        pl.when(last)(wait)

    return pl.pallas_call(
        full_body, name=name, grid=grid, in_specs=in_specs + [_ANY] * ns, out_specs=out_specs + [_ANY] * ns,
        out_shape=out_shape + side.out_shape, scratch_shapes=scratch_shapes + side.sems,
        input_output_aliases=aliases, compiler_params=_params(*(["arbitrary"] * len(grid))))(*args, *side.arrays)


WHOLE_K = dict(tm_cap=688, tn_cap=512, tk_cap=1 << 20)
WHOLE_K_T = dict(tm_cap=512, tn_cap=512, tk_cap=1 << 20)

def _mm(a, b, mode, name, *, tm_cap=1408, tn_cap=1024, tk_cap=2048, out_dtype=f32, acc_in=None, side=None,
        col_slabs=False, b_cols=None, plus=None):
    if mode == "nn":
        (M, K), (K2, N) = a.shape, b.shape
    elif mode == "nt":
        (M, K), (N, K2) = a.shape, b.shape
    else:
        (K, M), (K2, N) = a.shape, b.shape
    assert K == K2, (name, a.shape, b.shape)
    b_first = 0
    if b_cols is not None:
        assert mode != "nt"
        b_first, N = b_cols
    tm = _tile(M // 2 if col_slabs else M, tm_cap)
    tn = _tile(N // N_CHIP if col_slabs else N, tn_cap, 128)
    tk = _tile(K, tk_cap, 128 if K % 128 == 0 else 16)
    nk = K // tk
    dims = {"nn": NN, "nt": NT, "tn": TN}[mode]
    use_scratch = nk > 1 and out_dtype != f32

    def body(*refs):
        a_ref, b_ref, *rest = refs
        c_ref = rest.pop(0) if acc_in is not None else None
        p = _mmb(a_ref[...], b_ref[...], dims)
        if plus is not None:
            p = p + _mmb(rest.pop(0)[...], rest.pop(0)[...], dims)
        o_ref, *scr = rest
        if nk == 1:
            if c_ref is not None:
                p = p + c_ref[...]
            o_ref[...] = p.astype(out_dtype)
            return
        k = pl.program_id(2)
        acc = scr[0] if use_scratch else o_ref

        @pl.when(k == 0)
        def _():
            acc[...] = p if c_ref is None else p + c_ref[...]

        @pl.when(k > 0)
        def _():
            acc[...] += p

        if use_scratch:
            @pl.when(k == nk - 1)
            def _():
                o_ref[...] = acc[...].astype(out_dtype)

    if mode == "tn":
        a_spec = pl.BlockSpec((tk, tm), lambda i, j, k: (k, i))
    else:
        a_spec = pl.BlockSpec((tm, tk), lambda i, j, k: (i, k))
    if mode == "nt":
        b_spec = pl.BlockSpec((tn, tk), lambda i, j, k: (j, k))
    else:
        assert b_first % tn == 0, (name, b_first, tn)
        b_spec = pl.BlockSpec((tk, tn), lambda i, j, k: (k, j + b_first // tn))
    if col_slabs:
        assert acc_in is None
        ni, nj = M // 2 // tm, N // N_CHIP // tn
        o_spec = pl.BlockSpec((None, tm, tn), lambda i, j, k: (2 * (j // nj) + i // ni, i % ni, j % nj))
        o_shape = jax.ShapeDtypeStruct((N_DEV, M // 2, N // N_CHIP), out_dtype)
    else:
        o_spec = pl.BlockSpec((tm, tn), lambda i, j, k: (i, j))
        o_shape = jax.ShapeDtypeStruct((M, N), out_dtype)
    in_specs = [a_spec, b_spec]
    args = [a, b]
    if acc_in is not None:
        in_specs.append(o_spec)
        args.append(acc_in)
    if plus is not None:
        assert mode == "nt" and nk == 1, (name, mode, nk)
        k2 = plus[0].shape[1]
        assert plus[0].shape == (M, k2) and plus[1].shape == (N, k2), (name, plus[0].shape, plus[1].shape)
        in_specs += [pl.BlockSpec((tm, k2), lambda i, j, k: (i, 0)), pl.BlockSpec((tn, k2), lambda i, j, k: (j, 0))]
        args += list(plus)
    out = _call(body, name=name, grid=(M // tm, N // tn, nk), in_specs=in_specs, out_specs=[o_spec],
                out_shape=[o_shape], args=args,
                scratch_shapes=[pltpu.VMEM((tm, tn), f32)] if use_scratch else [],
                sem=("parallel", "parallel", "arbitrary"), side=side)
    return out[0] if side is None else (out[0], out[1:])


def _embed_norm(x3, m64, w, to_bf16=(), side=None):
    _, S, D = x3.shape
    Lp = OFF + S
    steps = Lp // ROWS
    nc = len(to_bf16)

    def body(x_ref, m_ref, w_ref, *rest):
        c_in, (h_ref, n_ref), c_out = rest[:nc], rest[nc:nc + 2], rest[nc + 2:]
        i = pl.program_id(0)
        h = jnp.where(i == 0, m_ref[...], x_ref[...])
        h_ref[...] = h
        xh, _ = _rms(h)
        n_ref[...] = (xh * w_ref[...]).astype(bf16)
        for ci, co in zip(c_in, c_out):
            co[...] = ci[...].astype(bf16)

    def piece(a):
        R, C = a.shape
        rb = next(d for d in range(16, R + 1, 16) if R % d == 0 and d * steps >= R)
        return pl.BlockSpec((rb, C), lambda i: (jnp.minimum(i, R // rb - 1), 0))

    row = pl.BlockSpec((ROWS, D), lambda i: (i, 0))
    pieces = [piece(a) for a in to_bf16]
    out = _call(
        body, name="embed_norm", grid=(steps,),
        in_specs=[pl.BlockSpec((None, ROWS, D), lambda i: (0, jnp.maximum(i - 1, 0), 0)),
                  pl.BlockSpec((ROWS, D), lambda i: (0, 0)),
                  pl.BlockSpec((1, D), lambda i: (0, 0))] + pieces,
        out_specs=[row, row] + pieces,
        out_shape=[jax.ShapeDtypeStruct((Lp, D), f32), jax.ShapeDtypeStruct((Lp, D), bf16)]
        + [jax.ShapeDtypeStruct(a.shape, bf16) for a in to_bf16],
        args=[x3, m64, w, *to_bf16], sem=("arbitrary",), side=side)
    return out[0], out[1], out[2:2 + nc], out[2 + nc:]


def _add_norm(h, d, w):
    Lp, D = h.shape
    tr = _tile(Lp, 256)

    def body(h_ref, d_ref, w_ref, o_ref, n_ref):
        h1 = h_ref[...] + d_ref[...]
        o_ref[...] = h1
        xh, _ = _rms(h1)
        n_ref[...] = (xh * w_ref[...]).astype(bf16)

    row = pl.BlockSpec((tr, D), lambda i: (i, 0))
    return pl.pallas_call(
        body, name="add_norm", grid=(Lp // tr,),
        in_specs=[row, row, pl.BlockSpec((1, D), lambda i: (0, 0))], out_specs=[row, row],
        out_shape=[jax.ShapeDtypeStruct((Lp, D), f32), jax.ShapeDtypeStruct((Lp, D), bf16)],
        compiler_params=_params("parallel"),
    )(h, d, w)


NORM_BWD_READ_AHEAD = 3


def _norm_bwd(dn, h, dh, w):
    Lp, D = h.shape
    tr = _tile(Lp, 256)
    n = Lp // tr
    nbuf = min(NORM_BWD_READ_AHEAD, n)

    def body(dn_hbm, h_hbm, dh_hbm, w_ref, o_hbm, ob_hbm, gw_ref, ibuf, obuf, obbuf, isem, osem):
        srcs = (dn_hbm, h_hbm, dh_hbm)

        def rows(i):
            return pl.ds(pl.multiple_of(i * tr, 8), tr)

        def read(s, i, slot):
            return pltpu.make_async_copy(srcs[s].at[rows(i)], ibuf.at[s, slot], isem.at[s, slot])

        def writes(i, slot):
            return (pltpu.make_async_copy(obuf.at[slot], o_hbm.at[rows(i)], osem.at[0, slot]),
                    pltpu.make_async_copy(obbuf.at[slot], ob_hbm.at[rows(i)], osem.at[1, slot]))

        for j in range(nbuf):
            for s in range(3):
                read(s, j, j).start()
        gw_ref[...] = jnp.zeros_like(gw_ref)

        def step(i, carry):
            slot, oslot = i % nbuf, i % 2
            for s in range(3):
                read(s, i, slot).wait()

            @pl.when(i >= 2)
            def _():
                for c in writes(i - 2, oslot):
                    c.wait()

            xh, r = _rms(ibuf[1, slot])
            dn_ = ibuf[0, slot]
            o = ibuf[2, slot] + _rms_bwd(dn_, xh, r, w_ref[...])
            obuf[oslot] = o
            obbuf[oslot] = o.astype(bf16)
            gw_ref[...] += jnp.sum(dn_ * xh, axis=0, keepdims=True)
            for c in writes(i, oslot):
                c.start()

            @pl.when(i + nbuf < n)
            def _():
                for s in range(3):
                    read(s, i + nbuf, slot).start()

            return carry

        lax.fori_loop(0, n, step, 0)
        for i in range(max(n - 2, 0), n):
            for c in writes(i, i % 2):
                c.wait()

    vmem = pl.BlockSpec(memory_space=pltpu.VMEM)
    return pl.pallas_call(
        body, name="norm_bwd", in_specs=[_ANY, _ANY, _ANY, vmem], out_specs=[_ANY, _ANY, vmem],
        out_shape=[jax.ShapeDtypeStruct((Lp, D), f32), jax.ShapeDtypeStruct((Lp, D), bf16),
                   jax.ShapeDtypeStruct((1, D), f32)],
        scratch_shapes=[pltpu.VMEM((3, nbuf, tr, D), f32), pltpu.VMEM((2, tr, D), f32), pltpu.VMEM((2, tr, D), bf16),
                        pltpu.SemaphoreType.DMA((3, nbuf)), pltpu.SemaphoreType.DMA((2, 2))],
        compiler_params=_params(),
    )(dn, h, dh, w)


def _embed_norm_bwd(dn, h, dh, w, S, side=None):
    Lp, D = h.shape
    tr = _tile(S, 256, OFF)

    def body(dn_ref, h_ref, dh_ref, dn0_ref, h0_ref, dh0_ref, w_ref, gx_ref, gm_ref, gw_ref):
        i = pl.program_id(0)

        def rows(dn_, h_, dh_):
            xh, r = _rms(h_)
            return dh_ + _rms_bwd(dn_, xh, r, w_ref[...]), jnp.sum(dn_ * xh, axis=0, keepdims=True)

        d, gw = rows(dn_ref[...], h_ref[...], dh_ref[...])
        gx_ref[...] = d

        @pl.when(i == 0)
        def _():
            d0, gw0 = rows(dn0_ref[...], h0_ref[...], dh0_ref[...])
            gm_ref[...] = d0[PAD:OFF, :]
            gw_ref[...] = gw0 + gw

        @pl.when(i > 0)
        def _():
            gw_ref[...] += gw

    win = pl.BlockSpec((pl.Element(tr), pl.Element(D)), lambda i: (pl.multiple_of(OFF + i * tr, OFF), 0))
    head = pl.BlockSpec((OFF, D), lambda i: (0, 0))
    vec = pl.BlockSpec((1, D), lambda i: (0, 0))
    out = _call(
        body, name="embed_norm_bwd", grid=(S // tr,), in_specs=[win, win, win, head, head, head, vec],
        out_specs=[pl.BlockSpec((None, tr, D), lambda i: (0, i, 0)),
                   pl.BlockSpec((N_META, D), lambda i: (0, 0)), vec],
        out_shape=[jax.ShapeDtypeStruct((1, S, D), f32), jax.ShapeDtypeStruct((N_META, D), f32),
                   jax.ShapeDtypeStruct((1, D), f32)],
        args=[dn, h, dh, dn, h, dh, w], sem=("arbitrary",), side=side)
    return out[0], out[1], out[2], out[3:]


def _final(h1, ffn, tgt3, w):
    Lp, D = h1.shape
    _, S, _ = tgt3.shape
    tr = _tile(Lp, min(256, S), OFF)

    def body(h_ref, f_ref, t_ref, w_ref, d_ref, db_ref, l_ref, gw_ref):
        i = pl.program_id(0)
        h2 = h_ref[...] + f_ref[...]
        xh, r = _rms(h2)
        w_ = w_ref[...]
        t = t_ref[...]
        t = jnp.where(i == 0, pltpu.roll(t, OFF, 0), t)
        valid = (lax.broadcasted_iota(jnp.int32, (tr, 1), 0) + i * tr >= OFF).astype(f32)
        e = xh * w_ - t
        loss = 0.5 * jnp.sum(jnp.mean(e * e, axis=-1, keepdims=True) * valid, axis=0, keepdims=True)
        dy = e * (valid / D)
        d = _rms_bwd(dy, xh, r, w_)
        d_ref[...] = d
        db_ref[...] = d.astype(bf16)
        gw = jnp.sum(dy * xh, axis=0, keepdims=True)

        @pl.when(i == 0)
        def _():
            l_ref[...] = jnp.zeros_like(l_ref)
            gw_ref[...] = jnp.zeros_like(gw_ref)

        l_ref[...] += jnp.broadcast_to(loss, l_ref.shape)
        gw_ref[...] += gw

    row = pl.BlockSpec((tr, D), lambda i: (i, 0))
    vec = pl.BlockSpec((1, D), lambda i: (0, 0))
    tgt = pl.BlockSpec((pl.Element(tr), pl.Element(D)),
                       lambda i: (pl.multiple_of(jnp.maximum(i * tr - OFF, 0), OFF), 0))
    return pl.pallas_call(
        body, name="final_loss", grid=(Lp // tr,), in_specs=[row, row, tgt, vec],
        out_specs=[row, row, pl.BlockSpec((8, 128), lambda i: (0, 0)), vec],
        out_shape=[jax.ShapeDtypeStruct((Lp, D), f32), jax.ShapeDtypeStruct((Lp, D), bf16),
                   jax.ShapeDtypeStruct((8, 128), f32), jax.ShapeDtypeStruct((1, D), f32)],
        compiler_params=_params("arbitrary"),
    )(h1, ffn, tgt3.reshape(S, D), w)


def _ffn_in(n, w_gate, w_up, side=None):
    M, K = n.shape
    F = w_gate.shape[1]
    tm = _tile(M, 1408)
    tn = _tile(F, 512, 128)

    def body(a_ref, bg_ref, bu_ref, act_ref, pg_ref, pu_ref):
        a = a_ref[...]
        g = _mmb(a, bg_ref[...], NN)
        u = _mmb(a, bu_ref[...], NN)
        s = _sigmoid(g)
        gs = g * s
        act_ref[...] = (gs * u).astype(bf16)
        pg_ref[...] = (u * (s + gs * (1.0 - s))).astype(bf16)
        pu_ref[...] = gs.astype(bf16)

    wsp = pl.BlockSpec((K, tn), lambda i, j: (0, j))
    osp = pl.BlockSpec((tm, tn), lambda i, j: (i, j))
    out = _call(body, name="ffn_in", grid=(M // tm, F // tn),
                in_specs=[pl.BlockSpec((tm, K), lambda i, j: (i, 0)), wsp, wsp], out_specs=[osp] * 3,
                out_shape=[jax.ShapeDtypeStruct((M, F), bf16)] * 3, args=[n, w_gate, w_up],
                sem=("parallel", "parallel"), side=side)
    return out[0], out[1], out[2], out[3:]


def _ffn_dact(d, w_down, pg, pu):
    M, K = d.shape
    F = w_down.shape[0]
    tm = _tile(M, 1408)
    tn = _tile(F, 512, 128)

    def body(d_ref, w_ref, pg_ref, pu_ref, dg_ref, du_ref):
        da = _mmb(d_ref[...], w_ref[...], NT)
        dg_ref[...] = (da * pg_ref[...].astype(f32)).astype(bf16)
        du_ref[...] = (da * pu_ref[...].astype(f32)).astype(bf16)

    osp = pl.BlockSpec((tm, tn), lambda i, j: (i, j))
    return pl.pallas_call(
        body, name="ffn_dact", grid=(M // tm, F // tn),
        in_specs=[pl.BlockSpec((tm, K), lambda i, j: (i, 0)), pl.BlockSpec((tn, K), lambda i, j: (j, 0)), osp, osp],
        out_specs=[osp, osp], out_shape=[jax.ShapeDtypeStruct((M, F), bf16)] * 2,
        compiler_params=_params("parallel", "parallel"),
    )(d, w_down, pg, pu)


def _ffn_dn(dg, du, w_gate, w_up):
    M, F = dg.shape
    D = w_gate.shape[0]
    tm = _tile(M, 688)
    tn = _tile(D, 256, 128)

    def body(dg_ref, du_ref, wg_ref, wu_ref, o_ref):
        o_ref[...] = _mmb(dg_ref[...], wg_ref[...], NT) + _mmb(du_ref[...], wu_ref[...], NT)

    asp = pl.BlockSpec((tm, F), lambda i, j: (i, 0))
    wsp = pl.BlockSpec((tn, F), lambda i, j: (j, 0))
    return pl.pallas_call(
        body, name="d_n2", grid=(M // tm, D // tn), in_specs=[asp, asp, wsp, wsp],
        out_specs=pl.BlockSpec((tm, tn), lambda i, j: (i, j)), out_shape=jax.ShapeDtypeStruct((M, D), f32),
        compiler_params=_params("parallel", "parallel"),
    )(dg, du, w_gate, w_up)


def _gates(psm, w2p, gate_b, alog, dtb):
    Lp = psm.shape[0]
    tr = _tile(Lp, 256)

    def body(p_ref, w_ref, b_ref, a_ref, t_ref, gb_ref, la_ref):
        i = pl.program_id(0)
        psm_ = p_ref[...]
        lane = lax.broadcasted_iota(jnp.int32, psm_.shape, 1)
        rowi = lax.broadcasted_iota(jnp.int32, (tr, 1), 0) + i * tr
        g = -jnp.exp(a_ref[...]) * _softplus(psm_ + t_ref[...])
        beta = _sigmoid(psm_)
        gb = jnp.where(lane < GDN_H, g, jnp.where(lane < 2 * GDN_H, beta, 0.0))
        gb_ref[...] = gb * (rowi >= PAD).astype(f32)
        logit = _mmb(psm_, w_ref[...], NN) + b_ref[...]
        la_ref[...] = _log_sigmoid(logit) * (1.0 / GATE_NORMALIZER)

    row = pl.BlockSpec((tr, SM_W), lambda i: (i, 0))
    return pl.pallas_call(
        body, name="gates", grid=(Lp // tr,),
        in_specs=[row, pl.BlockSpec((SM_W, GLA_QK), lambda i: (0, 0)), pl.BlockSpec((1, GLA_QK), lambda i: (0, 0)),
                  pl.BlockSpec((1, SM_W), lambda i: (0, 0)), pl.BlockSpec((1, SM_W), lambda i: (0, 0))],
        out_specs=[row, pl.BlockSpec((tr, GLA_QK), lambda i: (i, 0))],
        out_shape=[jax.ShapeDtypeStruct((Lp, SM_W), f32), jax.ShapeDtypeStruct((Lp, GLA_QK), f32)],
        compiler_params=_params("parallel"),
    )(psm, w2p, gate_b, alog, dtb)


def _gates_bwd(psm, w2p, gate_b, alog, dtb, dgb, dla):
    Lp = psm.shape[0]
    tr = _tile(Lp, 256)

    def body(p_ref, w_ref, b_ref, a_ref, t_ref, dgb_ref, dla_ref, dp_ref, gw_ref, gb_ref, ga_ref, gt_ref):
        i = pl.program_id(0)
        psm_ = p_ref[...]
        lane = lax.broadcasted_iota(jnp.int32, psm_.shape, 1)
        rowi = lax.broadcasted_iota(jnp.int32, (tr, 1), 0) + i * tr
        d = dgb_ref[...] * (rowi >= PAD).astype(f32)
        ea = jnp.exp(a_ref[...])
        z = psm_ + t_ref[...]
        is_g = lane < GDN_H
        dz = jnp.where(is_g, -ea * _sigmoid(z) * d, 0.0)
        dalog = jnp.where(is_g, -ea * _softplus(z) * d, 0.0)
        beta = _sigmoid(psm_)
        dbeta = jnp.where(jnp.logical_and(lane >= GDN_H, lane < 2 * GDN_H), beta * (1.0 - beta) * d, 0.0)
        logit = _mmb(psm_, w_ref[...], NN) + b_ref[...]
        dlogit = dla_ref[...] * (_sigmoid(-logit) * (1.0 / GATE_NORMALIZER))
        dlr = _mmb(dlogit, w_ref[...], NT)
        dp_ref[...] = (dz + dbeta + dlr).astype(bf16)
        gw = _mmb(psm_, dlogit, TN)
        gb = jnp.sum(dlogit, axis=0, keepdims=True)
        ga = jnp.sum(dalog, axis=0, keepdims=True)
        gt = jnp.sum(dz, axis=0, keepdims=True)

        @pl.when(i == 0)
        def _():
            gw_ref[...] = gw
            gb_ref[...] = gb
            ga_ref[...] = ga
            gt_ref[...] = gt

        @pl.when(i > 0)
        def _():
            gw_ref[...] += gw
            gb_ref[...] += gb
            ga_ref[...] += ga
            gt_ref[...] += gt

    row = pl.BlockSpec((tr, SM_W), lambda i: (i, 0))
    wsp = pl.BlockSpec((SM_W, GLA_QK), lambda i: (0, 0))
    bsp = pl.BlockSpec((1, GLA_QK), lambda i: (0, 0))
    vsp = pl.BlockSpec((1, SM_W), lambda i: (0, 0))
    return pl.pallas_call(
        body, name="gates_bwd", grid=(Lp // tr,),
        in_specs=[row, wsp, bsp, vsp, vsp, row, pl.BlockSpec((tr, GLA_QK), lambda i: (i, 0))],
        out_specs=[row, wsp, bsp, vsp, vsp],
        out_shape=[jax.ShapeDtypeStruct((Lp, SM_W), bf16), jax.ShapeDtypeStruct((SM_W, GLA_QK), f32),
                   jax.ShapeDtypeStruct((1, GLA_QK), f32), jax.ShapeDtypeStruct((1, SM_W), f32),
                   jax.ShapeDtypeStruct((1, SM_W), f32)],
        compiler_params=_params("arbitrary"),
    )(psm, w2p, gate_b, alog, dtb, dgb, dla)


def _conv_pre(x_ext, w, n):
    rows = x_ext.shape[0]
    y = x_ext * w[CONV_K - 1:CONV_K, :]
    for s in range(1, CONV_K):
        y = y + pltpu.roll(x_ext, s, 0) * w[CONV_K - 1 - s:CONV_K - s, :]
    return y[rows - n:, :]


def _conv(proj, cw, side=None):
    Lp = proj.shape[0]
    W = cw.shape[1]
    tr = _tile(Lp, 256, 64)
    tc = W

    def body(h_ref, x_ref, w_ref, o_ref):
        i = pl.program_id(1)
        halo = jnp.where(i == 0, 0.0, h_ref[...])
        x_ext = jnp.concatenate([halo, x_ref[...]], axis=0)
        o_ref[...] = _silu(_conv_pre(x_ext, w_ref[...], tr))

    def window(rows, first_row):
        return pl.BlockSpec((pl.Element(rows), pl.Element(tc)),
                            lambda j, i: (pl.multiple_of(first_row(i), 8), pl.multiple_of(C_QKV + j * tc, LANES)))

    out = _call(
        body, name="conv", grid=(W // tc, Lp // tr),
        in_specs=[window(8, lambda i: jnp.maximum(i * tr - 8, 0)), window(tr, lambda i: i * tr),
                  pl.BlockSpec((CONV_K, tc), lambda j, i: (0, j))],
        out_specs=[pl.BlockSpec((tr, tc), lambda j, i: (i, j))],
        out_shape=[jax.ShapeDtypeStruct((Lp, W), f32)], args=[proj, proj, cw],
        sem=("parallel", "parallel"), side=side)
    return out[0] if side is None else (out[0], out[1:])


def _conv_bwd(proj, cw, dy, dproj, side=None):
    Lp = proj.shape[0]
    W = cw.shape[1]
    tr = _tile(Lp, 256, 64)
    tc = _tile(W, 1024, 128)
    c0 = C_QKV // tc
    nr = Lp // tr
    last8 = Lp // 8 - 1

    def body(xp_ref, x_ref, xn_ref, w_ref, d_ref, dn_ref, dproj_ref, o_ref, gw_ref):
        del dproj_ref
        i = pl.program_id(1)
        w = w_ref[...]
        xp = jnp.where(i == 0, 0.0, xp_ref[...])
        x_ext = jnp.concatenate([xp, x_ref[...], xn_ref[...]], axis=0)
        n = tr + 8
        pre = _conv_pre(x_ext, w, n)
        dn = jnp.where(i == nr - 1, 0.0, dn_ref[...])
        dpre = jnp.concatenate([d_ref[...], dn], axis=0) * _dsilu(pre)
        dx = dpre * w[CONV_K - 1:CONV_K, :]
        for s in range(1, CONV_K):
            dx = dx + pltpu.roll(dpre, n - s, 0) * w[CONV_K - 1 - s:CONV_K - s, :]
        o_ref[...] = dx[:tr, :].astype(bf16)
        dp = dpre[:tr, :]
        rows = []
        for k in range(CONV_K):
            xs = x_ext if k == CONV_K - 1 else pltpu.roll(x_ext, CONV_K - 1 - k, 0)
            rows.append(jnp.sum(dp * xs[8:8 + tr, :], axis=0, keepdims=True))
        gw = jnp.concatenate(rows, axis=0)

        @pl.when(i == 0)
        def _():
            gw_ref[...] = gw

        @pl.when(i > 0)
        def _():
            gw_ref[...] += gw

    cur = pl.BlockSpec((tr, tc), lambda j, i: (i, j))
    nxt = pl.BlockSpec((8, tc), lambda j, i: (jnp.minimum((i + 1) * (tr // 8), last8), j))
    pcur = pl.BlockSpec((tr, tc), lambda j, i: (i, j + c0))
    pprev = pl.BlockSpec((8, tc), lambda j, i: (jnp.maximum(i * (tr // 8) - 1, 0), j + c0))
    pnext = pl.BlockSpec((8, tc), lambda j, i: (jnp.minimum((i + 1) * (tr // 8), last8), j + c0))
    wsp = pl.BlockSpec((CONV_K, tc), lambda j, i: (0, j))
    out = _call(
        body, name="conv_bwd", grid=(W // tc, nr),
        in_specs=[pprev, pcur, pnext, wsp, cur, nxt, _ANY], out_specs=[pcur, wsp],
        out_shape=[jax.ShapeDtypeStruct(dproj.shape, dproj.dtype), jax.ShapeDtypeStruct((CONV_K, W), f32)],
        aliases={6: 0}, args=[proj, proj, proj, cw, dy, dy, dproj], sem=("parallel", "arbitrary"), side=side)
    return out[0], out[1], out[2:]


GDN_FWD_GROUP = 3


def _gdn_group(Lp, most):
    n = Lp // GDN_C
    return next(g for g in range(most, 0, -1) if n % g == 0)


def _gdn_heads(x_ref, gb_ref, group):
    qs, ks, vs, bs, gs = [], [], [], [], []
    for chunk in range(group):
        r = slice(chunk * GDN_C, (chunk + 1) * GDN_C)
        gbv = gb_ref[r, :]
        for h in range(GDN_H):
            qs.append(x_ref[r, Q0 + h * GDN_DK:Q0 + (h + 1) * GDN_DK])
            ks.append(x_ref[r, K0 + h * GDN_DK:K0 + (h + 1) * GDN_DK])
            vs.append(x_ref[r, V0 + h * GDN_DV:V0 + (h + 1) * GDN_DV])
            bs.append(gbv[:, GDN_H + h:GDN_H + h + 1])
            gs.append(gbv[:, h:h + 1])
    return qs, ks, vs, bs, gs


def _gdn_fwd(qkvc, gb, side=None):
    Lp = qkvc.shape[0]
    group = _gdn_group(Lp, GDN_FWD_GROUP)
    rows = group * GDN_C
    steps = Lp // rows
    R = range(GDN_H)

    def body(x_ref, gb_ref, o_ref, sall_ref, pall_ref, s_scr):
        @pl.when(pl.program_id(0) == 0)
        def _():
            s_scr[...] = jnp.zeros_like(s_scr)

        S2, o, p, entering = _gdn_chunk([s_scr[h] for h in R], *_gdn_heads(x_ref, gb_ref, group))
        for h in R:
            s_scr[h] = S2[h]
        for chunk in range(group):
            for h in R:
                i = chunk * GDN_H + h
                o_ref[chunk * GDN_C:(chunk + 1) * GDN_C, h * GDN_DV:(h + 1) * GDN_DV] = o[i]
                pall_ref[chunk, h] = p[i]
                sall_ref[chunk, h] = entering[i]

    out = _call(
        body, name="gdn_fwd", grid=(steps,),
        in_specs=[pl.BlockSpec((rows, QKV_W), lambda n: (n, 0)), pl.BlockSpec((rows, SM_W), lambda n: (n, 0))],
        out_specs=[pl.BlockSpec((rows, GDN_V), lambda n: (n, 0)),
                   pl.BlockSpec((group, GDN_H, GDN_DK, GDN_DV), lambda n: (n, 0, 0, 0)),
                   pl.BlockSpec((group, GDN_H, GDN_C, GDN_C), lambda n: (n, 0, 0, 0))],
        out_shape=[jax.ShapeDtypeStruct((Lp, GDN_V), f32),
                   jax.ShapeDtypeStruct((Lp // GDN_C, GDN_H, GDN_DK, GDN_DV), f32),
                   jax.ShapeDtypeStruct((Lp // GDN_C, GDN_H, GDN_C, GDN_C), f32)],
        scratch_shapes=[pltpu.VMEM((GDN_H, GDN_DK, GDN_DV), f32)], args=[qkvc, gb], sem=("arbitrary",), side=side)
    return out[0], out[1], out[2], out[3:]


def _gdn_bwd(qkvc, gb, sall, pall, do, side=None):
    Lp = qkvc.shape[0]
    group = 1
    rows = group * GDN_C
    steps = Lp // rows
    R = range(GDN_H)

    def body(x_ref, gb_ref, sall_ref, pall_ref, do_ref, dx_ref, dgb_ref, ds_scr):
        @pl.when(pl.program_id(0) == 0)
        def _():
            ds_scr[...] = jnp.zeros_like(ds_scr)

        lane = lax.broadcasted_iota(jnp.int32, (GDN_C, SM_W), 1)
        ps = [pall_ref[chunk, h] for chunk in range(group) for h in R]
        _, vjp = jax.vjp(lambda *a: _gdn_chunk(*a, Ps=ps)[:2],
                         [sall_ref[0, h] for h in R], *_gdn_heads(x_ref, gb_ref, group))
        do = [do_ref[chunk * GDN_C:(chunk + 1) * GDN_C, h * GDN_DV:(h + 1) * GDN_DV]
              for chunk in range(group) for h in R]
        dS, dq, dk, dv, dbeta, dg = vjp(([ds_scr[h] for h in R], do))
        for h in R:
            ds_scr[h] = dS[h]
        for chunk in range(group):
            r = slice(chunk * GDN_C, (chunk + 1) * GDN_C)
            acc = jnp.zeros((GDN_C, SM_W), f32)
            for h in R:
                i = chunk * GDN_H + h
                dx_ref[r, Q0 + h * GDN_DK:Q0 + (h + 1) * GDN_DK] = dq[i]
                dx_ref[r, K0 + h * GDN_DK:K0 + (h + 1) * GDN_DK] = dk[i]
                dx_ref[r, V0 + h * GDN_DV:V0 + (h + 1) * GDN_DV] = dv[i]
                acc = acc + jnp.where(lane == h, dg[i], 0.0) + jnp.where(lane == GDN_H + h, dbeta[i], 0.0)
            dgb_ref[r, :] = acc

    rev = lambda n: (steps - 1 - n, 0)
    out = _call(
        body, name="gdn_bwd", grid=(steps,),
        in_specs=[pl.BlockSpec((rows, QKV_W), rev), pl.BlockSpec((rows, SM_W), rev),
                  pl.BlockSpec((1, GDN_H, GDN_DK, GDN_DV), lambda n: (steps - 1 - n, 0, 0, 0)),
                  pl.BlockSpec((group, GDN_H, GDN_C, GDN_C), lambda n: (steps - 1 - n, 0, 0, 0)),
                  pl.BlockSpec((rows, GDN_V), rev)],
        out_specs=[pl.BlockSpec((rows, QKV_W), rev), pl.BlockSpec((rows, SM_W), rev)],
        out_shape=[jax.ShapeDtypeStruct((Lp, QKV_W), f32), jax.ShapeDtypeStruct((Lp, SM_W), f32)],
        scratch_shapes=[pltpu.VMEM((GDN_H, GDN_DK, GDN_DV), f32)], args=[qkvc, gb, sall, pall, do],
        sem=("arbitrary",), side=side)
    return out[0], out[1], out[2:]


GLA_BLOCK = 64


def _gla_group(Lp):
    nb = Lp // GLA_BLOCK
    return next(g for g in (3, 2, 1) if nb % g == 0)


def _gla_slices(h):
    sq = slice(h * GLA_DK, (h + 1) * GLA_DK)
    sk = slice(GLA_QK + h * GLA_DK, GLA_QK + (h + 1) * GLA_DK)
    sv = slice(2 * GLA_QK + h * GLA_DV, 2 * GLA_QK + (h + 1) * GLA_DV)
    return sq, sk, sv


def _gla_heads(x_ref, la_ref, group):
    qs, ks, vs, ls = [], [], [], []
    for blk in range(group):
        r = slice(blk * GLA_BLOCK, (blk + 1) * GLA_BLOCK)
        for h in range(GLA_H):
            sq, sk, sv = _gla_slices(h)
            qs.append(x_ref[r, sq])
            ks.append(x_ref[r, sk])
            vs.append(x_ref[r, sv])
            ls.append(la_ref[r, sq])
    return qs, ks, vs, ls


def _gla_fwd(proj, la):
    Lp = proj.shape[0]
    group = _gla_group(Lp)
    rows = group * GLA_BLOCK
    steps = Lp // rows
    R = range(GLA_H)

    def body(x_ref, la_ref, o_ref, sall_ref, s_scr):
        @pl.when(pl.program_id(0) == 0)
        def _():
            s_scr[...] = jnp.zeros_like(s_scr)

        Sts = [s_scr[h] for h in R]
        for h in R:
            sall_ref[0, h] = Sts[h]
        St2, o = _gla_blocks(Sts, *_gla_heads(x_ref, la_ref, group))
        for h in R:
            s_scr[h] = St2[h]
        for blk in range(group):
            for h in R:
                o_ref[blk * GLA_BLOCK:(blk + 1) * GLA_BLOCK, h * GLA_DV:(h + 1) * GLA_DV] = o[blk * GLA_H + h]

    return pl.pallas_call(
        body, name="gla_fwd", grid=(steps,),
        in_specs=[pl.BlockSpec((rows, G_W), lambda n: (n, C_G // G_W)),
                  pl.BlockSpec((rows, GLA_QK), lambda n: (n, 0))],
        out_specs=[pl.BlockSpec((rows, GLA_V), lambda n: (n, 0)),
                   pl.BlockSpec((1, GLA_H, GLA_DV, GLA_DK), lambda n: (n, 0, 0, 0))],
        out_shape=[jax.ShapeDtypeStruct((Lp, GLA_V), f32),
                   jax.ShapeDtypeStruct((steps, GLA_H, GLA_DV, GLA_DK), f32)],
        scratch_shapes=[pltpu.VMEM((GLA_H, GLA_DV, GLA_DK), f32)],
        compiler_params=_params("arbitrary"),
    )(proj, la)


def _gla_bwd(proj, la, sall, do, dproj, side=None):
    Lp = proj.shape[0]
    group = _gla_group(Lp)
    rows = group * GLA_BLOCK
    steps = Lp // rows
    R = range(GLA_H)

    def body(x_ref, la_ref, sall_ref, do_ref, dproj_ref, dx_ref, dla_ref, ds_scr):
        del dproj_ref

        @pl.when(pl.program_id(0) == 0)
        def _():
            ds_scr[...] = jnp.zeros_like(ds_scr)

        _, vjp = jax.vjp(_gla_blocks, [sall_ref[0, h] for h in R], *_gla_heads(x_ref, la_ref, group))
        do = [do_ref[blk * GLA_BLOCK:(blk + 1) * GLA_BLOCK, h * GLA_DV:(h + 1) * GLA_DV]
              for blk in range(group) for h in R]
        dS, dq, dk, dv, dl = vjp(([ds_scr[h] for h in R], do))
        for h in R:
            ds_scr[h] = dS[h]
        for blk in range(group):
            r = slice(blk * GLA_BLOCK, (blk + 1) * GLA_BLOCK)
            for h in R:
                sq, sk, sv = _gla_slices(h)
                i = blk * GLA_H + h
                dx_ref[r, sq] = dq[i].astype(bf16)
                dx_ref[r, sk] = dk[i].astype(bf16)
                dx_ref[r, sv] = dv[i].astype(bf16)
                dla_ref[r, sq] = dl[i]

    x_spec = pl.BlockSpec((rows, G_W), lambda n: (steps - 1 - n, C_G // G_W))
    rev = lambda n: (steps - 1 - n, 0)
    out = _call(
        body, name="gla_bwd", grid=(steps,),
        in_specs=[x_spec, pl.BlockSpec((rows, GLA_QK), rev),
                  pl.BlockSpec((1, GLA_H, GLA_DV, GLA_DK), lambda n: (steps - 1 - n, 0, 0, 0)),
                  pl.BlockSpec((rows, GLA_V), rev), _ANY],
        out_specs=[x_spec, pl.BlockSpec((rows, GLA_QK), rev)],
        out_shape=[jax.ShapeDtypeStruct(dproj.shape, dproj.dtype), jax.ShapeDtypeStruct((Lp, GLA_QK), f32)],
        aliases={4: 0}, scratch_shapes=[pltpu.VMEM((GLA_H, GLA_DV, GLA_DK), f32)],
        args=[proj, la, sall, do, dproj], sem=("arbitrary",), side=side)
    return out[0], out[1], out[2:]


def _gated_norm_fn(og, ol, zr, wg, wl):
    outs = []
    for h in range(GDN_H):
        s = slice(h * GDN_DV, (h + 1) * GDN_DV)
        outs.append(_rms(og[:, s])[0] * wg * _silu(zr[:, s]))
    for h in range(GLA_H):
        s = slice(h * GLA_DV, (h + 1) * GLA_DV)
        sr = slice(GDN_V + h * GLA_DV, GDN_V + (h + 1) * GLA_DV)
        outs.append(_rms(ol[:, s])[0] * wl * _silu(zr[:, sr]))
    return jnp.concatenate(outs, axis=-1)


def _gated_norm(og, ol, proj, wg, wl):
    Lp = og.shape[0]
    tr = _tile(Lp, 256)

    def body(og_ref, ol_ref, zr_ref, wg_ref, wl_ref, o_ref):
        o_ref[...] = _gated_norm_fn(og_ref[...], ol_ref[...], zr_ref[...], wg_ref[...], wl_ref[...]).astype(bf16)

    return pl.pallas_call(
        body, name="gated_norm", grid=(Lp // tr,),
        in_specs=[pl.BlockSpec((tr, GDN_V), lambda i: (i, 0)), pl.BlockSpec((tr, GLA_V), lambda i: (i, 0)),
                  pl.BlockSpec((tr, ZR_W), lambda i: (i, C_ZR // ZR_W)),
                  pl.BlockSpec((1, GDN_DV), lambda i: (0, 0)), pl.BlockSpec((1, GLA_DV), lambda i: (0, 0))],
        out_specs=pl.BlockSpec((tr, ZR_W), lambda i: (i, 0)),
        out_shape=jax.ShapeDtypeStruct((Lp, ZR_W), bf16),
        compiler_params=_params("parallel"),
    )(og, ol, proj, wg, wl)


def _gated_norm_bwd(og, ol, proj, wg, wl, dmix):
    Lp = og.shape[0]
    tr = _tile(Lp, 256)

    def body(og_ref, ol_ref, zr_ref, wg_ref, wl_ref, d_ref, dog_ref, dol_ref, dzr_ref, gwg_ref, gwl_ref):
        i = pl.program_id(0)
        _, vjp = jax.vjp(_gated_norm_fn, og_ref[...], ol_ref[...], zr_ref[...], wg_ref[...], wl_ref[...])
        dog, dol, dzr, gwg, gwl = vjp(d_ref[...])
        dog_ref[...] = dog
        dol_ref[...] = dol
        dzr_ref[...] = dzr.astype(bf16)

        @pl.when(i == 0)
        def _():
            gwg_ref[...] = gwg
            gwl_ref[...] = gwl

        @pl.when(i > 0)
        def _():
            gwg_ref[...] += gwg
            gwl_ref[...] += gwl

    og_spec = pl.BlockSpec((tr, GDN_V), lambda i: (i, 0))
    ol_spec = pl.BlockSpec((tr, GLA_V), lambda i: (i, 0))
    zr_spec = pl.BlockSpec((tr, ZR_W), lambda i: (i, C_ZR // ZR_W))
    vg = pl.BlockSpec((1, GDN_DV), lambda i: (0, 0))
    vl = pl.BlockSpec((1, GLA_DV), lambda i: (0, 0))
    return pl.pallas_call(
        body, name="gated_norm_bwd", grid=(Lp // tr,),
        in_specs=[og_spec, ol_spec, zr_spec, vg, vl, pl.BlockSpec((tr, ZR_W), lambda i: (i, 0))],
        out_specs=[og_spec, ol_spec, zr_spec, vg, vl],
        out_shape=[jax.ShapeDtypeStruct((Lp, GDN_V), f32), jax.ShapeDtypeStruct((Lp, GLA_V), f32),
                   jax.ShapeDtypeStruct((Lp, C_END), bf16),
                   jax.ShapeDtypeStruct((1, GDN_DV), f32), jax.ShapeDtypeStruct((1, GLA_DV), f32)],
        compiler_params=_params("arbitrary"),
    )(og, ol, proj, wg, wl, dmix)


def _adamw_rule(g_, w_, m_, v_):
    c1 = 1.0 - ADAM_B1 ** ADAM_STEP
    c2 = 1.0 - ADAM_B2 ** ADAM_STEP
    m2 = ADAM_B1 * m_ + (1.0 - ADAM_B1) * g_
    v2 = ADAM_B2 * v_ + (1.0 - ADAM_B2) * (g_ * g_)
    return -ADAM_LR * ((m2 / c1) / (jnp.sqrt(v2 / c2) + ADAM_EPS) + ADAM_WD * w_), m2, v2


def _adamw(g, w, m, v, name, copy_g=False):
    R, C = g.shape
    tr = _tile(R, 256, 8) if R % 8 == 0 and R > 256 else R

    def body(g_ref, w_ref, m_ref, v_ref, *o_refs):
        g_ = g_ref[...]
        if copy_g:
            o_refs[0][...] = g_
        d_ref, mo_ref, vo_ref = o_refs[-3:]
        d_ref[...], mo_ref[...], vo_ref[...] = _adamw_rule(g_, w_ref[...], m_ref[...], v_ref[...])

    blk = pl.BlockSpec((tr, C), lambda i: (i, 0))
    n_out = 4 if copy_g else 3
    return pl.pallas_call(
        body, name=name, grid=(R // tr,), in_specs=[blk] * 4, out_specs=[blk] * n_out,
        out_shape=[jax.ShapeDtypeStruct((R, C), f32)] * n_out,
        compiler_params=_params("parallel"),
    )(g, w, m, v)


def _adamw_transposed(gt, w, m, v, name):
    n, C, rb = gt.shape
    assert w.shape == (C, n * rb), (gt.shape, w.shape)

    def body(g_ref, w_ref, m_ref, v_ref, go_ref, d_ref, mo_ref, vo_ref):
        g_ = g_ref[...].astype(f32)
        go_ref[...] = g_
        d_ref[...], mo_ref[...], vo_ref[...] = _adamw_rule(g_, w_ref[...], m_ref[...], v_ref[...])

    blk = pl.BlockSpec((C, rb), lambda j: (0, j))
    return pl.pallas_call(
        body, name=name, grid=(n,), in_specs=[pl.BlockSpec((None, C, rb), lambda j: (j, 0, 0))] + [blk] * 3,
        out_specs=[blk] * 4, out_shape=[jax.ShapeDtypeStruct((C, n * rb), f32)] * 4,
        compiler_params=_params("parallel"),
    )(gt, w, m, v)


def _sum_slots(r, name):
    n, R, C = r.shape
    tr = _tile(R, 128, 16) if R % 16 == 0 and R > 128 else R

    def body(r_ref, o_ref):
        acc = r_ref[0].astype(f32)
        for s in range(1, n):
            acc = acc + r_ref[s].astype(f32)
        o_ref[...] = acc

    return pl.pallas_call(
        body, name=name, grid=(R // tr,),
        in_specs=[pl.BlockSpec((n, tr, C), lambda i: (0, i, 0))],
        out_specs=pl.BlockSpec((tr, C), lambda i: (i, 0)),
        out_shape=jax.ShapeDtypeStruct((R, C), f32),
        compiler_params=_params("parallel"),
    )(r)


SIBLING_PARTS = 8


class _Siblings:
    def __init__(self, arrays):
        self.arrays = list(arrays)
        self.n = len(self.arrays)
        self.parts = [next(p for p in range(SIBLING_PARTS, 0, -1) if a.shape[0] % (8 * p) == 0 or p == 1)
                      for a in self.arrays]
        total = sum(self.parts)
        self.out_shape = [jax.ShapeDtypeStruct((2,) + a.shape, a.dtype) for a in self.arrays]
        self.sems = [pltpu.SemaphoreType.DMA((total,)), pltpu.SemaphoreType.DMA((total,)),
                     pltpu.SemaphoreType.DMA((self.n,))]

    def hooks(self, ins, outs, send, recv, lsem):
        def copies():
            x, y, c = lax.axis_index("x"), lax.axis_index("y"), lax.axis_index("c")
            out, k = [], 0
            for a in range(self.n):
                out.append((pltpu.make_async_copy(ins[a], outs[a].at[c], lsem.at[a]), None))
                rows = self.arrays[a].shape[0] // self.parts[a]
                for part in range(self.parts[a]):
                    r = pl.ds(part * rows, rows)
                    mk = lambda dst, a=a, r=r, k=k: pltpu.make_async_remote_copy(
                        src_ref=ins[a].at[r], dst_ref=dst.at[r], send_sem=send.at[k], recv_sem=recv.at[k],
                        device_id=(x, y, 1 - c), device_id_type=MESH)
                    out.append((mk(outs[a].at[c]), mk(outs[a].at[1 - c])))
                    k += 1
            return out

        return _start_wait(copies)


def _comm_now(name, sides):
    total = sum(s.n for s in sides)

    def body(*refs):
        ins, outs, sems = refs[:total], refs[total:2 * total], refs[2 * total:]
        hooks, o = [], 0
        for i, s in enumerate(sides):
            hooks.append(s.hooks(ins[o:o + s.n], outs[o:o + s.n], *sems[3 * i:3 * i + 3]))
            o += s.n
        for start, _ in hooks:
            start()
        for _, wait in hooks:
            wait()

    out = pl.pallas_call(
        body, name=name, in_specs=[_ANY] * total, out_specs=[_ANY] * total,
        out_shape=[sh for s in sides for sh in s.out_shape], scratch_shapes=[sm for s in sides for sm in s.sems],
    )(*[a for s in sides for a in s.arrays])
    res, o = [], 0
    for s in sides:
        res.append(list(out[o:o + s.n]))
        o += s.n
    return res


def _cat_cols(g):
    return jnp.concatenate([g[i] for i in range(N_CHIP)], axis=-1)


def _row_slabs(a):
    return a.reshape(N_DEV, a.shape[0] // N_DEV, a.shape[1])


def _w_in_columns(g_wp, g_wsm):
    return jnp.concatenate([g_wp[:, C_QKV:C_END], g_wp[:, C_ZR:C_ZR + GDN_V], g_wsm[:, :SM_LR],
                            g_wp[:, C_G:C_G + G_W], g_wp[:, C_ZR + GDN_V:C_ZR + ZR_W],
                            g_wsm[:, SM_LR:SM_LR + GATE_RANK]], axis=1)


def _step(x, loss_target, p, meta, shard):
    _, S, D = x.shape
    alog_p = jnp.pad(p["gdn_a_log"], ((0, 0), (0, SM_W - GDN_H)))
    dtb_p = jnp.pad(p["gdn_dt_bias"], ((0, 0), (0, SM_W - GDN_H)))
    m64 = jnp.concatenate([jnp.zeros((PAD, D), f32), meta], axis=0)
    gate_b, gdn_norm_w, gla_norm_w = p["gla_gate_b"], p["gdn_norm_w"], p["gla_norm_w"]

    later = ("w_up", "w_out", "w_gate", "w_down")
    h0, n1, in_bf16, (w_in4, conv4, w24) = _embed_norm(
        x, m64, p["attn_norm_w"], to_bf16=[shard[k] for k in later],
        side=_Gather([shard["w_in"], shard["gdn_conv_w"], shard["gla_gate_w2"]]))
    shard = dict(shard, **dict(zip(later, in_bf16)))
    w_in, conv_w, w2 = _cat_cols(w_in4), _cat_cols(conv4), _cat_cols(w24)
    wp = jnp.concatenate([w_in[:, R_Z:R_AB], w_in[:, R_GR:R_LR], w_in[:, R_G:R_GR], w_in[:, R_QKV:R_Z]], axis=1)
    wsm = jnp.concatenate([w_in[:, R_AB:R_G], w_in[:, R_LR:R_END],
                           jnp.zeros((D, SM_W - SM_LR - GATE_RANK), w_in.dtype)], axis=1)
    w2p = jnp.pad(w2, ((SM_LR, SM_W - SM_LR - GATE_RANK), (0, 0)))
    proj, (w_up,) = _mm(n1, wp, "nn", "proj", side=_Gather([shard["w_up"]], by_columns=True))
    psm = _mm(n1, wsm, "nn", "proj_small")
    gb, la = _gates(psm, w2p, gate_b, alog_p, dtb_p)
    qkvc, (w_out4,) = _conv(proj, conv_w, side=_Gather([shard["w_out"]]))
    w_out = w_out4.reshape(-1, D)
    og, sall, pall, (w_gate,) = _gdn_fwd(qkvc, gb, side=_Gather([shard["w_gate"]], by_columns=True))
    ol, stall = _gla_fwd(proj, la)
    mixed = _gated_norm(og, ol, proj, gdn_norm_w, gla_norm_w)
    attn = _mm(mixed, w_out, "nn", "out_proj")
    h1, n2 = _add_norm(h0, attn, p["ffn_norm_w"])
    act, act_dgate, act_dup, (w_down4,) = _ffn_in(n2, w_gate, w_up, side=_Gather([shard["w_down"]]))
    w_down = w_down4.reshape(-1, D)
    ffn = _mm(act, w_down, "nn", "ffn_down", **WHOLE_K)
    dh2, dh2b, lossp, g_final = _final(h1, ffn, loss_target, p["final_norm_w"])

    g_down = _mm(act, dh2b, "tn", "g_w_down", out_dtype=bf16, **WHOLE_K_T)
    dg, du = _ffn_dact(dh2b, w_down, act_dgate, act_dup)
    g_gate = _mm(n2, dg, "tn", "g_w_gate", tm_cap=512, tn_cap=1408, tk_cap=2752, out_dtype=bf16, col_slabs=True)
    g_up = _mm(n2, du, "tn", "g_w_up", tm_cap=512, tn_cap=1408, tk_cap=2752, out_dtype=bf16, col_slabs=True)
    dn2 = _ffn_dn(dg, du, w_gate, w_up)
    dh1, dh1b, g_ffn_norm = _norm_bwd(dn2, h1, dh2, p["ffn_norm_w"])
    dmix = _mm(dh1b, w_out, "nt", "d_mixed")
    g_out = _mm(mixed, dh1b, "tn", "g_w_out", out_dtype=bf16, **WHOLE_K_T)
    dog, dol, dproj, g_gdn_norm, g_gla_norm = _gated_norm_bwd(og, ol, proj, gdn_norm_w, gla_norm_w, dmix)
    dproj, dla, (r_down,) = _gla_bwd(proj, la, stall, dol, dproj, side=_Exchange([_row_slabs(g_down)]))
    dqkvc, dgb, (r_gate, r_up, r_out, h_down) = _gdn_bwd(
        qkvc, gb, sall, pall, dog,
        side=_Sides(_Exchange([g_gate, g_up, _row_slabs(g_out)]), _Siblings([_sum_slots(r_down, "sum_w_down")])))
    dproj, g_conv, (h_gate,) = _conv_bwd(proj, conv_w, dqkvc, dproj,
                                         side=_Siblings([_sum_slots(r_gate, "sum_w_gate")]))
    dpsm, g_w2p, g_gate_b, g_alog, g_dtb = _gates_bwd(psm, w2p, gate_b, alog_p, dtb_p, dgb, dla)
    g_wsm = _mm(n1, dpsm, "tn", "g_w_in_small", out_dtype=bf16, **WHOLE_K_T)
    g_wp_a, (h_out,) = _mm(n1, dproj, "tn", "g_w_in_a", out_dtype=bf16, b_cols=(0, W_IN_SPLIT),
                           side=_Siblings([_sum_slots(r_out, "sum_w_out")]), **WHOLE_K_T)
    g_wp_b, r_in_a = _mm(n1, dproj, "tn", "g_w_in_b", out_dtype=bf16, b_cols=(W_IN_SPLIT, C_END - W_IN_SPLIT),
                         side=_Exchange([_row_slabs(g_wp_a), _row_slabs(g_wsm)]), **WHOLE_K_T)
    dn1, (r_in_b, h_up) = _mm(
        dproj, wp, "nt", "d_n1", plus=(dpsm, wsm),
        side=_Sides(_Exchange([_row_slabs(g_wp_b)]), _Siblings([_sum_slots(r_up, "sum_w_up")])), **WHOLE_K)
    s_wp = jnp.concatenate([_sum_slots(r_in_a[0], "sum_w_in_a"), _sum_slots(r_in_b, "sum_w_in_b")], axis=1)
    s_in = _w_in_columns(s_wp, _sum_slots(r_in_a[1], "sum_w_in_small"))
    in_by_chip = s_in.reshape(s_in.shape[0], N_CHIP, -1).transpose(1, 2, 0).astype(bf16)
    grad_x, g_meta, g_attn_norm, (h_in,) = _embed_norm_bwd(dn1, h0, dh1, p["attn_norm_w"], S,
                                                           side=_Exchange([], by_chip=[in_by_chip]))

    received = dict(w_in=h_in, w_gate=h_gate, w_up=h_up, w_out=h_out, w_down=h_down)
    small = dict(
        meta_tokens=g_meta, attn_norm_w=g_attn_norm, gdn_conv_w=g_conv, gdn_a_log=g_alog[:, :GDN_H],
        gdn_dt_bias=g_dtb[:, :GDN_H], gdn_norm_w=g_gdn_norm, gla_gate_w2=g_w2p[SM_LR:SM_LR + GATE_RANK],
        gla_gate_b=g_gate_b, gla_norm_w=g_gla_norm, ffn_norm_w=g_ffn_norm, final_norm_w=g_final)
    return lossp[0, 0], grad_x, received, small


_WEIGHTS = ("meta_tokens", "attn_norm_w", "w_in", "gdn_conv_w", "gdn_a_log", "gdn_dt_bias", "gdn_norm_w",
            "gla_gate_w2", "gla_gate_b", "gla_norm_w", "w_out", "ffn_norm_w", "w_gate", "w_up", "w_down",
            "final_norm_w")
_BIG_COLS = ("w_in", "w_gate", "w_up")
_BIG_ROWS = ("w_out", "w_down")
_SMALL_SHARDED = ("meta_tokens", "gdn_conv_w", "gla_gate_w2")


def kernel(x, meta_tokens, attn_norm_w, w_in, gdn_conv_w, gdn_a_log, gdn_dt_bias, gdn_norm_w, gla_gate_w2, gla_gate_b, gla_norm_w, w_out, ffn_norm_w, w_gate, w_up, w_down, final_norm_w, loss_target, m_meta_tokens, m_attn_norm_w, m_w_in, m_gdn_conv_w, m_gdn_a_log, m_gdn_dt_bias, m_gdn_norm_w, m_gla_gate_w2, m_gla_gate_b, m_gla_norm_w, m_w_out, m_ffn_norm_w, m_w_gate, m_w_up, m_w_down, m_final_norm_w, v_meta_tokens, v_attn_norm_w, v_w_in, v_gdn_conv_w, v_gdn_a_log, v_gdn_dt_bias, v_gdn_norm_w, v_gla_gate_w2, v_gla_gate_b, v_gla_norm_w, v_w_out, v_ffn_norm_w, v_w_gate, v_w_up, v_w_down, v_final_norm_w):
    w = dict(meta_tokens=meta_tokens, attn_norm_w=attn_norm_w, w_in=w_in, gdn_conv_w=gdn_conv_w, gdn_a_log=gdn_a_log,
             gdn_dt_bias=gdn_dt_bias, gdn_norm_w=gdn_norm_w, gla_gate_w2=gla_gate_w2, gla_gate_b=gla_gate_b,
             gla_norm_w=gla_norm_w, w_out=w_out, ffn_norm_w=ffn_norm_w, w_gate=w_gate, w_up=w_up, w_down=w_down,
             final_norm_w=final_norm_w)
    m = dict(meta_tokens=m_meta_tokens, attn_norm_w=m_attn_norm_w, w_in=m_w_in, gdn_conv_w=m_gdn_conv_w,
             gdn_a_log=m_gdn_a_log, gdn_dt_bias=m_gdn_dt_bias, gdn_norm_w=m_gdn_norm_w, gla_gate_w2=m_gla_gate_w2,
             gla_gate_b=m_gla_gate_b, gla_norm_w=m_gla_norm_w, w_out=m_w_out, ffn_norm_w=m_ffn_norm_w,
             w_gate=m_w_gate, w_up=m_w_up, w_down=m_w_down, final_norm_w=m_final_norm_w)
    v = dict(meta_tokens=v_meta_tokens, attn_norm_w=v_attn_norm_w, w_in=v_w_in, gdn_conv_w=v_gdn_conv_w,
             gdn_a_log=v_gdn_a_log, gdn_dt_bias=v_gdn_dt_bias, gdn_norm_w=v_gdn_norm_w, gla_gate_w2=v_gla_gate_w2,
             gla_gate_b=v_gla_gate_b, gla_norm_w=v_gla_norm_w, w_out=v_w_out, ffn_norm_w=v_ffn_norm_w,
             w_gate=v_w_gate, w_up=v_w_up, w_down=v_w_down, final_norm_w=v_final_norm_w)
    chip = 2 * lax.axis_index("x") + lax.axis_index("y")

    def two_d(a):
        return a.reshape(1, -1) if a.ndim == 1 else a.reshape(-1, a.shape[-1])

    w2d = {k: two_d(a) for k, a in w.items()}
    big = _BIG_COLS + _BIG_ROWS
    small = tuple(k for k in _WEIGHTS if k not in big)

    (meta4,), = _comm_now("gather_meta", [_Gather([w2d["meta_tokens"]])])
    shard = {k: w2d[k] for k in big + ("gdn_conv_w", "gla_gate_w2")}
    shard["w_in"] = w2d["w_in"].astype(bf16)
    lossp, grad_x, received, g = _step(x, loss_target, {k: w2d[k] for k in small}, _cat_cols(meta4), shard)
    loss = lax.psum(lossp, ("x", "y", "c"))

    sizes = [g[k].size for k in small]
    total = sum(sizes)
    rows = -(-total // 1024)
    rows += (-rows) % 8
    packed = jnp.concatenate([g[k].reshape(-1) for k in small] + [jnp.zeros((rows * 1024 - total,), f32)])
    (packed8,), = _comm_now("exchange_small", [_Exchange([], [packed.reshape(rows, 1024)])])
    red = {k: h.reshape(w2d[k].shape) for k, h in received.items() if k != "w_in"}
    psum_small = _sum_slots(packed8, "sum_small").reshape(-1)
    off = 0
    for k, n in zip(small, sizes):
        a = psum_small[off:off + n].reshape(g[k].shape)
        off += n
        if k in _SMALL_SHARDED:
            c = w2d[k].shape[1]
            a = lax.dynamic_slice_in_dim(a, chip * c, c, axis=1)
        red[k] = a

    grads, deltas, new_m, new_v = [], [], [], []
    for k in _WEIGHTS:
        shape = w[k].shape
        if k == "w_in":
            flip = lambda a: jnp.swapaxes(a, 1, 2).reshape(shape[2], shape[1])
            unflip = lambda a: jnp.swapaxes(a.reshape(shape[0], shape[2], shape[1]), 1, 2)
            out = _adamw_transposed(received[k], flip(w[k]), flip(m[k]), flip(v[k]), "adamw_" + k)
            gk, d, m2, v2 = [unflip(a) for a in out]
        elif k in big:
            gk, d, m2, v2 = [a.reshape(shape) for a in
                             _adamw(red[k], w2d[k], two_d(m[k]), two_d(v[k]), "adamw_" + k, copy_g=True)]
        else:
            gk = red[k].reshape(shape)
            d, m2, v2 = [a.reshape(shape) for a in _adamw(red[k], w2d[k], two_d(m[k]), two_d(v[k]), "adamw_" + k)]
        grads.append(gk)
        deltas.append(d)
        new_m.append(m2)
        new_v.append(v2)
    return (loss, grad_x, *grads, *deltas, *new_m, *new_v)
```
